```python
import jax, jax.numpy as jnp
from jax import lax
import numpy as np

D_MODEL = 1024
BATCH = 8
SEQ = 4096
DEPTH = 1

MEM_LEN = 256
MIX_WIDTH = D_MODEL
SGU_WIDTH = MIX_WIDTH // 2
SGU_GROUPS = 4
SGU_GROUP_DIM = SGU_WIDTH // SGU_GROUPS
CHUNK = 128
SB_WIDTH = MIX_WIDTH - SGU_WIDTH
SB_HEAD_DIM = 64
SB_HEADS = SB_WIDTH // SB_HEAD_DIM
Q_BLOCK = 128
XA_HEADS = 4
XA_HEAD_DIM = D_MODEL // XA_HEADS
D_FF = ((8 * D_MODEL // 3 + 127) // 128) * 128
IN_COLS = 2 * SGU_WIDTH + 3 * SB_WIDTH
EPS = 1e-6

kernel_name = "hybrid_sgu_stickbreaking_macaron_block"


def rmsnorm(x, g):
    xf = x.astype(jnp.float32)
    y = xf * lax.rsqrt(jnp.mean(xf * xf, axis=-1, keepdims=True) + EPS)
    return (y * g.astype(jnp.float32)).astype(x.dtype)


def swiglu(x, w_gate, w_up, w_down):
    return (jax.nn.silu(x @ w_gate) * (x @ w_up)) @ w_down


def chunked_sgu(u, v, norm_g, norm_b, w_s, b_s):
    B, S, G, Dg = v.shape
    vf = v.astype(jnp.float32)
    mu = jnp.mean(vf, axis=-1, keepdims=True)
    var = jnp.mean((vf - mu) ** 2, axis=-1, keepdims=True)
    vn = ((vf - mu) * lax.rsqrt(var + EPS) * norm_g.astype(jnp.float32)
          + norm_b.astype(jnp.float32)).astype(v.dtype)
    vc = vn.reshape(B, S // CHUNK, CHUNK, G, Dg)
    causal = jnp.tril(jnp.ones((CHUNK, CHUNK), dtype=bool))
    w = jnp.where(causal[None], w_s, jnp.zeros_like(w_s)).astype(v.dtype)
    mixed = jnp.einsum('gts,bcsgd->bctgd', w, vc) + b_s.T.astype(v.dtype)[None, None, :, :, None]
    return u * mixed.reshape(B, S, G, Dg)


def stick_breaking_attention(q, k, v):
    B, H, S, Dh = q.shape
    scale = Dh ** -0.5
    outs = []
    for blk in range(S // Q_BLOCK):
        t0 = blk * Q_BLOCK
        t1 = t0 + Q_BLOCK
        qb = q[:, :, t0:t1]
        kb = k[:, :, :t1]
        vb = v[:, :, :t1]
        z = jnp.einsum('bhtd,bhsd->bhts', qb, kb,
                       preferred_element_type=jnp.float32) * scale
        t_idx = t0 + jnp.arange(Q_BLOCK)[:, None]
        s_idx = jnp.arange(t1)[None, :]
        strict = s_idx < t_idx
        log_beta = jax.nn.log_sigmoid(z)
        log_1m = jnp.where(strict, jax.nn.log_sigmoid(-z), 0.0)
        suffix = lax.cumsum(log_1m, axis=log_1m.ndim - 1, reverse=True) - log_1m
        a = jnp.where(strict, jnp.exp(log_beta + suffix), 0.0)
        outs.append(jnp.einsum('bhts,bhsd->bhtd', a.astype(v.dtype), vb))
    return jnp.concatenate(outs, axis=2)


def memory_cross_attention(x, memn, w_q, w_kv, w_o):
    B, S, D = x.shape
    M = memn.shape[1]
    q = (x @ w_q).reshape(B, S, XA_HEADS, XA_HEAD_DIM)
    kv = (memn @ w_kv).reshape(B, M, 2, XA_HEADS, XA_HEAD_DIM)
    k, v = kv[:, :, 0], kv[:, :, 1]
    logits = jnp.einsum('bshd,bmhd->bhsm', q, k,
                        preferred_element_type=jnp.float32) * (XA_HEAD_DIM ** -0.5)
    p = jax.nn.softmax(logits, axis=-1).astype(v.dtype)
    o = jnp.einsum('bhsm,bmhd->bshd', p, v).reshape(B, S, XA_HEADS * XA_HEAD_DIM)
    return o @ w_o


def _fwd_setup_inputs(seed: int = 0) -> dict:
    key = jax.random.key(seed)
    ks = iter(jax.random.split(key, 40))
    L, D, F = DEPTH, D_MODEL, D_FF

    def nrm(shape, scale):
        return jax.random.normal(next(ks), shape, jnp.float32) * scale

    def gain(shape):
        return 1.0 + nrm(shape, 0.02)

    return {
        "x": nrm((BATCH, SEQ, D), 1.0),
        "mem": nrm((BATCH, MEM_LEN, D), 1.0),
        "ffn1_pre_g": gain((L, D)),
        "ffn1_post_g": gain((L, D)),
        "ffn1_w_gate": nrm((L, D, F), D ** -0.5),
        "ffn1_w_up": nrm((L, D, F), D ** -0.5),
        "ffn1_w_down": nrm((L, F, D), F ** -0.5),
        "mix_pre_g": gain((L, D)),
        "mix_post_g": gain((L, D)),
        "w_in": nrm((L, D, IN_COLS), D ** -0.5),
        "sgu_norm_g": gain((L, SGU_GROUPS, SGU_GROUP_DIM)),
        "sgu_norm_b": nrm((L, SGU_GROUPS, SGU_GROUP_DIM), 0.02),
        "sgu_w_s": nrm((L, SGU_GROUPS, CHUNK, CHUNK), CHUNK ** -0.5),
        "sgu_b_s": 1.0 + nrm((L, SGU_GROUPS, CHUNK), 0.02),
        "sgu_out_g": gain((L, SGU_WIDTH)),
        "sb_out_g": gain((L, SB_WIDTH)),
        "w_out": nrm((L, MIX_WIDTH, D), MIX_WIDTH ** -0.5),
        "xa_pre_g": gain((L, D)),
        "xa_post_g": gain((L, D)),
        "mem_norm_g": gain((L, D)),
        "xa_w_q": nrm((L, D, XA_HEADS * XA_HEAD_DIM), D ** -0.5),
        "xa_w_kv": nrm((L, D, 2 * XA_HEADS * XA_HEAD_DIM), D ** -0.5),
        "xa_w_o": nrm((L, XA_HEADS * XA_HEAD_DIM, D), (XA_HEADS * XA_HEAD_DIM) ** -0.5),
        "ffn2_pre_g": gain((L, D)),
        "ffn2_post_g": gain((L, D)),
        "ffn2_w_gate": nrm((L, D, F), D ** -0.5),
        "ffn2_w_up": nrm((L, D, F), D ** -0.5),
        "ffn2_w_down": nrm((L, F, D), F ** -0.5),
        "final_norm_g": gain((L, D)),
    }


def _fwd_reference(x, mem, ffn1_pre_g, ffn1_post_g, ffn1_w_gate, ffn1_w_up, ffn1_w_down,
              mix_pre_g, mix_post_g, w_in, sgu_norm_g, sgu_norm_b, sgu_w_s, sgu_b_s,
              sgu_out_g, sb_out_g, w_out, xa_pre_g, xa_post_g, mem_norm_g, xa_w_q,
              xa_w_kv, xa_w_o, ffn2_pre_g, ffn2_post_g, ffn2_w_gate, ffn2_w_up,
              ffn2_w_down, final_norm_g):
    B, S, D = x.shape
    splits = [SGU_WIDTH, 2 * SGU_WIDTH, 2 * SGU_WIDTH + SB_WIDTH,
              2 * SGU_WIDTH + 2 * SB_WIDTH]
    h = x
    for l in range(DEPTH):
        f = swiglu(rmsnorm(h, ffn1_pre_g[l]), ffn1_w_gate[l], ffn1_w_up[l], ffn1_w_down[l])
        h = h + 0.5 * rmsnorm(f, ffn1_post_g[l])

        n = rmsnorm(h, mix_pre_g[l])
        proj = n @ w_in[l]
        u, vg, q, k, vs = jnp.split(proj, splits, axis=-1)
        u = jax.nn.gelu(u).reshape(B, S, SGU_GROUPS, SGU_GROUP_DIM)
        vg = jax.nn.gelu(vg).reshape(B, S, SGU_GROUPS, SGU_GROUP_DIM)
        out_a = chunked_sgu(u, vg, sgu_norm_g[l], sgu_norm_b[l],
                            sgu_w_s[l], sgu_b_s[l]).reshape(B, S, SGU_WIDTH)

        def heads(t):
            return t.reshape(B, S, SB_HEADS, SB_HEAD_DIM).transpose(0, 2, 1, 3)
        out_b = stick_breaking_attention(heads(q), heads(k), heads(vs))
        out_b = out_b.transpose(0, 2, 1, 3).reshape(B, S, SB_WIDTH)

        merged = jnp.concatenate([rmsnorm(out_a, sgu_out_g[l]),
                                  rmsnorm(out_b, sb_out_g[l])], axis=-1)
        h = h + rmsnorm(merged @ w_out[l], mix_post_g[l])

        c = memory_cross_attention(rmsnorm(h, xa_pre_g[l]), rmsnorm(mem, mem_norm_g[l]),
                                   xa_w_q[l], xa_w_kv[l], xa_w_o[l])
        h = h + rmsnorm(c, xa_post_g[l])

        f = swiglu(rmsnorm(h, ffn2_pre_g[l]), ffn2_w_gate[l], ffn2_w_up[l], ffn2_w_down[l])
        h = h + 0.5 * rmsnorm(f, ffn2_post_g[l])

        h = rmsnorm(h, final_norm_g[l])
    return h


import jax as _jax
import jax.numpy as _jnp

TWIN_FORMAT = 'train_step'
FWD_PARAMS = ['x', 'mem', 'ffn1_pre_g', 'ffn1_post_g', 'ffn1_w_gate', 'ffn1_w_up', 'ffn1_w_down', 'mix_pre_g', 'mix_post_g', 'w_in', 'sgu_norm_g', 'sgu_norm_b', 'sgu_w_s', 'sgu_b_s', 'sgu_out_g', 'sb_out_g', 'w_out', 'xa_pre_g', 'xa_post_g', 'mem_norm_g', 'xa_w_q', 'xa_w_kv', 'xa_w_o', 'ffn2_pre_g', 'ffn2_post_g', 'ffn2_w_gate', 'ffn2_w_up', 'ffn2_w_down', 'final_norm_g']
TWIN_WEIGHTS = ['ffn1_pre_g', 'ffn1_post_g', 'ffn1_w_gate', 'ffn1_w_up', 'ffn1_w_down', 'mix_pre_g', 'mix_post_g', 'w_in', 'sgu_norm_g', 'sgu_norm_b', 'sgu_w_s', 'sgu_b_s', 'sgu_out_g', 'sb_out_g', 'w_out', 'xa_pre_g', 'xa_post_g', 'mem_norm_g', 'xa_w_q', 'xa_w_kv', 'xa_w_o', 'ffn2_pre_g', 'ffn2_post_g', 'ffn2_w_gate', 'ffn2_w_up', 'ffn2_w_down', 'final_norm_g']
TWIN_DIFF_INPUT = 'x'
TWIN_INPUTS = ['x', 'mem', 'ffn1_pre_g', 'ffn1_post_g', 'ffn1_w_gate', 'ffn1_w_up', 'ffn1_w_down', 'mix_pre_g', 'mix_post_g', 'w_in', 'sgu_norm_g', 'sgu_norm_b', 'sgu_w_s', 'sgu_b_s', 'sgu_out_g', 'sb_out_g', 'w_out', 'xa_pre_g', 'xa_post_g', 'mem_norm_g', 'xa_w_q', 'xa_w_kv', 'xa_w_o', 'ffn2_pre_g', 'ffn2_post_g', 'ffn2_w_gate', 'ffn2_w_up', 'ffn2_w_down', 'final_norm_g', 'loss_target', 'm_ffn1_pre_g', 'm_ffn1_post_g', 'm_ffn1_w_gate', 'm_ffn1_w_up', 'm_ffn1_w_down', 'm_mix_pre_g', 'm_mix_post_g', 'm_w_in', 'm_sgu_norm_g', 'm_sgu_norm_b', 'm_sgu_w_s', 'm_sgu_b_s', 'm_sgu_out_g', 'm_sb_out_g', 'm_w_out', 'm_xa_pre_g', 'm_xa_post_g', 'm_mem_norm_g', 'm_xa_w_q', 'm_xa_w_kv', 'm_xa_w_o', 'm_ffn2_pre_g', 'm_ffn2_post_g', 'm_ffn2_w_gate', 'm_ffn2_w_up', 'm_ffn2_w_down', 'm_final_norm_g', 'v_ffn1_pre_g', 'v_ffn1_post_g', 'v_ffn1_w_gate', 'v_ffn1_w_up', 'v_ffn1_w_down', 'v_mix_pre_g', 'v_mix_post_g', 'v_w_in', 'v_sgu_norm_g', 'v_sgu_norm_b', 'v_sgu_w_s', 'v_sgu_b_s', 'v_sgu_out_g', 'v_sb_out_g', 'v_w_out', 'v_xa_pre_g', 'v_xa_post_g', 'v_mem_norm_g', 'v_xa_w_q', 'v_xa_w_kv', 'v_xa_w_o', 'v_ffn2_pre_g', 'v_ffn2_post_g', 'v_ffn2_w_gate', 'v_ffn2_w_up', 'v_ffn2_w_down', 'v_final_norm_g']
TWIN_OUTPUTS = ['loss', 'grad_x', 'grad_ffn1_pre_g', 'grad_ffn1_post_g', 'grad_ffn1_w_gate', 'grad_ffn1_w_up', 'grad_ffn1_w_down', 'grad_mix_pre_g', 'grad_mix_post_g', 'grad_w_in', 'grad_sgu_norm_g', 'grad_sgu_norm_b', 'grad_sgu_w_s', 'grad_sgu_b_s', 'grad_sgu_out_g', 'grad_sb_out_g', 'grad_w_out', 'grad_xa_pre_g', 'grad_xa_post_g', 'grad_mem_norm_g', 'grad_xa_w_q', 'grad_xa_w_kv', 'grad_xa_w_o', 'grad_ffn2_pre_g', 'grad_ffn2_post_g', 'grad_ffn2_w_gate', 'grad_ffn2_w_up', 'grad_ffn2_w_down', 'grad_final_norm_g', 'delta_ffn1_pre_g', 'delta_ffn1_post_g', 'delta_ffn1_w_gate', 'delta_ffn1_w_up', 'delta_ffn1_w_down', 'delta_mix_pre_g', 'delta_mix_post_g', 'delta_w_in', 'delta_sgu_norm_g', 'delta_sgu_norm_b', 'delta_sgu_w_s', 'delta_sgu_b_s', 'delta_sgu_out_g', 'delta_sb_out_g', 'delta_w_out', 'delta_xa_pre_g', 'delta_xa_post_g', 'delta_mem_norm_g', 'delta_xa_w_q', 'delta_xa_w_kv', 'delta_xa_w_o', 'delta_ffn2_pre_g', 'delta_ffn2_post_g', 'delta_ffn2_w_gate', 'delta_ffn2_w_up', 'delta_ffn2_w_down', 'delta_final_norm_g', 'new_m_ffn1_pre_g', 'new_m_ffn1_post_g', 'new_m_ffn1_w_gate', 'new_m_ffn1_w_up', 'new_m_ffn1_w_down', 'new_m_mix_pre_g', 'new_m_mix_post_g', 'new_m_w_in', 'new_m_sgu_norm_g', 'new_m_sgu_norm_b', 'new_m_sgu_w_s', 'new_m_sgu_b_s', 'new_m_sgu_out_g', 'new_m_sb_out_g', 'new_m_w_out', 'new_m_xa_pre_g', 'new_m_xa_post_g', 'new_m_mem_norm_g', 'new_m_xa_w_q', 'new_m_xa_w_kv', 'new_m_xa_w_o', 'new_m_ffn2_pre_g', 'new_m_ffn2_post_g', 'new_m_ffn2_w_gate', 'new_m_ffn2_w_up', 'new_m_ffn2_w_down', 'new_m_final_norm_g', 'new_v_ffn1_pre_g', 'new_v_ffn1_post_g', 'new_v_ffn1_w_gate', 'new_v_ffn1_w_up', 'new_v_ffn1_w_down', 'new_v_mix_pre_g', 'new_v_mix_post_g', 'new_v_w_in', 'new_v_sgu_norm_g', 'new_v_sgu_norm_b', 'new_v_sgu_w_s', 'new_v_sgu_b_s', 'new_v_sgu_out_g', 'new_v_sb_out_g', 'new_v_w_out', 'new_v_xa_pre_g', 'new_v_xa_post_g', 'new_v_mem_norm_g', 'new_v_xa_w_q', 'new_v_xa_w_kv', 'new_v_xa_w_o', 'new_v_ffn2_pre_g', 'new_v_ffn2_post_g', 'new_v_ffn2_w_gate', 'new_v_ffn2_w_up', 'new_v_ffn2_w_down', 'new_v_final_norm_g']
TWIN_LEAF_KINDS = {'loss': 'loss', 'grad_x': 'grad_x', 'grad_ffn1_pre_g': 'grad_w', 'grad_ffn1_post_g': 'grad_w', 'grad_ffn1_w_gate': 'grad_w', 'grad_ffn1_w_up': 'grad_w', 'grad_ffn1_w_down': 'grad_w', 'grad_mix_pre_g': 'grad_w', 'grad_mix_post_g': 'grad_w', 'grad_w_in': 'grad_w', 'grad_sgu_norm_g': 'grad_w', 'grad_sgu_norm_b': 'grad_w', 'grad_sgu_w_s': 'grad_w', 'grad_sgu_b_s': 'grad_w', 'grad_sgu_out_g': 'grad_w', 'grad_sb_out_g': 'grad_w', 'grad_w_out': 'grad_w', 'grad_xa_pre_g': 'grad_w', 'grad_xa_post_g': 'grad_w', 'grad_mem_norm_g': 'grad_w', 'grad_xa_w_q': 'grad_w', 'grad_xa_w_kv': 'grad_w', 'grad_xa_w_o': 'grad_w', 'grad_ffn2_pre_g': 'grad_w', 'grad_ffn2_post_g': 'grad_w', 'grad_ffn2_w_gate': 'grad_w', 'grad_ffn2_w_up': 'grad_w', 'grad_ffn2_w_down': 'grad_w', 'grad_final_norm_g': 'grad_w', 'delta_ffn1_pre_g': 'delta_w', 'delta_ffn1_post_g': 'delta_w', 'delta_ffn1_w_gate': 'delta_w', 'delta_ffn1_w_up': 'delta_w', 'delta_ffn1_w_down': 'delta_w', 'delta_mix_pre_g': 'delta_w', 'delta_mix_post_g': 'delta_w', 'delta_w_in': 'delta_w', 'delta_sgu_norm_g': 'delta_w', 'delta_sgu_norm_b': 'delta_w', 'delta_sgu_w_s': 'delta_w', 'delta_sgu_b_s': 'delta_w', 'delta_sgu_out_g': 'delta_w', 'delta_sb_out_g': 'delta_w', 'delta_w_out': 'delta_w', 'delta_xa_pre_g': 'delta_w', 'delta_xa_post_g': 'delta_w', 'delta_mem_norm_g': 'delta_w', 'delta_xa_w_q': 'delta_w', 'delta_xa_w_kv': 'delta_w', 'delta_xa_w_o': 'delta_w', 'delta_ffn2_pre_g': 'delta_w', 'delta_ffn2_post_g': 'delta_w', 'delta_ffn2_w_gate': 'delta_w', 'delta_ffn2_w_up': 'delta_w', 'delta_ffn2_w_down': 'delta_w', 'delta_final_norm_g': 'delta_w', 'new_m_ffn1_pre_g': 'new_m', 'new_m_ffn1_post_g': 'new_m', 'new_m_ffn1_w_gate': 'new_m', 'new_m_ffn1_w_up': 'new_m', 'new_m_ffn1_w_down': 'new_m', 'new_m_mix_pre_g': 'new_m', 'new_m_mix_post_g': 'new_m', 'new_m_w_in': 'new_m', 'new_m_sgu_norm_g': 'new_m', 'new_m_sgu_norm_b': 'new_m', 'new_m_sgu_w_s': 'new_m', 'new_m_sgu_b_s': 'new_m', 'new_m_sgu_out_g': 'new_m', 'new_m_sb_out_g': 'new_m', 'new_m_w_out': 'new_m', 'new_m_xa_pre_g': 'new_m', 'new_m_xa_post_g': 'new_m', 'new_m_mem_norm_g': 'new_m', 'new_m_xa_w_q': 'new_m', 'new_m_xa_w_kv': 'new_m', 'new_m_xa_w_o': 'new_m', 'new_m_ffn2_pre_g': 'new_m', 'new_m_ffn2_post_g': 'new_m', 'new_m_ffn2_w_gate': 'new_m', 'new_m_ffn2_w_up': 'new_m', 'new_m_ffn2_w_down': 'new_m', 'new_m_final_norm_g': 'new_m', 'new_v_ffn1_pre_g': 'new_v', 'new_v_ffn1_post_g': 'new_v', 'new_v_ffn1_w_gate': 'new_v', 'new_v_ffn1_w_up': 'new_v', 'new_v_ffn1_w_down': 'new_v', 'new_v_mix_pre_g': 'new_v', 'new_v_mix_post_g': 'new_v', 'new_v_w_in': 'new_v', 'new_v_sgu_norm_g': 'new_v', 'new_v_sgu_norm_b': 'new_v', 'new_v_sgu_w_s': 'new_v', 'new_v_sgu_b_s': 'new_v', 'new_v_sgu_out_g': 'new_v', 'new_v_sb_out_g': 'new_v', 'new_v_w_out': 'new_v', 'new_v_xa_pre_g': 'new_v', 'new_v_xa_post_g': 'new_v', 'new_v_mem_norm_g': 'new_v', 'new_v_xa_w_q': 'new_v', 'new_v_xa_w_kv': 'new_v', 'new_v_xa_w_o': 'new_v', 'new_v_ffn2_pre_g': 'new_v', 'new_v_ffn2_post_g': 'new_v', 'new_v_ffn2_w_gate': 'new_v', 'new_v_ffn2_w_up': 'new_v', 'new_v_ffn2_w_down': 'new_v', 'new_v_final_norm_g': 'new_v'}


def _forward(args):
    return _fwd_reference(*[args[k] for k in FWD_PARAMS])


def _output_shape():
    def fwd():
        inp = _fwd_setup_inputs(0)
        return _fwd_reference(*[inp[k] for k in FWD_PARAMS])
    out = _jax.eval_shape(fwd)
    return out.shape, out.dtype

N_MICROBATCH = 1
ADAM_LR = 0.001
ADAM_B1 = 0.9
ADAM_B2 = 0.999
ADAM_EPS = 1e-08
ADAM_WD = 0.01
ADAM_STEP = 10
PER_EXAMPLE_BATCH_AXIS = {'x': 0, 'mem': 0, 'loss_target': 0}
SHARED_INPUTS = []
_WEIGHT_DTYPES = {'ffn1_pre_g': _jnp.float32, 'ffn1_post_g': _jnp.float32, 'ffn1_w_gate': _jnp.float32, 'ffn1_w_up': _jnp.float32, 'ffn1_w_down': _jnp.float32, 'mix_pre_g': _jnp.float32, 'mix_post_g': _jnp.float32, 'w_in': _jnp.float32, 'sgu_norm_g': _jnp.float32, 'sgu_norm_b': _jnp.float32, 'sgu_w_s': _jnp.float32, 'sgu_b_s': _jnp.float32, 'sgu_out_g': _jnp.float32, 'sb_out_g': _jnp.float32, 'w_out': _jnp.float32, 'xa_pre_g': _jnp.float32, 'xa_post_g': _jnp.float32, 'mem_norm_g': _jnp.float32, 'xa_w_q': _jnp.float32, 'xa_w_kv': _jnp.float32, 'xa_w_o': _jnp.float32, 'ffn2_pre_g': _jnp.float32, 'ffn2_post_g': _jnp.float32, 'ffn2_w_gate': _jnp.float32, 'ffn2_w_up': _jnp.float32, 'ffn2_w_down': _jnp.float32, 'final_norm_g': _jnp.float32}
MOMENT_SCALE = {'ffn1_pre_g': 1.338146e-01, 'ffn1_post_g': 9.883798e-02, 'ffn1_w_gate': 5.590860e-02, 'ffn1_w_up': 5.406417e-02, 'ffn1_w_down': 8.961814e-02, 'mix_pre_g': 1.467427e-01, 'mix_post_g': 3.119095e-01, 'w_in': 9.324200e-02, 'sgu_norm_g': 6.905904e-02, 'sgu_norm_b': 7.109363e-02, 'sgu_w_s': 6.746692e-02, 'sgu_b_s': 9.083312e-02, 'sgu_out_g': 1.288201e-01, 'sb_out_g': 1.210293e-01, 'w_out': 1.244877e-01, 'xa_pre_g': 1.005191e-01, 'xa_post_g': 3.224249e-01, 'mem_norm_g': 1.632741e-01, 'xa_w_q': 9.968138e-02, 'xa_w_kv': 1.039835e-01, 'xa_w_o': 1.082893e-01, 'ffn2_pre_g': 7.037792e-02, 'ffn2_post_g': 8.208632e-02, 'ffn2_w_gate': 2.952231e-02, 'ffn2_w_up': 2.877565e-02, 'ffn2_w_down': 4.772754e-02, 'final_norm_g': 3.205982e+01}


def _to_microbatches(a, axis):
    t = _jnp.moveaxis(a, axis, 0)
    t = t.reshape((N_MICROBATCH, t.shape[0] // N_MICROBATCH) + t.shape[1:])
    return _jnp.moveaxis(t, 1, axis + 1)


def setup_inputs(seed: int = 0) -> dict:
    inp = _fwd_setup_inputs(seed)
    key = _jax.random.fold_in(_jax.random.key(seed), 7919)
    shape, _ = _output_shape()
    out = dict(inp)
    out["loss_target"] = _jax.random.normal(_jax.random.fold_in(key, 0), shape, _jnp.float32)
    for i, name in enumerate(TWIN_WEIGHTS):
        w = inp[name].astype(_jnp.float32)
        if MOMENT_SCALE is None:
            s = _jnp.sqrt(_jnp.mean(_jnp.square(w)) + 1e-30)
        else:
            s = MOMENT_SCALE[name]
        km, kv = _jax.random.split(_jax.random.fold_in(key, i + 1))
        out[name] = w
        out["m_" + name] = s * _jax.random.normal(km, w.shape, _jnp.float32)
        out["v_" + name] = (s * s) * _jax.random.uniform(kv, w.shape, _jnp.float32, 0.5, 1.5)
    if N_MICROBATCH > 1:
        for name, axis in PER_EXAMPLE_BATCH_AXIS.items():
            out[name] = _to_microbatches(out[name], axis)
    return {'x': out['x'], 'mem': out['mem'], 'ffn1_pre_g': out['ffn1_pre_g'], 'ffn1_post_g': out['ffn1_post_g'], 'ffn1_w_gate': out['ffn1_w_gate'], 'ffn1_w_up': out['ffn1_w_up'], 'ffn1_w_down': out['ffn1_w_down'], 'mix_pre_g': out['mix_pre_g'], 'mix_post_g': out['mix_post_g'], 'w_in': out['w_in'], 'sgu_norm_g': out['sgu_norm_g'], 'sgu_norm_b': out['sgu_norm_b'], 'sgu_w_s': out['sgu_w_s'], 'sgu_b_s': out['sgu_b_s'], 'sgu_out_g': out['sgu_out_g'], 'sb_out_g': out['sb_out_g'], 'w_out': out['w_out'], 'xa_pre_g': out['xa_pre_g'], 'xa_post_g': out['xa_post_g'], 'mem_norm_g': out['mem_norm_g'], 'xa_w_q': out['xa_w_q'], 'xa_w_kv': out['xa_w_kv'], 'xa_w_o': out['xa_w_o'], 'ffn2_pre_g': out['ffn2_pre_g'], 'ffn2_post_g': out['ffn2_post_g'], 'ffn2_w_gate': out['ffn2_w_gate'], 'ffn2_w_up': out['ffn2_w_up'], 'ffn2_w_down': out['ffn2_w_down'], 'final_norm_g': out['final_norm_g'], 'loss_target': out['loss_target'], 'm_ffn1_pre_g': out['m_ffn1_pre_g'], 'm_ffn1_post_g': out['m_ffn1_post_g'], 'm_ffn1_w_gate': out['m_ffn1_w_gate'], 'm_ffn1_w_up': out['m_ffn1_w_up'], 'm_ffn1_w_down': out['m_ffn1_w_down'], 'm_mix_pre_g': out['m_mix_pre_g'], 'm_mix_post_g': out['m_mix_post_g'], 'm_w_in': out['m_w_in'], 'm_sgu_norm_g': out['m_sgu_norm_g'], 'm_sgu_norm_b': out['m_sgu_norm_b'], 'm_sgu_w_s': out['m_sgu_w_s'], 'm_sgu_b_s': out['m_sgu_b_s'], 'm_sgu_out_g': out['m_sgu_out_g'], 'm_sb_out_g': out['m_sb_out_g'], 'm_w_out': out['m_w_out'], 'm_xa_pre_g': out['m_xa_pre_g'], 'm_xa_post_g': out['m_xa_post_g'], 'm_mem_norm_g': out['m_mem_norm_g'], 'm_xa_w_q': out['m_xa_w_q'], 'm_xa_w_kv': out['m_xa_w_kv'], 'm_xa_w_o': out['m_xa_w_o'], 'm_ffn2_pre_g': out['m_ffn2_pre_g'], 'm_ffn2_post_g': out['m_ffn2_post_g'], 'm_ffn2_w_gate': out['m_ffn2_w_gate'], 'm_ffn2_w_up': out['m_ffn2_w_up'], 'm_ffn2_w_down': out['m_ffn2_w_down'], 'm_final_norm_g': out['m_final_norm_g'], 'v_ffn1_pre_g': out['v_ffn1_pre_g'], 'v_ffn1_post_g': out['v_ffn1_post_g'], 'v_ffn1_w_gate': out['v_ffn1_w_gate'], 'v_ffn1_w_up': out['v_ffn1_w_up'], 'v_ffn1_w_down': out['v_ffn1_w_down'], 'v_mix_pre_g': out['v_mix_pre_g'], 'v_mix_post_g': out['v_mix_post_g'], 'v_w_in': out['v_w_in'], 'v_sgu_norm_g': out['v_sgu_norm_g'], 'v_sgu_norm_b': out['v_sgu_norm_b'], 'v_sgu_w_s': out['v_sgu_w_s'], 'v_sgu_b_s': out['v_sgu_b_s'], 'v_sgu_out_g': out['v_sgu_out_g'], 'v_sb_out_g': out['v_sb_out_g'], 'v_w_out': out['v_w_out'], 'v_xa_pre_g': out['v_xa_pre_g'], 'v_xa_post_g': out['v_xa_post_g'], 'v_mem_norm_g': out['v_mem_norm_g'], 'v_xa_w_q': out['v_xa_w_q'], 'v_xa_w_kv': out['v_xa_w_kv'], 'v_xa_w_o': out['v_xa_w_o'], 'v_ffn2_pre_g': out['v_ffn2_pre_g'], 'v_ffn2_post_g': out['v_ffn2_post_g'], 'v_ffn2_w_gate': out['v_ffn2_w_gate'], 'v_ffn2_w_up': out['v_ffn2_w_up'], 'v_ffn2_w_down': out['v_ffn2_w_down'], 'v_final_norm_g': out['v_final_norm_g']}


def _loss(weights, diff, rest, loss_target):
    with _jax.named_scope("forward"):
        args = {**rest, TWIN_DIFF_INPUT: diff, **{k: w.astype(_WEIGHT_DTYPES[k]) for k, w in weights.items()}}
        y = _forward(args)
    with _jax.named_scope("loss_head"):
        err = _jnp.square(y.astype(_jnp.float32) - loss_target)
        return 0.5 * _jnp.sum(_jnp.mean(err, axis=-1)) if err.ndim else 0.5 * err


def _adamw(w, g, m, v):
    m = ADAM_B1 * m + (1.0 - ADAM_B1) * g
    v = ADAM_B2 * v + (1.0 - ADAM_B2) * _jnp.square(g)
    m_hat = m / (1.0 - ADAM_B1 ** ADAM_STEP)
    v_hat = v / (1.0 - ADAM_B2 ** ADAM_STEP)
    delta = -ADAM_LR * (m_hat / (_jnp.sqrt(v_hat) + ADAM_EPS) + ADAM_WD * w)
    return delta, m, v


def reference(x, mem, ffn1_pre_g, ffn1_post_g, ffn1_w_gate, ffn1_w_up, ffn1_w_down, mix_pre_g, mix_post_g, w_in, sgu_norm_g, sgu_norm_b, sgu_w_s, sgu_b_s, sgu_out_g, sb_out_g, w_out, xa_pre_g, xa_post_g, mem_norm_g, xa_w_q, xa_w_kv, xa_w_o, ffn2_pre_g, ffn2_post_g, ffn2_w_gate, ffn2_w_up, ffn2_w_down, final_norm_g, loss_target, m_ffn1_pre_g, m_ffn1_post_g, m_ffn1_w_gate, m_ffn1_w_up, m_ffn1_w_down, m_mix_pre_g, m_mix_post_g, m_w_in, m_sgu_norm_g, m_sgu_norm_b, m_sgu_w_s, m_sgu_b_s, m_sgu_out_g, m_sb_out_g, m_w_out, m_xa_pre_g, m_xa_post_g, m_mem_norm_g, m_xa_w_q, m_xa_w_kv, m_xa_w_o, m_ffn2_pre_g, m_ffn2_post_g, m_ffn2_w_gate, m_ffn2_w_up, m_ffn2_w_down, m_final_norm_g, v_ffn1_pre_g, v_ffn1_post_g, v_ffn1_w_gate, v_ffn1_w_up, v_ffn1_w_down, v_mix_pre_g, v_mix_post_g, v_w_in, v_sgu_norm_g, v_sgu_norm_b, v_sgu_w_s, v_sgu_b_s, v_sgu_out_g, v_sb_out_g, v_w_out, v_xa_pre_g, v_xa_post_g, v_mem_norm_g, v_xa_w_q, v_xa_w_kv, v_xa_w_o, v_ffn2_pre_g, v_ffn2_post_g, v_ffn2_w_gate, v_ffn2_w_up, v_ffn2_w_down, v_final_norm_g):
    given = dict(x=x, mem=mem, ffn1_pre_g=ffn1_pre_g, ffn1_post_g=ffn1_post_g, ffn1_w_gate=ffn1_w_gate, ffn1_w_up=ffn1_w_up, ffn1_w_down=ffn1_w_down, mix_pre_g=mix_pre_g, mix_post_g=mix_post_g, w_in=w_in, sgu_norm_g=sgu_norm_g, sgu_norm_b=sgu_norm_b, sgu_w_s=sgu_w_s, sgu_b_s=sgu_b_s, sgu_out_g=sgu_out_g, sb_out_g=sb_out_g, w_out=w_out, xa_pre_g=xa_pre_g, xa_post_g=xa_post_g, mem_norm_g=mem_norm_g, xa_w_q=xa_w_q, xa_w_kv=xa_w_kv, xa_w_o=xa_w_o, ffn2_pre_g=ffn2_pre_g, ffn2_post_g=ffn2_post_g, ffn2_w_gate=ffn2_w_gate, ffn2_w_up=ffn2_w_up, ffn2_w_down=ffn2_w_down, final_norm_g=final_norm_g, loss_target=loss_target, m_ffn1_pre_g=m_ffn1_pre_g, m_ffn1_post_g=m_ffn1_post_g, m_ffn1_w_gate=m_ffn1_w_gate, m_ffn1_w_up=m_ffn1_w_up, m_ffn1_w_down=m_ffn1_w_down, m_mix_pre_g=m_mix_pre_g, m_mix_post_g=m_mix_post_g, m_w_in=m_w_in, m_sgu_norm_g=m_sgu_norm_g, m_sgu_norm_b=m_sgu_norm_b, m_sgu_w_s=m_sgu_w_s, m_sgu_b_s=m_sgu_b_s, m_sgu_out_g=m_sgu_out_g, m_sb_out_g=m_sb_out_g, m_w_out=m_w_out, m_xa_pre_g=m_xa_pre_g, m_xa_post_g=m_xa_post_g, m_mem_norm_g=m_mem_norm_g, m_xa_w_q=m_xa_w_q, m_xa_w_kv=m_xa_w_kv, m_xa_w_o=m_xa_w_o, m_ffn2_pre_g=m_ffn2_pre_g, m_ffn2_post_g=m_ffn2_post_g, m_ffn2_w_gate=m_ffn2_w_gate, m_ffn2_w_up=m_ffn2_w_up, m_ffn2_w_down=m_ffn2_w_down, m_final_norm_g=m_final_norm_g, v_ffn1_pre_g=v_ffn1_pre_g, v_ffn1_post_g=v_ffn1_post_g, v_ffn1_w_gate=v_ffn1_w_gate, v_ffn1_w_up=v_ffn1_w_up, v_ffn1_w_down=v_ffn1_w_down, v_mix_pre_g=v_mix_pre_g, v_mix_post_g=v_mix_post_g, v_w_in=v_w_in, v_sgu_norm_g=v_sgu_norm_g, v_sgu_norm_b=v_sgu_norm_b, v_sgu_w_s=v_sgu_w_s, v_sgu_b_s=v_sgu_b_s, v_sgu_out_g=v_sgu_out_g, v_sb_out_g=v_sb_out_g, v_w_out=v_w_out, v_xa_pre_g=v_xa_pre_g, v_xa_post_g=v_xa_post_g, v_mem_norm_g=v_mem_norm_g, v_xa_w_q=v_xa_w_q, v_xa_w_kv=v_xa_w_kv, v_xa_w_o=v_xa_w_o, v_ffn2_pre_g=v_ffn2_pre_g, v_ffn2_post_g=v_ffn2_post_g, v_ffn2_w_gate=v_ffn2_w_gate, v_ffn2_w_up=v_ffn2_w_up, v_ffn2_w_down=v_ffn2_w_down, v_final_norm_g=v_final_norm_g)
    weights = {n: given[n] for n in TWIN_WEIGHTS}
    shared = {n: given[n] for n in SHARED_INPUTS}
    per_example = {n: given[n] for n in ['x', 'mem']}
    grad_fn = _jax.value_and_grad(_loss, argnums=(0, 1))

    def one_microbatch(ex, loss_target):
        ex = dict(ex)
        diff = ex.pop(TWIN_DIFF_INPUT)
        return grad_fn(weights, diff, {**shared, **ex}, loss_target)

    if N_MICROBATCH == 1:
        loss, (grad_w, grad_x) = one_microbatch(per_example, given["loss_target"])
    else:
        def body(carry, xs):
            loss_sum, grad_sum = carry
            l_k, (gw_k, gx_k) = one_microbatch(xs[0], xs[1])
            with _jax.named_scope("update"):
                return (loss_sum + l_k, _jax.tree.map(_jnp.add, grad_sum, gw_k)), gx_k

        init = (_jnp.zeros((), _jnp.float32), _jax.tree.map(_jnp.zeros_like, weights))
        (loss, grad_w), grad_x = _jax.lax.scan(body, init, (per_example, given["loss_target"]))
    with _jax.named_scope("update"):
        delta_w, new_m, new_v = {}, {}, {}
        for n in TWIN_WEIGHTS:
            delta_w[n], new_m[n], new_v[n] = _adamw(weights[n], grad_w[n], given["m_" + n], given["v_" + n])
    return (loss, grad_x, *[grad_w[n] for n in TWIN_WEIGHTS], *[delta_w[n] for n in TWIN_WEIGHTS],
            *[new_m[n] for n in TWIN_WEIGHTS], *[new_v[n] for n in TWIN_WEIGHTS])
```

```python
import functools

import jax
import jax.numpy as jnp
from jax import lax
from jax.experimental import pallas as pl
from jax.experimental.pallas import tpu as pltpu

F32 = jnp.float32
BF16 = jnp.bfloat16
EPS = 1e-6
MESH = pl.DeviceIdType.MESH
N_DEV = 8

SGU_GROUPS = 4
GROUP_DIM = 128
CHUNK = 128
SB_HEADS = 8
SB_HEAD_DIM = 64
Q_BLOCK = 128
XA_HEADS = 4
XA_HEAD_DIM = 256

ADAM_LR = 0.001
ADAM_B1 = 0.9
ADAM_B2 = 0.999
ADAM_EPS = 1e-08
ADAM_WD = 0.01
ADAM_STEP = 10

VMEM_LIMIT_V7X = 56 * 1024 * 1024
GELU_C0 = 0.7978845608028654
GELU_C1 = 0.044715


def _dot(a, b):
    return jnp.dot(a.astype(BF16), b.astype(BF16), preferred_element_type=F32)


def _dot_nt(a, b):
    return lax.dot_general(a.astype(BF16), b.astype(BF16), (((1,), (1,)), ((), ())),
                           preferred_element_type=F32)


def _dot_tn(a, b):
    return lax.dot_general(a.astype(BF16), b.astype(BF16), (((0,), (0,)), ((), ())),
                           preferred_element_type=F32)


def _rms(x, g):
    r = lax.rsqrt(jnp.mean(x * x, axis=-1, keepdims=True) + EPS)
    return x * r * g, r


def _rms_bwd(x, g, dy):
    r = lax.rsqrt(jnp.mean(x * x, axis=-1, keepdims=True) + EPS)
    xh = x * r
    gy = dy * g
    dx = r * (gy - xh * jnp.mean(gy * xh, axis=-1, keepdims=True))
    dg = jnp.sum(dy * xh, axis=0, keepdims=True)
    return dx, dg


def _sigmoid(x):
    return 1.0 / (1.0 + jnp.exp(-x))


def _gelu(x):
    t = jnp.tanh(GELU_C0 * (x + GELU_C1 * x * x * x))
    return 0.5 * x * (1.0 + t)


def _gelu_grad(x):
    t = jnp.tanh(GELU_C0 * (x + GELU_C1 * x * x * x))
    return 0.5 * (1.0 + t) + 0.5 * x * (1.0 - t * t) * GELU_C0 * (1.0 + 3.0 * GELU_C1 * x * x)


def _split_bf16(x):
    hi = x.astype(BF16)
    lo = (x - hi.astype(F32)).astype(BF16)
    return hi, lo


def _row_spec(tm, cols):
    return pl.BlockSpec((tm, cols), lambda i: (i, 0))


def _full_spec(shape):
    nd = len(shape)
    return pl.BlockSpec(tuple(shape), lambda i: (0,) * nd)


def _token_tile(s):
    return min(256, s)


def _row_call(body, name, s, tiled_in, full_in, tiled_out, acc_out):
    tm = _token_tile(s)
    in_specs = [_row_spec(tm, a.shape[1]) for a in tiled_in] + [_full_spec(a.shape) for a in full_in]
    out_specs = [_row_spec(tm, c) for c, _ in tiled_out] + [_full_spec(sh) for sh, _ in acc_out]
    out_shape = [jax.ShapeDtypeStruct((s, c), dt) for c, dt in tiled_out]
    out_shape += [jax.ShapeDtypeStruct(sh, dt) for sh, dt in acc_out]
    return pl.pallas_call(
        functools.partial(body),
        name=name,
        grid=(s // tm,),
        in_specs=in_specs,
        out_specs=out_specs,
        out_shape=out_shape,
        compiler_params=pltpu.CompilerParams(
            dimension_semantics=("arbitrary",), vmem_limit_bytes=VMEM_LIMIT_V7X),
    )(*tiled_in, *full_in)


def _acc(ref, val):
    @pl.when(pl.program_id(0) == 0)
    def _():
        ref[...] = val

    @pl.when(pl.program_id(0) != 0)
    def _():
        ref[...] += val


def _ffn_fwd_body(x_ref, pre_ref, post_ref, wgt_ref, wut_ref, wd_ref,
                  h_ref, n_ref, a_ref, b_ref, f_ref):
    x = x_ref[...]
    n, _ = _rms(x, pre_ref[...])
    nb = n.astype(BF16)
    n_ref[...] = nb
    a = _dot_nt(nb, wgt_ref[...])
    b = _dot_nt(nb, wut_ref[...])
    a_ref[...] = a.astype(BF16)
    b_ref[...] = b.astype(BF16)
    hmid = a * _sigmoid(a) * b
    f = jnp.dot(hmid.astype(BF16), wd_ref[...], preferred_element_type=F32)
    f_ref[...] = f
    y, _ = _rms(f, post_ref[...])
    h_ref[...] = x + 0.5 * y


def _ffn_fwd(x, pre_g, post_g, wgt, wut, wd, name):
    s, d = x.shape
    f = wgt.shape[0]
    return _row_call(_ffn_fwd_body, name, s, [x], [pre_g, post_g, wgt, wut, wd],
                     [(d, F32), (d, BF16), (f, BF16), (f, BF16), (d, F32)], [])


def _ffn_bwd_body(dh_ref, x_ref, a_ref, b_ref, f_ref, pre_ref, post_ref, wgt_ref, wut_ref, wd_ref,
                  dx_ref, da_ref, db_ref, hm_ref, df_ref, dpre_ref, dpost_ref):
    dh = dh_ref[...]
    df, dpost = _rms_bwd(f_ref[...], post_ref[...], 0.5 * dh)
    dfb = df.astype(BF16)
    df_ref[...] = dfb
    dhmid = _dot_nt(dfb, wd_ref[...])
    a = a_ref[...].astype(F32)
    b = b_ref[...].astype(F32)
    sig = _sigmoid(a)
    sa = a * sig
    hm_ref[...] = (sa * b).astype(BF16)
    dab = (dhmid * b * sig * (1.0 + a * (1.0 - sig))).astype(BF16)
    dbb = (dhmid * sa).astype(BF16)
    da_ref[...] = dab
    db_ref[...] = dbb
    dn = _dot(dab, wgt_ref[...]) + _dot(dbb, wut_ref[...])
    dxn, dpre = _rms_bwd(x_ref[...], pre_ref[...], dn)
    dx_ref[...] = dh + dxn
    _acc(dpre_ref, dpre)
    _acc(dpost_ref, dpost)


def _ffn_bwd(dh, x, a, b, f, pre_g, post_g, wgt, wut, wd, name):
    s, d = x.shape
    ff = wgt.shape[0]
    return _row_call(_ffn_bwd_body, name, s, [dh, x, a, b, f], [pre_g, post_g, wgt, wut, wd],
                     [(d, F32), (ff, BF16), (ff, BF16), (ff, BF16), (d, BF16)],
                     [((1, d), F32), ((1, d), F32)])


def _inproj_fwd_body(h_ref, g_ref, wt_ref, n_ref, uv_ref, qkv_ref):
    n, _ = _rms(h_ref[...], g_ref[...])
    nb = n.astype(BF16)
    n_ref[...] = nb
    proj = _dot_nt(nb, wt_ref[...])
    nuv = uv_ref.shape[1]
    uv_ref[...] = proj[:, :nuv]
    qkv_ref[...] = proj[:, nuv:].astype(BF16)


def _inproj_fwd(h, g, w_in_t):
    s, d = h.shape
    sgu_w = SGU_GROUPS * GROUP_DIM
    sb_w = SB_HEADS * SB_HEAD_DIM
    return _row_call(_inproj_fwd_body, "inproj_fwd", s, [h], [g, w_in_t],
                     [(d, BF16), (2 * sgu_w, F32), (3 * sb_w, BF16)], [])


def _inproj_bwd_body(dh_ref, dproj_ref, h_ref, g_ref, wt_ref, dhout_ref, dg_ref):
    dn = _dot(dproj_ref[...], wt_ref[...])
    dhn, dg = _rms_bwd(h_ref[...], g_ref[...], dn)
    dhout_ref[...] = dh_ref[...] + dhn
    _acc(dg_ref, dg)


def _inproj_bwd(dh, dproj, h, g, w_in_t):
    s, d = h.shape
    return _row_call(_inproj_bwd_body, "inproj_bwd", s, [dh, dproj, h], [g, w_in_t],
                     [(d, F32)], [((1, d), F32)])


def _causal_w(ws_ref, g):
    row = lax.broadcasted_iota(jnp.int32, (CHUNK, CHUNK), 0)
    col = lax.broadcasted_iota(jnp.int32, (CHUNK, CHUNK), 1)
    return jnp.where(row >= col, ws_ref[g], 0.0), row >= col


def _group_norm(v):
    mu = jnp.mean(v, axis=-1, keepdims=True)
    d = v - mu
    rstd = lax.rsqrt(jnp.mean(d * d, axis=-1, keepdims=True) + EPS)
    return d * rstd, rstd


def _sgu_fwd_body(uv_ref, ng_ref, nb_ref, ws_ref, bs_ref, out_ref):
    width = SGU_GROUPS * GROUP_DIM
    for c in range(uv_ref.shape[0] // CHUNK):
        rows = pl.ds(c * CHUNK, CHUNK)
        for g in range(SGU_GROUPS):
            lanes = pl.ds(g * GROUP_DIM, GROUP_DIM)
            u = _gelu(uv_ref[rows, lanes])
            v = _gelu(uv_ref[rows, pl.ds(width + g * GROUP_DIM, GROUP_DIM)])
            vhat, _ = _group_norm(v)
            vn = vhat * ng_ref[:, lanes] + nb_ref[:, lanes]
            w, _ = _causal_w(ws_ref, g)
            mixed = _dot(w, vn) + bs_ref[g]
            out_ref[rows, lanes] = u * mixed


def _sgu_fwd(uv_pre, ng, nb, ws, bs):
    s = uv_pre.shape[0]
    return _row_call(_sgu_fwd_body, "sgu_fwd", s, [uv_pre], [ng, nb, ws, bs],
                     [(SGU_GROUPS * GROUP_DIM, F32)], [])[0]


def _sgu_bwd_body(uv_ref, do_ref, ng_ref, nb_ref, ws_ref, bs_ref,
                  duv_ref, dws_ref, dbs_ref, dng_ref, dnb_ref):
    width = SGU_GROUPS * GROUP_DIM

    @pl.when(pl.program_id(0) == 0)
    def _():
        dws_ref[...] = jnp.zeros_like(dws_ref)
        dbs_ref[...] = jnp.zeros_like(dbs_ref)
        dng_ref[...] = jnp.zeros_like(dng_ref)
        dnb_ref[...] = jnp.zeros_like(dnb_ref)

    for c in range(uv_ref.shape[0] // CHUNK):
        rows = pl.ds(c * CHUNK, CHUNK)
        for g in range(SGU_GROUPS):
            lanes = pl.ds(g * GROUP_DIM, GROUP_DIM)
            vlanes = pl.ds(width + g * GROUP_DIM, GROUP_DIM)
            u_pre = uv_ref[rows, lanes]
            v_pre = uv_ref[rows, vlanes]
            u = _gelu(u_pre)
            v = _gelu(v_pre)
            vhat, rstd = _group_norm(v)
            gain = ng_ref[:, lanes]
            vn = vhat * gain + nb_ref[:, lanes]
            w, causal = _causal_w(ws_ref, g)
            mixed = _dot(w, vn) + bs_ref[g]
            dout = do_ref[rows, lanes]
            du = dout * mixed
            dmixed = dout * u
            dbs_ref[g] += jnp.sum(dmixed, axis=1, keepdims=True)
            dws_ref[g] += jnp.where(causal, _dot_nt(dmixed, vn), 0.0)
            dvn = _dot_tn(w, dmixed)
            dng_ref[:, lanes] += jnp.sum(dvn * vhat, axis=0, keepdims=True)
            dnb_ref[:, lanes] += jnp.sum(dvn, axis=0, keepdims=True)
            dvh = dvn * gain
            dv = rstd * (dvh - jnp.mean(dvh, axis=-1, keepdims=True)
                         - vhat * jnp.mean(dvh * vhat, axis=-1, keepdims=True))
            duv_ref[rows, lanes] = (du * _gelu_grad(u_pre)).astype(BF16)
            duv_ref[rows, vlanes] = (dv * _gelu_grad(v_pre)).astype(BF16)


def _sgu_bwd(uv_pre, dout_a, ng, nb, ws, bs):
    s = uv_pre.shape[0]
    width = SGU_GROUPS * GROUP_DIM
    return _row_call(_sgu_bwd_body, "sgu_bwd", s, [uv_pre, dout_a], [ng, nb, ws, bs],
                     [(2 * width, BF16)],
                     [(ws.shape, F32), (bs.shape, F32), ((1, width), F32), ((1, width), F32)])


def _sb_logs(q, k, diagonal):
    row = lax.broadcasted_iota(jnp.int32, (Q_BLOCK, Q_BLOCK), 0)
    col = lax.broadcasted_iota(jnp.int32, (Q_BLOCK, Q_BLOCK), 1)
    strict = jnp.logical_or(jnp.logical_not(diagonal), col < row)
    z = _dot_nt(q, k) * (SB_HEAD_DIM ** -0.5)
    sp = jnp.log1p(jnp.exp(-jnp.abs(z)))
    log_beta = jnp.minimum(z, 0.0) - sp
    log_1m_raw = -jnp.maximum(z, 0.0) - sp
    return strict, log_beta, log_1m_raw, jnp.where(strict, log_1m_raw, 0.0)


def _key_sums(x, pick):
    hi, lo = _split_bf16(x)
    return jnp.dot(hi, pick, preferred_element_type=F32) + jnp.dot(lo, pick, preferred_element_type=F32)


def _key_order():
    row = lax.broadcasted_iota(jnp.int32, (Q_BLOCK, Q_BLOCK), 0)
    col = lax.broadcasted_iota(jnp.int32, (Q_BLOCK, Q_BLOCK), 1)
    return row, col


def _sb_fwd_body(q_ref, k_ref, v_ref, o_ref, tot_ref):
    qb = pl.program_id(1)
    q = q_ref[0]
    row, col = _key_order()
    later = (row > col).astype(BF16)

    def step(i, carry):
        c, acc = carry
        off = pl.multiple_of((qb - i) * Q_BLOCK, Q_BLOCK)
        k = k_ref[0, pl.ds(off, Q_BLOCK), :]
        v = v_ref[0, pl.ds(off, Q_BLOCK), :]
        strict, log_beta, _, log_1m = _sb_logs(q, k, i == 0)
        a = jnp.where(strict, jnp.exp(log_beta + _key_sums(log_1m, later) + c), 0.0)
        acc = acc + jnp.dot(a.astype(BF16), v, preferred_element_type=F32)
        return c + jnp.sum(log_1m, axis=1, keepdims=True), acc

    init = (jnp.zeros((Q_BLOCK, 1), F32), jnp.zeros((Q_BLOCK, SB_HEAD_DIM), F32))
    c, acc = lax.fori_loop(0, qb + 1, step, init)
    o_ref[0] = acc
    tot_ref[0] = c


def _sb_fwd(q3, k3, v3):
    h, s, dh = q3.shape
    blk = pl.BlockSpec((1, Q_BLOCK, dh), lambda hh, i: (hh, i, 0))
    col1 = pl.BlockSpec((1, Q_BLOCK, 1), lambda hh, i: (hh, i, 0))
    whole = pl.BlockSpec((1, s, dh), lambda hh, i: (hh, 0, 0))
    return pl.pallas_call(
        functools.partial(_sb_fwd_body),
        name="sb_fwd",
        grid=(h, s // Q_BLOCK),
        in_specs=[blk, whole, whole],
        out_specs=[blk, col1],
        out_shape=[jax.ShapeDtypeStruct((h, s, dh), F32), jax.ShapeDtypeStruct((h, s, 1), F32)],
        compiler_params=pltpu.CompilerParams(
            dimension_semantics=("arbitrary", "arbitrary"), vmem_limit_bytes=VMEM_LIMIT_V7X),
    )(q3, k3, v3)


def _sb_bwd_body(q_ref, k_ref, v_ref, tot_ref, do_ref, dq_ref, dk_ref, dv_ref):
    qb = pl.program_id(1)

    @pl.when(qb == 0)
    def _():
        dk_ref[...] = jnp.zeros_like(dk_ref)
        dv_ref[...] = jnp.zeros_like(dv_ref)

    q = q_ref[0]
    dob = do_ref[0].astype(BF16)
    tot = tot_ref[0]
    row, col = _key_order()
    up_to = (row <= col).astype(BF16)
    earlier = (row < col).astype(BF16)
    scale = SB_HEAD_DIM ** -0.5

    def step(i, carry):
        c, ce, dq = carry
        rows = pl.ds(pl.multiple_of(i * Q_BLOCK, Q_BLOCK), Q_BLOCK)
        k = k_ref[0, rows, :]
        v = v_ref[0, rows, :]
        strict, log_beta, log_1m_raw, log_1m = _sb_logs(q, k, i == qb)
        suffix = tot - c - _key_sums(log_1m, up_to)
        a = jnp.where(strict, jnp.exp(log_beta + suffix), 0.0)
        de = _dot_nt(dob, v) * a
        before = ce + _key_sums(de, earlier)
        dz = jnp.where(strict, de * jnp.exp(log_1m_raw) - before * jnp.exp(log_beta), 0.0)
        dzb = dz.astype(BF16)
        dq = dq + jnp.dot(dzb, k, preferred_element_type=F32)
        dk_ref[0, rows, :] += _dot_tn(dzb, q) * scale
        dv_ref[0, rows, :] += _dot_tn(a, dob)
        return (c + jnp.sum(log_1m, axis=1, keepdims=True),
                ce + jnp.sum(de, axis=1, keepdims=True), dq)

    init = (jnp.zeros((Q_BLOCK, 1), F32), jnp.zeros((Q_BLOCK, 1), F32),
            jnp.zeros((Q_BLOCK, SB_HEAD_DIM), F32))
    _, _, dq = lax.fori_loop(0, qb + 1, step, init)
    dq_ref[0] = dq * scale


def _sb_bwd(q3, k3, v3, tot3, do3):
    h, s, dh = q3.shape
    blk = pl.BlockSpec((1, Q_BLOCK, dh), lambda hh, i: (hh, i, 0))
    col1 = pl.BlockSpec((1, Q_BLOCK, 1), lambda hh, i: (hh, i, 0))
    whole = pl.BlockSpec((1, s, dh), lambda hh, i: (hh, 0, 0))
    out = jax.ShapeDtypeStruct((h, s, dh), F32)
    return pl.pallas_call(
        functools.partial(_sb_bwd_body),
        name="sb_bwd",
        grid=(h, s // Q_BLOCK),
        in_specs=[blk, whole, whole, col1, blk],
        out_specs=[blk, whole, whole],
        out_shape=[out, out, out],
        compiler_params=pltpu.CompilerParams(
            dimension_semantics=("arbitrary", "arbitrary"), vmem_limit_bytes=VMEM_LIMIT_V7X),
    )(q3, k3, v3, tot3, do3)


def _to_heads(t):
    s = t.shape[0]
    return t.reshape(s, SB_HEADS, SB_HEAD_DIM).transpose(1, 0, 2)


def _from_heads(t3):
    s = t3.shape[1]
    return t3.transpose(1, 0, 2).reshape(s, SB_HEADS * SB_HEAD_DIM)


def _outproj_fwd_body(oa_ref, ob_ref, h_ref, ga_ref, gb_ref, gpost_ref, w_ref,
                      merged_ref, mo_ref, hout_ref):
    half = oa_ref.shape[1]
    ma, _ = _rms(oa_ref[...], ga_ref[...])
    mb, _ = _rms(ob_ref[...], gb_ref[...])
    mab = ma.astype(BF16)
    mbb = mb.astype(BF16)
    merged_ref[:, :half] = mab
    merged_ref[:, half:] = mbb
    mo = (jnp.dot(mab, w_ref[:half, :], preferred_element_type=F32)
          + jnp.dot(mbb, w_ref[half:, :], preferred_element_type=F32))
    mo_ref[...] = mo
    y, _ = _rms(mo, gpost_ref[...])
    hout_ref[...] = h_ref[...] + y


def _outproj_fwd(out_a, out_b, h, ga, gb, gpost, w_out):
    s, d = h.shape
    return _row_call(_outproj_fwd_body, "outproj_fwd", s, [out_a, out_b, h], [ga, gb, gpost, w_out],
                     [(d, BF16), (d, F32), (d, F32)], [])


def _outproj_bwd_body(dh_ref, mo_ref, oa_ref, ob_ref, ga_ref, gb_ref, gpost_ref, w_ref,
                      dmo_ref, doa_ref, dob_ref, dga_ref, dgb_ref, dgpost_ref):
    half = oa_ref.shape[1]
    dmo, dgpost = _rms_bwd(mo_ref[...], gpost_ref[...], dh_ref[...])
    dmob = dmo.astype(BF16)
    dmo_ref[...] = dmob
    dma = _dot_nt(dmob, w_ref[:half, :])
    dmb = _dot_nt(dmob, w_ref[half:, :])
    doa, dga = _rms_bwd(oa_ref[...], ga_ref[...], dma)
    dob, dgb = _rms_bwd(ob_ref[...], gb_ref[...], dmb)
    doa_ref[...] = doa
    dob_ref[...] = dob
    _acc(dga_ref, dga)
    _acc(dgb_ref, dgb)
    _acc(dgpost_ref, dgpost)


def _outproj_bwd(dh, mo, out_a, out_b, ga, gb, gpost, w_out):
    s, d = dh.shape
    half = out_a.shape[1]
    return _row_call(_outproj_bwd_body, "outproj_bwd", s, [dh, mo, out_a, out_b], [ga, gb, gpost, w_out],
                     [(d, BF16), (half, F32), (half, F32)],
                     [((1, half), F32), ((1, half), F32), ((1, d), F32)])


def _kv_fwd_body(mem_ref, g_ref, wt_ref, memn_ref, kv_ref):
    n, _ = _rms(mem_ref[...], g_ref[...])
    nb = n.astype(BF16)
    memn_ref[...] = nb
    kv_ref[...] = _dot_nt(nb, wt_ref[...]).astype(BF16)


def _kv_fwd(mem, g, w_kv_t):
    m, d = mem.shape
    return _row_call(_kv_fwd_body, "kv_fwd", m, [mem], [g, w_kv_t], [(d, BF16), (w_kv_t.shape[0], BF16)], [])


def _kv_bwd_body(dkv_ref, mem_ref, memn_ref, g_ref, wt_ref, dwt_ref, dg_ref):
    dkvb = dkv_ref[...].astype(BF16)
    dwt_ref[...] = _dot_tn(dkvb, memn_ref[...])
    dmemn = _dot(dkvb, wt_ref[...])
    _, dg = _rms_bwd(mem_ref[...], g_ref[...], dmemn)
    dg_ref[...] = dg


def _kv_bwd(dkv, mem, memn, g, w_kv_t):
    m, d = mem.shape
    return pl.pallas_call(
        functools.partial(_kv_bwd_body),
        name="kv_bwd",
        out_shape=[jax.ShapeDtypeStruct(w_kv_t.shape, F32), jax.ShapeDtypeStruct((1, d), F32)],
        compiler_params=pltpu.CompilerParams(vmem_limit_bytes=VMEM_LIMIT_V7X),
    )(dkv, mem, memn, g, w_kv_t)


def _xa_fwd_body(h_ref, gpre_ref, gpost_ref, wq_ref, wo_ref, kv_ref,
                 n_ref, q_ref, o_ref, c_ref, hout_ref):
    h = h_ref[...]
    d = h.shape[1]
    n, _ = _rms(h, gpre_ref[...])
    nb = n.astype(BF16)
    n_ref[...] = nb
    qb = jnp.dot(nb, wq_ref[...], preferred_element_type=F32).astype(BF16)
    q_ref[...] = qb
    for hd in range(XA_HEADS):
        lanes = slice(hd * XA_HEAD_DIM, (hd + 1) * XA_HEAD_DIM)
        k = kv_ref[:, lanes]
        v = kv_ref[:, d + hd * XA_HEAD_DIM:d + (hd + 1) * XA_HEAD_DIM]
        logits = _dot_nt(qb[:, lanes], k) * (XA_HEAD_DIM ** -0.5)
        e = jnp.exp(logits - jnp.max(logits, axis=-1, keepdims=True))
        p = e / jnp.sum(e, axis=-1, keepdims=True)
        o_ref[:, lanes] = jnp.dot(p.astype(BF16), v, preferred_element_type=F32).astype(BF16)
    c = jnp.dot(o_ref[...], wo_ref[...], preferred_element_type=F32)
    c_ref[...] = c
    y, _ = _rms(c, gpost_ref[...])
    hout_ref[...] = h + y


def _xa_fwd(h, gpre, gpost, wq, wo, kv):
    s, d = h.shape
    return _row_call(_xa_fwd_body, "xa_fwd", s, [h], [gpre, gpost, wq, wo, kv],
                     [(d, BF16), (d, BF16), (d, BF16), (d, F32), (d, F32)], [])


def _xa_bwd_body(dh_ref, h_ref, c_ref, q_ref, o_ref, gpre_ref, gpost_ref, wq_ref, wo_ref, kv_ref,
                 dhout_ref, dc_ref, dq_ref, dkv_ref, dgpre_ref, dgpost_ref):
    dh = dh_ref[...]
    d = dh.shape[1]
    scale = XA_HEAD_DIM ** -0.5
    dc, dgpost = _rms_bwd(c_ref[...], gpost_ref[...], dh)
    dcb = dc.astype(BF16)
    dc_ref[...] = dcb
    dob = _dot_nt(dcb, wo_ref[...]).astype(BF16)

    @pl.when(pl.program_id(0) == 0)
    def _():
        dkv_ref[...] = jnp.zeros_like(dkv_ref)

    for hd in range(XA_HEADS):
        lanes = slice(hd * XA_HEAD_DIM, (hd + 1) * XA_HEAD_DIM)
        vlanes = slice(d + hd * XA_HEAD_DIM, d + (hd + 1) * XA_HEAD_DIM)
        qh = q_ref[:, lanes]
        k = kv_ref[:, lanes]
        v = kv_ref[:, vlanes]
        logits = _dot_nt(qh, k) * scale
        e = jnp.exp(logits - jnp.max(logits, axis=-1, keepdims=True))
        p = e / jnp.sum(e, axis=-1, keepdims=True)
        doh = dob[:, lanes]
        dp = _dot_nt(doh, v)
        dl = (p * (dp - jnp.sum(dp * p, axis=-1, keepdims=True)) * scale).astype(BF16)
        dq_ref[:, lanes] = jnp.dot(dl, k, preferred_element_type=F32).astype(BF16)
        dkv_ref[:, lanes] += _dot_tn(dl, qh)
        dkv_ref[:, vlanes] += _dot_tn(p, doh)
    dn = _dot_nt(dq_ref[...], wq_ref[...])
    dhn, dgpre = _rms_bwd(h_ref[...], gpre_ref[...], dn)
    dhout_ref[...] = dh + dhn
    _acc(dgpre_ref, dgpre)
    _acc(dgpost_ref, dgpost)


def _xa_bwd(dh, h, c, q, o, gpre, gpost, wq, wo, kv):
    s, d = h.shape
    return _row_call(_xa_bwd_body, "xa_bwd", s, [dh, h, c, q, o], [gpre, gpost, wq, wo, kv],
                     [(d, F32), (d, BF16), (d, BF16)],
                     [(kv.shape, F32), ((1, d), F32), ((1, d), F32)])


def _final_body(h_ref, t_ref, g_ref, dh_ref, loss_ref, dg_ref):
    h = h_ref[...]
    d = h.shape[1]
    y, _ = _rms(h, g_ref[...])
    err = y - t_ref[...]
    part = (0.5 / d) * jnp.sum(jnp.sum(err * err, axis=1, keepdims=True), axis=0, keepdims=True)
    dh, dg = _rms_bwd(h, g_ref[...], err * (1.0 / d))
    dh_ref[...] = dh
    _acc(loss_ref, part)
    _acc(dg_ref, dg)


def _final(h, g, target):
    s, d = h.shape
    return _row_call(_final_body, "final_loss", s, [h, target], [g],
                     [(d, F32)], [((1, 1), F32), ((1, d), F32)])


def _largest_tile(n, cap):
    best = 128
    for t in range(128, cap + 1, 128):
        if n % t == 0:
            best = t
    return best


def _mm_tn(a, bs, name):
    s, k = a.shape
    n = bs[0].shape[1]
    nb = len(bs)
    ts = min(512, s)
    tk = _largest_tile(k, 1536)
    tn = _largest_tile(n, 1536 // nb)

    def body(a_ref, *refs):
        b_refs, o_refs = refs[:nb], refs[nb:]
        at = a_ref[...]

        @pl.when(pl.program_id(2) == 0)
        def _():
            for o_ref in o_refs:
                o_ref[...] = jnp.zeros_like(o_ref)

        for b_ref, o_ref in zip(b_refs, o_refs):
            o_ref[...] += _dot_tn(at, b_ref[...])

    return pl.pallas_call(
        body,
        name=name,
        grid=(k // tk, n // tn, s // ts),
        in_specs=[pl.BlockSpec((ts, tk), lambda i, j, t: (t, i))]
        + [pl.BlockSpec((ts, tn), lambda i, j, t: (t, j))] * nb,
        out_specs=[pl.BlockSpec((tk, tn), lambda i, j, t: (i, j))] * nb,
        out_shape=[jax.ShapeDtypeStruct((k, n), F32)] * nb,
        compiler_params=pltpu.CompilerParams(
            dimension_semantics=("arbitrary", "arbitrary", "arbitrary"),
            vmem_limit_bytes=VMEM_LIMIT_V7X),
    )(a, *bs)


_SMALL_SHAPES = {
    "sgu_norm_g": (1, SGU_GROUPS * GROUP_DIM),
    "sgu_norm_b": (1, SGU_GROUPS * GROUP_DIM),
    "sgu_w_s": (SGU_GROUPS, CHUNK, CHUNK),
    "sgu_b_s": (SGU_GROUPS, CHUNK, 1),
}


def _small_views(small):
    return {n: v.reshape(_SMALL_SHAPES.get(n, v.shape)) for n, v in small.items()}


def _small_unviews(views, like):
    return {n: v.reshape(like[n].shape) for n, v in views.items()}


def _local_step(x, mem, target, small, big):
    sm, w = small, big
    h1, n1, a1, b1, f1 = _ffn_fwd(x, sm["ffn1_pre_g"], sm["ffn1_post_g"],
                                  w["ffn1_w_gate"], w["ffn1_w_up"], w["ffn1_w_down"], "ffn1_fwd")
    n2, uv_pre, qkv = _inproj_fwd(h1, sm["mix_pre_g"], w["w_in"])
    out_a = _sgu_fwd(uv_pre, sm["sgu_norm_g"], sm["sgu_norm_b"], sm["sgu_w_s"], sm["sgu_b_s"])
    sbw = SB_HEADS * SB_HEAD_DIM
    q3, k3, v3 = (_to_heads(qkv[:, i * sbw:(i + 1) * sbw]) for i in range(3))
    ob3, tot3 = _sb_fwd(q3, k3, v3)
    out_b = _from_heads(ob3)
    merged, mo, h2 = _outproj_fwd(out_a, out_b, h1, sm["sgu_out_g"], sm["sb_out_g"],
                                  sm["mix_post_g"], w["w_out"])
    memn, kv = _kv_fwd(mem, sm["mem_norm_g"], w["xa_w_kv"])
    n3, qx, ox, cx, h3 = _xa_fwd(h2, sm["xa_pre_g"], sm["xa_post_g"], w["xa_w_q"], w["xa_w_o"], kv)
    h4, n4, a2, b2, f2 = _ffn_fwd(h3, sm["ffn2_pre_g"], sm["ffn2_post_g"],
                                  w["ffn2_w_gate"], w["ffn2_w_up"], w["ffn2_w_down"], "ffn2_fwd")
    dh4, loss, dg_final = _final(h4, sm["final_norm_g"], target)

    gs, gw = {"final_norm_g": dg_final}, {}
    dh3, da2, db2, hm2, df2, gs["ffn2_pre_g"], gs["ffn2_post_g"] = _ffn_bwd(
        dh4, h3, a2, b2, f2, sm["ffn2_pre_g"], sm["ffn2_post_g"],
        w["ffn2_w_gate"], w["ffn2_w_up"], w["ffn2_w_down"], "ffn2_bwd")
    gw["ffn2_w_gate"], = _mm_tn(da2, [n4], "ffn2_dw_gate")
    gw["ffn2_w_up"], = _mm_tn(db2, [n4], "ffn2_dw_up")
    gw["ffn2_w_down"], = _mm_tn(hm2, [df2], "ffn2_dw_down")

    dh2, dc, dqx, dkv, gs["xa_pre_g"], gs["xa_post_g"] = _xa_bwd(
        dh3, h2, cx, qx, ox, sm["xa_pre_g"], sm["xa_post_g"], w["xa_w_q"], w["xa_w_o"], kv)
    gw["xa_w_o"], = _mm_tn(ox, [dc], "xa_dw_o")
    gw["xa_w_q"], = _mm_tn(n3, [dqx], "xa_dw_q")
    gw["xa_w_kv"], gs["mem_norm_g"] = _kv_bwd(dkv, mem, memn, sm["mem_norm_g"], w["xa_w_kv"])

    dmo, dout_a, dout_b, gs["sgu_out_g"], gs["sb_out_g"], gs["mix_post_g"] = _outproj_bwd(
        dh2, mo, out_a, out_b, sm["sgu_out_g"], sm["sb_out_g"], sm["mix_post_g"], w["w_out"])
    gw["w_out"], = _mm_tn(merged, [dmo], "mix_dw_out")
    dq3, dk3, dv3 = _sb_bwd(q3, k3, v3, tot3, _to_heads(dout_b))
    duv, gs["sgu_w_s"], gs["sgu_b_s"], gs["sgu_norm_g"], gs["sgu_norm_b"] = _sgu_bwd(
        uv_pre, dout_a, sm["sgu_norm_g"], sm["sgu_norm_b"], sm["sgu_w_s"], sm["sgu_b_s"])
    dproj = jnp.concatenate(
        [duv] + [_from_heads(t).astype(BF16) for t in (dq3, dk3, dv3)], axis=1)
    dh1, gs["mix_pre_g"] = _inproj_bwd(dh2, dproj, h1, sm["mix_pre_g"], w["w_in"])
    gw["w_in"], = _mm_tn(dproj, [n2], "mix_dw_in")

    dx, da1, db1, hm1, df1, gs["ffn1_pre_g"], gs["ffn1_post_g"] = _ffn_bwd(
        dh1, x, a1, b1, f1, sm["ffn1_pre_g"], sm["ffn1_post_g"],
        w["ffn1_w_gate"], w["ffn1_w_up"], w["ffn1_w_down"], "ffn1_bwd")
    gw["ffn1_w_gate"], = _mm_tn(da1, [n1], "ffn1_dw_gate")
    gw["ffn1_w_up"], = _mm_tn(db1, [n1], "ffn1_dw_up")
    gw["ffn1_w_down"], = _mm_tn(hm1, [df1], "ffn1_dw_down")
    return loss, dx, gs, gw


def _mesh_place():
    return lax.axis_index("x"), lax.axis_index("y"), lax.axis_index("c")


def _other_chips(mx, my):
    return [(1 - mx, my), (mx, 1 - my), (1 - mx, 1 - my)]


_ANY = pl.BlockSpec(memory_space=pl.ANY)


def _all_gather(x, name):
    r, c = x.shape

    def body(x_ref, out_ref, send_sems, recv_sems, local_sem):
        mx, my, mc = _mesh_place()
        me, sibling = (mx, my, mc), (mx, my, 1 - mc)
        chips = _other_chips(mx, my)

        def slot(px, py, pc):
            return out_ref.at[4 * px + 2 * py + pc]

        def copy(k, block, to, src=None):
            return pltpu.make_async_remote_copy(
                src_ref=slot(*block) if src is None else src, dst_ref=slot(*block),
                send_sem=send_sems.at[k], recv_sem=recv_sems.at[k],
                device_id=to, device_id_type=MESH)

        mine = pltpu.make_async_copy(x_ref, slot(*me), local_sem)
        mine.start()
        first = [copy(0, me, sibling, src=x_ref)]
        first += [copy(1 + j, me, (*chip, mc), src=x_ref) for j, chip in enumerate(chips)]
        for cp in first:
            cp.start()
        passed = [copy(4 + j, (*chip, mc), sibling) for j, chip in enumerate(chips)]
        for j, chip in enumerate(chips):
            copy(1 + j, (*chip, mc), me).wait_recv()
            passed[j].start()
        copy(0, sibling, me).wait_recv()
        for j, chip in enumerate(chips):
            copy(4 + j, (*chip, 1 - mc), me).wait_recv()
        for cp in first + passed:
            cp.wait_send()
        mine.wait()

    return pl.pallas_call(
        body,
        name=name,
        out_shape=jax.ShapeDtypeStruct((N_DEV, r, c), x.dtype),
        in_specs=[_ANY],
        out_specs=_ANY,
        scratch_shapes=[pltpu.SemaphoreType.DMA((7,)), pltpu.SemaphoreType.DMA((7,)),
                        pltpu.SemaphoreType.DMA(())],
    )(x)


def _pair_exchange(p4):
    nchip, _, r, c = p4.shape

    def body(p_ref, out_ref, send_sems, recv_sems):
        mx, my, mc = _mesh_place()
        copies = [pltpu.make_async_remote_copy(
            src_ref=p_ref.at[j, 1 - mc], dst_ref=out_ref.at[j],
            send_sem=send_sems.at[j], recv_sem=recv_sems.at[j],
            device_id=(mx, my, 1 - mc), device_id_type=MESH) for j in range(nchip)]
        for cp in copies:
            cp.start()
        for cp in copies:
            cp.wait()

    return pl.pallas_call(
        body,
        name="rs_pair_exchange",
        out_shape=jax.ShapeDtypeStruct((nchip, r, c), p4.dtype),
        in_specs=[_ANY],
        out_specs=_ANY,
        scratch_shapes=[pltpu.SemaphoreType.DMA((nchip,)), pltpu.SemaphoreType.DMA((nchip,))],
    )(p4)


def _chip_exchange(q):
    _, r, c = q.shape

    def body(q_ref, out_ref, send_sems, recv_sems):
        mx, my, mc = _mesh_place()
        copies = [pltpu.make_async_remote_copy(
            src_ref=q_ref.at[2 * cx + cy], dst_ref=out_ref.at[k],
            send_sem=send_sems.at[k], recv_sem=recv_sems.at[k],
            device_id=(cx, cy, mc), device_id_type=MESH)
            for k, (cx, cy) in enumerate(_other_chips(mx, my))]
        for cp in copies:
            cp.start()
        for cp in copies:
            cp.wait()

    return pl.pallas_call(
        body,
        name="rs_chip_exchange",
        out_shape=jax.ShapeDtypeStruct((3, r, c), q.dtype),
        in_specs=[_ANY],
        out_specs=_ANY,
        scratch_shapes=[pltpu.SemaphoreType.DMA((3,)), pltpu.SemaphoreType.DMA((3,))],
    )(q)


def _rs_row_tile(r):
    return _largest_tile(r, 512)


def _pair_sum(place, p4, recv_a):
    nchip, _, r, c = p4.shape
    tr = _rs_row_tile(r)

    def body(place_ref, p_ref, a_ref, q_ref):
        q_ref[0] = (p_ref[0, 0].astype(F32) + a_ref[0].astype(F32)).astype(BF16)

    return pl.pallas_call(
        body,
        name="rs_pair_sum",
        grid_spec=pltpu.PrefetchScalarGridSpec(
            num_scalar_prefetch=1,
            grid=(nchip, r // tr),
            in_specs=[pl.BlockSpec((1, 1, tr, c), lambda j, i, pref: (j, pref[0], i, 0)),
                      pl.BlockSpec((1, tr, c), lambda j, i, pref: (j, i, 0))],
            out_specs=pl.BlockSpec((1, tr, c), lambda j, i, pref: (j, i, 0)),
        ),
        out_shape=jax.ShapeDtypeStruct((nchip, r, c), BF16),
    )(place, p4, recv_a)


def _rs_final(place, p4, recv_a, recv_b):
    _, _, r, c = p4.shape
    tr = _rs_row_tile(r)

    def body(place_ref, p_ref, a_ref, b_ref, g_ref):
        g = p_ref[0, 0].astype(F32) + a_ref[0].astype(F32)
        for k in range(3):
            g = g + b_ref[k].astype(F32)
        g_ref[...] = g

    return pl.pallas_call(
        body,
        name="rs_final_sum",
        grid_spec=pltpu.PrefetchScalarGridSpec(
            num_scalar_prefetch=1,
            grid=(r // tr,),
            in_specs=[pl.BlockSpec((1, 1, tr, c), lambda i, pref: (pref[1], pref[0], i, 0)),
                      pl.BlockSpec((1, tr, c), lambda i, pref: (pref[1], i, 0)),
                      pl.BlockSpec((3, tr, c), lambda i, pref: (0, i, 0))],
            out_specs=pl.BlockSpec((tr, c), lambda i, pref: (i, 0)),
        ),
        out_shape=jax.ShapeDtypeStruct((r, c), F32),
    )(place, p4, recv_a, recv_b)


def _adamw_math(w, g, m, v):
    m = ADAM_B1 * m + (1.0 - ADAM_B1) * g
    v = ADAM_B2 * v + (1.0 - ADAM_B2) * (g * g)
    m_hat = m / (1.0 - ADAM_B1 ** ADAM_STEP)
    v_hat = v / (1.0 - ADAM_B2 ** ADAM_STEP)
    delta = -ADAM_LR * (m_hat / (jnp.sqrt(v_hat) + ADAM_EPS) + ADAM_WD * w)
    return delta, m, v


def _adamw(w, g, m, v, name):
    r, c = w.shape
    tr = r if r <= 512 else 256

    def body(w_ref, g_ref, m_ref, v_ref, d_ref, mo_ref, vo_ref):
        d_ref[...], mo_ref[...], vo_ref[...] = _adamw_math(w_ref[...], g_ref[...], m_ref[...], v_ref[...])

    spec = pl.BlockSpec((tr, c), lambda i: (i, 0))
    out = jax.ShapeDtypeStruct((r, c), F32)
    return pl.pallas_call(
        body, name=name, grid=(r // tr,), in_specs=[spec] * 4, out_specs=[spec] * 3,
        out_shape=[out] * 3,
    )(w, g, m, v)


def _small_sum_adamw(gathered, w, m, v):
    _, r, c = gathered.shape

    def body(ga_ref, w_ref, m_ref, v_ref, g_ref, d_ref, mo_ref, vo_ref):
        g = ga_ref[0]
        for k in range(1, N_DEV):
            g = g + ga_ref[k]
        g_ref[...] = g
        d_ref[...], mo_ref[...], vo_ref[...] = _adamw_math(w_ref[...], g, m_ref[...], v_ref[...])

    out = jax.ShapeDtypeStruct((r, c), F32)
    return pl.pallas_call(body, name="small_sum_adamw", out_shape=[out] * 4)(gathered, w, m, v)


_WEIGHTS = ["ffn1_pre_g", "ffn1_post_g", "ffn1_w_gate", "ffn1_w_up", "ffn1_w_down", "mix_pre_g",
            "mix_post_g", "w_in", "sgu_norm_g", "sgu_norm_b", "sgu_w_s", "sgu_b_s", "sgu_out_g",
            "sb_out_g", "w_out", "xa_pre_g", "xa_post_g", "mem_norm_g", "xa_w_q", "xa_w_kv", "xa_w_o",
            "ffn2_pre_g", "ffn2_post_g", "ffn2_w_gate", "ffn2_w_up", "ffn2_w_down", "final_norm_g"]
_BIG = ["ffn1_w_gate", "ffn1_w_up", "ffn1_w_down", "w_in", "w_out", "xa_w_q", "xa_w_kv", "xa_w_o",
        "ffn2_w_gate", "ffn2_w_up", "ffn2_w_down"]
_COL_SHARDED = ("ffn1_w_gate", "ffn1_w_up", "w_in", "xa_w_kv", "ffn2_w_gate", "ffn2_w_up")
_SMALL = [n for n in _WEIGHTS if n not in _BIG]
SMALL_LANES = 128
SMALL_ROW_ALIGN = 8


def _pack_small(tensors):
    parts = []
    for n in _SMALL:
        t = tensors[n].reshape(-1, SMALL_LANES)
        pad = (-t.shape[0]) % SMALL_ROW_ALIGN
        parts.append(jnp.pad(t, ((0, pad), (0, 0))) if pad else t)
    return jnp.concatenate(parts, axis=0)


def _unpack_small(packed, like):
    out, off = {}, 0
    for n in _SMALL:
        size = like[n].size
        rows = size // SMALL_LANES
        out[n] = packed[off:off + rows].reshape(like[n].shape)
        off += rows + (-rows) % SMALL_ROW_ALIGN
    return out


def kernel(x, mem, ffn1_pre_g, ffn1_post_g, ffn1_w_gate, ffn1_w_up, ffn1_w_down, mix_pre_g, mix_post_g, w_in, sgu_norm_g, sgu_norm_b, sgu_w_s, sgu_b_s, sgu_out_g, sb_out_g, w_out, xa_pre_g, xa_post_g, mem_norm_g, xa_w_q, xa_w_kv, xa_w_o, ffn2_pre_g, ffn2_post_g, ffn2_w_gate, ffn2_w_up, ffn2_w_down, final_norm_g, loss_target, m_ffn1_pre_g, m_ffn1_post_g, m_ffn1_w_gate, m_ffn1_w_up, m_ffn1_w_down, m_mix_pre_g, m_mix_post_g, m_w_in, m_sgu_norm_g, m_sgu_norm_b, m_sgu_w_s, m_sgu_b_s, m_sgu_out_g, m_sb_out_g, m_w_out, m_xa_pre_g, m_xa_post_g, m_mem_norm_g, m_xa_w_q, m_xa_w_kv, m_xa_w_o, m_ffn2_pre_g, m_ffn2_post_g, m_ffn2_w_gate, m_ffn2_w_up, m_ffn2_w_down, m_final_norm_g, v_ffn1_pre_g, v_ffn1_post_g, v_ffn1_w_gate, v_ffn1_w_up, v_ffn1_w_down, v_mix_pre_g, v_mix_post_g, v_w_in, v_sgu_norm_g, v_sgu_norm_b, v_sgu_w_s, v_sgu_b_s, v_sgu_out_g, v_sb_out_g, v_w_out, v_xa_pre_g, v_xa_post_g, v_mem_norm_g, v_xa_w_q, v_xa_w_kv, v_xa_w_o, v_ffn2_pre_g, v_ffn2_post_g, v_ffn2_w_gate, v_ffn2_w_up, v_ffn2_w_down, v_final_norm_g):
    vals = dict(locals())
    d_model = x.shape[-1]

    pieces = []
    for n in _BIG:
        shard = vals[n][0]
        pieces.append((shard.T if n in _COL_SHARDED else shard).astype(BF16))
    gathered = _all_gather(jnp.concatenate(pieces, axis=0), "ag_weights")
    big, off = {}, 0
    for n, piece in zip(_BIG, pieces):
        rows = piece.shape[0]
        big[n] = gathered[:, off:off + rows, :].reshape(N_DEV * rows, d_model)
        off += rows

    small = {n: vals[n] for n in _SMALL}
    loss_part, dx, gs, gw = _local_step(x[0], mem[0], loss_target[0], _small_views(small), big)
    loss = lax.psum(loss_part[0, 0], ("x", "y", "c"))

    parts = [gw[n].reshape(N_DEV, gw[n].shape[0] // N_DEV, d_model).astype(BF16) for n in _BIG]
    p4 = jnp.concatenate(parts, axis=1).reshape(N_DEV // 2, 2, off, d_model)
    mx, my, mc = _mesh_place()
    place = jnp.stack([mc, 2 * mx + my]).astype(jnp.int32)
    recv_a = _pair_exchange(p4)
    recv_b = _chip_exchange(_pair_sum(place, p4, recv_a))
    g_rows = _rs_final(place, p4, recv_a, recv_b)

    grads, deltas, new_m, new_v = {}, {}, {}, {}
    off = 0
    for n, piece in zip(_BIG, pieces):
        rows = piece.shape[0]
        g = g_rows[off:off + rows]
        off += rows
        g = g.T if n in _COL_SHARDED else g
        grads[n] = g[None]
        d, m1, v1 = _adamw(vals[n][0], g, vals["m_" + n][0], vals["v_" + n][0], "adamw_" + n)
        deltas[n], new_m[n], new_v[n] = d[None], m1[None], v1[None]

    gathered_small = _all_gather(_pack_small(gs), "ag_small_grads")
    outs = _small_sum_adamw(gathered_small, _pack_small(small),
                            _pack_small({n: vals["m_" + n] for n in _SMALL}),
                            _pack_small({n: vals["v_" + n] for n in _SMALL}))
    for dst, packed in zip((grads, deltas, new_m, new_v), outs):
        dst.update(_unpack_small(packed, small))

    return (loss, dx[None], *[grads[n] for n in _WEIGHTS], *[deltas[n] for n in _WEIGHTS],
            *[new_m[n] for n in _WEIGHTS], *[new_v[n] for n in _WEIGHTS])
```

```python
import functools

import jax
import jax.numpy as jnp
from jax import lax
from jax.experimental import pallas as pl
from jax.experimental.pallas import tpu as pltpu

F32 = jnp.float32
BF16 = jnp.bfloat16
EPS = 1e-6
MESH = pl.DeviceIdType.MESH
N_DEV = 8

SGU_GROUPS = 4
GROUP_DIM = 128
CHUNK = 128
SB_HEADS = 8
SB_HEAD_DIM = 64
Q_BLOCK = 128
XA_HEADS = 4
XA_HEAD_DIM = 256

ADAM_LR = 0.001
ADAM_B1 = 0.9
ADAM_B2 = 0.999
ADAM_EPS = 1e-08
ADAM_WD = 0.01
ADAM_STEP = 10

VMEM_LIMIT_V7X = 56 * 1024 * 1024
GELU_C0 = 0.7978845608028654
GELU_C1 = 0.044715


def _dot(a, b):
    return jnp.dot(a.astype(BF16), b.astype(BF16), preferred_element_type=F32)


def _dot_nt(a, b):
    return lax.dot_general(a.astype(BF16), b.astype(BF16), (((1,), (1,)), ((), ())),
                           preferred_element_type=F32)


def _dot_tn(a, b):
    return lax.dot_general(a.astype(BF16), b.astype(BF16), (((0,), (0,)), ((), ())),
                           preferred_element_type=F32)


def _rms(x, g):
    r = lax.rsqrt(jnp.mean(x * x, axis=-1, keepdims=True) + EPS)
    return x * r * g, r


def _rms_bwd(x, g, dy):
    r = lax.rsqrt(jnp.mean(x * x, axis=-1, keepdims=True) + EPS)
    xh = x * r
    gy = dy * g
    dx = r * (gy - xh * jnp.mean(gy * xh, axis=-1, keepdims=True))
    dg = jnp.sum(dy * xh, axis=0, keepdims=True)
    return dx, dg


def _sigmoid(x):
    return 1.0 / (1.0 + jnp.exp(-x))


def _gelu(x):
    t = jnp.tanh(GELU_C0 * (x + GELU_C1 * x * x * x))
    return 0.5 * x * (1.0 + t)


def _gelu_grad(x):
    t = jnp.tanh(GELU_C0 * (x + GELU_C1 * x * x * x))
    return 0.5 * (1.0 + t) + 0.5 * x * (1.0 - t * t) * GELU_C0 * (1.0 + 3.0 * GELU_C1 * x * x)


def _split_bf16(x):
    hi = x.astype(BF16)
    lo = (x - hi.astype(F32)).astype(BF16)
    return hi, lo


def _row_spec(tm, cols):
    return pl.BlockSpec((tm, cols), lambda i: (i, 0))


def _full_spec(shape):
    nd = len(shape)
    return pl.BlockSpec(tuple(shape), lambda i: (0,) * nd)


def _token_tile(s):
    return min(256, s)


def _row_call(body, name, s, tiled_in, full_in, tiled_out, acc_out):
    tm = _token_tile(s)
    in_specs = [_row_spec(tm, a.shape[1]) for a in tiled_in] + [_full_spec(a.shape) for a in full_in]
    out_specs = [_row_spec(tm, c) for c, _ in tiled_out] + [_full_spec(sh) for sh, _ in acc_out]
    out_shape = [jax.ShapeDtypeStruct((s, c), dt) for c, dt in tiled_out]
    out_shape += [jax.ShapeDtypeStruct(sh, dt) for sh, dt in acc_out]
    return pl.pallas_call(
        functools.partial(body),
        name=name,
        grid=(s // tm,),
        in_specs=in_specs,
        out_specs=out_specs,
        out_shape=out_shape,
        compiler_params=pltpu.CompilerParams(
            dimension_semantics=("arbitrary",), vmem_limit_bytes=VMEM_LIMIT_V7X),
    )(*tiled_in, *full_in)


def _acc(ref, val):
    @pl.when(pl.program_id(0) == 0)
    def _():
        ref[...] = val

    @pl.when(pl.program_id(0) != 0)
    def _():
        ref[...] += val


def _ffn_fwd_body(x_ref, pre_ref, post_ref, wgt_ref, wut_ref, wd_ref,
                  h_ref, n_ref, a_ref, b_ref, f_ref):
    x = x_ref[...]
    n, _ = _rms(x, pre_ref[...])
    nb = n.astype(BF16)
    n_ref[...] = nb
    a = _dot_nt(nb, wgt_ref[...])
    b = _dot_nt(nb, wut_ref[...])
    a_ref[...] = a.astype(BF16)
    b_ref[...] = b.astype(BF16)
    hmid = a * _sigmoid(a) * b
    f = jnp.dot(hmid.astype(BF16), wd_ref[...], preferred_element_type=F32)
    f_ref[...] = f
    y, _ = _rms(f, post_ref[...])
    h_ref[...] = x + 0.5 * y


def _ffn_fwd(x, pre_g, post_g, wgt, wut, wd, name):
    s, d = x.shape
    f = wgt.shape[0]
    return _row_call(_ffn_fwd_body, name, s, [x], [pre_g, post_g, wgt, wut, wd],
                     [(d, F32), (d, BF16), (f, BF16), (f, BF16), (d, F32)], [])


def _ffn_bwd_body(dh_ref, x_ref, a_ref, b_ref, f_ref, pre_ref, post_ref, wgt_ref, wut_ref, wd_ref,
                  dx_ref, da_ref, db_ref, hm_ref, df_ref, dpre_ref, dpost_ref):
    dh = dh_ref[...]
    df, dpost = _rms_bwd(f_ref[...], post_ref[...], 0.5 * dh)
    dfb = df.astype(BF16)
    df_ref[...] = dfb
    dhmid = _dot_nt(dfb, wd_ref[...])
    a = a_ref[...].astype(F32)
    b = b_ref[...].astype(F32)
    sig = _sigmoid(a)
    sa = a * sig
    hm_ref[...] = (sa * b).astype(BF16)
    dab = (dhmid * b * sig * (1.0 + a * (1.0 - sig))).astype(BF16)
    dbb = (dhmid * sa).astype(BF16)
    da_ref[...] = dab
    db_ref[...] = dbb
    dn = _dot(dab, wgt_ref[...]) + _dot(dbb, wut_ref[...])
    dxn, dpre = _rms_bwd(x_ref[...], pre_ref[...], dn)
    dx_ref[...] = dh + dxn
    _acc(dpre_ref, dpre)
    _acc(dpost_ref, dpost)


def _ffn_bwd(dh, x, a, b, f, pre_g, post_g, wgt, wut, wd, name):
    s, d = x.shape
    ff = wgt.shape[0]
    return _row_call(_ffn_bwd_body, name, s, [dh, x, a, b, f], [pre_g, post_g, wgt, wut, wd],
                     [(d, F32), (ff, BF16), (ff, BF16), (ff, BF16), (d, BF16)],
                     [((1, d), F32), ((1, d), F32)])


def _inproj_fwd_body(h_ref, g_ref, wt_ref, n_ref, uv_ref, qkv_ref):
    n, _ = _rms(h_ref[...], g_ref[...])
    nb = n.astype(BF16)
    n_ref[...] = nb
    proj = _dot_nt(nb, wt_ref[...])
    nuv = uv_ref.shape[1]
    uv_ref[...] = proj[:, :nuv]
    qkv_ref[...] = proj[:, nuv:].astype(BF16)


def _inproj_fwd(h, g, w_in_t):
    s, d = h.shape
    sgu_w = SGU_GROUPS * GROUP_DIM
    sb_w = SB_HEADS * SB_HEAD_DIM
    return _row_call(_inproj_fwd_body, "inproj_fwd", s, [h], [g, w_in_t],
                     [(d, BF16), (2 * sgu_w, F32), (3 * sb_w, BF16)], [])


def _inproj_bwd_body(dh_ref, dproj_ref, h_ref, g_ref, wt_ref, dhout_ref, dg_ref):
    dn = _dot(dproj_ref[...], wt_ref[...])
    dhn, dg = _rms_bwd(h_ref[...], g_ref[...], dn)
    dhout_ref[...] = dh_ref[...] + dhn
    _acc(dg_ref, dg)


def _inproj_bwd(dh, dproj, h, g, w_in_t):
    s, d = h.shape
    return _row_call(_inproj_bwd_body, "inproj_bwd", s, [dh, dproj, h], [g, w_in_t],
                     [(d, F32)], [((1, d), F32)])


def _causal_w(ws_ref, g):
    row = lax.broadcasted_iota(jnp.int32, (CHUNK, CHUNK), 0)
    col = lax.broadcasted_iota(jnp.int32, (CHUNK, CHUNK), 1)
    return jnp.where(row >= col, ws_ref[g], 0.0), row >= col


def _group_norm(v):
    mu = jnp.mean(v, axis=-1, keepdims=True)
    d = v - mu
    rstd = lax.rsqrt(jnp.mean(d * d, axis=-1, keepdims=True) + EPS)
    return d * rstd, rstd


def _sgu_fwd_body(uv_ref, ng_ref, nb_ref, ws_ref, bs_ref, out_ref):
    width = SGU_GROUPS * GROUP_DIM
    for c in range(uv_ref.shape[0] // CHUNK):
        rows = pl.ds(c * CHUNK, CHUNK)
        for g in range(SGU_GROUPS):
            lanes = pl.ds(g * GROUP_DIM, GROUP_DIM)
            u = _gelu(uv_ref[rows, lanes])
            v = _gelu(uv_ref[rows, pl.ds(width + g * GROUP_DIM, GROUP_DIM)])
            vhat, _ = _group_norm(v)
            vn = vhat * ng_ref[:, lanes] + nb_ref[:, lanes]
            w, _ = _causal_w(ws_ref, g)
            mixed = _dot(w, vn) + bs_ref[g]
            out_ref[rows, lanes] = u * mixed


def _sgu_fwd(uv_pre, ng, nb, ws, bs):
    s = uv_pre.shape[0]
    return _row_call(_sgu_fwd_body, "sgu_fwd", s, [uv_pre], [ng, nb, ws, bs],
                     [(SGU_GROUPS * GROUP_DIM, F32)], [])[0]


def _sgu_bwd_body(uv_ref, do_ref, ng_ref, nb_ref, ws_ref, bs_ref,
                  duv_ref, dws_ref, dbs_ref, dng_ref, dnb_ref):
    width = SGU_GROUPS * GROUP_DIM

    @pl.when(pl.program_id(0) == 0)
    def _():
        dws_ref[...] = jnp.zeros_like(dws_ref)
        dbs_ref[...] = jnp.zeros_like(dbs_ref)
        dng_ref[...] = jnp.zeros_like(dng_ref)
        dnb_ref[...] = jnp.zeros_like(dnb_ref)

    for c in range(uv_ref.shape[0] // CHUNK):
        rows = pl.ds(c * CHUNK, CHUNK)
        for g in range(SGU_GROUPS):
            lanes = pl.ds(g * GROUP_DIM, GROUP_DIM)
            vlanes = pl.ds(width + g * GROUP_DIM, GROUP_DIM)
            u_pre = uv_ref[rows, lanes]
            v_pre = uv_ref[rows, vlanes]
            u = _gelu(u_pre)
            v = _gelu(v_pre)
            vhat, rstd = _group_norm(v)
            gain = ng_ref[:, lanes]
            vn = vhat * gain + nb_ref[:, lanes]
            w, causal = _causal_w(ws_ref, g)
            mixed = _dot(w, vn) + bs_ref[g]
            dout = do_ref[rows, lanes]
            du = dout * mixed
            dmixed = dout * u
            dbs_ref[g] += jnp.sum(dmixed, axis=1, keepdims=True)
            dws_ref[g] += jnp.where(causal, _dot_nt(dmixed, vn), 0.0)
            dvn = _dot_tn(w, dmixed)
            dng_ref[:, lanes] += jnp.sum(dvn * vhat, axis=0, keepdims=True)
            dnb_ref[:, lanes] += jnp.sum(dvn, axis=0, keepdims=True)
            dvh = dvn * gain
            dv = rstd * (dvh - jnp.mean(dvh, axis=-1, keepdims=True)
                         - vhat * jnp.mean(dvh * vhat, axis=-1, keepdims=True))
            duv_ref[rows, lanes] = (du * _gelu_grad(u_pre)).astype(BF16)
            duv_ref[rows, vlanes] = (dv * _gelu_grad(v_pre)).astype(BF16)


def _sgu_bwd(uv_pre, dout_a, ng, nb, ws, bs):
    s = uv_pre.shape[0]
    width = SGU_GROUPS * GROUP_DIM
    return _row_call(_sgu_bwd_body, "sgu_bwd", s, [uv_pre, dout_a], [ng, nb, ws, bs],
                     [(2 * width, BF16)],
                     [(ws.shape, F32), (bs.shape, F32), ((1, width), F32), ((1, width), F32)])


SB_DEAD = -105.0
HEADS_PER_TILE = 2


def _sb_logs(q, k, diagonal):
    row, col = _key_order()
    strict = jnp.logical_or(jnp.logical_not(diagonal), col < row)
    z = _dot_nt(q, k) * (SB_HEAD_DIM ** -0.5)
    sp = jnp.log1p(jnp.exp(-jnp.abs(z)))
    log_beta = jnp.minimum(z, 0.0) - sp
    log_1m_raw = -jnp.maximum(z, 0.0) - sp
    return strict, log_beta, log_1m_raw, jnp.where(strict, log_1m_raw, 0.0)


def _key_sums(x, pick):
    hi, lo = _split_bf16(x)
    return jnp.dot(hi, pick, preferred_element_type=F32) + jnp.dot(lo, pick, preferred_element_type=F32)


def _key_order():
    row = lax.broadcasted_iota(jnp.int32, (Q_BLOCK, Q_BLOCK), 0)
    col = lax.broadcasted_iota(jnp.int32, (Q_BLOCK, Q_BLOCK), 1)
    return row, col


def _head_halves(x):
    first = lax.broadcasted_iota(jnp.int32, x.shape, 1) < SB_HEAD_DIM
    zero = jnp.zeros_like(x)
    return jnp.where(first, x, zero), jnp.where(first, zero, x)


def _pick_halves(xa, xb):
    first = lax.broadcasted_iota(jnp.int32, xa.shape, 1) < SB_HEAD_DIM
    return jnp.where(first, xa, xb)


def _sb_fwd_body(q_ref, k_ref, v_ref, o_ref, tot_ref, cnt_ref):
    pair, qb = pl.program_id(0), pl.program_id(1)
    qs = _head_halves(q_ref[...])
    row, col = _key_order()
    later = (row > col).astype(BF16)

    def alive(carry):
        i, cs, _ = carry
        return jnp.logical_and(i <= qb, jnp.max(jnp.maximum(cs[0], cs[1])) > SB_DEAD)

    def step(carry):
        i, cs, accs = carry
        rows = pl.ds(pl.multiple_of((qb - i) * Q_BLOCK, Q_BLOCK), Q_BLOCK)
        k = k_ref[rows, :]
        v = v_ref[rows, :]
        new_cs, new_accs = [], []
        for q, c, acc in zip(qs, cs, accs):
            strict, log_beta, _, log_1m = _sb_logs(q, k, i == 0)
            a = jnp.where(strict, jnp.exp(log_beta + _key_sums(log_1m, later) + c), 0.0)
            new_accs.append(acc + jnp.dot(a.astype(BF16), v, preferred_element_type=F32))
            new_cs.append(c + jnp.sum(log_1m, axis=1, keepdims=True))
        return i + 1, tuple(new_cs), tuple(new_accs)

    zc = jnp.zeros((Q_BLOCK, 1), F32)
    za = jnp.zeros((Q_BLOCK, HEADS_PER_TILE * SB_HEAD_DIM), F32)
    n, cs, accs = lax.while_loop(alive, step, (jnp.int32(0), (zc, zc), (za, za)))
    o_ref[...] = _pick_halves(*accs)
    tot_ref[0] = cs[0]
    tot_ref[1] = cs[1]
    cnt_ref[pair, qb] = n


def _sb_fwd(qkv):
    s = qkv.shape[0]
    lanes = HEADS_PER_TILE * SB_HEAD_DIM
    tiles = SB_HEADS // HEADS_PER_TILE
    nq = s // Q_BLOCK
    return pl.pallas_call(
        functools.partial(_sb_fwd_body),
        name="sb_fwd",
        grid=(tiles, nq),
        in_specs=[pl.BlockSpec((Q_BLOCK, lanes), lambda p, i: (i, p)),
                  pl.BlockSpec((s, lanes), lambda p, i: (0, tiles + p)),
                  pl.BlockSpec((s, lanes), lambda p, i: (0, 2 * tiles + p))],
        out_specs=[pl.BlockSpec((Q_BLOCK, lanes), lambda p, i: (i, p)),
                   pl.BlockSpec((HEADS_PER_TILE, Q_BLOCK, 1), lambda p, i: (p, i, 0)),
                   pl.BlockSpec(memory_space=pltpu.SMEM)],
        out_shape=[jax.ShapeDtypeStruct((s, SB_HEADS * SB_HEAD_DIM), F32),
                   jax.ShapeDtypeStruct((SB_HEADS, s, 1), F32),
                   jax.ShapeDtypeStruct((tiles, nq), jnp.int32)],
        compiler_params=pltpu.CompilerParams(
            dimension_semantics=("arbitrary", "arbitrary"), vmem_limit_bytes=VMEM_LIMIT_V7X),
    )(qkv, qkv, qkv)


def _sb_bwd_body(cnt_ref, q_ref, k_ref, v_ref, tot_ref, do_ref, dq_ref, dk_ref, dv_ref):
    pair, qb = pl.program_id(0), pl.program_id(1)

    @pl.when(qb == 0)
    def _():
        dk_ref[...] = jnp.zeros_like(dk_ref)
        dv_ref[...] = jnp.zeros_like(dv_ref)

    qs = _head_halves(q_ref[...])
    dos = _head_halves(do_ref[...].astype(BF16))
    tots = (tot_ref[0], tot_ref[1])
    row, col = _key_order()
    up_to = (row <= col).astype(BF16)
    earlier = (row < col).astype(BF16)
    scale = SB_HEAD_DIM ** -0.5
    n = cnt_ref[pair, qb]
    start = qb - n + 1

    def step(i, carry):
        cs, ces, dqs = carry
        kb = start + i
        rows = pl.ds(pl.multiple_of(kb * Q_BLOCK, Q_BLOCK), Q_BLOCK)
        k = k_ref[rows, :]
        v = v_ref[rows, :]
        new_cs, new_ces, new_dqs = [], [], []
        dk = jnp.zeros((Q_BLOCK, HEADS_PER_TILE * SB_HEAD_DIM), F32)
        dv = jnp.zeros((Q_BLOCK, HEADS_PER_TILE * SB_HEAD_DIM), F32)
        for q, dob, tot, c, ce, dq in zip(qs, dos, tots, cs, ces, dqs):
            strict, log_beta, log_1m_raw, log_1m = _sb_logs(q, k, kb == qb)
            suffix = tot - c - _key_sums(log_1m, up_to)
            a = jnp.where(strict, jnp.exp(log_beta + suffix), 0.0)
            de = _dot_nt(dob, v) * a
            before = ce + _key_sums(de, earlier)
            dz = jnp.where(strict, de * jnp.exp(log_1m_raw) - before * jnp.exp(log_beta), 0.0)
            dzb = dz.astype(BF16)
            new_dqs.append(dq + jnp.dot(dzb, k, preferred_element_type=F32))
            dk = dk + _dot_tn(dzb, q)
            dv = dv + _dot_tn(a, dob)
            new_cs.append(c + jnp.sum(log_1m, axis=1, keepdims=True))
            new_ces.append(ce + jnp.sum(de, axis=1, keepdims=True))
        dk_ref[rows, :] += dk * scale
        dv_ref[rows, :] += dv
        return tuple(new_cs), tuple(new_ces), tuple(new_dqs)

    zc = jnp.zeros((Q_BLOCK, 1), F32)
    za = jnp.zeros((Q_BLOCK, HEADS_PER_TILE * SB_HEAD_DIM), F32)
    _, _, dqs = lax.fori_loop(0, n, step, ((zc, zc), (zc, zc), (za, za)))
    dq_ref[...] = _pick_halves(*dqs) * scale


def _sb_bwd(cnt, qkv, tot, dout_b):
    s = qkv.shape[0]
    lanes = HEADS_PER_TILE * SB_HEAD_DIM
    tiles = SB_HEADS // HEADS_PER_TILE
    return pl.pallas_call(
        functools.partial(_sb_bwd_body),
        name="sb_bwd",
        grid_spec=pltpu.PrefetchScalarGridSpec(
            num_scalar_prefetch=1,
            grid=(tiles, s // Q_BLOCK),
            in_specs=[pl.BlockSpec((Q_BLOCK, lanes), lambda p, i, cnt: (i, p)),
                      pl.BlockSpec((s, lanes), lambda p, i, cnt: (0, tiles + p)),
                      pl.BlockSpec((s, lanes), lambda p, i, cnt: (0, 2 * tiles + p)),
                      pl.BlockSpec((HEADS_PER_TILE, Q_BLOCK, 1), lambda p, i, cnt: (p, i, 0)),
                      pl.BlockSpec((Q_BLOCK, lanes), lambda p, i, cnt: (i, p))],
            out_specs=[pl.BlockSpec((Q_BLOCK, lanes), lambda p, i, cnt: (i, p)),
                       pl.BlockSpec((s, lanes), lambda p, i, cnt: (0, p)),
                       pl.BlockSpec((s, lanes), lambda p, i, cnt: (0, p))],
        ),
        out_shape=[jax.ShapeDtypeStruct((s, SB_HEADS * SB_HEAD_DIM), F32)] * 3,
        compiler_params=pltpu.CompilerParams(
            dimension_semantics=("arbitrary", "arbitrary"), vmem_limit_bytes=VMEM_LIMIT_V7X),
    )(cnt, qkv, qkv, qkv, tot, dout_b)


def _outproj_fwd_body(oa_ref, ob_ref, h_ref, ga_ref, gb_ref, gpost_ref, w_ref,
                      merged_ref, mo_ref, hout_ref):
    half = oa_ref.shape[1]
    ma, _ = _rms(oa_ref[...], ga_ref[...])
    mb, _ = _rms(ob_ref[...], gb_ref[...])
    mab = ma.astype(BF16)
    mbb = mb.astype(BF16)
    merged_ref[:, :half] = mab
    merged_ref[:, half:] = mbb
    mo = (jnp.dot(mab, w_ref[:half, :], preferred_element_type=F32)
          + jnp.dot(mbb, w_ref[half:, :], preferred_element_type=F32))
    mo_ref[...] = mo
    y, _ = _rms(mo, gpost_ref[...])
    hout_ref[...] = h_ref[...] + y


def _outproj_fwd(out_a, out_b, h, ga, gb, gpost, w_out):
    s, d = h.shape
    return _row_call(_outproj_fwd_body, "outproj_fwd", s, [out_a, out_b, h], [ga, gb, gpost, w_out],
                     [(d, BF16), (d, F32), (d, F32)], [])


def _outproj_bwd_body(dh_ref, mo_ref, oa_ref, ob_ref, ga_ref, gb_ref, gpost_ref, w_ref,
                      dmo_ref, doa_ref, dob_ref, dga_ref, dgb_ref, dgpost_ref):
    half = oa_ref.shape[1]
    dmo, dgpost = _rms_bwd(mo_ref[...], gpost_ref[...], dh_ref[...])
    dmob = dmo.astype(BF16)
    dmo_ref[...] = dmob
    dma = _dot_nt(dmob, w_ref[:half, :])
    dmb = _dot_nt(dmob, w_ref[half:, :])
    doa, dga = _rms_bwd(oa_ref[...], ga_ref[...], dma)
    dob, dgb = _rms_bwd(ob_ref[...], gb_ref[...], dmb)
    doa_ref[...] = doa
    dob_ref[...] = dob
    _acc(dga_ref, dga)
    _acc(dgb_ref, dgb)
    _acc(dgpost_ref, dgpost)


def _outproj_bwd(dh, mo, out_a, out_b, ga, gb, gpost, w_out):
    s, d = dh.shape
    half = out_a.shape[1]
    return _row_call(_outproj_bwd_body, "outproj_bwd", s, [dh, mo, out_a, out_b], [ga, gb, gpost, w_out],
                     [(d, BF16), (half, F32), (half, F32)],
                     [((1, half), F32), ((1, half), F32), ((1, d), F32)])


def _kv_fwd_body(mem_ref, g_ref, wt_ref, memn_ref, kv_ref):
    n, _ = _rms(mem_ref[...], g_ref[...])
    nb = n.astype(BF16)
    memn_ref[...] = nb
    kv_ref[...] = _dot_nt(nb, wt_ref[...]).astype(BF16)


def _kv_fwd(mem, g, w_kv_t):
    m, d = mem.shape
    return _row_call(_kv_fwd_body, "kv_fwd", m, [mem], [g, w_kv_t], [(d, BF16), (w_kv_t.shape[0], BF16)], [])


def _kv_bwd_body(dkv_ref, mem_ref, memn_ref, g_ref, wt_ref, dwt_ref, dg_ref):
    dkvb = dkv_ref[...].astype(BF16)
    dwt_ref[...] = _dot_tn(dkvb, memn_ref[...])
    dmemn = _dot(dkvb, wt_ref[...])
    _, dg = _rms_bwd(mem_ref[...], g_ref[...], dmemn)
    dg_ref[...] = dg


def _kv_bwd(dkv, mem, memn, g, w_kv_t):
    m, d = mem.shape
    return pl.pallas_call(
        functools.partial(_kv_bwd_body),
        name="kv_bwd",
        out_shape=[jax.ShapeDtypeStruct(w_kv_t.shape, F32), jax.ShapeDtypeStruct((1, d), F32)],
        compiler_params=pltpu.CompilerParams(vmem_limit_bytes=VMEM_LIMIT_V7X),
    )(dkv, mem, memn, g, w_kv_t)


def _xa_fwd_body(h_ref, gpre_ref, gpost_ref, wq_ref, wo_ref, kv_ref,
                 n_ref, q_ref, o_ref, c_ref, hout_ref):
    h = h_ref[...]
    d = h.shape[1]
    n, _ = _rms(h, gpre_ref[...])
    nb = n.astype(BF16)
    n_ref[...] = nb
    qb = jnp.dot(nb, wq_ref[...], preferred_element_type=F32).astype(BF16)
    q_ref[...] = qb
    for hd in range(XA_HEADS):
        lanes = slice(hd * XA_HEAD_DIM, (hd + 1) * XA_HEAD_DIM)
        k = kv_ref[:, lanes]
        v = kv_ref[:, d + hd * XA_HEAD_DIM:d + (hd + 1) * XA_HEAD_DIM]
        logits = _dot_nt(qb[:, lanes], k) * (XA_HEAD_DIM ** -0.5)
        e = jnp.exp(logits - jnp.max(logits, axis=-1, keepdims=True))
        p = e / jnp.sum(e, axis=-1, keepdims=True)
        o_ref[:, lanes] = jnp.dot(p.astype(BF16), v, preferred_element_type=F32).astype(BF16)
    c = jnp.dot(o_ref[...], wo_ref[...], preferred_element_type=F32)
    c_ref[...] = c
    y, _ = _rms(c, gpost_ref[...])
    hout_ref[...] = h + y


def _xa_fwd(h, gpre, gpost, wq, wo, kv):
    s, d = h.shape
    return _row_call(_xa_fwd_body, "xa_fwd", s, [h], [gpre, gpost, wq, wo, kv],
                     [(d, BF16), (d, BF16), (d, BF16), (d, F32), (d, F32)], [])


def _xa_bwd_body(dh_ref, h_ref, c_ref, q_ref, o_ref, gpre_ref, gpost_ref, wq_ref, wo_ref, kv_ref,
                 dhout_ref, dc_ref, dq_ref, dkv_ref, dgpre_ref, dgpost_ref):
    dh = dh_ref[...]
    d = dh.shape[1]
    scale = XA_HEAD_DIM ** -0.5
    dc, dgpost = _rms_bwd(c_ref[...], gpost_ref[...], dh)
    dcb = dc.astype(BF16)
    dc_ref[...] = dcb
    dob = _dot_nt(dcb, wo_ref[...]).astype(BF16)

    @pl.when(pl.program_id(0) == 0)
    def _():
        dkv_ref[...] = jnp.zeros_like(dkv_ref)

    for hd in range(XA_HEADS):
        lanes = slice(hd * XA_HEAD_DIM, (hd + 1) * XA_HEAD_DIM)
        vlanes = slice(d + hd * XA_HEAD_DIM, d + (hd + 1) * XA_HEAD_DIM)
        qh = q_ref[:, lanes]
        k = kv_ref[:, lanes]
        v = kv_ref[:, vlanes]
        logits = _dot_nt(qh, k) * scale
        e = jnp.exp(logits - jnp.max(logits, axis=-1, keepdims=True))
        p = e / jnp.sum(e, axis=-1, keepdims=True)
        doh = dob[:, lanes]
        dp = _dot_nt(doh, v)
        dl = (p * (dp - jnp.sum(dp * p, axis=-1, keepdims=True)) * scale).astype(BF16)
        dq_ref[:, lanes] = jnp.dot(dl, k, preferred_element_type=F32).astype(BF16)
        dkv_ref[:, lanes] += _dot_tn(dl, qh)
        dkv_ref[:, vlanes] += _dot_tn(p, doh)
    dn = _dot_nt(dq_ref[...], wq_ref[...])
    dhn, dgpre = _rms_bwd(h_ref[...], gpre_ref[...], dn)
    dhout_ref[...] = dh + dhn
    _acc(dgpre_ref, dgpre)
    _acc(dgpost_ref, dgpost)


def _xa_bwd(dh, h, c, q, o, gpre, gpost, wq, wo, kv):
    s, d = h.shape
    return _row_call(_xa_bwd_body, "xa_bwd", s, [dh, h, c, q, o], [gpre, gpost, wq, wo, kv],
                     [(d, F32), (d, BF16), (d, BF16)],
                     [(kv.shape, F32), ((1, d), F32), ((1, d), F32)])


def _final_body(h_ref, t_ref, g_ref, dh_ref, loss_ref, dg_ref):
    h = h_ref[...]
    d = h.shape[1]
    y, _ = _rms(h, g_ref[...])
    err = y - t_ref[...]
    part = (0.5 / d) * jnp.sum(jnp.sum(err * err, axis=1, keepdims=True), axis=0, keepdims=True)
    dh, dg = _rms_bwd(h, g_ref[...], err * (1.0 / d))
    dh_ref[...] = dh
    _acc(loss_ref, part)
    _acc(dg_ref, dg)


def _final(h, g, target):
    s, d = h.shape
    return _row_call(_final_body, "final_loss", s, [h, target], [g],
                     [(d, F32)], [((1, 1), F32), ((1, d), F32)])


def _largest_tile(n, cap):
    best = 128
    for t in range(128, cap + 1, 128):
        if n % t == 0:
            best = t
    return best


def _mm_tn(a, bs, name):
    s, k = a.shape
    n = bs[0].shape[1]
    nb = len(bs)
    ts = min(512, s)
    tk = _largest_tile(k, 1536)
    tn = _largest_tile(n, 1536 // nb)

    def body(a_ref, *refs):
        b_refs, o_refs = refs[:nb], refs[nb:]
        at = a_ref[...]

        @pl.when(pl.program_id(2) == 0)
        def _():
            for o_ref in o_refs:
                o_ref[...] = jnp.zeros_like(o_ref)

        for b_ref, o_ref in zip(b_refs, o_refs):
            o_ref[...] += _dot_tn(at, b_ref[...])

    return pl.pallas_call(
        body,
        name=name,
        grid=(k // tk, n // tn, s // ts),
        in_specs=[pl.BlockSpec((ts, tk), lambda i, j, t: (t, i))]
        + [pl.BlockSpec((ts, tn), lambda i, j, t: (t, j))] * nb,
        out_specs=[pl.BlockSpec((tk, tn), lambda i, j, t: (i, j))] * nb,
        out_shape=[jax.ShapeDtypeStruct((k, n), F32)] * nb,
        compiler_params=pltpu.CompilerParams(
            dimension_semantics=("arbitrary", "arbitrary", "arbitrary"),
            vmem_limit_bytes=VMEM_LIMIT_V7X),
    )(a, *bs)


_SMALL_SHAPES = {
    "sgu_norm_g": (1, SGU_GROUPS * GROUP_DIM),
    "sgu_norm_b": (1, SGU_GROUPS * GROUP_DIM),
    "sgu_w_s": (SGU_GROUPS, CHUNK, CHUNK),
    "sgu_b_s": (SGU_GROUPS, CHUNK, 1),
}


def _small_views(small):
    return {n: v.reshape(_SMALL_SHAPES.get(n, v.shape)) for n, v in small.items()}


def _small_unviews(views, like):
    return {n: v.reshape(like[n].shape) for n, v in views.items()}


def _local_step(x, mem, target, small, big):
    sm, w = small, big
    h1, n1, a1, b1, f1 = _ffn_fwd(x, sm["ffn1_pre_g"], sm["ffn1_post_g"],
                                  w["ffn1_w_gate"], w["ffn1_w_up"], w["ffn1_w_down"], "ffn1_fwd")
    n2, uv_pre, qkv = _inproj_fwd(h1, sm["mix_pre_g"], w["w_in"])
    out_a = _sgu_fwd(uv_pre, sm["sgu_norm_g"], sm["sgu_norm_b"], sm["sgu_w_s"], sm["sgu_b_s"])
    out_b, tot, cnt = _sb_fwd(qkv)
    merged, mo, h2 = _outproj_fwd(out_a, out_b, h1, sm["sgu_out_g"], sm["sb_out_g"],
                                  sm["mix_post_g"], w["w_out"])
    memn, kv = _kv_fwd(mem, sm["mem_norm_g"], w["xa_w_kv"])
    n3, qx, ox, cx, h3 = _xa_fwd(h2, sm["xa_pre_g"], sm["xa_post_g"], w["xa_w_q"], w["xa_w_o"], kv)
    h4, n4, a2, b2, f2 = _ffn_fwd(h3, sm["ffn2_pre_g"], sm["ffn2_post_g"],
                                  w["ffn2_w_gate"], w["ffn2_w_up"], w["ffn2_w_down"], "ffn2_fwd")
    dh4, loss, dg_final = _final(h4, sm["final_norm_g"], target)

    gs, gw = {"final_norm_g": dg_final}, {}
    dh3, da2, db2, hm2, df2, gs["ffn2_pre_g"], gs["ffn2_post_g"] = _ffn_bwd(
        dh4, h3, a2, b2, f2, sm["ffn2_pre_g"], sm["ffn2_post_g"],
        w["ffn2_w_gate"], w["ffn2_w_up"], w["ffn2_w_down"], "ffn2_bwd")
    gw["ffn2_w_gate"], = _mm_tn(da2, [n4], "ffn2_dw_gate")
    gw["ffn2_w_up"], = _mm_tn(db2, [n4], "ffn2_dw_up")
    gw["ffn2_w_down"], = _mm_tn(hm2, [df2], "ffn2_dw_down")

    dh2, dc, dqx, dkv, gs["xa_pre_g"], gs["xa_post_g"] = _xa_bwd(
        dh3, h2, cx, qx, ox, sm["xa_pre_g"], sm["xa_post_g"], w["xa_w_q"], w["xa_w_o"], kv)
    gw["xa_w_o"], = _mm_tn(ox, [dc], "xa_dw_o")
    gw["xa_w_q"], = _mm_tn(n3, [dqx], "xa_dw_q")
    gw["xa_w_kv"], gs["mem_norm_g"] = _kv_bwd(dkv, mem, memn, sm["mem_norm_g"], w["xa_w_kv"])

    dmo, dout_a, dout_b, gs["sgu_out_g"], gs["sb_out_g"], gs["mix_post_g"] = _outproj_bwd(
        dh2, mo, out_a, out_b, sm["sgu_out_g"], sm["sb_out_g"], sm["mix_post_g"], w["w_out"])
    gw["w_out"], = _mm_tn(merged, [dmo], "mix_dw_out")
    dq, dk, dv = _sb_bwd(cnt, qkv, tot, dout_b)
    duv, gs["sgu_w_s"], gs["sgu_b_s"], gs["sgu_norm_g"], gs["sgu_norm_b"] = _sgu_bwd(
        uv_pre, dout_a, sm["sgu_norm_g"], sm["sgu_norm_b"], sm["sgu_w_s"], sm["sgu_b_s"])
    dproj = jnp.concatenate([duv] + [t.astype(BF16) for t in (dq, dk, dv)], axis=1)
    dh1, gs["mix_pre_g"] = _inproj_bwd(dh2, dproj, h1, sm["mix_pre_g"], w["w_in"])
    gw["w_in"], = _mm_tn(dproj, [n2], "mix_dw_in")

    dx, da1, db1, hm1, df1, gs["ffn1_pre_g"], gs["ffn1_post_g"] = _ffn_bwd(
        dh1, x, a1, b1, f1, sm["ffn1_pre_g"], sm["ffn1_post_g"],
        w["ffn1_w_gate"], w["ffn1_w_up"], w["ffn1_w_down"], "ffn1_bwd")
    gw["ffn1_w_gate"], = _mm_tn(da1, [n1], "ffn1_dw_gate")
    gw["ffn1_w_up"], = _mm_tn(db1, [n1], "ffn1_dw_up")
    gw["ffn1_w_down"], = _mm_tn(hm1, [df1], "ffn1_dw_down")
    return loss, dx, gs, gw


def _mesh_place():
    return lax.axis_index("x"), lax.axis_index("y"), lax.axis_index("c")


def _other_chips(mx, my):
    return [(1 - mx, my), (mx, 1 - my), (1 - mx, 1 - my)]


_ANY = pl.BlockSpec(memory_space=pl.ANY)


def _all_gather(x, name):
    r, c = x.shape

    def body(x_ref, out_ref, send_sems, recv_sems, local_sem):
        mx, my, mc = _mesh_place()
        me, sibling = (mx, my, mc), (mx, my, 1 - mc)
        chips = _other_chips(mx, my)

        def slot(px, py, pc):
            return out_ref.at[4 * px + 2 * py + pc]

        def copy(k, block, to, src=None):
            return pltpu.make_async_remote_copy(
                src_ref=slot(*block) if src is None else src, dst_ref=slot(*block),
                send_sem=send_sems.at[k], recv_sem=recv_sems.at[k],
                device_id=to, device_id_type=MESH)

        mine = pltpu.make_async_copy(x_ref, slot(*me), local_sem)
        mine.start()
        first = [copy(0, me, sibling, src=x_ref)]
        first += [copy(1 + j, me, (*chip, mc), src=x_ref) for j, chip in enumerate(chips)]
        for cp in first:
            cp.start()
        passed = [copy(4 + j, (*chip, mc), sibling) for j, chip in enumerate(chips)]
        for j, chip in enumerate(chips):
            copy(1 + j, (*chip, mc), me).wait_recv()
            passed[j].start()
        copy(0, sibling, me).wait_recv()
        for j, chip in enumerate(chips):
            copy(4 + j, (*chip, 1 - mc), me).wait_recv()
        for cp in first + passed:
            cp.wait_send()
        mine.wait()

    return pl.pallas_call(
        body,
        name=name,
        out_shape=jax.ShapeDtypeStruct((N_DEV, r, c), x.dtype),
        in_specs=[_ANY],
        out_specs=_ANY,
        scratch_shapes=[pltpu.SemaphoreType.DMA((7,)), pltpu.SemaphoreType.DMA((7,)),
                        pltpu.SemaphoreType.DMA(())],
    )(x)


def _pair_exchange(p4):
    nchip, _, r, c = p4.shape

    def body(p_ref, out_ref, send_sems, recv_sems):
        mx, my, mc = _mesh_place()
        copies = [pltpu.make_async_remote_copy(
            src_ref=p_ref.at[j, 1 - mc], dst_ref=out_ref.at[j],
            send_sem=send_sems.at[j], recv_sem=recv_sems.at[j],
            device_id=(mx, my, 1 - mc), device_id_type=MESH) for j in range(nchip)]
        for cp in copies:
            cp.start()
        for cp in copies:
            cp.wait()

    return pl.pallas_call(
        body,
        name="rs_pair_exchange",
        out_shape=jax.ShapeDtypeStruct((nchip, r, c), p4.dtype),
        in_specs=[_ANY],
        out_specs=_ANY,
        scratch_shapes=[pltpu.SemaphoreType.DMA((nchip,)), pltpu.SemaphoreType.DMA((nchip,))],
    )(p4)


def _chip_exchange(q):
    _, r, c = q.shape

    def body(q_ref, out_ref, send_sems, recv_sems):
        mx, my, mc = _mesh_place()
        copies = [pltpu.make_async_remote_copy(
            src_ref=q_ref.at[2 * cx + cy], dst_ref=out_ref.at[k],
            send_sem=send_sems.at[k], recv_sem=recv_sems.at[k],
            device_id=(cx, cy, mc), device_id_type=MESH)
            for k, (cx, cy) in enumerate(_other_chips(mx, my))]
        for cp in copies:
            cp.start()
        for cp in copies:
            cp.wait()

    return pl.pallas_call(
        body,
        name="rs_chip_exchange",
        out_shape=jax.ShapeDtypeStruct((3, r, c), q.dtype),
        in_specs=[_ANY],
        out_specs=_ANY,
        scratch_shapes=[pltpu.SemaphoreType.DMA((3,)), pltpu.SemaphoreType.DMA((3,))],
    )(q)


def _rs_row_tile(r):
    return _largest_tile(r, 512)


def _pair_sum(place, p4, recv_a):
    nchip, _, r, c = p4.shape
    tr = _rs_row_tile(r)

    def body(place_ref, p_ref, a_ref, q_ref):
        q_ref[0] = (p_ref[0, 0].astype(F32) + a_ref[0].astype(F32)).astype(BF16)

    return pl.pallas_call(
        body,
        name="rs_pair_sum",
        grid_spec=pltpu.PrefetchScalarGridSpec(
            num_scalar_prefetch=1,
            grid=(nchip, r // tr),
            in_specs=[pl.BlockSpec((1, 1, tr, c), lambda j, i, pref: (j, pref[0], i, 0)),
                      pl.BlockSpec((1, tr, c), lambda j, i, pref: (j, i, 0))],
            out_specs=pl.BlockSpec((1, tr, c), lambda j, i, pref: (j, i, 0)),
        ),
        out_shape=jax.ShapeDtypeStruct((nchip, r, c), BF16),
    )(place, p4, recv_a)


def _rs_final(place, p4, recv_a, recv_b):
    _, _, r, c = p4.shape
    tr = _rs_row_tile(r)

    def body(place_ref, p_ref, a_ref, b_ref, g_ref):
        g = p_ref[0, 0].astype(F32) + a_ref[0].astype(F32)
        for k in range(3):
            g = g + b_ref[k].astype(F32)
        g_ref[...] = g

    return pl.pallas_call(
        body,
        name="rs_final_sum",
        grid_spec=pltpu.PrefetchScalarGridSpec(
            num_scalar_prefetch=1,
            grid=(r // tr,),
            in_specs=[pl.BlockSpec((1, 1, tr, c), lambda i, pref: (pref[1], pref[0], i, 0)),
                      pl.BlockSpec((1, tr, c), lambda i, pref: (pref[1], i, 0)),
                      pl.BlockSpec((3, tr, c), lambda i, pref: (0, i, 0))],
            out_specs=pl.BlockSpec((tr, c), lambda i, pref: (i, 0)),
        ),
        out_shape=jax.ShapeDtypeStruct((r, c), F32),
    )(place, p4, recv_a, recv_b)


def _adamw_math(w, g, m, v):
    m = ADAM_B1 * m + (1.0 - ADAM_B1) * g
    v = ADAM_B2 * v + (1.0 - ADAM_B2) * (g * g)
    m_hat = m / (1.0 - ADAM_B1 ** ADAM_STEP)
    v_hat = v / (1.0 - ADAM_B2 ** ADAM_STEP)
    delta = -ADAM_LR * (m_hat / (jnp.sqrt(v_hat) + ADAM_EPS) + ADAM_WD * w)
    return delta, m, v


def _adamw(w, g, m, v, name):
    r, c = w.shape
    tr = r if r <= 512 else 256

    def body(w_ref, g_ref, m_ref, v_ref, d_ref, mo_ref, vo_ref):
        d_ref[...], mo_ref[...], vo_ref[...] = _adamw_math(w_ref[...], g_ref[...], m_ref[...], v_ref[...])

    spec = pl.BlockSpec((tr, c), lambda i: (i, 0))
    out = jax.ShapeDtypeStruct((r, c), F32)
    return pl.pallas_call(
        body, name=name, grid=(r // tr,), in_specs=[spec] * 4, out_specs=[spec] * 3,
        out_shape=[out] * 3,
    )(w, g, m, v)


def _small_sum_adamw(gathered, w, m, v):
    _, r, c = gathered.shape

    def body(ga_ref, w_ref, m_ref, v_ref, g_ref, d_ref, mo_ref, vo_ref):
        g = ga_ref[0]
        for k in range(1, N_DEV):
            g = g + ga_ref[k]
        g_ref[...] = g
        d_ref[...], mo_ref[...], vo_ref[...] = _adamw_math(w_ref[...], g, m_ref[...], v_ref[...])

    out = jax.ShapeDtypeStruct((r, c), F32)
    return pl.pallas_call(body, name="small_sum_adamw", out_shape=[out] * 4)(gathered, w, m, v)


_WEIGHTS = ["ffn1_pre_g", "ffn1_post_g", "ffn1_w_gate", "ffn1_w_up", "ffn1_w_down", "mix_pre_g",
            "mix_post_g", "w_in", "sgu_norm_g", "sgu_norm_b", "sgu_w_s", "sgu_b_s", "sgu_out_g",
            "sb_out_g", "w_out", "xa_pre_g", "xa_post_g", "mem_norm_g", "xa_w_q", "xa_w_kv", "xa_w_o",
            "ffn2_pre_g", "ffn2_post_g", "ffn2_w_gate", "ffn2_w_up", "ffn2_w_down", "final_norm_g"]
_BIG = ["ffn1_w_gate", "ffn1_w_up", "ffn1_w_down", "w_in", "w_out", "xa_w_q", "xa_w_kv", "xa_w_o",
        "ffn2_w_gate", "ffn2_w_up", "ffn2_w_down"]
_COL_SHARDED = ("ffn1_w_gate", "ffn1_w_up", "w_in", "xa_w_kv", "ffn2_w_gate", "ffn2_w_up")
_SMALL = [n for n in _WEIGHTS if n not in _BIG]
SMALL_LANES = 128
SMALL_ROW_ALIGN = 8


def _pack_small(tensors):
    parts = []
    for n in _SMALL:
        t = tensors[n].reshape(-1, SMALL_LANES)
        pad = (-t.shape[0]) % SMALL_ROW_ALIGN
        parts.append(jnp.pad(t, ((0, pad), (0, 0))) if pad else t)
    return jnp.concatenate(parts, axis=0)


def _unpack_small(packed, like):
    out, off = {}, 0
    for n in _SMALL:
        size = like[n].size
        rows = size // SMALL_LANES
        out[n] = packed[off:off + rows].reshape(like[n].shape)
        off += rows + (-rows) % SMALL_ROW_ALIGN
    return out


def kernel(x, mem, ffn1_pre_g, ffn1_post_g, ffn1_w_gate, ffn1_w_up, ffn1_w_down, mix_pre_g, mix_post_g, w_in, sgu_norm_g, sgu_norm_b, sgu_w_s, sgu_b_s, sgu_out_g, sb_out_g, w_out, xa_pre_g, xa_post_g, mem_norm_g, xa_w_q, xa_w_kv, xa_w_o, ffn2_pre_g, ffn2_post_g, ffn2_w_gate, ffn2_w_up, ffn2_w_down, final_norm_g, loss_target, m_ffn1_pre_g, m_ffn1_post_g, m_ffn1_w_gate, m_ffn1_w_up, m_ffn1_w_down, m_mix_pre_g, m_mix_post_g, m_w_in, m_sgu_norm_g, m_sgu_norm_b, m_sgu_w_s, m_sgu_b_s, m_sgu_out_g, m_sb_out_g, m_w_out, m_xa_pre_g, m_xa_post_g, m_mem_norm_g, m_xa_w_q, m_xa_w_kv, m_xa_w_o, m_ffn2_pre_g, m_ffn2_post_g, m_ffn2_w_gate, m_ffn2_w_up, m_ffn2_w_down, m_final_norm_g, v_ffn1_pre_g, v_ffn1_post_g, v_ffn1_w_gate, v_ffn1_w_up, v_ffn1_w_down, v_mix_pre_g, v_mix_post_g, v_w_in, v_sgu_norm_g, v_sgu_norm_b, v_sgu_w_s, v_sgu_b_s, v_sgu_out_g, v_sb_out_g, v_w_out, v_xa_pre_g, v_xa_post_g, v_mem_norm_g, v_xa_w_q, v_xa_w_kv, v_xa_w_o, v_ffn2_pre_g, v_ffn2_post_g, v_ffn2_w_gate, v_ffn2_w_up, v_ffn2_w_down, v_final_norm_g):
    vals = dict(locals())
    d_model = x.shape[-1]

    pieces = []
    for n in _BIG:
        shard = vals[n][0]
        pieces.append((shard.T if n in _COL_SHARDED else shard).astype(BF16))
    gathered = _all_gather(jnp.concatenate(pieces, axis=0), "ag_weights")
    big, off = {}, 0
    for n, piece in zip(_BIG, pieces):
        rows = piece.shape[0]
        big[n] = gathered[:, off:off + rows, :].reshape(N_DEV * rows, d_model)
        off += rows

    small = {n: vals[n] for n in _SMALL}
    loss_part, dx, gs, gw = _local_step(x[0], mem[0], loss_target[0], _small_views(small), big)
    loss = lax.psum(loss_part[0, 0], ("x", "y", "c"))

    parts = [gw[n].reshape(N_DEV, gw[n].shape[0] // N_DEV, d_model).astype(BF16) for n in _BIG]
    p4 = jnp.concatenate(parts, axis=1).reshape(N_DEV // 2, 2, off, d_model)
    mx, my, mc = _mesh_place()
    place = jnp.stack([mc, 2 * mx + my]).astype(jnp.int32)
    recv_a = _pair_exchange(p4)
    recv_b = _chip_exchange(_pair_sum(place, p4, recv_a))
    g_rows = _rs_final(place, p4, recv_a, recv_b)

    grads, deltas, new_m, new_v = {}, {}, {}, {}
    off = 0
    for n, piece in zip(_BIG, pieces):
        rows = piece.shape[0]
        g = g_rows[off:off + rows]
        off += rows
        g = g.T if n in _COL_SHARDED else g
        grads[n] = g[None]
        d, m1, v1 = _adamw(vals[n][0], g, vals["m_" + n][0], vals["v_" + n][0], "adamw_" + n)
        deltas[n], new_m[n], new_v[n] = d[None], m1[None], v1[None]

    gathered_small = _all_gather(_pack_small(gs), "ag_small_grads")
    outs = _small_sum_adamw(gathered_small, _pack_small(small),
                            _pack_small({n: vals["m_" + n] for n in _SMALL}),
                            _pack_small({n: vals["v_" + n] for n in _SMALL}))
    for dst, packed in zip((grads, deltas, new_m, new_v), outs):
        dst.update(_unpack_small(packed, small))

    return (loss, dx[None], *[grads[n] for n in _WEIGHTS], *[deltas[n] for n in _WEIGHTS],
            *[new_m[n] for n in _WEIGHTS], *[new_v[n] for n in _WEIGHTS])
```

```python
import functools

import jax
import jax.numpy as jnp
from jax import lax
from jax.experimental import pallas as pl
from jax.experimental.pallas import tpu as pltpu

F32 = jnp.float32
BF16 = jnp.bfloat16
EPS = 1e-6
MESH = pl.DeviceIdType.MESH
N_DEV = 8

SGU_GROUPS = 4
GROUP_DIM = 128
CHUNK = 128
SB_HEADS = 8
SB_HEAD_DIM = 64
Q_BLOCK = 128
XA_HEADS = 4
XA_HEAD_DIM = 256

ADAM_LR = 0.001
ADAM_B1 = 0.9
ADAM_B2 = 0.999
ADAM_EPS = 1e-08
ADAM_WD = 0.01
ADAM_STEP = 10

VMEM_LIMIT_V7X = 56 * 1024 * 1024
GELU_C0 = 0.7978845608028654
GELU_C1 = 0.044715


def _dot(a, b):
    return jnp.dot(a.astype(BF16), b.astype(BF16), preferred_element_type=F32)


def _dot_nt(a, b):
    return lax.dot_general(a.astype(BF16), b.astype(BF16), (((1,), (1,)), ((), ())),
                           preferred_element_type=F32)


def _dot_tn(a, b):
    return lax.dot_general(a.astype(BF16), b.astype(BF16), (((0,), (0,)), ((), ())),
                           preferred_element_type=F32)


def _rms(x, g):
    r = lax.rsqrt(jnp.mean(x * x, axis=-1, keepdims=True) + EPS)
    return x * r * g, r


def _rms_bwd(x, g, dy):
    r = lax.rsqrt(jnp.mean(x * x, axis=-1, keepdims=True) + EPS)
    xh = x * r
    gy = dy * g
    dx = r * (gy - xh * jnp.mean(gy * xh, axis=-1, keepdims=True))
    dg = jnp.sum(dy * xh, axis=0, keepdims=True)
    return dx, dg


def _sigmoid(x):
    return 1.0 / (1.0 + jnp.exp(-x))


def _gelu(x):
    t = jnp.tanh(GELU_C0 * (x + GELU_C1 * x * x * x))
    return 0.5 * x * (1.0 + t)


def _gelu_grad(x):
    t = jnp.tanh(GELU_C0 * (x + GELU_C1 * x * x * x))
    return 0.5 * (1.0 + t) + 0.5 * x * (1.0 - t * t) * GELU_C0 * (1.0 + 3.0 * GELU_C1 * x * x)


def _split_bf16(x):
    hi = x.astype(BF16)
    lo = (x - hi.astype(F32)).astype(BF16)
    return hi, lo


def _row_spec(tm, cols):
    return pl.BlockSpec((tm, cols), lambda i: (i, 0))


def _full_spec(shape):
    nd = len(shape)
    return pl.BlockSpec(tuple(shape), lambda i: (0,) * nd)


def _token_tile(s):
    return min(256, s)


def _row_call(body, name, s, tiled_in, full_in, tiled_out, acc_out):
    tm = _token_tile(s)
    in_specs = [_row_spec(tm, a.shape[1]) for a in tiled_in] + [_full_spec(a.shape) for a in full_in]
    out_specs = [_row_spec(tm, c) for c, _ in tiled_out] + [_full_spec(sh) for sh, _ in acc_out]
    out_shape = [jax.ShapeDtypeStruct((s, c), dt) for c, dt in tiled_out]
    out_shape += [jax.ShapeDtypeStruct(sh, dt) for sh, dt in acc_out]
    return pl.pallas_call(
        functools.partial(body),
        name=name,
        grid=(s // tm,),
        in_specs=in_specs,
        out_specs=out_specs,
        out_shape=out_shape,
        compiler_params=pltpu.CompilerParams(
            dimension_semantics=("arbitrary",), vmem_limit_bytes=VMEM_LIMIT_V7X),
    )(*tiled_in, *full_in)


def _acc(ref, val):
    @pl.when(pl.program_id(0) == 0)
    def _():
        ref[...] = val

    @pl.when(pl.program_id(0) != 0)
    def _():
        ref[...] += val


def _ffn_fwd_body(x_ref, pre_ref, post_ref, wgt_ref, wut_ref, wd_ref,
                  h_ref, n_ref, a_ref, b_ref, f_ref):
    x = x_ref[...]
    n, _ = _rms(x, pre_ref[...])
    nb = n.astype(BF16)
    n_ref[...] = nb
    a = _dot_nt(nb, wgt_ref[...])
    b = _dot_nt(nb, wut_ref[...])
    a_ref[...] = a.astype(BF16)
    b_ref[...] = b.astype(BF16)
    hmid = a * _sigmoid(a) * b
    f = jnp.dot(hmid.astype(BF16), wd_ref[...], preferred_element_type=F32)
    f_ref[...] = f
    y, _ = _rms(f, post_ref[...])
    h_ref[...] = x + 0.5 * y


def _ffn_fwd(x, pre_g, post_g, wgt, wut, wd, name):
    s, d = x.shape
    f = wgt.shape[0]
    return _row_call(_ffn_fwd_body, name, s, [x], [pre_g, post_g, wgt, wut, wd],
                     [(d, F32), (d, BF16), (f, BF16), (f, BF16), (d, F32)], [])


def _ffn_bwd_body(dh_ref, x_ref, a_ref, b_ref, f_ref, pre_ref, post_ref, wgt_ref, wut_ref, wd_ref,
                  dx_ref, da_ref, db_ref, hm_ref, df_ref, dpre_ref, dpost_ref):
    dh = dh_ref[...]
    df, dpost = _rms_bwd(f_ref[...], post_ref[...], 0.5 * dh)
    dfb = df.astype(BF16)
    df_ref[...] = dfb
    dhmid = _dot_nt(dfb, wd_ref[...])
    a = a_ref[...].astype(F32)
    b = b_ref[...].astype(F32)
    sig = _sigmoid(a)
    sa = a * sig
    hm_ref[...] = (sa * b).astype(BF16)
    dab = (dhmid * b * sig * (1.0 + a * (1.0 - sig))).astype(BF16)
    dbb = (dhmid * sa).astype(BF16)
    da_ref[...] = dab
    db_ref[...] = dbb
    dn = _dot(dab, wgt_ref[...]) + _dot(dbb, wut_ref[...])
    dxn, dpre = _rms_bwd(x_ref[...], pre_ref[...], dn)
    dx_ref[...] = dh + dxn
    _acc(dpre_ref, dpre)
    _acc(dpost_ref, dpost)


def _ffn_bwd(dh, x, a, b, f, pre_g, post_g, wgt, wut, wd, name):
    s, d = x.shape
    ff = wgt.shape[0]
    return _row_call(_ffn_bwd_body, name, s, [dh, x, a, b, f], [pre_g, post_g, wgt, wut, wd],
                     [(d, F32), (ff, BF16), (ff, BF16), (ff, BF16), (d, BF16)],
                     [((1, d), F32), ((1, d), F32)])


def _inproj_fwd_body(h_ref, g_ref, wt_ref, n_ref, uv_ref, qkv_ref):
    n, _ = _rms(h_ref[...], g_ref[...])
    nb = n.astype(BF16)
    n_ref[...] = nb
    proj = _dot_nt(nb, wt_ref[...])
    nuv = uv_ref.shape[1]
    uv_ref[...] = proj[:, :nuv]
    qkv_ref[...] = proj[:, nuv:].astype(BF16)


def _inproj_fwd(h, g, w_in_t):
    s, d = h.shape
    sgu_w = SGU_GROUPS * GROUP_DIM
    sb_w = SB_HEADS * SB_HEAD_DIM
    return _row_call(_inproj_fwd_body, "inproj_fwd", s, [h], [g, w_in_t],
                     [(d, BF16), (2 * sgu_w, F32), (3 * sb_w, BF16)], [])


def _inproj_bwd_body(dh_ref, dproj_ref, h_ref, g_ref, wt_ref, dhout_ref, dg_ref):
    dn = _dot(dproj_ref[...], wt_ref[...])
    dhn, dg = _rms_bwd(h_ref[...], g_ref[...], dn)
    dhout_ref[...] = dh_ref[...] + dhn
    _acc(dg_ref, dg)


def _inproj_bwd(dh, dproj, h, g, w_in_t):
    s, d = h.shape
    return _row_call(_inproj_bwd_body, "inproj_bwd", s, [dh, dproj, h], [g, w_in_t],
                     [(d, F32)], [((1, d), F32)])


def _causal_w(ws_ref, g):
    row = lax.broadcasted_iota(jnp.int32, (CHUNK, CHUNK), 0)
    col = lax.broadcasted_iota(jnp.int32, (CHUNK, CHUNK), 1)
    return jnp.where(row >= col, ws_ref[g], 0.0), row >= col


def _group_norm(v):
    mu = jnp.mean(v, axis=-1, keepdims=True)
    d = v - mu
    rstd = lax.rsqrt(jnp.mean(d * d, axis=-1, keepdims=True) + EPS)
    return d * rstd, rstd


def _sgu_fwd_body(uv_ref, ng_ref, nb_ref, ws_ref, bs_ref, out_ref):
    width = SGU_GROUPS * GROUP_DIM
    for c in range(uv_ref.shape[0] // CHUNK):
        rows = pl.ds(c * CHUNK, CHUNK)
        for g in range(SGU_GROUPS):
            lanes = pl.ds(g * GROUP_DIM, GROUP_DIM)
            u = _gelu(uv_ref[rows, lanes])
            v = _gelu(uv_ref[rows, pl.ds(width + g * GROUP_DIM, GROUP_DIM)])
            vhat, _ = _group_norm(v)
            vn = vhat * ng_ref[:, lanes] + nb_ref[:, lanes]
            w, _ = _causal_w(ws_ref, g)
            mixed = _dot(w, vn) + bs_ref[g]
            out_ref[rows, lanes] = u * mixed


def _sgu_fwd(uv_pre, ng, nb, ws, bs):
    s = uv_pre.shape[0]
    return _row_call(_sgu_fwd_body, "sgu_fwd", s, [uv_pre], [ng, nb, ws, bs],
                     [(SGU_GROUPS * GROUP_DIM, F32)], [])[0]


def _sgu_bwd_body(uv_ref, do_ref, ng_ref, nb_ref, ws_ref, bs_ref,
                  duv_ref, dws_ref, dbs_ref, dng_ref, dnb_ref):
    width = SGU_GROUPS * GROUP_DIM

    @pl.when(pl.program_id(0) == 0)
    def _():
        dws_ref[...] = jnp.zeros_like(dws_ref)
        dbs_ref[...] = jnp.zeros_like(dbs_ref)
        dng_ref[...] = jnp.zeros_like(dng_ref)
        dnb_ref[...] = jnp.zeros_like(dnb_ref)

    for c in range(uv_ref.shape[0] // CHUNK):
        rows = pl.ds(c * CHUNK, CHUNK)
        for g in range(SGU_GROUPS):
            lanes = pl.ds(g * GROUP_DIM, GROUP_DIM)
            vlanes = pl.ds(width + g * GROUP_DIM, GROUP_DIM)
            u_pre = uv_ref[rows, lanes]
            v_pre = uv_ref[rows, vlanes]
            u = _gelu(u_pre)
            v = _gelu(v_pre)
            vhat, rstd = _group_norm(v)
            gain = ng_ref[:, lanes]
            vn = vhat * gain + nb_ref[:, lanes]
            w, causal = _causal_w(ws_ref, g)
            mixed = _dot(w, vn) + bs_ref[g]
            dout = do_ref[rows, lanes]
            du = dout * mixed
            dmixed = dout * u
            dbs_ref[g] += jnp.sum(dmixed, axis=1, keepdims=True)
            dws_ref[g] += jnp.where(causal, _dot_nt(dmixed, vn), 0.0)
            dvn = _dot_tn(w, dmixed)
            dng_ref[:, lanes] += jnp.sum(dvn * vhat, axis=0, keepdims=True)
            dnb_ref[:, lanes] += jnp.sum(dvn, axis=0, keepdims=True)
            dvh = dvn * gain
            dv = rstd * (dvh - jnp.mean(dvh, axis=-1, keepdims=True)
                         - vhat * jnp.mean(dvh * vhat, axis=-1, keepdims=True))
            duv_ref[rows, lanes] = (du * _gelu_grad(u_pre)).astype(BF16)
            duv_ref[rows, vlanes] = (dv * _gelu_grad(v_pre)).astype(BF16)


def _sgu_bwd(uv_pre, dout_a, ng, nb, ws, bs):
    s = uv_pre.shape[0]
    width = SGU_GROUPS * GROUP_DIM
    return _row_call(_sgu_bwd_body, "sgu_bwd", s, [uv_pre, dout_a], [ng, nb, ws, bs],
                     [(2 * width, BF16)],
                     [(ws.shape, F32), (bs.shape, F32), ((1, width), F32), ((1, width), F32)])


def _mesh_place():
    return lax.axis_index("x"), lax.axis_index("y"), lax.axis_index("c")


def _other_chips(mx, my):
    return [(1 - mx, my), (mx, 1 - my), (1 - mx, 1 - my)]


_ANY = pl.BlockSpec(memory_space=pl.ANY)
AG_SEMS = 7
_AG_SCRATCH = [pltpu.SemaphoreType.DMA((AG_SEMS,)), pltpu.SemaphoreType.DMA((AG_SEMS,)),
               pltpu.SemaphoreType.DMA(())]


def _gather_phases(x_ref, out_ref, send_sems, recv_sems, local_sem):
    mx, my, mc = _mesh_place()
    me, sibling = (mx, my, mc), (mx, my, 1 - mc)
    chips = _other_chips(mx, my)

    def slot(px, py, pc):
        return out_ref.at[4 * px + 2 * py + pc]

    def copy(k, block, to, src=None):
        return pltpu.make_async_remote_copy(
            src_ref=slot(*block) if src is None else src, dst_ref=slot(*block),
            send_sem=send_sems.at[k], recv_sem=recv_sems.at[k],
            device_id=to, device_id_type=MESH)

    mine = pltpu.make_async_copy(x_ref, slot(*me), local_sem)
    first = [copy(0, me, sibling, src=x_ref)]
    first += [copy(1 + j, me, (*chip, mc), src=x_ref) for j, chip in enumerate(chips)]
    passed = [copy(4 + j, (*chip, mc), sibling) for j, chip in enumerate(chips)]

    def start():
        mine.start()
        for cp in first:
            cp.start()

    def forward():
        for j, chip in enumerate(chips):
            copy(1 + j, (*chip, mc), me).wait_recv()
            passed[j].start()

    def finish():
        copy(0, sibling, me).wait_recv()
        for j, chip in enumerate(chips):
            copy(4 + j, (*chip, 1 - mc), me).wait_recv()
        for cp in first + passed:
            cp.wait_send()
        mine.wait()

    return start, forward, finish


def _all_gather(x, name):
    r, c = x.shape

    def body(x_ref, out_ref, send_sems, recv_sems, local_sem):
        start, forward, finish = _gather_phases(x_ref, out_ref, send_sems, recv_sems, local_sem)
        start()
        forward()
        finish()

    return pl.pallas_call(
        body,
        name=name,
        out_shape=jax.ShapeDtypeStruct((N_DEV, r, c), x.dtype),
        in_specs=[_ANY],
        out_specs=_ANY,
        scratch_shapes=list(_AG_SCRATCH),
    )(x)


def _scatter_phases(p_ref, out_ref, send_sems, recv_sems, local_sem):
    mx, my, mc = _mesh_place()
    me = 4 * mx + 2 * my + mc
    copies = []
    for k in range(1, N_DEV):
        tx, ty, tc = mx ^ ((k >> 2) & 1), my ^ ((k >> 1) & 1), mc ^ (k & 1)
        copies.append(pltpu.make_async_remote_copy(
            src_ref=p_ref.at[4 * tx + 2 * ty + tc], dst_ref=out_ref.at[me],
            send_sem=send_sems.at[k - 1], recv_sem=recv_sems.at[k - 1],
            device_id=(tx, ty, tc), device_id_type=MESH))
    mine = pltpu.make_async_copy(p_ref.at[me], out_ref.at[me], local_sem)

    def start():
        mine.start()
        for cp in copies:
            cp.start()

    def finish():
        for cp in copies:
            cp.wait()
        mine.wait()

    return start, finish


SB_DEAD = -105.0
HEADS_PER_TILE = 2


def _sb_logs(q, k, diagonal):
    row, col = _key_order()
    strict = jnp.logical_or(jnp.logical_not(diagonal), col < row)
    z = _dot_nt(q, k) * (SB_HEAD_DIM ** -0.5)
    sp = jnp.log1p(jnp.exp(-jnp.abs(z)))
    log_beta = jnp.minimum(z, 0.0) - sp
    log_1m_raw = -jnp.maximum(z, 0.0) - sp
    return strict, log_beta, log_1m_raw, jnp.where(strict, log_1m_raw, 0.0)


def _key_sums(x, pick):
    hi, lo = _split_bf16(x)
    return jnp.dot(hi, pick, preferred_element_type=F32) + jnp.dot(lo, pick, preferred_element_type=F32)


def _key_order():
    row = lax.broadcasted_iota(jnp.int32, (Q_BLOCK, Q_BLOCK), 0)
    col = lax.broadcasted_iota(jnp.int32, (Q_BLOCK, Q_BLOCK), 1)
    return row, col


def _head_halves(x):
    first = lax.broadcasted_iota(jnp.int32, x.shape, 1) < SB_HEAD_DIM
    zero = jnp.zeros_like(x)
    return jnp.where(first, x, zero), jnp.where(first, zero, x)


def _pick_halves(xa, xb):
    first = lax.broadcasted_iota(jnp.int32, xa.shape, 1) < SB_HEAD_DIM
    return jnp.where(first, xa, xb)


def _sb_fwd_body(q_ref, k_ref, v_ref, shard_ref, o_ref, tot_ref, cnt_ref, gathered_ref,
                 send_sems, recv_sems, local_sem):
    pair, qb = pl.program_id(0), pl.program_id(1)
    last_pair, last_qb = pl.num_programs(0) - 1, pl.num_programs(1) - 1
    ag_start, ag_forward, ag_finish = _gather_phases(shard_ref, gathered_ref, send_sems, recv_sems, local_sem)
    pl.when(jnp.logical_and(pair == 0, qb == 0))(ag_start)
    qs = _head_halves(q_ref[...])
    row, col = _key_order()
    later = (row > col).astype(BF16)

    def alive(carry):
        i, cs, _ = carry
        return jnp.logical_and(i <= qb, jnp.max(jnp.maximum(cs[0], cs[1])) > SB_DEAD)

    def step(carry):
        i, cs, accs = carry
        rows = pl.ds(pl.multiple_of((qb - i) * Q_BLOCK, Q_BLOCK), Q_BLOCK)
        k = k_ref[rows, :]
        v = v_ref[rows, :]
        new_cs, new_accs = [], []
        for q, c, acc in zip(qs, cs, accs):
            strict, log_beta, _, log_1m = _sb_logs(q, k, i == 0)
            a = jnp.where(strict, jnp.exp(log_beta + _key_sums(log_1m, later) + c), 0.0)
            new_accs.append(acc + jnp.dot(a.astype(BF16), v, preferred_element_type=F32))
            new_cs.append(c + jnp.sum(log_1m, axis=1, keepdims=True))
        return i + 1, tuple(new_cs), tuple(new_accs)

    zc = jnp.zeros((Q_BLOCK, 1), F32)
    za = jnp.zeros((Q_BLOCK, HEADS_PER_TILE * SB_HEAD_DIM), F32)
    n, cs, accs = lax.while_loop(alive, step, (jnp.int32(0), (zc, zc), (za, za)))
    o_ref[...] = _pick_halves(*accs)
    tot_ref[0] = cs[0]
    tot_ref[1] = cs[1]
    cnt_ref[pair, qb] = n.astype(F32)
    pl.when(jnp.logical_and(pair == last_pair // 2 + 1, qb == 0))(ag_forward)
    pl.when(jnp.logical_and(pair == last_pair, qb == last_qb))(ag_finish)


def _sb_fwd(qkv, shard):
    s = qkv.shape[0]
    lanes = HEADS_PER_TILE * SB_HEAD_DIM
    tiles = SB_HEADS // HEADS_PER_TILE
    nq = s // Q_BLOCK
    return pl.pallas_call(
        functools.partial(_sb_fwd_body),
        name="sb_fwd",
        grid=(tiles, nq),
        in_specs=[pl.BlockSpec((Q_BLOCK, lanes), lambda p, i: (i, p)),
                  pl.BlockSpec((s, lanes), lambda p, i: (0, tiles + p)),
                  pl.BlockSpec((s, lanes), lambda p, i: (0, 2 * tiles + p)),
                  _ANY],
        out_specs=[pl.BlockSpec((Q_BLOCK, lanes), lambda p, i: (i, p)),
                   pl.BlockSpec((HEADS_PER_TILE, Q_BLOCK, 1), lambda p, i: (p, i, 0)),
                   pl.BlockSpec(memory_space=pltpu.SMEM),
                   _ANY],
        out_shape=[jax.ShapeDtypeStruct((s, SB_HEADS * SB_HEAD_DIM), F32),
                   jax.ShapeDtypeStruct((SB_HEADS, s, 1), F32),
                   jax.ShapeDtypeStruct((tiles, nq), F32),
                   jax.ShapeDtypeStruct((N_DEV,) + shard.shape, shard.dtype)],
        scratch_shapes=list(_AG_SCRATCH),
        compiler_params=pltpu.CompilerParams(
            dimension_semantics=("arbitrary", "arbitrary"), vmem_limit_bytes=VMEM_LIMIT_V7X),
    )(qkv, qkv, qkv, shard)


def _sb_bwd_body(cnt_ref, q_ref, k_ref, v_ref, tot_ref, do_ref, part_ref, dq_ref, dk_ref, dv_ref, recv_ref,
                 send_sems, recv_sems, local_sem):
    pair, qb = pl.program_id(0), pl.program_id(1)
    last_pair, last_qb = pl.num_programs(0) - 1, pl.num_programs(1) - 1
    rs_start, rs_finish = _scatter_phases(part_ref, recv_ref, send_sems, recv_sems, local_sem)
    pl.when(jnp.logical_and(pair == 0, qb == 0))(rs_start)

    @pl.when(qb == 0)
    def _():
        dk_ref[...] = jnp.zeros_like(dk_ref)
        dv_ref[...] = jnp.zeros_like(dv_ref)

    qs = _head_halves(q_ref[...])
    dos = _head_halves(do_ref[...].astype(BF16))
    tots = (tot_ref[0], tot_ref[1])
    row, col = _key_order()
    up_to = (row <= col).astype(BF16)
    earlier = (row < col).astype(BF16)
    scale = SB_HEAD_DIM ** -0.5
    n = jnp.clip(cnt_ref[pair, qb].astype(jnp.int32), 0, qb + 1)
    start = qb - n + 1

    def step(i, carry):
        cs, ces, dqs = carry
        kb = start + i
        rows = pl.ds(pl.multiple_of(kb * Q_BLOCK, Q_BLOCK), Q_BLOCK)
        k = k_ref[rows, :]
        v = v_ref[rows, :]
        new_cs, new_ces, new_dqs = [], [], []
        dk = jnp.zeros((Q_BLOCK, HEADS_PER_TILE * SB_HEAD_DIM), F32)
        dv = jnp.zeros((Q_BLOCK, HEADS_PER_TILE * SB_HEAD_DIM), F32)
        for q, dob, tot, c, ce, dq in zip(qs, dos, tots, cs, ces, dqs):
            strict, log_beta, log_1m_raw, log_1m = _sb_logs(q, k, kb == qb)
            suffix = tot - c - _key_sums(log_1m, up_to)
            a = jnp.where(strict, jnp.exp(log_beta + suffix), 0.0)
            de = _dot_nt(dob, v) * a
            before = ce + _key_sums(de, earlier)
            dz = jnp.where(strict, de * jnp.exp(log_1m_raw) - before * jnp.exp(log_beta), 0.0)
            dzb = dz.astype(BF16)
            new_dqs.append(dq + jnp.dot(dzb, k, preferred_element_type=F32))
            dk = dk + _dot_tn(dzb, q)
            dv = dv + _dot_tn(a, dob)
            new_cs.append(c + jnp.sum(log_1m, axis=1, keepdims=True))
            new_ces.append(ce + jnp.sum(de, axis=1, keepdims=True))
        dk_ref[rows, :] += dk * scale
        dv_ref[rows, :] += dv
        return tuple(new_cs), tuple(new_ces), tuple(new_dqs)

    zc = jnp.zeros((Q_BLOCK, 1), F32)
    za = jnp.zeros((Q_BLOCK, HEADS_PER_TILE * SB_HEAD_DIM), F32)
    _, _, dqs = lax.fori_loop(0, n, step, ((zc, zc), (zc, zc), (za, za)))
    dq_ref[...] = _pick_halves(*dqs) * scale
    pl.when(jnp.logical_and(pair == last_pair, qb == last_qb))(rs_finish)


def _sb_bwd(cnt, qkv, tot, dout_b, parts):
    s = qkv.shape[0]
    lanes = HEADS_PER_TILE * SB_HEAD_DIM
    tiles = SB_HEADS // HEADS_PER_TILE
    return pl.pallas_call(
        functools.partial(_sb_bwd_body),
        name="sb_bwd",
        grid=(tiles, s // Q_BLOCK),
        in_specs=[pl.BlockSpec(memory_space=pltpu.SMEM),
                  pl.BlockSpec((Q_BLOCK, lanes), lambda p, i: (i, p)),
                  pl.BlockSpec((s, lanes), lambda p, i: (0, tiles + p)),
                  pl.BlockSpec((s, lanes), lambda p, i: (0, 2 * tiles + p)),
                  pl.BlockSpec((HEADS_PER_TILE, Q_BLOCK, 1), lambda p, i: (p, i, 0)),
                  pl.BlockSpec((Q_BLOCK, lanes), lambda p, i: (i, p)),
                  _ANY],
        out_specs=[pl.BlockSpec((Q_BLOCK, lanes), lambda p, i: (i, p)),
                   pl.BlockSpec((s, lanes), lambda p, i: (0, p)),
                   pl.BlockSpec((s, lanes), lambda p, i: (0, p)),
                   _ANY],
        out_shape=[jax.ShapeDtypeStruct((s, SB_HEADS * SB_HEAD_DIM), F32)] * 3
        + [jax.ShapeDtypeStruct(parts.shape, parts.dtype)],
        scratch_shapes=list(_AG_SCRATCH),
        compiler_params=pltpu.CompilerParams(
            dimension_semantics=("arbitrary", "arbitrary"), vmem_limit_bytes=VMEM_LIMIT_V7X),
    )(cnt, qkv, qkv, qkv, tot, dout_b, parts)


def _outproj_fwd_body(oa_ref, ob_ref, h_ref, ga_ref, gb_ref, gpost_ref, w_ref,
                      merged_ref, mo_ref, hout_ref):
    half = oa_ref.shape[1]
    ma, _ = _rms(oa_ref[...], ga_ref[...])
    mb, _ = _rms(ob_ref[...], gb_ref[...])
    mab = ma.astype(BF16)
    mbb = mb.astype(BF16)
    merged_ref[:, :half] = mab
    merged_ref[:, half:] = mbb
    mo = (jnp.dot(mab, w_ref[:half, :], preferred_element_type=F32)
          + jnp.dot(mbb, w_ref[half:, :], preferred_element_type=F32))
    mo_ref[...] = mo
    y, _ = _rms(mo, gpost_ref[...])
    hout_ref[...] = h_ref[...] + y


def _outproj_fwd(out_a, out_b, h, ga, gb, gpost, w_out):
    s, d = h.shape
    return _row_call(_outproj_fwd_body, "outproj_fwd", s, [out_a, out_b, h], [ga, gb, gpost, w_out],
                     [(d, BF16), (d, F32), (d, F32)], [])


def _outproj_bwd_body(dh_ref, mo_ref, oa_ref, ob_ref, ga_ref, gb_ref, gpost_ref, w_ref,
                      dmo_ref, doa_ref, dob_ref, dga_ref, dgb_ref, dgpost_ref):
    half = oa_ref.shape[1]
    dmo, dgpost = _rms_bwd(mo_ref[...], gpost_ref[...], dh_ref[...])
    dmob = dmo.astype(BF16)
    dmo_ref[...] = dmob
    dma = _dot_nt(dmob, w_ref[:half, :])
    dmb = _dot_nt(dmob, w_ref[half:, :])
    doa, dga = _rms_bwd(oa_ref[...], ga_ref[...], dma)
    dob, dgb = _rms_bwd(ob_ref[...], gb_ref[...], dmb)
    doa_ref[...] = doa
    dob_ref[...] = dob
    _acc(dga_ref, dga)
    _acc(dgb_ref, dgb)
    _acc(dgpost_ref, dgpost)


def _outproj_bwd(dh, mo, out_a, out_b, ga, gb, gpost, w_out):
    s, d = dh.shape
    half = out_a.shape[1]
    return _row_call(_outproj_bwd_body, "outproj_bwd", s, [dh, mo, out_a, out_b], [ga, gb, gpost, w_out],
                     [(d, BF16), (half, F32), (half, F32)],
                     [((1, half), F32), ((1, half), F32), ((1, d), F32)])


def _kv_fwd_body(mem_ref, g_ref, wt_ref, memn_ref, kv_ref):
    n, _ = _rms(mem_ref[...], g_ref[...])
    nb = n.astype(BF16)
    memn_ref[...] = nb
    kv_ref[...] = _dot_nt(nb, wt_ref[...]).astype(BF16)


def _kv_fwd(mem, g, w_kv_t):
    m, d = mem.shape
    return _row_call(_kv_fwd_body, "kv_fwd", m, [mem], [g, w_kv_t], [(d, BF16), (w_kv_t.shape[0], BF16)], [])


def _kv_bwd_body(dkv_ref, mem_ref, memn_ref, g_ref, wt_ref, dwt_ref, dg_ref):
    dkvb = dkv_ref[...].astype(BF16)
    dwt_ref[...] = _dot_tn(dkvb, memn_ref[...])
    dmemn = _dot(dkvb, wt_ref[...])
    _, dg = _rms_bwd(mem_ref[...], g_ref[...], dmemn)
    dg_ref[...] = dg


def _kv_bwd(dkv, mem, memn, g, w_kv_t):
    m, d = mem.shape
    return pl.pallas_call(
        functools.partial(_kv_bwd_body),
        name="kv_bwd",
        out_shape=[jax.ShapeDtypeStruct(w_kv_t.shape, F32), jax.ShapeDtypeStruct((1, d), F32)],
        compiler_params=pltpu.CompilerParams(vmem_limit_bytes=VMEM_LIMIT_V7X),
    )(dkv, mem, memn, g, w_kv_t)


def _xa_fwd_body(h_ref, gpre_ref, gpost_ref, wq_ref, wo_ref, kv_ref,
                 n_ref, q_ref, o_ref, c_ref, hout_ref):
    h = h_ref[...]
    d = h.shape[1]
    n, _ = _rms(h, gpre_ref[...])
    nb = n.astype(BF16)
    n_ref[...] = nb
    qb = jnp.dot(nb, wq_ref[...], preferred_element_type=F32).astype(BF16)
    q_ref[...] = qb
    for hd in range(XA_HEADS):
        lanes = slice(hd * XA_HEAD_DIM, (hd + 1) * XA_HEAD_DIM)
        k = kv_ref[:, lanes]
        v = kv_ref[:, d + hd * XA_HEAD_DIM:d + (hd + 1) * XA_HEAD_DIM]
        logits = _dot_nt(qb[:, lanes], k) * (XA_HEAD_DIM ** -0.5)
        e = jnp.exp(logits - jnp.max(logits, axis=-1, keepdims=True))
        p = e / jnp.sum(e, axis=-1, keepdims=True)
        o_ref[:, lanes] = jnp.dot(p.astype(BF16), v, preferred_element_type=F32).astype(BF16)
    c = jnp.dot(o_ref[...], wo_ref[...], preferred_element_type=F32)
    c_ref[...] = c
    y, _ = _rms(c, gpost_ref[...])
    hout_ref[...] = h + y


def _xa_fwd(h, gpre, gpost, wq, wo, kv):
    s, d = h.shape
    return _row_call(_xa_fwd_body, "xa_fwd", s, [h], [gpre, gpost, wq, wo, kv],
                     [(d, BF16), (d, BF16), (d, BF16), (d, F32), (d, F32)], [])


def _xa_bwd_body(dh_ref, h_ref, c_ref, q_ref, o_ref, gpre_ref, gpost_ref, wq_ref, wo_ref, kv_ref,
                 dhout_ref, dc_ref, dq_ref, dkv_ref, dgpre_ref, dgpost_ref):
    dh = dh_ref[...]
    d = dh.shape[1]
    scale = XA_HEAD_DIM ** -0.5
    dc, dgpost = _rms_bwd(c_ref[...], gpost_ref[...], dh)
    dcb = dc.astype(BF16)
    dc_ref[...] = dcb
    dob = _dot_nt(dcb, wo_ref[...]).astype(BF16)

    @pl.when(pl.program_id(0) == 0)
    def _():
        dkv_ref[...] = jnp.zeros_like(dkv_ref)

    for hd in range(XA_HEADS):
        lanes = slice(hd * XA_HEAD_DIM, (hd + 1) * XA_HEAD_DIM)
        vlanes = slice(d + hd * XA_HEAD_DIM, d + (hd + 1) * XA_HEAD_DIM)
        qh = q_ref[:, lanes]
        k = kv_ref[:, lanes]
        v = kv_ref[:, vlanes]
        logits = _dot_nt(qh, k) * scale
        e = jnp.exp(logits - jnp.max(logits, axis=-1, keepdims=True))
        p = e / jnp.sum(e, axis=-1, keepdims=True)
        doh = dob[:, lanes]
        dp = _dot_nt(doh, v)
        dl = (p * (dp - jnp.sum(dp * p, axis=-1, keepdims=True)) * scale).astype(BF16)
        dq_ref[:, lanes] = jnp.dot(dl, k, preferred_element_type=F32).astype(BF16)
        dkv_ref[:, lanes] += _dot_tn(dl, qh)
        dkv_ref[:, vlanes] += _dot_tn(p, doh)
    dn = _dot_nt(dq_ref[...], wq_ref[...])
    dhn, dgpre = _rms_bwd(h_ref[...], gpre_ref[...], dn)
    dhout_ref[...] = dh + dhn
    _acc(dgpre_ref, dgpre)
    _acc(dgpost_ref, dgpost)


def _xa_bwd(dh, h, c, q, o, gpre, gpost, wq, wo, kv):
    s, d = h.shape
    return _row_call(_xa_bwd_body, "xa_bwd", s, [dh, h, c, q, o], [gpre, gpost, wq, wo, kv],
                     [(d, F32), (d, BF16), (d, BF16)],
                     [(kv.shape, F32), ((1, d), F32), ((1, d), F32)])


def _final_body(h_ref, t_ref, g_ref, dh_ref, loss_ref, dg_ref):
    h = h_ref[...]
    d = h.shape[1]
    y, _ = _rms(h, g_ref[...])
    err = y - t_ref[...]
    part = (0.5 / d) * jnp.sum(jnp.sum(err * err, axis=1, keepdims=True), axis=0, keepdims=True)
    dh, dg = _rms_bwd(h, g_ref[...], err * (1.0 / d))
    dh_ref[...] = dh
    _acc(loss_ref, part)
    _acc(dg_ref, dg)


def _final(h, g, target):
    s, d = h.shape
    return _row_call(_final_body, "final_loss", s, [h, target], [g],
                     [(d, F32)], [((1, 1), F32), ((1, d), F32)])


def _largest_tile(n, cap):
    best = 128
    for t in range(128, cap + 1, 128):
        if n % t == 0:
            best = t
    return best


def _mm_tn(a, bs, name):
    s, k = a.shape
    n = bs[0].shape[1]
    nb = len(bs)
    ts = min(512, s)
    tk = _largest_tile(k, 1536)
    tn = _largest_tile(n, 1536 // nb)

    def body(a_ref, *refs):
        b_refs, o_refs = refs[:nb], refs[nb:]
        at = a_ref[...]

        @pl.when(pl.program_id(2) == 0)
        def _():
            for o_ref in o_refs:
                o_ref[...] = jnp.zeros_like(o_ref)

        for b_ref, o_ref in zip(b_refs, o_refs):
            o_ref[...] += _dot_tn(at, b_ref[...])

    return pl.pallas_call(
        body,
        name=name,
        grid=(k // tk, n // tn, s // ts),
        in_specs=[pl.BlockSpec((ts, tk), lambda i, j, t: (t, i))]
        + [pl.BlockSpec((ts, tn), lambda i, j, t: (t, j))] * nb,
        out_specs=[pl.BlockSpec((tk, tn), lambda i, j, t: (i, j))] * nb,
        out_shape=[jax.ShapeDtypeStruct((k, n), F32)] * nb,
        compiler_params=pltpu.CompilerParams(
            dimension_semantics=("arbitrary", "arbitrary", "arbitrary"),
            vmem_limit_bytes=VMEM_LIMIT_V7X),
    )(a, *bs)


_SMALL_SHAPES = {
    "sgu_norm_g": (1, SGU_GROUPS * GROUP_DIM),
    "sgu_norm_b": (1, SGU_GROUPS * GROUP_DIM),
    "sgu_w_s": (SGU_GROUPS, CHUNK, CHUNK),
    "sgu_b_s": (SGU_GROUPS, CHUNK, 1),
}


def _small_views(small):
    return {n: v.reshape(_SMALL_SHAPES.get(n, v.shape)) for n, v in small.items()}


def _small_unviews(views, like):
    return {n: v.reshape(like[n].shape) for n, v in views.items()}


def _unpack_rows(gathered, names, shard_rows):
    out, off = {}, 0
    for n in names:
        rows = shard_rows[n]
        out[n] = gathered[:, off:off + rows, :].reshape(N_DEV * rows, gathered.shape[2])
        off += rows
    return out


def _row_tile(r, cap):
    best = 16
    for t in range(16, cap + 1, 16):
        if r % t == 0:
            best = t
    return best


def _sum_received(received):
    _, r, c = received.shape
    tr = _row_tile(r, 1024)

    def body(rc_ref, g_ref):
        g = rc_ref[0].astype(F32)
        for t in range(1, N_DEV):
            g = g + rc_ref[t].astype(F32)
        g_ref[...] = g

    return pl.pallas_call(
        body, name="rs_sum_received", grid=(r // tr,),
        in_specs=[pl.BlockSpec((N_DEV, tr, c), lambda i: (0, i, 0))],
        out_specs=pl.BlockSpec((tr, c), lambda i: (i, 0)),
        out_shape=jax.ShapeDtypeStruct((r, c), F32),
    )(received)


def _local_step(x, mem, target, small, big, late_shard, late_names, shard_rows):
    sm, w = small, dict(big)
    d_model = x.shape[1]
    h1, n1, a1, b1, f1 = _ffn_fwd(x, sm["ffn1_pre_g"], sm["ffn1_post_g"],
                                  w["ffn1_w_gate"], w["ffn1_w_up"], w["ffn1_w_down"], "ffn1_fwd")
    n2, uv_pre, qkv = _inproj_fwd(h1, sm["mix_pre_g"], w["w_in"])
    out_a = _sgu_fwd(uv_pre, sm["sgu_norm_g"], sm["sgu_norm_b"], sm["sgu_w_s"], sm["sgu_b_s"])
    out_b, tot, cnt, late = _sb_fwd(qkv, late_shard)
    w.update(_unpack_rows(late, late_names, shard_rows))
    merged, mo, h2 = _outproj_fwd(out_a, out_b, h1, sm["sgu_out_g"], sm["sb_out_g"],
                                  sm["mix_post_g"], w["w_out"])
    memn, kv = _kv_fwd(mem, sm["mem_norm_g"], w["xa_w_kv"])
    n3, qx, ox, cx, h3 = _xa_fwd(h2, sm["xa_pre_g"], sm["xa_post_g"], w["xa_w_q"], w["xa_w_o"], kv)
    h4, n4, a2, b2, f2 = _ffn_fwd(h3, sm["ffn2_pre_g"], sm["ffn2_post_g"],
                                  w["ffn2_w_gate"], w["ffn2_w_up"], w["ffn2_w_down"], "ffn2_fwd")
    dh4, loss, dg_final = _final(h4, sm["final_norm_g"], target)

    gs, gw = {"final_norm_g": dg_final}, {}
    dh3, da2, db2, hm2, df2, gs["ffn2_pre_g"], gs["ffn2_post_g"] = _ffn_bwd(
        dh4, h3, a2, b2, f2, sm["ffn2_pre_g"], sm["ffn2_post_g"],
        w["ffn2_w_gate"], w["ffn2_w_up"], w["ffn2_w_down"], "ffn2_bwd")
    gw["ffn2_w_gate"], = _mm_tn(da2, [n4], "ffn2_dw_gate")
    gw["ffn2_w_up"], = _mm_tn(db2, [n4], "ffn2_dw_up")
    gw["ffn2_w_down"], = _mm_tn(hm2, [df2], "ffn2_dw_down")

    dh2, dc, dqx, dkv, gs["xa_pre_g"], gs["xa_post_g"] = _xa_bwd(
        dh3, h2, cx, qx, ox, sm["xa_pre_g"], sm["xa_post_g"], w["xa_w_q"], w["xa_w_o"], kv)
    gw["xa_w_o"], = _mm_tn(ox, [dc], "xa_dw_o")
    gw["xa_w_q"], = _mm_tn(n3, [dqx], "xa_dw_q")
    gw["xa_w_kv"], gs["mem_norm_g"] = _kv_bwd(dkv, mem, memn, sm["mem_norm_g"], w["xa_w_kv"])

    dmo, dout_a, dout_b, gs["sgu_out_g"], gs["sb_out_g"], gs["mix_post_g"] = _outproj_bwd(
        dh2, mo, out_a, out_b, sm["sgu_out_g"], sm["sb_out_g"], sm["mix_post_g"], w["w_out"])
    gw["w_out"], = _mm_tn(merged, [dmo], "mix_dw_out")
    parts = jnp.concatenate(
        [gw.pop(n).reshape(N_DEV, -1, d_model).astype(BF16) for n in late_names], axis=1)
    dq, dk, dv, received = _sb_bwd(cnt, qkv, tot, dout_b, parts)
    duv, gs["sgu_w_s"], gs["sgu_b_s"], gs["sgu_norm_g"], gs["sgu_norm_b"] = _sgu_bwd(
        uv_pre, dout_a, sm["sgu_norm_g"], sm["sgu_norm_b"], sm["sgu_w_s"], sm["sgu_b_s"])
    dproj = jnp.concatenate([duv] + [t.astype(BF16) for t in (dq, dk, dv)], axis=1)
    dh1, gs["mix_pre_g"] = _inproj_bwd(dh2, dproj, h1, sm["mix_pre_g"], w["w_in"])
    gw["w_in"], = _mm_tn(dproj, [n2], "mix_dw_in")

    dx, da1, db1, hm1, df1, gs["ffn1_pre_g"], gs["ffn1_post_g"] = _ffn_bwd(
        dh1, x, a1, b1, f1, sm["ffn1_pre_g"], sm["ffn1_post_g"],
        w["ffn1_w_gate"], w["ffn1_w_up"], w["ffn1_w_down"], "ffn1_bwd")
    gw["ffn1_w_gate"], = _mm_tn(da1, [n1], "ffn1_dw_gate")
    gw["ffn1_w_up"], = _mm_tn(db1, [n1], "ffn1_dw_up")
    gw["ffn1_w_down"], = _mm_tn(hm1, [df1], "ffn1_dw_down")
    return loss, dx, gs, gw, _sum_received(received)


def _pair_exchange(p4):
    nchip, _, r, c = p4.shape

    def body(p_ref, out_ref, send_sems, recv_sems):
        mx, my, mc = _mesh_place()
        copies = [pltpu.make_async_remote_copy(
            src_ref=p_ref.at[j, 1 - mc], dst_ref=out_ref.at[j],
            send_sem=send_sems.at[j], recv_sem=recv_sems.at[j],
            device_id=(mx, my, 1 - mc), device_id_type=MESH) for j in range(nchip)]
        for cp in copies:
            cp.start()
        for cp in copies:
            cp.wait()

    return pl.pallas_call(
        body,
        name="rs_pair_exchange",
        out_shape=jax.ShapeDtypeStruct((nchip, r, c), p4.dtype),
        in_specs=[_ANY],
        out_specs=_ANY,
        scratch_shapes=[pltpu.SemaphoreType.DMA((nchip,)), pltpu.SemaphoreType.DMA((nchip,))],
    )(p4)


def _chip_exchange(q):
    _, r, c = q.shape

    def body(q_ref, out_ref, send_sems, recv_sems):
        mx, my, mc = _mesh_place()
        copies = [pltpu.make_async_remote_copy(
            src_ref=q_ref.at[2 * cx + cy], dst_ref=out_ref.at[k],
            send_sem=send_sems.at[k], recv_sem=recv_sems.at[k],
            device_id=(cx, cy, mc), device_id_type=MESH)
            for k, (cx, cy) in enumerate(_other_chips(mx, my))]
        for cp in copies:
            cp.start()
        for cp in copies:
            cp.wait()

    return pl.pallas_call(
        body,
        name="rs_chip_exchange",
        out_shape=jax.ShapeDtypeStruct((3, r, c), q.dtype),
        in_specs=[_ANY],
        out_specs=_ANY,
        scratch_shapes=[pltpu.SemaphoreType.DMA((3,)), pltpu.SemaphoreType.DMA((3,))],
    )(q)


def _rs_row_tile(r):
    return _row_tile(r, 1024)


def _pair_sum(place, p4, recv_a):
    nchip, _, r, c = p4.shape
    tr = _rs_row_tile(r)

    def body(place_ref, p_ref, a_ref, q_ref):
        q_ref[0] = (p_ref[0, 0].astype(F32) + a_ref[0].astype(F32)).astype(BF16)

    return pl.pallas_call(
        body,
        name="rs_pair_sum",
        grid_spec=pltpu.PrefetchScalarGridSpec(
            num_scalar_prefetch=1,
            grid=(nchip, r // tr),
            in_specs=[pl.BlockSpec((1, 1, tr, c), lambda j, i, pref: (j, pref[0], i, 0)),
                      pl.BlockSpec((1, tr, c), lambda j, i, pref: (j, i, 0))],
            out_specs=pl.BlockSpec((1, tr, c), lambda j, i, pref: (j, i, 0)),
        ),
        out_shape=jax.ShapeDtypeStruct((nchip, r, c), BF16),
    )(place, p4, recv_a)


def _rs_final(place, p4, recv_a, recv_b):
    _, _, r, c = p4.shape
    tr = _rs_row_tile(r)

    def body(place_ref, p_ref, a_ref, b_ref, g_ref):
        g = p_ref[0, 0].astype(F32) + a_ref[0].astype(F32)
        for k in range(3):
            g = g + b_ref[k].astype(F32)
        g_ref[...] = g

    return pl.pallas_call(
        body,
        name="rs_final_sum",
        grid_spec=pltpu.PrefetchScalarGridSpec(
            num_scalar_prefetch=1,
            grid=(r // tr,),
            in_specs=[pl.BlockSpec((1, 1, tr, c), lambda i, pref: (pref[1], pref[0], i, 0)),
                      pl.BlockSpec((1, tr, c), lambda i, pref: (pref[1], i, 0)),
                      pl.BlockSpec((3, tr, c), lambda i, pref: (0, i, 0))],
            out_specs=pl.BlockSpec((tr, c), lambda i, pref: (i, 0)),
        ),
        out_shape=jax.ShapeDtypeStruct((r, c), F32),
    )(place, p4, recv_a, recv_b)


def _adamw_math(w, g, m, v):
    m = ADAM_B1 * m + (1.0 - ADAM_B1) * g
    v = ADAM_B2 * v + (1.0 - ADAM_B2) * (g * g)
    m_hat = m / (1.0 - ADAM_B1 ** ADAM_STEP)
    v_hat = v / (1.0 - ADAM_B2 ** ADAM_STEP)
    delta = -ADAM_LR * (m_hat / (jnp.sqrt(v_hat) + ADAM_EPS) + ADAM_WD * w)
    return delta, m, v


def _adamw(w, g, m, v, name):
    r, c = w.shape
    tr = r if r <= 512 else 256

    def body(w_ref, g_ref, m_ref, v_ref, d_ref, mo_ref, vo_ref):
        d_ref[...], mo_ref[...], vo_ref[...] = _adamw_math(w_ref[...], g_ref[...], m_ref[...], v_ref[...])

    spec = pl.BlockSpec((tr, c), lambda i: (i, 0))
    out = jax.ShapeDtypeStruct((r, c), F32)
    return pl.pallas_call(
        body, name=name, grid=(r // tr,), in_specs=[spec] * 4, out_specs=[spec] * 3,
        out_shape=[out] * 3,
    )(w, g, m, v)


def _small_sum_adamw(gathered, w, m, v):
    _, r, c = gathered.shape

    def body(ga_ref, w_ref, m_ref, v_ref, g_ref, d_ref, mo_ref, vo_ref):
        g = ga_ref[0]
        for k in range(1, N_DEV):
            g = g + ga_ref[k]
        g_ref[...] = g
        d_ref[...], mo_ref[...], vo_ref[...] = _adamw_math(w_ref[...], g, m_ref[...], v_ref[...])

    out = jax.ShapeDtypeStruct((r, c), F32)
    return pl.pallas_call(body, name="small_sum_adamw", out_shape=[out] * 4)(gathered, w, m, v)


_WEIGHTS = ["ffn1_pre_g", "ffn1_post_g", "ffn1_w_gate", "ffn1_w_up", "ffn1_w_down", "mix_pre_g",
            "mix_post_g", "w_in", "sgu_norm_g", "sgu_norm_b", "sgu_w_s", "sgu_b_s", "sgu_out_g",
            "sb_out_g", "w_out", "xa_pre_g", "xa_post_g", "mem_norm_g", "xa_w_q", "xa_w_kv", "xa_w_o",
            "ffn2_pre_g", "ffn2_post_g", "ffn2_w_gate", "ffn2_w_up", "ffn2_w_down", "final_norm_g"]
_BIG = ["ffn1_w_gate", "ffn1_w_up", "ffn1_w_down", "w_in", "w_out", "xa_w_q", "xa_w_kv", "xa_w_o",
        "ffn2_w_gate", "ffn2_w_up", "ffn2_w_down"]
_COL_SHARDED = ("ffn1_w_gate", "ffn1_w_up", "w_in", "xa_w_kv", "ffn2_w_gate", "ffn2_w_up")
_EARLY = ["ffn1_w_gate", "ffn1_w_up", "ffn1_w_down", "w_in"]
_LATE = [n for n in _BIG if n not in _EARLY]
_SMALL = [n for n in _WEIGHTS if n not in _BIG]
SMALL_LANES = 128
SMALL_ROW_ALIGN = 8


def _pack_small(tensors):
    parts = []
    for n in _SMALL:
        t = tensors[n].reshape(-1, SMALL_LANES)
        pad = (-t.shape[0]) % SMALL_ROW_ALIGN
        parts.append(jnp.pad(t, ((0, pad), (0, 0))) if pad else t)
    return jnp.concatenate(parts, axis=0)


def _unpack_small(packed, like):
    out, off = {}, 0
    for n in _SMALL:
        size = like[n].size
        rows = size // SMALL_LANES
        out[n] = packed[off:off + rows].reshape(like[n].shape)
        off += rows + (-rows) % SMALL_ROW_ALIGN
    return out


def kernel(x, mem, ffn1_pre_g, ffn1_post_g, ffn1_w_gate, ffn1_w_up, ffn1_w_down, mix_pre_g, mix_post_g, w_in, sgu_norm_g, sgu_norm_b, sgu_w_s, sgu_b_s, sgu_out_g, sb_out_g, w_out, xa_pre_g, xa_post_g, mem_norm_g, xa_w_q, xa_w_kv, xa_w_o, ffn2_pre_g, ffn2_post_g, ffn2_w_gate, ffn2_w_up, ffn2_w_down, final_norm_g, loss_target, m_ffn1_pre_g, m_ffn1_post_g, m_ffn1_w_gate, m_ffn1_w_up, m_ffn1_w_down, m_mix_pre_g, m_mix_post_g, m_w_in, m_sgu_norm_g, m_sgu_norm_b, m_sgu_w_s, m_sgu_b_s, m_sgu_out_g, m_sb_out_g, m_w_out, m_xa_pre_g, m_xa_post_g, m_mem_norm_g, m_xa_w_q, m_xa_w_kv, m_xa_w_o, m_ffn2_pre_g, m_ffn2_post_g, m_ffn2_w_gate, m_ffn2_w_up, m_ffn2_w_down, m_final_norm_g, v_ffn1_pre_g, v_ffn1_post_g, v_ffn1_w_gate, v_ffn1_w_up, v_ffn1_w_down, v_mix_pre_g, v_mix_post_g, v_w_in, v_sgu_norm_g, v_sgu_norm_b, v_sgu_w_s, v_sgu_b_s, v_sgu_out_g, v_sb_out_g, v_w_out, v_xa_pre_g, v_xa_post_g, v_mem_norm_g, v_xa_w_q, v_xa_w_kv, v_xa_w_o, v_ffn2_pre_g, v_ffn2_post_g, v_ffn2_w_gate, v_ffn2_w_up, v_ffn2_w_down, v_final_norm_g):
    vals = dict(locals())
    d_model = x.shape[-1]

    def packed(names):
        return jnp.concatenate(
            [(vals[n][0].T if n in _COL_SHARDED else vals[n][0]).astype(BF16) for n in names], axis=0)

    shard_rows = {n: vals[n].shape[2 if n in _COL_SHARDED else 1] for n in _BIG}
    big = _unpack_rows(_all_gather(packed(_EARLY), "ag_weights"), _EARLY, shard_rows)

    small = {n: vals[n] for n in _SMALL}
    loss_part, dx, gs, gw, late_rows = _local_step(
        x[0], mem[0], loss_target[0], _small_views(small), big, packed(_LATE), _LATE, shard_rows)
    loss = lax.psum(loss_part[0, 0], ("x", "y", "c"))

    parts = [gw[n].reshape(N_DEV, -1, d_model).astype(BF16) for n in _EARLY]
    rows = sum(p.shape[1] for p in parts)
    p4 = jnp.concatenate(parts, axis=1).reshape(N_DEV // 2, 2, rows, d_model)
    mx, my, mc = _mesh_place()
    place = jnp.stack([mc, 2 * mx + my]).astype(jnp.int32)
    recv_a = _pair_exchange(p4)
    recv_b = _chip_exchange(_pair_sum(place, p4, recv_a))
    early_rows = _rs_final(place, p4, recv_a, recv_b)

    grads, deltas, new_m, new_v = {}, {}, {}, {}
    for names, g_rows in ((_EARLY, early_rows), (_LATE, late_rows)):
        off = 0
        for n in names:
            rows = shard_rows[n]
            g = g_rows[off:off + rows]
            off += rows
            g = g.T if n in _COL_SHARDED else g
            grads[n] = g[None]
            d, m1, v1 = _adamw(vals[n][0], g, vals["m_" + n][0], vals["v_" + n][0], "adamw_" + n)
            deltas[n], new_m[n], new_v[n] = d[None], m1[None], v1[None]

    gathered_small = _all_gather(_pack_small(gs), "ag_small_grads")
    outs = _small_sum_adamw(gathered_small, _pack_small(small),
                            _pack_small({n: vals["m_" + n] for n in _SMALL}),
                            _pack_small({n: vals["v_" + n] for n in _SMALL}))
    for dst, packed in zip((grads, deltas, new_m, new_v), outs):
        dst.update(_unpack_small(packed, small))

    return (loss, dx[None], *[grads[n] for n in _WEIGHTS], *[deltas[n] for n in _WEIGHTS],
            *[new_m[n] for n in _WEIGHTS], *[new_v[n] for n in _WEIGHTS])
```

```python
import functools

import jax
import jax.numpy as jnp
from jax import lax
from jax.experimental import pallas as pl
from jax.experimental.pallas import tpu as pltpu

F32 = jnp.float32
BF16 = jnp.bfloat16
EPS = 1e-6
MESH = pl.DeviceIdType.MESH
N_DEV = 8

SGU_GROUPS = 4
GROUP_DIM = 128
CHUNK = 128
SB_HEADS = 8
SB_HEAD_DIM = 64
Q_BLOCK = 128
XA_HEADS = 4
XA_HEAD_DIM = 256

ADAM_LR = 0.001
ADAM_B1 = 0.9
ADAM_B2 = 0.999
ADAM_EPS = 1e-08
ADAM_WD = 0.01
ADAM_STEP = 10

VMEM_LIMIT_V7X = 56 * 1024 * 1024
GELU_C0 = 0.7978845608028654
GELU_C1 = 0.044715


def _dot(a, b):
    return jnp.dot(a.astype(BF16), b.astype(BF16), preferred_element_type=F32)


def _dot_nt(a, b):
    return lax.dot_general(a.astype(BF16), b.astype(BF16), (((1,), (1,)), ((), ())),
                           preferred_element_type=F32)


def _dot_tn(a, b):
    return lax.dot_general(a.astype(BF16), b.astype(BF16), (((0,), (0,)), ((), ())),
                           preferred_element_type=F32)


def _rms(x, g):
    r = lax.rsqrt(jnp.mean(x * x, axis=-1, keepdims=True) + EPS)
    return x * r * g, r


def _rms_bwd(x, g, dy):
    r = lax.rsqrt(jnp.mean(x * x, axis=-1, keepdims=True) + EPS)
    xh = x * r
    gy = dy * g
    dx = r * (gy - xh * jnp.mean(gy * xh, axis=-1, keepdims=True))
    dg = jnp.sum(dy * xh, axis=0, keepdims=True)
    return dx, dg


def _sigmoid(x):
    return 1.0 / (1.0 + jnp.exp(-x))


def _gelu(x):
    t = jnp.tanh(GELU_C0 * (x + GELU_C1 * x * x * x))
    return 0.5 * x * (1.0 + t)


def _gelu_grad(x):
    t = jnp.tanh(GELU_C0 * (x + GELU_C1 * x * x * x))
    return 0.5 * (1.0 + t) + 0.5 * x * (1.0 - t * t) * GELU_C0 * (1.0 + 3.0 * GELU_C1 * x * x)


def _split_bf16(x):
    hi = x.astype(BF16)
    lo = (x - hi.astype(F32)).astype(BF16)
    return hi, lo


def _row_spec(tm, cols):
    return pl.BlockSpec((tm, cols), lambda i: (i, 0))


def _full_spec(shape):
    nd = len(shape)
    return pl.BlockSpec(tuple(shape), lambda i: (0,) * nd)


def _token_tile(s):
    return min(256, s)


def _row_call(body, name, s, tiled_in, full_in, tiled_out, acc_out):
    tm = _token_tile(s)
    in_specs = [_row_spec(tm, a.shape[1]) for a in tiled_in] + [_full_spec(a.shape) for a in full_in]
    out_specs = [_row_spec(tm, c) for c, _ in tiled_out] + [_full_spec(sh) for sh, _ in acc_out]
    out_shape = [jax.ShapeDtypeStruct((s, c), dt) for c, dt in tiled_out]
    out_shape += [jax.ShapeDtypeStruct(sh, dt) for sh, dt in acc_out]
    return pl.pallas_call(
        functools.partial(body),
        name=name,
        grid=(s // tm,),
        in_specs=in_specs,
        out_specs=out_specs,
        out_shape=out_shape,
        compiler_params=pltpu.CompilerParams(
            dimension_semantics=("arbitrary",), vmem_limit_bytes=VMEM_LIMIT_V7X),
    )(*tiled_in, *full_in)


def _acc(ref, val):
    @pl.when(pl.program_id(0) == 0)
    def _():
        ref[...] = val

    @pl.when(pl.program_id(0) != 0)
    def _():
        ref[...] += val


def _ffn_fwd_body(x_ref, pre_ref, post_ref, wgt_ref, wut_ref, wd_ref,
                  h_ref, n_ref, a_ref, b_ref, f_ref):
    x = x_ref[...]
    n, _ = _rms(x, pre_ref[...])
    nb = n.astype(BF16)
    n_ref[...] = nb
    a = _dot_nt(nb, wgt_ref[...])
    b = _dot_nt(nb, wut_ref[...])
    a_ref[...] = a.astype(BF16)
    b_ref[...] = b.astype(BF16)
    hmid = a * _sigmoid(a) * b
    f = jnp.dot(hmid.astype(BF16), wd_ref[...], preferred_element_type=F32)
    f_ref[...] = f
    y, _ = _rms(f, post_ref[...])
    h_ref[...] = x + 0.5 * y


def _ffn_fwd(x, pre_g, post_g, wgt, wut, wd, name):
    s, d = x.shape
    f = wgt.shape[0]
    return _row_call(_ffn_fwd_body, name, s, [x], [pre_g, post_g, wgt, wut, wd],
                     [(d, F32), (d, BF16), (f, BF16), (f, BF16), (d, F32)], [])


def _ffn_bwd_body(dh_ref, x_ref, a_ref, b_ref, f_ref, pre_ref, post_ref, wgt_ref, wut_ref, wd_ref,
                  dx_ref, da_ref, db_ref, hm_ref, df_ref, dpre_ref, dpost_ref):
    dh = dh_ref[...]
    df, dpost = _rms_bwd(f_ref[...], post_ref[...], 0.5 * dh)
    dfb = df.astype(BF16)
    df_ref[...] = dfb
    dhmid = _dot_nt(dfb, wd_ref[...])
    a = a_ref[...].astype(F32)
    b = b_ref[...].astype(F32)
    sig = _sigmoid(a)
    sa = a * sig
    hm_ref[...] = (sa * b).astype(BF16)
    dab = (dhmid * b * sig * (1.0 + a * (1.0 - sig))).astype(BF16)
    dbb = (dhmid * sa).astype(BF16)
    da_ref[...] = dab
    db_ref[...] = dbb
    dn = _dot(dab, wgt_ref[...]) + _dot(dbb, wut_ref[...])
    dxn, dpre = _rms_bwd(x_ref[...], pre_ref[...], dn)
    dx_ref[...] = dh + dxn
    _acc(dpre_ref, dpre)
    _acc(dpost_ref, dpost)


def _ffn_bwd(dh, x, a, b, f, pre_g, post_g, wgt, wut, wd, name):
    s, d = x.shape
    ff = wgt.shape[0]
    return _row_call(_ffn_bwd_body, name, s, [dh, x, a, b, f], [pre_g, post_g, wgt, wut, wd],
                     [(d, F32), (ff, BF16), (ff, BF16), (ff, BF16), (d, BF16)],
                     [((1, d), F32), ((1, d), F32)])


def _inproj_fwd_body(h_ref, g_ref, wt_ref, n_ref, uv_ref, qkv_ref):
    n, _ = _rms(h_ref[...], g_ref[...])
    nb = n.astype(BF16)
    n_ref[...] = nb
    proj = _dot_nt(nb, wt_ref[...])
    nuv = uv_ref.shape[1]
    uv_ref[...] = proj[:, :nuv]
    qkv_ref[...] = proj[:, nuv:].astype(BF16)


def _inproj_fwd(h, g, w_in_t):
    s, d = h.shape
    sgu_w = SGU_GROUPS * GROUP_DIM
    sb_w = SB_HEADS * SB_HEAD_DIM
    return _row_call(_inproj_fwd_body, "inproj_fwd", s, [h], [g, w_in_t],
                     [(d, BF16), (2 * sgu_w, F32), (3 * sb_w, BF16)], [])


def _inproj_bwd_body(dh_ref, dproj_ref, h_ref, g_ref, wt_ref, dhout_ref, dg_ref):
    dn = _dot(dproj_ref[...], wt_ref[...])
    dhn, dg = _rms_bwd(h_ref[...], g_ref[...], dn)
    dhout_ref[...] = dh_ref[...] + dhn
    _acc(dg_ref, dg)


def _inproj_bwd(dh, dproj, h, g, w_in_t):
    s, d = h.shape
    return _row_call(_inproj_bwd_body, "inproj_bwd", s, [dh, dproj, h], [g, w_in_t],
                     [(d, F32)], [((1, d), F32)])


def _causal_w(ws_ref, g):
    row = lax.broadcasted_iota(jnp.int32, (CHUNK, CHUNK), 0)
    col = lax.broadcasted_iota(jnp.int32, (CHUNK, CHUNK), 1)
    return jnp.where(row >= col, ws_ref[g], 0.0), row >= col


def _group_norm(v):
    mu = jnp.mean(v, axis=-1, keepdims=True)
    d = v - mu
    rstd = lax.rsqrt(jnp.mean(d * d, axis=-1, keepdims=True) + EPS)
    return d * rstd, rstd


def _sgu_fwd_body(uv_ref, ng_ref, nb_ref, ws_ref, bs_ref, out_ref):
    width = SGU_GROUPS * GROUP_DIM
    for c in range(uv_ref.shape[0] // CHUNK):
        rows = pl.ds(c * CHUNK, CHUNK)
        for g in range(SGU_GROUPS):
            lanes = pl.ds(g * GROUP_DIM, GROUP_DIM)
            u = _gelu(uv_ref[rows, lanes])
            v = _gelu(uv_ref[rows, pl.ds(width + g * GROUP_DIM, GROUP_DIM)])
            vhat, _ = _group_norm(v)
            vn = vhat * ng_ref[:, lanes] + nb_ref[:, lanes]
            w, _ = _causal_w(ws_ref, g)
            mixed = _dot(w, vn) + bs_ref[g]
            out_ref[rows, lanes] = u * mixed


def _sgu_fwd(uv_pre, ng, nb, ws, bs):
    s = uv_pre.shape[0]
    return _row_call(_sgu_fwd_body, "sgu_fwd", s, [uv_pre], [ng, nb, ws, bs],
                     [(SGU_GROUPS * GROUP_DIM, F32)], [])[0]


def _sgu_bwd_body(uv_ref, do_ref, ng_ref, nb_ref, ws_ref, bs_ref,
                  duv_ref, dws_ref, dbs_ref, dng_ref, dnb_ref):
    width = SGU_GROUPS * GROUP_DIM

    @pl.when(pl.program_id(0) == 0)
    def _():
        dws_ref[...] = jnp.zeros_like(dws_ref)
        dbs_ref[...] = jnp.zeros_like(dbs_ref)
        dng_ref[...] = jnp.zeros_like(dng_ref)
        dnb_ref[...] = jnp.zeros_like(dnb_ref)

    for c in range(uv_ref.shape[0] // CHUNK):
        rows = pl.ds(c * CHUNK, CHUNK)
        for g in range(SGU_GROUPS):
            lanes = pl.ds(g * GROUP_DIM, GROUP_DIM)
            vlanes = pl.ds(width + g * GROUP_DIM, GROUP_DIM)
            u_pre = uv_ref[rows, lanes]
            v_pre = uv_ref[rows, vlanes]
            u = _gelu(u_pre)
            v = _gelu(v_pre)
            vhat, rstd = _group_norm(v)
            gain = ng_ref[:, lanes]
            vn = vhat * gain + nb_ref[:, lanes]
            w, causal = _causal_w(ws_ref, g)
            mixed = _dot(w, vn) + bs_ref[g]
            dout = do_ref[rows, lanes]
            du = dout * mixed
            dmixed = dout * u
            dbs_ref[g] += jnp.sum(dmixed, axis=1, keepdims=True)
            dws_ref[g] += jnp.where(causal, _dot_nt(dmixed, vn), 0.0)
            dvn = _dot_tn(w, dmixed)
            dng_ref[:, lanes] += jnp.sum(dvn * vhat, axis=0, keepdims=True)
            dnb_ref[:, lanes] += jnp.sum(dvn, axis=0, keepdims=True)
            dvh = dvn * gain
            dv = rstd * (dvh - jnp.mean(dvh, axis=-1, keepdims=True)
                         - vhat * jnp.mean(dvh * vhat, axis=-1, keepdims=True))
            duv_ref[rows, lanes] = (du * _gelu_grad(u_pre)).astype(BF16)
            duv_ref[rows, vlanes] = (dv * _gelu_grad(v_pre)).astype(BF16)


def _sgu_bwd(uv_pre, dout_a, ng, nb, ws, bs):
    s = uv_pre.shape[0]
    width = SGU_GROUPS * GROUP_DIM
    return _row_call(_sgu_bwd_body, "sgu_bwd", s, [uv_pre, dout_a], [ng, nb, ws, bs],
                     [(2 * width, BF16)],
                     [(ws.shape, F32), (bs.shape, F32), ((1, width), F32), ((1, width), F32)])


def _mesh_place():
    return lax.axis_index("x"), lax.axis_index("y"), lax.axis_index("c")


def _other_chips(mx, my):
    return [(1 - mx, my), (mx, 1 - my), (1 - mx, 1 - my)]


_ANY = pl.BlockSpec(memory_space=pl.ANY)
AG_SEMS = 7
_AG_SCRATCH = [pltpu.SemaphoreType.DMA((AG_SEMS,)), pltpu.SemaphoreType.DMA((AG_SEMS,)),
               pltpu.SemaphoreType.DMA(())]


def _gather_phases(x_ref, out_ref, send_sems, recv_sems, local_sem):
    mx, my, mc = _mesh_place()
    me, sibling = (mx, my, mc), (mx, my, 1 - mc)
    chips = _other_chips(mx, my)

    def slot(px, py, pc):
        return out_ref.at[4 * px + 2 * py + pc]

    def copy(k, block, to, src=None):
        return pltpu.make_async_remote_copy(
            src_ref=slot(*block) if src is None else src, dst_ref=slot(*block),
            send_sem=send_sems.at[k], recv_sem=recv_sems.at[k],
            device_id=to, device_id_type=MESH)

    mine = pltpu.make_async_copy(x_ref, slot(*me), local_sem)
    first = [copy(0, me, sibling, src=x_ref)]
    first += [copy(1 + j, me, (*chip, mc), src=x_ref) for j, chip in enumerate(chips)]
    passed = [copy(4 + j, (*chip, mc), sibling) for j, chip in enumerate(chips)]

    def start():
        mine.start()
        for cp in first:
            cp.start()

    def forward():
        for j, chip in enumerate(chips):
            copy(1 + j, (*chip, mc), me).wait_recv()
            passed[j].start()

    def finish():
        copy(0, sibling, me).wait_recv()
        for j, chip in enumerate(chips):
            copy(4 + j, (*chip, 1 - mc), me).wait_recv()
        for cp in first + passed:
            cp.wait_send()
        mine.wait()

    return start, forward, finish


def _all_gather(x, name):
    r, c = x.shape

    def body(x_ref, out_ref, send_sems, recv_sems, local_sem):
        start, forward, finish = _gather_phases(x_ref, out_ref, send_sems, recv_sems, local_sem)
        start()
        forward()
        finish()

    return pl.pallas_call(
        body,
        name=name,
        out_shape=jax.ShapeDtypeStruct((N_DEV, r, c), x.dtype),
        in_specs=[_ANY],
        out_specs=_ANY,
        scratch_shapes=list(_AG_SCRATCH),
    )(x)


def _scatter_phases(p_ref, out_ref, send_sems, recv_sems, local_sem):
    mx, my, mc = _mesh_place()
    me = 4 * mx + 2 * my + mc
    copies = []
    for k in range(1, N_DEV):
        tx, ty, tc = mx ^ ((k >> 2) & 1), my ^ ((k >> 1) & 1), mc ^ (k & 1)
        copies.append(pltpu.make_async_remote_copy(
            src_ref=p_ref.at[4 * tx + 2 * ty + tc], dst_ref=out_ref.at[me],
            send_sem=send_sems.at[k - 1], recv_sem=recv_sems.at[k - 1],
            device_id=(tx, ty, tc), device_id_type=MESH))
    mine = pltpu.make_async_copy(p_ref.at[me], out_ref.at[me], local_sem)

    def start():
        mine.start()
        for cp in copies:
            cp.start()

    def finish():
        for cp in copies:
            cp.wait()
        mine.wait()

    return start, finish


SB_DEAD = -105.0
HEADS_PER_TILE = 2
TILES_PER_STEP = 2
HEADS_PER_STEP = HEADS_PER_TILE * TILES_PER_STEP
STEP_LANES = TILES_PER_STEP * HEADS_PER_TILE * SB_HEAD_DIM
TILE_LANES = HEADS_PER_TILE * SB_HEAD_DIM
STACK_ROWS = HEADS_PER_STEP * Q_BLOCK
TILE_ROWS = HEADS_PER_TILE * Q_BLOCK


def _stack_heads(x):
    lane = lax.broadcasted_iota(jnp.int32, x.shape, 1)
    zero = jnp.zeros_like(x)
    return jnp.concatenate(
        [jnp.where(lane // SB_HEAD_DIM == h, x, zero) for h in range(HEADS_PER_STEP)], axis=0)


def _unstack_tile(x):
    first = lax.broadcasted_iota(jnp.int32, (Q_BLOCK, TILE_LANES), 1) < SB_HEAD_DIM
    return jnp.where(first, x[:Q_BLOCK], x[Q_BLOCK:])


def _sb_logs(qs, k, diagonal):
    z = _dot_nt(qs, k) * (SB_HEAD_DIM ** -0.5)
    sp = jnp.log1p(jnp.exp(-jnp.abs(z)))
    log_beta = jnp.minimum(z, 0.0) - sp
    log_1m_raw = -jnp.maximum(z, 0.0) - sp
    if not diagonal:
        return None, log_beta, log_1m_raw, log_1m_raw
    row = lax.broadcasted_iota(jnp.int32, z.shape, 0)
    col = lax.broadcasted_iota(jnp.int32, z.shape, 1)
    strict = col < jnp.bitwise_and(row, Q_BLOCK - 1)
    return strict, log_beta, log_1m_raw, jnp.where(strict, log_1m_raw, 0.0)


def _masked(strict, x):
    return x if strict is None else jnp.where(strict, x, 0.0)


def _key_sums(x, pick):
    hi, lo = _split_bf16(x)
    both = jnp.dot(jnp.concatenate([hi, lo], axis=0), pick, preferred_element_type=F32)
    return both[:x.shape[0]] + both[x.shape[0]:]


def _key_order():
    row = lax.broadcasted_iota(jnp.int32, (Q_BLOCK, Q_BLOCK), 0)
    col = lax.broadcasted_iota(jnp.int32, (Q_BLOCK, Q_BLOCK), 1)
    return row, col


def _sb_fwd_body(q_ref, k_ref, v_ref, shard_ref, o_ref, tot_ref, cnt_ref, gathered_ref,
                 acc_ref, send_sems, recv_sems, local_sem):
    grp, qb = pl.program_id(0), pl.program_id(1)
    last_grp, last_qb = pl.num_programs(0) - 1, pl.num_programs(1) - 1
    ag_start, ag_forward, ag_finish = _gather_phases(shard_ref, gathered_ref, send_sems, recv_sems, local_sem)
    pl.when(jnp.logical_and(grp == 0, qb == 0))(ag_start)

    qs = _stack_heads(q_ref[...])
    row, col = _key_order()
    later = (row > col).astype(BF16)

    def block(i, c, diagonal):
        rows = pl.ds(pl.multiple_of((qb - i) * Q_BLOCK, Q_BLOCK), Q_BLOCK)
        strict, log_beta, _, log_1m = _sb_logs(qs, k_ref[rows, :], diagonal)
        a = _masked(strict, jnp.exp(log_beta + _key_sums(log_1m, later) + c)).astype(BF16)
        for t in range(TILES_PER_STEP):
            part = jnp.dot(a[t * TILE_ROWS:(t + 1) * TILE_ROWS], v_ref[rows, t * TILE_LANES:(t + 1) * TILE_LANES],
                           preferred_element_type=F32)
            if diagonal:
                acc_ref[t] = part
            else:
                acc_ref[t] += part
        return c + jnp.sum(log_1m, axis=1, keepdims=True)

    c = block(0, jnp.zeros((STACK_ROWS, 1), F32), True)

    def alive(carry):
        i, c = carry
        return jnp.logical_and(i <= qb, jnp.max(c) > SB_DEAD)

    def step(carry):
        i, c = carry
        return i + 1, block(i, c, False)

    n, c = lax.while_loop(alive, step, (jnp.int32(1), c))
    for t in range(TILES_PER_STEP):
        o_ref[:, t * TILE_LANES:(t + 1) * TILE_LANES] = _unstack_tile(acc_ref[t])
    for h in range(HEADS_PER_STEP):
        tot_ref[h] = c[h * Q_BLOCK:(h + 1) * Q_BLOCK]
    cnt_ref[grp, qb] = n.astype(F32)
    pl.when(jnp.logical_and(grp == last_grp // 2 + 1, qb == 0))(ag_forward)
    pl.when(jnp.logical_and(grp == last_grp, qb == last_qb))(ag_finish)


def _sb_fwd(qkv, shard):
    s = qkv.shape[0]
    groups = SB_HEADS // HEADS_PER_STEP
    nq = s // Q_BLOCK
    return pl.pallas_call(
        functools.partial(_sb_fwd_body),
        name="sb_fwd",
        grid=(groups, nq),
        in_specs=[pl.BlockSpec((Q_BLOCK, STEP_LANES), lambda g, i: (i, g)),
                  pl.BlockSpec((s, STEP_LANES), lambda g, i: (0, groups + g)),
                  pl.BlockSpec((s, STEP_LANES), lambda g, i: (0, 2 * groups + g)),
                  _ANY],
        out_specs=[pl.BlockSpec((Q_BLOCK, STEP_LANES), lambda g, i: (i, g)),
                   pl.BlockSpec((HEADS_PER_STEP, Q_BLOCK, 1), lambda g, i: (g, i, 0)),
                   pl.BlockSpec(memory_space=pltpu.SMEM),
                   _ANY],
        out_shape=[jax.ShapeDtypeStruct((s, SB_HEADS * SB_HEAD_DIM), F32),
                   jax.ShapeDtypeStruct((SB_HEADS, s, 1), F32),
                   jax.ShapeDtypeStruct((groups, nq), F32),
                   jax.ShapeDtypeStruct((N_DEV,) + shard.shape, shard.dtype)],
        scratch_shapes=[pltpu.VMEM((TILES_PER_STEP, TILE_ROWS, TILE_LANES), F32)] + list(_AG_SCRATCH),
        compiler_params=pltpu.CompilerParams(
            dimension_semantics=("arbitrary", "arbitrary"), vmem_limit_bytes=VMEM_LIMIT_V7X),
    )(qkv, qkv, qkv, shard)


def _sb_bwd_body(cnt_ref, q_ref, k_ref, v_ref, tot_ref, do_ref, part_ref, dq_ref, dk_ref, dv_ref, recv_ref,
                 acc_ref, send_sems, recv_sems, local_sem):
    grp, qb = pl.program_id(0), pl.program_id(1)
    last_grp, last_qb = pl.num_programs(0) - 1, pl.num_programs(1) - 1
    rs_start, rs_finish = _scatter_phases(part_ref, recv_ref, send_sems, recv_sems, local_sem)
    pl.when(jnp.logical_and(grp == 0, qb == 0))(rs_start)

    @pl.when(qb == 0)
    def _():
        dk_ref[...] = jnp.zeros_like(dk_ref)
        dv_ref[...] = jnp.zeros_like(dv_ref)

    acc_ref[...] = jnp.zeros_like(acc_ref)
    qs = _stack_heads(q_ref[...])
    dos = _stack_heads(do_ref[...].astype(BF16))
    tot = jnp.concatenate([tot_ref[h] for h in range(HEADS_PER_STEP)], axis=0)
    row, col = _key_order()
    up_to = (row <= col).astype(BF16)
    earlier = (row < col).astype(BF16)
    scale = SB_HEAD_DIM ** -0.5
    n = jnp.clip(cnt_ref[grp, qb].astype(jnp.int32), 1, qb + 1)

    def block(kb, c, ce, diagonal):
        rows = pl.ds(pl.multiple_of(kb * Q_BLOCK, Q_BLOCK), Q_BLOCK)
        k = k_ref[rows, :]
        strict, log_beta, log_1m_raw, log_1m = _sb_logs(qs, k, diagonal)
        suffix = tot - c - _key_sums(log_1m, up_to)
        a = _masked(strict, jnp.exp(log_beta + suffix))
        de = _dot_nt(dos, v_ref[rows, :]) * a
        before = ce + _key_sums(de, earlier)
        dz = _masked(strict, de * jnp.exp(log_1m_raw) - before * jnp.exp(log_beta)).astype(BF16)
        for t in range(TILES_PER_STEP):
            acc_ref[t] += jnp.dot(dz[t * TILE_ROWS:(t + 1) * TILE_ROWS], k[:, t * TILE_LANES:(t + 1) * TILE_LANES],
                                  preferred_element_type=F32)
        dk_ref[rows, :] += _dot_tn(dz, qs) * scale
        dv_ref[rows, :] += _dot_tn(a, dos)
        return c + jnp.sum(log_1m, axis=1, keepdims=True), ce + jnp.sum(de, axis=1, keepdims=True)

    def step(i, carry):
        return block(qb - n + 1 + i, *carry, False)

    zc = jnp.zeros((STACK_ROWS, 1), F32)
    c, ce = lax.fori_loop(0, n - 1, step, (zc, zc))
    block(qb, c, ce, True)
    for t in range(TILES_PER_STEP):
        dq_ref[:, t * TILE_LANES:(t + 1) * TILE_LANES] = _unstack_tile(acc_ref[t]) * scale
    pl.when(jnp.logical_and(grp == last_grp, qb == last_qb))(rs_finish)


def _sb_bwd(cnt, qkv, tot, dout_b, parts):
    s = qkv.shape[0]
    groups = SB_HEADS // HEADS_PER_STEP
    return pl.pallas_call(
        functools.partial(_sb_bwd_body),
        name="sb_bwd",
        grid=(groups, s // Q_BLOCK),
        in_specs=[pl.BlockSpec(memory_space=pltpu.SMEM),
                  pl.BlockSpec((Q_BLOCK, STEP_LANES), lambda g, i: (i, g)),
                  pl.BlockSpec((s, STEP_LANES), lambda g, i: (0, groups + g)),
                  pl.BlockSpec((s, STEP_LANES), lambda g, i: (0, 2 * groups + g)),
                  pl.BlockSpec((HEADS_PER_STEP, Q_BLOCK, 1), lambda g, i: (g, i, 0)),
                  pl.BlockSpec((Q_BLOCK, STEP_LANES), lambda g, i: (i, g)),
                  _ANY],
        out_specs=[pl.BlockSpec((Q_BLOCK, STEP_LANES), lambda g, i: (i, g)),
                   pl.BlockSpec((s, STEP_LANES), lambda g, i: (0, g)),
                   pl.BlockSpec((s, STEP_LANES), lambda g, i: (0, g)),
                   _ANY],
        out_shape=[jax.ShapeDtypeStruct((s, SB_HEADS * SB_HEAD_DIM), F32)] * 3
        + [jax.ShapeDtypeStruct(parts.shape, parts.dtype)],
        scratch_shapes=[pltpu.VMEM((TILES_PER_STEP, TILE_ROWS, TILE_LANES), F32)] + list(_AG_SCRATCH),
        compiler_params=pltpu.CompilerParams(
            dimension_semantics=("arbitrary", "arbitrary"), vmem_limit_bytes=VMEM_LIMIT_V7X),
    )(cnt, qkv, qkv, qkv, tot, dout_b, parts)


def _outproj_fwd_body(oa_ref, ob_ref, h_ref, ga_ref, gb_ref, gpost_ref, w_ref,
                      merged_ref, mo_ref, hout_ref):
    half = oa_ref.shape[1]
    ma, _ = _rms(oa_ref[...], ga_ref[...])
    mb, _ = _rms(ob_ref[...], gb_ref[...])
    mab = ma.astype(BF16)
    mbb = mb.astype(BF16)
    merged_ref[:, :half] = mab
    merged_ref[:, half:] = mbb
    mo = (jnp.dot(mab, w_ref[:half, :], preferred_element_type=F32)
          + jnp.dot(mbb, w_ref[half:, :], preferred_element_type=F32))
    mo_ref[...] = mo
    y, _ = _rms(mo, gpost_ref[...])
    hout_ref[...] = h_ref[...] + y


def _outproj_fwd(out_a, out_b, h, ga, gb, gpost, w_out):
    s, d = h.shape
    return _row_call(_outproj_fwd_body, "outproj_fwd", s, [out_a, out_b, h], [ga, gb, gpost, w_out],
                     [(d, BF16), (d, F32), (d, F32)], [])


def _outproj_bwd_body(dh_ref, mo_ref, oa_ref, ob_ref, ga_ref, gb_ref, gpost_ref, w_ref,
                      dmo_ref, doa_ref, dob_ref, dga_ref, dgb_ref, dgpost_ref):
    half = oa_ref.shape[1]
    dmo, dgpost = _rms_bwd(mo_ref[...], gpost_ref[...], dh_ref[...])
    dmob = dmo.astype(BF16)
    dmo_ref[...] = dmob
    dma = _dot_nt(dmob, w_ref[:half, :])
    dmb = _dot_nt(dmob, w_ref[half:, :])
    doa, dga = _rms_bwd(oa_ref[...], ga_ref[...], dma)
    dob, dgb = _rms_bwd(ob_ref[...], gb_ref[...], dmb)
    doa_ref[...] = doa
    dob_ref[...] = dob
    _acc(dga_ref, dga)
    _acc(dgb_ref, dgb)
    _acc(dgpost_ref, dgpost)


def _outproj_bwd(dh, mo, out_a, out_b, ga, gb, gpost, w_out):
    s, d = dh.shape
    half = out_a.shape[1]
    return _row_call(_outproj_bwd_body, "outproj_bwd", s, [dh, mo, out_a, out_b], [ga, gb, gpost, w_out],
                     [(d, BF16), (half, F32), (half, F32)],
                     [((1, half), F32), ((1, half), F32), ((1, d), F32)])


def _kv_fwd_body(mem_ref, g_ref, wt_ref, memn_ref, kv_ref):
    n, _ = _rms(mem_ref[...], g_ref[...])
    nb = n.astype(BF16)
    memn_ref[...] = nb
    kv_ref[...] = _dot_nt(nb, wt_ref[...]).astype(BF16)


def _kv_fwd(mem, g, w_kv_t):
    m, d = mem.shape
    return _row_call(_kv_fwd_body, "kv_fwd", m, [mem], [g, w_kv_t], [(d, BF16), (w_kv_t.shape[0], BF16)], [])


def _kv_bwd_body(dkv_ref, mem_ref, memn_ref, g_ref, wt_ref, dwt_ref, dg_ref):
    dkvb = dkv_ref[...].astype(BF16)
    dwt_ref[...] = _dot_tn(dkvb, memn_ref[...]).astype(BF16)
    dmemn = _dot(dkvb, wt_ref[...])
    _, dg = _rms_bwd(mem_ref[...], g_ref[...], dmemn)
    dg_ref[...] = dg


def _kv_bwd(dkv, mem, memn, g, w_kv_t):
    m, d = mem.shape
    return pl.pallas_call(
        functools.partial(_kv_bwd_body),
        name="kv_bwd",
        out_shape=[jax.ShapeDtypeStruct(w_kv_t.shape, BF16), jax.ShapeDtypeStruct((1, d), F32)],
        compiler_params=pltpu.CompilerParams(vmem_limit_bytes=VMEM_LIMIT_V7X),
    )(dkv, mem, memn, g, w_kv_t)


def _xa_fwd_body(h_ref, gpre_ref, gpost_ref, wq_ref, wo_ref, kv_ref,
                 n_ref, q_ref, o_ref, c_ref, hout_ref):
    h = h_ref[...]
    d = h.shape[1]
    n, _ = _rms(h, gpre_ref[...])
    nb = n.astype(BF16)
    n_ref[...] = nb
    qb = jnp.dot(nb, wq_ref[...], preferred_element_type=F32).astype(BF16)
    q_ref[...] = qb
    for hd in range(XA_HEADS):
        lanes = slice(hd * XA_HEAD_DIM, (hd + 1) * XA_HEAD_DIM)
        k = kv_ref[:, lanes]
        v = kv_ref[:, d + hd * XA_HEAD_DIM:d + (hd + 1) * XA_HEAD_DIM]
        logits = _dot_nt(qb[:, lanes], k) * (XA_HEAD_DIM ** -0.5)
        e = jnp.exp(logits - jnp.max(logits, axis=-1, keepdims=True))
        p = e / jnp.sum(e, axis=-1, keepdims=True)
        o_ref[:, lanes] = jnp.dot(p.astype(BF16), v, preferred_element_type=F32).astype(BF16)
    c = jnp.dot(o_ref[...], wo_ref[...], preferred_element_type=F32)
    c_ref[...] = c
    y, _ = _rms(c, gpost_ref[...])
    hout_ref[...] = h + y


def _xa_fwd(h, gpre, gpost, wq, wo, kv):
    s, d = h.shape
    return _row_call(_xa_fwd_body, "xa_fwd", s, [h], [gpre, gpost, wq, wo, kv],
                     [(d, BF16), (d, BF16), (d, BF16), (d, F32), (d, F32)], [])


def _xa_bwd_body(dh_ref, h_ref, c_ref, q_ref, o_ref, gpre_ref, gpost_ref, wq_ref, wo_ref, kv_ref,
                 dhout_ref, dc_ref, dq_ref, dkv_ref, dgpre_ref, dgpost_ref):
    dh = dh_ref[...]
    d = dh.shape[1]
    scale = XA_HEAD_DIM ** -0.5
    dc, dgpost = _rms_bwd(c_ref[...], gpost_ref[...], dh)
    dcb = dc.astype(BF16)
    dc_ref[...] = dcb
    dob = _dot_nt(dcb, wo_ref[...]).astype(BF16)

    @pl.when(pl.program_id(0) == 0)
    def _():
        dkv_ref[...] = jnp.zeros_like(dkv_ref)

    for hd in range(XA_HEADS):
        lanes = slice(hd * XA_HEAD_DIM, (hd + 1) * XA_HEAD_DIM)
        vlanes = slice(d + hd * XA_HEAD_DIM, d + (hd + 1) * XA_HEAD_DIM)
        qh = q_ref[:, lanes]
        k = kv_ref[:, lanes]
        v = kv_ref[:, vlanes]
        logits = _dot_nt(qh, k) * scale
        e = jnp.exp(logits - jnp.max(logits, axis=-1, keepdims=True))
        p = e / jnp.sum(e, axis=-1, keepdims=True)
        doh = dob[:, lanes]
        dp = _dot_nt(doh, v)
        dl = (p * (dp - jnp.sum(dp * p, axis=-1, keepdims=True)) * scale).astype(BF16)
        dq_ref[:, lanes] = jnp.dot(dl, k, preferred_element_type=F32).astype(BF16)
        dkv_ref[:, lanes] += _dot_tn(dl, qh)
        dkv_ref[:, vlanes] += _dot_tn(p, doh)
    dn = _dot_nt(dq_ref[...], wq_ref[...])
    dhn, dgpre = _rms_bwd(h_ref[...], gpre_ref[...], dn)
    dhout_ref[...] = dh + dhn
    _acc(dgpre_ref, dgpre)
    _acc(dgpost_ref, dgpost)


def _xa_bwd(dh, h, c, q, o, gpre, gpost, wq, wo, kv):
    s, d = h.shape
    return _row_call(_xa_bwd_body, "xa_bwd", s, [dh, h, c, q, o], [gpre, gpost, wq, wo, kv],
                     [(d, F32), (d, BF16), (d, BF16)],
                     [(kv.shape, F32), ((1, d), F32), ((1, d), F32)])


def _final_body(h_ref, t_ref, g_ref, dh_ref, loss_ref, dg_ref):
    h = h_ref[...]
    d = h.shape[1]
    y, _ = _rms(h, g_ref[...])
    err = y - t_ref[...]
    part = (0.5 / d) * jnp.sum(jnp.sum(err * err, axis=1, keepdims=True), axis=0, keepdims=True)
    dh, dg = _rms_bwd(h, g_ref[...], err * (1.0 / d))
    dh_ref[...] = dh
    _acc(loss_ref, part)
    _acc(dg_ref, dg)


def _final(h, g, target):
    s, d = h.shape
    return _row_call(_final_body, "final_loss", s, [h, target], [g],
                     [(d, F32)], [((1, 1), F32), ((1, d), F32)])


def _largest_tile(n, cap):
    best = 128
    for t in range(128, cap + 1, 128):
        if n % t == 0:
            best = t
    return best


def _mm_tn(a, bs, name):
    s, k = a.shape
    n = bs[0].shape[1]
    nb = len(bs)
    ts = min(512, s)
    tk = _largest_tile(k, 1536)
    tn = _largest_tile(n, 1536 // nb)

    steps = s // ts

    def body(a_ref, *refs):
        b_refs, o_refs, acc_refs = refs[:nb], refs[nb:2 * nb], refs[2 * nb:]
        at = a_ref[...]
        t = pl.program_id(2)

        @pl.when(t == 0)
        def _():
            for acc_ref in acc_refs:
                acc_ref[...] = jnp.zeros_like(acc_ref)

        for b_ref, acc_ref in zip(b_refs, acc_refs):
            acc_ref[...] += _dot_tn(at, b_ref[...])

        @pl.when(t == steps - 1)
        def _():
            for o_ref, acc_ref in zip(o_refs, acc_refs):
                o_ref[...] = acc_ref[...].astype(BF16)

    return pl.pallas_call(
        body,
        name=name,
        grid=(k // tk, n // tn, steps),
        in_specs=[pl.BlockSpec((ts, tk), lambda i, j, t: (t, i))]
        + [pl.BlockSpec((ts, tn), lambda i, j, t: (t, j))] * nb,
        out_specs=[pl.BlockSpec((tk, tn), lambda i, j, t: (i, j))] * nb,
        out_shape=[jax.ShapeDtypeStruct((k, n), BF16)] * nb,
        scratch_shapes=[pltpu.VMEM((tk, tn), F32)] * nb,
        compiler_params=pltpu.CompilerParams(
            dimension_semantics=("arbitrary", "arbitrary", "arbitrary"),
            vmem_limit_bytes=VMEM_LIMIT_V7X),
    )(a, *bs)


_SMALL_SHAPES = {
    "sgu_norm_g": (1, SGU_GROUPS * GROUP_DIM),
    "sgu_norm_b": (1, SGU_GROUPS * GROUP_DIM),
    "sgu_w_s": (SGU_GROUPS, CHUNK, CHUNK),
    "sgu_b_s": (SGU_GROUPS, CHUNK, 1),
}


def _small_views(small):
    return {n: v.reshape(_SMALL_SHAPES.get(n, v.shape)) for n, v in small.items()}


def _small_unviews(views, like):
    return {n: v.reshape(like[n].shape) for n, v in views.items()}


def _unpack_rows(gathered, names, shard_rows):
    out, off = {}, 0
    for n in names:
        rows = shard_rows[n]
        out[n] = gathered[:, off:off + rows, :].reshape(N_DEV * rows, gathered.shape[2])
        off += rows
    return out


def _row_tile(r, cap):
    best = 16
    for t in range(16, cap + 1, 16):
        if r % t == 0:
            best = t
    return best


def _sum_received(received):
    _, r, c = received.shape
    tr = _row_tile(r, 1024)

    def body(rc_ref, g_ref):
        g = rc_ref[0].astype(F32)
        for t in range(1, N_DEV):
            g = g + rc_ref[t].astype(F32)
        g_ref[...] = g

    return pl.pallas_call(
        body, name="rs_sum_received", grid=(r // tr,),
        in_specs=[pl.BlockSpec((N_DEV, tr, c), lambda i: (0, i, 0))],
        out_specs=pl.BlockSpec((tr, c), lambda i: (i, 0)),
        out_shape=jax.ShapeDtypeStruct((r, c), F32),
    )(received)


def _local_step(x, mem, target, small, big, late_shard, late_names, shard_rows):
    sm, w = small, dict(big)
    d_model = x.shape[1]
    h1, n1, a1, b1, f1 = _ffn_fwd(x, sm["ffn1_pre_g"], sm["ffn1_post_g"],
                                  w["ffn1_w_gate"], w["ffn1_w_up"], w["ffn1_w_down"], "ffn1_fwd")
    n2, uv_pre, qkv = _inproj_fwd(h1, sm["mix_pre_g"], w["w_in"])
    out_a = _sgu_fwd(uv_pre, sm["sgu_norm_g"], sm["sgu_norm_b"], sm["sgu_w_s"], sm["sgu_b_s"])
    out_b, tot, cnt, late = _sb_fwd(qkv, late_shard)
    w.update(_unpack_rows(late, late_names, shard_rows))
    merged, mo, h2 = _outproj_fwd(out_a, out_b, h1, sm["sgu_out_g"], sm["sb_out_g"],
                                  sm["mix_post_g"], w["w_out"])
    memn, kv = _kv_fwd(mem, sm["mem_norm_g"], w["xa_w_kv"])
    n3, qx, ox, cx, h3 = _xa_fwd(h2, sm["xa_pre_g"], sm["xa_post_g"], w["xa_w_q"], w["xa_w_o"], kv)
    h4, n4, a2, b2, f2 = _ffn_fwd(h3, sm["ffn2_pre_g"], sm["ffn2_post_g"],
                                  w["ffn2_w_gate"], w["ffn2_w_up"], w["ffn2_w_down"], "ffn2_fwd")
    dh4, loss, dg_final = _final(h4, sm["final_norm_g"], target)

    gs, gw = {"final_norm_g": dg_final}, {}
    dh3, da2, db2, hm2, df2, gs["ffn2_pre_g"], gs["ffn2_post_g"] = _ffn_bwd(
        dh4, h3, a2, b2, f2, sm["ffn2_pre_g"], sm["ffn2_post_g"],
        w["ffn2_w_gate"], w["ffn2_w_up"], w["ffn2_w_down"], "ffn2_bwd")
    gw["ffn2_w_gate"], = _mm_tn(da2, [n4], "ffn2_dw_gate")
    gw["ffn2_w_up"], = _mm_tn(db2, [n4], "ffn2_dw_up")
    gw["ffn2_w_down"], = _mm_tn(hm2, [df2], "ffn2_dw_down")

    dh2, dc, dqx, dkv, gs["xa_pre_g"], gs["xa_post_g"] = _xa_bwd(
        dh3, h2, cx, qx, ox, sm["xa_pre_g"], sm["xa_post_g"], w["xa_w_q"], w["xa_w_o"], kv)
    gw["xa_w_o"], = _mm_tn(ox, [dc], "xa_dw_o")
    gw["xa_w_q"], = _mm_tn(n3, [dqx], "xa_dw_q")
    gw["xa_w_kv"], gs["mem_norm_g"] = _kv_bwd(dkv, mem, memn, sm["mem_norm_g"], w["xa_w_kv"])

    dmo, dout_a, dout_b, gs["sgu_out_g"], gs["sb_out_g"], gs["mix_post_g"] = _outproj_bwd(
        dh2, mo, out_a, out_b, sm["sgu_out_g"], sm["sb_out_g"], sm["mix_post_g"], w["w_out"])
    gw["w_out"], = _mm_tn(merged, [dmo], "mix_dw_out")
    parts = jnp.concatenate(
        [gw.pop(n).reshape(N_DEV, -1, d_model).astype(BF16) for n in late_names], axis=1)
    dq, dk, dv, received = _sb_bwd(cnt, qkv, tot, dout_b, parts)
    duv, gs["sgu_w_s"], gs["sgu_b_s"], gs["sgu_norm_g"], gs["sgu_norm_b"] = _sgu_bwd(
        uv_pre, dout_a, sm["sgu_norm_g"], sm["sgu_norm_b"], sm["sgu_w_s"], sm["sgu_b_s"])
    dproj = jnp.concatenate([duv] + [t.astype(BF16) for t in (dq, dk, dv)], axis=1)
    dh1, gs["mix_pre_g"] = _inproj_bwd(dh2, dproj, h1, sm["mix_pre_g"], w["w_in"])
    gw["w_in"], = _mm_tn(dproj, [n2], "mix_dw_in")

    dx, da1, db1, hm1, df1, gs["ffn1_pre_g"], gs["ffn1_post_g"] = _ffn_bwd(
        dh1, x, a1, b1, f1, sm["ffn1_pre_g"], sm["ffn1_post_g"],
        w["ffn1_w_gate"], w["ffn1_w_up"], w["ffn1_w_down"], "ffn1_bwd")
    gw["ffn1_w_gate"], = _mm_tn(da1, [n1], "ffn1_dw_gate")
    gw["ffn1_w_up"], = _mm_tn(db1, [n1], "ffn1_dw_up")
    gw["ffn1_w_down"], = _mm_tn(hm1, [df1], "ffn1_dw_down")
    return loss, dx, gs, gw, _sum_received(received)


def _pair_exchange(p4):
    nchip, _, r, c = p4.shape

    def body(p_ref, out_ref, send_sems, recv_sems):
        mx, my, mc = _mesh_place()
        copies = [pltpu.make_async_remote_copy(
            src_ref=p_ref.at[j, 1 - mc], dst_ref=out_ref.at[j],
            send_sem=send_sems.at[j], recv_sem=recv_sems.at[j],
            device_id=(mx, my, 1 - mc), device_id_type=MESH) for j in range(nchip)]
        for cp in copies:
            cp.start()
        for cp in copies:
            cp.wait()

    return pl.pallas_call(
        body,
        name="rs_pair_exchange",
        out_shape=jax.ShapeDtypeStruct((nchip, r, c), p4.dtype),
        in_specs=[_ANY],
        out_specs=_ANY,
        scratch_shapes=[pltpu.SemaphoreType.DMA((nchip,)), pltpu.SemaphoreType.DMA((nchip,))],
    )(p4)


def _chip_exchange(q):
    _, r, c = q.shape

    def body(q_ref, out_ref, send_sems, recv_sems):
        mx, my, mc = _mesh_place()
        copies = [pltpu.make_async_remote_copy(
            src_ref=q_ref.at[2 * cx + cy], dst_ref=out_ref.at[k],
            send_sem=send_sems.at[k], recv_sem=recv_sems.at[k],
            device_id=(cx, cy, mc), device_id_type=MESH)
            for k, (cx, cy) in enumerate(_other_chips(mx, my))]
        for cp in copies:
            cp.start()
        for cp in copies:
            cp.wait()

    return pl.pallas_call(
        body,
        name="rs_chip_exchange",
        out_shape=jax.ShapeDtypeStruct((3, r, c), q.dtype),
        in_specs=[_ANY],
        out_specs=_ANY,
        scratch_shapes=[pltpu.SemaphoreType.DMA((3,)), pltpu.SemaphoreType.DMA((3,))],
    )(q)


def _rs_row_tile(r):
    return _row_tile(r, 1024)


def _pair_sum(place, p4, recv_a):
    nchip, _, r, c = p4.shape
    tr = _rs_row_tile(r)

    def body(place_ref, p_ref, a_ref, q_ref):
        q_ref[0] = (p_ref[0, 0].astype(F32) + a_ref[0].astype(F32)).astype(BF16)

    return pl.pallas_call(
        body,
        name="rs_pair_sum",
        grid_spec=pltpu.PrefetchScalarGridSpec(
            num_scalar_prefetch=1,
            grid=(nchip, r // tr),
            in_specs=[pl.BlockSpec((1, 1, tr, c), lambda j, i, pref: (j, pref[0], i, 0)),
                      pl.BlockSpec((1, tr, c), lambda j, i, pref: (j, i, 0))],
            out_specs=pl.BlockSpec((1, tr, c), lambda j, i, pref: (j, i, 0)),
        ),
        out_shape=jax.ShapeDtypeStruct((nchip, r, c), BF16),
    )(place, p4, recv_a)


def _rs_final(place, p4, recv_a, recv_b):
    _, _, r, c = p4.shape
    tr = _rs_row_tile(r)

    def body(place_ref, p_ref, a_ref, b_ref, g_ref):
        g = p_ref[0, 0].astype(F32) + a_ref[0].astype(F32)
        for k in range(3):
            g = g + b_ref[k].astype(F32)
        g_ref[...] = g

    return pl.pallas_call(
        body,
        name="rs_final_sum",
        grid_spec=pltpu.PrefetchScalarGridSpec(
            num_scalar_prefetch=1,
            grid=(r // tr,),
            in_specs=[pl.BlockSpec((1, 1, tr, c), lambda i, pref: (pref[1], pref[0], i, 0)),
                      pl.BlockSpec((1, tr, c), lambda i, pref: (pref[1], i, 0)),
                      pl.BlockSpec((3, tr, c), lambda i, pref: (0, i, 0))],
            out_specs=pl.BlockSpec((tr, c), lambda i, pref: (i, 0)),
        ),
        out_shape=jax.ShapeDtypeStruct((r, c), F32),
    )(place, p4, recv_a, recv_b)


def _adamw_math(w, g, m, v):
    m = ADAM_B1 * m + (1.0 - ADAM_B1) * g
    v = ADAM_B2 * v + (1.0 - ADAM_B2) * (g * g)
    m_hat = m / (1.0 - ADAM_B1 ** ADAM_STEP)
    v_hat = v / (1.0 - ADAM_B2 ** ADAM_STEP)
    delta = -ADAM_LR * (m_hat / (jnp.sqrt(v_hat) + ADAM_EPS) + ADAM_WD * w)
    return delta, m, v


def _adamw(w, g, m, v, name):
    r, c = w.shape
    tr = r if r <= 512 else 256

    def body(w_ref, g_ref, m_ref, v_ref, d_ref, mo_ref, vo_ref):
        d_ref[...], mo_ref[...], vo_ref[...] = _adamw_math(w_ref[...], g_ref[...], m_ref[...], v_ref[...])

    spec = pl.BlockSpec((tr, c), lambda i: (i, 0))
    out = jax.ShapeDtypeStruct((r, c), F32)
    return pl.pallas_call(
        body, name=name, grid=(r // tr,), in_specs=[spec] * 4, out_specs=[spec] * 3,
        out_shape=[out] * 3,
    )(w, g, m, v)


def _small_sum_adamw(gathered, w, m, v):
    _, r, c = gathered.shape

    def body(ga_ref, w_ref, m_ref, v_ref, g_ref, d_ref, mo_ref, vo_ref):
        g = ga_ref[0]
        for k in range(1, N_DEV):
            g = g + ga_ref[k]
        g_ref[...] = g
        d_ref[...], mo_ref[...], vo_ref[...] = _adamw_math(w_ref[...], g, m_ref[...], v_ref[...])

    out = jax.ShapeDtypeStruct((r, c), F32)
    return pl.pallas_call(body, name="small_sum_adamw", out_shape=[out] * 4)(gathered, w, m, v)


_WEIGHTS = ["ffn1_pre_g", "ffn1_post_g", "ffn1_w_gate", "ffn1_w_up", "ffn1_w_down", "mix_pre_g",
            "mix_post_g", "w_in", "sgu_norm_g", "sgu_norm_b", "sgu_w_s", "sgu_b_s", "sgu_out_g",
            "sb_out_g", "w_out", "xa_pre_g", "xa_post_g", "mem_norm_g", "xa_w_q", "xa_w_kv", "xa_w_o",
            "ffn2_pre_g", "ffn2_post_g", "ffn2_w_gate", "ffn2_w_up", "ffn2_w_down", "final_norm_g"]
_BIG = ["ffn1_w_gate", "ffn1_w_up", "ffn1_w_down", "w_in", "w_out", "xa_w_q", "xa_w_kv", "xa_w_o",
        "ffn2_w_gate", "ffn2_w_up", "ffn2_w_down"]
_COL_SHARDED = ("ffn1_w_gate", "ffn1_w_up", "w_in", "xa_w_kv", "ffn2_w_gate", "ffn2_w_up")
_EARLY = ["ffn1_w_gate", "ffn1_w_up", "ffn1_w_down", "w_in"]
_LATE = [n for n in _BIG if n not in _EARLY]
_SMALL = [n for n in _WEIGHTS if n not in _BIG]
SMALL_LANES = 128
SMALL_ROW_ALIGN = 8


def _pack_small(tensors):
    parts = []
    for n in _SMALL:
        t = tensors[n].reshape(-1, SMALL_LANES)
        pad = (-t.shape[0]) % SMALL_ROW_ALIGN
        parts.append(jnp.pad(t, ((0, pad), (0, 0))) if pad else t)
    return jnp.concatenate(parts, axis=0)


def _unpack_small(packed, like):
    out, off = {}, 0
    for n in _SMALL:
        size = like[n].size
        rows = size // SMALL_LANES
        out[n] = packed[off:off + rows].reshape(like[n].shape)
        off += rows + (-rows) % SMALL_ROW_ALIGN
    return out


def kernel(x, mem, ffn1_pre_g, ffn1_post_g, ffn1_w_gate, ffn1_w_up, ffn1_w_down, mix_pre_g, mix_post_g, w_in, sgu_norm_g, sgu_norm_b, sgu_w_s, sgu_b_s, sgu_out_g, sb_out_g, w_out, xa_pre_g, xa_post_g, mem_norm_g, xa_w_q, xa_w_kv, xa_w_o, ffn2_pre_g, ffn2_post_g, ffn2_w_gate, ffn2_w_up, ffn2_w_down, final_norm_g, loss_target, m_ffn1_pre_g, m_ffn1_post_g, m_ffn1_w_gate, m_ffn1_w_up, m_ffn1_w_down, m_mix_pre_g, m_mix_post_g, m_w_in, m_sgu_norm_g, m_sgu_norm_b, m_sgu_w_s, m_sgu_b_s, m_sgu_out_g, m_sb_out_g, m_w_out, m_xa_pre_g, m_xa_post_g, m_mem_norm_g, m_xa_w_q, m_xa_w_kv, m_xa_w_o, m_ffn2_pre_g, m_ffn2_post_g, m_ffn2_w_gate, m_ffn2_w_up, m_ffn2_w_down, m_final_norm_g, v_ffn1_pre_g, v_ffn1_post_g, v_ffn1_w_gate, v_ffn1_w_up, v_ffn1_w_down, v_mix_pre_g, v_mix_post_g, v_w_in, v_sgu_norm_g, v_sgu_norm_b, v_sgu_w_s, v_sgu_b_s, v_sgu_out_g, v_sb_out_g, v_w_out, v_xa_pre_g, v_xa_post_g, v_mem_norm_g, v_xa_w_q, v_xa_w_kv, v_xa_w_o, v_ffn2_pre_g, v_ffn2_post_g, v_ffn2_w_gate, v_ffn2_w_up, v_ffn2_w_down, v_final_norm_g):
    vals = dict(locals())
    d_model = x.shape[-1]

    def packed(names):
        return jnp.concatenate(
            [(vals[n][0].T if n in _COL_SHARDED else vals[n][0]).astype(BF16) for n in names], axis=0)

    shard_rows = {n: vals[n].shape[2 if n in _COL_SHARDED else 1] for n in _BIG}
    big = _unpack_rows(_all_gather(packed(_EARLY), "ag_weights"), _EARLY, shard_rows)

    small = {n: vals[n] for n in _SMALL}
    loss_part, dx, gs, gw, late_rows = _local_step(
        x[0], mem[0], loss_target[0], _small_views(small), big, packed(_LATE), _LATE, shard_rows)
    loss = lax.psum(loss_part[0, 0], ("x", "y", "c"))

    parts = [gw[n].reshape(N_DEV, -1, d_model).astype(BF16) for n in _EARLY]
    rows = sum(p.shape[1] for p in parts)
    p4 = jnp.concatenate(parts, axis=1).reshape(N_DEV // 2, 2, rows, d_model)
    mx, my, mc = _mesh_place()
    place = jnp.stack([mc, 2 * mx + my]).astype(jnp.int32)
    recv_a = _pair_exchange(p4)
    recv_b = _chip_exchange(_pair_sum(place, p4, recv_a))
    early_rows = _rs_final(place, p4, recv_a, recv_b)

    grads, deltas, new_m, new_v = {}, {}, {}, {}
    for names, g_rows in ((_EARLY, early_rows), (_LATE, late_rows)):
        off = 0
        for n in names:
            rows = shard_rows[n]
            g = g_rows[off:off + rows]
            off += rows
            g = g.T if n in _COL_SHARDED else g
            grads[n] = g[None]
            d, m1, v1 = _adamw(vals[n][0], g, vals["m_" + n][0], vals["v_" + n][0], "adamw_" + n)
            deltas[n], new_m[n], new_v[n] = d[None], m1[None], v1[None]

    gathered_small = _all_gather(_pack_small(gs), "ag_small_grads")
    outs = _small_sum_adamw(gathered_small, _pack_small(small),
                            _pack_small({n: vals["m_" + n] for n in _SMALL}),
                            _pack_small({n: vals["v_" + n] for n in _SMALL}))
    for dst, packed in zip((grads, deltas, new_m, new_v), outs):
        dst.update(_unpack_small(packed, small))

    return (loss, dx[None], *[grads[n] for n in _WEIGHTS], *[deltas[n] for n in _WEIGHTS],
            *[new_m[n] for n in _WEIGHTS], *[new_v[n] for n in _WEIGHTS])
```

```python
import functools

import jax
import jax.numpy as jnp
from jax import lax
from jax.experimental import pallas as pl
from jax.experimental.pallas import tpu as pltpu

F32 = jnp.float32
BF16 = jnp.bfloat16
EPS = 1e-6
MESH = pl.DeviceIdType.MESH
N_DEV = 8

SGU_GROUPS = 4
GROUP_DIM = 128
CHUNK = 128
SB_HEADS = 8
SB_HEAD_DIM = 64
Q_BLOCK = 128
XA_HEADS = 4
XA_HEAD_DIM = 256

ADAM_LR = 0.001
ADAM_B1 = 0.9
ADAM_B2 = 0.999
ADAM_EPS = 1e-08
ADAM_WD = 0.01
ADAM_STEP = 10

VMEM_LIMIT_V7X = 56 * 1024 * 1024
GELU_C0 = 0.7978845608028654
GELU_C1 = 0.044715


def _dot(a, b):
    return jnp.dot(a.astype(BF16), b.astype(BF16), preferred_element_type=F32)


def _dot_nt(a, b):
    return lax.dot_general(a.astype(BF16), b.astype(BF16), (((1,), (1,)), ((), ())),
                           preferred_element_type=F32)


def _dot_tn(a, b):
    return lax.dot_general(a.astype(BF16), b.astype(BF16), (((0,), (0,)), ((), ())),
                           preferred_element_type=F32)


def _rms(x, g):
    r = lax.rsqrt(jnp.mean(x * x, axis=-1, keepdims=True) + EPS)
    return x * r * g, r


def _rms_bwd(x, g, dy):
    r = lax.rsqrt(jnp.mean(x * x, axis=-1, keepdims=True) + EPS)
    xh = x * r
    gy = dy * g
    dx = r * (gy - xh * jnp.mean(gy * xh, axis=-1, keepdims=True))
    dg = jnp.sum(dy * xh, axis=0, keepdims=True)
    return dx, dg


def _sigmoid(x):
    return 1.0 / (1.0 + jnp.exp(-x))


def _gelu(x):
    t = jnp.tanh(GELU_C0 * (x + GELU_C1 * x * x * x))
    return 0.5 * x * (1.0 + t)


def _gelu_grad(x):
    t = jnp.tanh(GELU_C0 * (x + GELU_C1 * x * x * x))
    return 0.5 * (1.0 + t) + 0.5 * x * (1.0 - t * t) * GELU_C0 * (1.0 + 3.0 * GELU_C1 * x * x)


def _split_bf16(x):
    hi = x.astype(BF16)
    lo = (x - hi.astype(F32)).astype(BF16)
    return hi, lo


def _row_spec(tm, cols):
    return pl.BlockSpec((tm, cols), lambda i: (i, 0))


def _full_spec(shape):
    nd = len(shape)
    return pl.BlockSpec(tuple(shape), lambda i: (0,) * nd)


def _token_tile(s):
    return min(256, s)


def _row_call(body, name, s, tiled_in, full_in, tiled_out, acc_out, gather=None, scatter=None):
    tm = _token_tile(s)
    steps = s // tm
    in_specs = [_row_spec(tm, a.shape[1]) for a in tiled_in] + [_full_spec(a.shape) for a in full_in]
    out_specs = [_row_spec(tm, c) for c, _ in tiled_out] + [_full_spec(sh) for sh, _ in acc_out]
    out_shape = [jax.ShapeDtypeStruct((s, c), dt) for c, dt in tiled_out]
    out_shape += [jax.ShapeDtypeStruct(sh, dt) for sh, dt in acc_out]
    operands = [*tiled_in, *full_in]
    scratch = []
    kernel_body = functools.partial(body)
    sent = gather if gather is not None else scatter
    if sent is not None:
        n_in, n_out = len(operands), len(out_shape)
        out_shape.append(jax.ShapeDtypeStruct(
            (N_DEV,) + sent.shape if gather is not None else sent.shape, sent.dtype))
        operands.append(sent)
        in_specs.append(_ANY)
        out_specs.append(_ANY)
        scratch = list(_AG_SCRATCH)

        def kernel_body(*refs):
            ins, sent_ref = refs[:n_in], refs[n_in]
            outs, landed_ref = refs[n_in + 1:n_in + 1 + n_out], refs[n_in + 1 + n_out]
            step = pl.program_id(0)
            if gather is not None:
                start, forward, finish = _gather_phases(sent_ref, landed_ref, *refs[-3:])
            else:
                start, finish = _scatter_phases(sent_ref, landed_ref, *refs[-3:])
            pl.when(step == 0)(start)
            body(*ins, *outs)
            if gather is not None:
                pl.when(step == (2 * steps) // 3)(forward)
            pl.when(step == steps - 1)(finish)

    return pl.pallas_call(
        kernel_body,
        name=name,
        grid=(steps,),
        in_specs=in_specs,
        out_specs=out_specs,
        out_shape=out_shape,
        scratch_shapes=scratch,
        compiler_params=pltpu.CompilerParams(
            dimension_semantics=("arbitrary",), vmem_limit_bytes=VMEM_LIMIT_V7X),
    )(*operands)


def _acc(ref, val):
    @pl.when(pl.program_id(0) == 0)
    def _():
        ref[...] = val

    @pl.when(pl.program_id(0) != 0)
    def _():
        ref[...] += val


def _ffn_fwd_body(x_ref, pre_ref, post_ref, wgt_ref, wut_ref, wd_ref,
                  h_ref, n_ref, a_ref, b_ref, f_ref):
    x = x_ref[...]
    n, _ = _rms(x, pre_ref[...])
    nb = n.astype(BF16)
    n_ref[...] = nb
    a = _dot_nt(nb, wgt_ref[...])
    b = _dot_nt(nb, wut_ref[...])
    a_ref[...] = a.astype(BF16)
    b_ref[...] = b.astype(BF16)
    hmid = a * _sigmoid(a) * b
    f = jnp.dot(hmid.astype(BF16), wd_ref[...], preferred_element_type=F32)
    f_ref[...] = f
    y, _ = _rms(f, post_ref[...])
    h_ref[...] = x + 0.5 * y


def _ffn_fwd(x, pre_g, post_g, wgt, wut, wd, name, gather=None):
    s, d = x.shape
    f = wgt.shape[0]
    return _row_call(_ffn_fwd_body, name, s, [x], [pre_g, post_g, wgt, wut, wd],
                     [(d, F32), (d, BF16), (f, BF16), (f, BF16), (d, F32)], [], gather=gather)


def _ffn_bwd_body(dh_ref, x_ref, a_ref, b_ref, f_ref, pre_ref, post_ref, wgt_ref, wut_ref, wd_ref,
                  dx_ref, da_ref, db_ref, hm_ref, df_ref, dpre_ref, dpost_ref):
    dh = dh_ref[...]
    df, dpost = _rms_bwd(f_ref[...], post_ref[...], 0.5 * dh)
    dfb = df.astype(BF16)
    df_ref[...] = dfb
    dhmid = _dot_nt(dfb, wd_ref[...])
    a = a_ref[...].astype(F32)
    b = b_ref[...].astype(F32)
    sig = _sigmoid(a)
    sa = a * sig
    hm_ref[...] = (sa * b).astype(BF16)
    dab = (dhmid * b * sig * (1.0 + a * (1.0 - sig))).astype(BF16)
    dbb = (dhmid * sa).astype(BF16)
    da_ref[...] = dab
    db_ref[...] = dbb
    dn = _dot(dab, wgt_ref[...]) + _dot(dbb, wut_ref[...])
    dxn, dpre = _rms_bwd(x_ref[...], pre_ref[...], dn)
    dx_ref[...] = dh + dxn
    _acc(dpre_ref, dpre)
    _acc(dpost_ref, dpost)


def _ffn_bwd(dh, x, a, b, f, pre_g, post_g, wgt, wut, wd, name, scatter=None):
    s, d = x.shape
    ff = wgt.shape[0]
    return _row_call(_ffn_bwd_body, name, s, [dh, x, a, b, f], [pre_g, post_g, wgt, wut, wd],
                     [(d, F32), (ff, BF16), (ff, BF16), (ff, BF16), (d, BF16)],
                     [((1, d), F32), ((1, d), F32)], scatter=scatter)


def _inproj_fwd_body(h_ref, g_ref, wt_ref, n_ref, uv_ref, qkv_ref):
    n, _ = _rms(h_ref[...], g_ref[...])
    nb = n.astype(BF16)
    n_ref[...] = nb
    proj = _dot_nt(nb, wt_ref[...])
    nuv = uv_ref.shape[1]
    uv_ref[...] = proj[:, :nuv]
    qkv_ref[...] = proj[:, nuv:].astype(BF16)


def _inproj_fwd(h, g, w_in_t):
    s, d = h.shape
    sgu_w = SGU_GROUPS * GROUP_DIM
    sb_w = SB_HEADS * SB_HEAD_DIM
    return _row_call(_inproj_fwd_body, "inproj_fwd", s, [h], [g, w_in_t],
                     [(d, BF16), (2 * sgu_w, F32), (3 * sb_w, BF16)], [])


def _inproj_bwd_body(dh_ref, dproj_ref, h_ref, g_ref, wt_ref, dhout_ref, dg_ref):
    dn = _dot(dproj_ref[...], wt_ref[...])
    dhn, dg = _rms_bwd(h_ref[...], g_ref[...], dn)
    dhout_ref[...] = dh_ref[...] + dhn
    _acc(dg_ref, dg)


def _inproj_bwd(dh, dproj, h, g, w_in_t):
    s, d = h.shape
    return _row_call(_inproj_bwd_body, "inproj_bwd", s, [dh, dproj, h], [g, w_in_t],
                     [(d, F32)], [((1, d), F32)])


def _causal_w(ws_ref, g):
    row = lax.broadcasted_iota(jnp.int32, (CHUNK, CHUNK), 0)
    col = lax.broadcasted_iota(jnp.int32, (CHUNK, CHUNK), 1)
    return jnp.where(row >= col, ws_ref[g], 0.0), row >= col


def _group_norm(v):
    mu = jnp.mean(v, axis=-1, keepdims=True)
    d = v - mu
    rstd = lax.rsqrt(jnp.mean(d * d, axis=-1, keepdims=True) + EPS)
    return d * rstd, rstd


def _sgu_fwd_body(uv_ref, ng_ref, nb_ref, ws_ref, bs_ref, out_ref):
    width = SGU_GROUPS * GROUP_DIM
    for c in range(uv_ref.shape[0] // CHUNK):
        rows = pl.ds(c * CHUNK, CHUNK)
        for g in range(SGU_GROUPS):
            lanes = pl.ds(g * GROUP_DIM, GROUP_DIM)
            u = _gelu(uv_ref[rows, lanes])
            v = _gelu(uv_ref[rows, pl.ds(width + g * GROUP_DIM, GROUP_DIM)])
            vhat, _ = _group_norm(v)
            vn = vhat * ng_ref[:, lanes] + nb_ref[:, lanes]
            w, _ = _causal_w(ws_ref, g)
            mixed = _dot(w, vn) + bs_ref[g]
            out_ref[rows, lanes] = u * mixed


def _sgu_fwd(uv_pre, ng, nb, ws, bs):
    s = uv_pre.shape[0]
    return _row_call(_sgu_fwd_body, "sgu_fwd", s, [uv_pre], [ng, nb, ws, bs],
                     [(SGU_GROUPS * GROUP_DIM, F32)], [])[0]


def _sgu_bwd_body(uv_ref, do_ref, ng_ref, nb_ref, ws_ref, bs_ref,
                  duv_ref, dws_ref, dbs_ref, dng_ref, dnb_ref):
    width = SGU_GROUPS * GROUP_DIM

    @pl.when(pl.program_id(0) == 0)
    def _():
        dws_ref[...] = jnp.zeros_like(dws_ref)
        dbs_ref[...] = jnp.zeros_like(dbs_ref)
        dng_ref[...] = jnp.zeros_like(dng_ref)
        dnb_ref[...] = jnp.zeros_like(dnb_ref)

    for c in range(uv_ref.shape[0] // CHUNK):
        rows = pl.ds(c * CHUNK, CHUNK)
        for g in range(SGU_GROUPS):
            lanes = pl.ds(g * GROUP_DIM, GROUP_DIM)
            vlanes = pl.ds(width + g * GROUP_DIM, GROUP_DIM)
            u_pre = uv_ref[rows, lanes]
            v_pre = uv_ref[rows, vlanes]
            u = _gelu(u_pre)
            v = _gelu(v_pre)
            vhat, rstd = _group_norm(v)
            gain = ng_ref[:, lanes]
            vn = vhat * gain + nb_ref[:, lanes]
            w, causal = _causal_w(ws_ref, g)
            mixed = _dot(w, vn) + bs_ref[g]
            dout = do_ref[rows, lanes]
            du = dout * mixed
            dmixed = dout * u
            dbs_ref[g] += jnp.sum(dmixed, axis=1, keepdims=True)
            dws_ref[g] += jnp.where(causal, _dot_nt(dmixed, vn), 0.0)
            dvn = _dot_tn(w, dmixed)
            dng_ref[:, lanes] += jnp.sum(dvn * vhat, axis=0, keepdims=True)
            dnb_ref[:, lanes] += jnp.sum(dvn, axis=0, keepdims=True)
            dvh = dvn * gain
            dv = rstd * (dvh - jnp.mean(dvh, axis=-1, keepdims=True)
                         - vhat * jnp.mean(dvh * vhat, axis=-1, keepdims=True))
            duv_ref[rows, lanes] = (du * _gelu_grad(u_pre)).astype(BF16)
            duv_ref[rows, vlanes] = (dv * _gelu_grad(v_pre)).astype(BF16)


def _sgu_bwd(uv_pre, dout_a, ng, nb, ws, bs):
    s = uv_pre.shape[0]
    width = SGU_GROUPS * GROUP_DIM
    return _row_call(_sgu_bwd_body, "sgu_bwd", s, [uv_pre, dout_a], [ng, nb, ws, bs],
                     [(2 * width, BF16)],
                     [(ws.shape, F32), (bs.shape, F32), ((1, width), F32), ((1, width), F32)])


def _mesh_place():
    return lax.axis_index("x"), lax.axis_index("y"), lax.axis_index("c")


def _other_chips(mx, my):
    return [(1 - mx, my), (mx, 1 - my), (1 - mx, 1 - my)]


_ANY = pl.BlockSpec(memory_space=pl.ANY)
AG_SEMS = 7
_AG_SCRATCH = [pltpu.SemaphoreType.DMA((AG_SEMS,)), pltpu.SemaphoreType.DMA((AG_SEMS,)),
               pltpu.SemaphoreType.DMA(())]


def _gather_phases(x_ref, out_ref, send_sems, recv_sems, local_sem):
    mx, my, mc = _mesh_place()
    me, sibling = (mx, my, mc), (mx, my, 1 - mc)
    chips = _other_chips(mx, my)

    def slot(px, py, pc):
        return out_ref.at[4 * px + 2 * py + pc]

    def copy(k, block, to, src=None):
        return pltpu.make_async_remote_copy(
            src_ref=slot(*block) if src is None else src, dst_ref=slot(*block),
            send_sem=send_sems.at[k], recv_sem=recv_sems.at[k],
            device_id=to, device_id_type=MESH)

    mine = pltpu.make_async_copy(x_ref, slot(*me), local_sem)
    first = [copy(0, me, sibling, src=x_ref)]
    first += [copy(1 + j, me, (*chip, mc), src=x_ref) for j, chip in enumerate(chips)]
    passed = [copy(4 + j, (*chip, mc), sibling) for j, chip in enumerate(chips)]

    def start():
        mine.start()
        for cp in first:
            cp.start()

    def forward():
        for j, chip in enumerate(chips):
            copy(1 + j, (*chip, mc), me).wait_recv()
            passed[j].start()

    def finish():
        copy(0, sibling, me).wait_recv()
        for j, chip in enumerate(chips):
            copy(4 + j, (*chip, 1 - mc), me).wait_recv()
        for cp in first + passed:
            cp.wait_send()
        mine.wait()

    return start, forward, finish


def _all_gather(x, name):
    r, c = x.shape

    def body(x_ref, out_ref, send_sems, recv_sems, local_sem):
        start, forward, finish = _gather_phases(x_ref, out_ref, send_sems, recv_sems, local_sem)
        start()
        forward()
        finish()

    return pl.pallas_call(
        body,
        name=name,
        out_shape=jax.ShapeDtypeStruct((N_DEV, r, c), x.dtype),
        in_specs=[_ANY],
        out_specs=_ANY,
        scratch_shapes=list(_AG_SCRATCH),
    )(x)


def _scatter_phases(p_ref, out_ref, send_sems, recv_sems, local_sem):
    mx, my, mc = _mesh_place()
    me = 4 * mx + 2 * my + mc
    copies = []
    for k in range(1, N_DEV):
        tx, ty, tc = mx ^ ((k >> 2) & 1), my ^ ((k >> 1) & 1), mc ^ (k & 1)
        copies.append(pltpu.make_async_remote_copy(
            src_ref=p_ref.at[4 * tx + 2 * ty + tc], dst_ref=out_ref.at[me],
            send_sem=send_sems.at[k - 1], recv_sem=recv_sems.at[k - 1],
            device_id=(tx, ty, tc), device_id_type=MESH))
    mine = pltpu.make_async_copy(p_ref.at[me], out_ref.at[me], local_sem)

    def start():
        mine.start()
        for cp in copies:
            cp.start()

    def finish():
        for cp in copies:
            cp.wait()
        mine.wait()

    return start, finish


SB_DEAD = -105.0
HEADS_PER_TILE = 2
TILES_PER_STEP = 2
HEADS_PER_STEP = HEADS_PER_TILE * TILES_PER_STEP
STEP_LANES = TILES_PER_STEP * HEADS_PER_TILE * SB_HEAD_DIM
TILE_LANES = HEADS_PER_TILE * SB_HEAD_DIM
STACK_ROWS = HEADS_PER_STEP * Q_BLOCK
TILE_ROWS = HEADS_PER_TILE * Q_BLOCK
SB_FORWARD_LEAD = 12


def _stack_heads(x):
    lane = lax.broadcasted_iota(jnp.int32, x.shape, 1)
    zero = jnp.zeros_like(x)
    return jnp.concatenate(
        [jnp.where(lane // SB_HEAD_DIM == h, x, zero) for h in range(HEADS_PER_STEP)], axis=0)


def _unstack_tile(x):
    first = lax.broadcasted_iota(jnp.int32, (Q_BLOCK, TILE_LANES), 1) < SB_HEAD_DIM
    return jnp.where(first, x[:Q_BLOCK], x[Q_BLOCK:])


def _sb_logs(qs, k, diagonal):
    z = _dot_nt(qs, k) * (SB_HEAD_DIM ** -0.5)
    sp = jnp.log1p(jnp.exp(-jnp.abs(z)))
    log_beta = jnp.minimum(z, 0.0) - sp
    log_1m_raw = -jnp.maximum(z, 0.0) - sp
    if not diagonal:
        return None, log_beta, log_1m_raw, log_1m_raw
    row = lax.broadcasted_iota(jnp.int32, z.shape, 0)
    col = lax.broadcasted_iota(jnp.int32, z.shape, 1)
    strict = col < jnp.bitwise_and(row, Q_BLOCK - 1)
    return strict, log_beta, log_1m_raw, jnp.where(strict, log_1m_raw, 0.0)


def _masked(strict, x):
    return x if strict is None else jnp.where(strict, x, 0.0)


def _key_sums(x, pick):
    hi, lo = _split_bf16(x)
    both = jnp.dot(jnp.concatenate([hi, lo], axis=0), pick, preferred_element_type=F32)
    return both[:x.shape[0]] + both[x.shape[0]:]


def _key_order():
    row = lax.broadcasted_iota(jnp.int32, (Q_BLOCK, Q_BLOCK), 0)
    col = lax.broadcasted_iota(jnp.int32, (Q_BLOCK, Q_BLOCK), 1)
    return row, col


def _sb_fwd_body(q_ref, k_ref, v_ref, shard_ref, o_ref, tot_ref, cnt_ref, gathered_ref,
                 acc_ref, send_sems, recv_sems, local_sem):
    grp, qb = pl.program_id(0), pl.program_id(1)
    last_grp, last_qb = pl.num_programs(0) - 1, pl.num_programs(1) - 1
    ag_start, ag_forward, ag_finish = _gather_phases(shard_ref, gathered_ref, send_sems, recv_sems, local_sem)
    pl.when(jnp.logical_and(grp == 0, qb == 0))(ag_start)

    qs = _stack_heads(q_ref[...])
    row, col = _key_order()
    later = (row > col).astype(BF16)

    def block(i, c, diagonal):
        rows = pl.ds(pl.multiple_of((qb - i) * Q_BLOCK, Q_BLOCK), Q_BLOCK)
        strict, log_beta, _, log_1m = _sb_logs(qs, k_ref[rows, :], diagonal)
        a = _masked(strict, jnp.exp(log_beta + _key_sums(log_1m, later) + c)).astype(BF16)
        for t in range(TILES_PER_STEP):
            part = jnp.dot(a[t * TILE_ROWS:(t + 1) * TILE_ROWS], v_ref[rows, t * TILE_LANES:(t + 1) * TILE_LANES],
                           preferred_element_type=F32)
            if diagonal:
                acc_ref[t] = part
            else:
                acc_ref[t] += part
        return c + jnp.sum(log_1m, axis=1, keepdims=True)

    c = block(0, jnp.zeros((STACK_ROWS, 1), F32), True)

    def alive(carry):
        i, c = carry
        return jnp.logical_and(i <= qb, jnp.max(c) > SB_DEAD)

    def step(carry):
        i, c = carry
        return i + 1, block(i, c, False)

    n, c = lax.while_loop(alive, step, (jnp.int32(1), c))
    for t in range(TILES_PER_STEP):
        o_ref[:, t * TILE_LANES:(t + 1) * TILE_LANES] = _unstack_tile(acc_ref[t])
    for h in range(HEADS_PER_STEP):
        tot_ref[h] = c[h * Q_BLOCK:(h + 1) * Q_BLOCK]
    cnt_ref[grp, qb] = n.astype(F32)
    pl.when(jnp.logical_and(grp == last_grp, qb == jnp.maximum(last_qb - SB_FORWARD_LEAD, 0)))(ag_forward)
    pl.when(jnp.logical_and(grp == last_grp, qb == last_qb))(ag_finish)


def _sb_fwd(qkv, shard):
    s = qkv.shape[0]
    groups = SB_HEADS // HEADS_PER_STEP
    nq = s // Q_BLOCK
    return pl.pallas_call(
        functools.partial(_sb_fwd_body),
        name="sb_fwd",
        grid=(groups, nq),
        in_specs=[pl.BlockSpec((Q_BLOCK, STEP_LANES), lambda g, i: (i, g)),
                  pl.BlockSpec((s, STEP_LANES), lambda g, i: (0, groups + g)),
                  pl.BlockSpec((s, STEP_LANES), lambda g, i: (0, 2 * groups + g)),
                  _ANY],
        out_specs=[pl.BlockSpec((Q_BLOCK, STEP_LANES), lambda g, i: (i, g)),
                   pl.BlockSpec((HEADS_PER_STEP, Q_BLOCK, 1), lambda g, i: (g, i, 0)),
                   pl.BlockSpec(memory_space=pltpu.SMEM),
                   _ANY],
        out_shape=[jax.ShapeDtypeStruct((s, SB_HEADS * SB_HEAD_DIM), F32),
                   jax.ShapeDtypeStruct((SB_HEADS, s, 1), F32),
                   jax.ShapeDtypeStruct((groups, nq), F32),
                   jax.ShapeDtypeStruct((N_DEV,) + shard.shape, shard.dtype)],
        scratch_shapes=[pltpu.VMEM((TILES_PER_STEP, TILE_ROWS, TILE_LANES), F32)] + list(_AG_SCRATCH),
        compiler_params=pltpu.CompilerParams(
            dimension_semantics=("arbitrary", "arbitrary"), vmem_limit_bytes=VMEM_LIMIT_V7X),
    )(qkv, qkv, qkv, shard)


def _sb_bwd_body(cnt_ref, q_ref, k_ref, v_ref, tot_ref, do_ref, part_ref, dq_ref, dk_ref, dv_ref, recv_ref,
                 acc_ref, send_sems, recv_sems, local_sem):
    grp, qb = pl.program_id(0), pl.program_id(1)
    last_grp, last_qb = pl.num_programs(0) - 1, pl.num_programs(1) - 1
    rs_start, rs_finish = _scatter_phases(part_ref, recv_ref, send_sems, recv_sems, local_sem)
    pl.when(jnp.logical_and(grp == 0, qb == 0))(rs_start)

    @pl.when(qb == 0)
    def _():
        dk_ref[...] = jnp.zeros_like(dk_ref)
        dv_ref[...] = jnp.zeros_like(dv_ref)

    acc_ref[...] = jnp.zeros_like(acc_ref)
    qs = _stack_heads(q_ref[...])
    dos = _stack_heads(do_ref[...].astype(BF16))
    tot = jnp.concatenate([tot_ref[h] for h in range(HEADS_PER_STEP)], axis=0)
    row, col = _key_order()
    up_to = (row <= col).astype(BF16)
    earlier = (row < col).astype(BF16)
    scale = SB_HEAD_DIM ** -0.5
    n = jnp.clip(cnt_ref[grp, qb].astype(jnp.int32), 1, qb + 1)

    def block(kb, c, ce, diagonal):
        rows = pl.ds(pl.multiple_of(kb * Q_BLOCK, Q_BLOCK), Q_BLOCK)
        k = k_ref[rows, :]
        strict, log_beta, log_1m_raw, log_1m = _sb_logs(qs, k, diagonal)
        suffix = tot - c - _key_sums(log_1m, up_to)
        a = _masked(strict, jnp.exp(log_beta + suffix))
        de = _dot_nt(dos, v_ref[rows, :]) * a
        before = ce + _key_sums(de, earlier)
        dz = _masked(strict, de * jnp.exp(log_1m_raw) - before * jnp.exp(log_beta)).astype(BF16)
        for t in range(TILES_PER_STEP):
            acc_ref[t] += jnp.dot(dz[t * TILE_ROWS:(t + 1) * TILE_ROWS], k[:, t * TILE_LANES:(t + 1) * TILE_LANES],
                                  preferred_element_type=F32)
        dk_ref[rows, :] += _dot_tn(dz, qs) * scale
        dv_ref[rows, :] += _dot_tn(a, dos)
        return c + jnp.sum(log_1m, axis=1, keepdims=True), ce + jnp.sum(de, axis=1, keepdims=True)

    def step(i, carry):
        return block(qb - n + 1 + i, *carry, False)

    zc = jnp.zeros((STACK_ROWS, 1), F32)
    c, ce = lax.fori_loop(0, n - 1, step, (zc, zc))
    block(qb, c, ce, True)
    for t in range(TILES_PER_STEP):
        dq_ref[:, t * TILE_LANES:(t + 1) * TILE_LANES] = _unstack_tile(acc_ref[t]) * scale
    pl.when(jnp.logical_and(grp == last_grp, qb == last_qb))(rs_finish)


def _sb_bwd(cnt, qkv, tot, dout_b, parts):
    s = qkv.shape[0]
    groups = SB_HEADS // HEADS_PER_STEP
    return pl.pallas_call(
        functools.partial(_sb_bwd_body),
        name="sb_bwd",
        grid=(groups, s // Q_BLOCK),
        in_specs=[pl.BlockSpec(memory_space=pltpu.SMEM),
                  pl.BlockSpec((Q_BLOCK, STEP_LANES), lambda g, i: (i, g)),
                  pl.BlockSpec((s, STEP_LANES), lambda g, i: (0, groups + g)),
                  pl.BlockSpec((s, STEP_LANES), lambda g, i: (0, 2 * groups + g)),
                  pl.BlockSpec((HEADS_PER_STEP, Q_BLOCK, 1), lambda g, i: (g, i, 0)),
                  pl.BlockSpec((Q_BLOCK, STEP_LANES), lambda g, i: (i, g)),
                  _ANY],
        out_specs=[pl.BlockSpec((Q_BLOCK, STEP_LANES), lambda g, i: (i, g)),
                   pl.BlockSpec((s, STEP_LANES), lambda g, i: (0, g)),
                   pl.BlockSpec((s, STEP_LANES), lambda g, i: (0, g)),
                   _ANY],
        out_shape=[jax.ShapeDtypeStruct((s, SB_HEADS * SB_HEAD_DIM), F32)] * 3
        + [jax.ShapeDtypeStruct(parts.shape, parts.dtype)],
        scratch_shapes=[pltpu.VMEM((TILES_PER_STEP, TILE_ROWS, TILE_LANES), F32)] + list(_AG_SCRATCH),
        compiler_params=pltpu.CompilerParams(
            dimension_semantics=("arbitrary", "arbitrary"), vmem_limit_bytes=VMEM_LIMIT_V7X),
    )(cnt, qkv, qkv, qkv, tot, dout_b, parts)


def _outproj_fwd_body(oa_ref, ob_ref, h_ref, ga_ref, gb_ref, gpost_ref, w_ref,
                      merged_ref, mo_ref, hout_ref):
    half = oa_ref.shape[1]
    ma, _ = _rms(oa_ref[...], ga_ref[...])
    mb, _ = _rms(ob_ref[...], gb_ref[...])
    mab = ma.astype(BF16)
    mbb = mb.astype(BF16)
    merged_ref[:, :half] = mab
    merged_ref[:, half:] = mbb
    mo = (jnp.dot(mab, w_ref[:half, :], preferred_element_type=F32)
          + jnp.dot(mbb, w_ref[half:, :], preferred_element_type=F32))
    mo_ref[...] = mo
    y, _ = _rms(mo, gpost_ref[...])
    hout_ref[...] = h_ref[...] + y


def _outproj_fwd(out_a, out_b, h, ga, gb, gpost, w_out):
    s, d = h.shape
    return _row_call(_outproj_fwd_body, "outproj_fwd", s, [out_a, out_b, h], [ga, gb, gpost, w_out],
                     [(d, BF16), (d, F32), (d, F32)], [])


def _outproj_bwd_body(dh_ref, mo_ref, oa_ref, ob_ref, ga_ref, gb_ref, gpost_ref, w_ref,
                      dmo_ref, doa_ref, dob_ref, dga_ref, dgb_ref, dgpost_ref):
    half = oa_ref.shape[1]
    dmo, dgpost = _rms_bwd(mo_ref[...], gpost_ref[...], dh_ref[...])
    dmob = dmo.astype(BF16)
    dmo_ref[...] = dmob
    dma = _dot_nt(dmob, w_ref[:half, :])
    dmb = _dot_nt(dmob, w_ref[half:, :])
    doa, dga = _rms_bwd(oa_ref[...], ga_ref[...], dma)
    dob, dgb = _rms_bwd(ob_ref[...], gb_ref[...], dmb)
    doa_ref[...] = doa
    dob_ref[...] = dob
    _acc(dga_ref, dga)
    _acc(dgb_ref, dgb)
    _acc(dgpost_ref, dgpost)


def _outproj_bwd(dh, mo, out_a, out_b, ga, gb, gpost, w_out):
    s, d = dh.shape
    half = out_a.shape[1]
    return _row_call(_outproj_bwd_body, "outproj_bwd", s, [dh, mo, out_a, out_b], [ga, gb, gpost, w_out],
                     [(d, BF16), (half, F32), (half, F32)],
                     [((1, half), F32), ((1, half), F32), ((1, d), F32)])


def _kv_fwd_body(mem_ref, g_ref, wt_ref, memn_ref, kv_ref):
    n, _ = _rms(mem_ref[...], g_ref[...])
    nb = n.astype(BF16)
    memn_ref[...] = nb
    kv_ref[...] = _dot_nt(nb, wt_ref[...]).astype(BF16)


def _kv_fwd(mem, g, w_kv_t):
    m, d = mem.shape
    return _row_call(_kv_fwd_body, "kv_fwd", m, [mem], [g, w_kv_t], [(d, BF16), (w_kv_t.shape[0], BF16)], [])


def _kv_bwd_body(dkv_ref, mem_ref, memn_ref, g_ref, wt_ref, dwt_ref, dg_ref):
    dkvb = dkv_ref[...].astype(BF16)
    dwt_ref[...] = _dot_tn(dkvb, memn_ref[...]).astype(BF16)
    dmemn = _dot(dkvb, wt_ref[...])
    _, dg = _rms_bwd(mem_ref[...], g_ref[...], dmemn)
    dg_ref[...] = dg


def _kv_bwd(dkv, mem, memn, g, w_kv_t):
    m, d = mem.shape
    return pl.pallas_call(
        functools.partial(_kv_bwd_body),
        name="kv_bwd",
        out_shape=[jax.ShapeDtypeStruct(w_kv_t.shape, BF16), jax.ShapeDtypeStruct((1, d), F32)],
        compiler_params=pltpu.CompilerParams(vmem_limit_bytes=VMEM_LIMIT_V7X),
    )(dkv, mem, memn, g, w_kv_t)


def _xa_fwd_body(h_ref, gpre_ref, gpost_ref, wq_ref, wo_ref, kv_ref,
                 n_ref, q_ref, o_ref, c_ref, hout_ref):
    h = h_ref[...]
    d = h.shape[1]
    n, _ = _rms(h, gpre_ref[...])
    nb = n.astype(BF16)
    n_ref[...] = nb
    qb = jnp.dot(nb, wq_ref[...], preferred_element_type=F32).astype(BF16)
    q_ref[...] = qb
    for hd in range(XA_HEADS):
        lanes = slice(hd * XA_HEAD_DIM, (hd + 1) * XA_HEAD_DIM)
        k = kv_ref[:, lanes]
        v = kv_ref[:, d + hd * XA_HEAD_DIM:d + (hd + 1) * XA_HEAD_DIM]
        logits = _dot_nt(qb[:, lanes], k) * (XA_HEAD_DIM ** -0.5)
        e = jnp.exp(logits - jnp.max(logits, axis=-1, keepdims=True))
        p = e / jnp.sum(e, axis=-1, keepdims=True)
        o_ref[:, lanes] = jnp.dot(p.astype(BF16), v, preferred_element_type=F32).astype(BF16)
    c = jnp.dot(o_ref[...], wo_ref[...], preferred_element_type=F32)
    c_ref[...] = c
    y, _ = _rms(c, gpost_ref[...])
    hout_ref[...] = h + y


def _xa_fwd(h, gpre, gpost, wq, wo, kv):
    s, d = h.shape
    return _row_call(_xa_fwd_body, "xa_fwd", s, [h], [gpre, gpost, wq, wo, kv],
                     [(d, BF16), (d, BF16), (d, BF16), (d, F32), (d, F32)], [])


def _xa_bwd_body(dh_ref, h_ref, c_ref, q_ref, o_ref, gpre_ref, gpost_ref, wq_ref, wo_ref, kv_ref,
                 dhout_ref, dc_ref, dq_ref, dkv_ref, dgpre_ref, dgpost_ref):
    dh = dh_ref[...]
    d = dh.shape[1]
    scale = XA_HEAD_DIM ** -0.5
    dc, dgpost = _rms_bwd(c_ref[...], gpost_ref[...], dh)
    dcb = dc.astype(BF16)
    dc_ref[...] = dcb
    dob = _dot_nt(dcb, wo_ref[...]).astype(BF16)

    @pl.when(pl.program_id(0) == 0)
    def _():
        dkv_ref[...] = jnp.zeros_like(dkv_ref)

    for hd in range(XA_HEADS):
        lanes = slice(hd * XA_HEAD_DIM, (hd + 1) * XA_HEAD_DIM)
        vlanes = slice(d + hd * XA_HEAD_DIM, d + (hd + 1) * XA_HEAD_DIM)
        qh = q_ref[:, lanes]
        k = kv_ref[:, lanes]
        v = kv_ref[:, vlanes]
        logits = _dot_nt(qh, k) * scale
        e = jnp.exp(logits - jnp.max(logits, axis=-1, keepdims=True))
        p = e / jnp.sum(e, axis=-1, keepdims=True)
        doh = dob[:, lanes]
        dp = _dot_nt(doh, v)
        dl = (p * (dp - jnp.sum(dp * p, axis=-1, keepdims=True)) * scale).astype(BF16)
        dq_ref[:, lanes] = jnp.dot(dl, k, preferred_element_type=F32).astype(BF16)
        dkv_ref[:, lanes] += _dot_tn(dl, qh)
        dkv_ref[:, vlanes] += _dot_tn(p, doh)
    dn = _dot_nt(dq_ref[...], wq_ref[...])
    dhn, dgpre = _rms_bwd(h_ref[...], gpre_ref[...], dn)
    dhout_ref[...] = dh + dhn
    _acc(dgpre_ref, dgpre)
    _acc(dgpost_ref, dgpost)


def _xa_bwd(dh, h, c, q, o, gpre, gpost, wq, wo, kv):
    s, d = h.shape
    return _row_call(_xa_bwd_body, "xa_bwd", s, [dh, h, c, q, o], [gpre, gpost, wq, wo, kv],
                     [(d, F32), (d, BF16), (d, BF16)],
                     [(kv.shape, F32), ((1, d), F32), ((1, d), F32)])


def _final_body(h_ref, t_ref, g_ref, dh_ref, loss_ref, dg_ref):
    h = h_ref[...]
    d = h.shape[1]
    y, _ = _rms(h, g_ref[...])
    err = y - t_ref[...]
    part = (0.5 / d) * jnp.sum(jnp.sum(err * err, axis=1, keepdims=True), axis=0, keepdims=True)
    dh, dg = _rms_bwd(h, g_ref[...], err * (1.0 / d))
    dh_ref[...] = dh
    _acc(loss_ref, part)
    _acc(dg_ref, dg)


def _final(h, g, target):
    s, d = h.shape
    return _row_call(_final_body, "final_loss", s, [h, target], [g],
                     [(d, F32)], [((1, 1), F32), ((1, d), F32)])


def _largest_tile(n, cap):
    best = 128
    for t in range(128, cap + 1, 128):
        if n % t == 0:
            best = t
    return best


def _mm_tn(a, bs, name):
    s, k = a.shape
    n = bs[0].shape[1]
    nb = len(bs)
    ts = min(512, s)
    tk = _largest_tile(k, 1536)
    tn = _largest_tile(n, 1536 // nb)

    steps = s // ts

    def body(a_ref, *refs):
        b_refs, o_refs, acc_refs = refs[:nb], refs[nb:2 * nb], refs[2 * nb:]
        at = a_ref[...]
        t = pl.program_id(2)

        @pl.when(t == 0)
        def _():
            for acc_ref in acc_refs:
                acc_ref[...] = jnp.zeros_like(acc_ref)

        for b_ref, acc_ref in zip(b_refs, acc_refs):
            acc_ref[...] += _dot_tn(at, b_ref[...])

        @pl.when(t == steps - 1)
        def _():
            for o_ref, acc_ref in zip(o_refs, acc_refs):
                o_ref[...] = acc_ref[...].astype(BF16)

    return pl.pallas_call(
        body,
        name=name,
        grid=(k // tk, n // tn, steps),
        in_specs=[pl.BlockSpec((ts, tk), lambda i, j, t: (t, i))]
        + [pl.BlockSpec((ts, tn), lambda i, j, t: (t, j))] * nb,
        out_specs=[pl.BlockSpec((tk, tn), lambda i, j, t: (i, j))] * nb,
        out_shape=[jax.ShapeDtypeStruct((k, n), BF16)] * nb,
        scratch_shapes=[pltpu.VMEM((tk, tn), F32)] * nb,
        compiler_params=pltpu.CompilerParams(
            dimension_semantics=("arbitrary", "arbitrary", "arbitrary"),
            vmem_limit_bytes=VMEM_LIMIT_V7X),
    )(a, *bs)


_SMALL_SHAPES = {
    "sgu_norm_g": (1, SGU_GROUPS * GROUP_DIM),
    "sgu_norm_b": (1, SGU_GROUPS * GROUP_DIM),
    "sgu_w_s": (SGU_GROUPS, CHUNK, CHUNK),
    "sgu_b_s": (SGU_GROUPS, CHUNK, 1),
}


def _small_views(small):
    return {n: v.reshape(_SMALL_SHAPES.get(n, v.shape)) for n, v in small.items()}


def _small_unviews(views, like):
    return {n: v.reshape(like[n].shape) for n, v in views.items()}


def _unpack_rows(gathered, names, shard_rows):
    out, off = {}, 0
    for n in names:
        rows = shard_rows[n]
        out[n] = gathered[:, off:off + rows, :].reshape(N_DEV * rows, gathered.shape[2])
        off += rows
    return out


def _row_tile(r, cap):
    best = 16
    for t in range(16, cap + 1, 16):
        if r % t == 0:
            best = t
    return best


def _sum_received(received, name):
    _, r, c = received.shape
    tr = _row_tile(r, 1024)

    def body(rc_ref, g_ref):
        g = rc_ref[0].astype(F32)
        for t in range(1, N_DEV):
            g = g + rc_ref[t].astype(F32)
        g_ref[...] = g

    return pl.pallas_call(
        body, name=name, grid=(r // tr,),
        in_specs=[pl.BlockSpec((N_DEV, tr, c), lambda i: (0, i, 0))],
        out_specs=pl.BlockSpec((tr, c), lambda i: (i, 0)),
        out_shape=jax.ShapeDtypeStruct((r, c), F32),
    )(received)


def _local_step(x, mem, target, small, big, w_in_shard, late_shard, late_names, shard_rows):
    sm, w = small, dict(big)
    d_model = x.shape[1]
    h1, n1, a1, b1, f1, w_in_all = _ffn_fwd(
        x, sm["ffn1_pre_g"], sm["ffn1_post_g"], w["ffn1_w_gate"], w["ffn1_w_up"], w["ffn1_w_down"],
        "ffn1_fwd", gather=w_in_shard)
    w["w_in"] = w_in_all.reshape(-1, d_model)
    n2, uv_pre, qkv = _inproj_fwd(h1, sm["mix_pre_g"], w["w_in"])
    out_a = _sgu_fwd(uv_pre, sm["sgu_norm_g"], sm["sgu_norm_b"], sm["sgu_w_s"], sm["sgu_b_s"])
    out_b, tot, cnt, late = _sb_fwd(qkv, late_shard)
    w.update(_unpack_rows(late, late_names, shard_rows))
    merged, mo, h2 = _outproj_fwd(out_a, out_b, h1, sm["sgu_out_g"], sm["sb_out_g"],
                                  sm["mix_post_g"], w["w_out"])
    memn, kv = _kv_fwd(mem, sm["mem_norm_g"], w["xa_w_kv"])
    n3, qx, ox, cx, h3 = _xa_fwd(h2, sm["xa_pre_g"], sm["xa_post_g"], w["xa_w_q"], w["xa_w_o"], kv)
    h4, n4, a2, b2, f2 = _ffn_fwd(h3, sm["ffn2_pre_g"], sm["ffn2_post_g"],
                                  w["ffn2_w_gate"], w["ffn2_w_up"], w["ffn2_w_down"], "ffn2_fwd")
    dh4, loss, dg_final = _final(h4, sm["final_norm_g"], target)

    gs, gw = {"final_norm_g": dg_final}, {}
    dh3, da2, db2, hm2, df2, gs["ffn2_pre_g"], gs["ffn2_post_g"] = _ffn_bwd(
        dh4, h3, a2, b2, f2, sm["ffn2_pre_g"], sm["ffn2_post_g"],
        w["ffn2_w_gate"], w["ffn2_w_up"], w["ffn2_w_down"], "ffn2_bwd")
    gw["ffn2_w_gate"], = _mm_tn(da2, [n4], "ffn2_dw_gate")
    gw["ffn2_w_up"], = _mm_tn(db2, [n4], "ffn2_dw_up")
    gw["ffn2_w_down"], = _mm_tn(hm2, [df2], "ffn2_dw_down")

    dh2, dc, dqx, dkv, gs["xa_pre_g"], gs["xa_post_g"] = _xa_bwd(
        dh3, h2, cx, qx, ox, sm["xa_pre_g"], sm["xa_post_g"], w["xa_w_q"], w["xa_w_o"], kv)
    gw["xa_w_o"], = _mm_tn(ox, [dc], "xa_dw_o")
    gw["xa_w_q"], = _mm_tn(n3, [dqx], "xa_dw_q")
    gw["xa_w_kv"], gs["mem_norm_g"] = _kv_bwd(dkv, mem, memn, sm["mem_norm_g"], w["xa_w_kv"])

    dmo, dout_a, dout_b, gs["sgu_out_g"], gs["sb_out_g"], gs["mix_post_g"] = _outproj_bwd(
        dh2, mo, out_a, out_b, sm["sgu_out_g"], sm["sb_out_g"], sm["mix_post_g"], w["w_out"])
    gw["w_out"], = _mm_tn(merged, [dmo], "mix_dw_out")
    parts = jnp.concatenate(
        [gw.pop(n).reshape(N_DEV, -1, d_model).astype(BF16) for n in late_names], axis=1)
    dq, dk, dv, received = _sb_bwd(cnt, qkv, tot, dout_b, parts)
    duv, gs["sgu_w_s"], gs["sgu_b_s"], gs["sgu_norm_g"], gs["sgu_norm_b"] = _sgu_bwd(
        uv_pre, dout_a, sm["sgu_norm_g"], sm["sgu_norm_b"], sm["sgu_w_s"], sm["sgu_b_s"])
    dproj = jnp.concatenate([duv] + [t.astype(BF16) for t in (dq, dk, dv)], axis=1)
    dh1, gs["mix_pre_g"] = _inproj_bwd(dh2, dproj, h1, sm["mix_pre_g"], w["w_in"])
    dw_in, = _mm_tn(dproj, [n2], "mix_dw_in")

    dx, da1, db1, hm1, df1, gs["ffn1_pre_g"], gs["ffn1_post_g"], w_in_received = _ffn_bwd(
        dh1, x, a1, b1, f1, sm["ffn1_pre_g"], sm["ffn1_post_g"],
        w["ffn1_w_gate"], w["ffn1_w_up"], w["ffn1_w_down"], "ffn1_bwd",
        scatter=dw_in.reshape(N_DEV, -1, d_model))
    gw["ffn1_w_gate"], = _mm_tn(da1, [n1], "ffn1_dw_gate")
    gw["ffn1_w_up"], = _mm_tn(db1, [n1], "ffn1_dw_up")
    gw["ffn1_w_down"], = _mm_tn(hm1, [df1], "ffn1_dw_down")
    return loss, dx, gs, gw, _sum_received(w_in_received, "rs_sum_w_in"), _sum_received(received, "rs_sum_late")


def _pair_exchange(p4):
    nchip, _, r, c = p4.shape

    def body(p_ref, out_ref, send_sems, recv_sems):
        mx, my, mc = _mesh_place()
        copies = [pltpu.make_async_remote_copy(
            src_ref=p_ref.at[j, 1 - mc], dst_ref=out_ref.at[j],
            send_sem=send_sems.at[j], recv_sem=recv_sems.at[j],
            device_id=(mx, my, 1 - mc), device_id_type=MESH) for j in range(nchip)]
        for cp in copies:
            cp.start()
        for cp in copies:
            cp.wait()

    return pl.pallas_call(
        body,
        name="rs_pair_exchange",
        out_shape=jax.ShapeDtypeStruct((nchip, r, c), p4.dtype),
        in_specs=[_ANY],
        out_specs=_ANY,
        scratch_shapes=[pltpu.SemaphoreType.DMA((nchip,)), pltpu.SemaphoreType.DMA((nchip,))],
    )(p4)


def _chip_exchange(q):
    _, r, c = q.shape

    def body(q_ref, out_ref, send_sems, recv_sems):
        mx, my, mc = _mesh_place()
        copies = [pltpu.make_async_remote_copy(
            src_ref=q_ref.at[2 * cx + cy], dst_ref=out_ref.at[k],
            send_sem=send_sems.at[k], recv_sem=recv_sems.at[k],
            device_id=(cx, cy, mc), device_id_type=MESH)
            for k, (cx, cy) in enumerate(_other_chips(mx, my))]
        for cp in copies:
            cp.start()
        for cp in copies:
            cp.wait()

    return pl.pallas_call(
        body,
        name="rs_chip_exchange",
        out_shape=jax.ShapeDtypeStruct((3, r, c), q.dtype),
        in_specs=[_ANY],
        out_specs=_ANY,
        scratch_shapes=[pltpu.SemaphoreType.DMA((3,)), pltpu.SemaphoreType.DMA((3,))],
    )(q)


def _rs_row_tile(r):
    return _row_tile(r, 1024)


def _pair_sum(place, p4, recv_a):
    nchip, _, r, c = p4.shape
    tr = _rs_row_tile(r)

    def body(place_ref, p_ref, a_ref, q_ref):
        q_ref[0] = (p_ref[0, 0].astype(F32) + a_ref[0].astype(F32)).astype(BF16)

    return pl.pallas_call(
        body,
        name="rs_pair_sum",
        grid_spec=pltpu.PrefetchScalarGridSpec(
            num_scalar_prefetch=1,
            grid=(nchip, r // tr),
            in_specs=[pl.BlockSpec((1, 1, tr, c), lambda j, i, pref: (j, pref[0], i, 0)),
                      pl.BlockSpec((1, tr, c), lambda j, i, pref: (j, i, 0))],
            out_specs=pl.BlockSpec((1, tr, c), lambda j, i, pref: (j, i, 0)),
        ),
        out_shape=jax.ShapeDtypeStruct((nchip, r, c), BF16),
    )(place, p4, recv_a)


def _rs_final(place, p4, recv_a, recv_b):
    _, _, r, c = p4.shape
    tr = _rs_row_tile(r)

    def body(place_ref, p_ref, a_ref, b_ref, g_ref):
        g = p_ref[0, 0].astype(F32) + a_ref[0].astype(F32)
        for k in range(3):
            g = g + b_ref[k].astype(F32)
        g_ref[...] = g

    return pl.pallas_call(
        body,
        name="rs_final_sum",
        grid_spec=pltpu.PrefetchScalarGridSpec(
            num_scalar_prefetch=1,
            grid=(r // tr,),
            in_specs=[pl.BlockSpec((1, 1, tr, c), lambda i, pref: (pref[1], pref[0], i, 0)),
                      pl.BlockSpec((1, tr, c), lambda i, pref: (pref[1], i, 0)),
                      pl.BlockSpec((3, tr, c), lambda i, pref: (0, i, 0))],
            out_specs=pl.BlockSpec((tr, c), lambda i, pref: (i, 0)),
        ),
        out_shape=jax.ShapeDtypeStruct((r, c), F32),
    )(place, p4, recv_a, recv_b)


def _adamw_math(w, g, m, v):
    m = ADAM_B1 * m + (1.0 - ADAM_B1) * g
    v = ADAM_B2 * v + (1.0 - ADAM_B2) * (g * g)
    m_hat = m / (1.0 - ADAM_B1 ** ADAM_STEP)
    v_hat = v / (1.0 - ADAM_B2 ** ADAM_STEP)
    delta = -ADAM_LR * (m_hat / (jnp.sqrt(v_hat) + ADAM_EPS) + ADAM_WD * w)
    return delta, m, v


def _adamw(w, g, m, v, name):
    r, c = w.shape
    tr = r if r <= 512 else 256

    def body(w_ref, g_ref, m_ref, v_ref, d_ref, mo_ref, vo_ref):
        d_ref[...], mo_ref[...], vo_ref[...] = _adamw_math(w_ref[...], g_ref[...], m_ref[...], v_ref[...])

    spec = pl.BlockSpec((tr, c), lambda i: (i, 0))
    out = jax.ShapeDtypeStruct((r, c), F32)
    return pl.pallas_call(
        body, name=name, grid=(r // tr,), in_specs=[spec] * 4, out_specs=[spec] * 3,
        out_shape=[out] * 3,
    )(w, g, m, v)


def _small_sum_adamw(gathered, w, m, v):
    _, r, c = gathered.shape

    def body(ga_ref, w_ref, m_ref, v_ref, g_ref, d_ref, mo_ref, vo_ref):
        g = ga_ref[0]
        for k in range(1, N_DEV):
            g = g + ga_ref[k]
        g_ref[...] = g
        d_ref[...], mo_ref[...], vo_ref[...] = _adamw_math(w_ref[...], g, m_ref[...], v_ref[...])

    out = jax.ShapeDtypeStruct((r, c), F32)
    return pl.pallas_call(body, name="small_sum_adamw", out_shape=[out] * 4)(gathered, w, m, v)


_WEIGHTS = ["ffn1_pre_g", "ffn1_post_g", "ffn1_w_gate", "ffn1_w_up", "ffn1_w_down", "mix_pre_g",
            "mix_post_g", "w_in", "sgu_norm_g", "sgu_norm_b", "sgu_w_s", "sgu_b_s", "sgu_out_g",
            "sb_out_g", "w_out", "xa_pre_g", "xa_post_g", "mem_norm_g", "xa_w_q", "xa_w_kv", "xa_w_o",
            "ffn2_pre_g", "ffn2_post_g", "ffn2_w_gate", "ffn2_w_up", "ffn2_w_down", "final_norm_g"]
_BIG = ["ffn1_w_gate", "ffn1_w_up", "ffn1_w_down", "w_in", "w_out", "xa_w_q", "xa_w_kv", "xa_w_o",
        "ffn2_w_gate", "ffn2_w_up", "ffn2_w_down"]
_COL_SHARDED = ("ffn1_w_gate", "ffn1_w_up", "w_in", "xa_w_kv", "ffn2_w_gate", "ffn2_w_up")
_EARLY = ["ffn1_w_gate", "ffn1_w_up", "ffn1_w_down"]
_LATE = [n for n in _BIG if n not in _EARLY and n != "w_in"]
_SMALL = [n for n in _WEIGHTS if n not in _BIG]
SMALL_LANES = 128
SMALL_ROW_ALIGN = 8


def _pack_small(tensors):
    parts = []
    for n in _SMALL:
        t = tensors[n].reshape(-1, SMALL_LANES)
        pad = (-t.shape[0]) % SMALL_ROW_ALIGN
        parts.append(jnp.pad(t, ((0, pad), (0, 0))) if pad else t)
    return jnp.concatenate(parts, axis=0)


def _unpack_small(packed, like):
    out, off = {}, 0
    for n in _SMALL:
        size = like[n].size
        rows = size // SMALL_LANES
        out[n] = packed[off:off + rows].reshape(like[n].shape)
        off += rows + (-rows) % SMALL_ROW_ALIGN
    return out


def kernel(x, mem, ffn1_pre_g, ffn1_post_g, ffn1_w_gate, ffn1_w_up, ffn1_w_down, mix_pre_g, mix_post_g, w_in, sgu_norm_g, sgu_norm_b, sgu_w_s, sgu_b_s, sgu_out_g, sb_out_g, w_out, xa_pre_g, xa_post_g, mem_norm_g, xa_w_q, xa_w_kv, xa_w_o, ffn2_pre_g, ffn2_post_g, ffn2_w_gate, ffn2_w_up, ffn2_w_down, final_norm_g, loss_target, m_ffn1_pre_g, m_ffn1_post_g, m_ffn1_w_gate, m_ffn1_w_up, m_ffn1_w_down, m_mix_pre_g, m_mix_post_g, m_w_in, m_sgu_norm_g, m_sgu_norm_b, m_sgu_w_s, m_sgu_b_s, m_sgu_out_g, m_sb_out_g, m_w_out, m_xa_pre_g, m_xa_post_g, m_mem_norm_g, m_xa_w_q, m_xa_w_kv, m_xa_w_o, m_ffn2_pre_g, m_ffn2_post_g, m_ffn2_w_gate, m_ffn2_w_up, m_ffn2_w_down, m_final_norm_g, v_ffn1_pre_g, v_ffn1_post_g, v_ffn1_w_gate, v_ffn1_w_up, v_ffn1_w_down, v_mix_pre_g, v_mix_post_g, v_w_in, v_sgu_norm_g, v_sgu_norm_b, v_sgu_w_s, v_sgu_b_s, v_sgu_out_g, v_sb_out_g, v_w_out, v_xa_pre_g, v_xa_post_g, v_mem_norm_g, v_xa_w_q, v_xa_w_kv, v_xa_w_o, v_ffn2_pre_g, v_ffn2_post_g, v_ffn2_w_gate, v_ffn2_w_up, v_ffn2_w_down, v_final_norm_g):
    vals = dict(locals())
    d_model = x.shape[-1]

    def packed(names):
        return jnp.concatenate(
            [(vals[n][0].T if n in _COL_SHARDED else vals[n][0]).astype(BF16) for n in names], axis=0)

    shard_rows = {n: vals[n].shape[2 if n in _COL_SHARDED else 1] for n in _BIG}
    big = _unpack_rows(_all_gather(packed(_EARLY), "ag_weights"), _EARLY, shard_rows)

    small = {n: vals[n] for n in _SMALL}
    loss_part, dx, gs, gw, w_in_rows, late_rows = _local_step(
        x[0], mem[0], loss_target[0], _small_views(small), big, packed(["w_in"]), packed(_LATE), _LATE,
        shard_rows)
    loss = lax.psum(loss_part[0, 0], ("x", "y", "c"))

    parts = [gw[n].reshape(N_DEV, -1, d_model).astype(BF16) for n in _EARLY]
    rows = sum(p.shape[1] for p in parts)
    p4 = jnp.concatenate(parts, axis=1).reshape(N_DEV // 2, 2, rows, d_model)
    mx, my, mc = _mesh_place()
    place = jnp.stack([mc, 2 * mx + my]).astype(jnp.int32)
    recv_a = _pair_exchange(p4)
    recv_b = _chip_exchange(_pair_sum(place, p4, recv_a))
    early_rows = _rs_final(place, p4, recv_a, recv_b)

    grads, deltas, new_m, new_v = {}, {}, {}, {}
    for names, g_rows in ((_EARLY, early_rows), (["w_in"], w_in_rows), (_LATE, late_rows)):
        off = 0
        for n in names:
            rows = shard_rows[n]
            g = g_rows[off:off + rows]
            off += rows
            g = g.T if n in _COL_SHARDED else g
            grads[n] = g[None]
            d, m1, v1 = _adamw(vals[n][0], g, vals["m_" + n][0], vals["v_" + n][0], "adamw_" + n)
            deltas[n], new_m[n], new_v[n] = d[None], m1[None], v1[None]

    gathered_small = _all_gather(_pack_small(gs), "ag_small_grads")
    outs = _small_sum_adamw(gathered_small, _pack_small(small),
                            _pack_small({n: vals["m_" + n] for n in _SMALL}),
                            _pack_small({n: vals["v_" + n] for n in _SMALL}))
    for dst, packed in zip((grads, deltas, new_m, new_v), outs):
        dst.update(_unpack_small(packed, small))

    return (loss, dx[None], *[grads[n] for n in _WEIGHTS], *[deltas[n] for n in _WEIGHTS],
            *[new_m[n] for n in _WEIGHTS], *[new_v[n] for n in _WEIGHTS])
```

```python
import functools

import jax
import jax.numpy as jnp
from jax import lax
from jax.experimental import pallas as pl
from jax.experimental.pallas import tpu as pltpu

F32 = jnp.float32
BF16 = jnp.bfloat16
EPS = 1e-6
MESH = pl.DeviceIdType.MESH
N_DEV = 8

SGU_GROUPS = 4
GROUP_DIM = 128
CHUNK = 128
SB_HEADS = 8
SB_HEAD_DIM = 64
Q_BLOCK = 128
XA_HEADS = 4
XA_HEAD_DIM = 256

ADAM_LR = 0.001
ADAM_B1 = 0.9
ADAM_B2 = 0.999
ADAM_EPS = 1e-08
ADAM_WD = 0.01
ADAM_STEP = 10

VMEM_LIMIT_V7X = 56 * 1024 * 1024
GELU_C0 = 0.7978845608028654
GELU_C1 = 0.044715


def _dot(a, b):
    return jnp.dot(a.astype(BF16), b.astype(BF16), preferred_element_type=F32)


def _dot_nt(a, b):
    return lax.dot_general(a.astype(BF16), b.astype(BF16), (((1,), (1,)), ((), ())),
                           preferred_element_type=F32)


def _dot_tn(a, b):
    return lax.dot_general(a.astype(BF16), b.astype(BF16), (((0,), (0,)), ((), ())),
                           preferred_element_type=F32)


def _rms(x, g):
    r = lax.rsqrt(jnp.mean(x * x, axis=-1, keepdims=True) + EPS)
    return x * r * g, r


def _rms_bwd(x, g, dy):
    r = lax.rsqrt(jnp.mean(x * x, axis=-1, keepdims=True) + EPS)
    xh = x * r
    gy = dy * g
    dx = r * (gy - xh * jnp.mean(gy * xh, axis=-1, keepdims=True))
    dg = jnp.sum(dy * xh, axis=0, keepdims=True)
    return dx, dg


def _sigmoid(x):
    return 1.0 / (1.0 + jnp.exp(-x))


def _gelu(x):
    t = jnp.tanh(GELU_C0 * (x + GELU_C1 * x * x * x))
    return 0.5 * x * (1.0 + t)


def _gelu_grad(x):
    t = jnp.tanh(GELU_C0 * (x + GELU_C1 * x * x * x))
    return 0.5 * (1.0 + t) + 0.5 * x * (1.0 - t * t) * GELU_C0 * (1.0 + 3.0 * GELU_C1 * x * x)


def _split_bf16(x):
    hi = x.astype(BF16)
    lo = (x - hi.astype(F32)).astype(BF16)
    return hi, lo


def _row_spec(tm, cols):
    return pl.BlockSpec((tm, cols), lambda i: (i, 0))


def _full_spec(shape, buffers=None):
    nd = len(shape)
    mode = None if buffers is None else pl.Buffered(buffers)
    return pl.BlockSpec(tuple(shape), lambda i: (0,) * nd, pipeline_mode=mode)


FFN_TOKEN_TILE = 512


def _token_tile(s):
    return min(256, s)


def _row_call(body, name, s, tiled_in, full_in, tiled_out, acc_out, gather=None, scatter=None, tm=None):
    tm = _token_tile(s) if tm is None else min(tm, s)
    steps = s // tm
    in_specs = [_row_spec(tm, a.shape[1]) for a in tiled_in] + [_full_spec(a.shape, buffers=1) for a in full_in]
    out_specs = [_row_spec(tm, c) for c, _ in tiled_out] + [_full_spec(sh) for sh, _ in acc_out]
    out_shape = [jax.ShapeDtypeStruct((s, c), dt) for c, dt in tiled_out]
    out_shape += [jax.ShapeDtypeStruct(sh, dt) for sh, dt in acc_out]
    operands = [*tiled_in, *full_in]
    scratch = []
    kernel_body = functools.partial(body)
    sent = gather if gather is not None else scatter
    if sent is not None:
        n_in, n_out = len(operands), len(out_shape)
        out_shape.append(jax.ShapeDtypeStruct(
            (N_DEV,) + sent.shape if gather is not None else sent.shape, sent.dtype))
        operands.append(sent)
        in_specs.append(_ANY)
        out_specs.append(_ANY)
        scratch = list(_AG_SCRATCH)

        def kernel_body(*refs):
            ins, sent_ref = refs[:n_in], refs[n_in]
            outs, landed_ref = refs[n_in + 1:n_in + 1 + n_out], refs[n_in + 1 + n_out]
            step = pl.program_id(0)
            if gather is not None:
                start, forward, finish = _gather_phases(sent_ref, landed_ref, *refs[-3:])
            else:
                start, finish = _scatter_phases(sent_ref, landed_ref, *refs[-3:])
            pl.when(step == 0)(start)
            body(*ins, *outs)
            if gather is not None:
                pl.when(step == (2 * steps) // 3)(forward)
            pl.when(step == steps - 1)(finish)

    return pl.pallas_call(
        kernel_body,
        name=name,
        grid=(steps,),
        in_specs=in_specs,
        out_specs=out_specs,
        out_shape=out_shape,
        scratch_shapes=scratch,
        compiler_params=pltpu.CompilerParams(
            dimension_semantics=("arbitrary",), vmem_limit_bytes=VMEM_LIMIT_V7X),
    )(*operands)


def _acc(ref, val):
    @pl.when(pl.program_id(0) == 0)
    def _():
        ref[...] = val

    @pl.when(pl.program_id(0) != 0)
    def _():
        ref[...] += val


def _ffn_fwd_body(x_ref, pre_ref, post_ref, wgt_ref, wut_ref, wd_ref,
                  h_ref, n_ref, a_ref, b_ref, f_ref):
    x = x_ref[...]
    n, _ = _rms(x, pre_ref[...])
    nb = n.astype(BF16)
    n_ref[...] = nb
    a = _dot_nt(nb, wgt_ref[...])
    b = _dot_nt(nb, wut_ref[...])
    a_ref[...] = a.astype(BF16)
    b_ref[...] = b.astype(BF16)
    hmid = a * _sigmoid(a) * b
    f = jnp.dot(hmid.astype(BF16), wd_ref[...], preferred_element_type=F32)
    f_ref[...] = f
    y, _ = _rms(f, post_ref[...])
    h_ref[...] = x + 0.5 * y


def _ffn_fwd(x, pre_g, post_g, wgt, wut, wd, name, gather=None):
    s, d = x.shape
    f = wgt.shape[0]
    return _row_call(_ffn_fwd_body, name, s, [x], [pre_g, post_g, wgt, wut, wd],
                     [(d, F32), (d, BF16), (f, BF16), (f, BF16), (d, F32)], [], gather=gather, tm=FFN_TOKEN_TILE)


def _ffn_bwd_body(dh_ref, x_ref, a_ref, b_ref, f_ref, pre_ref, post_ref, wgt_ref, wut_ref, wd_ref,
                  dx_ref, da_ref, db_ref, hm_ref, df_ref, dpre_ref, dpost_ref):
    dh = dh_ref[...]
    df, dpost = _rms_bwd(f_ref[...], post_ref[...], 0.5 * dh)
    dfb = df.astype(BF16)
    df_ref[...] = dfb
    dhmid = _dot_nt(dfb, wd_ref[...])
    a = a_ref[...].astype(F32)
    b = b_ref[...].astype(F32)
    sig = _sigmoid(a)
    sa = a * sig
    hm_ref[...] = (sa * b).astype(BF16)
    dab = (dhmid * b * sig * (1.0 + a * (1.0 - sig))).astype(BF16)
    dbb = (dhmid * sa).astype(BF16)
    da_ref[...] = dab
    db_ref[...] = dbb
    dn = _dot(dab, wgt_ref[...]) + _dot(dbb, wut_ref[...])
    dxn, dpre = _rms_bwd(x_ref[...], pre_ref[...], dn)
    dx_ref[...] = dh + dxn
    _acc(dpre_ref, dpre)
    _acc(dpost_ref, dpost)


def _ffn_bwd(dh, x, a, b, f, pre_g, post_g, wgt, wut, wd, name, scatter=None):
    s, d = x.shape
    ff = wgt.shape[0]
    return _row_call(_ffn_bwd_body, name, s, [dh, x, a, b, f], [pre_g, post_g, wgt, wut, wd],
                     [(d, F32), (ff, BF16), (ff, BF16), (ff, BF16), (d, BF16)],
                     [((1, d), F32), ((1, d), F32)], scatter=scatter)


def _inproj_fwd_body(h_ref, g_ref, wt_ref, n_ref, uv_ref, qkv_ref):
    n, _ = _rms(h_ref[...], g_ref[...])
    nb = n.astype(BF16)
    n_ref[...] = nb
    proj = _dot_nt(nb, wt_ref[...])
    nuv = uv_ref.shape[1]
    uv_ref[...] = proj[:, :nuv]
    qkv_ref[...] = proj[:, nuv:].astype(BF16)


def _inproj_fwd(h, g, w_in_t):
    s, d = h.shape
    sgu_w = SGU_GROUPS * GROUP_DIM
    sb_w = SB_HEADS * SB_HEAD_DIM
    return _row_call(_inproj_fwd_body, "inproj_fwd", s, [h], [g, w_in_t],
                     [(d, BF16), (2 * sgu_w, F32), (3 * sb_w, BF16)], [])


def _inproj_bwd_body(dh_ref, dproj_ref, h_ref, g_ref, wt_ref, dhout_ref, dg_ref):
    dn = _dot(dproj_ref[...], wt_ref[...])
    dhn, dg = _rms_bwd(h_ref[...], g_ref[...], dn)
    dhout_ref[...] = dh_ref[...] + dhn
    _acc(dg_ref, dg)


def _inproj_bwd(dh, dproj, h, g, w_in_t):
    s, d = h.shape
    return _row_call(_inproj_bwd_body, "inproj_bwd", s, [dh, dproj, h], [g, w_in_t],
                     [(d, F32)], [((1, d), F32)])


def _causal_w(ws_ref, g):
    row = lax.broadcasted_iota(jnp.int32, (CHUNK, CHUNK), 0)
    col = lax.broadcasted_iota(jnp.int32, (CHUNK, CHUNK), 1)
    return jnp.where(row >= col, ws_ref[g], 0.0), row >= col


def _group_norm(v):
    mu = jnp.mean(v, axis=-1, keepdims=True)
    d = v - mu
    rstd = lax.rsqrt(jnp.mean(d * d, axis=-1, keepdims=True) + EPS)
    return d * rstd, rstd


def _sgu_fwd_body(uv_ref, ng_ref, nb_ref, ws_ref, bs_ref, out_ref):
    width = SGU_GROUPS * GROUP_DIM
    for c in range(uv_ref.shape[0] // CHUNK):
        rows = pl.ds(c * CHUNK, CHUNK)
        for g in range(SGU_GROUPS):
            lanes = pl.ds(g * GROUP_DIM, GROUP_DIM)
            u = _gelu(uv_ref[rows, lanes])
            v = _gelu(uv_ref[rows, pl.ds(width + g * GROUP_DIM, GROUP_DIM)])
            vhat, _ = _group_norm(v)
            vn = vhat * ng_ref[:, lanes] + nb_ref[:, lanes]
            w, _ = _causal_w(ws_ref, g)
            mixed = _dot(w, vn) + bs_ref[g]
            out_ref[rows, lanes] = u * mixed


def _sgu_fwd(uv_pre, ng, nb, ws, bs):
    s = uv_pre.shape[0]
    return _row_call(_sgu_fwd_body, "sgu_fwd", s, [uv_pre], [ng, nb, ws, bs],
                     [(SGU_GROUPS * GROUP_DIM, F32)], [])[0]


def _sgu_bwd_body(uv_ref, do_ref, ng_ref, nb_ref, ws_ref, bs_ref,
                  duv_ref, dws_ref, dbs_ref, dng_ref, dnb_ref):
    width = SGU_GROUPS * GROUP_DIM

    @pl.when(pl.program_id(0) == 0)
    def _():
        dws_ref[...] = jnp.zeros_like(dws_ref)
        dbs_ref[...] = jnp.zeros_like(dbs_ref)
        dng_ref[...] = jnp.zeros_like(dng_ref)
        dnb_ref[...] = jnp.zeros_like(dnb_ref)

    for c in range(uv_ref.shape[0] // CHUNK):
        rows = pl.ds(c * CHUNK, CHUNK)
        for g in range(SGU_GROUPS):
            lanes = pl.ds(g * GROUP_DIM, GROUP_DIM)
            vlanes = pl.ds(width + g * GROUP_DIM, GROUP_DIM)
            u_pre = uv_ref[rows, lanes]
            v_pre = uv_ref[rows, vlanes]
            u = _gelu(u_pre)
            v = _gelu(v_pre)
            vhat, rstd = _group_norm(v)
            gain = ng_ref[:, lanes]
            vn = vhat * gain + nb_ref[:, lanes]
            w, causal = _causal_w(ws_ref, g)
            mixed = _dot(w, vn) + bs_ref[g]
            dout = do_ref[rows, lanes]
            du = dout * mixed
            dmixed = dout * u
            dbs_ref[g] += jnp.sum(dmixed, axis=1, keepdims=True)
            dws_ref[g] += jnp.where(causal, _dot_nt(dmixed, vn), 0.0)
            dvn = _dot_tn(w, dmixed)
            dng_ref[:, lanes] += jnp.sum(dvn * vhat, axis=0, keepdims=True)
            dnb_ref[:, lanes] += jnp.sum(dvn, axis=0, keepdims=True)
            dvh = dvn * gain
            dv = rstd * (dvh - jnp.mean(dvh, axis=-1, keepdims=True)
                         - vhat * jnp.mean(dvh * vhat, axis=-1, keepdims=True))
            duv_ref[rows, lanes] = (du * _gelu_grad(u_pre)).astype(BF16)
            duv_ref[rows, vlanes] = (dv * _gelu_grad(v_pre)).astype(BF16)


def _sgu_bwd(uv_pre, dout_a, ng, nb, ws, bs):
    s = uv_pre.shape[0]
    width = SGU_GROUPS * GROUP_DIM
    return _row_call(_sgu_bwd_body, "sgu_bwd", s, [uv_pre, dout_a], [ng, nb, ws, bs],
                     [(2 * width, BF16)],
                     [(ws.shape, F32), (bs.shape, F32), ((1, width), F32), ((1, width), F32)])


def _mesh_place():
    return lax.axis_index("x"), lax.axis_index("y"), lax.axis_index("c")


def _other_chips(mx, my):
    return [(1 - mx, my), (mx, 1 - my), (1 - mx, 1 - my)]


_ANY = pl.BlockSpec(memory_space=pl.ANY)
AG_SEMS = 7
_AG_SCRATCH = [pltpu.SemaphoreType.DMA((AG_SEMS,)), pltpu.SemaphoreType.DMA((AG_SEMS,)),
               pltpu.SemaphoreType.DMA(())]


def _gather_phases(x_ref, out_ref, send_sems, recv_sems, local_sem):
    mx, my, mc = _mesh_place()
    me, sibling = (mx, my, mc), (mx, my, 1 - mc)
    chips = _other_chips(mx, my)

    def slot(px, py, pc):
        return out_ref.at[4 * px + 2 * py + pc]

    def copy(k, block, to, src=None):
        return pltpu.make_async_remote_copy(
            src_ref=slot(*block) if src is None else src, dst_ref=slot(*block),
            send_sem=send_sems.at[k], recv_sem=recv_sems.at[k],
            device_id=to, device_id_type=MESH)

    mine = pltpu.make_async_copy(x_ref, slot(*me), local_sem)
    first = [copy(0, me, sibling, src=x_ref)]
    first += [copy(1 + j, me, (*chip, mc), src=x_ref) for j, chip in enumerate(chips)]
    passed = [copy(4 + j, (*chip, mc), sibling) for j, chip in enumerate(chips)]

    def start():
        mine.start()
        for cp in first:
            cp.start()

    def forward():
        for j, chip in enumerate(chips):
            copy(1 + j, (*chip, mc), me).wait_recv()
            passed[j].start()

    def finish():
        copy(0, sibling, me).wait_recv()
        for j, chip in enumerate(chips):
            copy(4 + j, (*chip, 1 - mc), me).wait_recv()
        for cp in first + passed:
            cp.wait_send()
        mine.wait()

    return start, forward, finish


def _all_gather(x, name):
    r, c = x.shape

    def body(x_ref, out_ref, send_sems, recv_sems, local_sem):
        start, forward, finish = _gather_phases(x_ref, out_ref, send_sems, recv_sems, local_sem)
        start()
        forward()
        finish()

    return pl.pallas_call(
        body,
        name=name,
        out_shape=jax.ShapeDtypeStruct((N_DEV, r, c), x.dtype),
        in_specs=[_ANY],
        out_specs=_ANY,
        scratch_shapes=list(_AG_SCRATCH),
    )(x)


def _scatter_phases(p_ref, out_ref, send_sems, recv_sems, local_sem):
    mx, my, mc = _mesh_place()
    me = 4 * mx + 2 * my + mc
    copies = []
    for k in range(1, N_DEV):
        tx, ty, tc = mx ^ ((k >> 2) & 1), my ^ ((k >> 1) & 1), mc ^ (k & 1)
        copies.append(pltpu.make_async_remote_copy(
            src_ref=p_ref.at[4 * tx + 2 * ty + tc], dst_ref=out_ref.at[me],
            send_sem=send_sems.at[k - 1], recv_sem=recv_sems.at[k - 1],
            device_id=(tx, ty, tc), device_id_type=MESH))
    mine = pltpu.make_async_copy(p_ref.at[me], out_ref.at[me], local_sem)

    def start():
        mine.start()
        for cp in copies:
            cp.start()

    def finish():
        for cp in copies:
            cp.wait()
        mine.wait()

    return start, finish


SB_DEAD = -105.0
HEADS_PER_TILE = 2
TILES_PER_STEP = 2
HEADS_PER_STEP = HEADS_PER_TILE * TILES_PER_STEP
STEP_LANES = TILES_PER_STEP * HEADS_PER_TILE * SB_HEAD_DIM
TILE_LANES = HEADS_PER_TILE * SB_HEAD_DIM
STACK_ROWS = HEADS_PER_STEP * Q_BLOCK
TILE_ROWS = HEADS_PER_TILE * Q_BLOCK
SB_FORWARD_LEAD = 12


def _stack_heads(x):
    lane = lax.broadcasted_iota(jnp.int32, x.shape, 1)
    zero = jnp.zeros_like(x)
    return jnp.concatenate(
        [jnp.where(lane // SB_HEAD_DIM == h, x, zero) for h in range(HEADS_PER_STEP)], axis=0)


def _unstack_tile(x):
    first = lax.broadcasted_iota(jnp.int32, (Q_BLOCK, TILE_LANES), 1) < SB_HEAD_DIM
    return jnp.where(first, x[:Q_BLOCK], x[Q_BLOCK:])


def _sb_logs(qs, k, diagonal):
    z = _dot_nt(qs, k) * (SB_HEAD_DIM ** -0.5)
    sp = jnp.log1p(jnp.exp(-jnp.abs(z)))
    log_beta = jnp.minimum(z, 0.0) - sp
    log_1m_raw = -jnp.maximum(z, 0.0) - sp
    if not diagonal:
        return None, log_beta, log_1m_raw, log_1m_raw
    row = lax.broadcasted_iota(jnp.int32, z.shape, 0)
    col = lax.broadcasted_iota(jnp.int32, z.shape, 1)
    strict = col < jnp.bitwise_and(row, Q_BLOCK - 1)
    return strict, log_beta, log_1m_raw, jnp.where(strict, log_1m_raw, 0.0)


def _masked(strict, x):
    return x if strict is None else jnp.where(strict, x, 0.0)


def _key_sums(x, pick):
    hi, lo = _split_bf16(x)
    both = jnp.dot(jnp.concatenate([hi, lo], axis=0), pick, preferred_element_type=F32)
    return both[:x.shape[0]] + both[x.shape[0]:]


def _key_order():
    row = lax.broadcasted_iota(jnp.int32, (Q_BLOCK, Q_BLOCK), 0)
    col = lax.broadcasted_iota(jnp.int32, (Q_BLOCK, Q_BLOCK), 1)
    return row, col


def _sb_fwd_body(q_ref, k_ref, v_ref, shard_ref, o_ref, tot_ref, cnt_ref, gathered_ref,
                 acc_ref, send_sems, recv_sems, local_sem):
    grp, qb = pl.program_id(0), pl.program_id(1)
    last_grp, last_qb = pl.num_programs(0) - 1, pl.num_programs(1) - 1
    ag_start, ag_forward, ag_finish = _gather_phases(shard_ref, gathered_ref, send_sems, recv_sems, local_sem)
    pl.when(jnp.logical_and(grp == 0, qb == 0))(ag_start)

    qs = _stack_heads(q_ref[...])
    row, col = _key_order()
    later = (row > col).astype(BF16)

    def block(i, c, diagonal):
        rows = pl.ds(pl.multiple_of((qb - i) * Q_BLOCK, Q_BLOCK), Q_BLOCK)
        strict, log_beta, _, log_1m = _sb_logs(qs, k_ref[rows, :], diagonal)
        a = _masked(strict, jnp.exp(log_beta + _key_sums(log_1m, later) + c)).astype(BF16)
        for t in range(TILES_PER_STEP):
            part = jnp.dot(a[t * TILE_ROWS:(t + 1) * TILE_ROWS], v_ref[rows, t * TILE_LANES:(t + 1) * TILE_LANES],
                           preferred_element_type=F32)
            if diagonal:
                acc_ref[t] = part
            else:
                acc_ref[t] += part
        return c + jnp.sum(log_1m, axis=1, keepdims=True)

    c = block(0, jnp.zeros((STACK_ROWS, 1), F32), True)

    def alive(carry):
        i, c = carry
        return jnp.logical_and(i <= qb, jnp.max(c) > SB_DEAD)

    def step(carry):
        i, c = carry
        return i + 1, block(i, c, False)

    n, c = lax.while_loop(alive, step, (jnp.int32(1), c))
    for t in range(TILES_PER_STEP):
        o_ref[:, t * TILE_LANES:(t + 1) * TILE_LANES] = _unstack_tile(acc_ref[t])
    for h in range(HEADS_PER_STEP):
        tot_ref[h] = c[h * Q_BLOCK:(h + 1) * Q_BLOCK]
    cnt_ref[grp, qb] = n.astype(F32)
    pl.when(jnp.logical_and(grp == last_grp, qb == jnp.maximum(last_qb - SB_FORWARD_LEAD, 0)))(ag_forward)
    pl.when(jnp.logical_and(grp == last_grp, qb == last_qb))(ag_finish)


def _sb_fwd(qkv, shard):
    s = qkv.shape[0]
    groups = SB_HEADS // HEADS_PER_STEP
    nq = s // Q_BLOCK
    return pl.pallas_call(
        functools.partial(_sb_fwd_body),
        name="sb_fwd",
        grid=(groups, nq),
        in_specs=[pl.BlockSpec((Q_BLOCK, STEP_LANES), lambda g, i: (i, g)),
                  pl.BlockSpec((s, STEP_LANES), lambda g, i: (0, groups + g)),
                  pl.BlockSpec((s, STEP_LANES), lambda g, i: (0, 2 * groups + g)),
                  _ANY],
        out_specs=[pl.BlockSpec((Q_BLOCK, STEP_LANES), lambda g, i: (i, g)),
                   pl.BlockSpec((HEADS_PER_STEP, Q_BLOCK, 1), lambda g, i: (g, i, 0)),
                   pl.BlockSpec(memory_space=pltpu.SMEM),
                   _ANY],
        out_shape=[jax.ShapeDtypeStruct((s, SB_HEADS * SB_HEAD_DIM), F32),
                   jax.ShapeDtypeStruct((SB_HEADS, s, 1), F32),
                   jax.ShapeDtypeStruct((groups, nq), F32),
                   jax.ShapeDtypeStruct((N_DEV,) + shard.shape, shard.dtype)],
        scratch_shapes=[pltpu.VMEM((TILES_PER_STEP, TILE_ROWS, TILE_LANES), F32)] + list(_AG_SCRATCH),
        compiler_params=pltpu.CompilerParams(
            dimension_semantics=("arbitrary", "arbitrary"), vmem_limit_bytes=VMEM_LIMIT_V7X),
    )(qkv, qkv, qkv, shard)


def _sb_bwd_body(cnt_ref, q_ref, k_ref, v_ref, tot_ref, do_ref, part_ref, dq_ref, dk_ref, dv_ref, recv_ref,
                 acc_ref, send_sems, recv_sems, local_sem):
    grp, qb = pl.program_id(0), pl.program_id(1)
    last_grp, last_qb = pl.num_programs(0) - 1, pl.num_programs(1) - 1
    rs_start, rs_finish = _scatter_phases(part_ref, recv_ref, send_sems, recv_sems, local_sem)
    pl.when(jnp.logical_and(grp == 0, qb == 0))(rs_start)

    @pl.when(qb == 0)
    def _():
        dk_ref[...] = jnp.zeros_like(dk_ref)
        dv_ref[...] = jnp.zeros_like(dv_ref)

    acc_ref[...] = jnp.zeros_like(acc_ref)
    qs = _stack_heads(q_ref[...])
    dos = _stack_heads(do_ref[...].astype(BF16))
    tot = jnp.concatenate([tot_ref[h] for h in range(HEADS_PER_STEP)], axis=0)
    row, col = _key_order()
    up_to = (row <= col).astype(BF16)
    earlier = (row < col).astype(BF16)
    scale = SB_HEAD_DIM ** -0.5
    n = jnp.clip(cnt_ref[grp, qb].astype(jnp.int32), 1, qb + 1)

    def block(kb, c, ce, diagonal):
        rows = pl.ds(pl.multiple_of(kb * Q_BLOCK, Q_BLOCK), Q_BLOCK)
        k = k_ref[rows, :]
        strict, log_beta, log_1m_raw, log_1m = _sb_logs(qs, k, diagonal)
        suffix = tot - c - _key_sums(log_1m, up_to)
        a = _masked(strict, jnp.exp(log_beta + suffix))
        de = _dot_nt(dos, v_ref[rows, :]) * a
        before = ce + _key_sums(de, earlier)
        dz = _masked(strict, de * jnp.exp(log_1m_raw) - before * jnp.exp(log_beta)).astype(BF16)
        for t in range(TILES_PER_STEP):
            acc_ref[t] += jnp.dot(dz[t * TILE_ROWS:(t + 1) * TILE_ROWS], k[:, t * TILE_LANES:(t + 1) * TILE_LANES],
                                  preferred_element_type=F32)
        dk_ref[rows, :] += _dot_tn(dz, qs) * scale
        dv_ref[rows, :] += _dot_tn(a, dos)
        return c + jnp.sum(log_1m, axis=1, keepdims=True), ce + jnp.sum(de, axis=1, keepdims=True)

    def step(i, carry):
        return block(qb - n + 1 + i, *carry, False)

    zc = jnp.zeros((STACK_ROWS, 1), F32)
    c, ce = lax.fori_loop(0, n - 1, step, (zc, zc))
    block(qb, c, ce, True)
    for t in range(TILES_PER_STEP):
        dq_ref[:, t * TILE_LANES:(t + 1) * TILE_LANES] = _unstack_tile(acc_ref[t]) * scale
    pl.when(jnp.logical_and(grp == last_grp, qb == last_qb))(rs_finish)


def _sb_bwd(cnt, qkv, tot, dout_b, parts):
    s = qkv.shape[0]
    groups = SB_HEADS // HEADS_PER_STEP
    return pl.pallas_call(
        functools.partial(_sb_bwd_body),
        name="sb_bwd",
        grid=(groups, s // Q_BLOCK),
        in_specs=[pl.BlockSpec(memory_space=pltpu.SMEM),
                  pl.BlockSpec((Q_BLOCK, STEP_LANES), lambda g, i: (i, g)),
                  pl.BlockSpec((s, STEP_LANES), lambda g, i: (0, groups + g)),
                  pl.BlockSpec((s, STEP_LANES), lambda g, i: (0, 2 * groups + g)),
                  pl.BlockSpec((HEADS_PER_STEP, Q_BLOCK, 1), lambda g, i: (g, i, 0)),
                  pl.BlockSpec((Q_BLOCK, STEP_LANES), lambda g, i: (i, g)),
                  _ANY],
        out_specs=[pl.BlockSpec((Q_BLOCK, STEP_LANES), lambda g, i: (i, g)),
                   pl.BlockSpec((s, STEP_LANES), lambda g, i: (0, g)),
                   pl.BlockSpec((s, STEP_LANES), lambda g, i: (0, g)),
                   _ANY],
        out_shape=[jax.ShapeDtypeStruct((s, SB_HEADS * SB_HEAD_DIM), F32)] * 3
        + [jax.ShapeDtypeStruct(parts.shape, parts.dtype)],
        scratch_shapes=[pltpu.VMEM((TILES_PER_STEP, TILE_ROWS, TILE_LANES), F32)] + list(_AG_SCRATCH),
        compiler_params=pltpu.CompilerParams(
            dimension_semantics=("arbitrary", "arbitrary"), vmem_limit_bytes=VMEM_LIMIT_V7X),
    )(cnt, qkv, qkv, qkv, tot, dout_b, parts)


def _outproj_fwd_body(oa_ref, ob_ref, h_ref, ga_ref, gb_ref, gpost_ref, w_ref,
                      merged_ref, mo_ref, hout_ref):
    half = oa_ref.shape[1]
    ma, _ = _rms(oa_ref[...], ga_ref[...])
    mb, _ = _rms(ob_ref[...], gb_ref[...])
    mab = ma.astype(BF16)
    mbb = mb.astype(BF16)
    merged_ref[:, :half] = mab
    merged_ref[:, half:] = mbb
    mo = (jnp.dot(mab, w_ref[:half, :], preferred_element_type=F32)
          + jnp.dot(mbb, w_ref[half:, :], preferred_element_type=F32))
    mo_ref[...] = mo
    y, _ = _rms(mo, gpost_ref[...])
    hout_ref[...] = h_ref[...] + y


def _outproj_fwd(out_a, out_b, h, ga, gb, gpost, w_out):
    s, d = h.shape
    return _row_call(_outproj_fwd_body, "outproj_fwd", s, [out_a, out_b, h], [ga, gb, gpost, w_out],
                     [(d, BF16), (d, F32), (d, F32)], [])


def _outproj_bwd_body(dh_ref, mo_ref, oa_ref, ob_ref, ga_ref, gb_ref, gpost_ref, w_ref,
                      dmo_ref, doa_ref, dob_ref, dga_ref, dgb_ref, dgpost_ref):
    half = oa_ref.shape[1]
    dmo, dgpost = _rms_bwd(mo_ref[...], gpost_ref[...], dh_ref[...])
    dmob = dmo.astype(BF16)
    dmo_ref[...] = dmob
    dma = _dot_nt(dmob, w_ref[:half, :])
    dmb = _dot_nt(dmob, w_ref[half:, :])
    doa, dga = _rms_bwd(oa_ref[...], ga_ref[...], dma)
    dob, dgb = _rms_bwd(ob_ref[...], gb_ref[...], dmb)
    doa_ref[...] = doa
    dob_ref[...] = dob
    _acc(dga_ref, dga)
    _acc(dgb_ref, dgb)
    _acc(dgpost_ref, dgpost)


def _outproj_bwd(dh, mo, out_a, out_b, ga, gb, gpost, w_out):
    s, d = dh.shape
    half = out_a.shape[1]
    return _row_call(_outproj_bwd_body, "outproj_bwd", s, [dh, mo, out_a, out_b], [ga, gb, gpost, w_out],
                     [(d, BF16), (half, F32), (half, F32)],
                     [((1, half), F32), ((1, half), F32), ((1, d), F32)])


def _kv_fwd_body(mem_ref, g_ref, wt_ref, memn_ref, kv_ref):
    n, _ = _rms(mem_ref[...], g_ref[...])
    nb = n.astype(BF16)
    memn_ref[...] = nb
    kv_ref[...] = _dot_nt(nb, wt_ref[...]).astype(BF16)


def _kv_fwd(mem, g, w_kv_t):
    m, d = mem.shape
    return _row_call(_kv_fwd_body, "kv_fwd", m, [mem], [g, w_kv_t], [(d, BF16), (w_kv_t.shape[0], BF16)], [])


def _kv_bwd_body(dkv_ref, mem_ref, memn_ref, g_ref, wt_ref, dwt_ref, dg_ref):
    dkvb = dkv_ref[...].astype(BF16)
    dwt_ref[...] = _dot_tn(dkvb, memn_ref[...]).astype(BF16)
    dmemn = _dot(dkvb, wt_ref[...])
    _, dg = _rms_bwd(mem_ref[...], g_ref[...], dmemn)
    dg_ref[...] = dg


def _kv_bwd(dkv, mem, memn, g, w_kv_t):
    m, d = mem.shape
    return pl.pallas_call(
        functools.partial(_kv_bwd_body),
        name="kv_bwd",
        out_shape=[jax.ShapeDtypeStruct(w_kv_t.shape, BF16), jax.ShapeDtypeStruct((1, d), F32)],
        compiler_params=pltpu.CompilerParams(vmem_limit_bytes=VMEM_LIMIT_V7X),
    )(dkv, mem, memn, g, w_kv_t)


def _xa_fwd_body(h_ref, gpre_ref, gpost_ref, wq_ref, wo_ref, kv_ref,
                 n_ref, q_ref, o_ref, c_ref, hout_ref):
    h = h_ref[...]
    d = h.shape[1]
    n, _ = _rms(h, gpre_ref[...])
    nb = n.astype(BF16)
    n_ref[...] = nb
    qb = jnp.dot(nb, wq_ref[...], preferred_element_type=F32).astype(BF16)
    q_ref[...] = qb
    for hd in range(XA_HEADS):
        lanes = slice(hd * XA_HEAD_DIM, (hd + 1) * XA_HEAD_DIM)
        k = kv_ref[:, lanes]
        v = kv_ref[:, d + hd * XA_HEAD_DIM:d + (hd + 1) * XA_HEAD_DIM]
        logits = _dot_nt(qb[:, lanes], k) * (XA_HEAD_DIM ** -0.5)
        e = jnp.exp(logits - jnp.max(logits, axis=-1, keepdims=True))
        p = e / jnp.sum(e, axis=-1, keepdims=True)
        o_ref[:, lanes] = jnp.dot(p.astype(BF16), v, preferred_element_type=F32).astype(BF16)
    c = jnp.dot(o_ref[...], wo_ref[...], preferred_element_type=F32)
    c_ref[...] = c
    y, _ = _rms(c, gpost_ref[...])
    hout_ref[...] = h + y


def _xa_fwd(h, gpre, gpost, wq, wo, kv):
    s, d = h.shape
    return _row_call(_xa_fwd_body, "xa_fwd", s, [h], [gpre, gpost, wq, wo, kv],
                     [(d, BF16), (d, BF16), (d, BF16), (d, F32), (d, F32)], [])


def _xa_bwd_body(dh_ref, h_ref, c_ref, q_ref, o_ref, gpre_ref, gpost_ref, wq_ref, wo_ref, kv_ref,
                 dhout_ref, dc_ref, dq_ref, dkv_ref, dgpre_ref, dgpost_ref):
    dh = dh_ref[...]
    d = dh.shape[1]
    scale = XA_HEAD_DIM ** -0.5
    dc, dgpost = _rms_bwd(c_ref[...], gpost_ref[...], dh)
    dcb = dc.astype(BF16)
    dc_ref[...] = dcb
    dob = _dot_nt(dcb, wo_ref[...]).astype(BF16)

    @pl.when(pl.program_id(0) == 0)
    def _():
        dkv_ref[...] = jnp.zeros_like(dkv_ref)

    for hd in range(XA_HEADS):
        lanes = slice(hd * XA_HEAD_DIM, (hd + 1) * XA_HEAD_DIM)
        vlanes = slice(d + hd * XA_HEAD_DIM, d + (hd + 1) * XA_HEAD_DIM)
        qh = q_ref[:, lanes]
        k = kv_ref[:, lanes]
        v = kv_ref[:, vlanes]
        logits = _dot_nt(qh, k) * scale
        e = jnp.exp(logits - jnp.max(logits, axis=-1, keepdims=True))
        p = e / jnp.sum(e, axis=-1, keepdims=True)
        doh = dob[:, lanes]
        dp = _dot_nt(doh, v)
        dl = (p * (dp - jnp.sum(dp * p, axis=-1, keepdims=True)) * scale).astype(BF16)
        dq_ref[:, lanes] = jnp.dot(dl, k, preferred_element_type=F32).astype(BF16)
        dkv_ref[:, lanes] += _dot_tn(dl, qh)
        dkv_ref[:, vlanes] += _dot_tn(p, doh)
    dn = _dot_nt(dq_ref[...], wq_ref[...])
    dhn, dgpre = _rms_bwd(h_ref[...], gpre_ref[...], dn)
    dhout_ref[...] = dh + dhn
    _acc(dgpre_ref, dgpre)
    _acc(dgpost_ref, dgpost)


def _xa_bwd(dh, h, c, q, o, gpre, gpost, wq, wo, kv):
    s, d = h.shape
    return _row_call(_xa_bwd_body, "xa_bwd", s, [dh, h, c, q, o], [gpre, gpost, wq, wo, kv],
                     [(d, F32), (d, BF16), (d, BF16)],
                     [(kv.shape, F32), ((1, d), F32), ((1, d), F32)])


def _final_body(h_ref, t_ref, g_ref, dh_ref, loss_ref, dg_ref):
    h = h_ref[...]
    d = h.shape[1]
    y, _ = _rms(h, g_ref[...])
    err = y - t_ref[...]
    part = (0.5 / d) * jnp.sum(jnp.sum(err * err, axis=1, keepdims=True), axis=0, keepdims=True)
    dh, dg = _rms_bwd(h, g_ref[...], err * (1.0 / d))
    dh_ref[...] = dh
    _acc(loss_ref, part)
    _acc(dg_ref, dg)


def _final(h, g, target):
    s, d = h.shape
    return _row_call(_final_body, "final_loss", s, [h, target], [g],
                     [(d, F32)], [((1, 1), F32), ((1, d), F32)])


def _largest_tile(n, cap):
    best = 128
    for t in range(128, cap + 1, 128):
        if n % t == 0:
            best = t
    return best


def _mm_tn(a, bs, name):
    s, k = a.shape
    n = bs[0].shape[1]
    nb = len(bs)
    ts = min(2048, s)
    tk = _largest_tile(k, 1536)
    tn = _largest_tile(n, 1536 // nb)

    steps = s // ts

    def body(a_ref, *refs):
        b_refs, o_refs, acc_refs = refs[:nb], refs[nb:2 * nb], refs[2 * nb:]
        at = a_ref[...]
        t = pl.program_id(2)

        @pl.when(t == 0)
        def _():
            for acc_ref in acc_refs:
                acc_ref[...] = jnp.zeros_like(acc_ref)

        for b_ref, acc_ref in zip(b_refs, acc_refs):
            acc_ref[...] += _dot_tn(at, b_ref[...])

        @pl.when(t == steps - 1)
        def _():
            for o_ref, acc_ref in zip(o_refs, acc_refs):
                o_ref[...] = acc_ref[...].astype(BF16)

    return pl.pallas_call(
        body,
        name=name,
        grid=(k // tk, n // tn, steps),
        in_specs=[pl.BlockSpec((ts, tk), lambda i, j, t: (t, i))]
        + [pl.BlockSpec((ts, tn), lambda i, j, t: (t, j))] * nb,
        out_specs=[pl.BlockSpec((tk, tn), lambda i, j, t: (i, j))] * nb,
        out_shape=[jax.ShapeDtypeStruct((k, n), BF16)] * nb,
        scratch_shapes=[pltpu.VMEM((tk, tn), F32)] * nb,
        compiler_params=pltpu.CompilerParams(
            dimension_semantics=("arbitrary", "arbitrary", "arbitrary"),
            vmem_limit_bytes=VMEM_LIMIT_V7X),
    )(a, *bs)


_SMALL_SHAPES = {
    "sgu_norm_g": (1, SGU_GROUPS * GROUP_DIM),
    "sgu_norm_b": (1, SGU_GROUPS * GROUP_DIM),
    "sgu_w_s": (SGU_GROUPS, CHUNK, CHUNK),
    "sgu_b_s": (SGU_GROUPS, CHUNK, 1),
}


def _small_views(small):
    return {n: v.reshape(_SMALL_SHAPES.get(n, v.shape)) for n, v in small.items()}


def _small_unviews(views, like):
    return {n: v.reshape(like[n].shape) for n, v in views.items()}


def _unpack_rows(gathered, names, shard_rows):
    out, off = {}, 0
    for n in names:
        rows = shard_rows[n]
        out[n] = gathered[:, off:off + rows, :].reshape(N_DEV * rows, gathered.shape[2])
        off += rows
    return out


def _row_tile(r, cap):
    best = 16
    for t in range(16, cap + 1, 16):
        if r % t == 0:
            best = t
    return best


def _sum_received(received, name):
    _, r, c = received.shape
    tr = _row_tile(r, 1024)

    def body(rc_ref, g_ref):
        g = rc_ref[0].astype(F32)
        for t in range(1, N_DEV):
            g = g + rc_ref[t].astype(F32)
        g_ref[...] = g

    return pl.pallas_call(
        body, name=name, grid=(r // tr,),
        in_specs=[pl.BlockSpec((N_DEV, tr, c), lambda i: (0, i, 0))],
        out_specs=pl.BlockSpec((tr, c), lambda i: (i, 0)),
        out_shape=jax.ShapeDtypeStruct((r, c), F32),
    )(received)


def _local_step(x, mem, target, small, big, w_in_shard, late_shard, late_names, shard_rows):
    sm, w = small, dict(big)
    d_model = x.shape[1]
    h1, n1, a1, b1, f1, w_in_all = _ffn_fwd(
        x, sm["ffn1_pre_g"], sm["ffn1_post_g"], w["ffn1_w_gate"], w["ffn1_w_up"], w["ffn1_w_down"],
        "ffn1_fwd", gather=w_in_shard)
    w["w_in"] = w_in_all.reshape(-1, d_model)
    n2, uv_pre, qkv = _inproj_fwd(h1, sm["mix_pre_g"], w["w_in"])
    out_a = _sgu_fwd(uv_pre, sm["sgu_norm_g"], sm["sgu_norm_b"], sm["sgu_w_s"], sm["sgu_b_s"])
    out_b, tot, cnt, late = _sb_fwd(qkv, late_shard)
    w.update(_unpack_rows(late, late_names, shard_rows))
    merged, mo, h2 = _outproj_fwd(out_a, out_b, h1, sm["sgu_out_g"], sm["sb_out_g"],
                                  sm["mix_post_g"], w["w_out"])
    memn, kv = _kv_fwd(mem, sm["mem_norm_g"], w["xa_w_kv"])
    n3, qx, ox, cx, h3 = _xa_fwd(h2, sm["xa_pre_g"], sm["xa_post_g"], w["xa_w_q"], w["xa_w_o"], kv)
    h4, n4, a2, b2, f2 = _ffn_fwd(h3, sm["ffn2_pre_g"], sm["ffn2_post_g"],
                                  w["ffn2_w_gate"], w["ffn2_w_up"], w["ffn2_w_down"], "ffn2_fwd")
    dh4, loss, dg_final = _final(h4, sm["final_norm_g"], target)

    gs, gw = {"final_norm_g": dg_final}, {}
    dh3, da2, db2, hm2, df2, gs["ffn2_pre_g"], gs["ffn2_post_g"] = _ffn_bwd(
        dh4, h3, a2, b2, f2, sm["ffn2_pre_g"], sm["ffn2_post_g"],
        w["ffn2_w_gate"], w["ffn2_w_up"], w["ffn2_w_down"], "ffn2_bwd")
    gw["ffn2_w_gate"], = _mm_tn(da2, [n4], "ffn2_dw_gate")
    gw["ffn2_w_up"], = _mm_tn(db2, [n4], "ffn2_dw_up")
    gw["ffn2_w_down"], = _mm_tn(hm2, [df2], "ffn2_dw_down")

    dh2, dc, dqx, dkv, gs["xa_pre_g"], gs["xa_post_g"] = _xa_bwd(
        dh3, h2, cx, qx, ox, sm["xa_pre_g"], sm["xa_post_g"], w["xa_w_q"], w["xa_w_o"], kv)
    gw["xa_w_o"], = _mm_tn(ox, [dc], "xa_dw_o")
    gw["xa_w_q"], = _mm_tn(n3, [dqx], "xa_dw_q")
    gw["xa_w_kv"], gs["mem_norm_g"] = _kv_bwd(dkv, mem, memn, sm["mem_norm_g"], w["xa_w_kv"])

    dmo, dout_a, dout_b, gs["sgu_out_g"], gs["sb_out_g"], gs["mix_post_g"] = _outproj_bwd(
        dh2, mo, out_a, out_b, sm["sgu_out_g"], sm["sb_out_g"], sm["mix_post_g"], w["w_out"])
    gw["w_out"], = _mm_tn(merged, [dmo], "mix_dw_out")
    parts = jnp.concatenate(
        [gw.pop(n).reshape(N_DEV, -1, d_model).astype(BF16) for n in late_names], axis=1)
    dq, dk, dv, received = _sb_bwd(cnt, qkv, tot, dout_b, parts)
    duv, gs["sgu_w_s"], gs["sgu_b_s"], gs["sgu_norm_g"], gs["sgu_norm_b"] = _sgu_bwd(
        uv_pre, dout_a, sm["sgu_norm_g"], sm["sgu_norm_b"], sm["sgu_w_s"], sm["sgu_b_s"])
    dproj = jnp.concatenate([duv] + [t.astype(BF16) for t in (dq, dk, dv)], axis=1)
    dh1, gs["mix_pre_g"] = _inproj_bwd(dh2, dproj, h1, sm["mix_pre_g"], w["w_in"])
    dw_in, = _mm_tn(dproj, [n2], "mix_dw_in")

    dx, da1, db1, hm1, df1, gs["ffn1_pre_g"], gs["ffn1_post_g"], w_in_received = _ffn_bwd(
        dh1, x, a1, b1, f1, sm["ffn1_pre_g"], sm["ffn1_post_g"],
        w["ffn1_w_gate"], w["ffn1_w_up"], w["ffn1_w_down"], "ffn1_bwd",
        scatter=dw_in.reshape(N_DEV, -1, d_model))
    gw["ffn1_w_gate"], = _mm_tn(da1, [n1], "ffn1_dw_gate")
    gw["ffn1_w_up"], = _mm_tn(db1, [n1], "ffn1_dw_up")
    gw["ffn1_w_down"], = _mm_tn(hm1, [df1], "ffn1_dw_down")
    return loss, dx, gs, gw, _sum_received(w_in_received, "rs_sum_w_in"), _sum_received(received, "rs_sum_late")


def _pair_exchange(p4):
    nchip, _, r, c = p4.shape

    def body(p_ref, out_ref, send_sems, recv_sems):
        mx, my, mc = _mesh_place()
        copies = [pltpu.make_async_remote_copy(
            src_ref=p_ref.at[j, 1 - mc], dst_ref=out_ref.at[j],
            send_sem=send_sems.at[j], recv_sem=recv_sems.at[j],
            device_id=(mx, my, 1 - mc), device_id_type=MESH) for j in range(nchip)]
        for cp in copies:
            cp.start()
        for cp in copies:
            cp.wait()

    return pl.pallas_call(
        body,
        name="rs_pair_exchange",
        out_shape=jax.ShapeDtypeStruct((nchip, r, c), p4.dtype),
        in_specs=[_ANY],
        out_specs=_ANY,
        scratch_shapes=[pltpu.SemaphoreType.DMA((nchip,)), pltpu.SemaphoreType.DMA((nchip,))],
    )(p4)


def _chip_exchange(q):
    _, r, c = q.shape

    def body(q_ref, out_ref, send_sems, recv_sems):
        mx, my, mc = _mesh_place()
        copies = [pltpu.make_async_remote_copy(
            src_ref=q_ref.at[2 * cx + cy], dst_ref=out_ref.at[k],
            send_sem=send_sems.at[k], recv_sem=recv_sems.at[k],
            device_id=(cx, cy, mc), device_id_type=MESH)
            for k, (cx, cy) in enumerate(_other_chips(mx, my))]
        for cp in copies:
            cp.start()
        for cp in copies:
            cp.wait()

    return pl.pallas_call(
        body,
        name="rs_chip_exchange",
        out_shape=jax.ShapeDtypeStruct((3, r, c), q.dtype),
        in_specs=[_ANY],
        out_specs=_ANY,
        scratch_shapes=[pltpu.SemaphoreType.DMA((3,)), pltpu.SemaphoreType.DMA((3,))],
    )(q)


def _rs_row_tile(r):
    return _row_tile(r, 1024)


def _pair_sum(place, p4, recv_a):
    nchip, _, r, c = p4.shape
    tr = _rs_row_tile(r)

    def body(place_ref, p_ref, a_ref, q_ref):
        q_ref[0] = (p_ref[0, 0].astype(F32) + a_ref[0].astype(F32)).astype(BF16)

    return pl.pallas_call(
        body,
        name="rs_pair_sum",
        grid_spec=pltpu.PrefetchScalarGridSpec(
            num_scalar_prefetch=1,
            grid=(nchip, r // tr),
            in_specs=[pl.BlockSpec((1, 1, tr, c), lambda j, i, pref: (j, pref[0], i, 0)),
                      pl.BlockSpec((1, tr, c), lambda j, i, pref: (j, i, 0))],
            out_specs=pl.BlockSpec((1, tr, c), lambda j, i, pref: (j, i, 0)),
        ),
        out_shape=jax.ShapeDtypeStruct((nchip, r, c), BF16),
    )(place, p4, recv_a)


def _rs_final(place, p4, recv_a, recv_b):
    _, _, r, c = p4.shape
    tr = _rs_row_tile(r)

    def body(place_ref, p_ref, a_ref, b_ref, g_ref):
        g = p_ref[0, 0].astype(F32) + a_ref[0].astype(F32)
        for k in range(3):
            g = g + b_ref[k].astype(F32)
        g_ref[...] = g

    return pl.pallas_call(
        body,
        name="rs_final_sum",
        grid_spec=pltpu.PrefetchScalarGridSpec(
            num_scalar_prefetch=1,
            grid=(r // tr,),
            in_specs=[pl.BlockSpec((1, 1, tr, c), lambda i, pref: (pref[1], pref[0], i, 0)),
                      pl.BlockSpec((1, tr, c), lambda i, pref: (pref[1], i, 0)),
                      pl.BlockSpec((3, tr, c), lambda i, pref: (0, i, 0))],
            out_specs=pl.BlockSpec((tr, c), lambda i, pref: (i, 0)),
        ),
        out_shape=jax.ShapeDtypeStruct((r, c), F32),
    )(place, p4, recv_a, recv_b)


def _adamw_math(w, g, m, v):
    m = ADAM_B1 * m + (1.0 - ADAM_B1) * g
    v = ADAM_B2 * v + (1.0 - ADAM_B2) * (g * g)
    m_hat = m / (1.0 - ADAM_B1 ** ADAM_STEP)
    v_hat = v / (1.0 - ADAM_B2 ** ADAM_STEP)
    delta = -ADAM_LR * (m_hat / (jnp.sqrt(v_hat) + ADAM_EPS) + ADAM_WD * w)
    return delta, m, v


def _adamw(w, g, m, v, name):
    r, c = w.shape
    tr = r if r <= 512 else 256

    def body(w_ref, g_ref, m_ref, v_ref, d_ref, mo_ref, vo_ref):
        d_ref[...], mo_ref[...], vo_ref[...] = _adamw_math(w_ref[...], g_ref[...], m_ref[...], v_ref[...])

    spec = pl.BlockSpec((tr, c), lambda i: (i, 0))
    out = jax.ShapeDtypeStruct((r, c), F32)
    return pl.pallas_call(
        body, name=name, grid=(r // tr,), in_specs=[spec] * 4, out_specs=[spec] * 3,
        out_shape=[out] * 3,
    )(w, g, m, v)


def _small_sum_adamw(gathered, w, m, v):
    _, r, c = gathered.shape

    def body(ga_ref, w_ref, m_ref, v_ref, g_ref, d_ref, mo_ref, vo_ref):
        g = ga_ref[0]
        for k in range(1, N_DEV):
            g = g + ga_ref[k]
        g_ref[...] = g
        d_ref[...], mo_ref[...], vo_ref[...] = _adamw_math(w_ref[...], g, m_ref[...], v_ref[...])

    out = jax.ShapeDtypeStruct((r, c), F32)
    return pl.pallas_call(body, name="small_sum_adamw", out_shape=[out] * 4)(gathered, w, m, v)


_WEIGHTS = ["ffn1_pre_g", "ffn1_post_g", "ffn1_w_gate", "ffn1_w_up", "ffn1_w_down", "mix_pre_g",
            "mix_post_g", "w_in", "sgu_norm_g", "sgu_norm_b", "sgu_w_s", "sgu_b_s", "sgu_out_g",
            "sb_out_g", "w_out", "xa_pre_g", "xa_post_g", "mem_norm_g", "xa_w_q", "xa_w_kv", "xa_w_o",
            "ffn2_pre_g", "ffn2_post_g", "ffn2_w_gate", "ffn2_w_up", "ffn2_w_down", "final_norm_g"]
_BIG = ["ffn1_w_gate", "ffn1_w_up", "ffn1_w_down", "w_in", "w_out", "xa_w_q", "xa_w_kv", "xa_w_o",
        "ffn2_w_gate", "ffn2_w_up", "ffn2_w_down"]
_COL_SHARDED = ("ffn1_w_gate", "ffn1_w_up", "w_in", "xa_w_kv", "ffn2_w_gate", "ffn2_w_up")
_EARLY = ["ffn1_w_gate", "ffn1_w_up", "ffn1_w_down"]
_LATE = [n for n in _BIG if n not in _EARLY and n != "w_in"]
_SMALL = [n for n in _WEIGHTS if n not in _BIG]
SMALL_LANES = 128
SMALL_ROW_ALIGN = 8


def _pack_small(tensors):
    parts = []
    for n in _SMALL:
        t = tensors[n].reshape(-1, SMALL_LANES)
        pad = (-t.shape[0]) % SMALL_ROW_ALIGN
        parts.append(jnp.pad(t, ((0, pad), (0, 0))) if pad else t)
    return jnp.concatenate(parts, axis=0)


def _unpack_small(packed, like):
    out, off = {}, 0
    for n in _SMALL:
        size = like[n].size
        rows = size // SMALL_LANES
        out[n] = packed[off:off + rows].reshape(like[n].shape)
        off += rows + (-rows) % SMALL_ROW_ALIGN
    return out


def kernel(x, mem, ffn1_pre_g, ffn1_post_g, ffn1_w_gate, ffn1_w_up, ffn1_w_down, mix_pre_g, mix_post_g, w_in, sgu_norm_g, sgu_norm_b, sgu_w_s, sgu_b_s, sgu_out_g, sb_out_g, w_out, xa_pre_g, xa_post_g, mem_norm_g, xa_w_q, xa_w_kv, xa_w_o, ffn2_pre_g, ffn2_post_g, ffn2_w_gate, ffn2_w_up, ffn2_w_down, final_norm_g, loss_target, m_ffn1_pre_g, m_ffn1_post_g, m_ffn1_w_gate, m_ffn1_w_up, m_ffn1_w_down, m_mix_pre_g, m_mix_post_g, m_w_in, m_sgu_norm_g, m_sgu_norm_b, m_sgu_w_s, m_sgu_b_s, m_sgu_out_g, m_sb_out_g, m_w_out, m_xa_pre_g, m_xa_post_g, m_mem_norm_g, m_xa_w_q, m_xa_w_kv, m_xa_w_o, m_ffn2_pre_g, m_ffn2_post_g, m_ffn2_w_gate, m_ffn2_w_up, m_ffn2_w_down, m_final_norm_g, v_ffn1_pre_g, v_ffn1_post_g, v_ffn1_w_gate, v_ffn1_w_up, v_ffn1_w_down, v_mix_pre_g, v_mix_post_g, v_w_in, v_sgu_norm_g, v_sgu_norm_b, v_sgu_w_s, v_sgu_b_s, v_sgu_out_g, v_sb_out_g, v_w_out, v_xa_pre_g, v_xa_post_g, v_mem_norm_g, v_xa_w_q, v_xa_w_kv, v_xa_w_o, v_ffn2_pre_g, v_ffn2_post_g, v_ffn2_w_gate, v_ffn2_w_up, v_ffn2_w_down, v_final_norm_g):
    vals = dict(locals())
    d_model = x.shape[-1]

    def packed(names):
        return jnp.concatenate(
            [(vals[n][0].T if n in _COL_SHARDED else vals[n][0]).astype(BF16) for n in names], axis=0)

    shard_rows = {n: vals[n].shape[2 if n in _COL_SHARDED else 1] for n in _BIG}
    big = _unpack_rows(_all_gather(packed(_EARLY), "ag_weights"), _EARLY, shard_rows)

    small = {n: vals[n] for n in _SMALL}
    loss_part, dx, gs, gw, w_in_rows, late_rows = _local_step(
        x[0], mem[0], loss_target[0], _small_views(small), big, packed(["w_in"]), packed(_LATE), _LATE,
        shard_rows)
    loss = lax.psum(loss_part[0, 0], ("x", "y", "c"))

    parts = [gw[n].reshape(N_DEV, -1, d_model).astype(BF16) for n in _EARLY]
    rows = sum(p.shape[1] for p in parts)
    p4 = jnp.concatenate(parts, axis=1).reshape(N_DEV // 2, 2, rows, d_model)
    mx, my, mc = _mesh_place()
    place = jnp.stack([mc, 2 * mx + my]).astype(jnp.int32)
    recv_a = _pair_exchange(p4)
    recv_b = _chip_exchange(_pair_sum(place, p4, recv_a))
    early_rows = _rs_final(place, p4, recv_a, recv_b)

    grads, deltas, new_m, new_v = {}, {}, {}, {}
    for names, g_rows in ((_EARLY, early_rows), (["w_in"], w_in_rows), (_LATE, late_rows)):
        off = 0
        for n in names:
            rows = shard_rows[n]
            g = g_rows[off:off + rows]
            off += rows
            g = g.T if n in _COL_SHARDED else g
            grads[n] = g[None]
            d, m1, v1 = _adamw(vals[n][0], g, vals["m_" + n][0], vals["v_" + n][0], "adamw_" + n)
            deltas[n], new_m[n], new_v[n] = d[None], m1[None], v1[None]

    gathered_small = _all_gather(_pack_small(gs), "ag_small_grads")
    outs = _small_sum_adamw(gathered_small, _pack_small(small),
                            _pack_small({n: vals["m_" + n] for n in _SMALL}),
                            _pack_small({n: vals["v_" + n] for n in _SMALL}))
    for dst, packed in zip((grads, deltas, new_m, new_v), outs):
        dst.update(_unpack_small(packed, small))

    return (loss, dx[None], *[grads[n] for n in _WEIGHTS], *[deltas[n] for n in _WEIGHTS],
            *[new_m[n] for n in _WEIGHTS], *[new_v[n] for n in _WEIGHTS])
```

```python
import functools

import jax
import jax.numpy as jnp
from jax import lax
from jax.experimental import pallas as pl
from jax.experimental.pallas import tpu as pltpu

F32 = jnp.float32
BF16 = jnp.bfloat16
EPS = 1e-6
MESH = pl.DeviceIdType.MESH
N_DEV = 8

SGU_GROUPS = 4
GROUP_DIM = 128
CHUNK = 128
SB_HEADS = 8
SB_HEAD_DIM = 64
Q_BLOCK = 128
XA_HEADS = 4
XA_HEAD_DIM = 256

ADAM_LR = 0.001
ADAM_B1 = 0.9
ADAM_B2 = 0.999
ADAM_EPS = 1e-08
ADAM_WD = 0.01
ADAM_STEP = 10

VMEM_LIMIT_V7X = 56 * 1024 * 1024
GELU_C0 = 0.7978845608028654
GELU_C1 = 0.044715


def _dot(a, b):
    return jnp.dot(a.astype(BF16), b.astype(BF16), preferred_element_type=F32)


def _dot_nt(a, b):
    return lax.dot_general(a.astype(BF16), b.astype(BF16), (((1,), (1,)), ((), ())),
                           preferred_element_type=F32)


def _dot_tn(a, b):
    return lax.dot_general(a.astype(BF16), b.astype(BF16), (((0,), (0,)), ((), ())),
                           preferred_element_type=F32)


def _rms(x, g):
    r = lax.rsqrt(jnp.mean(x * x, axis=-1, keepdims=True) + EPS)
    return x * r * g, r


def _rms_bwd(x, g, dy):
    r = lax.rsqrt(jnp.mean(x * x, axis=-1, keepdims=True) + EPS)
    xh = x * r
    gy = dy * g
    dx = r * (gy - xh * jnp.mean(gy * xh, axis=-1, keepdims=True))
    dg = jnp.sum(dy * xh, axis=0, keepdims=True)
    return dx, dg


def _sigmoid(x):
    return 1.0 / (1.0 + jnp.exp(-x))


def _gelu(x):
    t = jnp.tanh(GELU_C0 * (x + GELU_C1 * x * x * x))
    return 0.5 * x * (1.0 + t)


def _gelu_grad(x):
    t = jnp.tanh(GELU_C0 * (x + GELU_C1 * x * x * x))
    return 0.5 * (1.0 + t) + 0.5 * x * (1.0 - t * t) * GELU_C0 * (1.0 + 3.0 * GELU_C1 * x * x)


def _split_bf16(x):
    hi = x.astype(BF16)
    lo = (x - hi.astype(F32)).astype(BF16)
    return hi, lo


def _row_spec(tm, cols):
    return pl.BlockSpec((tm, cols), lambda i: (i, 0))


def _full_spec(shape, buffers=None):
    nd = len(shape)
    mode = None if buffers is None else pl.Buffered(buffers)
    return pl.BlockSpec(tuple(shape), lambda i: (0,) * nd, pipeline_mode=mode)


FFN_TOKEN_TILE = 512


def _token_tile(s):
    return min(256, s)


def _row_call(body, name, s, tiled_in, full_in, tiled_out, acc_out, gather=None, scatter=None, tm=None):
    tm = _token_tile(s) if tm is None else min(tm, s)
    steps = s // tm
    in_specs = [_row_spec(tm, a.shape[1]) for a in tiled_in] + [_full_spec(a.shape, buffers=1) for a in full_in]
    out_specs = [_row_spec(tm, c) for c, _ in tiled_out] + [_full_spec(sh) for sh, _ in acc_out]
    out_shape = [jax.ShapeDtypeStruct((s, c), dt) for c, dt in tiled_out]
    out_shape += [jax.ShapeDtypeStruct(sh, dt) for sh, dt in acc_out]
    operands = [*tiled_in, *full_in]
    scratch = []
    kernel_body = functools.partial(body)
    sent = gather if gather is not None else scatter
    if sent is not None:
        n_in, n_out = len(operands), len(out_shape)
        out_shape.append(jax.ShapeDtypeStruct(
            (N_DEV,) + sent.shape if gather is not None else sent.shape, sent.dtype))
        operands.append(sent)
        in_specs.append(_ANY)
        out_specs.append(_ANY)
        scratch = list(_AG_SCRATCH)

        def kernel_body(*refs):
            ins, sent_ref = refs[:n_in], refs[n_in]
            outs, landed_ref = refs[n_in + 1:n_in + 1 + n_out], refs[n_in + 1 + n_out]
            step = pl.program_id(0)
            if gather is not None:
                start, forward, finish = _gather_phases(sent_ref, landed_ref, *refs[-3:])
            else:
                start, finish = _scatter_phases(sent_ref, landed_ref, *refs[-3:])
            pl.when(step == 0)(start)
            body(*ins, *outs)
            if gather is not None:
                pl.when(step == (2 * steps) // 3)(forward)
            pl.when(step == steps - 1)(finish)

    return pl.pallas_call(
        kernel_body,
        name=name,
        grid=(steps,),
        in_specs=in_specs,
        out_specs=out_specs,
        out_shape=out_shape,
        scratch_shapes=scratch,
        compiler_params=pltpu.CompilerParams(
            dimension_semantics=("arbitrary",), vmem_limit_bytes=VMEM_LIMIT_V7X),
    )(*operands)


def _acc(ref, val):
    @pl.when(pl.program_id(0) == 0)
    def _():
        ref[...] = val

    @pl.when(pl.program_id(0) != 0)
    def _():
        ref[...] += val


def _ffn_fwd_tile(x_ref, pre_ref, post_ref, wgt_ref, wut_ref, wd_ref, n_ref, a_ref, b_ref, f_ref):
    x = x_ref[...]
    n, _ = _rms(x, pre_ref[...])
    nb = n.astype(BF16)
    n_ref[...] = nb
    a = _dot_nt(nb, wgt_ref[...])
    b = _dot_nt(nb, wut_ref[...])
    a_ref[...] = a.astype(BF16)
    b_ref[...] = b.astype(BF16)
    hmid = a * _sigmoid(a) * b
    f = jnp.dot(hmid.astype(BF16), wd_ref[...], preferred_element_type=F32)
    f_ref[...] = f
    y, _ = _rms(f, post_ref[...])
    return x + 0.5 * y


def _ffn_fwd_body(x_ref, pre_ref, post_ref, wgt_ref, wut_ref, wd_ref,
                  h_ref, n_ref, a_ref, b_ref, f_ref):
    h_ref[...] = _ffn_fwd_tile(x_ref, pre_ref, post_ref, wgt_ref, wut_ref, wd_ref, n_ref, a_ref, b_ref, f_ref)


def _ffn_loss_body(x_ref, t_ref, pre_ref, post_ref, gfin_ref, wgt_ref, wut_ref, wd_ref,
                   n_ref, a_ref, b_ref, f_ref, dh_ref, loss_ref, dg_ref):
    h = _ffn_fwd_tile(x_ref, pre_ref, post_ref, wgt_ref, wut_ref, wd_ref, n_ref, a_ref, b_ref, f_ref)
    d = h.shape[1]
    y, _ = _rms(h, gfin_ref[...])
    err = y - t_ref[...]
    part = (0.5 / d) * jnp.sum(jnp.sum(err * err, axis=1, keepdims=True), axis=0, keepdims=True)
    dh, dg = _rms_bwd(h, gfin_ref[...], err * (1.0 / d))
    dh_ref[...] = dh
    _acc(loss_ref, part)
    _acc(dg_ref, dg)


def _ffn_fwd(x, pre_g, post_g, wgt, wut, wd, name, gather=None):
    s, d = x.shape
    f = wgt.shape[0]
    return _row_call(_ffn_fwd_body, name, s, [x], [pre_g, post_g, wgt, wut, wd],
                     [(d, F32), (d, BF16), (f, BF16), (f, BF16), (d, F32)], [], gather=gather, tm=FFN_TOKEN_TILE)


def _ffn_loss_fwd(x, target, pre_g, post_g, final_g, wgt, wut, wd, name):
    s, d = x.shape
    f = wgt.shape[0]
    return _row_call(_ffn_loss_body, name, s, [x, target], [pre_g, post_g, final_g, wgt, wut, wd],
                     [(d, BF16), (f, BF16), (f, BF16), (d, F32), (d, F32)],
                     [((1, 1), F32), ((1, d), F32)], tm=FFN_TOKEN_TILE)


def _ffn_bwd_body(dh_ref, x_ref, a_ref, b_ref, f_ref, pre_ref, post_ref, wgt_ref, wut_ref, wd_ref,
                  dx_ref, da_ref, db_ref, hm_ref, df_ref, dpre_ref, dpost_ref):
    dh = dh_ref[...]
    df, dpost = _rms_bwd(f_ref[...], post_ref[...], 0.5 * dh)
    dfb = df.astype(BF16)
    df_ref[...] = dfb
    dhmid = _dot_nt(dfb, wd_ref[...])
    a = a_ref[...].astype(F32)
    b = b_ref[...].astype(F32)
    sig = _sigmoid(a)
    sa = a * sig
    hm_ref[...] = (sa * b).astype(BF16)
    dab = (dhmid * b * sig * (1.0 + a * (1.0 - sig))).astype(BF16)
    dbb = (dhmid * sa).astype(BF16)
    da_ref[...] = dab
    db_ref[...] = dbb
    dn = _dot(dab, wgt_ref[...]) + _dot(dbb, wut_ref[...])
    dxn, dpre = _rms_bwd(x_ref[...], pre_ref[...], dn)
    dx_ref[...] = dh + dxn
    _acc(dpre_ref, dpre)
    _acc(dpost_ref, dpost)


def _ffn_bwd(dh, x, a, b, f, pre_g, post_g, wgt, wut, wd, name, scatter=None):
    s, d = x.shape
    ff = wgt.shape[0]
    return _row_call(_ffn_bwd_body, name, s, [dh, x, a, b, f], [pre_g, post_g, wgt, wut, wd],
                     [(d, F32), (ff, BF16), (ff, BF16), (ff, BF16), (d, BF16)],
                     [((1, d), F32), ((1, d), F32)], scatter=scatter)


def _inproj_fwd_body(h_ref, g_ref, wt_ref, n_ref, uv_ref, qkv_ref):
    n, _ = _rms(h_ref[...], g_ref[...])
    nb = n.astype(BF16)
    n_ref[...] = nb
    proj = _dot_nt(nb, wt_ref[...])
    nuv = uv_ref.shape[1]
    uv_ref[...] = proj[:, :nuv]
    qkv_ref[...] = proj[:, nuv:].astype(BF16)


def _inproj_fwd(h, g, w_in_t):
    s, d = h.shape
    sgu_w = SGU_GROUPS * GROUP_DIM
    sb_w = SB_HEADS * SB_HEAD_DIM
    return _row_call(_inproj_fwd_body, "inproj_fwd", s, [h], [g, w_in_t],
                     [(d, BF16), (2 * sgu_w, F32), (3 * sb_w, BF16)], [])


def _inproj_bwd_body(dh_ref, dproj_ref, h_ref, g_ref, wt_ref, dhout_ref, dg_ref):
    dn = _dot(dproj_ref[...], wt_ref[...])
    dhn, dg = _rms_bwd(h_ref[...], g_ref[...], dn)
    dhout_ref[...] = dh_ref[...] + dhn
    _acc(dg_ref, dg)


def _inproj_bwd(dh, dproj, h, g, w_in_t):
    s, d = h.shape
    return _row_call(_inproj_bwd_body, "inproj_bwd", s, [dh, dproj, h], [g, w_in_t],
                     [(d, F32)], [((1, d), F32)])


def _causal_w(ws_ref, g):
    row = lax.broadcasted_iota(jnp.int32, (CHUNK, CHUNK), 0)
    col = lax.broadcasted_iota(jnp.int32, (CHUNK, CHUNK), 1)
    return jnp.where(row >= col, ws_ref[g], 0.0), row >= col


def _group_norm(v):
    mu = jnp.mean(v, axis=-1, keepdims=True)
    d = v - mu
    rstd = lax.rsqrt(jnp.mean(d * d, axis=-1, keepdims=True) + EPS)
    return d * rstd, rstd


def _sgu_fwd_body(uv_ref, ng_ref, nb_ref, ws_ref, bs_ref, out_ref):
    width = SGU_GROUPS * GROUP_DIM
    for c in range(uv_ref.shape[0] // CHUNK):
        rows = pl.ds(c * CHUNK, CHUNK)
        for g in range(SGU_GROUPS):
            lanes = pl.ds(g * GROUP_DIM, GROUP_DIM)
            u = _gelu(uv_ref[rows, lanes])
            v = _gelu(uv_ref[rows, pl.ds(width + g * GROUP_DIM, GROUP_DIM)])
            vhat, _ = _group_norm(v)
            vn = vhat * ng_ref[:, lanes] + nb_ref[:, lanes]
            w, _ = _causal_w(ws_ref, g)
            mixed = _dot(w, vn) + bs_ref[g]
            out_ref[rows, lanes] = u * mixed


def _sgu_fwd(uv_pre, ng, nb, ws, bs):
    s = uv_pre.shape[0]
    return _row_call(_sgu_fwd_body, "sgu_fwd", s, [uv_pre], [ng, nb, ws, bs],
                     [(SGU_GROUPS * GROUP_DIM, F32)], [])[0]


def _sgu_bwd_body(uv_ref, do_ref, ng_ref, nb_ref, ws_ref, bs_ref,
                  duv_ref, dws_ref, dbs_ref, dng_ref, dnb_ref):
    width = SGU_GROUPS * GROUP_DIM

    @pl.when(pl.program_id(0) == 0)
    def _():
        dws_ref[...] = jnp.zeros_like(dws_ref)
        dbs_ref[...] = jnp.zeros_like(dbs_ref)
        dng_ref[...] = jnp.zeros_like(dng_ref)
        dnb_ref[...] = jnp.zeros_like(dnb_ref)

    for c in range(uv_ref.shape[0] // CHUNK):
        rows = pl.ds(c * CHUNK, CHUNK)
        for g in range(SGU_GROUPS):
            lanes = pl.ds(g * GROUP_DIM, GROUP_DIM)
            vlanes = pl.ds(width + g * GROUP_DIM, GROUP_DIM)
            u_pre = uv_ref[rows, lanes]
            v_pre = uv_ref[rows, vlanes]
            u = _gelu(u_pre)
            v = _gelu(v_pre)
            vhat, rstd = _group_norm(v)
            gain = ng_ref[:, lanes]
            vn = vhat * gain + nb_ref[:, lanes]
            w, causal = _causal_w(ws_ref, g)
            mixed = _dot(w, vn) + bs_ref[g]
            dout = do_ref[rows, lanes]
            du = dout * mixed
            dmixed = dout * u
            dbs_ref[g] += jnp.sum(dmixed, axis=1, keepdims=True)
            dws_ref[g] += jnp.where(causal, _dot_nt(dmixed, vn), 0.0)
            dvn = _dot_tn(w, dmixed)
            dng_ref[:, lanes] += jnp.sum(dvn * vhat, axis=0, keepdims=True)
            dnb_ref[:, lanes] += jnp.sum(dvn, axis=0, keepdims=True)
            dvh = dvn * gain
            dv = rstd * (dvh - jnp.mean(dvh, axis=-1, keepdims=True)
                         - vhat * jnp.mean(dvh * vhat, axis=-1, keepdims=True))
            duv_ref[rows, lanes] = (du * _gelu_grad(u_pre)).astype(BF16)
            duv_ref[rows, vlanes] = (dv * _gelu_grad(v_pre)).astype(BF16)


def _sgu_bwd(uv_pre, dout_a, ng, nb, ws, bs):
    s = uv_pre.shape[0]
    width = SGU_GROUPS * GROUP_DIM
    return _row_call(_sgu_bwd_body, "sgu_bwd", s, [uv_pre, dout_a], [ng, nb, ws, bs],
                     [(2 * width, BF16)],
                     [(ws.shape, F32), (bs.shape, F32), ((1, width), F32), ((1, width), F32)])


def _mesh_place():
    return lax.axis_index("x"), lax.axis_index("y"), lax.axis_index("c")


def _other_chips(mx, my):
    return [(1 - mx, my), (mx, 1 - my), (1 - mx, 1 - my)]


_ANY = pl.BlockSpec(memory_space=pl.ANY)
AG_SEMS = 7
_AG_SCRATCH = [pltpu.SemaphoreType.DMA((AG_SEMS,)), pltpu.SemaphoreType.DMA((AG_SEMS,)),
               pltpu.SemaphoreType.DMA(())]


def _gather_phases(x_ref, out_ref, send_sems, recv_sems, local_sem):
    mx, my, mc = _mesh_place()
    me, sibling = (mx, my, mc), (mx, my, 1 - mc)
    chips = _other_chips(mx, my)

    def slot(px, py, pc):
        return out_ref.at[4 * px + 2 * py + pc]

    def copy(k, block, to, src=None):
        return pltpu.make_async_remote_copy(
            src_ref=slot(*block) if src is None else src, dst_ref=slot(*block),
            send_sem=send_sems.at[k], recv_sem=recv_sems.at[k],
            device_id=to, device_id_type=MESH)

    mine = pltpu.make_async_copy(x_ref, slot(*me), local_sem)
    first = [copy(0, me, sibling, src=x_ref)]
    first += [copy(1 + j, me, (*chip, mc), src=x_ref) for j, chip in enumerate(chips)]
    passed = [copy(4 + j, (*chip, mc), sibling) for j, chip in enumerate(chips)]

    def start():
        mine.start()
        for cp in first:
            cp.start()

    def forward():
        for j, chip in enumerate(chips):
            copy(1 + j, (*chip, mc), me).wait_recv()
            passed[j].start()

    def finish():
        copy(0, sibling, me).wait_recv()
        for j, chip in enumerate(chips):
            copy(4 + j, (*chip, 1 - mc), me).wait_recv()
        for cp in first + passed:
            cp.wait_send()
        mine.wait()

    return start, forward, finish


def _all_gather(x, name):
    r, c = x.shape

    def body(x_ref, out_ref, send_sems, recv_sems, local_sem):
        start, forward, finish = _gather_phases(x_ref, out_ref, send_sems, recv_sems, local_sem)
        start()
        forward()
        finish()

    return pl.pallas_call(
        body,
        name=name,
        out_shape=jax.ShapeDtypeStruct((N_DEV, r, c), x.dtype),
        in_specs=[_ANY],
        out_specs=_ANY,
        scratch_shapes=list(_AG_SCRATCH),
    )(x)


def _scatter_phases(p_ref, out_ref, send_sems, recv_sems, local_sem):
    mx, my, mc = _mesh_place()
    me = 4 * mx + 2 * my + mc
    copies = []
    for k in range(1, N_DEV):
        tx, ty, tc = mx ^ ((k >> 2) & 1), my ^ ((k >> 1) & 1), mc ^ (k & 1)
        copies.append(pltpu.make_async_remote_copy(
            src_ref=p_ref.at[4 * tx + 2 * ty + tc], dst_ref=out_ref.at[me],
            send_sem=send_sems.at[k - 1], recv_sem=recv_sems.at[k - 1],
            device_id=(tx, ty, tc), device_id_type=MESH))
    mine = pltpu.make_async_copy(p_ref.at[me], out_ref.at[me], local_sem)

    def start():
        mine.start()
        for cp in copies:
            cp.start()

    def finish():
        for cp in copies:
            cp.wait()
        mine.wait()

    return start, finish


SB_DEAD = -105.0
HEADS_PER_TILE = 2
TILES_PER_STEP = 2
HEADS_PER_STEP = HEADS_PER_TILE * TILES_PER_STEP
STEP_LANES = TILES_PER_STEP * HEADS_PER_TILE * SB_HEAD_DIM
TILE_LANES = HEADS_PER_TILE * SB_HEAD_DIM
STACK_ROWS = HEADS_PER_STEP * Q_BLOCK
TILE_ROWS = HEADS_PER_TILE * Q_BLOCK
SB_FORWARD_LEAD = 12


def _stack_heads(x):
    lane = lax.broadcasted_iota(jnp.int32, x.shape, 1)
    zero = jnp.zeros_like(x)
    return jnp.concatenate(
        [jnp.where(lane // SB_HEAD_DIM == h, x, zero) for h in range(HEADS_PER_STEP)], axis=0)


def _unstack_tile(x):
    first = lax.broadcasted_iota(jnp.int32, (Q_BLOCK, TILE_LANES), 1) < SB_HEAD_DIM
    return jnp.where(first, x[:Q_BLOCK], x[Q_BLOCK:])


def _sb_logs(qs, k, diagonal):
    z = _dot_nt(qs, k) * (SB_HEAD_DIM ** -0.5)
    sp = jnp.log1p(jnp.exp(-jnp.abs(z)))
    log_beta = jnp.minimum(z, 0.0) - sp
    log_1m_raw = -jnp.maximum(z, 0.0) - sp
    if not diagonal:
        return None, log_beta, log_1m_raw, log_1m_raw
    row = lax.broadcasted_iota(jnp.int32, z.shape, 0)
    col = lax.broadcasted_iota(jnp.int32, z.shape, 1)
    strict = col < jnp.bitwise_and(row, Q_BLOCK - 1)
    return strict, log_beta, log_1m_raw, jnp.where(strict, log_1m_raw, 0.0)


def _masked(strict, x):
    return x if strict is None else jnp.where(strict, x, 0.0)


def _key_sums(x, pick):
    hi, lo = _split_bf16(x)
    both = jnp.dot(jnp.concatenate([hi, lo], axis=0), pick, preferred_element_type=F32)
    return both[:x.shape[0]] + both[x.shape[0]:]


def _key_order():
    row = lax.broadcasted_iota(jnp.int32, (Q_BLOCK, Q_BLOCK), 0)
    col = lax.broadcasted_iota(jnp.int32, (Q_BLOCK, Q_BLOCK), 1)
    return row, col


def _sb_fwd_body(q_ref, k_ref, v_ref, shard_ref, o_ref, tot_ref, cnt_ref, gathered_ref,
                 acc_ref, send_sems, recv_sems, local_sem):
    grp, qb = pl.program_id(0), pl.program_id(1)
    last_grp, last_qb = pl.num_programs(0) - 1, pl.num_programs(1) - 1
    ag_start, ag_forward, ag_finish = _gather_phases(shard_ref, gathered_ref, send_sems, recv_sems, local_sem)
    pl.when(jnp.logical_and(grp == 0, qb == 0))(ag_start)

    qs = _stack_heads(q_ref[...])
    row, col = _key_order()
    later = (row > col).astype(BF16)

    def block(i, c, diagonal):
        rows = pl.ds(pl.multiple_of((qb - i) * Q_BLOCK, Q_BLOCK), Q_BLOCK)
        strict, log_beta, _, log_1m = _sb_logs(qs, k_ref[rows, :], diagonal)
        a = _masked(strict, jnp.exp(log_beta + _key_sums(log_1m, later) + c)).astype(BF16)
        for t in range(TILES_PER_STEP):
            part = jnp.dot(a[t * TILE_ROWS:(t + 1) * TILE_ROWS], v_ref[rows, t * TILE_LANES:(t + 1) * TILE_LANES],
                           preferred_element_type=F32)
            if diagonal:
                acc_ref[t] = part
            else:
                acc_ref[t] += part
        return c + jnp.sum(log_1m, axis=1, keepdims=True)

    c = block(0, jnp.zeros((STACK_ROWS, 1), F32), True)

    def alive(carry):
        i, c = carry
        return jnp.logical_and(i <= qb, jnp.max(c) > SB_DEAD)

    def step(carry):
        i, c = carry
        return i + 1, block(i, c, False)

    n, c = lax.while_loop(alive, step, (jnp.int32(1), c))
    for t in range(TILES_PER_STEP):
        o_ref[:, t * TILE_LANES:(t + 1) * TILE_LANES] = _unstack_tile(acc_ref[t])
    for h in range(HEADS_PER_STEP):
        tot_ref[h] = c[h * Q_BLOCK:(h + 1) * Q_BLOCK]
    cnt_ref[grp, qb] = n.astype(F32)
    pl.when(jnp.logical_and(grp == last_grp, qb == jnp.maximum(last_qb - SB_FORWARD_LEAD, 0)))(ag_forward)
    pl.when(jnp.logical_and(grp == last_grp, qb == last_qb))(ag_finish)


def _sb_fwd(qkv, shard):
    s = qkv.shape[0]
    groups = SB_HEADS // HEADS_PER_STEP
    nq = s // Q_BLOCK
    return pl.pallas_call(
        functools.partial(_sb_fwd_body),
        name="sb_fwd",
        grid=(groups, nq),
        in_specs=[pl.BlockSpec((Q_BLOCK, STEP_LANES), lambda g, i: (i, g)),
                  pl.BlockSpec((s, STEP_LANES), lambda g, i: (0, groups + g)),
                  pl.BlockSpec((s, STEP_LANES), lambda g, i: (0, 2 * groups + g)),
                  _ANY],
        out_specs=[pl.BlockSpec((Q_BLOCK, STEP_LANES), lambda g, i: (i, g)),
                   pl.BlockSpec((HEADS_PER_STEP, Q_BLOCK, 1), lambda g, i: (g, i, 0)),
                   pl.BlockSpec(memory_space=pltpu.SMEM),
                   _ANY],
        out_shape=[jax.ShapeDtypeStruct((s, SB_HEADS * SB_HEAD_DIM), F32),
                   jax.ShapeDtypeStruct((SB_HEADS, s, 1), F32),
                   jax.ShapeDtypeStruct((groups, nq), F32),
                   jax.ShapeDtypeStruct((N_DEV,) + shard.shape, shard.dtype)],
        scratch_shapes=[pltpu.VMEM((TILES_PER_STEP, TILE_ROWS, TILE_LANES), F32)] + list(_AG_SCRATCH),
        compiler_params=pltpu.CompilerParams(
            dimension_semantics=("arbitrary", "arbitrary"), vmem_limit_bytes=VMEM_LIMIT_V7X),
    )(qkv, qkv, qkv, shard)


def _sb_bwd_body(cnt_ref, q_ref, k_ref, v_ref, tot_ref, do_ref, part_ref, dq_ref, dk_ref, dv_ref, recv_ref,
                 acc_ref, send_sems, recv_sems, local_sem):
    grp, qb = pl.program_id(0), pl.program_id(1)
    last_grp, last_qb = pl.num_programs(0) - 1, pl.num_programs(1) - 1
    rs_start, rs_finish = _scatter_phases(part_ref, recv_ref, send_sems, recv_sems, local_sem)
    pl.when(jnp.logical_and(grp == 0, qb == 0))(rs_start)

    @pl.when(qb == 0)
    def _():
        dk_ref[...] = jnp.zeros_like(dk_ref)
        dv_ref[...] = jnp.zeros_like(dv_ref)

    acc_ref[...] = jnp.zeros_like(acc_ref)
    qs = _stack_heads(q_ref[...])
    dos = _stack_heads(do_ref[...].astype(BF16))
    tot = jnp.concatenate([tot_ref[h] for h in range(HEADS_PER_STEP)], axis=0)
    row, col = _key_order()
    up_to = (row <= col).astype(BF16)
    earlier = (row < col).astype(BF16)
    scale = SB_HEAD_DIM ** -0.5
    n = jnp.clip(cnt_ref[grp, qb].astype(jnp.int32), 1, qb + 1)

    def block(kb, c, ce, diagonal):
        rows = pl.ds(pl.multiple_of(kb * Q_BLOCK, Q_BLOCK), Q_BLOCK)
        k = k_ref[rows, :]
        strict, log_beta, log_1m_raw, log_1m = _sb_logs(qs, k, diagonal)
        suffix = tot - c - _key_sums(log_1m, up_to)
        a = _masked(strict, jnp.exp(log_beta + suffix))
        de = _dot_nt(dos, v_ref[rows, :]) * a
        before = ce + _key_sums(de, earlier)
        dz = _masked(strict, de * jnp.exp(log_1m_raw) - before * jnp.exp(log_beta)).astype(BF16)
        for t in range(TILES_PER_STEP):
            acc_ref[t] += jnp.dot(dz[t * TILE_ROWS:(t + 1) * TILE_ROWS], k[:, t * TILE_LANES:(t + 1) * TILE_LANES],
                                  preferred_element_type=F32)
        dk_ref[rows, :] += _dot_tn(dz, qs) * scale
        dv_ref[rows, :] += _dot_tn(a, dos)
        return c + jnp.sum(log_1m, axis=1, keepdims=True), ce + jnp.sum(de, axis=1, keepdims=True)

    def step(i, carry):
        return block(qb - n + 1 + i, *carry, False)

    zc = jnp.zeros((STACK_ROWS, 1), F32)
    c, ce = lax.fori_loop(0, n - 1, step, (zc, zc))
    block(qb, c, ce, True)
    for t in range(TILES_PER_STEP):
        dq_ref[:, t * TILE_LANES:(t + 1) * TILE_LANES] = _unstack_tile(acc_ref[t]) * scale
    pl.when(jnp.logical_and(grp == last_grp, qb == last_qb))(rs_finish)


def _sb_bwd(cnt, qkv, tot, dout_b, parts):
    s = qkv.shape[0]
    groups = SB_HEADS // HEADS_PER_STEP
    return pl.pallas_call(
        functools.partial(_sb_bwd_body),
        name="sb_bwd",
        grid=(groups, s // Q_BLOCK),
        in_specs=[pl.BlockSpec(memory_space=pltpu.SMEM),
                  pl.BlockSpec((Q_BLOCK, STEP_LANES), lambda g, i: (i, g)),
                  pl.BlockSpec((s, STEP_LANES), lambda g, i: (0, groups + g)),
                  pl.BlockSpec((s, STEP_LANES), lambda g, i: (0, 2 * groups + g)),
                  pl.BlockSpec((HEADS_PER_STEP, Q_BLOCK, 1), lambda g, i: (g, i, 0)),
                  pl.BlockSpec((Q_BLOCK, STEP_LANES), lambda g, i: (i, g)),
                  _ANY],
        out_specs=[pl.BlockSpec((Q_BLOCK, STEP_LANES), lambda g, i: (i, g)),
                   pl.BlockSpec((s, STEP_LANES), lambda g, i: (0, g)),
                   pl.BlockSpec((s, STEP_LANES), lambda g, i: (0, g)),
                   _ANY],
        out_shape=[jax.ShapeDtypeStruct((s, SB_HEADS * SB_HEAD_DIM), F32)] * 3
        + [jax.ShapeDtypeStruct(parts.shape, parts.dtype)],
        scratch_shapes=[pltpu.VMEM((TILES_PER_STEP, TILE_ROWS, TILE_LANES), F32)] + list(_AG_SCRATCH),
        compiler_params=pltpu.CompilerParams(
            dimension_semantics=("arbitrary", "arbitrary"), vmem_limit_bytes=VMEM_LIMIT_V7X),
    )(cnt, qkv, qkv, qkv, tot, dout_b, parts)


def _outproj_fwd_body(oa_ref, ob_ref, h_ref, ga_ref, gb_ref, gpost_ref, w_ref,
                      merged_ref, mo_ref, hout_ref):
    half = oa_ref.shape[1]
    ma, _ = _rms(oa_ref[...], ga_ref[...])
    mb, _ = _rms(ob_ref[...], gb_ref[...])
    mab = ma.astype(BF16)
    mbb = mb.astype(BF16)
    merged_ref[:, :half] = mab
    merged_ref[:, half:] = mbb
    mo = (jnp.dot(mab, w_ref[:half, :], preferred_element_type=F32)
          + jnp.dot(mbb, w_ref[half:, :], preferred_element_type=F32))
    mo_ref[...] = mo
    y, _ = _rms(mo, gpost_ref[...])
    hout_ref[...] = h_ref[...] + y


def _outproj_fwd(out_a, out_b, h, ga, gb, gpost, w_out):
    s, d = h.shape
    return _row_call(_outproj_fwd_body, "outproj_fwd", s, [out_a, out_b, h], [ga, gb, gpost, w_out],
                     [(d, BF16), (d, F32), (d, F32)], [])


def _outproj_bwd_body(dh_ref, mo_ref, oa_ref, ob_ref, ga_ref, gb_ref, gpost_ref, w_ref,
                      dmo_ref, doa_ref, dob_ref, dga_ref, dgb_ref, dgpost_ref):
    half = oa_ref.shape[1]
    dmo, dgpost = _rms_bwd(mo_ref[...], gpost_ref[...], dh_ref[...])
    dmob = dmo.astype(BF16)
    dmo_ref[...] = dmob
    dma = _dot_nt(dmob, w_ref[:half, :])
    dmb = _dot_nt(dmob, w_ref[half:, :])
    doa, dga = _rms_bwd(oa_ref[...], ga_ref[...], dma)
    dob, dgb = _rms_bwd(ob_ref[...], gb_ref[...], dmb)
    doa_ref[...] = doa
    dob_ref[...] = dob
    _acc(dga_ref, dga)
    _acc(dgb_ref, dgb)
    _acc(dgpost_ref, dgpost)


def _outproj_bwd(dh, mo, out_a, out_b, ga, gb, gpost, w_out):
    s, d = dh.shape
    half = out_a.shape[1]
    return _row_call(_outproj_bwd_body, "outproj_bwd", s, [dh, mo, out_a, out_b], [ga, gb, gpost, w_out],
                     [(d, BF16), (half, F32), (half, F32)],
                     [((1, half), F32), ((1, half), F32), ((1, d), F32)])


def _kv_fwd_body(mem_ref, g_ref, wt_ref, memn_ref, kv_ref):
    n, _ = _rms(mem_ref[...], g_ref[...])
    nb = n.astype(BF16)
    memn_ref[...] = nb
    kv_ref[...] = _dot_nt(nb, wt_ref[...]).astype(BF16)


def _kv_fwd(mem, g, w_kv_t):
    m, d = mem.shape
    return _row_call(_kv_fwd_body, "kv_fwd", m, [mem], [g, w_kv_t], [(d, BF16), (w_kv_t.shape[0], BF16)], [])


def _kv_bwd_body(dkv_ref, mem_ref, memn_ref, g_ref, wt_ref, dwt_ref, dg_ref):
    dkvb = dkv_ref[...].astype(BF16)
    dwt_ref[...] = _dot_tn(dkvb, memn_ref[...]).astype(BF16)
    dmemn = _dot(dkvb, wt_ref[...])
    _, dg = _rms_bwd(mem_ref[...], g_ref[...], dmemn)
    dg_ref[...] = dg


def _kv_bwd(dkv, mem, memn, g, w_kv_t):
    m, d = mem.shape
    return pl.pallas_call(
        functools.partial(_kv_bwd_body),
        name="kv_bwd",
        out_shape=[jax.ShapeDtypeStruct(w_kv_t.shape, BF16), jax.ShapeDtypeStruct((1, d), F32)],
        compiler_params=pltpu.CompilerParams(vmem_limit_bytes=VMEM_LIMIT_V7X),
    )(dkv, mem, memn, g, w_kv_t)


def _xa_fwd_body(h_ref, gpre_ref, gpost_ref, wq_ref, wo_ref, kv_ref,
                 n_ref, q_ref, o_ref, c_ref, hout_ref):
    h = h_ref[...]
    d = h.shape[1]
    n, _ = _rms(h, gpre_ref[...])
    nb = n.astype(BF16)
    n_ref[...] = nb
    qb = jnp.dot(nb, wq_ref[...], preferred_element_type=F32).astype(BF16)
    q_ref[...] = qb
    for hd in range(XA_HEADS):
        lanes = slice(hd * XA_HEAD_DIM, (hd + 1) * XA_HEAD_DIM)
        k = kv_ref[:, lanes]
        v = kv_ref[:, d + hd * XA_HEAD_DIM:d + (hd + 1) * XA_HEAD_DIM]
        logits = _dot_nt(qb[:, lanes], k) * (XA_HEAD_DIM ** -0.5)
        e = jnp.exp(logits - jnp.max(logits, axis=-1, keepdims=True))
        p = e / jnp.sum(e, axis=-1, keepdims=True)
        o_ref[:, lanes] = jnp.dot(p.astype(BF16), v, preferred_element_type=F32).astype(BF16)
    c = jnp.dot(o_ref[...], wo_ref[...], preferred_element_type=F32)
    c_ref[...] = c
    y, _ = _rms(c, gpost_ref[...])
    hout_ref[...] = h + y


def _xa_fwd(h, gpre, gpost, wq, wo, kv):
    s, d = h.shape
    return _row_call(_xa_fwd_body, "xa_fwd", s, [h], [gpre, gpost, wq, wo, kv],
                     [(d, BF16), (d, BF16), (d, BF16), (d, F32), (d, F32)], [])


def _xa_bwd_body(dh_ref, h_ref, c_ref, q_ref, o_ref, gpre_ref, gpost_ref, wq_ref, wo_ref, kv_ref,
                 dhout_ref, dc_ref, dq_ref, dkv_ref, dgpre_ref, dgpost_ref):
    dh = dh_ref[...]
    d = dh.shape[1]
    scale = XA_HEAD_DIM ** -0.5
    dc, dgpost = _rms_bwd(c_ref[...], gpost_ref[...], dh)
    dcb = dc.astype(BF16)
    dc_ref[...] = dcb
    dob = _dot_nt(dcb, wo_ref[...]).astype(BF16)

    @pl.when(pl.program_id(0) == 0)
    def _():
        dkv_ref[...] = jnp.zeros_like(dkv_ref)

    for hd in range(XA_HEADS):
        lanes = slice(hd * XA_HEAD_DIM, (hd + 1) * XA_HEAD_DIM)
        vlanes = slice(d + hd * XA_HEAD_DIM, d + (hd + 1) * XA_HEAD_DIM)
        qh = q_ref[:, lanes]
        k = kv_ref[:, lanes]
        v = kv_ref[:, vlanes]
        logits = _dot_nt(qh, k) * scale
        e = jnp.exp(logits - jnp.max(logits, axis=-1, keepdims=True))
        p = e / jnp.sum(e, axis=-1, keepdims=True)
        doh = dob[:, lanes]
        dp = _dot_nt(doh, v)
        dl = (p * (dp - jnp.sum(dp * p, axis=-1, keepdims=True)) * scale).astype(BF16)
        dq_ref[:, lanes] = jnp.dot(dl, k, preferred_element_type=F32).astype(BF16)
        dkv_ref[:, lanes] += _dot_tn(dl, qh)
        dkv_ref[:, vlanes] += _dot_tn(p, doh)
    dn = _dot_nt(dq_ref[...], wq_ref[...])
    dhn, dgpre = _rms_bwd(h_ref[...], gpre_ref[...], dn)
    dhout_ref[...] = dh + dhn
    _acc(dgpre_ref, dgpre)
    _acc(dgpost_ref, dgpost)


def _xa_bwd(dh, h, c, q, o, gpre, gpost, wq, wo, kv, scatter=None):
    s, d = h.shape
    return _row_call(_xa_bwd_body, "xa_bwd", s, [dh, h, c, q, o], [gpre, gpost, wq, wo, kv],
                     [(d, F32), (d, BF16), (d, BF16)],
                     [(kv.shape, F32), ((1, d), F32), ((1, d), F32)], scatter=scatter)


def _largest_tile(n, cap):
    best = 128
    for t in range(128, cap + 1, 128):
        if n % t == 0:
            best = t
    return best


def _mm_tn(a, bs, name, gather=None):
    s, k = a.shape
    n = bs[0].shape[1]
    nb = len(bs)
    ts = min(2048, s)
    tk = _largest_tile(k, 1536)
    tn = _largest_tile(n, 1536 // nb)

    steps = s // ts
    grid = (k // tk, n // tn, steps)
    hosted = gather is not None

    def body(a_ref, *refs):
        if hosted:
            sent_ref, landed_ref, refs = refs[nb], refs[2 * nb + 1], refs[:nb] + refs[nb + 1:2 * nb + 1] + refs[2 * nb + 2:]
            start, forward, finish = _gather_phases(sent_ref, landed_ref, *refs[-3:])
            place = (pl.program_id(0) * grid[1] + pl.program_id(1)) * grid[2] + pl.program_id(2)
            pl.when(place == 0)(start)
        b_refs, o_refs, acc_refs = refs[:nb], refs[nb:2 * nb], refs[2 * nb:3 * nb]
        at = a_ref[...]
        t = pl.program_id(2)

        @pl.when(t == 0)
        def _():
            for acc_ref in acc_refs:
                acc_ref[...] = jnp.zeros_like(acc_ref)

        for b_ref, acc_ref in zip(b_refs, acc_refs):
            acc_ref[...] += _dot_tn(at, b_ref[...])

        @pl.when(t == steps - 1)
        def _():
            for o_ref, acc_ref in zip(o_refs, acc_refs):
                o_ref[...] = acc_ref[...].astype(BF16)

        if hosted:
            total = grid[0] * grid[1] * grid[2]

            @pl.when(place == total - 1)
            def _():
                forward()
                finish()

    return pl.pallas_call(
        body,
        name=name,
        grid=grid,
        in_specs=[pl.BlockSpec((ts, tk), lambda i, j, t: (t, i))]
        + [pl.BlockSpec((ts, tn), lambda i, j, t: (t, j))] * nb + ([_ANY] if hosted else []),
        out_specs=[pl.BlockSpec((tk, tn), lambda i, j, t: (i, j))] * nb + ([_ANY] if hosted else []),
        out_shape=[jax.ShapeDtypeStruct((k, n), BF16)] * nb
        + ([jax.ShapeDtypeStruct((N_DEV,) + gather.shape, gather.dtype)] if hosted else []),
        scratch_shapes=[pltpu.VMEM((tk, tn), F32)] * nb + (list(_AG_SCRATCH) if hosted else []),
        compiler_params=pltpu.CompilerParams(
            dimension_semantics=("arbitrary", "arbitrary", "arbitrary"),
            vmem_limit_bytes=VMEM_LIMIT_V7X),
    )(a, *bs, *([gather] if hosted else []))


_SMALL_SHAPES = {
    "sgu_norm_g": (1, SGU_GROUPS * GROUP_DIM),
    "sgu_norm_b": (1, SGU_GROUPS * GROUP_DIM),
    "sgu_w_s": (SGU_GROUPS, CHUNK, CHUNK),
    "sgu_b_s": (SGU_GROUPS, CHUNK, 1),
}


def _small_views(small):
    return {n: v.reshape(_SMALL_SHAPES.get(n, v.shape)) for n, v in small.items()}


def _small_unviews(views, like):
    return {n: v.reshape(like[n].shape) for n, v in views.items()}


def _unpack_rows(gathered, names, shard_rows):
    out, off = {}, 0
    for n in names:
        rows = shard_rows[n]
        out[n] = gathered[:, off:off + rows, :].reshape(N_DEV * rows, gathered.shape[2])
        off += rows
    return out


def _row_tile(r, cap):
    best = 16
    for t in range(16, cap + 1, 16):
        if r % t == 0:
            best = t
    return best


def _sum_received(received, name):
    _, r, c = received.shape
    tr = _row_tile(r, 1024)

    def body(rc_ref, g_ref):
        g = rc_ref[0].astype(F32)
        for t in range(1, N_DEV):
            g = g + rc_ref[t].astype(F32)
        g_ref[...] = g

    return pl.pallas_call(
        body, name=name, grid=(r // tr,),
        in_specs=[pl.BlockSpec((N_DEV, tr, c), lambda i: (0, i, 0))],
        out_specs=pl.BlockSpec((tr, c), lambda i: (i, 0)),
        out_shape=jax.ShapeDtypeStruct((r, c), F32),
    )(received)


def _local_step(x, mem, target, small, big, shards, shard_rows):
    sm, w = small, dict(big)
    d_model = x.shape[1]

    def parts(names):
        return jnp.concatenate([gw.pop(n).reshape(N_DEV, -1, d_model) for n in names], axis=1)

    h1, n1, a1, b1, f1, landed = _ffn_fwd(
        x, sm["ffn1_pre_g"], sm["ffn1_post_g"], w["ffn1_w_gate"], w["ffn1_w_up"], w["ffn1_w_down"],
        "ffn1_fwd", gather=shards["ffn1_fwd"])
    w.update(_unpack_rows(landed, GATHER_IN["ffn1_fwd"], shard_rows))
    n2, uv_pre, qkv = _inproj_fwd(h1, sm["mix_pre_g"], w["w_in"])
    out_a = _sgu_fwd(uv_pre, sm["sgu_norm_g"], sm["sgu_norm_b"], sm["sgu_w_s"], sm["sgu_b_s"])
    out_b, tot, cnt, landed = _sb_fwd(qkv, shards["sb_fwd"])
    w.update(_unpack_rows(landed, GATHER_IN["sb_fwd"], shard_rows))
    merged, mo, h2 = _outproj_fwd(out_a, out_b, h1, sm["sgu_out_g"], sm["sb_out_g"],
                                  sm["mix_post_g"], w["w_out"])
    memn, kv = _kv_fwd(mem, sm["mem_norm_g"], w["xa_w_kv"])
    n3, qx, ox, cx, h3 = _xa_fwd(h2, sm["xa_pre_g"], sm["xa_post_g"], w["xa_w_q"], w["xa_w_o"], kv)
    n4, a2, b2, f2, dh4, loss, dg_final = _ffn_loss_fwd(
        h3, target, sm["ffn2_pre_g"], sm["ffn2_post_g"], sm["final_norm_g"],
        w["ffn2_w_gate"], w["ffn2_w_up"], w["ffn2_w_down"], "ffn2_fwd")

    gs, gw = {"final_norm_g": dg_final}, {}
    dh3, da2, db2, hm2, df2, gs["ffn2_pre_g"], gs["ffn2_post_g"] = _ffn_bwd(
        dh4, h3, a2, b2, f2, sm["ffn2_pre_g"], sm["ffn2_post_g"],
        w["ffn2_w_gate"], w["ffn2_w_up"], w["ffn2_w_down"], "ffn2_bwd")
    gw["ffn2_w_gate"], = _mm_tn(da2, [n4], "ffn2_dw_gate")
    gw["ffn2_w_up"], = _mm_tn(db2, [n4], "ffn2_dw_up")
    gw["ffn2_w_down"], = _mm_tn(hm2, [df2], "ffn2_dw_down")

    received = {}
    dh2, dc, dqx, dkv, gs["xa_pre_g"], gs["xa_post_g"], received["xa_bwd"] = _xa_bwd(
        dh3, h2, cx, qx, ox, sm["xa_pre_g"], sm["xa_post_g"], w["xa_w_q"], w["xa_w_o"], kv,
        scatter=parts(SCATTER_IN["xa_bwd"]))
    gw["xa_w_o"], = _mm_tn(ox, [dc], "xa_dw_o")
    gw["xa_w_q"], = _mm_tn(n3, [dqx], "xa_dw_q")
    gw["xa_w_kv"], gs["mem_norm_g"] = _kv_bwd(dkv, mem, memn, sm["mem_norm_g"], w["xa_w_kv"])

    dmo, dout_a, dout_b, gs["sgu_out_g"], gs["sb_out_g"], gs["mix_post_g"] = _outproj_bwd(
        dh2, mo, out_a, out_b, sm["sgu_out_g"], sm["sb_out_g"], sm["mix_post_g"], w["w_out"])
    gw["w_out"], = _mm_tn(merged, [dmo], "mix_dw_out")
    dq, dk, dv, received["sb_bwd"] = _sb_bwd(cnt, qkv, tot, dout_b, parts(SCATTER_IN["sb_bwd"]))
    duv, gs["sgu_w_s"], gs["sgu_b_s"], gs["sgu_norm_g"], gs["sgu_norm_b"] = _sgu_bwd(
        uv_pre, dout_a, sm["sgu_norm_g"], sm["sgu_norm_b"], sm["sgu_w_s"], sm["sgu_b_s"])
    dproj = jnp.concatenate([duv] + [t.astype(BF16) for t in (dq, dk, dv)], axis=1)
    dh1, gs["mix_pre_g"] = _inproj_bwd(dh2, dproj, h1, sm["mix_pre_g"], w["w_in"])
    gw["w_in"], = _mm_tn(dproj, [n2], "mix_dw_in")

    dx, da1, db1, hm1, df1, gs["ffn1_pre_g"], gs["ffn1_post_g"], received["ffn1_bwd"] = _ffn_bwd(
        dh1, x, a1, b1, f1, sm["ffn1_pre_g"], sm["ffn1_post_g"],
        w["ffn1_w_gate"], w["ffn1_w_up"], w["ffn1_w_down"], "ffn1_bwd",
        scatter=parts(SCATTER_IN["ffn1_bwd"]))
    gw["ffn1_w_gate"], = _mm_tn(da1, [n1], "ffn1_dw_gate")
    gw["ffn1_w_up"], small_grads = _mm_tn(db1, [n1], "ffn1_dw_up", gather=_pack_small(gs))
    gw["ffn1_w_down"], = _mm_tn(hm1, [df1], "ffn1_dw_down")
    summed = {host: _sum_received(r, "rs_sum_" + host) for host, r in received.items()}
    return loss, dx, small_grads, gw, summed


def _pair_exchange(p4):
    nchip, _, r, c = p4.shape

    def body(p_ref, out_ref, send_sems, recv_sems):
        mx, my, mc = _mesh_place()
        copies = [pltpu.make_async_remote_copy(
            src_ref=p_ref.at[j, 1 - mc], dst_ref=out_ref.at[j],
            send_sem=send_sems.at[j], recv_sem=recv_sems.at[j],
            device_id=(mx, my, 1 - mc), device_id_type=MESH) for j in range(nchip)]
        for cp in copies:
            cp.start()
        for cp in copies:
            cp.wait()

    return pl.pallas_call(
        body,
        name="rs_pair_exchange",
        out_shape=jax.ShapeDtypeStruct((nchip, r, c), p4.dtype),
        in_specs=[_ANY],
        out_specs=_ANY,
        scratch_shapes=[pltpu.SemaphoreType.DMA((nchip,)), pltpu.SemaphoreType.DMA((nchip,))],
    )(p4)


def _chip_exchange(q):
    _, r, c = q.shape

    def body(q_ref, out_ref, send_sems, recv_sems):
        mx, my, mc = _mesh_place()
        copies = [pltpu.make_async_remote_copy(
            src_ref=q_ref.at[2 * cx + cy], dst_ref=out_ref.at[k],
            send_sem=send_sems.at[k], recv_sem=recv_sems.at[k],
            device_id=(cx, cy, mc), device_id_type=MESH)
            for k, (cx, cy) in enumerate(_other_chips(mx, my))]
        for cp in copies:
            cp.start()
        for cp in copies:
            cp.wait()

    return pl.pallas_call(
        body,
        name="rs_chip_exchange",
        out_shape=jax.ShapeDtypeStruct((3, r, c), q.dtype),
        in_specs=[_ANY],
        out_specs=_ANY,
        scratch_shapes=[pltpu.SemaphoreType.DMA((3,)), pltpu.SemaphoreType.DMA((3,))],
    )(q)


def _rs_row_tile(r):
    return _row_tile(r, 1024)


def _pair_sum(place, p4, recv_a):
    nchip, _, r, c = p4.shape
    tr = _rs_row_tile(r)

    def body(place_ref, p_ref, a_ref, q_ref):
        q_ref[0] = (p_ref[0, 0].astype(F32) + a_ref[0].astype(F32)).astype(BF16)

    return pl.pallas_call(
        body,
        name="rs_pair_sum",
        grid_spec=pltpu.PrefetchScalarGridSpec(
            num_scalar_prefetch=1,
            grid=(nchip, r // tr),
            in_specs=[pl.BlockSpec((1, 1, tr, c), lambda j, i, pref: (j, pref[0], i, 0)),
                      pl.BlockSpec((1, tr, c), lambda j, i, pref: (j, i, 0))],
            out_specs=pl.BlockSpec((1, tr, c), lambda j, i, pref: (j, i, 0)),
        ),
        out_shape=jax.ShapeDtypeStruct((nchip, r, c), BF16),
    )(place, p4, recv_a)


def _rs_final(place, p4, recv_a, recv_b):
    _, _, r, c = p4.shape
    tr = _rs_row_tile(r)

    def body(place_ref, p_ref, a_ref, b_ref, g_ref):
        g = p_ref[0, 0].astype(F32) + a_ref[0].astype(F32)
        for k in range(3):
            g = g + b_ref[k].astype(F32)
        g_ref[...] = g

    return pl.pallas_call(
        body,
        name="rs_final_sum",
        grid_spec=pltpu.PrefetchScalarGridSpec(
            num_scalar_prefetch=1,
            grid=(r // tr,),
            in_specs=[pl.BlockSpec((1, 1, tr, c), lambda i, pref: (pref[1], pref[0], i, 0)),
                      pl.BlockSpec((1, tr, c), lambda i, pref: (pref[1], i, 0)),
                      pl.BlockSpec((3, tr, c), lambda i, pref: (0, i, 0))],
            out_specs=pl.BlockSpec((tr, c), lambda i, pref: (i, 0)),
        ),
        out_shape=jax.ShapeDtypeStruct((r, c), F32),
    )(place, p4, recv_a, recv_b)


def _adamw_math(w, g, m, v):
    m = ADAM_B1 * m + (1.0 - ADAM_B1) * g
    v = ADAM_B2 * v + (1.0 - ADAM_B2) * (g * g)
    m_hat = m / (1.0 - ADAM_B1 ** ADAM_STEP)
    v_hat = v / (1.0 - ADAM_B2 ** ADAM_STEP)
    delta = -ADAM_LR * (m_hat / (jnp.sqrt(v_hat) + ADAM_EPS) + ADAM_WD * w)
    return delta, m, v


def _adamw(w, g, m, v, name):
    r, c = w.shape
    tr = r if r <= 512 else 256

    def body(w_ref, g_ref, m_ref, v_ref, d_ref, mo_ref, vo_ref):
        d_ref[...], mo_ref[...], vo_ref[...] = _adamw_math(w_ref[...], g_ref[...], m_ref[...], v_ref[...])

    spec = pl.BlockSpec((tr, c), lambda i: (i, 0))
    out = jax.ShapeDtypeStruct((r, c), F32)
    return pl.pallas_call(
        body, name=name, grid=(r // tr,), in_specs=[spec] * 4, out_specs=[spec] * 3,
        out_shape=[out] * 3,
    )(w, g, m, v)


def _small_sum_adamw(gathered, w, m, v):
    _, r, c = gathered.shape

    def body(ga_ref, w_ref, m_ref, v_ref, g_ref, d_ref, mo_ref, vo_ref):
        g = ga_ref[0]
        for k in range(1, N_DEV):
            g = g + ga_ref[k]
        g_ref[...] = g
        d_ref[...], mo_ref[...], vo_ref[...] = _adamw_math(w_ref[...], g, m_ref[...], v_ref[...])

    out = jax.ShapeDtypeStruct((r, c), F32)
    return pl.pallas_call(body, name="small_sum_adamw", out_shape=[out] * 4)(gathered, w, m, v)


_WEIGHTS = ["ffn1_pre_g", "ffn1_post_g", "ffn1_w_gate", "ffn1_w_up", "ffn1_w_down", "mix_pre_g",
            "mix_post_g", "w_in", "sgu_norm_g", "sgu_norm_b", "sgu_w_s", "sgu_b_s", "sgu_out_g",
            "sb_out_g", "w_out", "xa_pre_g", "xa_post_g", "mem_norm_g", "xa_w_q", "xa_w_kv", "xa_w_o",
            "ffn2_pre_g", "ffn2_post_g", "ffn2_w_gate", "ffn2_w_up", "ffn2_w_down", "final_norm_g"]
_BIG = ["ffn1_w_gate", "ffn1_w_up", "ffn1_w_down", "w_in", "w_out", "xa_w_q", "xa_w_kv", "xa_w_o",
        "ffn2_w_gate", "ffn2_w_up", "ffn2_w_down"]
_COL_SHARDED = ("ffn1_w_gate", "ffn1_w_up", "w_in", "xa_w_kv", "ffn2_w_gate", "ffn2_w_up")
_EARLY = ["ffn1_w_gate", "ffn1_w_up", "ffn1_w_down"]
GATHER_IN = {"ffn1_fwd": ["w_in", "ffn2_w_gate"],
             "sb_fwd": ["w_out", "xa_w_q", "xa_w_kv", "xa_w_o", "ffn2_w_up", "ffn2_w_down"]}
SCATTER_IN = {"xa_bwd": ["ffn2_w_gate"],
              "sb_bwd": ["ffn2_w_up", "ffn2_w_down", "xa_w_q", "xa_w_kv", "xa_w_o", "w_out"],
              "ffn1_bwd": ["w_in"]}
_SMALL = [n for n in _WEIGHTS if n not in _BIG]
SMALL_LANES = 128
SMALL_ROW_ALIGN = 8


def _pack_small(tensors):
    parts = []
    for n in _SMALL:
        t = tensors[n].reshape(-1, SMALL_LANES)
        pad = (-t.shape[0]) % SMALL_ROW_ALIGN
        parts.append(jnp.pad(t, ((0, pad), (0, 0))) if pad else t)
    return jnp.concatenate(parts, axis=0)


def _unpack_small(packed, like):
    out, off = {}, 0
    for n in _SMALL:
        size = like[n].size
        rows = size // SMALL_LANES
        out[n] = packed[off:off + rows].reshape(like[n].shape)
        off += rows + (-rows) % SMALL_ROW_ALIGN
    return out


def kernel(x, mem, ffn1_pre_g, ffn1_post_g, ffn1_w_gate, ffn1_w_up, ffn1_w_down, mix_pre_g, mix_post_g, w_in, sgu_norm_g, sgu_norm_b, sgu_w_s, sgu_b_s, sgu_out_g, sb_out_g, w_out, xa_pre_g, xa_post_g, mem_norm_g, xa_w_q, xa_w_kv, xa_w_o, ffn2_pre_g, ffn2_post_g, ffn2_w_gate, ffn2_w_up, ffn2_w_down, final_norm_g, loss_target, m_ffn1_pre_g, m_ffn1_post_g, m_ffn1_w_gate, m_ffn1_w_up, m_ffn1_w_down, m_mix_pre_g, m_mix_post_g, m_w_in, m_sgu_norm_g, m_sgu_norm_b, m_sgu_w_s, m_sgu_b_s, m_sgu_out_g, m_sb_out_g, m_w_out, m_xa_pre_g, m_xa_post_g, m_mem_norm_g, m_xa_w_q, m_xa_w_kv, m_xa_w_o, m_ffn2_pre_g, m_ffn2_post_g, m_ffn2_w_gate, m_ffn2_w_up, m_ffn2_w_down, m_final_norm_g, v_ffn1_pre_g, v_ffn1_post_g, v_ffn1_w_gate, v_ffn1_w_up, v_ffn1_w_down, v_mix_pre_g, v_mix_post_g, v_w_in, v_sgu_norm_g, v_sgu_norm_b, v_sgu_w_s, v_sgu_b_s, v_sgu_out_g, v_sb_out_g, v_w_out, v_xa_pre_g, v_xa_post_g, v_mem_norm_g, v_xa_w_q, v_xa_w_kv, v_xa_w_o, v_ffn2_pre_g, v_ffn2_post_g, v_ffn2_w_gate, v_ffn2_w_up, v_ffn2_w_down, v_final_norm_g):
    vals = dict(locals())
    d_model = x.shape[-1]

    def packed(names):
        return jnp.concatenate(
            [(vals[n][0].T if n in _COL_SHARDED else vals[n][0]).astype(BF16) for n in names], axis=0)

    shard_rows = {n: vals[n].shape[2 if n in _COL_SHARDED else 1] for n in _BIG}
    big = _unpack_rows(_all_gather(packed(_EARLY), "ag_weights"), _EARLY, shard_rows)

    small = {n: vals[n] for n in _SMALL}
    loss_part, dx, gathered_small, gw, summed = _local_step(
        x[0], mem[0], loss_target[0], _small_views(small), big,
        {host: packed(names) for host, names in GATHER_IN.items()}, shard_rows)
    loss = lax.psum(loss_part[0, 0], ("x", "y", "c"))

    parts = [gw[n].reshape(N_DEV, -1, d_model).astype(BF16) for n in _EARLY]
    rows = sum(p.shape[1] for p in parts)
    p4 = jnp.concatenate(parts, axis=1).reshape(N_DEV // 2, 2, rows, d_model)
    mx, my, mc = _mesh_place()
    place = jnp.stack([mc, 2 * mx + my]).astype(jnp.int32)
    recv_a = _pair_exchange(p4)
    recv_b = _chip_exchange(_pair_sum(place, p4, recv_a))
    early_rows = _rs_final(place, p4, recv_a, recv_b)

    grads, deltas, new_m, new_v = {}, {}, {}, {}
    for names, g_rows in [(_EARLY, early_rows)] + [(SCATTER_IN[host], summed[host]) for host in SCATTER_IN]:
        off = 0
        for n in names:
            rows = shard_rows[n]
            g = g_rows[off:off + rows]
            off += rows
            g = g.T if n in _COL_SHARDED else g
            grads[n] = g[None]
            d, m1, v1 = _adamw(vals[n][0], g, vals["m_" + n][0], vals["v_" + n][0], "adamw_" + n)
            deltas[n], new_m[n], new_v[n] = d[None], m1[None], v1[None]

    outs = _small_sum_adamw(gathered_small, _pack_small(small),
                            _pack_small({n: vals["m_" + n] for n in _SMALL}),
                            _pack_small({n: vals["v_" + n] for n in _SMALL}))
    for dst, packed in zip((grads, deltas, new_m, new_v), outs):
        dst.update(_unpack_small(packed, small))

    return (loss, dx[None], *[grads[n] for n in _WEIGHTS], *[deltas[n] for n in _WEIGHTS],
            *[new_m[n] for n in _WEIGHTS], *[new_v[n] for n in _WEIGHTS])
```

```python
import functools

import jax
import jax.numpy as jnp
from jax import lax
from jax.experimental import pallas as pl
from jax.experimental.pallas import tpu as pltpu

F32 = jnp.float32
BF16 = jnp.bfloat16
EPS = 1e-6
MESH = pl.DeviceIdType.MESH
N_DEV = 8

SGU_GROUPS = 4
GROUP_DIM = 128
CHUNK = 128
SB_HEADS = 8
SB_HEAD_DIM = 64
Q_BLOCK = 128
XA_HEADS = 4
XA_HEAD_DIM = 256

ADAM_LR = 0.001
ADAM_B1 = 0.9
ADAM_B2 = 0.999
ADAM_EPS = 1e-08
ADAM_WD = 0.01
ADAM_STEP = 10

VMEM_LIMIT_V7X = 56 * 1024 * 1024
GELU_C0 = 0.7978845608028654
GELU_C1 = 0.044715


def _dot(a, b):
    return jnp.dot(a.astype(BF16), b.astype(BF16), preferred_element_type=F32)


def _dot_nt(a, b):
    return lax.dot_general(a.astype(BF16), b.astype(BF16), (((1,), (1,)), ((), ())),
                           preferred_element_type=F32)


def _dot_tn(a, b):
    return lax.dot_general(a.astype(BF16), b.astype(BF16), (((0,), (0,)), ((), ())),
                           preferred_element_type=F32)


def _rms(x, g):
    r = lax.rsqrt(jnp.mean(x * x, axis=-1, keepdims=True) + EPS)
    return x * r * g, r


def _rms_bwd(x, g, dy):
    r = lax.rsqrt(jnp.mean(x * x, axis=-1, keepdims=True) + EPS)
    xh = x * r
    gy = dy * g
    dx = r * (gy - xh * jnp.mean(gy * xh, axis=-1, keepdims=True))
    dg = jnp.sum(dy * xh, axis=0, keepdims=True)
    return dx, dg


def _sigmoid(x):
    return 1.0 / (1.0 + jnp.exp(-x))


def _gelu(x):
    t = jnp.tanh(GELU_C0 * (x + GELU_C1 * x * x * x))
    return 0.5 * x * (1.0 + t)


def _gelu_grad(x):
    t = jnp.tanh(GELU_C0 * (x + GELU_C1 * x * x * x))
    return 0.5 * (1.0 + t) + 0.5 * x * (1.0 - t * t) * GELU_C0 * (1.0 + 3.0 * GELU_C1 * x * x)


def _split_bf16(x):
    hi = x.astype(BF16)
    lo = (x - hi.astype(F32)).astype(BF16)
    return hi, lo


def _row_spec(tm, cols):
    return pl.BlockSpec((tm, cols), lambda i: (i, 0))


def _full_spec(shape, buffers=None):
    nd = len(shape)
    mode = None if buffers is None else pl.Buffered(buffers)
    return pl.BlockSpec(tuple(shape), lambda i: (0,) * nd, pipeline_mode=mode)


FFN_TOKEN_TILE = 512


def _token_tile(s):
    return min(256, s)


def _row_call(body, name, s, tiled_in, full_in, tiled_out, acc_out, gather=None, scatter=None, tm=None):
    tm = _token_tile(s) if tm is None else min(tm, s)
    steps = s // tm
    in_specs = [_row_spec(tm, a.shape[1]) for a in tiled_in] + [_full_spec(a.shape, buffers=1) for a in full_in]
    out_specs = [_row_spec(tm, c) for c, _ in tiled_out] + [_full_spec(sh) for sh, _ in acc_out]
    out_shape = [jax.ShapeDtypeStruct((s, c), dt) for c, dt in tiled_out]
    out_shape += [jax.ShapeDtypeStruct(sh, dt) for sh, dt in acc_out]
    operands = [*tiled_in, *full_in]
    scratch = []
    kernel_body = functools.partial(body)
    sent = gather if gather is not None else scatter
    if sent is not None:
        n_in, n_out = len(operands), len(out_shape)
        out_shape.append(jax.ShapeDtypeStruct(
            (N_DEV,) + sent.shape if gather is not None else sent.shape, sent.dtype))
        operands.append(sent)
        in_specs.append(_ANY)
        out_specs.append(_ANY)
        scratch = list(_AG_SCRATCH)

        def kernel_body(*refs):
            ins, sent_ref = refs[:n_in], refs[n_in]
            outs, landed_ref = refs[n_in + 1:n_in + 1 + n_out], refs[n_in + 1 + n_out]
            step = pl.program_id(0)
            if gather is not None:
                start, forward, finish = _gather_phases(sent_ref, landed_ref, *refs[-3:])
            else:
                start, finish = _scatter_phases(sent_ref, landed_ref, *refs[-3:])
            pl.when(step == 0)(start)
            body(*ins, *outs)
            if gather is not None:
                pl.when(step == (2 * steps) // 3)(forward)
            pl.when(step == steps - 1)(finish)

    return pl.pallas_call(
        kernel_body,
        name=name,
        grid=(steps,),
        in_specs=in_specs,
        out_specs=out_specs,
        out_shape=out_shape,
        scratch_shapes=scratch,
        compiler_params=pltpu.CompilerParams(
            dimension_semantics=("arbitrary",), vmem_limit_bytes=VMEM_LIMIT_V7X),
    )(*operands)


def _acc(ref, val):
    @pl.when(pl.program_id(0) == 0)
    def _():
        ref[...] = val

    @pl.when(pl.program_id(0) != 0)
    def _():
        ref[...] += val


def _ffn_fwd_tile(x_ref, pre_ref, post_ref, wgt_ref, wut_ref, wd_ref, n_ref, a_ref, b_ref, f_ref):
    x = x_ref[...]
    n, _ = _rms(x, pre_ref[...])
    nb = n.astype(BF16)
    n_ref[...] = nb
    a = _dot_nt(nb, wgt_ref[...])
    b = _dot_nt(nb, wut_ref[...])
    a_ref[...] = a.astype(BF16)
    b_ref[...] = b.astype(BF16)
    hmid = a * _sigmoid(a) * b
    f = jnp.dot(hmid.astype(BF16), wd_ref[...], preferred_element_type=F32)
    f_ref[...] = f
    y, _ = _rms(f, post_ref[...])
    return x + 0.5 * y


def _ffn_fwd_body(x_ref, pre_ref, post_ref, wgt_ref, wut_ref, wd_ref,
                  h_ref, n_ref, a_ref, b_ref, f_ref):
    h_ref[...] = _ffn_fwd_tile(x_ref, pre_ref, post_ref, wgt_ref, wut_ref, wd_ref, n_ref, a_ref, b_ref, f_ref)


def _ffn_loss_body(x_ref, t_ref, pre_ref, post_ref, gfin_ref, wgt_ref, wut_ref, wd_ref,
                   n_ref, a_ref, b_ref, f_ref, dh_ref, loss_ref, dg_ref):
    h = _ffn_fwd_tile(x_ref, pre_ref, post_ref, wgt_ref, wut_ref, wd_ref, n_ref, a_ref, b_ref, f_ref)
    d = h.shape[1]
    y, _ = _rms(h, gfin_ref[...])
    err = y - t_ref[...]
    part = (0.5 / d) * jnp.sum(jnp.sum(err * err, axis=1, keepdims=True), axis=0, keepdims=True)
    dh, dg = _rms_bwd(h, gfin_ref[...], err * (1.0 / d))
    dh_ref[...] = dh
    _acc(loss_ref, part)
    _acc(dg_ref, dg)


def _ffn_fwd(x, pre_g, post_g, wgt, wut, wd, name, gather=None):
    s, d = x.shape
    f = wgt.shape[0]
    return _row_call(_ffn_fwd_body, name, s, [x], [pre_g, post_g, wgt, wut, wd],
                     [(d, F32), (d, BF16), (f, BF16), (f, BF16), (d, F32)], [], gather=gather, tm=FFN_TOKEN_TILE)


def _ffn_loss_fwd(x, target, pre_g, post_g, final_g, wgt, wut, wd, name):
    s, d = x.shape
    f = wgt.shape[0]
    return _row_call(_ffn_loss_body, name, s, [x, target], [pre_g, post_g, final_g, wgt, wut, wd],
                     [(d, BF16), (f, BF16), (f, BF16), (d, F32), (d, F32)],
                     [((1, 1), F32), ((1, d), F32)], tm=FFN_TOKEN_TILE)


def _ffn_bwd_body(dh_ref, x_ref, a_ref, b_ref, f_ref, pre_ref, post_ref, wgt_ref, wut_ref, wd_ref,
                  dx_ref, da_ref, db_ref, hm_ref, df_ref, dpre_ref, dpost_ref):
    dh = dh_ref[...]
    df, dpost = _rms_bwd(f_ref[...], post_ref[...], 0.5 * dh)
    dfb = df.astype(BF16)
    df_ref[...] = dfb
    dhmid = _dot_nt(dfb, wd_ref[...])
    a = a_ref[...].astype(F32)
    b = b_ref[...].astype(F32)
    sig = _sigmoid(a)
    sa = a * sig
    hm_ref[...] = (sa * b).astype(BF16)
    dab = (dhmid * b * sig * (1.0 + a * (1.0 - sig))).astype(BF16)
    dbb = (dhmid * sa).astype(BF16)
    da_ref[...] = dab
    db_ref[...] = dbb
    dn = _dot(dab, wgt_ref[...]) + _dot(dbb, wut_ref[...])
    dxn, dpre = _rms_bwd(x_ref[...], pre_ref[...], dn)
    dx_ref[...] = dh + dxn
    _acc(dpre_ref, dpre)
    _acc(dpost_ref, dpost)


def _ffn_bwd(dh, x, a, b, f, pre_g, post_g, wgt, wut, wd, name, scatter=None):
    s, d = x.shape
    ff = wgt.shape[0]
    return _row_call(_ffn_bwd_body, name, s, [dh, x, a, b, f], [pre_g, post_g, wgt, wut, wd],
                     [(d, F32), (ff, BF16), (ff, BF16), (ff, BF16), (d, BF16)],
                     [((1, d), F32), ((1, d), F32)], scatter=scatter)


def _inproj_fwd_body(h_ref, g_ref, wt_ref, n_ref, uv_ref, qkv_ref):
    n, _ = _rms(h_ref[...], g_ref[...])
    nb = n.astype(BF16)
    n_ref[...] = nb
    proj = _dot_nt(nb, wt_ref[...])
    nuv = uv_ref.shape[1]
    uv_ref[...] = proj[:, :nuv]
    qkv_ref[...] = proj[:, nuv:].astype(BF16)


def _inproj_fwd(h, g, w_in_t):
    s, d = h.shape
    sgu_w = SGU_GROUPS * GROUP_DIM
    sb_w = SB_HEADS * SB_HEAD_DIM
    return _row_call(_inproj_fwd_body, "inproj_fwd", s, [h], [g, w_in_t],
                     [(d, BF16), (2 * sgu_w, F32), (3 * sb_w, BF16)], [])


def _inproj_bwd_body(dh_ref, dproj_ref, h_ref, g_ref, wt_ref, dhout_ref, dg_ref):
    dn = _dot(dproj_ref[...], wt_ref[...])
    dhn, dg = _rms_bwd(h_ref[...], g_ref[...], dn)
    dhout_ref[...] = dh_ref[...] + dhn
    _acc(dg_ref, dg)


def _inproj_bwd(dh, dproj, h, g, w_in_t):
    s, d = h.shape
    return _row_call(_inproj_bwd_body, "inproj_bwd", s, [dh, dproj, h], [g, w_in_t],
                     [(d, F32)], [((1, d), F32)])


def _causal_w(ws_ref, g):
    row = lax.broadcasted_iota(jnp.int32, (CHUNK, CHUNK), 0)
    col = lax.broadcasted_iota(jnp.int32, (CHUNK, CHUNK), 1)
    return jnp.where(row >= col, ws_ref[g], 0.0), row >= col


def _group_norm(v):
    mu = jnp.mean(v, axis=-1, keepdims=True)
    d = v - mu
    rstd = lax.rsqrt(jnp.mean(d * d, axis=-1, keepdims=True) + EPS)
    return d * rstd, rstd


def _sgu_fwd_body(uv_ref, ng_ref, nb_ref, ws_ref, bs_ref, out_ref):
    width = SGU_GROUPS * GROUP_DIM
    for c in range(uv_ref.shape[0] // CHUNK):
        rows = pl.ds(c * CHUNK, CHUNK)
        for g in range(SGU_GROUPS):
            lanes = pl.ds(g * GROUP_DIM, GROUP_DIM)
            u = _gelu(uv_ref[rows, lanes])
            v = _gelu(uv_ref[rows, pl.ds(width + g * GROUP_DIM, GROUP_DIM)])
            vhat, _ = _group_norm(v)
            vn = vhat * ng_ref[:, lanes] + nb_ref[:, lanes]
            w, _ = _causal_w(ws_ref, g)
            mixed = _dot(w, vn) + bs_ref[g]
            out_ref[rows, lanes] = u * mixed


def _sgu_fwd(uv_pre, ng, nb, ws, bs):
    s = uv_pre.shape[0]
    return _row_call(_sgu_fwd_body, "sgu_fwd", s, [uv_pre], [ng, nb, ws, bs],
                     [(SGU_GROUPS * GROUP_DIM, F32)], [])[0]


def _sgu_bwd_body(uv_ref, do_ref, ng_ref, nb_ref, ws_ref, bs_ref,
                  duv_ref, dws_ref, dbs_ref, dng_ref, dnb_ref):
    width = SGU_GROUPS * GROUP_DIM

    @pl.when(pl.program_id(0) == 0)
    def _():
        dws_ref[...] = jnp.zeros_like(dws_ref)
        dbs_ref[...] = jnp.zeros_like(dbs_ref)
        dng_ref[...] = jnp.zeros_like(dng_ref)
        dnb_ref[...] = jnp.zeros_like(dnb_ref)

    for c in range(uv_ref.shape[0] // CHUNK):
        rows = pl.ds(c * CHUNK, CHUNK)
        for g in range(SGU_GROUPS):
            lanes = pl.ds(g * GROUP_DIM, GROUP_DIM)
            vlanes = pl.ds(width + g * GROUP_DIM, GROUP_DIM)
            u_pre = uv_ref[rows, lanes]
            v_pre = uv_ref[rows, vlanes]
            u = _gelu(u_pre)
            v = _gelu(v_pre)
            vhat, rstd = _group_norm(v)
            gain = ng_ref[:, lanes]
            vn = vhat * gain + nb_ref[:, lanes]
            w, causal = _causal_w(ws_ref, g)
            mixed = _dot(w, vn) + bs_ref[g]
            dout = do_ref[rows, lanes]
            du = dout * mixed
            dmixed = dout * u
            dbs_ref[g] += jnp.sum(dmixed, axis=1, keepdims=True)
            dws_ref[g] += jnp.where(causal, _dot_nt(dmixed, vn), 0.0)
            dvn = _dot_tn(w, dmixed)
            dng_ref[:, lanes] += jnp.sum(dvn * vhat, axis=0, keepdims=True)
            dnb_ref[:, lanes] += jnp.sum(dvn, axis=0, keepdims=True)
            dvh = dvn * gain
            dv = rstd * (dvh - jnp.mean(dvh, axis=-1, keepdims=True)
                         - vhat * jnp.mean(dvh * vhat, axis=-1, keepdims=True))
            duv_ref[rows, lanes] = (du * _gelu_grad(u_pre)).astype(BF16)
            duv_ref[rows, vlanes] = (dv * _gelu_grad(v_pre)).astype(BF16)


def _sgu_bwd(uv_pre, dout_a, ng, nb, ws, bs):
    s = uv_pre.shape[0]
    width = SGU_GROUPS * GROUP_DIM
    return _row_call(_sgu_bwd_body, "sgu_bwd", s, [uv_pre, dout_a], [ng, nb, ws, bs],
                     [(2 * width, BF16)],
                     [(ws.shape, F32), (bs.shape, F32), ((1, width), F32), ((1, width), F32)])


def _mesh_place():
    return lax.axis_index("x"), lax.axis_index("y"), lax.axis_index("c")


def _other_chips(mx, my):
    return [(1 - mx, my), (mx, 1 - my), (1 - mx, 1 - my)]


_ANY = pl.BlockSpec(memory_space=pl.ANY)
AG_SEMS = 7
_AG_SCRATCH = [pltpu.SemaphoreType.DMA((AG_SEMS,)), pltpu.SemaphoreType.DMA((AG_SEMS,)),
               pltpu.SemaphoreType.DMA(())]


def _gather_phases(x_ref, out_ref, send_sems, recv_sems, local_sem):
    mx, my, mc = _mesh_place()
    me, sibling = (mx, my, mc), (mx, my, 1 - mc)
    chips = _other_chips(mx, my)

    def slot(px, py, pc):
        return out_ref.at[4 * px + 2 * py + pc]

    def copy(k, block, to, src=None):
        return pltpu.make_async_remote_copy(
            src_ref=slot(*block) if src is None else src, dst_ref=slot(*block),
            send_sem=send_sems.at[k], recv_sem=recv_sems.at[k],
            device_id=to, device_id_type=MESH)

    mine = pltpu.make_async_copy(x_ref, slot(*me), local_sem)
    first = [copy(0, me, sibling, src=x_ref)]
    first += [copy(1 + j, me, (*chip, mc), src=x_ref) for j, chip in enumerate(chips)]
    passed = [copy(4 + j, (*chip, mc), sibling) for j, chip in enumerate(chips)]

    def start():
        mine.start()
        for cp in first:
            cp.start()

    def forward():
        for j, chip in enumerate(chips):
            copy(1 + j, (*chip, mc), me).wait_recv()
            passed[j].start()

    def finish():
        copy(0, sibling, me).wait_recv()
        for j, chip in enumerate(chips):
            copy(4 + j, (*chip, 1 - mc), me).wait_recv()
        for cp in first + passed:
            cp.wait_send()
        mine.wait()

    return start, forward, finish


def _all_gather(x, name):
    r, c = x.shape

    def body(x_ref, out_ref, send_sems, recv_sems, local_sem):
        start, forward, finish = _gather_phases(x_ref, out_ref, send_sems, recv_sems, local_sem)
        start()
        forward()
        finish()

    return pl.pallas_call(
        body,
        name=name,
        out_shape=jax.ShapeDtypeStruct((N_DEV, r, c), x.dtype),
        in_specs=[_ANY],
        out_specs=_ANY,
        scratch_shapes=list(_AG_SCRATCH),
    )(x)


def _scatter_phases(p_ref, out_ref, send_sems, recv_sems, local_sem):
    mx, my, mc = _mesh_place()
    me = 4 * mx + 2 * my + mc
    copies = []
    for k in range(1, N_DEV):
        tx, ty, tc = mx ^ ((k >> 2) & 1), my ^ ((k >> 1) & 1), mc ^ (k & 1)
        copies.append(pltpu.make_async_remote_copy(
            src_ref=p_ref.at[4 * tx + 2 * ty + tc], dst_ref=out_ref.at[me],
            send_sem=send_sems.at[k - 1], recv_sem=recv_sems.at[k - 1],
            device_id=(tx, ty, tc), device_id_type=MESH))
    mine = pltpu.make_async_copy(p_ref.at[me], out_ref.at[me], local_sem)

    def start():
        mine.start()
        for cp in copies:
            cp.start()

    def finish():
        for cp in copies:
            cp.wait()
        mine.wait()

    return start, finish


SB_DEAD = -105.0
HEADS_PER_TILE = 2
TILES_PER_STEP = 2
HEADS_PER_STEP = HEADS_PER_TILE * TILES_PER_STEP
STEP_LANES = TILES_PER_STEP * HEADS_PER_TILE * SB_HEAD_DIM
TILE_LANES = HEADS_PER_TILE * SB_HEAD_DIM
STACK_ROWS = HEADS_PER_STEP * Q_BLOCK
TILE_ROWS = HEADS_PER_TILE * Q_BLOCK
SB_FORWARD_LEAD = 12


def _stack_heads(x):
    lane = lax.broadcasted_iota(jnp.int32, x.shape, 1)
    zero = jnp.zeros_like(x)
    return jnp.concatenate(
        [jnp.where(lane // SB_HEAD_DIM == h, x, zero) for h in range(HEADS_PER_STEP)], axis=0)


def _unstack_tile(x):
    first = lax.broadcasted_iota(jnp.int32, (Q_BLOCK, TILE_LANES), 1) < SB_HEAD_DIM
    return jnp.where(first, x[:Q_BLOCK], x[Q_BLOCK:])


def _sb_logs(qs, k, diagonal):
    z = _dot_nt(qs, k) * (SB_HEAD_DIM ** -0.5)
    sp = jnp.log1p(jnp.exp(-jnp.abs(z)))
    log_beta = jnp.minimum(z, 0.0) - sp
    log_1m_raw = -jnp.maximum(z, 0.0) - sp
    if not diagonal:
        return None, log_beta, log_1m_raw, log_1m_raw
    row = lax.broadcasted_iota(jnp.int32, z.shape, 0)
    col = lax.broadcasted_iota(jnp.int32, z.shape, 1)
    strict = col < jnp.bitwise_and(row, Q_BLOCK - 1)
    return strict, log_beta, log_1m_raw, jnp.where(strict, log_1m_raw, 0.0)


def _masked(strict, x):
    return x if strict is None else jnp.where(strict, x, 0.0)


def _key_sums(x, pick):
    hi, lo = _split_bf16(x)
    both = jnp.dot(jnp.concatenate([hi, lo], axis=0), pick, preferred_element_type=F32)
    return both[:x.shape[0]] + both[x.shape[0]:]


def _key_order():
    row = lax.broadcasted_iota(jnp.int32, (Q_BLOCK, Q_BLOCK), 0)
    col = lax.broadcasted_iota(jnp.int32, (Q_BLOCK, Q_BLOCK), 1)
    return row, col


def _sb_fwd_body(q_ref, k_ref, v_ref, shard_ref, o_ref, tot_ref, cnt_ref, gathered_ref,
                 acc_ref, send_sems, recv_sems, local_sem):
    grp, qb = pl.program_id(0), pl.program_id(1)
    last_grp, last_qb = pl.num_programs(0) - 1, pl.num_programs(1) - 1
    ag_start, ag_forward, ag_finish = _gather_phases(shard_ref, gathered_ref, send_sems, recv_sems, local_sem)
    pl.when(jnp.logical_and(grp == 0, qb == 0))(ag_start)

    qs = _stack_heads(q_ref[...])
    row, col = _key_order()
    later = (row > col).astype(BF16)

    def block(i, c, diagonal):
        rows = pl.ds(pl.multiple_of((qb - i) * Q_BLOCK, Q_BLOCK), Q_BLOCK)
        strict, log_beta, _, log_1m = _sb_logs(qs, k_ref[rows, :], diagonal)
        a = _masked(strict, jnp.exp(log_beta + _key_sums(log_1m, later) + c)).astype(BF16)
        for t in range(TILES_PER_STEP):
            part = jnp.dot(a[t * TILE_ROWS:(t + 1) * TILE_ROWS], v_ref[rows, t * TILE_LANES:(t + 1) * TILE_LANES],
                           preferred_element_type=F32)
            if diagonal:
                acc_ref[t] = part
            else:
                acc_ref[t] += part
        return c + jnp.sum(log_1m, axis=1, keepdims=True)

    c = block(0, jnp.zeros((STACK_ROWS, 1), F32), True)

    def alive(carry):
        i, c = carry
        return jnp.logical_and(i <= qb, jnp.max(c) > SB_DEAD)

    def step(carry):
        i, c = carry
        return i + 1, block(i, c, False)

    n, c = lax.while_loop(alive, step, (jnp.int32(1), c))
    for t in range(TILES_PER_STEP):
        o_ref[:, t * TILE_LANES:(t + 1) * TILE_LANES] = _unstack_tile(acc_ref[t])
    for h in range(HEADS_PER_STEP):
        tot_ref[h] = c[h * Q_BLOCK:(h + 1) * Q_BLOCK]
    cnt_ref[grp, qb] = n.astype(F32)
    pl.when(jnp.logical_and(grp == last_grp, qb == jnp.maximum(last_qb - SB_FORWARD_LEAD, 0)))(ag_forward)
    pl.when(jnp.logical_and(grp == last_grp, qb == last_qb))(ag_finish)


def _sb_fwd(qkv, shard):
    s = qkv.shape[0]
    groups = SB_HEADS // HEADS_PER_STEP
    nq = s // Q_BLOCK
    return pl.pallas_call(
        functools.partial(_sb_fwd_body),
        name="sb_fwd",
        grid=(groups, nq),
        in_specs=[pl.BlockSpec((Q_BLOCK, STEP_LANES), lambda g, i: (i, g)),
                  pl.BlockSpec((s, STEP_LANES), lambda g, i: (0, groups + g)),
                  pl.BlockSpec((s, STEP_LANES), lambda g, i: (0, 2 * groups + g)),
                  _ANY],
        out_specs=[pl.BlockSpec((Q_BLOCK, STEP_LANES), lambda g, i: (i, g)),
                   pl.BlockSpec((HEADS_PER_STEP, Q_BLOCK, 1), lambda g, i: (g, i, 0)),
                   pl.BlockSpec(memory_space=pltpu.SMEM),
                   _ANY],
        out_shape=[jax.ShapeDtypeStruct((s, SB_HEADS * SB_HEAD_DIM), F32),
                   jax.ShapeDtypeStruct((SB_HEADS, s, 1), F32),
                   jax.ShapeDtypeStruct((groups, nq), F32),
                   jax.ShapeDtypeStruct((N_DEV,) + shard.shape, shard.dtype)],
        scratch_shapes=[pltpu.VMEM((TILES_PER_STEP, TILE_ROWS, TILE_LANES), F32)] + list(_AG_SCRATCH),
        compiler_params=pltpu.CompilerParams(
            dimension_semantics=("arbitrary", "arbitrary"), vmem_limit_bytes=VMEM_LIMIT_V7X),
    )(qkv, qkv, qkv, shard)


def _sb_bwd_body(cnt_ref, q_ref, k_ref, v_ref, tot_ref, do_ref, part_ref, dq_ref, dk_ref, dv_ref, recv_ref,
                 acc_ref, send_sems, recv_sems, local_sem):
    grp, qb = pl.program_id(0), pl.program_id(1)
    last_grp, last_qb = pl.num_programs(0) - 1, pl.num_programs(1) - 1
    rs_start, rs_finish = _scatter_phases(part_ref, recv_ref, send_sems, recv_sems, local_sem)
    pl.when(jnp.logical_and(grp == 0, qb == 0))(rs_start)

    @pl.when(qb == 0)
    def _():
        dk_ref[...] = jnp.zeros_like(dk_ref)
        dv_ref[...] = jnp.zeros_like(dv_ref)

    acc_ref[...] = jnp.zeros_like(acc_ref)
    qs = _stack_heads(q_ref[...])
    dos = _stack_heads(do_ref[...].astype(BF16))
    tot = jnp.concatenate([tot_ref[h] for h in range(HEADS_PER_STEP)], axis=0)
    row, col = _key_order()
    up_to = (row <= col).astype(BF16)
    earlier = (row < col).astype(BF16)
    scale = SB_HEAD_DIM ** -0.5
    n = jnp.clip(cnt_ref[grp, qb].astype(jnp.int32), 1, qb + 1)

    def block(kb, c, ce, diagonal):
        rows = pl.ds(pl.multiple_of(kb * Q_BLOCK, Q_BLOCK), Q_BLOCK)
        k = k_ref[rows, :]
        strict, log_beta, log_1m_raw, log_1m = _sb_logs(qs, k, diagonal)
        suffix = tot - c - _key_sums(log_1m, up_to)
        a = _masked(strict, jnp.exp(log_beta + suffix))
        de = _dot_nt(dos, v_ref[rows, :]) * a
        before = ce + _key_sums(de, earlier)
        dz = _masked(strict, de * jnp.exp(log_1m_raw) - before * jnp.exp(log_beta)).astype(BF16)
        for t in range(TILES_PER_STEP):
            acc_ref[t] += jnp.dot(dz[t * TILE_ROWS:(t + 1) * TILE_ROWS], k[:, t * TILE_LANES:(t + 1) * TILE_LANES],
                                  preferred_element_type=F32)
        dk_ref[rows, :] += _dot_tn(dz, qs) * scale
        dv_ref[rows, :] += _dot_tn(a, dos)
        return c + jnp.sum(log_1m, axis=1, keepdims=True), ce + jnp.sum(de, axis=1, keepdims=True)

    def step(i, carry):
        return block(qb - n + 1 + i, *carry, False)

    zc = jnp.zeros((STACK_ROWS, 1), F32)
    c, ce = lax.fori_loop(0, n - 1, step, (zc, zc))
    block(qb, c, ce, True)
    for t in range(TILES_PER_STEP):
        dq_ref[:, t * TILE_LANES:(t + 1) * TILE_LANES] = _unstack_tile(acc_ref[t]) * scale
    pl.when(jnp.logical_and(grp == last_grp, qb == last_qb))(rs_finish)


def _sb_bwd(cnt, qkv, tot, dout_b, parts):
    s = qkv.shape[0]
    groups = SB_HEADS // HEADS_PER_STEP
    return pl.pallas_call(
        functools.partial(_sb_bwd_body),
        name="sb_bwd",
        grid=(groups, s // Q_BLOCK),
        in_specs=[pl.BlockSpec(memory_space=pltpu.SMEM),
                  pl.BlockSpec((Q_BLOCK, STEP_LANES), lambda g, i: (i, g)),
                  pl.BlockSpec((s, STEP_LANES), lambda g, i: (0, groups + g)),
                  pl.BlockSpec((s, STEP_LANES), lambda g, i: (0, 2 * groups + g)),
                  pl.BlockSpec((HEADS_PER_STEP, Q_BLOCK, 1), lambda g, i: (g, i, 0)),
                  pl.BlockSpec((Q_BLOCK, STEP_LANES), lambda g, i: (i, g)),
                  _ANY],
        out_specs=[pl.BlockSpec((Q_BLOCK, STEP_LANES), lambda g, i: (i, g)),
                   pl.BlockSpec((s, STEP_LANES), lambda g, i: (0, g)),
                   pl.BlockSpec((s, STEP_LANES), lambda g, i: (0, g)),
                   _ANY],
        out_shape=[jax.ShapeDtypeStruct((s, SB_HEADS * SB_HEAD_DIM), F32)] * 3
        + [jax.ShapeDtypeStruct(parts.shape, parts.dtype)],
        scratch_shapes=[pltpu.VMEM((TILES_PER_STEP, TILE_ROWS, TILE_LANES), F32)] + list(_AG_SCRATCH),
        compiler_params=pltpu.CompilerParams(
            dimension_semantics=("arbitrary", "arbitrary"), vmem_limit_bytes=VMEM_LIMIT_V7X),
    )(cnt, qkv, qkv, qkv, tot, dout_b, parts)


def _outproj_fwd_body(oa_ref, ob_ref, h_ref, ga_ref, gb_ref, gpost_ref, w_ref,
                      merged_ref, mo_ref, hout_ref):
    half = oa_ref.shape[1]
    ma, _ = _rms(oa_ref[...], ga_ref[...])
    mb, _ = _rms(ob_ref[...], gb_ref[...])
    mab = ma.astype(BF16)
    mbb = mb.astype(BF16)
    merged_ref[:, :half] = mab
    merged_ref[:, half:] = mbb
    mo = (jnp.dot(mab, w_ref[:half, :], preferred_element_type=F32)
          + jnp.dot(mbb, w_ref[half:, :], preferred_element_type=F32))
    mo_ref[...] = mo
    y, _ = _rms(mo, gpost_ref[...])
    hout_ref[...] = h_ref[...] + y


def _outproj_fwd(out_a, out_b, h, ga, gb, gpost, w_out):
    s, d = h.shape
    return _row_call(_outproj_fwd_body, "outproj_fwd", s, [out_a, out_b, h], [ga, gb, gpost, w_out],
                     [(d, BF16), (d, F32), (d, F32)], [])


def _outproj_bwd_body(dh_ref, mo_ref, oa_ref, ob_ref, ga_ref, gb_ref, gpost_ref, w_ref,
                      dmo_ref, doa_ref, dob_ref, dga_ref, dgb_ref, dgpost_ref):
    half = oa_ref.shape[1]
    dmo, dgpost = _rms_bwd(mo_ref[...], gpost_ref[...], dh_ref[...])
    dmob = dmo.astype(BF16)
    dmo_ref[...] = dmob
    dma = _dot_nt(dmob, w_ref[:half, :])
    dmb = _dot_nt(dmob, w_ref[half:, :])
    doa, dga = _rms_bwd(oa_ref[...], ga_ref[...], dma)
    dob, dgb = _rms_bwd(ob_ref[...], gb_ref[...], dmb)
    doa_ref[...] = doa
    dob_ref[...] = dob
    _acc(dga_ref, dga)
    _acc(dgb_ref, dgb)
    _acc(dgpost_ref, dgpost)


def _outproj_bwd(dh, mo, out_a, out_b, ga, gb, gpost, w_out):
    s, d = dh.shape
    half = out_a.shape[1]
    return _row_call(_outproj_bwd_body, "outproj_bwd", s, [dh, mo, out_a, out_b], [ga, gb, gpost, w_out],
                     [(d, BF16), (half, F32), (half, F32)],
                     [((1, half), F32), ((1, half), F32), ((1, d), F32)])


def _kv_fwd_body(mem_ref, g_ref, wt_ref, memn_ref, kv_ref):
    n, _ = _rms(mem_ref[...], g_ref[...])
    nb = n.astype(BF16)
    memn_ref[...] = nb
    kv_ref[...] = _dot_nt(nb, wt_ref[...]).astype(BF16)


def _kv_fwd(mem, g, w_kv_t):
    m, d = mem.shape
    return _row_call(_kv_fwd_body, "kv_fwd", m, [mem], [g, w_kv_t], [(d, BF16), (w_kv_t.shape[0], BF16)], [])


def _kv_bwd_body(dkv_ref, mem_ref, memn_ref, g_ref, wt_ref, dwt_ref, dg_ref):
    dkvb = dkv_ref[...].astype(BF16)
    dwt_ref[...] = _dot_tn(dkvb, memn_ref[...]).astype(BF16)
    dmemn = _dot(dkvb, wt_ref[...])
    _, dg = _rms_bwd(mem_ref[...], g_ref[...], dmemn)
    dg_ref[...] = dg


def _kv_bwd(dkv, mem, memn, g, w_kv_t):
    m, d = mem.shape
    return pl.pallas_call(
        functools.partial(_kv_bwd_body),
        name="kv_bwd",
        out_shape=[jax.ShapeDtypeStruct(w_kv_t.shape, BF16), jax.ShapeDtypeStruct((1, d), F32)],
        compiler_params=pltpu.CompilerParams(vmem_limit_bytes=VMEM_LIMIT_V7X),
    )(dkv, mem, memn, g, w_kv_t)


def _xa_fwd_body(h_ref, gpre_ref, gpost_ref, wq_ref, wo_ref, kv_ref,
                 n_ref, q_ref, o_ref, c_ref, hout_ref):
    h = h_ref[...]
    d = h.shape[1]
    n, _ = _rms(h, gpre_ref[...])
    nb = n.astype(BF16)
    n_ref[...] = nb
    qb = jnp.dot(nb, wq_ref[...], preferred_element_type=F32).astype(BF16)
    q_ref[...] = qb
    for hd in range(XA_HEADS):
        lanes = slice(hd * XA_HEAD_DIM, (hd + 1) * XA_HEAD_DIM)
        k = kv_ref[:, lanes]
        v = kv_ref[:, d + hd * XA_HEAD_DIM:d + (hd + 1) * XA_HEAD_DIM]
        logits = _dot_nt(qb[:, lanes], k) * (XA_HEAD_DIM ** -0.5)
        e = jnp.exp(logits - jnp.max(logits, axis=-1, keepdims=True))
        p = e / jnp.sum(e, axis=-1, keepdims=True)
        o_ref[:, lanes] = jnp.dot(p.astype(BF16), v, preferred_element_type=F32).astype(BF16)
    c = jnp.dot(o_ref[...], wo_ref[...], preferred_element_type=F32)
    c_ref[...] = c
    y, _ = _rms(c, gpost_ref[...])
    hout_ref[...] = h + y


def _xa_fwd(h, gpre, gpost, wq, wo, kv):
    s, d = h.shape
    return _row_call(_xa_fwd_body, "xa_fwd", s, [h], [gpre, gpost, wq, wo, kv],
                     [(d, BF16), (d, BF16), (d, BF16), (d, F32), (d, F32)], [])


def _xa_bwd_body(dh_ref, h_ref, c_ref, q_ref, o_ref, gpre_ref, gpost_ref, wq_ref, wo_ref, kv_ref,
                 dhout_ref, dc_ref, dq_ref, dkv_ref, dgpre_ref, dgpost_ref):
    dh = dh_ref[...]
    d = dh.shape[1]
    scale = XA_HEAD_DIM ** -0.5
    dc, dgpost = _rms_bwd(c_ref[...], gpost_ref[...], dh)
    dcb = dc.astype(BF16)
    dc_ref[...] = dcb
    dob = _dot_nt(dcb, wo_ref[...]).astype(BF16)

    @pl.when(pl.program_id(0) == 0)
    def _():
        dkv_ref[...] = jnp.zeros_like(dkv_ref)

    for hd in range(XA_HEADS):
        lanes = slice(hd * XA_HEAD_DIM, (hd + 1) * XA_HEAD_DIM)
        vlanes = slice(d + hd * XA_HEAD_DIM, d + (hd + 1) * XA_HEAD_DIM)
        qh = q_ref[:, lanes]
        k = kv_ref[:, lanes]
        v = kv_ref[:, vlanes]
        logits = _dot_nt(qh, k) * scale
        e = jnp.exp(logits - jnp.max(logits, axis=-1, keepdims=True))
        p = e / jnp.sum(e, axis=-1, keepdims=True)
        doh = dob[:, lanes]
        dp = _dot_nt(doh, v)
        dl = (p * (dp - jnp.sum(dp * p, axis=-1, keepdims=True)) * scale).astype(BF16)
        dq_ref[:, lanes] = jnp.dot(dl, k, preferred_element_type=F32).astype(BF16)
        dkv_ref[:, lanes] += _dot_tn(dl, qh)
        dkv_ref[:, vlanes] += _dot_tn(p, doh)
    dn = _dot_nt(dq_ref[...], wq_ref[...])
    dhn, dgpre = _rms_bwd(h_ref[...], gpre_ref[...], dn)
    dhout_ref[...] = dh + dhn
    _acc(dgpre_ref, dgpre)
    _acc(dgpost_ref, dgpost)


def _xa_bwd(dh, h, c, q, o, gpre, gpost, wq, wo, kv, scatter=None):
    s, d = h.shape
    return _row_call(_xa_bwd_body, "xa_bwd", s, [dh, h, c, q, o], [gpre, gpost, wq, wo, kv],
                     [(d, F32), (d, BF16), (d, BF16)],
                     [(kv.shape, F32), ((1, d), F32), ((1, d), F32)], scatter=scatter)


def _largest_tile(n, cap):
    best = 128
    for t in range(128, cap + 1, 128):
        if n % t == 0:
            best = t
    return best


def _mm_tn(a, bs, name, gather=None):
    s, k = a.shape
    n = bs[0].shape[1]
    nb = len(bs)
    ts = min(2048, s)
    tk = _largest_tile(k, 1536)
    tn = _largest_tile(n, 1536 // nb)

    steps = s // ts
    grid = (k // tk, n // tn, steps)
    hosted = gather is not None

    def body(a_ref, *refs):
        if hosted:
            sent_ref, landed_ref, refs = refs[nb], refs[2 * nb + 1], refs[:nb] + refs[nb + 1:2 * nb + 1] + refs[2 * nb + 2:]
            start, forward, finish = _gather_phases(sent_ref, landed_ref, *refs[-3:])
            place = (pl.program_id(0) * grid[1] + pl.program_id(1)) * grid[2] + pl.program_id(2)
            pl.when(place == 0)(start)
        b_refs, o_refs, acc_refs = refs[:nb], refs[nb:2 * nb], refs[2 * nb:3 * nb]
        at = a_ref[...]
        t = pl.program_id(2)

        @pl.when(t == 0)
        def _():
            for acc_ref in acc_refs:
                acc_ref[...] = jnp.zeros_like(acc_ref)

        for b_ref, acc_ref in zip(b_refs, acc_refs):
            acc_ref[...] += _dot_tn(at, b_ref[...])

        @pl.when(t == steps - 1)
        def _():
            for o_ref, acc_ref in zip(o_refs, acc_refs):
                o_ref[...] = acc_ref[...].astype(BF16)

        if hosted:
            total = grid[0] * grid[1] * grid[2]

            @pl.when(place == total - 1)
            def _():
                forward()
                finish()

    return pl.pallas_call(
        body,
        name=name,
        grid=grid,
        in_specs=[pl.BlockSpec((ts, tk), lambda i, j, t: (t, i))]
        + [pl.BlockSpec((ts, tn), lambda i, j, t: (t, j))] * nb + ([_ANY] if hosted else []),
        out_specs=[pl.BlockSpec((tk, tn), lambda i, j, t: (i, j))] * nb + ([_ANY] if hosted else []),
        out_shape=[jax.ShapeDtypeStruct((k, n), BF16)] * nb
        + ([jax.ShapeDtypeStruct((N_DEV,) + gather.shape, gather.dtype)] if hosted else []),
        scratch_shapes=[pltpu.VMEM((tk, tn), F32)] * nb + (list(_AG_SCRATCH) if hosted else []),
        compiler_params=pltpu.CompilerParams(
            dimension_semantics=("arbitrary", "arbitrary", "arbitrary"),
            vmem_limit_bytes=VMEM_LIMIT_V7X),
    )(a, *bs, *([gather] if hosted else []))


_SMALL_SHAPES = {
    "sgu_norm_g": (1, SGU_GROUPS * GROUP_DIM),
    "sgu_norm_b": (1, SGU_GROUPS * GROUP_DIM),
    "sgu_w_s": (SGU_GROUPS, CHUNK, CHUNK),
    "sgu_b_s": (SGU_GROUPS, CHUNK, 1),
}


def _small_views(small):
    return {n: v.reshape(_SMALL_SHAPES.get(n, v.shape)) for n, v in small.items()}


def _small_unviews(views, like):
    return {n: v.reshape(like[n].shape) for n, v in views.items()}


def _unpack_rows(gathered, names, shard_rows):
    out, off = {}, 0
    for n in names:
        rows = shard_rows[n]
        out[n] = gathered[:, off:off + rows, :].reshape(N_DEV * rows, gathered.shape[2])
        off += rows
    return out


def _row_tile(r, cap):
    best = 16
    for t in range(16, cap + 1, 16):
        if r % t == 0:
            best = t
    return best


def _sum_received(received, name):
    _, r, c = received.shape
    tr = _row_tile(r, 1024)

    def body(rc_ref, g_ref):
        g = rc_ref[0].astype(F32)
        for t in range(1, N_DEV):
            g = g + rc_ref[t].astype(F32)
        g_ref[...] = g

    return pl.pallas_call(
        body, name=name, grid=(r // tr,),
        in_specs=[pl.BlockSpec((N_DEV, tr, c), lambda i: (0, i, 0))],
        out_specs=pl.BlockSpec((tr, c), lambda i: (i, 0)),
        out_shape=jax.ShapeDtypeStruct((r, c), F32),
    )(received)


def _local_step(x, mem, target, small, big, shards, shard_rows):
    sm, w = small, dict(big)
    d_model = x.shape[1]

    def parts(names):
        return jnp.concatenate([gw.pop(n).reshape(N_DEV, -1, d_model) for n in names], axis=1)

    h1, n1, a1, b1, f1, landed = _ffn_fwd(
        x, sm["ffn1_pre_g"], sm["ffn1_post_g"], w["ffn1_w_gate"], w["ffn1_w_up"], w["ffn1_w_down"],
        "ffn1_fwd", gather=shards["ffn1_fwd"])
    w.update(_unpack_rows(landed, GATHER_IN["ffn1_fwd"], shard_rows))
    n2, uv_pre, qkv = _inproj_fwd(h1, sm["mix_pre_g"], w["w_in"])
    out_a = _sgu_fwd(uv_pre, sm["sgu_norm_g"], sm["sgu_norm_b"], sm["sgu_w_s"], sm["sgu_b_s"])
    out_b, tot, cnt, landed = _sb_fwd(qkv, shards["sb_fwd"])
    w.update(_unpack_rows(landed, GATHER_IN["sb_fwd"], shard_rows))
    merged, mo, h2 = _outproj_fwd(out_a, out_b, h1, sm["sgu_out_g"], sm["sb_out_g"],
                                  sm["mix_post_g"], w["w_out"])
    memn, kv = _kv_fwd(mem, sm["mem_norm_g"], w["xa_w_kv"])
    n3, qx, ox, cx, h3 = _xa_fwd(h2, sm["xa_pre_g"], sm["xa_post_g"], w["xa_w_q"], w["xa_w_o"], kv)
    n4, a2, b2, f2, dh4, loss, dg_final = _ffn_loss_fwd(
        h3, target, sm["ffn2_pre_g"], sm["ffn2_post_g"], sm["final_norm_g"],
        w["ffn2_w_gate"], w["ffn2_w_up"], w["ffn2_w_down"], "ffn2_fwd")

    gs, gw = {"final_norm_g": dg_final}, {}
    dh3, da2, db2, hm2, df2, gs["ffn2_pre_g"], gs["ffn2_post_g"] = _ffn_bwd(
        dh4, h3, a2, b2, f2, sm["ffn2_pre_g"], sm["ffn2_post_g"],
        w["ffn2_w_gate"], w["ffn2_w_up"], w["ffn2_w_down"], "ffn2_bwd")
    gw["ffn2_w_gate"], = _mm_tn(da2, [n4], "ffn2_dw_gate")
    gw["ffn2_w_up"], = _mm_tn(db2, [n4], "ffn2_dw_up")
    gw["ffn2_w_down"], = _mm_tn(hm2, [df2], "ffn2_dw_down")

    received = {}
    dh2, dc, dqx, dkv, gs["xa_pre_g"], gs["xa_post_g"], received["xa_bwd"] = _xa_bwd(
        dh3, h2, cx, qx, ox, sm["xa_pre_g"], sm["xa_post_g"], w["xa_w_q"], w["xa_w_o"], kv,
        scatter=parts(SCATTER_IN["xa_bwd"]))
    gw["xa_w_o"], = _mm_tn(ox, [dc], "xa_dw_o")
    gw["xa_w_q"], = _mm_tn(n3, [dqx], "xa_dw_q")
    gw["xa_w_kv"], gs["mem_norm_g"] = _kv_bwd(dkv, mem, memn, sm["mem_norm_g"], w["xa_w_kv"])

    dmo, dout_a, dout_b, gs["sgu_out_g"], gs["sb_out_g"], gs["mix_post_g"] = _outproj_bwd(
        dh2, mo, out_a, out_b, sm["sgu_out_g"], sm["sb_out_g"], sm["mix_post_g"], w["w_out"])
    gw["w_out"], = _mm_tn(merged, [dmo], "mix_dw_out")
    dq, dk, dv, received["sb_bwd"] = _sb_bwd(cnt, qkv, tot, dout_b, parts(SCATTER_IN["sb_bwd"]))
    duv, gs["sgu_w_s"], gs["sgu_b_s"], gs["sgu_norm_g"], gs["sgu_norm_b"] = _sgu_bwd(
        uv_pre, dout_a, sm["sgu_norm_g"], sm["sgu_norm_b"], sm["sgu_w_s"], sm["sgu_b_s"])
    dproj = jnp.concatenate([duv] + [t.astype(BF16) for t in (dq, dk, dv)], axis=1)
    dh1, gs["mix_pre_g"] = _inproj_bwd(dh2, dproj, h1, sm["mix_pre_g"], w["w_in"])
    gw["w_in"], = _mm_tn(dproj, [n2], "mix_dw_in")

    dx, da1, db1, hm1, df1, gs["ffn1_pre_g"], gs["ffn1_post_g"], received["ffn1_bwd"] = _ffn_bwd(
        dh1, x, a1, b1, f1, sm["ffn1_pre_g"], sm["ffn1_post_g"],
        w["ffn1_w_gate"], w["ffn1_w_up"], w["ffn1_w_down"], "ffn1_bwd",
        scatter=parts(SCATTER_IN["ffn1_bwd"]))
    gw["ffn1_w_gate"], = _mm_tn(da1, [n1], "ffn1_dw_gate")
    gw["ffn1_w_up"], small_grads = _mm_tn(db1, [n1], "ffn1_dw_up", gather=_pack_small(gs))
    gw["ffn1_w_down"], = _mm_tn(hm1, [df1], "ffn1_dw_down")
    summed = {host: _sum_received(r, "rs_sum_" + host) for host, r in received.items()}
    return loss, dx, small_grads, gw, summed


def _pair_exchange(p4):
    nchip, _, r, c = p4.shape

    def body(p_ref, out_ref, send_sems, recv_sems):
        mx, my, mc = _mesh_place()
        copies = [pltpu.make_async_remote_copy(
            src_ref=p_ref.at[j, 1 - mc], dst_ref=out_ref.at[j],
            send_sem=send_sems.at[j], recv_sem=recv_sems.at[j],
            device_id=(mx, my, 1 - mc), device_id_type=MESH) for j in range(nchip)]
        for cp in copies:
            cp.start()
        for cp in copies:
            cp.wait()

    return pl.pallas_call(
        body,
        name="rs_pair_exchange",
        out_shape=jax.ShapeDtypeStruct((nchip, r, c), p4.dtype),
        in_specs=[_ANY],
        out_specs=_ANY,
        scratch_shapes=[pltpu.SemaphoreType.DMA((nchip,)), pltpu.SemaphoreType.DMA((nchip,))],
    )(p4)


def _chip_exchange(q):
    _, r, c = q.shape

    def body(q_ref, out_ref, send_sems, recv_sems):
        mx, my, mc = _mesh_place()
        copies = [pltpu.make_async_remote_copy(
            src_ref=q_ref.at[2 * cx + cy], dst_ref=out_ref.at[k],
            send_sem=send_sems.at[k], recv_sem=recv_sems.at[k],
            device_id=(cx, cy, mc), device_id_type=MESH)
            for k, (cx, cy) in enumerate(_other_chips(mx, my))]
        for cp in copies:
            cp.start()
        for cp in copies:
            cp.wait()

    return pl.pallas_call(
        body,
        name="rs_chip_exchange",
        out_shape=jax.ShapeDtypeStruct((3, r, c), q.dtype),
        in_specs=[_ANY],
        out_specs=_ANY,
        scratch_shapes=[pltpu.SemaphoreType.DMA((3,)), pltpu.SemaphoreType.DMA((3,))],
    )(q)


def _rs_row_tile(r):
    return _row_tile(r, 1024)


def _pair_sum(place, p4, recv_a):
    nchip, _, r, c = p4.shape
    tr = _rs_row_tile(r)

    def body(place_ref, p_ref, a_ref, q_ref):
        q_ref[0] = (p_ref[0, 0].astype(F32) + a_ref[0].astype(F32)).astype(BF16)

    return pl.pallas_call(
        body,
        name="rs_pair_sum",
        grid_spec=pltpu.PrefetchScalarGridSpec(
            num_scalar_prefetch=1,
            grid=(nchip, r // tr),
            in_specs=[pl.BlockSpec((1, 1, tr, c), lambda j, i, pref: (j, pref[0], i, 0)),
                      pl.BlockSpec((1, tr, c), lambda j, i, pref: (j, i, 0))],
            out_specs=pl.BlockSpec((1, tr, c), lambda j, i, pref: (j, i, 0)),
        ),
        out_shape=jax.ShapeDtypeStruct((nchip, r, c), BF16),
    )(place, p4, recv_a)


def _rs_final(place, p4, recv_a, recv_b):
    _, _, r, c = p4.shape
    tr = _rs_row_tile(r)

    def body(place_ref, p_ref, a_ref, b_ref, g_ref):
        g = p_ref[0, 0].astype(F32) + a_ref[0].astype(F32)
        for k in range(3):
            g = g + b_ref[k].astype(F32)
        g_ref[...] = g

    return pl.pallas_call(
        body,
        name="rs_final_sum",
        grid_spec=pltpu.PrefetchScalarGridSpec(
            num_scalar_prefetch=1,
            grid=(r // tr,),
            in_specs=[pl.BlockSpec((1, 1, tr, c), lambda i, pref: (pref[1], pref[0], i, 0)),
                      pl.BlockSpec((1, tr, c), lambda i, pref: (pref[1], i, 0)),
                      pl.BlockSpec((3, tr, c), lambda i, pref: (0, i, 0))],
            out_specs=pl.BlockSpec((tr, c), lambda i, pref: (i, 0)),
        ),
        out_shape=jax.ShapeDtypeStruct((r, c), F32),
    )(place, p4, recv_a, recv_b)


def _adamw_math(w, g, m, v):
    m = ADAM_B1 * m + (1.0 - ADAM_B1) * g
    v = ADAM_B2 * v + (1.0 - ADAM_B2) * (g * g)
    m_hat = m / (1.0 - ADAM_B1 ** ADAM_STEP)
    v_hat = v / (1.0 - ADAM_B2 ** ADAM_STEP)
    delta = -ADAM_LR * (m_hat / (jnp.sqrt(v_hat) + ADAM_EPS) + ADAM_WD * w)
    return delta, m, v


def _adamw(w, g, m, v, name):
    r, c = w.shape
    tr = r if r <= 512 else 256

    def body(w_ref, g_ref, m_ref, v_ref, d_ref, mo_ref, vo_ref):
        d_ref[...], mo_ref[...], vo_ref[...] = _adamw_math(w_ref[...], g_ref[...], m_ref[...], v_ref[...])

    spec = pl.BlockSpec((tr, c), lambda i: (i, 0))
    out = jax.ShapeDtypeStruct((r, c), F32)
    return pl.pallas_call(
        body, name=name, grid=(r // tr,), in_specs=[spec] * 4, out_specs=[spec] * 3,
        out_shape=[out] * 3,
    )(w, g, m, v)


def _small_sum_adamw(gathered, ws, ms, vs):
    _, r, c = gathered.shape
    count = len(ws)

    def body(ga_ref, *refs):
        w_refs, m_refs, v_refs = refs[:count], refs[count:2 * count], refs[2 * count:3 * count]
        out_refs = refs[3 * count:]
        for o_ref in out_refs:
            o_ref[...] = jnp.zeros_like(o_ref)
        off = 0
        for w_ref, m_ref, v_ref in zip(w_refs, m_refs, v_refs):
            rows = pl.ds(off, w_ref.shape[0])
            g = ga_ref[0, rows, :]
            for k in range(1, N_DEV):
                g = g + ga_ref[k, rows, :]
            results = (g, *_adamw_math(w_ref[...], g, m_ref[...], v_ref[...]))
            for o_ref, val in zip(out_refs, results):
                o_ref[rows, :] = val
            off += w_ref.shape[0] + (-w_ref.shape[0]) % SMALL_ROW_ALIGN

    out = jax.ShapeDtypeStruct((r, c), F32)
    return pl.pallas_call(body, name="small_sum_adamw", out_shape=[out] * 4)(gathered, *ws, *ms, *vs)


_WEIGHTS = ["ffn1_pre_g", "ffn1_post_g", "ffn1_w_gate", "ffn1_w_up", "ffn1_w_down", "mix_pre_g",
            "mix_post_g", "w_in", "sgu_norm_g", "sgu_norm_b", "sgu_w_s", "sgu_b_s", "sgu_out_g",
            "sb_out_g", "w_out", "xa_pre_g", "xa_post_g", "mem_norm_g", "xa_w_q", "xa_w_kv", "xa_w_o",
            "ffn2_pre_g", "ffn2_post_g", "ffn2_w_gate", "ffn2_w_up", "ffn2_w_down", "final_norm_g"]
_BIG = ["ffn1_w_gate", "ffn1_w_up", "ffn1_w_down", "w_in", "w_out", "xa_w_q", "xa_w_kv", "xa_w_o",
        "ffn2_w_gate", "ffn2_w_up", "ffn2_w_down"]
_COL_SHARDED = ("ffn1_w_gate", "ffn1_w_up", "w_in", "xa_w_kv", "ffn2_w_gate", "ffn2_w_up")
_EARLY = ["ffn1_w_gate", "ffn1_w_up", "ffn1_w_down"]
GATHER_IN = {"ffn1_fwd": ["w_in", "ffn2_w_gate"],
             "sb_fwd": ["w_out", "xa_w_q", "xa_w_kv", "xa_w_o", "ffn2_w_up", "ffn2_w_down"]}
SCATTER_IN = {"xa_bwd": ["ffn2_w_gate"],
              "sb_bwd": ["ffn2_w_up", "ffn2_w_down", "xa_w_q", "xa_w_kv", "xa_w_o", "w_out"],
              "ffn1_bwd": ["w_in"]}
_SMALL = [n for n in _WEIGHTS if n not in _BIG]
SMALL_LANES = 128
SMALL_ROW_ALIGN = 8


def _pack_small(tensors):
    parts = []
    for n in _SMALL:
        t = tensors[n].reshape(-1, SMALL_LANES)
        pad = (-t.shape[0]) % SMALL_ROW_ALIGN
        parts.append(jnp.pad(t, ((0, pad), (0, 0))) if pad else t)
    return jnp.concatenate(parts, axis=0)


def _unpack_small(packed, like):
    out, off = {}, 0
    for n in _SMALL:
        size = like[n].size
        rows = size // SMALL_LANES
        out[n] = packed[off:off + rows].reshape(like[n].shape)
        off += rows + (-rows) % SMALL_ROW_ALIGN
    return out


def kernel(x, mem, ffn1_pre_g, ffn1_post_g, ffn1_w_gate, ffn1_w_up, ffn1_w_down, mix_pre_g, mix_post_g, w_in, sgu_norm_g, sgu_norm_b, sgu_w_s, sgu_b_s, sgu_out_g, sb_out_g, w_out, xa_pre_g, xa_post_g, mem_norm_g, xa_w_q, xa_w_kv, xa_w_o, ffn2_pre_g, ffn2_post_g, ffn2_w_gate, ffn2_w_up, ffn2_w_down, final_norm_g, loss_target, m_ffn1_pre_g, m_ffn1_post_g, m_ffn1_w_gate, m_ffn1_w_up, m_ffn1_w_down, m_mix_pre_g, m_mix_post_g, m_w_in, m_sgu_norm_g, m_sgu_norm_b, m_sgu_w_s, m_sgu_b_s, m_sgu_out_g, m_sb_out_g, m_w_out, m_xa_pre_g, m_xa_post_g, m_mem_norm_g, m_xa_w_q, m_xa_w_kv, m_xa_w_o, m_ffn2_pre_g, m_ffn2_post_g, m_ffn2_w_gate, m_ffn2_w_up, m_ffn2_w_down, m_final_norm_g, v_ffn1_pre_g, v_ffn1_post_g, v_ffn1_w_gate, v_ffn1_w_up, v_ffn1_w_down, v_mix_pre_g, v_mix_post_g, v_w_in, v_sgu_norm_g, v_sgu_norm_b, v_sgu_w_s, v_sgu_b_s, v_sgu_out_g, v_sb_out_g, v_w_out, v_xa_pre_g, v_xa_post_g, v_mem_norm_g, v_xa_w_q, v_xa_w_kv, v_xa_w_o, v_ffn2_pre_g, v_ffn2_post_g, v_ffn2_w_gate, v_ffn2_w_up, v_ffn2_w_down, v_final_norm_g):
    vals = dict(locals())
    d_model = x.shape[-1]

    def packed(names):
        return jnp.concatenate(
            [(vals[n][0].T if n in _COL_SHARDED else vals[n][0]).astype(BF16) for n in names], axis=0)

    shard_rows = {n: vals[n].shape[2 if n in _COL_SHARDED else 1] for n in _BIG}
    big = _unpack_rows(_all_gather(packed(_EARLY), "ag_weights"), _EARLY, shard_rows)

    small = {n: vals[n] for n in _SMALL}
    loss_part, dx, gathered_small, gw, summed = _local_step(
        x[0], mem[0], loss_target[0], _small_views(small), big,
        {host: packed(names) for host, names in GATHER_IN.items()}, shard_rows)
    loss = lax.psum(loss_part[0, 0], ("x", "y", "c"))

    parts = [gw[n].reshape(N_DEV, -1, d_model).astype(BF16) for n in _EARLY]
    rows = sum(p.shape[1] for p in parts)
    p4 = jnp.concatenate(parts, axis=1).reshape(N_DEV // 2, 2, rows, d_model)
    mx, my, mc = _mesh_place()
    place = jnp.stack([mc, 2 * mx + my]).astype(jnp.int32)
    recv_a = _pair_exchange(p4)
    recv_b = _chip_exchange(_pair_sum(place, p4, recv_a))
    early_rows = _rs_final(place, p4, recv_a, recv_b)

    grads, deltas, new_m, new_v = {}, {}, {}, {}
    for names, g_rows in [(_EARLY, early_rows)] + [(SCATTER_IN[host], summed[host]) for host in SCATTER_IN]:
        off = 0
        for n in names:
            rows = shard_rows[n]
            g = g_rows[off:off + rows]
            off += rows
            state = [vals[n][0], vals["m_" + n][0], vals["v_" + n][0]]
            flipped = n in _COL_SHARDED and rows % SMALL_LANES != 0
            if flipped:
                state = [t.T for t in state]
            elif n in _COL_SHARDED:
                g = g.T
            outs = (g, *_adamw(state[0], g, state[1], state[2], "adamw_" + n))
            if flipped:
                outs = tuple(t.T for t in outs)
            grads[n], deltas[n], new_m[n], new_v[n] = (t[None] for t in outs)

    outs = _small_sum_adamw(gathered_small,
                            *([vals[pre + n].reshape(-1, SMALL_LANES) for n in _SMALL] for pre in ("", "m_", "v_")))
    for dst, packed in zip((grads, deltas, new_m, new_v), outs):
        dst.update(_unpack_small(packed, small))

    return (loss, dx[None], *[grads[n] for n in _WEIGHTS], *[deltas[n] for n in _WEIGHTS],
            *[new_m[n] for n in _WEIGHTS], *[new_v[n] for n in _WEIGHTS])
```

```python
import functools

import jax
import jax.numpy as jnp
from jax import lax
from jax.experimental import pallas as pl
from jax.experimental.pallas import tpu as pltpu

F32 = jnp.float32
BF16 = jnp.bfloat16
EPS = 1e-6
MESH = pl.DeviceIdType.MESH
N_DEV = 8

SGU_GROUPS = 4
GROUP_DIM = 128
CHUNK = 128
SB_HEADS = 8
SB_HEAD_DIM = 64
Q_BLOCK = 128
XA_HEADS = 4
XA_HEAD_DIM = 256

ADAM_LR = 0.001
ADAM_B1 = 0.9
ADAM_B2 = 0.999
ADAM_EPS = 1e-08
ADAM_WD = 0.01
ADAM_STEP = 10

VMEM_LIMIT_V7X = 56 * 1024 * 1024
GELU_C0 = 0.7978845608028654
GELU_C1 = 0.044715


def _dot(a, b):
    return jnp.dot(a.astype(BF16), b.astype(BF16), preferred_element_type=F32)


def _dot_nt(a, b):
    return lax.dot_general(a.astype(BF16), b.astype(BF16), (((1,), (1,)), ((), ())),
                           preferred_element_type=F32)


def _dot_tn(a, b):
    return lax.dot_general(a.astype(BF16), b.astype(BF16), (((0,), (0,)), ((), ())),
                           preferred_element_type=F32)


def _rms(x, g):
    r = lax.rsqrt(jnp.mean(x * x, axis=-1, keepdims=True) + EPS)
    return x * r * g, r


def _rms_bwd(x, g, dy):
    r = lax.rsqrt(jnp.mean(x * x, axis=-1, keepdims=True) + EPS)
    xh = x * r
    gy = dy * g
    dx = r * (gy - xh * jnp.mean(gy * xh, axis=-1, keepdims=True))
    dg = jnp.sum(dy * xh, axis=0, keepdims=True)
    return dx, dg


def _sigmoid(x):
    return jax.nn.sigmoid(x)


def _gelu(x):
    t = jnp.tanh(GELU_C0 * (x + GELU_C1 * x * x * x))
    return 0.5 * x * (1.0 + t)


def _gelu_grad(x):
    t = jnp.tanh(GELU_C0 * (x + GELU_C1 * x * x * x))
    return 0.5 * (1.0 + t) + 0.5 * x * (1.0 - t * t) * GELU_C0 * (1.0 + 3.0 * GELU_C1 * x * x)


def _split_bf16(x):
    hi = x.astype(BF16)
    lo = (x - hi.astype(F32)).astype(BF16)
    return hi, lo


def _row_spec(tm, cols):
    return pl.BlockSpec((tm, cols), lambda i: (i, 0))


def _full_spec(shape, buffers=None):
    nd = len(shape)
    mode = None if buffers is None else pl.Buffered(buffers)
    return pl.BlockSpec(tuple(shape), lambda i: (0,) * nd, pipeline_mode=mode)


FFN_TOKEN_TILE = 512


def _token_tile(s):
    return min(256, s)


def _row_call(body, name, s, tiled_in, full_in, tiled_out, acc_out, gather=None, scatter=None, tm=None):
    tm = _token_tile(s) if tm is None else min(tm, s)
    steps = s // tm
    in_specs = [_row_spec(tm, a.shape[1]) for a in tiled_in] + [_full_spec(a.shape, buffers=1) for a in full_in]
    out_specs = [_row_spec(tm, c) for c, _ in tiled_out] + [_full_spec(sh) for sh, _ in acc_out]
    out_shape = [jax.ShapeDtypeStruct((s, c), dt) for c, dt in tiled_out]
    out_shape += [jax.ShapeDtypeStruct(sh, dt) for sh, dt in acc_out]
    operands = [*tiled_in, *full_in]
    scratch = []
    kernel_body = functools.partial(body)
    sent = gather if gather is not None else scatter
    if sent is not None:
        n_in, n_out = len(operands), len(out_shape)
        out_shape.append(jax.ShapeDtypeStruct(
            (N_DEV,) + sent.shape if gather is not None else sent.shape, sent.dtype))
        operands.append(sent)
        in_specs.append(_ANY)
        out_specs.append(_ANY)
        scratch = list(_AG_SCRATCH)

        def kernel_body(*refs):
            ins, sent_ref = refs[:n_in], refs[n_in]
            outs, landed_ref = refs[n_in + 1:n_in + 1 + n_out], refs[n_in + 1 + n_out]
            step = pl.program_id(0)
            if gather is not None:
                start, forward, finish = _gather_phases(sent_ref, landed_ref, *refs[-3:])
            else:
                start, finish = _scatter_phases(sent_ref, landed_ref, *refs[-3:])
            pl.when(step == 0)(start)
            body(*ins, *outs)
            if gather is not None:
                pl.when(step == (2 * steps) // 3)(forward)
            pl.when(step == steps - 1)(finish)

    return pl.pallas_call(
        kernel_body,
        name=name,
        grid=(steps,),
        in_specs=in_specs,
        out_specs=out_specs,
        out_shape=out_shape,
        scratch_shapes=scratch,
        compiler_params=pltpu.CompilerParams(
            dimension_semantics=("arbitrary",), vmem_limit_bytes=VMEM_LIMIT_V7X),
    )(*operands)


def _acc(ref, val):
    @pl.when(pl.program_id(0) == 0)
    def _():
        ref[...] = val

    @pl.when(pl.program_id(0) != 0)
    def _():
        ref[...] += val


def _ffn_fwd_tile(x_ref, pre_ref, post_ref, wgt_ref, wut_ref, wd_ref, n_ref, a_ref, b_ref, f_ref):
    x = x_ref[...]
    n, _ = _rms(x, pre_ref[...])
    nb = n.astype(BF16)
    n_ref[...] = nb
    a = _dot_nt(nb, wgt_ref[...])
    b = _dot_nt(nb, wut_ref[...])
    a_ref[...] = a.astype(BF16)
    b_ref[...] = b.astype(BF16)
    hmid = a * _sigmoid(a) * b
    f = jnp.dot(hmid.astype(BF16), wd_ref[...], preferred_element_type=F32)
    f_ref[...] = f
    y, _ = _rms(f, post_ref[...])
    return x + 0.5 * y


def _ffn_fwd_body(x_ref, pre_ref, post_ref, wgt_ref, wut_ref, wd_ref,
                  h_ref, n_ref, a_ref, b_ref, f_ref):
    h_ref[...] = _ffn_fwd_tile(x_ref, pre_ref, post_ref, wgt_ref, wut_ref, wd_ref, n_ref, a_ref, b_ref, f_ref)


def _ffn_loss_body(x_ref, t_ref, pre_ref, post_ref, gfin_ref, wgt_ref, wut_ref, wd_ref,
                   n_ref, a_ref, b_ref, f_ref, dh_ref, loss_ref, dg_ref):
    h = _ffn_fwd_tile(x_ref, pre_ref, post_ref, wgt_ref, wut_ref, wd_ref, n_ref, a_ref, b_ref, f_ref)
    d = h.shape[1]
    y, _ = _rms(h, gfin_ref[...])
    err = y - t_ref[...]
    part = (0.5 / d) * jnp.sum(jnp.sum(err * err, axis=1, keepdims=True), axis=0, keepdims=True)
    dh, dg = _rms_bwd(h, gfin_ref[...], err * (1.0 / d))
    dh_ref[...] = dh
    _acc(loss_ref, part)
    _acc(dg_ref, dg)


def _ffn_fwd(x, pre_g, post_g, wgt, wut, wd, name, gather=None):
    s, d = x.shape
    f = wgt.shape[0]
    return _row_call(_ffn_fwd_body, name, s, [x], [pre_g, post_g, wgt, wut, wd],
                     [(d, F32), (d, BF16), (f, BF16), (f, BF16), (d, F32)], [], gather=gather, tm=FFN_TOKEN_TILE)


def _ffn_loss_fwd(x, target, pre_g, post_g, final_g, wgt, wut, wd, name):
    s, d = x.shape
    f = wgt.shape[0]
    return _row_call(_ffn_loss_body, name, s, [x, target], [pre_g, post_g, final_g, wgt, wut, wd],
                     [(d, BF16), (f, BF16), (f, BF16), (d, F32), (d, F32)],
                     [((1, 1), F32), ((1, d), F32)], tm=FFN_TOKEN_TILE)


def _ffn_bwd_body(dh_ref, x_ref, a_ref, b_ref, f_ref, pre_ref, post_ref, wgt_ref, wut_ref, wd_ref,
                  dx_ref, da_ref, db_ref, hm_ref, df_ref, dpre_ref, dpost_ref):
    dh = dh_ref[...]
    df, dpost = _rms_bwd(f_ref[...], post_ref[...], 0.5 * dh)
    dfb = df.astype(BF16)
    df_ref[...] = dfb
    dhmid = _dot_nt(dfb, wd_ref[...])
    a = a_ref[...].astype(F32)
    b = b_ref[...].astype(F32)
    sig = _sigmoid(a)
    sa = a * sig
    hm_ref[...] = (sa * b).astype(BF16)
    dab = (dhmid * b * sig * (1.0 + a * (1.0 - sig))).astype(BF16)
    dbb = (dhmid * sa).astype(BF16)
    da_ref[...] = dab
    db_ref[...] = dbb
    dn = _dot(dab, wgt_ref[...]) + _dot(dbb, wut_ref[...])
    dxn, dpre = _rms_bwd(x_ref[...], pre_ref[...], dn)
    dx_ref[...] = dh + dxn
    _acc(dpre_ref, dpre)
    _acc(dpost_ref, dpost)


def _ffn_bwd(dh, x, a, b, f, pre_g, post_g, wgt, wut, wd, name, scatter=None):
    s, d = x.shape
    ff = wgt.shape[0]
    return _row_call(_ffn_bwd_body, name, s, [dh, x, a, b, f], [pre_g, post_g, wgt, wut, wd],
                     [(d, F32), (ff, BF16), (ff, BF16), (ff, BF16), (d, BF16)],
                     [((1, d), F32), ((1, d), F32)], scatter=scatter)


def _inproj_fwd_body(h_ref, g_ref, wt_ref, n_ref, uv_ref, qkv_ref):
    n, _ = _rms(h_ref[...], g_ref[...])
    nb = n.astype(BF16)
    n_ref[...] = nb
    proj = _dot_nt(nb, wt_ref[...])
    nuv = uv_ref.shape[1]
    uv_ref[...] = proj[:, :nuv]
    qkv_ref[...] = proj[:, nuv:].astype(BF16)


def _inproj_fwd(h, g, w_in_t):
    s, d = h.shape
    sgu_w = SGU_GROUPS * GROUP_DIM
    sb_w = SB_HEADS * SB_HEAD_DIM
    return _row_call(_inproj_fwd_body, "inproj_fwd", s, [h], [g, w_in_t],
                     [(d, BF16), (2 * sgu_w, F32), (3 * sb_w, BF16)], [])


def _inproj_bwd_body(dh_ref, dproj_ref, h_ref, g_ref, wt_ref, dhout_ref, dg_ref):
    dn = _dot(dproj_ref[...], wt_ref[...])
    dhn, dg = _rms_bwd(h_ref[...], g_ref[...], dn)
    dhout_ref[...] = dh_ref[...] + dhn
    _acc(dg_ref, dg)


def _inproj_bwd(dh, dproj, h, g, w_in_t):
    s, d = h.shape
    return _row_call(_inproj_bwd_body, "inproj_bwd", s, [dh, dproj, h], [g, w_in_t],
                     [(d, F32)], [((1, d), F32)])


def _causal_w(ws_ref, g):
    row = lax.broadcasted_iota(jnp.int32, (CHUNK, CHUNK), 0)
    col = lax.broadcasted_iota(jnp.int32, (CHUNK, CHUNK), 1)
    return jnp.where(row >= col, ws_ref[g], 0.0), row >= col


def _group_norm(v):
    mu = jnp.mean(v, axis=-1, keepdims=True)
    d = v - mu
    rstd = lax.rsqrt(jnp.mean(d * d, axis=-1, keepdims=True) + EPS)
    return d * rstd, rstd


def _sgu_fwd_body(uv_ref, ng_ref, nb_ref, ws_ref, bs_ref, out_ref):
    width = SGU_GROUPS * GROUP_DIM
    for c in range(uv_ref.shape[0] // CHUNK):
        rows = pl.ds(c * CHUNK, CHUNK)
        for g in range(SGU_GROUPS):
            lanes = pl.ds(g * GROUP_DIM, GROUP_DIM)
            u = _gelu(uv_ref[rows, lanes])
            v = _gelu(uv_ref[rows, pl.ds(width + g * GROUP_DIM, GROUP_DIM)])
            vhat, _ = _group_norm(v)
            vn = vhat * ng_ref[:, lanes] + nb_ref[:, lanes]
            w, _ = _causal_w(ws_ref, g)
            mixed = _dot(w, vn) + bs_ref[g]
            out_ref[rows, lanes] = u * mixed


def _sgu_fwd(uv_pre, ng, nb, ws, bs):
    s = uv_pre.shape[0]
    return _row_call(_sgu_fwd_body, "sgu_fwd", s, [uv_pre], [ng, nb, ws, bs],
                     [(SGU_GROUPS * GROUP_DIM, F32)], [])[0]


def _sgu_bwd_body(uv_ref, do_ref, ng_ref, nb_ref, ws_ref, bs_ref,
                  duv_ref, dws_ref, dbs_ref, dng_ref, dnb_ref):
    width = SGU_GROUPS * GROUP_DIM

    @pl.when(pl.program_id(0) == 0)
    def _():
        dws_ref[...] = jnp.zeros_like(dws_ref)
        dbs_ref[...] = jnp.zeros_like(dbs_ref)
        dng_ref[...] = jnp.zeros_like(dng_ref)
        dnb_ref[...] = jnp.zeros_like(dnb_ref)

    for c in range(uv_ref.shape[0] // CHUNK):
        rows = pl.ds(c * CHUNK, CHUNK)
        for g in range(SGU_GROUPS):
            lanes = pl.ds(g * GROUP_DIM, GROUP_DIM)
            vlanes = pl.ds(width + g * GROUP_DIM, GROUP_DIM)
            u_pre = uv_ref[rows, lanes]
            v_pre = uv_ref[rows, vlanes]
            u = _gelu(u_pre)
            v = _gelu(v_pre)
            vhat, rstd = _group_norm(v)
            gain = ng_ref[:, lanes]
            vn = vhat * gain + nb_ref[:, lanes]
            w, causal = _causal_w(ws_ref, g)
            mixed = _dot(w, vn) + bs_ref[g]
            dout = do_ref[rows, lanes]
            du = dout * mixed
            dmixed = dout * u
            dbs_ref[g] += jnp.sum(dmixed, axis=1, keepdims=True)
            dws_ref[g] += jnp.where(causal, _dot_nt(dmixed, vn), 0.0)
            dvn = _dot_tn(w, dmixed)
            dng_ref[:, lanes] += jnp.sum(dvn * vhat, axis=0, keepdims=True)
            dnb_ref[:, lanes] += jnp.sum(dvn, axis=0, keepdims=True)
            dvh = dvn * gain
            dv = rstd * (dvh - jnp.mean(dvh, axis=-1, keepdims=True)
                         - vhat * jnp.mean(dvh * vhat, axis=-1, keepdims=True))
            duv_ref[rows, lanes] = (du * _gelu_grad(u_pre)).astype(BF16)
            duv_ref[rows, vlanes] = (dv * _gelu_grad(v_pre)).astype(BF16)


def _sgu_bwd(uv_pre, dout_a, ng, nb, ws, bs):
    s = uv_pre.shape[0]
    width = SGU_GROUPS * GROUP_DIM
    return _row_call(_sgu_bwd_body, "sgu_bwd", s, [uv_pre, dout_a], [ng, nb, ws, bs],
                     [(2 * width, BF16)],
                     [(ws.shape, F32), (bs.shape, F32), ((1, width), F32), ((1, width), F32)])


def _mesh_place():
    return lax.axis_index("x"), lax.axis_index("y"), lax.axis_index("c")


def _other_chips(mx, my):
    return [(1 - mx, my), (mx, 1 - my), (1 - mx, 1 - my)]


_ANY = pl.BlockSpec(memory_space=pl.ANY)
AG_SEMS = 7
_AG_SCRATCH = [pltpu.SemaphoreType.DMA((AG_SEMS,)), pltpu.SemaphoreType.DMA((AG_SEMS,)),
               pltpu.SemaphoreType.DMA(())]


def _gather_phases(x_ref, out_ref, send_sems, recv_sems, local_sem):
    mx, my, mc = _mesh_place()
    me, sibling = (mx, my, mc), (mx, my, 1 - mc)
    chips = _other_chips(mx, my)

    def slot(px, py, pc):
        return out_ref.at[4 * px + 2 * py + pc]

    def copy(k, block, to, src=None):
        return pltpu.make_async_remote_copy(
            src_ref=slot(*block) if src is None else src, dst_ref=slot(*block),
            send_sem=send_sems.at[k], recv_sem=recv_sems.at[k],
            device_id=to, device_id_type=MESH)

    mine = pltpu.make_async_copy(x_ref, slot(*me), local_sem)
    first = [copy(0, me, sibling, src=x_ref)]
    first += [copy(1 + j, me, (*chip, mc), src=x_ref) for j, chip in enumerate(chips)]
    passed = [copy(4 + j, (*chip, mc), sibling) for j, chip in enumerate(chips)]

    def start():
        mine.start()
        for cp in first:
            cp.start()

    def forward():
        for j, chip in enumerate(chips):
            copy(1 + j, (*chip, mc), me).wait_recv()
            passed[j].start()

    def finish():
        copy(0, sibling, me).wait_recv()
        for j, chip in enumerate(chips):
            copy(4 + j, (*chip, 1 - mc), me).wait_recv()
        for cp in first + passed:
            cp.wait_send()
        mine.wait()

    return start, forward, finish


def _all_gather(x, name):
    r, c = x.shape

    def body(x_ref, out_ref, send_sems, recv_sems, local_sem):
        start, forward, finish = _gather_phases(x_ref, out_ref, send_sems, recv_sems, local_sem)
        start()
        forward()
        finish()

    return pl.pallas_call(
        body,
        name=name,
        out_shape=jax.ShapeDtypeStruct((N_DEV, r, c), x.dtype),
        in_specs=[_ANY],
        out_specs=_ANY,
        scratch_shapes=list(_AG_SCRATCH),
    )(x)


def _scatter_phases(p_ref, out_ref, send_sems, recv_sems, local_sem):
    mx, my, mc = _mesh_place()
    me = 4 * mx + 2 * my + mc
    copies = []
    for k in range(1, N_DEV):
        tx, ty, tc = mx ^ ((k >> 2) & 1), my ^ ((k >> 1) & 1), mc ^ (k & 1)
        copies.append(pltpu.make_async_remote_copy(
            src_ref=p_ref.at[4 * tx + 2 * ty + tc], dst_ref=out_ref.at[me],
            send_sem=send_sems.at[k - 1], recv_sem=recv_sems.at[k - 1],
            device_id=(tx, ty, tc), device_id_type=MESH))
    mine = pltpu.make_async_copy(p_ref.at[me], out_ref.at[me], local_sem)

    def start():
        mine.start()
        for cp in copies:
            cp.start()

    def finish():
        for cp in copies:
            cp.wait()
        mine.wait()

    return start, finish


def _chip_exchange_phases(q_ref, out_ref, send_sems, recv_sems, local_sem):
    mx, my, mc = _mesh_place()
    copies = [pltpu.make_async_remote_copy(
        src_ref=q_ref.at[2 * cx + cy], dst_ref=out_ref.at[k],
        send_sem=send_sems.at[k], recv_sem=recv_sems.at[k],
        device_id=(cx, cy, mc), device_id_type=MESH)
        for k, (cx, cy) in enumerate(_other_chips(mx, my))]

    def start():
        for cp in copies:
            cp.start()

    def finish():
        for cp in copies:
            cp.wait()

    return start, None, finish


SB_DEAD = -105.0
HEADS_PER_TILE = 2
TILES_PER_STEP = 2
HEADS_PER_STEP = HEADS_PER_TILE * TILES_PER_STEP
STEP_LANES = TILES_PER_STEP * HEADS_PER_TILE * SB_HEAD_DIM
TILE_LANES = HEADS_PER_TILE * SB_HEAD_DIM
STACK_ROWS = HEADS_PER_STEP * Q_BLOCK
TILE_ROWS = HEADS_PER_TILE * Q_BLOCK
SB_FORWARD_LEAD = 12


def _stack_heads(x):
    lane = lax.broadcasted_iota(jnp.int32, x.shape, 1)
    zero = jnp.zeros_like(x)
    return jnp.concatenate(
        [jnp.where(lane // SB_HEAD_DIM == h, x, zero) for h in range(HEADS_PER_STEP)], axis=0)


def _unstack_tile(x):
    first = lax.broadcasted_iota(jnp.int32, (Q_BLOCK, TILE_LANES), 1) < SB_HEAD_DIM
    return jnp.where(first, x[:Q_BLOCK], x[Q_BLOCK:])


def _sb_logs(qs, k, diagonal):
    z = _dot_nt(qs, k) * (SB_HEAD_DIM ** -0.5)
    sp = jnp.log1p(jnp.exp(-jnp.abs(z)))
    log_beta = jnp.minimum(z, 0.0) - sp
    log_1m_raw = -jnp.maximum(z, 0.0) - sp
    if not diagonal:
        return None, log_beta, log_1m_raw, log_1m_raw
    row = lax.broadcasted_iota(jnp.int32, z.shape, 0)
    col = lax.broadcasted_iota(jnp.int32, z.shape, 1)
    strict = col < jnp.bitwise_and(row, Q_BLOCK - 1)
    return strict, log_beta, log_1m_raw, jnp.where(strict, log_1m_raw, 0.0)


def _masked(strict, x):
    return x if strict is None else jnp.where(strict, x, 0.0)


def _key_sums(x, pick):
    hi, lo = _split_bf16(x)
    both = jnp.dot(jnp.concatenate([hi, lo], axis=0), pick, preferred_element_type=F32)
    return both[:x.shape[0]] + both[x.shape[0]:]


def _key_order():
    row = lax.broadcasted_iota(jnp.int32, (Q_BLOCK, Q_BLOCK), 0)
    col = lax.broadcasted_iota(jnp.int32, (Q_BLOCK, Q_BLOCK), 1)
    return row, col


def _sb_fwd_body(q_ref, k_ref, v_ref, shard_ref, o_ref, tot_ref, cnt_ref, gathered_ref,
                 acc_ref, send_sems, recv_sems, local_sem):
    grp, qb = pl.program_id(0), pl.program_id(1)
    last_grp, last_qb = pl.num_programs(0) - 1, pl.num_programs(1) - 1
    ag_start, ag_forward, ag_finish = _gather_phases(shard_ref, gathered_ref, send_sems, recv_sems, local_sem)
    pl.when(jnp.logical_and(grp == 0, qb == 0))(ag_start)

    qs = _stack_heads(q_ref[...])
    row, col = _key_order()
    later = (row > col).astype(BF16)

    def block(i, c, diagonal):
        rows = pl.ds(pl.multiple_of((qb - i) * Q_BLOCK, Q_BLOCK), Q_BLOCK)
        strict, log_beta, _, log_1m = _sb_logs(qs, k_ref[rows, :], diagonal)
        a = _masked(strict, jnp.exp(log_beta + _key_sums(log_1m, later) + c)).astype(BF16)
        for t in range(TILES_PER_STEP):
            part = jnp.dot(a[t * TILE_ROWS:(t + 1) * TILE_ROWS], v_ref[rows, t * TILE_LANES:(t + 1) * TILE_LANES],
                           preferred_element_type=F32)
            if diagonal:
                acc_ref[t] = part
            else:
                acc_ref[t] += part
        return c + jnp.sum(log_1m, axis=1, keepdims=True)

    c = block(0, jnp.zeros((STACK_ROWS, 1), F32), True)

    def alive(carry):
        i, c = carry
        return jnp.logical_and(i <= qb, jnp.max(c) > SB_DEAD)

    def step(carry):
        i, c = carry
        return i + 1, block(i, c, False)

    n, c = lax.while_loop(alive, step, (jnp.int32(1), c))
    for t in range(TILES_PER_STEP):
        o_ref[:, t * TILE_LANES:(t + 1) * TILE_LANES] = _unstack_tile(acc_ref[t])
    for h in range(HEADS_PER_STEP):
        tot_ref[h] = c[h * Q_BLOCK:(h + 1) * Q_BLOCK]
    cnt_ref[grp, qb] = n.astype(F32)
    pl.when(jnp.logical_and(grp == last_grp, qb == jnp.maximum(last_qb - SB_FORWARD_LEAD, 0)))(ag_forward)
    pl.when(jnp.logical_and(grp == last_grp, qb == last_qb))(ag_finish)


def _sb_fwd(qkv, shard):
    s = qkv.shape[0]
    groups = SB_HEADS // HEADS_PER_STEP
    nq = s // Q_BLOCK
    return pl.pallas_call(
        functools.partial(_sb_fwd_body),
        name="sb_fwd",
        grid=(groups, nq),
        in_specs=[pl.BlockSpec((Q_BLOCK, STEP_LANES), lambda g, i: (i, g)),
                  pl.BlockSpec((s, STEP_LANES), lambda g, i: (0, groups + g)),
                  pl.BlockSpec((s, STEP_LANES), lambda g, i: (0, 2 * groups + g)),
                  _ANY],
        out_specs=[pl.BlockSpec((Q_BLOCK, STEP_LANES), lambda g, i: (i, g)),
                   pl.BlockSpec((HEADS_PER_STEP, Q_BLOCK, 1), lambda g, i: (g, i, 0)),
                   pl.BlockSpec(memory_space=pltpu.SMEM),
                   _ANY],
        out_shape=[jax.ShapeDtypeStruct((s, SB_HEADS * SB_HEAD_DIM), F32),
                   jax.ShapeDtypeStruct((SB_HEADS, s, 1), F32),
                   jax.ShapeDtypeStruct((groups, nq), F32),
                   jax.ShapeDtypeStruct((N_DEV,) + shard.shape, shard.dtype)],
        scratch_shapes=[pltpu.VMEM((TILES_PER_STEP, TILE_ROWS, TILE_LANES), F32)] + list(_AG_SCRATCH),
        compiler_params=pltpu.CompilerParams(
            dimension_semantics=("arbitrary", "arbitrary"), vmem_limit_bytes=VMEM_LIMIT_V7X),
    )(qkv, qkv, qkv, shard)


def _sb_bwd_body(cnt_ref, q_ref, k_ref, v_ref, tot_ref, do_ref, part_ref, dq_ref, dk_ref, dv_ref, recv_ref,
                 acc_ref, send_sems, recv_sems, local_sem):
    grp, qb = pl.program_id(0), pl.program_id(1)
    last_grp, last_qb = pl.num_programs(0) - 1, pl.num_programs(1) - 1
    rs_start, rs_finish = _scatter_phases(part_ref, recv_ref, send_sems, recv_sems, local_sem)
    pl.when(jnp.logical_and(grp == 0, qb == 0))(rs_start)

    @pl.when(qb == 0)
    def _():
        dk_ref[...] = jnp.zeros_like(dk_ref)
        dv_ref[...] = jnp.zeros_like(dv_ref)

    acc_ref[...] = jnp.zeros_like(acc_ref)
    qs = _stack_heads(q_ref[...])
    dos = _stack_heads(do_ref[...].astype(BF16))
    tot = jnp.concatenate([tot_ref[h] for h in range(HEADS_PER_STEP)], axis=0)
    row, col = _key_order()
    up_to = (row <= col).astype(BF16)
    earlier = (row < col).astype(BF16)
    scale = SB_HEAD_DIM ** -0.5
    n = jnp.clip(cnt_ref[grp, qb].astype(jnp.int32), 1, qb + 1)

    def block(kb, c, ce, diagonal):
        rows = pl.ds(pl.multiple_of(kb * Q_BLOCK, Q_BLOCK), Q_BLOCK)
        k = k_ref[rows, :]
        strict, log_beta, log_1m_raw, log_1m = _sb_logs(qs, k, diagonal)
        suffix = tot - c - _key_sums(log_1m, up_to)
        a = _masked(strict, jnp.exp(log_beta + suffix))
        de = _dot_nt(dos, v_ref[rows, :]) * a
        before = ce + _key_sums(de, earlier)
        dz = _masked(strict, de * jnp.exp(log_1m_raw) - before * jnp.exp(log_beta)).astype(BF16)
        for t in range(TILES_PER_STEP):
            acc_ref[t] += jnp.dot(dz[t * TILE_ROWS:(t + 1) * TILE_ROWS], k[:, t * TILE_LANES:(t + 1) * TILE_LANES],
                                  preferred_element_type=F32)
        dk_ref[rows, :] += _dot_tn(dz, qs) * scale
        dv_ref[rows, :] += _dot_tn(a, dos)
        return c + jnp.sum(log_1m, axis=1, keepdims=True), ce + jnp.sum(de, axis=1, keepdims=True)

    def step(i, carry):
        return block(qb - n + 1 + i, *carry, False)

    zc = jnp.zeros((STACK_ROWS, 1), F32)
    c, ce = lax.fori_loop(0, n - 1, step, (zc, zc))
    block(qb, c, ce, True)
    for t in range(TILES_PER_STEP):
        dq_ref[:, t * TILE_LANES:(t + 1) * TILE_LANES] = _unstack_tile(acc_ref[t]) * scale
    pl.when(jnp.logical_and(grp == last_grp, qb == last_qb))(rs_finish)


def _sb_bwd(cnt, qkv, tot, dout_b, parts):
    s = qkv.shape[0]
    groups = SB_HEADS // HEADS_PER_STEP
    return pl.pallas_call(
        functools.partial(_sb_bwd_body),
        name="sb_bwd",
        grid=(groups, s // Q_BLOCK),
        in_specs=[pl.BlockSpec(memory_space=pltpu.SMEM),
                  pl.BlockSpec((Q_BLOCK, STEP_LANES), lambda g, i: (i, g)),
                  pl.BlockSpec((s, STEP_LANES), lambda g, i: (0, groups + g)),
                  pl.BlockSpec((s, STEP_LANES), lambda g, i: (0, 2 * groups + g)),
                  pl.BlockSpec((HEADS_PER_STEP, Q_BLOCK, 1), lambda g, i: (g, i, 0)),
                  pl.BlockSpec((Q_BLOCK, STEP_LANES), lambda g, i: (i, g)),
                  _ANY],
        out_specs=[pl.BlockSpec((Q_BLOCK, STEP_LANES), lambda g, i: (i, g)),
                   pl.BlockSpec((s, STEP_LANES), lambda g, i: (0, g)),
                   pl.BlockSpec((s, STEP_LANES), lambda g, i: (0, g)),
                   _ANY],
        out_shape=[jax.ShapeDtypeStruct((s, SB_HEADS * SB_HEAD_DIM), F32)] * 3
        + [jax.ShapeDtypeStruct(parts.shape, parts.dtype)],
        scratch_shapes=[pltpu.VMEM((TILES_PER_STEP, TILE_ROWS, TILE_LANES), F32)] + list(_AG_SCRATCH),
        compiler_params=pltpu.CompilerParams(
            dimension_semantics=("arbitrary", "arbitrary"), vmem_limit_bytes=VMEM_LIMIT_V7X),
    )(cnt, qkv, qkv, qkv, tot, dout_b, parts)


def _outproj_fwd_body(oa_ref, ob_ref, h_ref, ga_ref, gb_ref, gpost_ref, w_ref,
                      merged_ref, mo_ref, hout_ref):
    half = oa_ref.shape[1]
    ma, _ = _rms(oa_ref[...], ga_ref[...])
    mb, _ = _rms(ob_ref[...], gb_ref[...])
    mab = ma.astype(BF16)
    mbb = mb.astype(BF16)
    merged_ref[:, :half] = mab
    merged_ref[:, half:] = mbb
    mo = (jnp.dot(mab, w_ref[:half, :], preferred_element_type=F32)
          + jnp.dot(mbb, w_ref[half:, :], preferred_element_type=F32))
    mo_ref[...] = mo
    y, _ = _rms(mo, gpost_ref[...])
    hout_ref[...] = h_ref[...] + y


def _outproj_fwd(out_a, out_b, h, ga, gb, gpost, w_out):
    s, d = h.shape
    return _row_call(_outproj_fwd_body, "outproj_fwd", s, [out_a, out_b, h], [ga, gb, gpost, w_out],
                     [(d, BF16), (d, F32), (d, F32)], [])


def _outproj_bwd_body(dh_ref, mo_ref, oa_ref, ob_ref, ga_ref, gb_ref, gpost_ref, w_ref,
                      dmo_ref, doa_ref, dob_ref, dga_ref, dgb_ref, dgpost_ref):
    half = oa_ref.shape[1]
    dmo, dgpost = _rms_bwd(mo_ref[...], gpost_ref[...], dh_ref[...])
    dmob = dmo.astype(BF16)
    dmo_ref[...] = dmob
    dma = _dot_nt(dmob, w_ref[:half, :])
    dmb = _dot_nt(dmob, w_ref[half:, :])
    doa, dga = _rms_bwd(oa_ref[...], ga_ref[...], dma)
    dob, dgb = _rms_bwd(ob_ref[...], gb_ref[...], dmb)
    doa_ref[...] = doa
    dob_ref[...] = dob
    _acc(dga_ref, dga)
    _acc(dgb_ref, dgb)
    _acc(dgpost_ref, dgpost)


def _outproj_bwd(dh, mo, out_a, out_b, ga, gb, gpost, w_out):
    s, d = dh.shape
    half = out_a.shape[1]
    return _row_call(_outproj_bwd_body, "outproj_bwd", s, [dh, mo, out_a, out_b], [ga, gb, gpost, w_out],
                     [(d, BF16), (half, F32), (half, F32)],
                     [((1, half), F32), ((1, half), F32), ((1, d), F32)])


def _kv_fwd_body(mem_ref, g_ref, wt_ref, memn_ref, kv_ref):
    n, _ = _rms(mem_ref[...], g_ref[...])
    nb = n.astype(BF16)
    memn_ref[...] = nb
    kv_ref[...] = _dot_nt(nb, wt_ref[...]).astype(BF16)


def _kv_fwd(mem, g, w_kv_t):
    m, d = mem.shape
    return _row_call(_kv_fwd_body, "kv_fwd", m, [mem], [g, w_kv_t], [(d, BF16), (w_kv_t.shape[0], BF16)], [])


def _kv_bwd_body(dkv_ref, mem_ref, memn_ref, g_ref, wt_ref, dwt_ref, dg_ref):
    dkvb = dkv_ref[...].astype(BF16)
    dwt_ref[...] = _dot_tn(dkvb, memn_ref[...]).astype(BF16)
    dmemn = _dot(dkvb, wt_ref[...])
    _, dg = _rms_bwd(mem_ref[...], g_ref[...], dmemn)
    dg_ref[...] = dg


def _kv_bwd(dkv, mem, memn, g, w_kv_t):
    m, d = mem.shape
    return pl.pallas_call(
        functools.partial(_kv_bwd_body),
        name="kv_bwd",
        out_shape=[jax.ShapeDtypeStruct(w_kv_t.shape, BF16), jax.ShapeDtypeStruct((1, d), F32)],
        compiler_params=pltpu.CompilerParams(vmem_limit_bytes=VMEM_LIMIT_V7X),
    )(dkv, mem, memn, g, w_kv_t)


def _xa_fwd_body(h_ref, gpre_ref, gpost_ref, wq_ref, wo_ref, kv_ref,
                 n_ref, q_ref, o_ref, c_ref, hout_ref):
    h = h_ref[...]
    d = h.shape[1]
    n, _ = _rms(h, gpre_ref[...])
    nb = n.astype(BF16)
    n_ref[...] = nb
    qb = jnp.dot(nb, wq_ref[...], preferred_element_type=F32).astype(BF16)
    q_ref[...] = qb
    for hd in range(XA_HEADS):
        lanes = slice(hd * XA_HEAD_DIM, (hd + 1) * XA_HEAD_DIM)
        k = kv_ref[:, lanes]
        v = kv_ref[:, d + hd * XA_HEAD_DIM:d + (hd + 1) * XA_HEAD_DIM]
        logits = _dot_nt(qb[:, lanes], k) * (XA_HEAD_DIM ** -0.5)
        e = jnp.exp(logits - jnp.max(logits, axis=-1, keepdims=True))
        p = e / jnp.sum(e, axis=-1, keepdims=True)
        o_ref[:, lanes] = jnp.dot(p.astype(BF16), v, preferred_element_type=F32).astype(BF16)
    c = jnp.dot(o_ref[...], wo_ref[...], preferred_element_type=F32)
    c_ref[...] = c
    y, _ = _rms(c, gpost_ref[...])
    hout_ref[...] = h + y


def _xa_fwd(h, gpre, gpost, wq, wo, kv):
    s, d = h.shape
    return _row_call(_xa_fwd_body, "xa_fwd", s, [h], [gpre, gpost, wq, wo, kv],
                     [(d, BF16), (d, BF16), (d, BF16), (d, F32), (d, F32)], [])


def _xa_bwd_body(dh_ref, h_ref, c_ref, q_ref, o_ref, gpre_ref, gpost_ref, wq_ref, wo_ref, kv_ref,
                 dhout_ref, dc_ref, dq_ref, dkv_ref, dgpre_ref, dgpost_ref):
    dh = dh_ref[...]
    d = dh.shape[1]
    scale = XA_HEAD_DIM ** -0.5
    dc, dgpost = _rms_bwd(c_ref[...], gpost_ref[...], dh)
    dcb = dc.astype(BF16)
    dc_ref[...] = dcb
    dob = _dot_nt(dcb, wo_ref[...]).astype(BF16)

    @pl.when(pl.program_id(0) == 0)
    def _():
        dkv_ref[...] = jnp.zeros_like(dkv_ref)

    for hd in range(XA_HEADS):
        lanes = slice(hd * XA_HEAD_DIM, (hd + 1) * XA_HEAD_DIM)
        vlanes = slice(d + hd * XA_HEAD_DIM, d + (hd + 1) * XA_HEAD_DIM)
        qh = q_ref[:, lanes]
        k = kv_ref[:, lanes]
        v = kv_ref[:, vlanes]
        logits = _dot_nt(qh, k) * scale
        e = jnp.exp(logits - jnp.max(logits, axis=-1, keepdims=True))
        p = e / jnp.sum(e, axis=-1, keepdims=True)
        doh = dob[:, lanes]
        dp = _dot_nt(doh, v)
        dl = (p * (dp - jnp.sum(dp * p, axis=-1, keepdims=True)) * scale).astype(BF16)
        dq_ref[:, lanes] = jnp.dot(dl, k, preferred_element_type=F32).astype(BF16)
        dkv_ref[:, lanes] += _dot_tn(dl, qh)
        dkv_ref[:, vlanes] += _dot_tn(p, doh)
    dn = _dot_nt(dq_ref[...], wq_ref[...])
    dhn, dgpre = _rms_bwd(h_ref[...], gpre_ref[...], dn)
    dhout_ref[...] = dh + dhn
    _acc(dgpre_ref, dgpre)
    _acc(dgpost_ref, dgpost)


def _xa_bwd(dh, h, c, q, o, gpre, gpost, wq, wo, kv, scatter=None):
    s, d = h.shape
    return _row_call(_xa_bwd_body, "xa_bwd", s, [dh, h, c, q, o], [gpre, gpost, wq, wo, kv],
                     [(d, F32), (d, BF16), (d, BF16)],
                     [(kv.shape, F32), ((1, d), F32), ((1, d), F32)], scatter=scatter)


def _largest_tile(n, cap):
    best = 128
    for t in range(128, cap + 1, 128):
        if n % t == 0:
            best = t
    return best


def _mm_tn(a, bs, name, gather=None, chip_exchange=None):
    s, k = a.shape
    n = bs[0].shape[1]
    nb = len(bs)
    ts = min(2048, s)
    tk = _largest_tile(k, 1536)
    tn = _largest_tile(n, 1536 // nb)

    steps = s // ts
    grid = (k // tk, n // tn, steps)
    sent = gather if gather is not None else chip_exchange
    hosted = sent is not None
    phases = _gather_phases if gather is not None else _chip_exchange_phases
    landed_shape = None
    if hosted:
        landed_shape = (N_DEV,) + sent.shape if gather is not None else (3,) + sent.shape[1:]

    def body(a_ref, *refs):
        if hosted:
            sent_ref, landed_ref, refs = refs[nb], refs[2 * nb + 1], refs[:nb] + refs[nb + 1:2 * nb + 1] + refs[2 * nb + 2:]
            start, forward, finish = phases(sent_ref, landed_ref, *refs[-3:])
            place = (pl.program_id(0) * grid[1] + pl.program_id(1)) * grid[2] + pl.program_id(2)
            pl.when(place == 0)(start)
        b_refs, o_refs, acc_refs = refs[:nb], refs[nb:2 * nb], refs[2 * nb:3 * nb]
        at = a_ref[...]
        t = pl.program_id(2)

        @pl.when(t == 0)
        def _():
            for acc_ref in acc_refs:
                acc_ref[...] = jnp.zeros_like(acc_ref)

        for b_ref, acc_ref in zip(b_refs, acc_refs):
            acc_ref[...] += _dot_tn(at, b_ref[...])

        @pl.when(t == steps - 1)
        def _():
            for o_ref, acc_ref in zip(o_refs, acc_refs):
                o_ref[...] = acc_ref[...].astype(BF16)

        if hosted:
            total = grid[0] * grid[1] * grid[2]

            @pl.when(place == total - 1)
            def _():
                if forward is not None:
                    forward()
                finish()

    return pl.pallas_call(
        body,
        name=name,
        grid=grid,
        in_specs=[pl.BlockSpec((ts, tk), lambda i, j, t: (t, i))]
        + [pl.BlockSpec((ts, tn), lambda i, j, t: (t, j))] * nb + ([_ANY] if hosted else []),
        out_specs=[pl.BlockSpec((tk, tn), lambda i, j, t: (i, j))] * nb + ([_ANY] if hosted else []),
        out_shape=[jax.ShapeDtypeStruct((k, n), BF16)] * nb
        + ([jax.ShapeDtypeStruct(landed_shape, sent.dtype)] if hosted else []),
        scratch_shapes=[pltpu.VMEM((tk, tn), F32)] * nb + (list(_AG_SCRATCH) if hosted else []),
        compiler_params=pltpu.CompilerParams(
            dimension_semantics=("arbitrary", "arbitrary", "arbitrary"),
            vmem_limit_bytes=VMEM_LIMIT_V7X),
    )(a, *bs, *([sent] if hosted else []))


_SMALL_SHAPES = {
    "sgu_norm_g": (1, SGU_GROUPS * GROUP_DIM),
    "sgu_norm_b": (1, SGU_GROUPS * GROUP_DIM),
    "sgu_w_s": (SGU_GROUPS, CHUNK, CHUNK),
    "sgu_b_s": (SGU_GROUPS, CHUNK, 1),
}


def _small_views(small):
    return {n: v.reshape(_SMALL_SHAPES.get(n, v.shape)) for n, v in small.items()}


def _small_unviews(views, like):
    return {n: v.reshape(like[n].shape) for n, v in views.items()}


def _unpack_rows(gathered, names, shard_rows):
    out, off = {}, 0
    for n in names:
        rows = shard_rows[n]
        out[n] = gathered[:, off:off + rows, :].reshape(N_DEV * rows, gathered.shape[2])
        off += rows
    return out


def _row_tile(r, cap):
    best = 16
    for t in range(16, cap + 1, 16):
        if r % t == 0:
            best = t
    return best


def _sum_received(received, name):
    _, r, c = received.shape
    tr = _row_tile(r, 1024)

    def body(rc_ref, g_ref):
        g = rc_ref[0].astype(F32)
        for t in range(1, N_DEV):
            g = g + rc_ref[t].astype(F32)
        g_ref[...] = g

    return pl.pallas_call(
        body, name=name, grid=(r // tr,),
        in_specs=[pl.BlockSpec((N_DEV, tr, c), lambda i: (0, i, 0))],
        out_specs=pl.BlockSpec((tr, c), lambda i: (i, 0)),
        out_shape=jax.ShapeDtypeStruct((r, c), F32),
    )(received)


def _local_step(x, mem, target, small, big, shards, shard_rows):
    sm, w = small, dict(big)
    d_model = x.shape[1]

    def parts(names):
        return jnp.concatenate([gw.pop(n).reshape(N_DEV, -1, d_model) for n in names], axis=1)

    h1, n1, a1, b1, f1, landed = _ffn_fwd(
        x, sm["ffn1_pre_g"], sm["ffn1_post_g"], w["ffn1_w_gate"], w["ffn1_w_up"], w["ffn1_w_down"],
        "ffn1_fwd", gather=shards["ffn1_fwd"])
    w.update(_unpack_rows(landed, GATHER_IN["ffn1_fwd"], shard_rows))
    n2, uv_pre, qkv = _inproj_fwd(h1, sm["mix_pre_g"], w["w_in"])
    out_a = _sgu_fwd(uv_pre, sm["sgu_norm_g"], sm["sgu_norm_b"], sm["sgu_w_s"], sm["sgu_b_s"])
    out_b, tot, cnt, landed = _sb_fwd(qkv, shards["sb_fwd"])
    w.update(_unpack_rows(landed, GATHER_IN["sb_fwd"], shard_rows))
    merged, mo, h2 = _outproj_fwd(out_a, out_b, h1, sm["sgu_out_g"], sm["sb_out_g"],
                                  sm["mix_post_g"], w["w_out"])
    memn, kv = _kv_fwd(mem, sm["mem_norm_g"], w["xa_w_kv"])
    n3, qx, ox, cx, h3 = _xa_fwd(h2, sm["xa_pre_g"], sm["xa_post_g"], w["xa_w_q"], w["xa_w_o"], kv)
    n4, a2, b2, f2, dh4, loss, dg_final = _ffn_loss_fwd(
        h3, target, sm["ffn2_pre_g"], sm["ffn2_post_g"], sm["final_norm_g"],
        w["ffn2_w_gate"], w["ffn2_w_up"], w["ffn2_w_down"], "ffn2_fwd")

    gs, gw = {"final_norm_g": dg_final}, {}
    dh3, da2, db2, hm2, df2, gs["ffn2_pre_g"], gs["ffn2_post_g"] = _ffn_bwd(
        dh4, h3, a2, b2, f2, sm["ffn2_pre_g"], sm["ffn2_post_g"],
        w["ffn2_w_gate"], w["ffn2_w_up"], w["ffn2_w_down"], "ffn2_bwd")
    gw["ffn2_w_gate"], = _mm_tn(da2, [n4], "ffn2_dw_gate")
    gw["ffn2_w_up"], = _mm_tn(db2, [n4], "ffn2_dw_up")
    gw["ffn2_w_down"], = _mm_tn(hm2, [df2], "ffn2_dw_down")

    received = {}
    dh2, dc, dqx, dkv, gs["xa_pre_g"], gs["xa_post_g"], received["xa_bwd"] = _xa_bwd(
        dh3, h2, cx, qx, ox, sm["xa_pre_g"], sm["xa_post_g"], w["xa_w_q"], w["xa_w_o"], kv,
        scatter=parts(SCATTER_IN["xa_bwd"]))
    gw["xa_w_o"], = _mm_tn(ox, [dc], "xa_dw_o")
    gw["xa_w_q"], = _mm_tn(n3, [dqx], "xa_dw_q")
    gw["xa_w_kv"], gs["mem_norm_g"] = _kv_bwd(dkv, mem, memn, sm["mem_norm_g"], w["xa_w_kv"])

    dmo, dout_a, dout_b, gs["sgu_out_g"], gs["sb_out_g"], gs["mix_post_g"] = _outproj_bwd(
        dh2, mo, out_a, out_b, sm["sgu_out_g"], sm["sb_out_g"], sm["mix_post_g"], w["w_out"])
    gw["w_out"], = _mm_tn(merged, [dmo], "mix_dw_out")
    dq, dk, dv, received["sb_bwd"] = _sb_bwd(cnt, qkv, tot, dout_b, parts(SCATTER_IN["sb_bwd"]))
    duv, gs["sgu_w_s"], gs["sgu_b_s"], gs["sgu_norm_g"], gs["sgu_norm_b"] = _sgu_bwd(
        uv_pre, dout_a, sm["sgu_norm_g"], sm["sgu_norm_b"], sm["sgu_w_s"], sm["sgu_b_s"])
    dproj = jnp.concatenate([duv] + [t.astype(BF16) for t in (dq, dk, dv)], axis=1)
    dh1, gs["mix_pre_g"] = _inproj_bwd(dh2, dproj, h1, sm["mix_pre_g"], w["w_in"])
    gw["w_in"], = _mm_tn(dproj, [n2], "mix_dw_in")

    dx, da1, db1, hm1, df1, gs["ffn1_pre_g"], gs["ffn1_post_g"], received["ffn1_bwd"] = _ffn_bwd(
        dh1, x, a1, b1, f1, sm["ffn1_pre_g"], sm["ffn1_post_g"],
        w["ffn1_w_gate"], w["ffn1_w_up"], w["ffn1_w_down"], "ffn1_bwd",
        scatter=parts(SCATTER_IN["ffn1_bwd"]))
    summed = {host: _sum_received(r, "rs_sum_" + host) for host, r in received.items()}

    mx, my, mc = _mesh_place()
    place = jnp.stack([mc, 2 * mx + my]).astype(jnp.int32)

    def pair_stage(dw, tag):
        p4 = dw.reshape(N_DEV // 2, 2, -1, d_model)
        recv_a = _pair_exchange(p4, "rs_pair_exchange_" + tag)
        return p4, recv_a, _pair_sum(place, p4, recv_a, "rs_pair_sum_" + tag)

    dw, small_grads = _mm_tn(da1, [n1], "ffn1_dw_gate", gather=_pack_small(gs))
    gate = pair_stage(dw, "gate")
    dw, gate_b = _mm_tn(db1, [n1], "ffn1_dw_up", chip_exchange=gate[2])
    up = pair_stage(dw, "up")
    dw, up_b = _mm_tn(hm1, [df1], "ffn1_dw_down", chip_exchange=up[2])
    down = pair_stage(dw, "down")
    down_b = _chip_exchange(down[2], "rs_chip_exchange_down")
    for n, (p4, recv_a, _), recv_b in (("ffn1_w_gate", gate, gate_b), ("ffn1_w_up", up, up_b),
                                       ("ffn1_w_down", down, down_b)):
        summed[n] = _rs_final(place, p4, recv_a, recv_b, "rs_final_" + n)
    return loss, dx, small_grads, summed


def _pair_exchange(p4, name):
    nchip, _, r, c = p4.shape

    def body(p_ref, out_ref, send_sems, recv_sems):
        mx, my, mc = _mesh_place()
        copies = [pltpu.make_async_remote_copy(
            src_ref=p_ref.at[j, 1 - mc], dst_ref=out_ref.at[j],
            send_sem=send_sems.at[j], recv_sem=recv_sems.at[j],
            device_id=(mx, my, 1 - mc), device_id_type=MESH) for j in range(nchip)]
        for cp in copies:
            cp.start()
        for cp in copies:
            cp.wait()

    return pl.pallas_call(
        body,
        name=name,
        out_shape=jax.ShapeDtypeStruct((nchip, r, c), p4.dtype),
        in_specs=[_ANY],
        out_specs=_ANY,
        scratch_shapes=[pltpu.SemaphoreType.DMA((nchip,)), pltpu.SemaphoreType.DMA((nchip,))],
    )(p4)


def _chip_exchange(q, name):
    _, r, c = q.shape

    def body(q_ref, out_ref, send_sems, recv_sems, local_sem):
        start, _, finish = _chip_exchange_phases(q_ref, out_ref, send_sems, recv_sems, local_sem)
        start()
        finish()

    return pl.pallas_call(
        body,
        name=name,
        out_shape=jax.ShapeDtypeStruct((3, r, c), q.dtype),
        in_specs=[_ANY],
        out_specs=_ANY,
        scratch_shapes=list(_AG_SCRATCH),
    )(q)


def _rs_row_tile(r):
    return _row_tile(r, 1024)


def _pair_sum(place, p4, recv_a, name):
    nchip, _, r, c = p4.shape
    tr = _rs_row_tile(r)

    def body(place_ref, p_ref, a_ref, q_ref):
        q_ref[0] = (p_ref[0, 0].astype(F32) + a_ref[0].astype(F32)).astype(BF16)

    return pl.pallas_call(
        body,
        name=name,
        grid_spec=pltpu.PrefetchScalarGridSpec(
            num_scalar_prefetch=1,
            grid=(nchip, r // tr),
            in_specs=[pl.BlockSpec((1, 1, tr, c), lambda j, i, pref: (j, pref[0], i, 0)),
                      pl.BlockSpec((1, tr, c), lambda j, i, pref: (j, i, 0))],
            out_specs=pl.BlockSpec((1, tr, c), lambda j, i, pref: (j, i, 0)),
        ),
        out_shape=jax.ShapeDtypeStruct((nchip, r, c), BF16),
    )(place, p4, recv_a)


def _rs_final(place, p4, recv_a, recv_b, name):
    _, _, r, c = p4.shape
    tr = _rs_row_tile(r)

    def body(place_ref, p_ref, a_ref, b_ref, g_ref):
        g = p_ref[0, 0].astype(F32) + a_ref[0].astype(F32)
        for k in range(3):
            g = g + b_ref[k].astype(F32)
        g_ref[...] = g

    return pl.pallas_call(
        body,
        name=name,
        grid_spec=pltpu.PrefetchScalarGridSpec(
            num_scalar_prefetch=1,
            grid=(r // tr,),
            in_specs=[pl.BlockSpec((1, 1, tr, c), lambda i, pref: (pref[1], pref[0], i, 0)),
                      pl.BlockSpec((1, tr, c), lambda i, pref: (pref[1], i, 0)),
                      pl.BlockSpec((3, tr, c), lambda i, pref: (0, i, 0))],
            out_specs=pl.BlockSpec((tr, c), lambda i, pref: (i, 0)),
        ),
        out_shape=jax.ShapeDtypeStruct((r, c), F32),
    )(place, p4, recv_a, recv_b)


def _adamw_math(w, g, m, v):
    m = ADAM_B1 * m + (1.0 - ADAM_B1) * g
    v = ADAM_B2 * v + (1.0 - ADAM_B2) * (g * g)
    m_hat = m / (1.0 - ADAM_B1 ** ADAM_STEP)
    v_hat = v / (1.0 - ADAM_B2 ** ADAM_STEP)
    delta = -ADAM_LR * (m_hat / (jnp.sqrt(v_hat) + ADAM_EPS) + ADAM_WD * w)
    return delta, m, v


def _adamw(w, g, m, v, name):
    r, c = w.shape
    tr = r if r <= 512 else 256

    def body(w_ref, g_ref, m_ref, v_ref, d_ref, mo_ref, vo_ref):
        d_ref[...], mo_ref[...], vo_ref[...] = _adamw_math(w_ref[...], g_ref[...], m_ref[...], v_ref[...])

    spec = pl.BlockSpec((tr, c), lambda i: (i, 0))
    out = jax.ShapeDtypeStruct((r, c), F32)
    return pl.pallas_call(
        body, name=name, grid=(r // tr,), in_specs=[spec] * 4, out_specs=[spec] * 3,
        out_shape=[out] * 3,
    )(w, g, m, v)


def _small_sum_adamw(gathered, ws, ms, vs):
    _, r, c = gathered.shape
    count = len(ws)

    def body(ga_ref, *refs):
        w_refs, m_refs, v_refs = refs[:count], refs[count:2 * count], refs[2 * count:3 * count]
        out_refs = refs[3 * count:]
        for o_ref in out_refs:
            o_ref[...] = jnp.zeros_like(o_ref)
        off = 0
        for w_ref, m_ref, v_ref in zip(w_refs, m_refs, v_refs):
            rows = pl.ds(off, w_ref.shape[0])
            g = ga_ref[0, rows, :]
            for k in range(1, N_DEV):
                g = g + ga_ref[k, rows, :]
            results = (g, *_adamw_math(w_ref[...], g, m_ref[...], v_ref[...]))
            for o_ref, val in zip(out_refs, results):
                o_ref[rows, :] = val
            off += w_ref.shape[0] + (-w_ref.shape[0]) % SMALL_ROW_ALIGN

    out = jax.ShapeDtypeStruct((r, c), F32)
    return pl.pallas_call(body, name="small_sum_adamw", out_shape=[out] * 4)(gathered, *ws, *ms, *vs)


_WEIGHTS = ["ffn1_pre_g", "ffn1_post_g", "ffn1_w_gate", "ffn1_w_up", "ffn1_w_down", "mix_pre_g",
            "mix_post_g", "w_in", "sgu_norm_g", "sgu_norm_b", "sgu_w_s", "sgu_b_s", "sgu_out_g",
            "sb_out_g", "w_out", "xa_pre_g", "xa_post_g", "mem_norm_g", "xa_w_q", "xa_w_kv", "xa_w_o",
            "ffn2_pre_g", "ffn2_post_g", "ffn2_w_gate", "ffn2_w_up", "ffn2_w_down", "final_norm_g"]
_BIG = ["ffn1_w_gate", "ffn1_w_up", "ffn1_w_down", "w_in", "w_out", "xa_w_q", "xa_w_kv", "xa_w_o",
        "ffn2_w_gate", "ffn2_w_up", "ffn2_w_down"]
_COL_SHARDED = ("ffn1_w_gate", "ffn1_w_up", "w_in", "xa_w_kv", "ffn2_w_gate", "ffn2_w_up")
_EARLY = ["ffn1_w_gate", "ffn1_w_up", "ffn1_w_down"]
GATHER_IN = {"ffn1_fwd": ["w_in", "ffn2_w_gate"],
             "sb_fwd": ["w_out", "xa_w_q", "xa_w_kv", "xa_w_o", "ffn2_w_up", "ffn2_w_down"]}
SCATTER_IN = {"xa_bwd": ["ffn2_w_gate"],
              "sb_bwd": ["ffn2_w_up", "ffn2_w_down", "xa_w_q", "xa_w_kv", "xa_w_o", "w_out"],
              "ffn1_bwd": ["w_in"]}
_SMALL = [n for n in _WEIGHTS if n not in _BIG]
SMALL_LANES = 128
SMALL_ROW_ALIGN = 8


def _pack_small(tensors):
    parts = []
    for n in _SMALL:
        t = tensors[n].reshape(-1, SMALL_LANES)
        pad = (-t.shape[0]) % SMALL_ROW_ALIGN
        parts.append(jnp.pad(t, ((0, pad), (0, 0))) if pad else t)
    return jnp.concatenate(parts, axis=0)


def _unpack_small(packed, like):
    out, off = {}, 0
    for n in _SMALL:
        size = like[n].size
        rows = size // SMALL_LANES
        out[n] = packed[off:off + rows].reshape(like[n].shape)
        off += rows + (-rows) % SMALL_ROW_ALIGN
    return out


def kernel(x, mem, ffn1_pre_g, ffn1_post_g, ffn1_w_gate, ffn1_w_up, ffn1_w_down, mix_pre_g, mix_post_g, w_in, sgu_norm_g, sgu_norm_b, sgu_w_s, sgu_b_s, sgu_out_g, sb_out_g, w_out, xa_pre_g, xa_post_g, mem_norm_g, xa_w_q, xa_w_kv, xa_w_o, ffn2_pre_g, ffn2_post_g, ffn2_w_gate, ffn2_w_up, ffn2_w_down, final_norm_g, loss_target, m_ffn1_pre_g, m_ffn1_post_g, m_ffn1_w_gate, m_ffn1_w_up, m_ffn1_w_down, m_mix_pre_g, m_mix_post_g, m_w_in, m_sgu_norm_g, m_sgu_norm_b, m_sgu_w_s, m_sgu_b_s, m_sgu_out_g, m_sb_out_g, m_w_out, m_xa_pre_g, m_xa_post_g, m_mem_norm_g, m_xa_w_q, m_xa_w_kv, m_xa_w_o, m_ffn2_pre_g, m_ffn2_post_g, m_ffn2_w_gate, m_ffn2_w_up, m_ffn2_w_down, m_final_norm_g, v_ffn1_pre_g, v_ffn1_post_g, v_ffn1_w_gate, v_ffn1_w_up, v_ffn1_w_down, v_mix_pre_g, v_mix_post_g, v_w_in, v_sgu_norm_g, v_sgu_norm_b, v_sgu_w_s, v_sgu_b_s, v_sgu_out_g, v_sb_out_g, v_w_out, v_xa_pre_g, v_xa_post_g, v_mem_norm_g, v_xa_w_q, v_xa_w_kv, v_xa_w_o, v_ffn2_pre_g, v_ffn2_post_g, v_ffn2_w_gate, v_ffn2_w_up, v_ffn2_w_down, v_final_norm_g):
    vals = dict(locals())
    d_model = x.shape[-1]

    def packed(names):
        return jnp.concatenate(
            [(vals[n][0].T if n in _COL_SHARDED else vals[n][0]).astype(BF16) for n in names], axis=0)

    shard_rows = {n: vals[n].shape[2 if n in _COL_SHARDED else 1] for n in _BIG}
    big = _unpack_rows(_all_gather(packed(_EARLY), "ag_weights"), _EARLY, shard_rows)

    small = {n: vals[n] for n in _SMALL}
    loss_part, dx, gathered_small, summed = _local_step(
        x[0], mem[0], loss_target[0], _small_views(small), big,
        {host: packed(names) for host, names in GATHER_IN.items()}, shard_rows)
    loss = lax.psum(loss_part[0, 0], ("x", "y", "c"))

    grads, deltas, new_m, new_v = {}, {}, {}, {}
    for names, g_rows in [([n], summed[n]) for n in _EARLY] + [(SCATTER_IN[h], summed[h]) for h in SCATTER_IN]:
        off = 0
        for n in names:
            rows = shard_rows[n]
            g = g_rows[off:off + rows]
            off += rows
            state = [vals[n][0], vals["m_" + n][0], vals["v_" + n][0]]
            flipped = n in _COL_SHARDED and rows % SMALL_LANES != 0
            if flipped:
                state = [t.T for t in state]
            elif n in _COL_SHARDED:
                g = g.T
            outs = (g, *_adamw(state[0], g, state[1], state[2], "adamw_" + n))
            if flipped:
                outs = tuple(t.T for t in outs)
            grads[n], deltas[n], new_m[n], new_v[n] = (t[None] for t in outs)

    outs = _small_sum_adamw(gathered_small,
                            *([vals[pre + n].reshape(-1, SMALL_LANES) for n in _SMALL] for pre in ("", "m_", "v_")))
    for dst, packed in zip((grads, deltas, new_m, new_v), outs):
        dst.update(_unpack_small(packed, small))

    return (loss, dx[None], *[grads[n] for n in _WEIGHTS], *[deltas[n] for n in _WEIGHTS],
            *[new_m[n] for n in _WEIGHTS], *[new_v[n] for n in _WEIGHTS])
```

```python
import functools

import jax
import jax.numpy as jnp
from jax import lax
from jax.experimental import pallas as pl
from jax.experimental.pallas import tpu as pltpu

F32 = jnp.float32
BF16 = jnp.bfloat16
EPS = 1e-6
MESH = pl.DeviceIdType.MESH
N_DEV = 8

SGU_GROUPS = 4
GROUP_DIM = 128
CHUNK = 128
SB_HEADS = 8
SB_HEAD_DIM = 64
Q_BLOCK = 128
XA_HEADS = 4
XA_HEAD_DIM = 256

ADAM_LR = 0.001
ADAM_B1 = 0.9
ADAM_B2 = 0.999
ADAM_EPS = 1e-08
ADAM_WD = 0.01
ADAM_STEP = 10

VMEM_LIMIT_V7X = 56 * 1024 * 1024
GELU_C0 = 0.7978845608028654
GELU_C1 = 0.044715


def _dot(a, b):
    return jnp.dot(a.astype(BF16), b.astype(BF16), preferred_element_type=F32)


def _dot_nt(a, b):
    return lax.dot_general(a.astype(BF16), b.astype(BF16), (((1,), (1,)), ((), ())),
                           preferred_element_type=F32)


def _dot_tn(a, b):
    return lax.dot_general(a.astype(BF16), b.astype(BF16), (((0,), (0,)), ((), ())),
                           preferred_element_type=F32)


def _rms(x, g):
    r = lax.rsqrt(jnp.mean(x * x, axis=-1, keepdims=True) + EPS)
    return x * r * g, r


def _rms_bwd(x, g, dy):
    r = lax.rsqrt(jnp.mean(x * x, axis=-1, keepdims=True) + EPS)
    xh = x * r
    gy = dy * g
    dx = r * (gy - xh * jnp.mean(gy * xh, axis=-1, keepdims=True))
    dg = jnp.sum(dy * xh, axis=0, keepdims=True)
    return dx, dg


def _sigmoid(x):
    return jax.nn.sigmoid(x)


def _gelu(x):
    t = jnp.tanh(GELU_C0 * (x + GELU_C1 * x * x * x))
    return 0.5 * x * (1.0 + t)


def _gelu_grad(x):
    t = jnp.tanh(GELU_C0 * (x + GELU_C1 * x * x * x))
    return 0.5 * (1.0 + t) + 0.5 * x * (1.0 - t * t) * GELU_C0 * (1.0 + 3.0 * GELU_C1 * x * x)


def _split_bf16(x):
    hi = x.astype(BF16)
    lo = (x - hi.astype(F32)).astype(BF16)
    return hi, lo


def _row_spec(tm, cols):
    return pl.BlockSpec((tm, cols), lambda i: (i, 0))


def _full_spec(shape, buffers=None):
    nd = len(shape)
    mode = None if buffers is None else pl.Buffered(buffers)
    return pl.BlockSpec(tuple(shape), lambda i: (0,) * nd, pipeline_mode=mode)


FFN_TOKEN_TILE = 512


def _token_tile(s):
    return min(256, s)


def _row_call(body, name, s, tiled_in, full_in, tiled_out, acc_out, gather=None, scatter=None, tm=None):
    tm = _token_tile(s) if tm is None else min(tm, s)
    steps = s // tm
    in_specs = [_row_spec(tm, a.shape[1]) for a in tiled_in] + [_full_spec(a.shape, buffers=1) for a in full_in]
    out_specs = [_row_spec(tm, c) for c, _ in tiled_out] + [_full_spec(sh) for sh, _ in acc_out]
    out_shape = [jax.ShapeDtypeStruct((s, c), dt) for c, dt in tiled_out]
    out_shape += [jax.ShapeDtypeStruct(sh, dt) for sh, dt in acc_out]
    operands = [*tiled_in, *full_in]
    scratch = []
    kernel_body = functools.partial(body)
    sent = gather if gather is not None else scatter
    if sent is not None:
        n_in, n_out = len(operands), len(out_shape)
        out_shape.append(jax.ShapeDtypeStruct(
            (N_DEV,) + sent.shape if gather is not None else sent.shape, sent.dtype))
        operands.append(sent)
        in_specs.append(_ANY)
        out_specs.append(_ANY)
        scratch = list(_AG_SCRATCH)

        def kernel_body(*refs):
            ins, sent_ref = refs[:n_in], refs[n_in]
            outs, landed_ref = refs[n_in + 1:n_in + 1 + n_out], refs[n_in + 1 + n_out]
            step = pl.program_id(0)
            if gather is not None:
                start, relay, forward, finish = _gather_phases(sent_ref, landed_ref, *refs[-3:])
            else:
                start, finish = _scatter_phases(sent_ref, landed_ref, *refs[-3:])
            pl.when(step == 0)(start)
            body(*ins, *outs)
            if gather is not None:
                pl.when(step == steps // 3)(relay)
                pl.when(step == (2 * steps) // 3)(forward)
            pl.when(step == steps - 1)(finish)

    return pl.pallas_call(
        kernel_body,
        name=name,
        grid=(steps,),
        in_specs=in_specs,
        out_specs=out_specs,
        out_shape=out_shape,
        scratch_shapes=scratch,
        compiler_params=pltpu.CompilerParams(
            dimension_semantics=("arbitrary",), vmem_limit_bytes=VMEM_LIMIT_V7X),
    )(*operands)


def _acc(ref, val):
    @pl.when(pl.program_id(0) == 0)
    def _():
        ref[...] = val

    @pl.when(pl.program_id(0) != 0)
    def _():
        ref[...] += val


def _ffn_fwd_tile(x_ref, pre_ref, post_ref, wgt_ref, wut_ref, wd_ref, n_ref, a_ref, b_ref, f_ref):
    x = x_ref[...]
    n, _ = _rms(x, pre_ref[...])
    nb = n.astype(BF16)
    n_ref[...] = nb
    a = _dot_nt(nb, wgt_ref[...])
    b = _dot_nt(nb, wut_ref[...])
    a_ref[...] = a.astype(BF16)
    b_ref[...] = b.astype(BF16)
    hmid = a * _sigmoid(a) * b
    f = jnp.dot(hmid.astype(BF16), wd_ref[...], preferred_element_type=F32)
    f_ref[...] = f
    y, _ = _rms(f, post_ref[...])
    return x + 0.5 * y


def _ffn_fwd_body(x_ref, pre_ref, post_ref, wgt_ref, wut_ref, wd_ref,
                  h_ref, n_ref, a_ref, b_ref, f_ref):
    h_ref[...] = _ffn_fwd_tile(x_ref, pre_ref, post_ref, wgt_ref, wut_ref, wd_ref, n_ref, a_ref, b_ref, f_ref)


def _ffn_loss_body(x_ref, t_ref, pre_ref, post_ref, gfin_ref, wgt_ref, wut_ref, wd_ref,
                   n_ref, a_ref, b_ref, f_ref, dh_ref, loss_ref, dg_ref):
    h = _ffn_fwd_tile(x_ref, pre_ref, post_ref, wgt_ref, wut_ref, wd_ref, n_ref, a_ref, b_ref, f_ref)
    d = h.shape[1]
    y, _ = _rms(h, gfin_ref[...])
    err = y - t_ref[...]
    part = (0.5 / d) * jnp.sum(jnp.sum(err * err, axis=1, keepdims=True), axis=0, keepdims=True)
    dh, dg = _rms_bwd(h, gfin_ref[...], err * (1.0 / d))
    dh_ref[...] = dh
    _acc(loss_ref, part)
    _acc(dg_ref, dg)


def _ffn_fwd(x, pre_g, post_g, wgt, wut, wd, name, gather=None):
    s, d = x.shape
    f = wgt.shape[0]
    return _row_call(_ffn_fwd_body, name, s, [x], [pre_g, post_g, wgt, wut, wd],
                     [(d, F32), (d, BF16), (f, BF16), (f, BF16), (d, F32)], [], gather=gather, tm=FFN_TOKEN_TILE)


def _ffn_loss_fwd(x, target, pre_g, post_g, final_g, wgt, wut, wd, name):
    s, d = x.shape
    f = wgt.shape[0]
    return _row_call(_ffn_loss_body, name, s, [x, target], [pre_g, post_g, final_g, wgt, wut, wd],
                     [(d, BF16), (f, BF16), (f, BF16), (d, F32), (d, F32)],
                     [((1, 1), F32), ((1, d), F32)], tm=FFN_TOKEN_TILE)


def _ffn_bwd_body(dh_ref, x_ref, a_ref, b_ref, f_ref, pre_ref, post_ref, wgt_ref, wut_ref, wd_ref,
                  dx_ref, da_ref, db_ref, hm_ref, df_ref, dpre_ref, dpost_ref):
    dh = dh_ref[...]
    df, dpost = _rms_bwd(f_ref[...], post_ref[...], 0.5 * dh)
    dfb = df.astype(BF16)
    df_ref[...] = dfb
    dhmid = _dot_nt(dfb, wd_ref[...])
    a = a_ref[...].astype(F32)
    b = b_ref[...].astype(F32)
    sig = _sigmoid(a)
    sa = a * sig
    hm_ref[...] = (sa * b).astype(BF16)
    dab = (dhmid * b * sig * (1.0 + a * (1.0 - sig))).astype(BF16)
    dbb = (dhmid * sa).astype(BF16)
    da_ref[...] = dab
    db_ref[...] = dbb
    dn = _dot(dab, wgt_ref[...]) + _dot(dbb, wut_ref[...])
    dxn, dpre = _rms_bwd(x_ref[...], pre_ref[...], dn)
    dx_ref[...] = dh + dxn
    _acc(dpre_ref, dpre)
    _acc(dpost_ref, dpost)


def _ffn_bwd(dh, x, a, b, f, pre_g, post_g, wgt, wut, wd, name, scatter=None):
    s, d = x.shape
    ff = wgt.shape[0]
    return _row_call(_ffn_bwd_body, name, s, [dh, x, a, b, f], [pre_g, post_g, wgt, wut, wd],
                     [(d, F32), (ff, BF16), (ff, BF16), (ff, BF16), (d, BF16)],
                     [((1, d), F32), ((1, d), F32)], scatter=scatter)


def _inproj_fwd_body(h_ref, g_ref, wt_ref, n_ref, uv_ref, qkv_ref):
    n, _ = _rms(h_ref[...], g_ref[...])
    nb = n.astype(BF16)
    n_ref[...] = nb
    proj = _dot_nt(nb, wt_ref[...])
    nuv = uv_ref.shape[1]
    uv_ref[...] = proj[:, :nuv]
    qkv_ref[...] = proj[:, nuv:].astype(BF16)


def _inproj_fwd(h, g, w_in_t):
    s, d = h.shape
    sgu_w = SGU_GROUPS * GROUP_DIM
    sb_w = SB_HEADS * SB_HEAD_DIM
    return _row_call(_inproj_fwd_body, "inproj_fwd", s, [h], [g, w_in_t],
                     [(d, BF16), (2 * sgu_w, F32), (3 * sb_w, BF16)], [])


def _inproj_bwd_body(dh_ref, dproj_ref, h_ref, g_ref, wt_ref, dhout_ref, dg_ref):
    dn = _dot(dproj_ref[...], wt_ref[...])
    dhn, dg = _rms_bwd(h_ref[...], g_ref[...], dn)
    dhout_ref[...] = dh_ref[...] + dhn
    _acc(dg_ref, dg)


def _inproj_bwd(dh, dproj, h, g, w_in_t):
    s, d = h.shape
    return _row_call(_inproj_bwd_body, "inproj_bwd", s, [dh, dproj, h], [g, w_in_t],
                     [(d, F32)], [((1, d), F32)])


def _causal_w(ws_ref, g):
    row = lax.broadcasted_iota(jnp.int32, (CHUNK, CHUNK), 0)
    col = lax.broadcasted_iota(jnp.int32, (CHUNK, CHUNK), 1)
    return jnp.where(row >= col, ws_ref[g], 0.0), row >= col


def _group_norm(v):
    mu = jnp.mean(v, axis=-1, keepdims=True)
    d = v - mu
    rstd = lax.rsqrt(jnp.mean(d * d, axis=-1, keepdims=True) + EPS)
    return d * rstd, rstd


def _sgu_fwd_body(uv_ref, ng_ref, nb_ref, ws_ref, bs_ref, out_ref):
    width = SGU_GROUPS * GROUP_DIM
    for c in range(uv_ref.shape[0] // CHUNK):
        rows = pl.ds(c * CHUNK, CHUNK)
        for g in range(SGU_GROUPS):
            lanes = pl.ds(g * GROUP_DIM, GROUP_DIM)
            u = _gelu(uv_ref[rows, lanes])
            v = _gelu(uv_ref[rows, pl.ds(width + g * GROUP_DIM, GROUP_DIM)])
            vhat, _ = _group_norm(v)
            vn = vhat * ng_ref[:, lanes] + nb_ref[:, lanes]
            w, _ = _causal_w(ws_ref, g)
            mixed = _dot(w, vn) + bs_ref[g]
            out_ref[rows, lanes] = u * mixed


def _sgu_fwd(uv_pre, ng, nb, ws, bs):
    s = uv_pre.shape[0]
    return _row_call(_sgu_fwd_body, "sgu_fwd", s, [uv_pre], [ng, nb, ws, bs],
                     [(SGU_GROUPS * GROUP_DIM, F32)], [])[0]


def _sgu_bwd_body(uv_ref, do_ref, ng_ref, nb_ref, ws_ref, bs_ref,
                  duv_ref, dws_ref, dbs_ref, dng_ref, dnb_ref):
    width = SGU_GROUPS * GROUP_DIM

    @pl.when(pl.program_id(0) == 0)
    def _():
        dws_ref[...] = jnp.zeros_like(dws_ref)
        dbs_ref[...] = jnp.zeros_like(dbs_ref)
        dng_ref[...] = jnp.zeros_like(dng_ref)
        dnb_ref[...] = jnp.zeros_like(dnb_ref)

    for c in range(uv_ref.shape[0] // CHUNK):
        rows = pl.ds(c * CHUNK, CHUNK)
        for g in range(SGU_GROUPS):
            lanes = pl.ds(g * GROUP_DIM, GROUP_DIM)
            vlanes = pl.ds(width + g * GROUP_DIM, GROUP_DIM)
            u_pre = uv_ref[rows, lanes]
            v_pre = uv_ref[rows, vlanes]
            u = _gelu(u_pre)
            v = _gelu(v_pre)
            vhat, rstd = _group_norm(v)
            gain = ng_ref[:, lanes]
            vn = vhat * gain + nb_ref[:, lanes]
            w, causal = _causal_w(ws_ref, g)
            mixed = _dot(w, vn) + bs_ref[g]
            dout = do_ref[rows, lanes]
            du = dout * mixed
            dmixed = dout * u
            dbs_ref[g] += jnp.sum(dmixed, axis=1, keepdims=True)
            dws_ref[g] += jnp.where(causal, _dot_nt(dmixed, vn), 0.0)
            dvn = _dot_tn(w, dmixed)
            dng_ref[:, lanes] += jnp.sum(dvn * vhat, axis=0, keepdims=True)
            dnb_ref[:, lanes] += jnp.sum(dvn, axis=0, keepdims=True)
            dvh = dvn * gain
            dv = rstd * (dvh - jnp.mean(dvh, axis=-1, keepdims=True)
                         - vhat * jnp.mean(dvh * vhat, axis=-1, keepdims=True))
            duv_ref[rows, lanes] = (du * _gelu_grad(u_pre)).astype(BF16)
            duv_ref[rows, vlanes] = (dv * _gelu_grad(v_pre)).astype(BF16)


def _sgu_bwd(uv_pre, dout_a, ng, nb, ws, bs):
    s = uv_pre.shape[0]
    width = SGU_GROUPS * GROUP_DIM
    return _row_call(_sgu_bwd_body, "sgu_bwd", s, [uv_pre, dout_a], [ng, nb, ws, bs],
                     [(2 * width, BF16)],
                     [(ws.shape, F32), (bs.shape, F32), ((1, width), F32), ((1, width), F32)])


def _mesh_place():
    return lax.axis_index("x"), lax.axis_index("y"), lax.axis_index("c")


def _other_chips(mx, my):
    return [(1 - mx, my), (mx, 1 - my), (1 - mx, 1 - my)]


_ANY = pl.BlockSpec(memory_space=pl.ANY)
AG_SEMS = 8
_AG_SCRATCH = [pltpu.SemaphoreType.DMA((AG_SEMS,)), pltpu.SemaphoreType.DMA((AG_SEMS,)),
               pltpu.SemaphoreType.DMA(())]
ROW_ALIGN_ANY_DTYPE = 16


def _gather_phases(x_ref, out_ref, send_sems, recv_sems, local_sem):
    mx, my, mc = _mesh_place()
    me, sibling = (mx, my, mc), (mx, my, 1 - mc)
    x_chip, y_chip, far_chip = _other_chips(mx, my)
    rows = x_ref.shape[0]
    cut = (rows // (2 * ROW_ALIGN_ANY_DTYPE)) * ROW_ALIGN_ANY_DTYPE
    parts = {3: pl.ds(0, cut), 7: pl.ds(cut, rows - cut)}

    def slot(px, py, pc):
        return out_ref.at[4 * px + 2 * py + pc]

    def copy(k, block, to, src=None):
        where = slot(*block) if k not in parts else slot(*block).at[parts[k]]
        return pltpu.make_async_remote_copy(
            src_ref=where if src is None else src, dst_ref=where,
            send_sem=send_sems.at[k], recv_sem=recv_sems.at[k],
            device_id=to, device_id_type=MESH)

    mine = pltpu.make_async_copy(x_ref, slot(*me), local_sem)
    first = [copy(0, me, sibling, src=x_ref), copy(1, me, (*x_chip, mc), src=x_ref),
             copy(2, me, (*y_chip, mc), src=x_ref)]
    relayed = [copy(3, (*x_chip, mc), (*y_chip, mc)), copy(7, (*y_chip, mc), (*x_chip, mc))]
    passed = [copy(4, (*x_chip, mc), sibling), copy(5, (*y_chip, mc), sibling), copy(6, (*far_chip, mc), sibling)]

    def start():
        mine.start()
        for cp in first:
            cp.start()

    def relay():
        copy(1, (*x_chip, mc), me).wait_recv()
        relayed[0].start()
        passed[0].start()
        copy(2, (*y_chip, mc), me).wait_recv()
        relayed[1].start()
        passed[1].start()

    def forward():
        copy(3, (*far_chip, mc), me).wait_recv()
        copy(7, (*far_chip, mc), me).wait_recv()
        passed[2].start()

    def finish():
        copy(0, sibling, me).wait_recv()
        for k, chip in ((4, x_chip), (5, y_chip), (6, far_chip)):
            copy(k, (*chip, 1 - mc), me).wait_recv()
        for cp in first + relayed + passed:
            cp.wait_send()
        mine.wait()

    return start, relay, forward, finish


def _all_gather(x, name):
    r, c = x.shape

    def body(x_ref, out_ref, send_sems, recv_sems, local_sem):
        for phase in _gather_phases(x_ref, out_ref, send_sems, recv_sems, local_sem):
            phase()

    return pl.pallas_call(
        body,
        name=name,
        out_shape=jax.ShapeDtypeStruct((N_DEV, r, c), x.dtype),
        in_specs=[_ANY],
        out_specs=_ANY,
        scratch_shapes=list(_AG_SCRATCH),
    )(x)


def _scatter_phases(p_ref, out_ref, send_sems, recv_sems, local_sem):
    mx, my, mc = _mesh_place()
    me = 4 * mx + 2 * my + mc
    copies = []
    for k in range(1, N_DEV):
        tx, ty, tc = mx ^ ((k >> 2) & 1), my ^ ((k >> 1) & 1), mc ^ (k & 1)
        copies.append(pltpu.make_async_remote_copy(
            src_ref=p_ref.at[4 * tx + 2 * ty + tc], dst_ref=out_ref.at[me],
            send_sem=send_sems.at[k - 1], recv_sem=recv_sems.at[k - 1],
            device_id=(tx, ty, tc), device_id_type=MESH))
    mine = pltpu.make_async_copy(p_ref.at[me], out_ref.at[me], local_sem)

    def start():
        mine.start()
        for cp in copies:
            cp.start()

    def finish():
        for cp in copies:
            cp.wait()
        mine.wait()

    return start, finish


def _chip_exchange_phases(q_ref, out_ref, send_sems, recv_sems, local_sem):
    mx, my, mc = _mesh_place()
    copies = [pltpu.make_async_remote_copy(
        src_ref=q_ref.at[2 * cx + cy], dst_ref=out_ref.at[k],
        send_sem=send_sems.at[k], recv_sem=recv_sems.at[k],
        device_id=(cx, cy, mc), device_id_type=MESH)
        for k, (cx, cy) in enumerate(_other_chips(mx, my))]

    def start():
        for cp in copies:
            cp.start()

    def finish():
        for cp in copies:
            cp.wait()

    return start, finish


SB_DEAD = -105.0
HEADS_PER_TILE = 2
TILES_PER_STEP = 2
HEADS_PER_STEP = HEADS_PER_TILE * TILES_PER_STEP
STEP_LANES = TILES_PER_STEP * HEADS_PER_TILE * SB_HEAD_DIM
TILE_LANES = HEADS_PER_TILE * SB_HEAD_DIM
STACK_ROWS = HEADS_PER_STEP * Q_BLOCK
TILE_ROWS = HEADS_PER_TILE * Q_BLOCK
SB_FORWARD_LEAD = 12


def _stack_heads(x):
    lane = lax.broadcasted_iota(jnp.int32, x.shape, 1)
    zero = jnp.zeros_like(x)
    return jnp.concatenate(
        [jnp.where(lane // SB_HEAD_DIM == h, x, zero) for h in range(HEADS_PER_STEP)], axis=0)


def _unstack_tile(x):
    first = lax.broadcasted_iota(jnp.int32, (Q_BLOCK, TILE_LANES), 1) < SB_HEAD_DIM
    return jnp.where(first, x[:Q_BLOCK], x[Q_BLOCK:])


def _sb_logs(qs, k, diagonal):
    z = _dot_nt(qs, k) * (SB_HEAD_DIM ** -0.5)
    sp = jnp.log1p(jnp.exp(-jnp.abs(z)))
    log_beta = jnp.minimum(z, 0.0) - sp
    log_1m_raw = -jnp.maximum(z, 0.0) - sp
    if not diagonal:
        return None, log_beta, log_1m_raw, log_1m_raw
    row = lax.broadcasted_iota(jnp.int32, z.shape, 0)
    col = lax.broadcasted_iota(jnp.int32, z.shape, 1)
    strict = col < jnp.bitwise_and(row, Q_BLOCK - 1)
    return strict, log_beta, log_1m_raw, jnp.where(strict, log_1m_raw, 0.0)


def _masked(strict, x):
    return x if strict is None else jnp.where(strict, x, 0.0)


def _key_sums(x, pick):
    hi, lo = _split_bf16(x)
    both = jnp.dot(jnp.concatenate([hi, lo], axis=0), pick, preferred_element_type=F32)
    return both[:x.shape[0]] + both[x.shape[0]:]


def _key_order():
    row = lax.broadcasted_iota(jnp.int32, (Q_BLOCK, Q_BLOCK), 0)
    col = lax.broadcasted_iota(jnp.int32, (Q_BLOCK, Q_BLOCK), 1)
    return row, col


def _sb_fwd_body(q_ref, k_ref, v_ref, shard_ref, o_ref, tot_ref, cnt_ref, gathered_ref,
                 acc_ref, send_sems, recv_sems, local_sem):
    grp, qb = pl.program_id(0), pl.program_id(1)
    last_grp, last_qb = pl.num_programs(0) - 1, pl.num_programs(1) - 1
    ag_start, ag_relay, ag_forward, ag_finish = _gather_phases(
        shard_ref, gathered_ref, send_sems, recv_sems, local_sem)
    pl.when(jnp.logical_and(grp == 0, qb == 0))(ag_start)

    qs = _stack_heads(q_ref[...])
    row, col = _key_order()
    later = (row > col).astype(BF16)

    def block(i, c, diagonal):
        rows = pl.ds(pl.multiple_of((qb - i) * Q_BLOCK, Q_BLOCK), Q_BLOCK)
        strict, log_beta, _, log_1m = _sb_logs(qs, k_ref[rows, :], diagonal)
        a = _masked(strict, jnp.exp(log_beta + _key_sums(log_1m, later) + c)).astype(BF16)
        for t in range(TILES_PER_STEP):
            part = jnp.dot(a[t * TILE_ROWS:(t + 1) * TILE_ROWS], v_ref[rows, t * TILE_LANES:(t + 1) * TILE_LANES],
                           preferred_element_type=F32)
            if diagonal:
                acc_ref[t] = part
            else:
                acc_ref[t] += part
        return c + jnp.sum(log_1m, axis=1, keepdims=True)

    c = block(0, jnp.zeros((STACK_ROWS, 1), F32), True)

    def alive(carry):
        i, c = carry
        return jnp.logical_and(i <= qb, jnp.max(c) > SB_DEAD)

    def step(carry):
        i, c = carry
        return i + 1, block(i, c, False)

    n, c = lax.while_loop(alive, step, (jnp.int32(1), c))
    for t in range(TILES_PER_STEP):
        o_ref[:, t * TILE_LANES:(t + 1) * TILE_LANES] = _unstack_tile(acc_ref[t])
    for h in range(HEADS_PER_STEP):
        tot_ref[h] = c[h * Q_BLOCK:(h + 1) * Q_BLOCK]
    cnt_ref[grp, qb] = n.astype(F32)
    pl.when(jnp.logical_and(grp == last_grp, qb == 0))(ag_relay)
    pl.when(jnp.logical_and(grp == last_grp, qb == jnp.maximum(last_qb - SB_FORWARD_LEAD, 0)))(ag_forward)
    pl.when(jnp.logical_and(grp == last_grp, qb == last_qb))(ag_finish)


def _sb_fwd(qkv, shard):
    s = qkv.shape[0]
    groups = SB_HEADS // HEADS_PER_STEP
    nq = s // Q_BLOCK
    return pl.pallas_call(
        functools.partial(_sb_fwd_body),
        name="sb_fwd",
        grid=(groups, nq),
        in_specs=[pl.BlockSpec((Q_BLOCK, STEP_LANES), lambda g, i: (i, g)),
                  pl.BlockSpec((s, STEP_LANES), lambda g, i: (0, groups + g)),
                  pl.BlockSpec((s, STEP_LANES), lambda g, i: (0, 2 * groups + g)),
                  _ANY],
        out_specs=[pl.BlockSpec((Q_BLOCK, STEP_LANES), lambda g, i: (i, g)),
                   pl.BlockSpec((HEADS_PER_STEP, Q_BLOCK, 1), lambda g, i: (g, i, 0)),
                   pl.BlockSpec(memory_space=pltpu.SMEM),
                   _ANY],
        out_shape=[jax.ShapeDtypeStruct((s, SB_HEADS * SB_HEAD_DIM), F32),
                   jax.ShapeDtypeStruct((SB_HEADS, s, 1), F32),
                   jax.ShapeDtypeStruct((groups, nq), F32),
                   jax.ShapeDtypeStruct((N_DEV,) + shard.shape, shard.dtype)],
        scratch_shapes=[pltpu.VMEM((TILES_PER_STEP, TILE_ROWS, TILE_LANES), F32)] + list(_AG_SCRATCH),
        compiler_params=pltpu.CompilerParams(
            dimension_semantics=("arbitrary", "arbitrary"), vmem_limit_bytes=VMEM_LIMIT_V7X),
    )(qkv, qkv, qkv, shard)


def _sb_bwd_body(cnt_ref, q_ref, k_ref, v_ref, tot_ref, do_ref, part_ref, dq_ref, dk_ref, dv_ref, recv_ref,
                 acc_ref, send_sems, recv_sems, local_sem):
    grp, qb = pl.program_id(0), pl.program_id(1)
    last_grp, last_qb = pl.num_programs(0) - 1, pl.num_programs(1) - 1
    rs_start, rs_finish = _scatter_phases(part_ref, recv_ref, send_sems, recv_sems, local_sem)
    pl.when(jnp.logical_and(grp == 0, qb == 0))(rs_start)

    @pl.when(qb == 0)
    def _():
        dk_ref[...] = jnp.zeros_like(dk_ref)
        dv_ref[...] = jnp.zeros_like(dv_ref)

    acc_ref[...] = jnp.zeros_like(acc_ref)
    qs = _stack_heads(q_ref[...])
    dos = _stack_heads(do_ref[...].astype(BF16))
    tot = jnp.concatenate([tot_ref[h] for h in range(HEADS_PER_STEP)], axis=0)
    row, col = _key_order()
    up_to = (row <= col).astype(BF16)
    earlier = (row < col).astype(BF16)
    scale = SB_HEAD_DIM ** -0.5
    n = jnp.clip(cnt_ref[grp, qb].astype(jnp.int32), 1, qb + 1)

    def block(kb, c, ce, diagonal):
        rows = pl.ds(pl.multiple_of(kb * Q_BLOCK, Q_BLOCK), Q_BLOCK)
        k = k_ref[rows, :]
        strict, log_beta, log_1m_raw, log_1m = _sb_logs(qs, k, diagonal)
        suffix = tot - c - _key_sums(log_1m, up_to)
        a = _masked(strict, jnp.exp(log_beta + suffix))
        de = _dot_nt(dos, v_ref[rows, :]) * a
        before = ce + _key_sums(de, earlier)
        dz = _masked(strict, de * jnp.exp(log_1m_raw) - before * jnp.exp(log_beta)).astype(BF16)
        for t in range(TILES_PER_STEP):
            acc_ref[t] += jnp.dot(dz[t * TILE_ROWS:(t + 1) * TILE_ROWS], k[:, t * TILE_LANES:(t + 1) * TILE_LANES],
                                  preferred_element_type=F32)
        dk_ref[rows, :] += _dot_tn(dz, qs) * scale
        dv_ref[rows, :] += _dot_tn(a, dos)
        return c + jnp.sum(log_1m, axis=1, keepdims=True), ce + jnp.sum(de, axis=1, keepdims=True)

    def step(i, carry):
        return block(qb - n + 1 + i, *carry, False)

    zc = jnp.zeros((STACK_ROWS, 1), F32)
    c, ce = lax.fori_loop(0, n - 1, step, (zc, zc))
    block(qb, c, ce, True)
    for t in range(TILES_PER_STEP):
        dq_ref[:, t * TILE_LANES:(t + 1) * TILE_LANES] = _unstack_tile(acc_ref[t]) * scale
    pl.when(jnp.logical_and(grp == last_grp, qb == last_qb))(rs_finish)


def _sb_bwd(cnt, qkv, tot, dout_b, parts):
    s = qkv.shape[0]
    groups = SB_HEADS // HEADS_PER_STEP
    return pl.pallas_call(
        functools.partial(_sb_bwd_body),
        name="sb_bwd",
        grid=(groups, s // Q_BLOCK),
        in_specs=[pl.BlockSpec(memory_space=pltpu.SMEM),
                  pl.BlockSpec((Q_BLOCK, STEP_LANES), lambda g, i: (i, g)),
                  pl.BlockSpec((s, STEP_LANES), lambda g, i: (0, groups + g)),
                  pl.BlockSpec((s, STEP_LANES), lambda g, i: (0, 2 * groups + g)),
                  pl.BlockSpec((HEADS_PER_STEP, Q_BLOCK, 1), lambda g, i: (g, i, 0)),
                  pl.BlockSpec((Q_BLOCK, STEP_LANES), lambda g, i: (i, g)),
                  _ANY],
        out_specs=[pl.BlockSpec((Q_BLOCK, STEP_LANES), lambda g, i: (i, g)),
                   pl.BlockSpec((s, STEP_LANES), lambda g, i: (0, g)),
                   pl.BlockSpec((s, STEP_LANES), lambda g, i: (0, g)),
                   _ANY],
        out_shape=[jax.ShapeDtypeStruct((s, SB_HEADS * SB_HEAD_DIM), F32)] * 3
        + [jax.ShapeDtypeStruct(parts.shape, parts.dtype)],
        scratch_shapes=[pltpu.VMEM((TILES_PER_STEP, TILE_ROWS, TILE_LANES), F32)] + list(_AG_SCRATCH),
        compiler_params=pltpu.CompilerParams(
            dimension_semantics=("arbitrary", "arbitrary"), vmem_limit_bytes=VMEM_LIMIT_V7X),
    )(cnt, qkv, qkv, qkv, tot, dout_b, parts)


def _outproj_fwd_body(oa_ref, ob_ref, h_ref, ga_ref, gb_ref, gpost_ref, w_ref,
                      merged_ref, mo_ref, hout_ref):
    half = oa_ref.shape[1]
    ma, _ = _rms(oa_ref[...], ga_ref[...])
    mb, _ = _rms(ob_ref[...], gb_ref[...])
    mab = ma.astype(BF16)
    mbb = mb.astype(BF16)
    merged_ref[:, :half] = mab
    merged_ref[:, half:] = mbb
    mo = (jnp.dot(mab, w_ref[:half, :], preferred_element_type=F32)
          + jnp.dot(mbb, w_ref[half:, :], preferred_element_type=F32))
    mo_ref[...] = mo
    y, _ = _rms(mo, gpost_ref[...])
    hout_ref[...] = h_ref[...] + y


def _outproj_fwd(out_a, out_b, h, ga, gb, gpost, w_out):
    s, d = h.shape
    return _row_call(_outproj_fwd_body, "outproj_fwd", s, [out_a, out_b, h], [ga, gb, gpost, w_out],
                     [(d, BF16), (d, F32), (d, F32)], [])


def _outproj_bwd_body(dh_ref, mo_ref, oa_ref, ob_ref, ga_ref, gb_ref, gpost_ref, w_ref,
                      dmo_ref, doa_ref, dob_ref, dga_ref, dgb_ref, dgpost_ref):
    half = oa_ref.shape[1]
    dmo, dgpost = _rms_bwd(mo_ref[...], gpost_ref[...], dh_ref[...])
    dmob = dmo.astype(BF16)
    dmo_ref[...] = dmob
    dma = _dot_nt(dmob, w_ref[:half, :])
    dmb = _dot_nt(dmob, w_ref[half:, :])
    doa, dga = _rms_bwd(oa_ref[...], ga_ref[...], dma)
    dob, dgb = _rms_bwd(ob_ref[...], gb_ref[...], dmb)
    doa_ref[...] = doa
    dob_ref[...] = dob
    _acc(dga_ref, dga)
    _acc(dgb_ref, dgb)
    _acc(dgpost_ref, dgpost)


def _outproj_bwd(dh, mo, out_a, out_b, ga, gb, gpost, w_out):
    s, d = dh.shape
    half = out_a.shape[1]
    return _row_call(_outproj_bwd_body, "outproj_bwd", s, [dh, mo, out_a, out_b], [ga, gb, gpost, w_out],
                     [(d, BF16), (half, F32), (half, F32)],
                     [((1, half), F32), ((1, half), F32), ((1, d), F32)])


def _kv_fwd_body(mem_ref, g_ref, wt_ref, memn_ref, kv_ref):
    n, _ = _rms(mem_ref[...], g_ref[...])
    nb = n.astype(BF16)
    memn_ref[...] = nb
    kv_ref[...] = _dot_nt(nb, wt_ref[...]).astype(BF16)


def _kv_fwd(mem, g, w_kv_t):
    m, d = mem.shape
    return _row_call(_kv_fwd_body, "kv_fwd", m, [mem], [g, w_kv_t], [(d, BF16), (w_kv_t.shape[0], BF16)], [])


def _kv_bwd_body(dkv_ref, mem_ref, memn_ref, g_ref, wt_ref, dwt_ref, dg_ref):
    dkvb = dkv_ref[...].astype(BF16)
    dwt_ref[...] = _dot_tn(dkvb, memn_ref[...]).astype(BF16)
    dmemn = _dot(dkvb, wt_ref[...])
    _, dg = _rms_bwd(mem_ref[...], g_ref[...], dmemn)
    dg_ref[...] = dg


def _kv_bwd(dkv, mem, memn, g, w_kv_t):
    m, d = mem.shape
    return pl.pallas_call(
        functools.partial(_kv_bwd_body),
        name="kv_bwd",
        out_shape=[jax.ShapeDtypeStruct(w_kv_t.shape, BF16), jax.ShapeDtypeStruct((1, d), F32)],
        compiler_params=pltpu.CompilerParams(vmem_limit_bytes=VMEM_LIMIT_V7X),
    )(dkv, mem, memn, g, w_kv_t)


def _xa_fwd_body(h_ref, gpre_ref, gpost_ref, wq_ref, wo_ref, kv_ref,
                 n_ref, q_ref, o_ref, c_ref, hout_ref):
    h = h_ref[...]
    d = h.shape[1]
    n, _ = _rms(h, gpre_ref[...])
    nb = n.astype(BF16)
    n_ref[...] = nb
    qb = jnp.dot(nb, wq_ref[...], preferred_element_type=F32).astype(BF16)
    q_ref[...] = qb
    for hd in range(XA_HEADS):
        lanes = slice(hd * XA_HEAD_DIM, (hd + 1) * XA_HEAD_DIM)
        k = kv_ref[:, lanes]
        v = kv_ref[:, d + hd * XA_HEAD_DIM:d + (hd + 1) * XA_HEAD_DIM]
        logits = _dot_nt(qb[:, lanes], k) * (XA_HEAD_DIM ** -0.5)
        e = jnp.exp(logits - jnp.max(logits, axis=-1, keepdims=True))
        p = e / jnp.sum(e, axis=-1, keepdims=True)
        o_ref[:, lanes] = jnp.dot(p.astype(BF16), v, preferred_element_type=F32).astype(BF16)
    c = jnp.dot(o_ref[...], wo_ref[...], preferred_element_type=F32)
    c_ref[...] = c
    y, _ = _rms(c, gpost_ref[...])
    hout_ref[...] = h + y


def _xa_fwd(h, gpre, gpost, wq, wo, kv):
    s, d = h.shape
    return _row_call(_xa_fwd_body, "xa_fwd", s, [h], [gpre, gpost, wq, wo, kv],
                     [(d, BF16), (d, BF16), (d, BF16), (d, F32), (d, F32)], [])


def _xa_bwd_body(dh_ref, h_ref, c_ref, q_ref, o_ref, gpre_ref, gpost_ref, wq_ref, wo_ref, kv_ref,
                 dhout_ref, dc_ref, dq_ref, dkv_ref, dgpre_ref, dgpost_ref):
    dh = dh_ref[...]
    d = dh.shape[1]
    scale = XA_HEAD_DIM ** -0.5
    dc, dgpost = _rms_bwd(c_ref[...], gpost_ref[...], dh)
    dcb = dc.astype(BF16)
    dc_ref[...] = dcb
    dob = _dot_nt(dcb, wo_ref[...]).astype(BF16)

    @pl.when(pl.program_id(0) == 0)
    def _():
        dkv_ref[...] = jnp.zeros_like(dkv_ref)

    for hd in range(XA_HEADS):
        lanes = slice(hd * XA_HEAD_DIM, (hd + 1) * XA_HEAD_DIM)
        vlanes = slice(d + hd * XA_HEAD_DIM, d + (hd + 1) * XA_HEAD_DIM)
        qh = q_ref[:, lanes]
        k = kv_ref[:, lanes]
        v = kv_ref[:, vlanes]
        logits = _dot_nt(qh, k) * scale
        e = jnp.exp(logits - jnp.max(logits, axis=-1, keepdims=True))
        p = e / jnp.sum(e, axis=-1, keepdims=True)
        doh = dob[:, lanes]
        dp = _dot_nt(doh, v)
        dl = (p * (dp - jnp.sum(dp * p, axis=-1, keepdims=True)) * scale).astype(BF16)
        dq_ref[:, lanes] = jnp.dot(dl, k, preferred_element_type=F32).astype(BF16)
        dkv_ref[:, lanes] += _dot_tn(dl, qh)
        dkv_ref[:, vlanes] += _dot_tn(p, doh)
    dn = _dot_nt(dq_ref[...], wq_ref[...])
    dhn, dgpre = _rms_bwd(h_ref[...], gpre_ref[...], dn)
    dhout_ref[...] = dh + dhn
    _acc(dgpre_ref, dgpre)
    _acc(dgpost_ref, dgpost)


def _xa_bwd(dh, h, c, q, o, gpre, gpost, wq, wo, kv, scatter=None):
    s, d = h.shape
    return _row_call(_xa_bwd_body, "xa_bwd", s, [dh, h, c, q, o], [gpre, gpost, wq, wo, kv],
                     [(d, F32), (d, BF16), (d, BF16)],
                     [(kv.shape, F32), ((1, d), F32), ((1, d), F32)], scatter=scatter)


def _largest_tile(n, cap):
    best = 128
    for t in range(128, cap + 1, 128):
        if n % t == 0:
            best = t
    return best


def _mm_tn(a, bs, name, gather=None, chip_exchange=None):
    s, k = a.shape
    n = bs[0].shape[1]
    nb = len(bs)
    ts = min(2048, s)
    tk = _largest_tile(k, 1536)
    tn = _largest_tile(n, 1536 // nb)

    steps = s // ts
    grid = (k // tk, n // tn, steps)
    sent = gather if gather is not None else chip_exchange
    hosted = sent is not None
    phases = _gather_phases if gather is not None else _chip_exchange_phases
    landed_shape = None
    if hosted:
        landed_shape = (N_DEV,) + sent.shape if gather is not None else (3,) + sent.shape[1:]

    def body(a_ref, *refs):
        if hosted:
            sent_ref, landed_ref, refs = refs[nb], refs[2 * nb + 1], refs[:nb] + refs[nb + 1:2 * nb + 1] + refs[2 * nb + 2:]
            start, *rest = phases(sent_ref, landed_ref, *refs[-3:])
            place = (pl.program_id(0) * grid[1] + pl.program_id(1)) * grid[2] + pl.program_id(2)
            pl.when(place == 0)(start)
        b_refs, o_refs, acc_refs = refs[:nb], refs[nb:2 * nb], refs[2 * nb:3 * nb]
        at = a_ref[...]
        t = pl.program_id(2)

        @pl.when(t == 0)
        def _():
            for acc_ref in acc_refs:
                acc_ref[...] = jnp.zeros_like(acc_ref)

        for b_ref, acc_ref in zip(b_refs, acc_refs):
            acc_ref[...] += _dot_tn(at, b_ref[...])

        @pl.when(t == steps - 1)
        def _():
            for o_ref, acc_ref in zip(o_refs, acc_refs):
                o_ref[...] = acc_ref[...].astype(BF16)

        if hosted:
            total = grid[0] * grid[1] * grid[2]

            @pl.when(place == total - 1)
            def _():
                for phase in rest:
                    phase()

    return pl.pallas_call(
        body,
        name=name,
        grid=grid,
        in_specs=[pl.BlockSpec((ts, tk), lambda i, j, t: (t, i))]
        + [pl.BlockSpec((ts, tn), lambda i, j, t: (t, j))] * nb + ([_ANY] if hosted else []),
        out_specs=[pl.BlockSpec((tk, tn), lambda i, j, t: (i, j))] * nb + ([_ANY] if hosted else []),
        out_shape=[jax.ShapeDtypeStruct((k, n), BF16)] * nb
        + ([jax.ShapeDtypeStruct(landed_shape, sent.dtype)] if hosted else []),
        scratch_shapes=[pltpu.VMEM((tk, tn), F32)] * nb + (list(_AG_SCRATCH) if hosted else []),
        compiler_params=pltpu.CompilerParams(
            dimension_semantics=("arbitrary", "arbitrary", "arbitrary"),
            vmem_limit_bytes=VMEM_LIMIT_V7X),
    )(a, *bs, *([sent] if hosted else []))


_SMALL_SHAPES = {
    "sgu_norm_g": (1, SGU_GROUPS * GROUP_DIM),
    "sgu_norm_b": (1, SGU_GROUPS * GROUP_DIM),
    "sgu_w_s": (SGU_GROUPS, CHUNK, CHUNK),
    "sgu_b_s": (SGU_GROUPS, CHUNK, 1),
}


def _small_views(small):
    return {n: v.reshape(_SMALL_SHAPES.get(n, v.shape)) for n, v in small.items()}


def _small_unviews(views, like):
    return {n: v.reshape(like[n].shape) for n, v in views.items()}


def _unpack_rows(gathered, names, shard_rows):
    out, off = {}, 0
    for n in names:
        rows = shard_rows[n]
        out[n] = gathered[:, off:off + rows, :].reshape(N_DEV * rows, gathered.shape[2])
        off += rows
    return out


def _row_tile(r, cap):
    best = 16
    for t in range(16, cap + 1, 16):
        if r % t == 0:
            best = t
    return best


def _sum_received(received, name):
    _, r, c = received.shape
    tr = _row_tile(r, 1024)

    def body(rc_ref, g_ref):
        g = rc_ref[0].astype(F32)
        for t in range(1, N_DEV):
            g = g + rc_ref[t].astype(F32)
        g_ref[...] = g

    return pl.pallas_call(
        body, name=name, grid=(r // tr,),
        in_specs=[pl.BlockSpec((N_DEV, tr, c), lambda i: (0, i, 0))],
        out_specs=pl.BlockSpec((tr, c), lambda i: (i, 0)),
        out_shape=jax.ShapeDtypeStruct((r, c), F32),
    )(received)


def _local_step(x, mem, target, small, big, shards, shard_rows):
    sm, w = small, dict(big)
    d_model = x.shape[1]

    def parts(names):
        return jnp.concatenate([gw.pop(n).reshape(N_DEV, -1, d_model) for n in names], axis=1)

    h1, n1, a1, b1, f1, landed = _ffn_fwd(
        x, sm["ffn1_pre_g"], sm["ffn1_post_g"], w["ffn1_w_gate"], w["ffn1_w_up"], w["ffn1_w_down"],
        "ffn1_fwd", gather=shards["ffn1_fwd"])
    w.update(_unpack_rows(landed, GATHER_IN["ffn1_fwd"], shard_rows))
    n2, uv_pre, qkv = _inproj_fwd(h1, sm["mix_pre_g"], w["w_in"])
    out_a = _sgu_fwd(uv_pre, sm["sgu_norm_g"], sm["sgu_norm_b"], sm["sgu_w_s"], sm["sgu_b_s"])
    out_b, tot, cnt, landed = _sb_fwd(qkv, shards["sb_fwd"])
    w.update(_unpack_rows(landed, GATHER_IN["sb_fwd"], shard_rows))
    merged, mo, h2 = _outproj_fwd(out_a, out_b, h1, sm["sgu_out_g"], sm["sb_out_g"],
                                  sm["mix_post_g"], w["w_out"])
    memn, kv = _kv_fwd(mem, sm["mem_norm_g"], w["xa_w_kv"])
    n3, qx, ox, cx, h3 = _xa_fwd(h2, sm["xa_pre_g"], sm["xa_post_g"], w["xa_w_q"], w["xa_w_o"], kv)
    n4, a2, b2, f2, dh4, loss, dg_final = _ffn_loss_fwd(
        h3, target, sm["ffn2_pre_g"], sm["ffn2_post_g"], sm["final_norm_g"],
        w["ffn2_w_gate"], w["ffn2_w_up"], w["ffn2_w_down"], "ffn2_fwd")

    gs, gw = {"final_norm_g": dg_final}, {}
    dh3, da2, db2, hm2, df2, gs["ffn2_pre_g"], gs["ffn2_post_g"] = _ffn_bwd(
        dh4, h3, a2, b2, f2, sm["ffn2_pre_g"], sm["ffn2_post_g"],
        w["ffn2_w_gate"], w["ffn2_w_up"], w["ffn2_w_down"], "ffn2_bwd")
    gw["ffn2_w_gate"], = _mm_tn(da2, [n4], "ffn2_dw_gate")
    gw["ffn2_w_up"], = _mm_tn(db2, [n4], "ffn2_dw_up")
    gw["ffn2_w_down"], = _mm_tn(hm2, [df2], "ffn2_dw_down")

    received = {}
    dh2, dc, dqx, dkv, gs["xa_pre_g"], gs["xa_post_g"], received["xa_bwd"] = _xa_bwd(
        dh3, h2, cx, qx, ox, sm["xa_pre_g"], sm["xa_post_g"], w["xa_w_q"], w["xa_w_o"], kv,
        scatter=parts(SCATTER_IN["xa_bwd"]))
    gw["xa_w_o"], = _mm_tn(ox, [dc], "xa_dw_o")
    gw["xa_w_q"], = _mm_tn(n3, [dqx], "xa_dw_q")
    gw["xa_w_kv"], gs["mem_norm_g"] = _kv_bwd(dkv, mem, memn, sm["mem_norm_g"], w["xa_w_kv"])

    dmo, dout_a, dout_b, gs["sgu_out_g"], gs["sb_out_g"], gs["mix_post_g"] = _outproj_bwd(
        dh2, mo, out_a, out_b, sm["sgu_out_g"], sm["sb_out_g"], sm["mix_post_g"], w["w_out"])
    gw["w_out"], = _mm_tn(merged, [dmo], "mix_dw_out")
    dq, dk, dv, received["sb_bwd"] = _sb_bwd(cnt, qkv, tot, dout_b, parts(SCATTER_IN["sb_bwd"]))
    duv, gs["sgu_w_s"], gs["sgu_b_s"], gs["sgu_norm_g"], gs["sgu_norm_b"] = _sgu_bwd(
        uv_pre, dout_a, sm["sgu_norm_g"], sm["sgu_norm_b"], sm["sgu_w_s"], sm["sgu_b_s"])
    dproj = jnp.concatenate([duv] + [t.astype(BF16) for t in (dq, dk, dv)], axis=1)
    dh1, gs["mix_pre_g"] = _inproj_bwd(dh2, dproj, h1, sm["mix_pre_g"], w["w_in"])
    gw["w_in"], = _mm_tn(dproj, [n2], "mix_dw_in")

    dx, da1, db1, hm1, df1, gs["ffn1_pre_g"], gs["ffn1_post_g"], received["ffn1_bwd"] = _ffn_bwd(
        dh1, x, a1, b1, f1, sm["ffn1_pre_g"], sm["ffn1_post_g"],
        w["ffn1_w_gate"], w["ffn1_w_up"], w["ffn1_w_down"], "ffn1_bwd",
        scatter=parts(SCATTER_IN["ffn1_bwd"]))
    summed = {host: _sum_received(r, "rs_sum_" + host) for host, r in received.items()}

    mx, my, mc = _mesh_place()
    place = jnp.stack([mc, 2 * mx + my]).astype(jnp.int32)

    def pair_stage(dw, tag):
        p4 = dw.reshape(N_DEV // 2, 2, -1, d_model)
        recv_a = _pair_exchange(p4, "rs_pair_exchange_" + tag)
        return p4, recv_a, _pair_sum(place, p4, recv_a, "rs_pair_sum_" + tag)

    dw, small_grads = _mm_tn(da1, [n1], "ffn1_dw_gate", gather=_pack_small(gs))
    gate = pair_stage(dw, "gate")
    dw, gate_b = _mm_tn(db1, [n1], "ffn1_dw_up", chip_exchange=gate[2])
    up = pair_stage(dw, "up")
    dw, up_b = _mm_tn(hm1, [df1], "ffn1_dw_down", chip_exchange=up[2])
    down = pair_stage(dw, "down")
    down_b = _chip_exchange(down[2], "rs_chip_exchange_down")
    for n, (p4, recv_a, _), recv_b in (("ffn1_w_gate", gate, gate_b), ("ffn1_w_up", up, up_b),
                                       ("ffn1_w_down", down, down_b)):
        summed[n] = _rs_final(place, p4, recv_a, recv_b, "rs_final_" + n)
    return loss, dx, small_grads, summed


def _pair_exchange(p4, name):
    nchip, _, r, c = p4.shape

    def body(p_ref, out_ref, send_sems, recv_sems):
        mx, my, mc = _mesh_place()
        copies = [pltpu.make_async_remote_copy(
            src_ref=p_ref.at[j, 1 - mc], dst_ref=out_ref.at[j],
            send_sem=send_sems.at[j], recv_sem=recv_sems.at[j],
            device_id=(mx, my, 1 - mc), device_id_type=MESH) for j in range(nchip)]
        for cp in copies:
            cp.start()
        for cp in copies:
            cp.wait()

    return pl.pallas_call(
        body,
        name=name,
        out_shape=jax.ShapeDtypeStruct((nchip, r, c), p4.dtype),
        in_specs=[_ANY],
        out_specs=_ANY,
        scratch_shapes=[pltpu.SemaphoreType.DMA((nchip,)), pltpu.SemaphoreType.DMA((nchip,))],
    )(p4)


def _chip_exchange(q, name):
    _, r, c = q.shape

    def body(q_ref, out_ref, send_sems, recv_sems, local_sem):
        for phase in _chip_exchange_phases(q_ref, out_ref, send_sems, recv_sems, local_sem):
            phase()

    return pl.pallas_call(
        body,
        name=name,
        out_shape=jax.ShapeDtypeStruct((3, r, c), q.dtype),
        in_specs=[_ANY],
        out_specs=_ANY,
        scratch_shapes=list(_AG_SCRATCH),
    )(q)


def _rs_row_tile(r):
    return _row_tile(r, 1024)


def _pair_sum(place, p4, recv_a, name):
    nchip, _, r, c = p4.shape
    tr = _rs_row_tile(r)

    def body(place_ref, p_ref, a_ref, q_ref):
        q_ref[0] = (p_ref[0, 0].astype(F32) + a_ref[0].astype(F32)).astype(BF16)

    return pl.pallas_call(
        body,
        name=name,
        grid_spec=pltpu.PrefetchScalarGridSpec(
            num_scalar_prefetch=1,
            grid=(nchip, r // tr),
            in_specs=[pl.BlockSpec((1, 1, tr, c), lambda j, i, pref: (j, pref[0], i, 0)),
                      pl.BlockSpec((1, tr, c), lambda j, i, pref: (j, i, 0))],
            out_specs=pl.BlockSpec((1, tr, c), lambda j, i, pref: (j, i, 0)),
        ),
        out_shape=jax.ShapeDtypeStruct((nchip, r, c), BF16),
    )(place, p4, recv_a)


def _rs_final(place, p4, recv_a, recv_b, name):
    _, _, r, c = p4.shape
    tr = _rs_row_tile(r)

    def body(place_ref, p_ref, a_ref, b_ref, g_ref):
        g = p_ref[0, 0].astype(F32) + a_ref[0].astype(F32)
        for k in range(3):
            g = g + b_ref[k].astype(F32)
        g_ref[...] = g

    return pl.pallas_call(
        body,
        name=name,
        grid_spec=pltpu.PrefetchScalarGridSpec(
            num_scalar_prefetch=1,
            grid=(r // tr,),
            in_specs=[pl.BlockSpec((1, 1, tr, c), lambda i, pref: (pref[1], pref[0], i, 0)),
                      pl.BlockSpec((1, tr, c), lambda i, pref: (pref[1], i, 0)),
                      pl.BlockSpec((3, tr, c), lambda i, pref: (0, i, 0))],
            out_specs=pl.BlockSpec((tr, c), lambda i, pref: (i, 0)),
        ),
        out_shape=jax.ShapeDtypeStruct((r, c), F32),
    )(place, p4, recv_a, recv_b)


def _adamw_math(w, g, m, v):
    m = ADAM_B1 * m + (1.0 - ADAM_B1) * g
    v = ADAM_B2 * v + (1.0 - ADAM_B2) * (g * g)
    m_hat = m / (1.0 - ADAM_B1 ** ADAM_STEP)
    v_hat = v / (1.0 - ADAM_B2 ** ADAM_STEP)
    delta = -ADAM_LR * (m_hat / (jnp.sqrt(v_hat) + ADAM_EPS) + ADAM_WD * w)
    return delta, m, v


def _adamw(w, g, m, v, name):
    r, c = w.shape
    tr = r if r <= 512 else 256

    def body(w_ref, g_ref, m_ref, v_ref, d_ref, mo_ref, vo_ref):
        d_ref[...], mo_ref[...], vo_ref[...] = _adamw_math(w_ref[...], g_ref[...], m_ref[...], v_ref[...])

    spec = pl.BlockSpec((tr, c), lambda i: (i, 0))
    out = jax.ShapeDtypeStruct((r, c), F32)
    return pl.pallas_call(
        body, name=name, grid=(r // tr,), in_specs=[spec] * 4, out_specs=[spec] * 3,
        out_shape=[out] * 3,
    )(w, g, m, v)


def _small_sum_adamw(gathered, ws, ms, vs):
    _, r, c = gathered.shape
    count = len(ws)

    def body(ga_ref, *refs):
        w_refs, m_refs, v_refs = refs[:count], refs[count:2 * count], refs[2 * count:3 * count]
        out_refs = refs[3 * count:]
        for o_ref in out_refs:
            o_ref[...] = jnp.zeros_like(o_ref)
        off = 0
        for w_ref, m_ref, v_ref in zip(w_refs, m_refs, v_refs):
            rows = pl.ds(off, w_ref.shape[0])
            g = ga_ref[0, rows, :]
            for k in range(1, N_DEV):
                g = g + ga_ref[k, rows, :]
            results = (g, *_adamw_math(w_ref[...], g, m_ref[...], v_ref[...]))
            for o_ref, val in zip(out_refs, results):
                o_ref[rows, :] = val
            off += w_ref.shape[0] + (-w_ref.shape[0]) % SMALL_ROW_ALIGN

    out = jax.ShapeDtypeStruct((r, c), F32)
    return pl.pallas_call(body, name="small_sum_adamw", out_shape=[out] * 4)(gathered, *ws, *ms, *vs)


_WEIGHTS = ["ffn1_pre_g", "ffn1_post_g", "ffn1_w_gate", "ffn1_w_up", "ffn1_w_down", "mix_pre_g",
            "mix_post_g", "w_in", "sgu_norm_g", "sgu_norm_b", "sgu_w_s", "sgu_b_s", "sgu_out_g",
            "sb_out_g", "w_out", "xa_pre_g", "xa_post_g", "mem_norm_g", "xa_w_q", "xa_w_kv", "xa_w_o",
            "ffn2_pre_g", "ffn2_post_g", "ffn2_w_gate", "ffn2_w_up", "ffn2_w_down", "final_norm_g"]
_BIG = ["ffn1_w_gate", "ffn1_w_up", "ffn1_w_down", "w_in", "w_out", "xa_w_q", "xa_w_kv", "xa_w_o",
        "ffn2_w_gate", "ffn2_w_up", "ffn2_w_down"]
_COL_SHARDED = ("ffn1_w_gate", "ffn1_w_up", "w_in", "xa_w_kv", "ffn2_w_gate", "ffn2_w_up")
_EARLY = ["ffn1_w_gate", "ffn1_w_up", "ffn1_w_down"]
GATHER_IN = {"ffn1_fwd": ["w_in", "ffn2_w_gate"],
             "sb_fwd": ["w_out", "xa_w_q", "xa_w_kv", "xa_w_o", "ffn2_w_up", "ffn2_w_down"]}
SCATTER_IN = {"xa_bwd": ["ffn2_w_gate"],
              "sb_bwd": ["ffn2_w_up", "ffn2_w_down", "xa_w_q", "xa_w_kv", "xa_w_o", "w_out"],
              "ffn1_bwd": ["w_in"]}
_SMALL = [n for n in _WEIGHTS if n not in _BIG]
SMALL_LANES = 128
SMALL_ROW_ALIGN = 8


def _pack_small(tensors):
    parts = []
    for n in _SMALL:
        t = tensors[n].reshape(-1, SMALL_LANES)
        pad = (-t.shape[0]) % SMALL_ROW_ALIGN
        parts.append(jnp.pad(t, ((0, pad), (0, 0))) if pad else t)
    return jnp.concatenate(parts, axis=0)


def _unpack_small(packed, like):
    out, off = {}, 0
    for n in _SMALL:
        size = like[n].size
        rows = size // SMALL_LANES
        out[n] = packed[off:off + rows].reshape(like[n].shape)
        off += rows + (-rows) % SMALL_ROW_ALIGN
    return out


def kernel(x, mem, ffn1_pre_g, ffn1_post_g, ffn1_w_gate, ffn1_w_up, ffn1_w_down, mix_pre_g, mix_post_g, w_in, sgu_norm_g, sgu_norm_b, sgu_w_s, sgu_b_s, sgu_out_g, sb_out_g, w_out, xa_pre_g, xa_post_g, mem_norm_g, xa_w_q, xa_w_kv, xa_w_o, ffn2_pre_g, ffn2_post_g, ffn2_w_gate, ffn2_w_up, ffn2_w_down, final_norm_g, loss_target, m_ffn1_pre_g, m_ffn1_post_g, m_ffn1_w_gate, m_ffn1_w_up, m_ffn1_w_down, m_mix_pre_g, m_mix_post_g, m_w_in, m_sgu_norm_g, m_sgu_norm_b, m_sgu_w_s, m_sgu_b_s, m_sgu_out_g, m_sb_out_g, m_w_out, m_xa_pre_g, m_xa_post_g, m_mem_norm_g, m_xa_w_q, m_xa_w_kv, m_xa_w_o, m_ffn2_pre_g, m_ffn2_post_g, m_ffn2_w_gate, m_ffn2_w_up, m_ffn2_w_down, m_final_norm_g, v_ffn1_pre_g, v_ffn1_post_g, v_ffn1_w_gate, v_ffn1_w_up, v_ffn1_w_down, v_mix_pre_g, v_mix_post_g, v_w_in, v_sgu_norm_g, v_sgu_norm_b, v_sgu_w_s, v_sgu_b_s, v_sgu_out_g, v_sb_out_g, v_w_out, v_xa_pre_g, v_xa_post_g, v_mem_norm_g, v_xa_w_q, v_xa_w_kv, v_xa_w_o, v_ffn2_pre_g, v_ffn2_post_g, v_ffn2_w_gate, v_ffn2_w_up, v_ffn2_w_down, v_final_norm_g):
    vals = dict(locals())
    d_model = x.shape[-1]

    def packed(names):
        return jnp.concatenate(
            [(vals[n][0].T if n in _COL_SHARDED else vals[n][0]).astype(BF16) for n in names], axis=0)

    shard_rows = {n: vals[n].shape[2 if n in _COL_SHARDED else 1] for n in _BIG}
    big = _unpack_rows(_all_gather(packed(_EARLY), "ag_weights"), _EARLY, shard_rows)

    small = {n: vals[n] for n in _SMALL}
    loss_part, dx, gathered_small, summed = _local_step(
        x[0], mem[0], loss_target[0], _small_views(small), big,
        {host: packed(names) for host, names in GATHER_IN.items()}, shard_rows)
    loss = lax.psum(loss_part[0, 0], ("x", "y", "c"))

    grads, deltas, new_m, new_v = {}, {}, {}, {}
    for names, g_rows in [([n], summed[n]) for n in _EARLY] + [(SCATTER_IN[h], summed[h]) for h in SCATTER_IN]:
        off = 0
        for n in names:
            rows = shard_rows[n]
            g = g_rows[off:off + rows]
            off += rows
            state = [vals[n][0], vals["m_" + n][0], vals["v_" + n][0]]
            flipped = n in _COL_SHARDED and rows % SMALL_LANES != 0
            if flipped:
                state = [t.T for t in state]
            elif n in _COL_SHARDED:
                g = g.T
            outs = (g, *_adamw(state[0], g, state[1], state[2], "adamw_" + n))
            if flipped:
                outs = tuple(t.T for t in outs)
            grads[n], deltas[n], new_m[n], new_v[n] = (t[None] for t in outs)

    outs = _small_sum_adamw(gathered_small,
                            *([vals[pre + n].reshape(-1, SMALL_LANES) for n in _SMALL] for pre in ("", "m_", "v_")))
    for dst, packed in zip((grads, deltas, new_m, new_v), outs):
        dst.update(_unpack_small(packed, small))

    return (loss, dx[None], *[grads[n] for n in _WEIGHTS], *[deltas[n] for n in _WEIGHTS],
            *[new_m[n] for n in _WEIGHTS], *[new_v[n] for n in _WEIGHTS])
```

```python
import functools

import jax
import jax.numpy as jnp
from jax import lax
from jax.experimental import pallas as pl
from jax.experimental.pallas import tpu as pltpu

F32 = jnp.float32
BF16 = jnp.bfloat16
EPS = 1e-6
MESH = pl.DeviceIdType.MESH
N_DEV = 8

SGU_GROUPS = 4
GROUP_DIM = 128
CHUNK = 128
SB_HEADS = 8
SB_HEAD_DIM = 64
Q_BLOCK = 128
XA_HEADS = 4
XA_HEAD_DIM = 256

ADAM_LR = 0.001
ADAM_B1 = 0.9
ADAM_B2 = 0.999
ADAM_EPS = 1e-08
ADAM_WD = 0.01
ADAM_STEP = 10

VMEM_LIMIT_V7X = 56 * 1024 * 1024
GELU_C0 = 0.7978845608028654
GELU_C1 = 0.044715


def _dot(a, b):
    return jnp.dot(a.astype(BF16), b.astype(BF16), preferred_element_type=F32)


def _dot_nt(a, b):
    return lax.dot_general(a.astype(BF16), b.astype(BF16), (((1,), (1,)), ((), ())),
                           preferred_element_type=F32)


def _dot_tn(a, b):
    return lax.dot_general(a.astype(BF16), b.astype(BF16), (((0,), (0,)), ((), ())),
                           preferred_element_type=F32)


def _rms(x, g):
    r = lax.rsqrt(jnp.mean(x * x, axis=-1, keepdims=True) + EPS)
    return x * r * g, r


def _rms_bwd(x, g, dy):
    r = lax.rsqrt(jnp.mean(x * x, axis=-1, keepdims=True) + EPS)
    xh = x * r
    gy = dy * g
    dx = r * (gy - xh * jnp.mean(gy * xh, axis=-1, keepdims=True))
    dg = jnp.sum(dy * xh, axis=0, keepdims=True)
    return dx, dg


def _sigmoid(x):
    return jax.nn.sigmoid(x)


def _gelu(x):
    t = jnp.tanh(GELU_C0 * (x + GELU_C1 * x * x * x))
    return 0.5 * x * (1.0 + t)


def _gelu_grad(x):
    t = jnp.tanh(GELU_C0 * (x + GELU_C1 * x * x * x))
    return 0.5 * (1.0 + t) + 0.5 * x * (1.0 - t * t) * GELU_C0 * (1.0 + 3.0 * GELU_C1 * x * x)


def _split_bf16(x):
    hi = x.astype(BF16)
    lo = (x - hi.astype(F32)).astype(BF16)
    return hi, lo


def _row_spec(tm, cols):
    return pl.BlockSpec((tm, cols), lambda i: (i, 0))


def _full_spec(shape, buffers=None):
    nd = len(shape)
    mode = None if buffers is None else pl.Buffered(buffers)
    return pl.BlockSpec(tuple(shape), lambda i: (0,) * nd, pipeline_mode=mode)


FFN_TOKEN_TILE = 512


def _token_tile(s):
    return min(256, s)


def _row_call(body, name, s, tiled_in, full_in, tiled_out, acc_out, gather=None, scatter=None, tm=None):
    tm = _token_tile(s) if tm is None else min(tm, s)
    steps = s // tm
    in_specs = [_row_spec(tm, a.shape[1]) for a in tiled_in] + [_full_spec(a.shape, buffers=1) for a in full_in]
    out_specs = [_row_spec(tm, c) for c, _ in tiled_out] + [_full_spec(sh) for sh, _ in acc_out]
    out_shape = [jax.ShapeDtypeStruct((s, c), dt) for c, dt in tiled_out]
    out_shape += [jax.ShapeDtypeStruct(sh, dt) for sh, dt in acc_out]
    operands = [*tiled_in, *full_in]
    scratch = []
    kernel_body = functools.partial(body)
    sent = gather if gather is not None else scatter
    if sent is not None:
        n_in, n_out = len(operands), len(out_shape)
        out_shape.append(jax.ShapeDtypeStruct(
            (N_DEV,) + sent.shape if gather is not None else sent.shape, sent.dtype))
        operands.append(sent)
        in_specs.append(_ANY)
        out_specs.append(_ANY)
        scratch = list(_AG_SCRATCH)

        def kernel_body(*refs):
            ins, sent_ref = refs[:n_in], refs[n_in]
            outs, landed_ref = refs[n_in + 1:n_in + 1 + n_out], refs[n_in + 1 + n_out]
            step = pl.program_id(0)
            if gather is not None:
                start, relay, forward, finish = _gather_phases(sent_ref, landed_ref, *refs[-3:])
            else:
                start, finish = _scatter_phases(sent_ref, landed_ref, *refs[-3:])
            pl.when(step == 0)(start)
            body(*ins, *outs)
            if gather is not None:
                pl.when(step == steps // 3)(relay)
                pl.when(step == (2 * steps) // 3)(forward)
            pl.when(step == steps - 1)(finish)

    return pl.pallas_call(
        kernel_body,
        name=name,
        grid=(steps,),
        in_specs=in_specs,
        out_specs=out_specs,
        out_shape=out_shape,
        scratch_shapes=scratch,
        compiler_params=pltpu.CompilerParams(
            dimension_semantics=("arbitrary",), vmem_limit_bytes=VMEM_LIMIT_V7X),
    )(*operands)


def _acc(ref, val):
    @pl.when(pl.program_id(0) == 0)
    def _():
        ref[...] = val

    @pl.when(pl.program_id(0) != 0)
    def _():
        ref[...] += val


def _ffn_fwd_tile(x_ref, pre_ref, post_ref, wgt_ref, wut_ref, wd_ref, n_ref, a_ref, b_ref, f_ref):
    x = x_ref[...]
    n, _ = _rms(x, pre_ref[...])
    nb = n.astype(BF16)
    n_ref[...] = nb
    a = _dot_nt(nb, wgt_ref[...])
    b = _dot_nt(nb, wut_ref[...])
    a_ref[...] = a.astype(BF16)
    b_ref[...] = b.astype(BF16)
    hmid = a * _sigmoid(a) * b
    f = jnp.dot(hmid.astype(BF16), wd_ref[...], preferred_element_type=F32)
    f_ref[...] = f
    y, _ = _rms(f, post_ref[...])
    return x + 0.5 * y


def _ffn_fwd_body(x_ref, pre_ref, post_ref, wgt_ref, wut_ref, wd_ref,
                  h_ref, n_ref, a_ref, b_ref, f_ref):
    h_ref[...] = _ffn_fwd_tile(x_ref, pre_ref, post_ref, wgt_ref, wut_ref, wd_ref, n_ref, a_ref, b_ref, f_ref)


def _ffn_loss_body(x_ref, t_ref, pre_ref, post_ref, gfin_ref, wgt_ref, wut_ref, wd_ref,
                   n_ref, a_ref, b_ref, f_ref, dh_ref, loss_ref, dg_ref):
    h = _ffn_fwd_tile(x_ref, pre_ref, post_ref, wgt_ref, wut_ref, wd_ref, n_ref, a_ref, b_ref, f_ref)
    d = h.shape[1]
    y, _ = _rms(h, gfin_ref[...])
    err = y - t_ref[...]
    part = (0.5 / d) * jnp.sum(jnp.sum(err * err, axis=1, keepdims=True), axis=0, keepdims=True)
    dh, dg = _rms_bwd(h, gfin_ref[...], err * (1.0 / d))
    dh_ref[...] = dh
    _acc(loss_ref, part)
    _acc(dg_ref, dg)


def _ffn_fwd(x, pre_g, post_g, wgt, wut, wd, name, gather=None):
    s, d = x.shape
    f = wgt.shape[0]
    return _row_call(_ffn_fwd_body, name, s, [x], [pre_g, post_g, wgt, wut, wd],
                     [(d, F32), (d, BF16), (f, BF16), (f, BF16), (d, F32)], [], gather=gather, tm=FFN_TOKEN_TILE)


def _ffn_loss_fwd(x, target, pre_g, post_g, final_g, wgt, wut, wd, name):
    s, d = x.shape
    f = wgt.shape[0]
    return _row_call(_ffn_loss_body, name, s, [x, target], [pre_g, post_g, final_g, wgt, wut, wd],
                     [(d, BF16), (f, BF16), (f, BF16), (d, F32), (d, F32)],
                     [((1, 1), F32), ((1, d), F32)], tm=FFN_TOKEN_TILE)


def _ffn_bwd_body(dh_ref, x_ref, a_ref, b_ref, f_ref, pre_ref, post_ref, wgt_ref, wut_ref, wd_ref,
                  dx_ref, da_ref, db_ref, hm_ref, df_ref, dpre_ref, dpost_ref):
    dh = dh_ref[...]
    df, dpost = _rms_bwd(f_ref[...], post_ref[...], 0.5 * dh)
    dfb = df.astype(BF16)
    df_ref[...] = dfb
    dhmid = _dot_nt(dfb, wd_ref[...])
    a = a_ref[...].astype(F32)
    b = b_ref[...].astype(F32)
    sig = _sigmoid(a)
    sa = a * sig
    hm_ref[...] = (sa * b).astype(BF16)
    dab = (dhmid * b * sig * (1.0 + a * (1.0 - sig))).astype(BF16)
    dbb = (dhmid * sa).astype(BF16)
    da_ref[...] = dab
    db_ref[...] = dbb
    dn = _dot(dab, wgt_ref[...]) + _dot(dbb, wut_ref[...])
    dxn, dpre = _rms_bwd(x_ref[...], pre_ref[...], dn)
    dx_ref[...] = dh + dxn
    _acc(dpre_ref, dpre)
    _acc(dpost_ref, dpost)


def _ffn_bwd(dh, x, a, b, f, pre_g, post_g, wgt, wut, wd, name, scatter=None):
    s, d = x.shape
    ff = wgt.shape[0]
    return _row_call(_ffn_bwd_body, name, s, [dh, x, a, b, f], [pre_g, post_g, wgt, wut, wd],
                     [(d, F32), (ff, BF16), (ff, BF16), (ff, BF16), (d, BF16)],
                     [((1, d), F32), ((1, d), F32)], scatter=scatter)


def _inproj_fwd_body(h_ref, g_ref, wt_ref, n_ref, uv_ref, qkv_ref):
    n, _ = _rms(h_ref[...], g_ref[...])
    nb = n.astype(BF16)
    n_ref[...] = nb
    proj = _dot_nt(nb, wt_ref[...])
    nuv = uv_ref.shape[1]
    uv_ref[...] = proj[:, :nuv]
    qkv_ref[...] = proj[:, nuv:].astype(BF16)


def _inproj_fwd(h, g, w_in_t):
    s, d = h.shape
    sgu_w = SGU_GROUPS * GROUP_DIM
    sb_w = SB_HEADS * SB_HEAD_DIM
    return _row_call(_inproj_fwd_body, "inproj_fwd", s, [h], [g, w_in_t],
                     [(d, BF16), (2 * sgu_w, F32), (3 * sb_w, BF16)], [])


def _inproj_bwd_body(dh_ref, dproj_ref, h_ref, g_ref, wt_ref, dhout_ref, dg_ref):
    dn = _dot(dproj_ref[...], wt_ref[...])
    dhn, dg = _rms_bwd(h_ref[...], g_ref[...], dn)
    dhout_ref[...] = dh_ref[...] + dhn
    _acc(dg_ref, dg)


def _inproj_bwd(dh, dproj, h, g, w_in_t, scatter=None):
    s, d = h.shape
    return _row_call(_inproj_bwd_body, "inproj_bwd", s, [dh, dproj, h], [g, w_in_t],
                     [(d, F32)], [((1, d), F32)], scatter=scatter)


def _causal_w(ws_ref, g):
    row = lax.broadcasted_iota(jnp.int32, (CHUNK, CHUNK), 0)
    col = lax.broadcasted_iota(jnp.int32, (CHUNK, CHUNK), 1)
    return jnp.where(row >= col, ws_ref[g], 0.0), row >= col


def _group_norm(v):
    mu = jnp.mean(v, axis=-1, keepdims=True)
    d = v - mu
    rstd = lax.rsqrt(jnp.mean(d * d, axis=-1, keepdims=True) + EPS)
    return d * rstd, rstd


def _sgu_fwd_body(uv_ref, ng_ref, nb_ref, ws_ref, bs_ref, out_ref):
    width = SGU_GROUPS * GROUP_DIM
    for c in range(uv_ref.shape[0] // CHUNK):
        rows = pl.ds(c * CHUNK, CHUNK)
        for g in range(SGU_GROUPS):
            lanes = pl.ds(g * GROUP_DIM, GROUP_DIM)
            u = _gelu(uv_ref[rows, lanes])
            v = _gelu(uv_ref[rows, pl.ds(width + g * GROUP_DIM, GROUP_DIM)])
            vhat, _ = _group_norm(v)
            vn = vhat * ng_ref[:, lanes] + nb_ref[:, lanes]
            w, _ = _causal_w(ws_ref, g)
            mixed = _dot(w, vn) + bs_ref[g]
            out_ref[rows, lanes] = u * mixed


def _sgu_fwd(uv_pre, ng, nb, ws, bs):
    s = uv_pre.shape[0]
    return _row_call(_sgu_fwd_body, "sgu_fwd", s, [uv_pre], [ng, nb, ws, bs],
                     [(SGU_GROUPS * GROUP_DIM, F32)], [])[0]


def _sgu_bwd_body(uv_ref, do_ref, ng_ref, nb_ref, ws_ref, bs_ref,
                  duv_ref, dws_ref, dbs_ref, dng_ref, dnb_ref):
    width = SGU_GROUPS * GROUP_DIM

    @pl.when(pl.program_id(0) == 0)
    def _():
        dws_ref[...] = jnp.zeros_like(dws_ref)
        dbs_ref[...] = jnp.zeros_like(dbs_ref)
        dng_ref[...] = jnp.zeros_like(dng_ref)
        dnb_ref[...] = jnp.zeros_like(dnb_ref)

    for c in range(uv_ref.shape[0] // CHUNK):
        rows = pl.ds(c * CHUNK, CHUNK)
        for g in range(SGU_GROUPS):
            lanes = pl.ds(g * GROUP_DIM, GROUP_DIM)
            vlanes = pl.ds(width + g * GROUP_DIM, GROUP_DIM)
            u_pre = uv_ref[rows, lanes]
            v_pre = uv_ref[rows, vlanes]
            u = _gelu(u_pre)
            v = _gelu(v_pre)
            vhat, rstd = _group_norm(v)
            gain = ng_ref[:, lanes]
            vn = vhat * gain + nb_ref[:, lanes]
            w, causal = _causal_w(ws_ref, g)
            mixed = _dot(w, vn) + bs_ref[g]
            dout = do_ref[rows, lanes]
            du = dout * mixed
            dmixed = dout * u
            dbs_ref[g] += jnp.sum(dmixed, axis=1, keepdims=True)
            dws_ref[g] += jnp.where(causal, _dot_nt(dmixed, vn), 0.0)
            dvn = _dot_tn(w, dmixed)
            dng_ref[:, lanes] += jnp.sum(dvn * vhat, axis=0, keepdims=True)
            dnb_ref[:, lanes] += jnp.sum(dvn, axis=0, keepdims=True)
            dvh = dvn * gain
            dv = rstd * (dvh - jnp.mean(dvh, axis=-1, keepdims=True)
                         - vhat * jnp.mean(dvh * vhat, axis=-1, keepdims=True))
            duv_ref[rows, lanes] = (du * _gelu_grad(u_pre)).astype(BF16)
            duv_ref[rows, vlanes] = (dv * _gelu_grad(v_pre)).astype(BF16)


def _sgu_bwd(uv_pre, dout_a, ng, nb, ws, bs, scatter=None):
    s = uv_pre.shape[0]
    width = SGU_GROUPS * GROUP_DIM
    return _row_call(_sgu_bwd_body, "sgu_bwd", s, [uv_pre, dout_a], [ng, nb, ws, bs],
                     [(2 * width, BF16)],
                     [(ws.shape, F32), (bs.shape, F32), ((1, width), F32), ((1, width), F32)], scatter=scatter)


def _mesh_place():
    return lax.axis_index("x"), lax.axis_index("y"), lax.axis_index("c")


def _other_chips(mx, my):
    return [(1 - mx, my), (mx, 1 - my), (1 - mx, 1 - my)]


_ANY = pl.BlockSpec(memory_space=pl.ANY)
AG_SEMS = 8
_AG_SCRATCH = [pltpu.SemaphoreType.DMA((AG_SEMS,)), pltpu.SemaphoreType.DMA((AG_SEMS,)),
               pltpu.SemaphoreType.DMA(())]
ROW_ALIGN_ANY_DTYPE = 16


def _gather_phases(x_ref, out_ref, send_sems, recv_sems, local_sem):
    mx, my, mc = _mesh_place()
    me, sibling = (mx, my, mc), (mx, my, 1 - mc)
    x_chip, y_chip, far_chip = _other_chips(mx, my)
    rows = x_ref.shape[0]
    cut = (rows // (2 * ROW_ALIGN_ANY_DTYPE)) * ROW_ALIGN_ANY_DTYPE
    parts = {3: pl.ds(0, cut), 7: pl.ds(cut, rows - cut)}

    def slot(px, py, pc):
        return out_ref.at[4 * px + 2 * py + pc]

    def copy(k, block, to, src=None):
        where = slot(*block) if k not in parts else slot(*block).at[parts[k]]
        return pltpu.make_async_remote_copy(
            src_ref=where if src is None else src, dst_ref=where,
            send_sem=send_sems.at[k], recv_sem=recv_sems.at[k],
            device_id=to, device_id_type=MESH)

    mine = pltpu.make_async_copy(x_ref, slot(*me), local_sem)
    first = [copy(0, me, sibling, src=x_ref), copy(1, me, (*x_chip, mc), src=x_ref),
             copy(2, me, (*y_chip, mc), src=x_ref)]
    relayed = [copy(3, (*x_chip, mc), (*y_chip, mc)), copy(7, (*y_chip, mc), (*x_chip, mc))]
    passed = [copy(4, (*x_chip, mc), sibling), copy(5, (*y_chip, mc), sibling), copy(6, (*far_chip, mc), sibling)]

    def start():
        mine.start()
        for cp in first:
            cp.start()

    def relay():
        copy(1, (*x_chip, mc), me).wait_recv()
        relayed[0].start()
        passed[0].start()
        copy(2, (*y_chip, mc), me).wait_recv()
        relayed[1].start()
        passed[1].start()

    def forward():
        copy(3, (*far_chip, mc), me).wait_recv()
        copy(7, (*far_chip, mc), me).wait_recv()
        passed[2].start()

    def finish():
        copy(0, sibling, me).wait_recv()
        for k, chip in ((4, x_chip), (5, y_chip), (6, far_chip)):
            copy(k, (*chip, 1 - mc), me).wait_recv()
        for cp in first + relayed + passed:
            cp.wait_send()
        mine.wait()

    return start, relay, forward, finish


def _all_gather(x, name):
    r, c = x.shape

    def body(x_ref, out_ref, send_sems, recv_sems, local_sem):
        for phase in _gather_phases(x_ref, out_ref, send_sems, recv_sems, local_sem):
            phase()

    return pl.pallas_call(
        body,
        name=name,
        out_shape=jax.ShapeDtypeStruct((N_DEV, r, c), x.dtype),
        in_specs=[_ANY],
        out_specs=_ANY,
        scratch_shapes=list(_AG_SCRATCH),
    )(x)


def _scatter_phases(p_ref, out_ref, send_sems, recv_sems, local_sem):
    mx, my, mc = _mesh_place()
    me = 4 * mx + 2 * my + mc
    copies = []
    for k in range(1, N_DEV):
        tx, ty, tc = mx ^ ((k >> 2) & 1), my ^ ((k >> 1) & 1), mc ^ (k & 1)
        copies.append(pltpu.make_async_remote_copy(
            src_ref=p_ref.at[4 * tx + 2 * ty + tc], dst_ref=out_ref.at[me],
            send_sem=send_sems.at[k - 1], recv_sem=recv_sems.at[k - 1],
            device_id=(tx, ty, tc), device_id_type=MESH))
    mine = pltpu.make_async_copy(p_ref.at[me], out_ref.at[me], local_sem)

    def start():
        mine.start()
        for cp in copies:
            cp.start()

    def finish():
        for cp in copies:
            cp.wait()
        mine.wait()

    return start, finish


def _chip_exchange_phases(q_ref, out_ref, send_sems, recv_sems, local_sem):
    mx, my, mc = _mesh_place()
    copies = [pltpu.make_async_remote_copy(
        src_ref=q_ref.at[2 * cx + cy], dst_ref=out_ref.at[k],
        send_sem=send_sems.at[k], recv_sem=recv_sems.at[k],
        device_id=(cx, cy, mc), device_id_type=MESH)
        for k, (cx, cy) in enumerate(_other_chips(mx, my))]

    def start():
        for cp in copies:
            cp.start()

    def finish():
        for cp in copies:
            cp.wait()

    return start, finish


SB_DEAD = -105.0
HEADS_PER_TILE = 2
TILES_PER_STEP = 2
HEADS_PER_STEP = HEADS_PER_TILE * TILES_PER_STEP
STEP_LANES = TILES_PER_STEP * HEADS_PER_TILE * SB_HEAD_DIM
TILE_LANES = HEADS_PER_TILE * SB_HEAD_DIM
STACK_ROWS = HEADS_PER_STEP * Q_BLOCK
TILE_ROWS = HEADS_PER_TILE * Q_BLOCK
STACKS_PER_STEP = 2
GROUP_LANES = STACKS_PER_STEP * STEP_LANES
HEADS_PER_GROUP = STACKS_PER_STEP * HEADS_PER_STEP
SB_FORWARD_LEAD = 6


def _stack_heads(x):
    lane = lax.broadcasted_iota(jnp.int32, x.shape, 1)
    zero = jnp.zeros_like(x)
    return jnp.concatenate(
        [jnp.where(lane // SB_HEAD_DIM == h, x, zero) for h in range(HEADS_PER_STEP)], axis=0)


def _unstack_tile(x):
    first = lax.broadcasted_iota(jnp.int32, (Q_BLOCK, TILE_LANES), 1) < SB_HEAD_DIM
    return jnp.where(first, x[:Q_BLOCK], x[Q_BLOCK:])


def _sb_logs(qs, k, diagonal):
    z = _dot_nt(qs, k) * (SB_HEAD_DIM ** -0.5)
    sp = jnp.log1p(jnp.exp(-jnp.abs(z)))
    log_beta = jnp.minimum(z, 0.0) - sp
    log_1m_raw = -jnp.maximum(z, 0.0) - sp
    if not diagonal:
        return None, log_beta, log_1m_raw, log_1m_raw
    row = lax.broadcasted_iota(jnp.int32, z.shape, 0)
    col = lax.broadcasted_iota(jnp.int32, z.shape, 1)
    strict = col < jnp.bitwise_and(row, Q_BLOCK - 1)
    return strict, log_beta, log_1m_raw, jnp.where(strict, log_1m_raw, 0.0)


def _masked(strict, x):
    return x if strict is None else jnp.where(strict, x, 0.0)


def _key_sums(x, pick):
    hi, lo = _split_bf16(x)
    both = jnp.dot(jnp.concatenate([hi, lo], axis=0), pick, preferred_element_type=F32)
    return both[:x.shape[0]] + both[x.shape[0]:]


def _key_order():
    row = lax.broadcasted_iota(jnp.int32, (Q_BLOCK, Q_BLOCK), 0)
    col = lax.broadcasted_iota(jnp.int32, (Q_BLOCK, Q_BLOCK), 1)
    return row, col


def _sb_fwd_body(q_ref, k_ref, v_ref, shard_ref, o_ref, tot_ref, cnt_ref, gathered_ref,
                 acc_ref, send_sems, recv_sems, local_sem):
    grp, qb = pl.program_id(0), pl.program_id(1)
    last_grp, last_qb = pl.num_programs(0) - 1, pl.num_programs(1) - 1
    ag_start, ag_relay, ag_forward, ag_finish = _gather_phases(
        shard_ref, gathered_ref, send_sems, recv_sems, local_sem)
    pl.when(jnp.logical_and(grp == 0, qb == 0))(ag_start)

    stacks = range(STACKS_PER_STEP)
    lanes = [slice(st * STEP_LANES, (st + 1) * STEP_LANES) for st in stacks]
    qs = [_stack_heads(q_ref[:, lanes[st]]) for st in stacks]
    row, col = _key_order()
    later = (row > col).astype(BF16)

    def block(i, cs, diagonal):
        rows = pl.ds(pl.multiple_of((qb - i) * Q_BLOCK, Q_BLOCK), Q_BLOCK)
        new_cs = []
        for st in stacks:
            strict, log_beta, _, log_1m = _sb_logs(qs[st], k_ref[rows, lanes[st]], diagonal)
            a = _masked(strict, jnp.exp(log_beta + _key_sums(log_1m, later) + cs[st])).astype(BF16)
            for t in range(TILES_PER_STEP):
                tile = st * TILES_PER_STEP + t
                part = jnp.dot(a[t * TILE_ROWS:(t + 1) * TILE_ROWS],
                               v_ref[rows, tile * TILE_LANES:(tile + 1) * TILE_LANES],
                               preferred_element_type=F32)
                if diagonal:
                    acc_ref[tile] = part
                else:
                    acc_ref[tile] += part
            new_cs.append(cs[st] + jnp.sum(log_1m, axis=1, keepdims=True))
        return tuple(new_cs)

    cs = block(0, (jnp.zeros((STACK_ROWS, 1), F32),) * STACKS_PER_STEP, True)

    def alive(carry):
        i, cs = carry
        return jnp.logical_and(i <= qb, jnp.max(functools.reduce(jnp.maximum, cs)) > SB_DEAD)

    def step(carry):
        i, cs = carry
        return i + 1, block(i, cs, False)

    n, cs = lax.while_loop(alive, step, (jnp.int32(1), cs))
    for tile in range(STACKS_PER_STEP * TILES_PER_STEP):
        o_ref[:, tile * TILE_LANES:(tile + 1) * TILE_LANES] = _unstack_tile(acc_ref[tile])
    for st in stacks:
        for h in range(HEADS_PER_STEP):
            tot_ref[st * HEADS_PER_STEP + h] = cs[st][h * Q_BLOCK:(h + 1) * Q_BLOCK]
    cnt_ref[grp, qb] = n.astype(F32)
    place = grp * (last_qb + 1) + qb
    last = (last_grp + 1) * (last_qb + 1) - 1
    pl.when(place == (last + 1) // 2)(ag_relay)
    pl.when(place == jnp.maximum(last - SB_FORWARD_LEAD, (last + 1) // 2))(ag_forward)
    pl.when(place == last)(ag_finish)


def _sb_fwd(qkv, shard):
    s = qkv.shape[0]
    groups = SB_HEADS // HEADS_PER_GROUP
    nq = s // Q_BLOCK
    return pl.pallas_call(
        functools.partial(_sb_fwd_body),
        name="sb_fwd",
        grid=(groups, nq),
        in_specs=[pl.BlockSpec((Q_BLOCK, GROUP_LANES), lambda g, i: (i, g)),
                  pl.BlockSpec((s, GROUP_LANES), lambda g, i: (0, groups + g)),
                  pl.BlockSpec((s, GROUP_LANES), lambda g, i: (0, 2 * groups + g)),
                  _ANY],
        out_specs=[pl.BlockSpec((Q_BLOCK, GROUP_LANES), lambda g, i: (i, g)),
                   pl.BlockSpec((HEADS_PER_GROUP, Q_BLOCK, 1), lambda g, i: (g, i, 0)),
                   pl.BlockSpec(memory_space=pltpu.SMEM),
                   _ANY],
        out_shape=[jax.ShapeDtypeStruct((s, SB_HEADS * SB_HEAD_DIM), F32),
                   jax.ShapeDtypeStruct((SB_HEADS, s, 1), F32),
                   jax.ShapeDtypeStruct((groups, nq), F32),
                   jax.ShapeDtypeStruct((N_DEV,) + shard.shape, shard.dtype)],
        scratch_shapes=[pltpu.VMEM((STACKS_PER_STEP * TILES_PER_STEP, TILE_ROWS, TILE_LANES), F32)]
        + list(_AG_SCRATCH),
        compiler_params=pltpu.CompilerParams(
            dimension_semantics=("arbitrary", "arbitrary"), vmem_limit_bytes=VMEM_LIMIT_V7X),
    )(qkv, qkv, qkv, shard)


def _sb_bwd_body(cnt_ref, q_ref, k_ref, v_ref, tot_ref, do_ref, part_ref, dq_ref, dk_ref, dv_ref, recv_ref,
                 acc_ref, send_sems, recv_sems, local_sem):
    grp, qb = pl.program_id(0), pl.program_id(1)
    last_grp, last_qb = pl.num_programs(0) - 1, pl.num_programs(1) - 1
    rs_start, rs_finish = _scatter_phases(part_ref, recv_ref, send_sems, recv_sems, local_sem)
    pl.when(jnp.logical_and(grp == 0, qb == 0))(rs_start)

    @pl.when(qb == 0)
    def _():
        dk_ref[...] = jnp.zeros_like(dk_ref)
        dv_ref[...] = jnp.zeros_like(dv_ref)

    acc_ref[...] = jnp.zeros_like(acc_ref)
    stacks = range(STACKS_PER_STEP)
    lanes = [slice(st * STEP_LANES, (st + 1) * STEP_LANES) for st in stacks]
    qs = [_stack_heads(q_ref[:, lanes[st]]) for st in stacks]
    dos = [_stack_heads(do_ref[:, lanes[st]].astype(BF16)) for st in stacks]
    tots = [jnp.concatenate([tot_ref[st * HEADS_PER_STEP + h] for h in range(HEADS_PER_STEP)], axis=0)
            for st in stacks]
    row, col = _key_order()
    up_to = (row <= col).astype(BF16)
    earlier = (row < col).astype(BF16)
    scale = SB_HEAD_DIM ** -0.5
    n = jnp.clip(cnt_ref[grp, qb].astype(jnp.int32), 1, qb + 1)

    def block(kb, cs, ces, diagonal):
        rows = pl.ds(pl.multiple_of(kb * Q_BLOCK, Q_BLOCK), Q_BLOCK)
        new_cs, new_ces = [], []
        for st in stacks:
            k = k_ref[rows, lanes[st]]
            strict, log_beta, log_1m_raw, log_1m = _sb_logs(qs[st], k, diagonal)
            suffix = tots[st] - cs[st] - _key_sums(log_1m, up_to)
            a = _masked(strict, jnp.exp(log_beta + suffix))
            de = _dot_nt(dos[st], v_ref[rows, lanes[st]]) * a
            before = ces[st] + _key_sums(de, earlier)
            dz = _masked(strict, de * jnp.exp(log_1m_raw) - before * jnp.exp(log_beta)).astype(BF16)
            for t in range(TILES_PER_STEP):
                acc_ref[st * TILES_PER_STEP + t] += jnp.dot(
                    dz[t * TILE_ROWS:(t + 1) * TILE_ROWS], k[:, t * TILE_LANES:(t + 1) * TILE_LANES],
                    preferred_element_type=F32)
            dk_ref[rows, lanes[st]] += _dot_tn(dz, qs[st]) * scale
            dv_ref[rows, lanes[st]] += _dot_tn(a, dos[st])
            new_cs.append(cs[st] + jnp.sum(log_1m, axis=1, keepdims=True))
            new_ces.append(ces[st] + jnp.sum(de, axis=1, keepdims=True))
        return tuple(new_cs), tuple(new_ces)

    def step(i, carry):
        return block(qb - n + 1 + i, *carry, False)

    zc = (jnp.zeros((STACK_ROWS, 1), F32),) * STACKS_PER_STEP
    cs, ces = lax.fori_loop(0, n - 1, step, (zc, zc))
    block(qb, cs, ces, True)
    for tile in range(STACKS_PER_STEP * TILES_PER_STEP):
        dq_ref[:, tile * TILE_LANES:(tile + 1) * TILE_LANES] = _unstack_tile(acc_ref[tile]) * scale
    pl.when(jnp.logical_and(grp == last_grp, qb == last_qb))(rs_finish)


def _sb_bwd(cnt, qkv, tot, dout_b, parts):
    s = qkv.shape[0]
    groups = SB_HEADS // HEADS_PER_GROUP
    return pl.pallas_call(
        functools.partial(_sb_bwd_body),
        name="sb_bwd",
        grid=(groups, s // Q_BLOCK),
        in_specs=[pl.BlockSpec(memory_space=pltpu.SMEM),
                  pl.BlockSpec((Q_BLOCK, GROUP_LANES), lambda g, i: (i, g)),
                  pl.BlockSpec((s, GROUP_LANES), lambda g, i: (0, groups + g)),
                  pl.BlockSpec((s, GROUP_LANES), lambda g, i: (0, 2 * groups + g)),
                  pl.BlockSpec((HEADS_PER_GROUP, Q_BLOCK, 1), lambda g, i: (g, i, 0)),
                  pl.BlockSpec((Q_BLOCK, GROUP_LANES), lambda g, i: (i, g)),
                  _ANY],
        out_specs=[pl.BlockSpec((Q_BLOCK, GROUP_LANES), lambda g, i: (i, g)),
                   pl.BlockSpec((s, GROUP_LANES), lambda g, i: (0, g)),
                   pl.BlockSpec((s, GROUP_LANES), lambda g, i: (0, g)),
                   _ANY],
        out_shape=[jax.ShapeDtypeStruct((s, SB_HEADS * SB_HEAD_DIM), F32)] * 3
        + [jax.ShapeDtypeStruct(parts.shape, parts.dtype)],
        scratch_shapes=[pltpu.VMEM((STACKS_PER_STEP * TILES_PER_STEP, TILE_ROWS, TILE_LANES), F32)]
        + list(_AG_SCRATCH),
        compiler_params=pltpu.CompilerParams(
            dimension_semantics=("arbitrary", "arbitrary"), vmem_limit_bytes=VMEM_LIMIT_V7X),
    )(cnt, qkv, qkv, qkv, tot, dout_b, parts)


def _outproj_fwd_body(oa_ref, ob_ref, h_ref, ga_ref, gb_ref, gpost_ref, w_ref,
                      merged_ref, mo_ref, hout_ref):
    half = oa_ref.shape[1]
    ma, _ = _rms(oa_ref[...], ga_ref[...])
    mb, _ = _rms(ob_ref[...], gb_ref[...])
    mab = ma.astype(BF16)
    mbb = mb.astype(BF16)
    merged_ref[:, :half] = mab
    merged_ref[:, half:] = mbb
    mo = (jnp.dot(mab, w_ref[:half, :], preferred_element_type=F32)
          + jnp.dot(mbb, w_ref[half:, :], preferred_element_type=F32))
    mo_ref[...] = mo
    y, _ = _rms(mo, gpost_ref[...])
    hout_ref[...] = h_ref[...] + y


def _outproj_fwd(out_a, out_b, h, ga, gb, gpost, w_out):
    s, d = h.shape
    return _row_call(_outproj_fwd_body, "outproj_fwd", s, [out_a, out_b, h], [ga, gb, gpost, w_out],
                     [(d, BF16), (d, F32), (d, F32)], [])


def _outproj_bwd_body(dh_ref, mo_ref, oa_ref, ob_ref, ga_ref, gb_ref, gpost_ref, w_ref,
                      dmo_ref, doa_ref, dob_ref, dga_ref, dgb_ref, dgpost_ref):
    half = oa_ref.shape[1]
    dmo, dgpost = _rms_bwd(mo_ref[...], gpost_ref[...], dh_ref[...])
    dmob = dmo.astype(BF16)
    dmo_ref[...] = dmob
    dma = _dot_nt(dmob, w_ref[:half, :])
    dmb = _dot_nt(dmob, w_ref[half:, :])
    doa, dga = _rms_bwd(oa_ref[...], ga_ref[...], dma)
    dob, dgb = _rms_bwd(ob_ref[...], gb_ref[...], dmb)
    doa_ref[...] = doa
    dob_ref[...] = dob
    _acc(dga_ref, dga)
    _acc(dgb_ref, dgb)
    _acc(dgpost_ref, dgpost)


def _outproj_bwd(dh, mo, out_a, out_b, ga, gb, gpost, w_out):
    s, d = dh.shape
    half = out_a.shape[1]
    return _row_call(_outproj_bwd_body, "outproj_bwd", s, [dh, mo, out_a, out_b], [ga, gb, gpost, w_out],
                     [(d, BF16), (half, F32), (half, F32)],
                     [((1, half), F32), ((1, half), F32), ((1, d), F32)])


def _kv_fwd_body(mem_ref, g_ref, wt_ref, memn_ref, kv_ref):
    n, _ = _rms(mem_ref[...], g_ref[...])
    nb = n.astype(BF16)
    memn_ref[...] = nb
    kv_ref[...] = _dot_nt(nb, wt_ref[...]).astype(BF16)


def _kv_fwd(mem, g, w_kv_t):
    m, d = mem.shape
    return _row_call(_kv_fwd_body, "kv_fwd", m, [mem], [g, w_kv_t], [(d, BF16), (w_kv_t.shape[0], BF16)], [])


def _kv_bwd_body(dkv_ref, mem_ref, memn_ref, g_ref, wt_ref, dwt_ref, dg_ref):
    dkvb = dkv_ref[...].astype(BF16)
    dwt_ref[...] = _dot_tn(dkvb, memn_ref[...]).astype(BF16)
    dmemn = _dot(dkvb, wt_ref[...])
    _, dg = _rms_bwd(mem_ref[...], g_ref[...], dmemn)
    dg_ref[...] = dg


def _kv_bwd(dkv, mem, memn, g, w_kv_t):
    m, d = mem.shape
    return pl.pallas_call(
        functools.partial(_kv_bwd_body),
        name="kv_bwd",
        out_shape=[jax.ShapeDtypeStruct(w_kv_t.shape, BF16), jax.ShapeDtypeStruct((1, d), F32)],
        compiler_params=pltpu.CompilerParams(vmem_limit_bytes=VMEM_LIMIT_V7X),
    )(dkv, mem, memn, g, w_kv_t)


def _xa_fwd_body(h_ref, gpre_ref, gpost_ref, wq_ref, wo_ref, kv_ref,
                 n_ref, q_ref, o_ref, c_ref, hout_ref):
    h = h_ref[...]
    d = h.shape[1]
    n, _ = _rms(h, gpre_ref[...])
    nb = n.astype(BF16)
    n_ref[...] = nb
    qb = jnp.dot(nb, wq_ref[...], preferred_element_type=F32).astype(BF16)
    q_ref[...] = qb
    for hd in range(XA_HEADS):
        lanes = slice(hd * XA_HEAD_DIM, (hd + 1) * XA_HEAD_DIM)
        k = kv_ref[:, lanes]
        v = kv_ref[:, d + hd * XA_HEAD_DIM:d + (hd + 1) * XA_HEAD_DIM]
        logits = _dot_nt(qb[:, lanes], k) * (XA_HEAD_DIM ** -0.5)
        e = jnp.exp(logits - jnp.max(logits, axis=-1, keepdims=True))
        p = e / jnp.sum(e, axis=-1, keepdims=True)
        o_ref[:, lanes] = jnp.dot(p.astype(BF16), v, preferred_element_type=F32).astype(BF16)
    c = jnp.dot(o_ref[...], wo_ref[...], preferred_element_type=F32)
    c_ref[...] = c
    y, _ = _rms(c, gpost_ref[...])
    hout_ref[...] = h + y


def _xa_fwd(h, gpre, gpost, wq, wo, kv):
    s, d = h.shape
    return _row_call(_xa_fwd_body, "xa_fwd", s, [h], [gpre, gpost, wq, wo, kv],
                     [(d, BF16), (d, BF16), (d, BF16), (d, F32), (d, F32)], [])


def _xa_bwd_body(dh_ref, h_ref, c_ref, q_ref, o_ref, gpre_ref, gpost_ref, wq_ref, wo_ref, kv_ref,
                 dhout_ref, dc_ref, dq_ref, dkv_ref, dgpre_ref, dgpost_ref):
    dh = dh_ref[...]
    d = dh.shape[1]
    scale = XA_HEAD_DIM ** -0.5
    dc, dgpost = _rms_bwd(c_ref[...], gpost_ref[...], dh)
    dcb = dc.astype(BF16)
    dc_ref[...] = dcb
    dob = _dot_nt(dcb, wo_ref[...]).astype(BF16)

    @pl.when(pl.program_id(0) == 0)
    def _():
        dkv_ref[...] = jnp.zeros_like(dkv_ref)

    for hd in range(XA_HEADS):
        lanes = slice(hd * XA_HEAD_DIM, (hd + 1) * XA_HEAD_DIM)
        vlanes = slice(d + hd * XA_HEAD_DIM, d + (hd + 1) * XA_HEAD_DIM)
        qh = q_ref[:, lanes]
        k = kv_ref[:, lanes]
        v = kv_ref[:, vlanes]
        logits = _dot_nt(qh, k) * scale
        e = jnp.exp(logits - jnp.max(logits, axis=-1, keepdims=True))
        p = e / jnp.sum(e, axis=-1, keepdims=True)
        doh = dob[:, lanes]
        dp = _dot_nt(doh, v)
        dl = (p * (dp - jnp.sum(dp * p, axis=-1, keepdims=True)) * scale).astype(BF16)
        dq_ref[:, lanes] = jnp.dot(dl, k, preferred_element_type=F32).astype(BF16)
        dkv_ref[:, lanes] += _dot_tn(dl, qh)
        dkv_ref[:, vlanes] += _dot_tn(p, doh)
    dn = _dot_nt(dq_ref[...], wq_ref[...])
    dhn, dgpre = _rms_bwd(h_ref[...], gpre_ref[...], dn)
    dhout_ref[...] = dh + dhn
    _acc(dgpre_ref, dgpre)
    _acc(dgpost_ref, dgpost)


def _xa_bwd(dh, h, c, q, o, gpre, gpost, wq, wo, kv, scatter=None):
    s, d = h.shape
    return _row_call(_xa_bwd_body, "xa_bwd", s, [dh, h, c, q, o], [gpre, gpost, wq, wo, kv],
                     [(d, F32), (d, BF16), (d, BF16)],
                     [(kv.shape, F32), ((1, d), F32), ((1, d), F32)], scatter=scatter)


def _largest_tile(n, cap):
    best = 128
    for t in range(128, cap + 1, 128):
        if n % t == 0:
            best = t
    return best


def _mm_tn(a, bs, name, gather=None, chip_exchange=None):
    s, k = a.shape
    n = bs[0].shape[1]
    nb = len(bs)
    ts = min(2048, s)
    tk = _largest_tile(k, 1536)
    tn = _largest_tile(n, 1536 // nb)

    steps = s // ts
    grid = (k // tk, n // tn, steps)
    sent = gather if gather is not None else chip_exchange
    hosted = sent is not None
    phases = _gather_phases if gather is not None else _chip_exchange_phases
    landed_shape = None
    if hosted:
        landed_shape = (N_DEV,) + sent.shape if gather is not None else (3,) + sent.shape[1:]

    def body(a_ref, *refs):
        if hosted:
            sent_ref, landed_ref, refs = refs[nb], refs[2 * nb + 1], refs[:nb] + refs[nb + 1:2 * nb + 1] + refs[2 * nb + 2:]
            start, *rest = phases(sent_ref, landed_ref, *refs[-3:])
            place = (pl.program_id(0) * grid[1] + pl.program_id(1)) * grid[2] + pl.program_id(2)
            pl.when(place == 0)(start)
        b_refs, o_refs, acc_refs = refs[:nb], refs[nb:2 * nb], refs[2 * nb:3 * nb]
        at = a_ref[...]
        t = pl.program_id(2)

        @pl.when(t == 0)
        def _():
            for acc_ref in acc_refs:
                acc_ref[...] = jnp.zeros_like(acc_ref)

        for b_ref, acc_ref in zip(b_refs, acc_refs):
            acc_ref[...] += _dot_tn(at, b_ref[...])

        @pl.when(t == steps - 1)
        def _():
            for o_ref, acc_ref in zip(o_refs, acc_refs):
                o_ref[...] = acc_ref[...].astype(BF16)

        if hosted:
            total = grid[0] * grid[1] * grid[2]

            @pl.when(place == total - 1)
            def _():
                for phase in rest:
                    phase()

    return pl.pallas_call(
        body,
        name=name,
        grid=grid,
        in_specs=[pl.BlockSpec((ts, tk), lambda i, j, t: (t, i))]
        + [pl.BlockSpec((ts, tn), lambda i, j, t: (t, j))] * nb + ([_ANY] if hosted else []),
        out_specs=[pl.BlockSpec((tk, tn), lambda i, j, t: (i, j))] * nb + ([_ANY] if hosted else []),
        out_shape=[jax.ShapeDtypeStruct((k, n), BF16)] * nb
        + ([jax.ShapeDtypeStruct(landed_shape, sent.dtype)] if hosted else []),
        scratch_shapes=[pltpu.VMEM((tk, tn), F32)] * nb + (list(_AG_SCRATCH) if hosted else []),
        compiler_params=pltpu.CompilerParams(
            dimension_semantics=("arbitrary", "arbitrary", "arbitrary"),
            vmem_limit_bytes=VMEM_LIMIT_V7X),
    )(a, *bs, *([sent] if hosted else []))


_SMALL_SHAPES = {
    "sgu_norm_g": (1, SGU_GROUPS * GROUP_DIM),
    "sgu_norm_b": (1, SGU_GROUPS * GROUP_DIM),
    "sgu_w_s": (SGU_GROUPS, CHUNK, CHUNK),
    "sgu_b_s": (SGU_GROUPS, CHUNK, 1),
}


def _small_views(small):
    return {n: v.reshape(_SMALL_SHAPES.get(n, v.shape)) for n, v in small.items()}


def _small_unviews(views, like):
    return {n: v.reshape(like[n].shape) for n, v in views.items()}


def _unpack_rows(gathered, names, shard_rows):
    out, off = {}, 0
    for n in names:
        rows = shard_rows[n]
        out[n] = gathered[:, off:off + rows, :].reshape(N_DEV * rows, gathered.shape[2])
        off += rows
    return out


def _row_tile(r, cap):
    best = 16
    for t in range(16, cap + 1, 16):
        if r % t == 0:
            best = t
    return best


def _sum_received(received, name):
    _, r, c = received.shape
    tr = _row_tile(r, 1024)

    def body(rc_ref, g_ref):
        g = rc_ref[0].astype(F32)
        for t in range(1, N_DEV):
            g = g + rc_ref[t].astype(F32)
        g_ref[...] = g

    return pl.pallas_call(
        body, name=name, grid=(r // tr,),
        in_specs=[pl.BlockSpec((N_DEV, tr, c), lambda i: (0, i, 0))],
        out_specs=pl.BlockSpec((tr, c), lambda i: (i, 0)),
        out_shape=jax.ShapeDtypeStruct((r, c), F32),
    )(received)


def _local_step(x, mem, target, small, big, shards, shard_rows):
    sm, w = small, dict(big)
    d_model = x.shape[1]

    def parts(names):
        return jnp.concatenate([gw.pop(n).reshape(N_DEV, -1, d_model) for n in names], axis=1)

    h1, n1, a1, b1, f1, landed = _ffn_fwd(
        x, sm["ffn1_pre_g"], sm["ffn1_post_g"], w["ffn1_w_gate"], w["ffn1_w_up"], w["ffn1_w_down"],
        "ffn1_fwd", gather=shards["ffn1_fwd"])
    w.update(_unpack_rows(landed, GATHER_IN["ffn1_fwd"], shard_rows))
    n2, uv_pre, qkv = _inproj_fwd(h1, sm["mix_pre_g"], w["w_in"])
    out_a = _sgu_fwd(uv_pre, sm["sgu_norm_g"], sm["sgu_norm_b"], sm["sgu_w_s"], sm["sgu_b_s"])
    out_b, tot, cnt, landed = _sb_fwd(qkv, shards["sb_fwd"])
    w.update(_unpack_rows(landed, GATHER_IN["sb_fwd"], shard_rows))
    merged, mo, h2 = _outproj_fwd(out_a, out_b, h1, sm["sgu_out_g"], sm["sb_out_g"],
                                  sm["mix_post_g"], w["w_out"])
    memn, kv = _kv_fwd(mem, sm["mem_norm_g"], w["xa_w_kv"])
    n3, qx, ox, cx, h3 = _xa_fwd(h2, sm["xa_pre_g"], sm["xa_post_g"], w["xa_w_q"], w["xa_w_o"], kv)
    n4, a2, b2, f2, dh4, loss, dg_final = _ffn_loss_fwd(
        h3, target, sm["ffn2_pre_g"], sm["ffn2_post_g"], sm["final_norm_g"],
        w["ffn2_w_gate"], w["ffn2_w_up"], w["ffn2_w_down"], "ffn2_fwd")

    gs, gw = {"final_norm_g": dg_final}, {}
    dh3, da2, db2, hm2, df2, gs["ffn2_pre_g"], gs["ffn2_post_g"] = _ffn_bwd(
        dh4, h3, a2, b2, f2, sm["ffn2_pre_g"], sm["ffn2_post_g"],
        w["ffn2_w_gate"], w["ffn2_w_up"], w["ffn2_w_down"], "ffn2_bwd")
    gw["ffn2_w_gate"], = _mm_tn(da2, [n4], "ffn2_dw_gate")
    gw["ffn2_w_up"], = _mm_tn(db2, [n4], "ffn2_dw_up")
    gw["ffn2_w_down"], = _mm_tn(hm2, [df2], "ffn2_dw_down")

    received = {}
    dh2, dc, dqx, dkv, gs["xa_pre_g"], gs["xa_post_g"], received["xa_bwd"] = _xa_bwd(
        dh3, h2, cx, qx, ox, sm["xa_pre_g"], sm["xa_post_g"], w["xa_w_q"], w["xa_w_o"], kv,
        scatter=parts(SCATTER_IN["xa_bwd"]))
    gw["xa_w_o"], = _mm_tn(ox, [dc], "xa_dw_o")
    gw["xa_w_q"], = _mm_tn(n3, [dqx], "xa_dw_q")
    gw["xa_w_kv"], gs["mem_norm_g"] = _kv_bwd(dkv, mem, memn, sm["mem_norm_g"], w["xa_w_kv"])

    dmo, dout_a, dout_b, gs["sgu_out_g"], gs["sb_out_g"], gs["mix_post_g"] = _outproj_bwd(
        dh2, mo, out_a, out_b, sm["sgu_out_g"], sm["sb_out_g"], sm["mix_post_g"], w["w_out"])
    gw["w_out"], = _mm_tn(merged, [dmo], "mix_dw_out")
    dq, dk, dv, received["sb_bwd"] = _sb_bwd(cnt, qkv, tot, dout_b, parts(SCATTER_IN["sb_bwd"]))
    duv, gs["sgu_w_s"], gs["sgu_b_s"], gs["sgu_norm_g"], gs["sgu_norm_b"], received["sgu_bwd"] = _sgu_bwd(
        uv_pre, dout_a, sm["sgu_norm_g"], sm["sgu_norm_b"], sm["sgu_w_s"], sm["sgu_b_s"],
        scatter=parts(SCATTER_IN["sgu_bwd"]))
    dproj = jnp.concatenate([duv] + [t.astype(BF16) for t in (dq, dk, dv)], axis=1)
    dh1, gs["mix_pre_g"], received["inproj_bwd"] = _inproj_bwd(
        dh2, dproj, h1, sm["mix_pre_g"], w["w_in"], scatter=parts(SCATTER_IN["inproj_bwd"]))
    gw["w_in"], = _mm_tn(dproj, [n2], "mix_dw_in")

    dx, da1, db1, hm1, df1, gs["ffn1_pre_g"], gs["ffn1_post_g"], received["ffn1_bwd"] = _ffn_bwd(
        dh1, x, a1, b1, f1, sm["ffn1_pre_g"], sm["ffn1_post_g"],
        w["ffn1_w_gate"], w["ffn1_w_up"], w["ffn1_w_down"], "ffn1_bwd",
        scatter=parts(SCATTER_IN["ffn1_bwd"]))
    summed = {host: _sum_received(r, "rs_sum_" + host) for host, r in received.items()}

    mx, my, mc = _mesh_place()
    place = jnp.stack([mc, 2 * mx + my]).astype(jnp.int32)

    def pair_stage(dw, tag):
        p4 = dw.reshape(N_DEV // 2, 2, -1, d_model)
        recv_a = _pair_exchange(p4, "rs_pair_exchange_" + tag)
        return p4, recv_a, _pair_sum(place, p4, recv_a, "rs_pair_sum_" + tag)

    dw, small_grads = _mm_tn(da1, [n1], "ffn1_dw_gate", gather=_pack_small(gs))
    gate = pair_stage(dw, "gate")
    dw, gate_b = _mm_tn(db1, [n1], "ffn1_dw_up", chip_exchange=gate[2])
    up = pair_stage(dw, "up")
    dw, up_b = _mm_tn(hm1, [df1], "ffn1_dw_down", chip_exchange=up[2])
    down = pair_stage(dw, "down")
    down_b = _chip_exchange(down[2], "rs_chip_exchange_down")
    for n, (p4, recv_a, _), recv_b in (("ffn1_w_gate", gate, gate_b), ("ffn1_w_up", up, up_b),
                                       ("ffn1_w_down", down, down_b)):
        summed[n] = _rs_final(place, p4, recv_a, recv_b, "rs_final_" + n)
    return loss, dx, small_grads, summed


def _pair_exchange(p4, name):
    nchip, _, r, c = p4.shape

    def body(p_ref, out_ref, send_sems, recv_sems):
        mx, my, mc = _mesh_place()
        copies = [pltpu.make_async_remote_copy(
            src_ref=p_ref.at[j, 1 - mc], dst_ref=out_ref.at[j],
            send_sem=send_sems.at[j], recv_sem=recv_sems.at[j],
            device_id=(mx, my, 1 - mc), device_id_type=MESH) for j in range(nchip)]
        for cp in copies:
            cp.start()
        for cp in copies:
            cp.wait()

    return pl.pallas_call(
        body,
        name=name,
        out_shape=jax.ShapeDtypeStruct((nchip, r, c), p4.dtype),
        in_specs=[_ANY],
        out_specs=_ANY,
        scratch_shapes=[pltpu.SemaphoreType.DMA((nchip,)), pltpu.SemaphoreType.DMA((nchip,))],
    )(p4)


def _chip_exchange(q, name):
    _, r, c = q.shape

    def body(q_ref, out_ref, send_sems, recv_sems, local_sem):
        for phase in _chip_exchange_phases(q_ref, out_ref, send_sems, recv_sems, local_sem):
            phase()

    return pl.pallas_call(
        body,
        name=name,
        out_shape=jax.ShapeDtypeStruct((3, r, c), q.dtype),
        in_specs=[_ANY],
        out_specs=_ANY,
        scratch_shapes=list(_AG_SCRATCH),
    )(q)


def _rs_row_tile(r):
    return _row_tile(r, 1024)


def _pair_sum(place, p4, recv_a, name):
    nchip, _, r, c = p4.shape
    tr = _rs_row_tile(r)

    def body(place_ref, p_ref, a_ref, q_ref):
        q_ref[0] = (p_ref[0, 0].astype(F32) + a_ref[0].astype(F32)).astype(BF16)

    return pl.pallas_call(
        body,
        name=name,
        grid_spec=pltpu.PrefetchScalarGridSpec(
            num_scalar_prefetch=1,
            grid=(nchip, r // tr),
            in_specs=[pl.BlockSpec((1, 1, tr, c), lambda j, i, pref: (j, pref[0], i, 0)),
                      pl.BlockSpec((1, tr, c), lambda j, i, pref: (j, i, 0))],
            out_specs=pl.BlockSpec((1, tr, c), lambda j, i, pref: (j, i, 0)),
        ),
        out_shape=jax.ShapeDtypeStruct((nchip, r, c), BF16),
    )(place, p4, recv_a)


def _rs_final(place, p4, recv_a, recv_b, name):
    _, _, r, c = p4.shape
    tr = _rs_row_tile(r)

    def body(place_ref, p_ref, a_ref, b_ref, g_ref):
        g = p_ref[0, 0].astype(F32) + a_ref[0].astype(F32)
        for k in range(3):
            g = g + b_ref[k].astype(F32)
        g_ref[...] = g

    return pl.pallas_call(
        body,
        name=name,
        grid_spec=pltpu.PrefetchScalarGridSpec(
            num_scalar_prefetch=1,
            grid=(r // tr,),
            in_specs=[pl.BlockSpec((1, 1, tr, c), lambda i, pref: (pref[1], pref[0], i, 0)),
                      pl.BlockSpec((1, tr, c), lambda i, pref: (pref[1], i, 0)),
                      pl.BlockSpec((3, tr, c), lambda i, pref: (0, i, 0))],
            out_specs=pl.BlockSpec((tr, c), lambda i, pref: (i, 0)),
        ),
        out_shape=jax.ShapeDtypeStruct((r, c), F32),
    )(place, p4, recv_a, recv_b)


def _adamw_math(w, g, m, v):
    m = ADAM_B1 * m + (1.0 - ADAM_B1) * g
    v = ADAM_B2 * v + (1.0 - ADAM_B2) * (g * g)
    m_hat = m / (1.0 - ADAM_B1 ** ADAM_STEP)
    v_hat = v / (1.0 - ADAM_B2 ** ADAM_STEP)
    delta = -ADAM_LR * (m_hat / (jnp.sqrt(v_hat) + ADAM_EPS) + ADAM_WD * w)
    return delta, m, v


def _adamw(w, g, m, v, name):
    r, c = w.shape
    tr = r if r <= 512 else 256

    def body(w_ref, g_ref, m_ref, v_ref, d_ref, mo_ref, vo_ref):
        d_ref[...], mo_ref[...], vo_ref[...] = _adamw_math(w_ref[...], g_ref[...], m_ref[...], v_ref[...])

    spec = pl.BlockSpec((tr, c), lambda i: (i, 0))
    out = jax.ShapeDtypeStruct((r, c), F32)
    return pl.pallas_call(
        body, name=name, grid=(r // tr,), in_specs=[spec] * 4, out_specs=[spec] * 3,
        out_shape=[out] * 3,
    )(w, g, m, v)


def _small_sum_adamw(gathered, ws, ms, vs):
    _, r, c = gathered.shape
    count = len(ws)

    def body(ga_ref, *refs):
        w_refs, m_refs, v_refs = refs[:count], refs[count:2 * count], refs[2 * count:3 * count]
        out_refs = refs[3 * count:]
        for o_ref in out_refs:
            o_ref[...] = jnp.zeros_like(o_ref)
        off = 0
        for w_ref, m_ref, v_ref in zip(w_refs, m_refs, v_refs):
            rows = pl.ds(off, w_ref.shape[0])
            g = ga_ref[0, rows, :]
            for k in range(1, N_DEV):
                g = g + ga_ref[k, rows, :]
            results = (g, *_adamw_math(w_ref[...], g, m_ref[...], v_ref[...]))
            for o_ref, val in zip(out_refs, results):
                o_ref[rows, :] = val
            off += w_ref.shape[0] + (-w_ref.shape[0]) % SMALL_ROW_ALIGN

    out = jax.ShapeDtypeStruct((r, c), F32)
    return pl.pallas_call(body, name="small_sum_adamw", out_shape=[out] * 4)(gathered, *ws, *ms, *vs)


_WEIGHTS = ["ffn1_pre_g", "ffn1_post_g", "ffn1_w_gate", "ffn1_w_up", "ffn1_w_down", "mix_pre_g",
            "mix_post_g", "w_in", "sgu_norm_g", "sgu_norm_b", "sgu_w_s", "sgu_b_s", "sgu_out_g",
            "sb_out_g", "w_out", "xa_pre_g", "xa_post_g", "mem_norm_g", "xa_w_q", "xa_w_kv", "xa_w_o",
            "ffn2_pre_g", "ffn2_post_g", "ffn2_w_gate", "ffn2_w_up", "ffn2_w_down", "final_norm_g"]
_BIG = ["ffn1_w_gate", "ffn1_w_up", "ffn1_w_down", "w_in", "w_out", "xa_w_q", "xa_w_kv", "xa_w_o",
        "ffn2_w_gate", "ffn2_w_up", "ffn2_w_down"]
_COL_SHARDED = ("ffn1_w_gate", "ffn1_w_up", "w_in", "xa_w_kv", "ffn2_w_gate", "ffn2_w_up")
_EARLY = ["ffn1_w_gate", "ffn1_w_up", "ffn1_w_down"]
GATHER_IN = {"ffn1_fwd": ["w_in", "ffn2_w_gate"],
             "sb_fwd": ["w_out", "xa_w_q", "xa_w_kv", "xa_w_o", "ffn2_w_up", "ffn2_w_down"]}
SCATTER_IN = {"xa_bwd": ["ffn2_w_gate"],
              "sb_bwd": ["ffn2_w_up", "ffn2_w_down", "xa_w_o", "w_out"],
              "sgu_bwd": ["xa_w_kv"],
              "inproj_bwd": ["xa_w_q"],
              "ffn1_bwd": ["w_in"]}
_SMALL = [n for n in _WEIGHTS if n not in _BIG]
SMALL_LANES = 128
SMALL_ROW_ALIGN = 8


def _pack_small(tensors):
    parts = []
    for n in _SMALL:
        t = tensors[n].reshape(-1, SMALL_LANES)
        pad = (-t.shape[0]) % SMALL_ROW_ALIGN
        parts.append(jnp.pad(t, ((0, pad), (0, 0))) if pad else t)
    return jnp.concatenate(parts, axis=0)


def _unpack_small(packed, like):
    out, off = {}, 0
    for n in _SMALL:
        size = like[n].size
        rows = size // SMALL_LANES
        out[n] = packed[off:off + rows].reshape(like[n].shape)
        off += rows + (-rows) % SMALL_ROW_ALIGN
    return out


def kernel(x, mem, ffn1_pre_g, ffn1_post_g, ffn1_w_gate, ffn1_w_up, ffn1_w_down, mix_pre_g, mix_post_g, w_in, sgu_norm_g, sgu_norm_b, sgu_w_s, sgu_b_s, sgu_out_g, sb_out_g, w_out, xa_pre_g, xa_post_g, mem_norm_g, xa_w_q, xa_w_kv, xa_w_o, ffn2_pre_g, ffn2_post_g, ffn2_w_gate, ffn2_w_up, ffn2_w_down, final_norm_g, loss_target, m_ffn1_pre_g, m_ffn1_post_g, m_ffn1_w_gate, m_ffn1_w_up, m_ffn1_w_down, m_mix_pre_g, m_mix_post_g, m_w_in, m_sgu_norm_g, m_sgu_norm_b, m_sgu_w_s, m_sgu_b_s, m_sgu_out_g, m_sb_out_g, m_w_out, m_xa_pre_g, m_xa_post_g, m_mem_norm_g, m_xa_w_q, m_xa_w_kv, m_xa_w_o, m_ffn2_pre_g, m_ffn2_post_g, m_ffn2_w_gate, m_ffn2_w_up, m_ffn2_w_down, m_final_norm_g, v_ffn1_pre_g, v_ffn1_post_g, v_ffn1_w_gate, v_ffn1_w_up, v_ffn1_w_down, v_mix_pre_g, v_mix_post_g, v_w_in, v_sgu_norm_g, v_sgu_norm_b, v_sgu_w_s, v_sgu_b_s, v_sgu_out_g, v_sb_out_g, v_w_out, v_xa_pre_g, v_xa_post_g, v_mem_norm_g, v_xa_w_q, v_xa_w_kv, v_xa_w_o, v_ffn2_pre_g, v_ffn2_post_g, v_ffn2_w_gate, v_ffn2_w_up, v_ffn2_w_down, v_final_norm_g):
    vals = dict(locals())
    d_model = x.shape[-1]

    def packed(names):
        return jnp.concatenate(
            [(vals[n][0].T if n in _COL_SHARDED else vals[n][0]).astype(BF16) for n in names], axis=0)

    shard_rows = {n: vals[n].shape[2 if n in _COL_SHARDED else 1] for n in _BIG}
    big = _unpack_rows(_all_gather(packed(_EARLY), "ag_weights"), _EARLY, shard_rows)

    small = {n: vals[n] for n in _SMALL}
    loss_part, dx, gathered_small, summed = _local_step(
        x[0], mem[0], loss_target[0], _small_views(small), big,
        {host: packed(names) for host, names in GATHER_IN.items()}, shard_rows)
    loss = lax.psum(loss_part[0, 0], ("x", "y", "c"))

    grads, deltas, new_m, new_v = {}, {}, {}, {}
    for names, g_rows in [([n], summed[n]) for n in _EARLY] + [(SCATTER_IN[h], summed[h]) for h in SCATTER_IN]:
        off = 0
        for n in names:
            rows = shard_rows[n]
            g = g_rows[off:off + rows]
            off += rows
            state = [vals[n][0], vals["m_" + n][0], vals["v_" + n][0]]
            flipped = n in _COL_SHARDED and rows % SMALL_LANES != 0
            if flipped:
                state = [t.T for t in state]
            elif n in _COL_SHARDED:
                g = g.T
            outs = (g, *_adamw(state[0], g, state[1], state[2], "adamw_" + n))
            if flipped:
                outs = tuple(t.T for t in outs)
            grads[n], deltas[n], new_m[n], new_v[n] = (t[None] for t in outs)

    outs = _small_sum_adamw(gathered_small,
                            *([vals[pre + n].reshape(-1, SMALL_LANES) for n in _SMALL] for pre in ("", "m_", "v_")))
    for dst, packed in zip((grads, deltas, new_m, new_v), outs):
        dst.update(_unpack_small(packed, small))

    return (loss, dx[None], *[grads[n] for n in _WEIGHTS], *[deltas[n] for n in _WEIGHTS],
            *[new_m[n] for n in _WEIGHTS], *[new_v[n] for n in _WEIGHTS])
```

```python
import functools

import jax
import jax.numpy as jnp
from jax import lax
from jax.experimental import pallas as pl
from jax.experimental.pallas import tpu as pltpu

F32 = jnp.float32
BF16 = jnp.bfloat16
EPS = 1e-6
MESH = pl.DeviceIdType.MESH
N_DEV = 8

SGU_GROUPS = 4
GROUP_DIM = 128
CHUNK = 128
SB_HEADS = 8
SB_HEAD_DIM = 64
Q_BLOCK = 128
XA_HEADS = 4
XA_HEAD_DIM = 256

ADAM_LR = 0.001
ADAM_B1 = 0.9
ADAM_B2 = 0.999
ADAM_EPS = 1e-08
ADAM_WD = 0.01
ADAM_STEP = 10

VMEM_LIMIT_V7X = 56 * 1024 * 1024
GELU_C0 = 0.7978845608028654
GELU_C1 = 0.044715


def _dot(a, b):
    return jnp.dot(a.astype(BF16), b.astype(BF16), preferred_element_type=F32)


def _dot_nt(a, b):
    return lax.dot_general(a.astype(BF16), b.astype(BF16), (((1,), (1,)), ((), ())),
                           preferred_element_type=F32)


def _dot_tn(a, b):
    return lax.dot_general(a.astype(BF16), b.astype(BF16), (((0,), (0,)), ((), ())),
                           preferred_element_type=F32)


def _rms(x, g):
    r = lax.rsqrt(jnp.mean(x * x, axis=-1, keepdims=True) + EPS)
    return x * r * g, r


def _rms_bwd(x, g, dy):
    r = lax.rsqrt(jnp.mean(x * x, axis=-1, keepdims=True) + EPS)
    xh = x * r
    gy = dy * g
    dx = r * (gy - xh * jnp.mean(gy * xh, axis=-1, keepdims=True))
    dg = jnp.sum(dy * xh, axis=0, keepdims=True)
    return dx, dg


def _sigmoid(x):
    return jax.nn.sigmoid(x)


def _gelu(x):
    t = jnp.tanh(GELU_C0 * (x + GELU_C1 * x * x * x))
    return 0.5 * x * (1.0 + t)


def _gelu_grad(x):
    t = jnp.tanh(GELU_C0 * (x + GELU_C1 * x * x * x))
    return 0.5 * (1.0 + t) + 0.5 * x * (1.0 - t * t) * GELU_C0 * (1.0 + 3.0 * GELU_C1 * x * x)


def _split_bf16(x):
    hi = x.astype(BF16)
    lo = (x - hi.astype(F32)).astype(BF16)
    return hi, lo


def _row_spec(tm, cols):
    return pl.BlockSpec((tm, cols), lambda i: (i, 0))


def _full_spec(shape, buffers=None):
    nd = len(shape)
    mode = None if buffers is None else pl.Buffered(buffers)
    return pl.BlockSpec(tuple(shape), lambda i: (0,) * nd, pipeline_mode=mode)


FFN_TOKEN_TILE = 512


def _token_tile(s):
    return min(256, s)


def _row_call(body, name, s, tiled_in, full_in, tiled_out, acc_out, gather=None, scatter=None, tm=None):
    tm = _token_tile(s) if tm is None else min(tm, s)
    steps = s // tm
    in_specs = [_row_spec(tm, a.shape[1]) for a in tiled_in] + [_full_spec(a.shape, buffers=1) for a in full_in]
    out_specs = [_row_spec(tm, c) for c, _ in tiled_out] + [_full_spec(sh) for sh, _ in acc_out]
    out_shape = [jax.ShapeDtypeStruct((s, c), dt) for c, dt in tiled_out]
    out_shape += [jax.ShapeDtypeStruct(sh, dt) for sh, dt in acc_out]
    operands = [*tiled_in, *full_in]
    scratch = []
    kernel_body = functools.partial(body)
    sent = gather if gather is not None else scatter
    if sent is not None:
        n_in, n_out = len(operands), len(out_shape)
        out_shape.append(jax.ShapeDtypeStruct(
            (N_DEV,) + sent.shape if gather is not None else sent.shape, sent.dtype))
        operands.append(sent)
        in_specs.append(_ANY)
        out_specs.append(_ANY)
        scratch = list(_AG_SCRATCH)

        def kernel_body(*refs):
            ins, sent_ref = refs[:n_in], refs[n_in]
            outs, landed_ref = refs[n_in + 1:n_in + 1 + n_out], refs[n_in + 1 + n_out]
            step = pl.program_id(0)
            if gather is not None:
                start, relay, forward, finish = _gather_phases(sent_ref, landed_ref, *refs[-3:])
            else:
                start, finish = _scatter_phases(sent_ref, landed_ref, *refs[-3:])
            pl.when(step == 0)(start)
            body(*ins, *outs)
            if gather is not None:
                pl.when(step == steps // 3)(relay)
                pl.when(step == (2 * steps) // 3)(forward)
            pl.when(step == steps - 1)(finish)

    return pl.pallas_call(
        kernel_body,
        name=name,
        grid=(steps,),
        in_specs=in_specs,
        out_specs=out_specs,
        out_shape=out_shape,
        scratch_shapes=scratch,
        compiler_params=pltpu.CompilerParams(
            dimension_semantics=("arbitrary",), vmem_limit_bytes=VMEM_LIMIT_V7X),
    )(*operands)


def _acc(ref, val):
    @pl.when(pl.program_id(0) == 0)
    def _():
        ref[...] = val

    @pl.when(pl.program_id(0) != 0)
    def _():
        ref[...] += val


def _ffn_fwd_tile(x_ref, pre_ref, post_ref, wgt_ref, wut_ref, wd_ref, n_ref, a_ref, b_ref, f_ref):
    x = x_ref[...]
    n, _ = _rms(x, pre_ref[...])
    nb = n.astype(BF16)
    n_ref[...] = nb
    a = _dot_nt(nb, wgt_ref[...])
    b = _dot_nt(nb, wut_ref[...])
    a_ref[...] = a.astype(BF16)
    b_ref[...] = b.astype(BF16)
    hmid = a * _sigmoid(a) * b
    f = jnp.dot(hmid.astype(BF16), wd_ref[...], preferred_element_type=F32)
    f_ref[...] = f
    y, _ = _rms(f, post_ref[...])
    return x + 0.5 * y


def _ffn_fwd_body(x_ref, pre_ref, post_ref, wgt_ref, wut_ref, wd_ref,
                  h_ref, n_ref, a_ref, b_ref, f_ref):
    h_ref[...] = _ffn_fwd_tile(x_ref, pre_ref, post_ref, wgt_ref, wut_ref, wd_ref, n_ref, a_ref, b_ref, f_ref)


def _ffn_loss_body(x_ref, t_ref, pre_ref, post_ref, gfin_ref, wgt_ref, wut_ref, wd_ref,
                   n_ref, a_ref, b_ref, f_ref, dh_ref, loss_ref, dg_ref):
    h = _ffn_fwd_tile(x_ref, pre_ref, post_ref, wgt_ref, wut_ref, wd_ref, n_ref, a_ref, b_ref, f_ref)
    d = h.shape[1]
    y, _ = _rms(h, gfin_ref[...])
    err = y - t_ref[...]
    part = (0.5 / d) * jnp.sum(jnp.sum(err * err, axis=1, keepdims=True), axis=0, keepdims=True)
    dh, dg = _rms_bwd(h, gfin_ref[...], err * (1.0 / d))
    dh_ref[...] = dh
    _acc(loss_ref, part)
    _acc(dg_ref, dg)


def _ffn_fwd(x, pre_g, post_g, wgt, wut, wd, name, gather=None):
    s, d = x.shape
    f = wgt.shape[0]
    return _row_call(_ffn_fwd_body, name, s, [x], [pre_g, post_g, wgt, wut, wd],
                     [(d, F32), (d, BF16), (f, BF16), (f, BF16), (d, F32)], [], gather=gather, tm=FFN_TOKEN_TILE)


def _ffn_loss_fwd(x, target, pre_g, post_g, final_g, wgt, wut, wd, name):
    s, d = x.shape
    f = wgt.shape[0]
    return _row_call(_ffn_loss_body, name, s, [x, target], [pre_g, post_g, final_g, wgt, wut, wd],
                     [(d, BF16), (f, BF16), (f, BF16), (d, F32), (d, F32)],
                     [((1, 1), F32), ((1, d), F32)], tm=FFN_TOKEN_TILE)


def _ffn_bwd_body(dh_ref, x_ref, a_ref, b_ref, f_ref, pre_ref, post_ref, wgt_ref, wut_ref, wd_ref,
                  dx_ref, da_ref, db_ref, hm_ref, df_ref, dpre_ref, dpost_ref):
    dh = dh_ref[...]
    df, dpost = _rms_bwd(f_ref[...], post_ref[...], 0.5 * dh)
    dfb = df.astype(BF16)
    df_ref[...] = dfb
    dhmid = _dot_nt(dfb, wd_ref[...])
    a = a_ref[...].astype(F32)
    b = b_ref[...].astype(F32)
    sig = _sigmoid(a)
    sa = a * sig
    hm_ref[...] = (sa * b).astype(BF16)
    dab = (dhmid * b * sig * (1.0 + a * (1.0 - sig))).astype(BF16)
    dbb = (dhmid * sa).astype(BF16)
    da_ref[...] = dab
    db_ref[...] = dbb
    dn = _dot(dab, wgt_ref[...]) + _dot(dbb, wut_ref[...])
    dxn, dpre = _rms_bwd(x_ref[...], pre_ref[...], dn)
    dx_ref[...] = dh + dxn
    _acc(dpre_ref, dpre)
    _acc(dpost_ref, dpost)


def _ffn_bwd(dh, x, a, b, f, pre_g, post_g, wgt, wut, wd, name, scatter=None):
    s, d = x.shape
    ff = wgt.shape[0]
    return _row_call(_ffn_bwd_body, name, s, [dh, x, a, b, f], [pre_g, post_g, wgt, wut, wd],
                     [(d, F32), (ff, BF16), (ff, BF16), (ff, BF16), (d, BF16)],
                     [((1, d), F32), ((1, d), F32)], scatter=scatter)


def _inproj_fwd_body(h_ref, g_ref, wt_ref, n_ref, uv_ref, qkv_ref):
    n, _ = _rms(h_ref[...], g_ref[...])
    nb = n.astype(BF16)
    n_ref[...] = nb
    proj = _dot_nt(nb, wt_ref[...])
    nuv = uv_ref.shape[1]
    uv_ref[...] = proj[:, :nuv]
    qkv_ref[...] = proj[:, nuv:].astype(BF16)


def _inproj_fwd(h, g, w_in_t):
    s, d = h.shape
    sgu_w = SGU_GROUPS * GROUP_DIM
    sb_w = SB_HEADS * SB_HEAD_DIM
    return _row_call(_inproj_fwd_body, "inproj_fwd", s, [h], [g, w_in_t],
                     [(d, BF16), (2 * sgu_w, F32), (3 * sb_w, BF16)], [])


def _inproj_bwd_body(dh_ref, dproj_ref, h_ref, g_ref, wt_ref, dhout_ref, dg_ref):
    dn = _dot(dproj_ref[...], wt_ref[...])
    dhn, dg = _rms_bwd(h_ref[...], g_ref[...], dn)
    dhout_ref[...] = dh_ref[...] + dhn
    _acc(dg_ref, dg)


def _inproj_bwd(dh, dproj, h, g, w_in_t, scatter=None):
    s, d = h.shape
    return _row_call(_inproj_bwd_body, "inproj_bwd", s, [dh, dproj, h], [g, w_in_t],
                     [(d, F32)], [((1, d), F32)], scatter=scatter)


def _causal_w(ws_ref, g):
    row = lax.broadcasted_iota(jnp.int32, (CHUNK, CHUNK), 0)
    col = lax.broadcasted_iota(jnp.int32, (CHUNK, CHUNK), 1)
    return jnp.where(row >= col, ws_ref[g], 0.0), row >= col


def _group_norm(v):
    mu = jnp.mean(v, axis=-1, keepdims=True)
    d = v - mu
    rstd = lax.rsqrt(jnp.mean(d * d, axis=-1, keepdims=True) + EPS)
    return d * rstd, rstd


def _sgu_fwd_body(uv_ref, ng_ref, nb_ref, ws_ref, bs_ref, out_ref):
    width = SGU_GROUPS * GROUP_DIM
    for c in range(uv_ref.shape[0] // CHUNK):
        rows = pl.ds(c * CHUNK, CHUNK)
        for g in range(SGU_GROUPS):
            lanes = pl.ds(g * GROUP_DIM, GROUP_DIM)
            u = _gelu(uv_ref[rows, lanes])
            v = _gelu(uv_ref[rows, pl.ds(width + g * GROUP_DIM, GROUP_DIM)])
            vhat, _ = _group_norm(v)
            vn = vhat * ng_ref[:, lanes] + nb_ref[:, lanes]
            w, _ = _causal_w(ws_ref, g)
            mixed = _dot(w, vn) + bs_ref[g]
            out_ref[rows, lanes] = u * mixed


def _sgu_fwd(uv_pre, ng, nb, ws, bs):
    s = uv_pre.shape[0]
    return _row_call(_sgu_fwd_body, "sgu_fwd", s, [uv_pre], [ng, nb, ws, bs],
                     [(SGU_GROUPS * GROUP_DIM, F32)], [])[0]


def _sgu_bwd_body(uv_ref, do_ref, ng_ref, nb_ref, ws_ref, bs_ref,
                  duv_ref, dws_ref, dbs_ref, dng_ref, dnb_ref):
    width = SGU_GROUPS * GROUP_DIM

    @pl.when(pl.program_id(0) == 0)
    def _():
        dws_ref[...] = jnp.zeros_like(dws_ref)
        dbs_ref[...] = jnp.zeros_like(dbs_ref)
        dng_ref[...] = jnp.zeros_like(dng_ref)
        dnb_ref[...] = jnp.zeros_like(dnb_ref)

    for c in range(uv_ref.shape[0] // CHUNK):
        rows = pl.ds(c * CHUNK, CHUNK)
        for g in range(SGU_GROUPS):
            lanes = pl.ds(g * GROUP_DIM, GROUP_DIM)
            vlanes = pl.ds(width + g * GROUP_DIM, GROUP_DIM)
            u_pre = uv_ref[rows, lanes]
            v_pre = uv_ref[rows, vlanes]
            u = _gelu(u_pre)
            v = _gelu(v_pre)
            vhat, rstd = _group_norm(v)
            gain = ng_ref[:, lanes]
            vn = vhat * gain + nb_ref[:, lanes]
            w, causal = _causal_w(ws_ref, g)
            mixed = _dot(w, vn) + bs_ref[g]
            dout = do_ref[rows, lanes]
            du = dout * mixed
            dmixed = dout * u
            dbs_ref[g] += jnp.sum(dmixed, axis=1, keepdims=True)
            dws_ref[g] += jnp.where(causal, _dot_nt(dmixed, vn), 0.0)
            dvn = _dot_tn(w, dmixed)
            dng_ref[:, lanes] += jnp.sum(dvn * vhat, axis=0, keepdims=True)
            dnb_ref[:, lanes] += jnp.sum(dvn, axis=0, keepdims=True)
            dvh = dvn * gain
            dv = rstd * (dvh - jnp.mean(dvh, axis=-1, keepdims=True)
                         - vhat * jnp.mean(dvh * vhat, axis=-1, keepdims=True))
            duv_ref[rows, lanes] = (du * _gelu_grad(u_pre)).astype(BF16)
            duv_ref[rows, vlanes] = (dv * _gelu_grad(v_pre)).astype(BF16)


def _sgu_bwd(uv_pre, dout_a, ng, nb, ws, bs, scatter=None):
    s = uv_pre.shape[0]
    width = SGU_GROUPS * GROUP_DIM
    return _row_call(_sgu_bwd_body, "sgu_bwd", s, [uv_pre, dout_a], [ng, nb, ws, bs],
                     [(2 * width, BF16)],
                     [(ws.shape, F32), (bs.shape, F32), ((1, width), F32), ((1, width), F32)], scatter=scatter)


def _mesh_place():
    return lax.axis_index("x"), lax.axis_index("y"), lax.axis_index("c")


def _other_chips(mx, my):
    return [(1 - mx, my), (mx, 1 - my), (1 - mx, 1 - my)]


_ANY = pl.BlockSpec(memory_space=pl.ANY)
AG_SEMS = 8
_AG_SCRATCH = [pltpu.SemaphoreType.DMA((AG_SEMS,)), pltpu.SemaphoreType.DMA((AG_SEMS,)),
               pltpu.SemaphoreType.DMA(())]
ROW_ALIGN_ANY_DTYPE = 16


def _gather_phases(x_ref, out_ref, send_sems, recv_sems, local_sem):
    mx, my, mc = _mesh_place()
    me, sibling = (mx, my, mc), (mx, my, 1 - mc)
    x_chip, y_chip, far_chip = _other_chips(mx, my)
    rows = x_ref.shape[0]
    cut = (rows // (2 * ROW_ALIGN_ANY_DTYPE)) * ROW_ALIGN_ANY_DTYPE
    parts = {3: pl.ds(0, cut), 7: pl.ds(cut, rows - cut)}

    def slot(px, py, pc):
        return out_ref.at[4 * px + 2 * py + pc]

    def copy(k, block, to, src=None):
        where = slot(*block) if k not in parts else slot(*block).at[parts[k]]
        return pltpu.make_async_remote_copy(
            src_ref=where if src is None else src, dst_ref=where,
            send_sem=send_sems.at[k], recv_sem=recv_sems.at[k],
            device_id=to, device_id_type=MESH)

    mine = pltpu.make_async_copy(x_ref, slot(*me), local_sem)
    first = [copy(0, me, sibling, src=x_ref), copy(1, me, (*x_chip, mc), src=x_ref),
             copy(2, me, (*y_chip, mc), src=x_ref)]
    relayed = [copy(3, (*x_chip, mc), (*y_chip, mc)), copy(7, (*y_chip, mc), (*x_chip, mc))]
    passed = [copy(4, (*x_chip, mc), sibling), copy(5, (*y_chip, mc), sibling), copy(6, (*far_chip, mc), sibling)]

    def start():
        mine.start()
        for cp in first:
            cp.start()

    def relay():
        copy(1, (*x_chip, mc), me).wait_recv()
        relayed[0].start()
        passed[0].start()
        copy(2, (*y_chip, mc), me).wait_recv()
        relayed[1].start()
        passed[1].start()

    def forward():
        copy(3, (*far_chip, mc), me).wait_recv()
        copy(7, (*far_chip, mc), me).wait_recv()
        passed[2].start()

    def finish():
        copy(0, sibling, me).wait_recv()
        for k, chip in ((4, x_chip), (5, y_chip), (6, far_chip)):
            copy(k, (*chip, 1 - mc), me).wait_recv()
        for cp in first + relayed + passed:
            cp.wait_send()
        mine.wait()

    return start, relay, forward, finish


def _all_gather(x, name):
    r, c = x.shape

    def body(x_ref, out_ref, send_sems, recv_sems, local_sem):
        for phase in _gather_phases(x_ref, out_ref, send_sems, recv_sems, local_sem):
            phase()

    return pl.pallas_call(
        body,
        name=name,
        out_shape=jax.ShapeDtypeStruct((N_DEV, r, c), x.dtype),
        in_specs=[_ANY],
        out_specs=_ANY,
        scratch_shapes=list(_AG_SCRATCH),
    )(x)


def _scatter_phases(p_ref, out_ref, send_sems, recv_sems, local_sem):
    mx, my, mc = _mesh_place()
    me = 4 * mx + 2 * my + mc
    copies = []
    for k in range(1, N_DEV):
        tx, ty, tc = mx ^ ((k >> 2) & 1), my ^ ((k >> 1) & 1), mc ^ (k & 1)
        copies.append(pltpu.make_async_remote_copy(
            src_ref=p_ref.at[4 * tx + 2 * ty + tc], dst_ref=out_ref.at[me],
            send_sem=send_sems.at[k - 1], recv_sem=recv_sems.at[k - 1],
            device_id=(tx, ty, tc), device_id_type=MESH))
    mine = pltpu.make_async_copy(p_ref.at[me], out_ref.at[me], local_sem)

    def start():
        mine.start()
        for cp in copies:
            cp.start()

    def finish():
        for cp in copies:
            cp.wait()
        mine.wait()

    return start, finish


def _chip_exchange_phases(q_ref, out_ref, send_sems, recv_sems, local_sem):
    mx, my, mc = _mesh_place()
    copies = [pltpu.make_async_remote_copy(
        src_ref=q_ref.at[2 * cx + cy], dst_ref=out_ref.at[k],
        send_sem=send_sems.at[k], recv_sem=recv_sems.at[k],
        device_id=(cx, cy, mc), device_id_type=MESH)
        for k, (cx, cy) in enumerate(_other_chips(mx, my))]

    def start():
        for cp in copies:
            cp.start()

    def finish():
        for cp in copies:
            cp.wait()

    return start, finish


SB_DEAD = -105.0
HEADS_PER_TILE = 2
TILES_PER_STEP = 2
HEADS_PER_STEP = HEADS_PER_TILE * TILES_PER_STEP
STEP_LANES = TILES_PER_STEP * HEADS_PER_TILE * SB_HEAD_DIM
TILE_LANES = HEADS_PER_TILE * SB_HEAD_DIM
STACK_ROWS = HEADS_PER_STEP * Q_BLOCK
TILE_ROWS = HEADS_PER_TILE * Q_BLOCK
STACKS_PER_STEP = 2
GROUP_LANES = STACKS_PER_STEP * STEP_LANES
HEADS_PER_GROUP = STACKS_PER_STEP * HEADS_PER_STEP
SB_FORWARD_LEAD = 6


def _stack_heads(x):
    lane = lax.broadcasted_iota(jnp.int32, x.shape, 1)
    zero = jnp.zeros_like(x)
    return jnp.concatenate(
        [jnp.where(lane // SB_HEAD_DIM == h, x, zero) for h in range(HEADS_PER_STEP)], axis=0)


def _unstack_tile(x):
    first = lax.broadcasted_iota(jnp.int32, (Q_BLOCK, TILE_LANES), 1) < SB_HEAD_DIM
    return jnp.where(first, x[:Q_BLOCK], x[Q_BLOCK:])


def _sb_logs(qs, k, diagonal):
    z = _dot_nt(qs, k) * (SB_HEAD_DIM ** -0.5)
    sp = jnp.log1p(jnp.exp(-jnp.abs(z)))
    log_beta = jnp.minimum(z, 0.0) - sp
    log_1m_raw = -jnp.maximum(z, 0.0) - sp
    if not diagonal:
        return None, log_beta, log_1m_raw, log_1m_raw
    row = lax.broadcasted_iota(jnp.int32, z.shape, 0)
    col = lax.broadcasted_iota(jnp.int32, z.shape, 1)
    strict = col < jnp.bitwise_and(row, Q_BLOCK - 1)
    return strict, log_beta, log_1m_raw, jnp.where(strict, log_1m_raw, 0.0)


def _masked(strict, x):
    return x if strict is None else jnp.where(strict, x, 0.0)


def _key_sums(x, pick):
    hi, lo = _split_bf16(x)
    both = jnp.dot(jnp.concatenate([hi, lo], axis=0), pick, preferred_element_type=F32)
    return both[:x.shape[0]] + both[x.shape[0]:]


def _key_order():
    row = lax.broadcasted_iota(jnp.int32, (Q_BLOCK, Q_BLOCK), 0)
    col = lax.broadcasted_iota(jnp.int32, (Q_BLOCK, Q_BLOCK), 1)
    return row, col


def _sb_fwd_body(q_ref, k_ref, v_ref, shard_ref, o_ref, tot_ref, cnt_ref, gathered_ref,
                 acc_ref, send_sems, recv_sems, local_sem):
    grp, qb = pl.program_id(0), pl.program_id(1)
    last_grp, last_qb = pl.num_programs(0) - 1, pl.num_programs(1) - 1
    ag_start, ag_relay, ag_forward, ag_finish = _gather_phases(
        shard_ref, gathered_ref, send_sems, recv_sems, local_sem)
    pl.when(jnp.logical_and(grp == 0, qb == 0))(ag_start)

    stacks = range(STACKS_PER_STEP)
    lanes = [slice(st * STEP_LANES, (st + 1) * STEP_LANES) for st in stacks]
    qs = [_stack_heads(q_ref[:, lanes[st]]) for st in stacks]
    row, col = _key_order()
    later = (row > col).astype(BF16)

    def block(i, cs, diagonal):
        rows = pl.ds(pl.multiple_of((qb - i) * Q_BLOCK, Q_BLOCK), Q_BLOCK)
        new_cs = []
        for st in stacks:
            strict, log_beta, _, log_1m = _sb_logs(qs[st], k_ref[rows, lanes[st]], diagonal)
            a = _masked(strict, jnp.exp(log_beta + _key_sums(log_1m, later) + cs[st])).astype(BF16)
            for t in range(TILES_PER_STEP):
                tile = st * TILES_PER_STEP + t
                part = jnp.dot(a[t * TILE_ROWS:(t + 1) * TILE_ROWS],
                               v_ref[rows, tile * TILE_LANES:(tile + 1) * TILE_LANES],
                               preferred_element_type=F32)
                if diagonal:
                    acc_ref[tile] = part
                else:
                    acc_ref[tile] += part
            new_cs.append(cs[st] + jnp.sum(log_1m, axis=1, keepdims=True))
        return tuple(new_cs)

    cs = block(0, (jnp.zeros((STACK_ROWS, 1), F32),) * STACKS_PER_STEP, True)

    def alive(carry):
        i, cs = carry
        return jnp.logical_and(i <= qb, jnp.max(functools.reduce(jnp.maximum, cs)) > SB_DEAD)

    def step(carry):
        i, cs = carry
        return i + 1, block(i, cs, False)

    n, cs = lax.while_loop(alive, step, (jnp.int32(1), cs))
    for tile in range(STACKS_PER_STEP * TILES_PER_STEP):
        o_ref[:, tile * TILE_LANES:(tile + 1) * TILE_LANES] = _unstack_tile(acc_ref[tile])
    for st in stacks:
        for h in range(HEADS_PER_STEP):
            tot_ref[st * HEADS_PER_STEP + h] = cs[st][h * Q_BLOCK:(h + 1) * Q_BLOCK]
    cnt_ref[grp, qb] = n.astype(F32)
    place = grp * (last_qb + 1) + qb
    last = (last_grp + 1) * (last_qb + 1) - 1
    pl.when(place == (last + 1) // 2)(ag_relay)
    pl.when(place == jnp.maximum(last - SB_FORWARD_LEAD, (last + 1) // 2))(ag_forward)
    pl.when(place == last)(ag_finish)


def _sb_fwd(qkv, shard):
    s = qkv.shape[0]
    groups = SB_HEADS // HEADS_PER_GROUP
    nq = s // Q_BLOCK
    return pl.pallas_call(
        functools.partial(_sb_fwd_body),
        name="sb_fwd",
        grid=(groups, nq),
        in_specs=[pl.BlockSpec((Q_BLOCK, GROUP_LANES), lambda g, i: (i, g)),
                  pl.BlockSpec((s, GROUP_LANES), lambda g, i: (0, groups + g)),
                  pl.BlockSpec((s, GROUP_LANES), lambda g, i: (0, 2 * groups + g)),
                  _ANY],
        out_specs=[pl.BlockSpec((Q_BLOCK, GROUP_LANES), lambda g, i: (i, g)),
                   pl.BlockSpec((HEADS_PER_GROUP, Q_BLOCK, 1), lambda g, i: (g, i, 0)),
                   pl.BlockSpec(memory_space=pltpu.SMEM),
                   _ANY],
        out_shape=[jax.ShapeDtypeStruct((s, SB_HEADS * SB_HEAD_DIM), F32),
                   jax.ShapeDtypeStruct((SB_HEADS, s, 1), F32),
                   jax.ShapeDtypeStruct((groups, nq), F32),
                   jax.ShapeDtypeStruct((N_DEV,) + shard.shape, shard.dtype)],
        scratch_shapes=[pltpu.VMEM((STACKS_PER_STEP * TILES_PER_STEP, TILE_ROWS, TILE_LANES), F32)]
        + list(_AG_SCRATCH),
        compiler_params=pltpu.CompilerParams(
            dimension_semantics=("arbitrary", "arbitrary"), vmem_limit_bytes=VMEM_LIMIT_V7X),
    )(qkv, qkv, qkv, shard)


def _sb_bwd_body(cnt_ref, q_ref, k_ref, v_ref, tot_ref, do_ref, part_ref, dq_ref, dk_ref, dv_ref, recv_ref,
                 acc_ref, send_sems, recv_sems, local_sem):
    grp, qb = pl.program_id(0), pl.program_id(1)
    last_grp, last_qb = pl.num_programs(0) - 1, pl.num_programs(1) - 1
    rs_start, rs_finish = _scatter_phases(part_ref, recv_ref, send_sems, recv_sems, local_sem)
    pl.when(jnp.logical_and(grp == 0, qb == 0))(rs_start)

    @pl.when(qb == 0)
    def _():
        dk_ref[...] = jnp.zeros_like(dk_ref)
        dv_ref[...] = jnp.zeros_like(dv_ref)

    acc_ref[...] = jnp.zeros_like(acc_ref)
    stacks = range(STACKS_PER_STEP)
    lanes = [slice(st * STEP_LANES, (st + 1) * STEP_LANES) for st in stacks]
    qs = [_stack_heads(q_ref[:, lanes[st]]) for st in stacks]
    dos = [_stack_heads(do_ref[:, lanes[st]].astype(BF16)) for st in stacks]
    tots = [jnp.concatenate([tot_ref[st * HEADS_PER_STEP + h] for h in range(HEADS_PER_STEP)], axis=0)
            for st in stacks]
    row, col = _key_order()
    up_to = (row <= col).astype(BF16)
    earlier = (row < col).astype(BF16)
    scale = SB_HEAD_DIM ** -0.5
    n = jnp.clip(cnt_ref[grp, qb].astype(jnp.int32), 1, qb + 1)

    def block(kb, cs, ces, diagonal):
        rows = pl.ds(pl.multiple_of(kb * Q_BLOCK, Q_BLOCK), Q_BLOCK)
        new_cs, new_ces = [], []
        for st in stacks:
            k = k_ref[rows, lanes[st]]
            strict, log_beta, log_1m_raw, log_1m = _sb_logs(qs[st], k, diagonal)
            suffix = tots[st] - cs[st] - _key_sums(log_1m, up_to)
            a = _masked(strict, jnp.exp(log_beta + suffix))
            de = _dot_nt(dos[st], v_ref[rows, lanes[st]]) * a
            before = ces[st] + _key_sums(de, earlier)
            dz = _masked(strict, de * jnp.exp(log_1m_raw) - before * jnp.exp(log_beta)).astype(BF16)
            for t in range(TILES_PER_STEP):
                acc_ref[st * TILES_PER_STEP + t] += jnp.dot(
                    dz[t * TILE_ROWS:(t + 1) * TILE_ROWS], k[:, t * TILE_LANES:(t + 1) * TILE_LANES],
                    preferred_element_type=F32)
            dk_ref[rows, lanes[st]] += _dot_tn(dz, qs[st]) * scale
            dv_ref[rows, lanes[st]] += _dot_tn(a, dos[st])
            new_cs.append(cs[st] + jnp.sum(log_1m, axis=1, keepdims=True))
            new_ces.append(ces[st] + jnp.sum(de, axis=1, keepdims=True))
        return tuple(new_cs), tuple(new_ces)

    def step(i, carry):
        return block(qb - n + 1 + i, *carry, False)

    zc = (jnp.zeros((STACK_ROWS, 1), F32),) * STACKS_PER_STEP
    cs, ces = lax.fori_loop(0, n - 1, step, (zc, zc))
    block(qb, cs, ces, True)
    for tile in range(STACKS_PER_STEP * TILES_PER_STEP):
        dq_ref[:, tile * TILE_LANES:(tile + 1) * TILE_LANES] = _unstack_tile(acc_ref[tile]) * scale
    pl.when(jnp.logical_and(grp == last_grp, qb == last_qb))(rs_finish)


def _sb_bwd(cnt, qkv, tot, dout_b, parts):
    s = qkv.shape[0]
    groups = SB_HEADS // HEADS_PER_GROUP
    return pl.pallas_call(
        functools.partial(_sb_bwd_body),
        name="sb_bwd",
        grid=(groups, s // Q_BLOCK),
        in_specs=[pl.BlockSpec(memory_space=pltpu.SMEM),
                  pl.BlockSpec((Q_BLOCK, GROUP_LANES), lambda g, i: (i, g)),
                  pl.BlockSpec((s, GROUP_LANES), lambda g, i: (0, groups + g)),
                  pl.BlockSpec((s, GROUP_LANES), lambda g, i: (0, 2 * groups + g)),
                  pl.BlockSpec((HEADS_PER_GROUP, Q_BLOCK, 1), lambda g, i: (g, i, 0)),
                  pl.BlockSpec((Q_BLOCK, GROUP_LANES), lambda g, i: (i, g)),
                  _ANY],
        out_specs=[pl.BlockSpec((Q_BLOCK, GROUP_LANES), lambda g, i: (i, g)),
                   pl.BlockSpec((s, GROUP_LANES), lambda g, i: (0, g)),
                   pl.BlockSpec((s, GROUP_LANES), lambda g, i: (0, g)),
                   _ANY],
        out_shape=[jax.ShapeDtypeStruct((s, SB_HEADS * SB_HEAD_DIM), F32)] * 3
        + [jax.ShapeDtypeStruct(parts.shape, parts.dtype)],
        scratch_shapes=[pltpu.VMEM((STACKS_PER_STEP * TILES_PER_STEP, TILE_ROWS, TILE_LANES), F32)]
        + list(_AG_SCRATCH),
        compiler_params=pltpu.CompilerParams(
            dimension_semantics=("arbitrary", "arbitrary"), vmem_limit_bytes=VMEM_LIMIT_V7X),
    )(cnt, qkv, qkv, qkv, tot, dout_b, parts)


def _outproj_fwd_body(oa_ref, ob_ref, h_ref, ga_ref, gb_ref, gpost_ref, w_ref,
                      merged_ref, mo_ref, hout_ref):
    half = oa_ref.shape[1]
    ma, _ = _rms(oa_ref[...], ga_ref[...])
    mb, _ = _rms(ob_ref[...], gb_ref[...])
    mab = ma.astype(BF16)
    mbb = mb.astype(BF16)
    merged_ref[:, :half] = mab
    merged_ref[:, half:] = mbb
    mo = (jnp.dot(mab, w_ref[:half, :], preferred_element_type=F32)
          + jnp.dot(mbb, w_ref[half:, :], preferred_element_type=F32))
    mo_ref[...] = mo
    y, _ = _rms(mo, gpost_ref[...])
    hout_ref[...] = h_ref[...] + y


def _outproj_fwd(out_a, out_b, h, ga, gb, gpost, w_out):
    s, d = h.shape
    return _row_call(_outproj_fwd_body, "outproj_fwd", s, [out_a, out_b, h], [ga, gb, gpost, w_out],
                     [(d, BF16), (d, F32), (d, F32)], [])


def _outproj_bwd_body(dh_ref, mo_ref, oa_ref, ob_ref, ga_ref, gb_ref, gpost_ref, w_ref,
                      dmo_ref, doa_ref, dob_ref, dga_ref, dgb_ref, dgpost_ref):
    half = oa_ref.shape[1]
    dmo, dgpost = _rms_bwd(mo_ref[...], gpost_ref[...], dh_ref[...])
    dmob = dmo.astype(BF16)
    dmo_ref[...] = dmob
    dma = _dot_nt(dmob, w_ref[:half, :])
    dmb = _dot_nt(dmob, w_ref[half:, :])
    doa, dga = _rms_bwd(oa_ref[...], ga_ref[...], dma)
    dob, dgb = _rms_bwd(ob_ref[...], gb_ref[...], dmb)
    doa_ref[...] = doa
    dob_ref[...] = dob
    _acc(dga_ref, dga)
    _acc(dgb_ref, dgb)
    _acc(dgpost_ref, dgpost)


def _outproj_bwd(dh, mo, out_a, out_b, ga, gb, gpost, w_out):
    s, d = dh.shape
    half = out_a.shape[1]
    return _row_call(_outproj_bwd_body, "outproj_bwd", s, [dh, mo, out_a, out_b], [ga, gb, gpost, w_out],
                     [(d, BF16), (half, F32), (half, F32)],
                     [((1, half), F32), ((1, half), F32), ((1, d), F32)])


def _kv_fwd_body(mem_ref, g_ref, wt_ref, memn_ref, kv_ref):
    n, _ = _rms(mem_ref[...], g_ref[...])
    nb = n.astype(BF16)
    memn_ref[...] = nb
    kv_ref[...] = _dot_nt(nb, wt_ref[...]).astype(BF16)


def _kv_fwd(mem, g, w_kv_t):
    m, d = mem.shape
    return _row_call(_kv_fwd_body, "kv_fwd", m, [mem], [g, w_kv_t], [(d, BF16), (w_kv_t.shape[0], BF16)], [])


def _kv_bwd_body(dkv_ref, mem_ref, memn_ref, g_ref, wt_ref, dwt_ref, dg_ref):
    dkvb = dkv_ref[...].astype(BF16)
    dwt_ref[...] = _dot_tn(dkvb, memn_ref[...]).astype(BF16)
    dmemn = _dot(dkvb, wt_ref[...])
    _, dg = _rms_bwd(mem_ref[...], g_ref[...], dmemn)
    dg_ref[...] = dg


def _kv_bwd(dkv, mem, memn, g, w_kv_t):
    m, d = mem.shape
    return pl.pallas_call(
        functools.partial(_kv_bwd_body),
        name="kv_bwd",
        out_shape=[jax.ShapeDtypeStruct(w_kv_t.shape, BF16), jax.ShapeDtypeStruct((1, d), F32)],
        compiler_params=pltpu.CompilerParams(vmem_limit_bytes=VMEM_LIMIT_V7X),
    )(dkv, mem, memn, g, w_kv_t)


def _xa_fwd_body(h_ref, gpre_ref, gpost_ref, wq_ref, wo_ref, kv_ref,
                 n_ref, q_ref, o_ref, c_ref, hout_ref):
    h = h_ref[...]
    d = h.shape[1]
    n, _ = _rms(h, gpre_ref[...])
    nb = n.astype(BF16)
    n_ref[...] = nb
    qb = jnp.dot(nb, wq_ref[...], preferred_element_type=F32).astype(BF16)
    q_ref[...] = qb
    for hd in range(XA_HEADS):
        lanes = slice(hd * XA_HEAD_DIM, (hd + 1) * XA_HEAD_DIM)
        k = kv_ref[:, lanes]
        v = kv_ref[:, d + hd * XA_HEAD_DIM:d + (hd + 1) * XA_HEAD_DIM]
        logits = _dot_nt(qb[:, lanes], k) * (XA_HEAD_DIM ** -0.5)
        e = jnp.exp(logits - jnp.max(logits, axis=-1, keepdims=True))
        p = e / jnp.sum(e, axis=-1, keepdims=True)
        o_ref[:, lanes] = jnp.dot(p.astype(BF16), v, preferred_element_type=F32).astype(BF16)
    c = jnp.dot(o_ref[...], wo_ref[...], preferred_element_type=F32)
    c_ref[...] = c
    y, _ = _rms(c, gpost_ref[...])
    hout_ref[...] = h + y


def _xa_fwd(h, gpre, gpost, wq, wo, kv):
    s, d = h.shape
    return _row_call(_xa_fwd_body, "xa_fwd", s, [h], [gpre, gpost, wq, wo, kv],
                     [(d, BF16), (d, BF16), (d, BF16), (d, F32), (d, F32)], [])


def _xa_bwd_body(dh_ref, h_ref, c_ref, q_ref, o_ref, gpre_ref, gpost_ref, wq_ref, wo_ref, kv_ref,
                 dhout_ref, dc_ref, dq_ref, dkv_ref, dgpre_ref, dgpost_ref):
    dh = dh_ref[...]
    d = dh.shape[1]
    scale = XA_HEAD_DIM ** -0.5
    dc, dgpost = _rms_bwd(c_ref[...], gpost_ref[...], dh)
    dcb = dc.astype(BF16)
    dc_ref[...] = dcb
    dob = _dot_nt(dcb, wo_ref[...]).astype(BF16)

    @pl.when(pl.program_id(0) == 0)
    def _():
        dkv_ref[...] = jnp.zeros_like(dkv_ref)

    for hd in range(XA_HEADS):
        lanes = slice(hd * XA_HEAD_DIM, (hd + 1) * XA_HEAD_DIM)
        vlanes = slice(d + hd * XA_HEAD_DIM, d + (hd + 1) * XA_HEAD_DIM)
        qh = q_ref[:, lanes]
        k = kv_ref[:, lanes]
        v = kv_ref[:, vlanes]
        logits = _dot_nt(qh, k) * scale
        e = jnp.exp(logits - jnp.max(logits, axis=-1, keepdims=True))
        p = e / jnp.sum(e, axis=-1, keepdims=True)
        doh = dob[:, lanes]
        dp = _dot_nt(doh, v)
        dl = (p * (dp - jnp.sum(dp * p, axis=-1, keepdims=True)) * scale).astype(BF16)
        dq_ref[:, lanes] = jnp.dot(dl, k, preferred_element_type=F32).astype(BF16)
        dkv_ref[:, lanes] += _dot_tn(dl, qh)
        dkv_ref[:, vlanes] += _dot_tn(p, doh)
    dn = _dot_nt(dq_ref[...], wq_ref[...])
    dhn, dgpre = _rms_bwd(h_ref[...], gpre_ref[...], dn)
    dhout_ref[...] = dh + dhn
    _acc(dgpre_ref, dgpre)
    _acc(dgpost_ref, dgpost)


def _xa_bwd(dh, h, c, q, o, gpre, gpost, wq, wo, kv, scatter=None):
    s, d = h.shape
    return _row_call(_xa_bwd_body, "xa_bwd", s, [dh, h, c, q, o], [gpre, gpost, wq, wo, kv],
                     [(d, F32), (d, BF16), (d, BF16)],
                     [(kv.shape, F32), ((1, d), F32), ((1, d), F32)], scatter=scatter)


def _largest_tile(n, cap):
    best = 128
    for t in range(128, cap + 1, 128):
        if n % t == 0:
            best = t
    return best


def _mm_tn(a, bs, name, gather=None, chip_exchange=None):
    s, k = a.shape
    n = bs[0].shape[1]
    nb = len(bs)
    ts = min(2048, s)
    tk = _largest_tile(k, 1536)
    tn = _largest_tile(n, 1536 // nb)

    steps = s // ts
    grid = (k // tk, n // tn, steps)
    sent = gather if gather is not None else chip_exchange
    hosted = sent is not None
    phases = _gather_phases if gather is not None else _chip_exchange_phases
    landed_shape = None
    if hosted:
        landed_shape = (N_DEV,) + sent.shape if gather is not None else (3,) + sent.shape[1:]

    def body(a_ref, *refs):
        if hosted:
            sent_ref, landed_ref, refs = refs[nb], refs[2 * nb + 1], refs[:nb] + refs[nb + 1:2 * nb + 1] + refs[2 * nb + 2:]
            start, *rest = phases(sent_ref, landed_ref, *refs[-3:])
            place = (pl.program_id(0) * grid[1] + pl.program_id(1)) * grid[2] + pl.program_id(2)
            pl.when(place == 0)(start)
        b_refs, o_refs, acc_refs = refs[:nb], refs[nb:2 * nb], refs[2 * nb:3 * nb]
        at = a_ref[...]
        t = pl.program_id(2)

        @pl.when(t == 0)
        def _():
            for acc_ref in acc_refs:
                acc_ref[...] = jnp.zeros_like(acc_ref)

        for b_ref, acc_ref in zip(b_refs, acc_refs):
            acc_ref[...] += _dot_tn(at, b_ref[...])

        @pl.when(t == steps - 1)
        def _():
            for o_ref, acc_ref in zip(o_refs, acc_refs):
                o_ref[...] = acc_ref[...].astype(BF16)

        if hosted:
            total = grid[0] * grid[1] * grid[2]

            @pl.when(place == total - 1)
            def _():
                for phase in rest:
                    phase()

    return pl.pallas_call(
        body,
        name=name,
        grid=grid,
        in_specs=[pl.BlockSpec((ts, tk), lambda i, j, t: (t, i))]
        + [pl.BlockSpec((ts, tn), lambda i, j, t: (t, j))] * nb + ([_ANY] if hosted else []),
        out_specs=[pl.BlockSpec((tk, tn), lambda i, j, t: (i, j))] * nb + ([_ANY] if hosted else []),
        out_shape=[jax.ShapeDtypeStruct((k, n), BF16)] * nb
        + ([jax.ShapeDtypeStruct(landed_shape, sent.dtype)] if hosted else []),
        scratch_shapes=[pltpu.VMEM((tk, tn), F32)] * nb + (list(_AG_SCRATCH) if hosted else []),
        compiler_params=pltpu.CompilerParams(
            dimension_semantics=("arbitrary", "arbitrary", "arbitrary"),
            vmem_limit_bytes=VMEM_LIMIT_V7X),
    )(a, *bs, *([sent] if hosted else []))


_SMALL_SHAPES = {
    "sgu_norm_g": (1, SGU_GROUPS * GROUP_DIM),
    "sgu_norm_b": (1, SGU_GROUPS * GROUP_DIM),
    "sgu_w_s": (SGU_GROUPS, CHUNK, CHUNK),
    "sgu_b_s": (SGU_GROUPS, CHUNK, 1),
}


def _small_views(small):
    return {n: v.reshape(_SMALL_SHAPES.get(n, v.shape)) for n, v in small.items()}


def _small_unviews(views, like):
    return {n: v.reshape(like[n].shape) for n, v in views.items()}


def _unpack_rows(gathered, names, shard_rows):
    out, off = {}, 0
    for n in names:
        rows = shard_rows[n]
        out[n] = gathered[:, off:off + rows, :].reshape(N_DEV * rows, gathered.shape[2])
        off += rows
    return out


def _row_tile(r, cap):
    best = 16
    for t in range(16, cap + 1, 16):
        if r % t == 0:
            best = t
    return best


def _sum_received(received, name):
    _, r, c = received.shape
    tr = _row_tile(r, 1024)

    def body(rc_ref, g_ref):
        g = rc_ref[0].astype(F32)
        for t in range(1, N_DEV):
            g = g + rc_ref[t].astype(F32)
        g_ref[...] = g

    return pl.pallas_call(
        body, name=name, grid=(r // tr,),
        in_specs=[pl.BlockSpec((N_DEV, tr, c), lambda i: (0, i, 0))],
        out_specs=pl.BlockSpec((tr, c), lambda i: (i, 0)),
        out_shape=jax.ShapeDtypeStruct((r, c), F32),
    )(received)


def _local_step(x, mem, target, small, big, shards, shard_rows):
    sm, w = small, dict(big)
    d_model = x.shape[1]

    def parts(names):
        return jnp.concatenate([gw.pop(n).reshape(N_DEV, -1, d_model) for n in names], axis=1)

    h1, n1, a1, b1, f1, landed = _ffn_fwd(
        x, sm["ffn1_pre_g"], sm["ffn1_post_g"], w["ffn1_w_gate"], w["ffn1_w_up"], w["ffn1_w_down"],
        "ffn1_fwd", gather=shards["ffn1_fwd"])
    w.update(_unpack_rows(landed, GATHER_IN["ffn1_fwd"], shard_rows))
    n2, uv_pre, qkv = _inproj_fwd(h1, sm["mix_pre_g"], w["w_in"])
    out_a = _sgu_fwd(uv_pre, sm["sgu_norm_g"], sm["sgu_norm_b"], sm["sgu_w_s"], sm["sgu_b_s"])
    out_b, tot, cnt, landed = _sb_fwd(qkv, shards["sb_fwd"])
    w.update(_unpack_rows(landed, GATHER_IN["sb_fwd"], shard_rows))
    merged, mo, h2 = _outproj_fwd(out_a, out_b, h1, sm["sgu_out_g"], sm["sb_out_g"],
                                  sm["mix_post_g"], w["w_out"])
    memn, kv = _kv_fwd(mem, sm["mem_norm_g"], w["xa_w_kv"])
    n3, qx, ox, cx, h3 = _xa_fwd(h2, sm["xa_pre_g"], sm["xa_post_g"], w["xa_w_q"], w["xa_w_o"], kv)
    n4, a2, b2, f2, dh4, loss, dg_final = _ffn_loss_fwd(
        h3, target, sm["ffn2_pre_g"], sm["ffn2_post_g"], sm["final_norm_g"],
        w["ffn2_w_gate"], w["ffn2_w_up"], w["ffn2_w_down"], "ffn2_fwd")

    gs, gw = {"final_norm_g": dg_final}, {}
    dh3, da2, db2, hm2, df2, gs["ffn2_pre_g"], gs["ffn2_post_g"] = _ffn_bwd(
        dh4, h3, a2, b2, f2, sm["ffn2_pre_g"], sm["ffn2_post_g"],
        w["ffn2_w_gate"], w["ffn2_w_up"], w["ffn2_w_down"], "ffn2_bwd")
    gw["ffn2_w_gate"], = _mm_tn(da2, [n4], "ffn2_dw_gate")
    gw["ffn2_w_up"], = _mm_tn(db2, [n4], "ffn2_dw_up")
    gw["ffn2_w_down"], = _mm_tn(hm2, [df2], "ffn2_dw_down")

    received = {}
    dh2, dc, dqx, dkv, gs["xa_pre_g"], gs["xa_post_g"], received["xa_bwd"] = _xa_bwd(
        dh3, h2, cx, qx, ox, sm["xa_pre_g"], sm["xa_post_g"], w["xa_w_q"], w["xa_w_o"], kv,
        scatter=parts(SCATTER_IN["xa_bwd"]))
    gw["xa_w_o"], = _mm_tn(ox, [dc], "xa_dw_o")
    gw["xa_w_q"], = _mm_tn(n3, [dqx], "xa_dw_q")
    gw["xa_w_kv"], gs["mem_norm_g"] = _kv_bwd(dkv, mem, memn, sm["mem_norm_g"], w["xa_w_kv"])

    dmo, dout_a, dout_b, gs["sgu_out_g"], gs["sb_out_g"], gs["mix_post_g"] = _outproj_bwd(
        dh2, mo, out_a, out_b, sm["sgu_out_g"], sm["sb_out_g"], sm["mix_post_g"], w["w_out"])
    gw["w_out"], = _mm_tn(merged, [dmo], "mix_dw_out")
    dq, dk, dv, received["sb_bwd"] = _sb_bwd(cnt, qkv, tot, dout_b, parts(SCATTER_IN["sb_bwd"]))
    duv, gs["sgu_w_s"], gs["sgu_b_s"], gs["sgu_norm_g"], gs["sgu_norm_b"], received["sgu_bwd"] = _sgu_bwd(
        uv_pre, dout_a, sm["sgu_norm_g"], sm["sgu_norm_b"], sm["sgu_w_s"], sm["sgu_b_s"],
        scatter=parts(SCATTER_IN["sgu_bwd"]))
    dproj = jnp.concatenate([duv] + [t.astype(BF16) for t in (dq, dk, dv)], axis=1)
    dh1, gs["mix_pre_g"], received["inproj_bwd"] = _inproj_bwd(
        dh2, dproj, h1, sm["mix_pre_g"], w["w_in"], scatter=parts(SCATTER_IN["inproj_bwd"]))
    dx, da1, db1, hm1, df1, gs["ffn1_pre_g"], gs["ffn1_post_g"] = _ffn_bwd(
        dh1, x, a1, b1, f1, sm["ffn1_pre_g"], sm["ffn1_post_g"],
        w["ffn1_w_gate"], w["ffn1_w_up"], w["ffn1_w_down"], "ffn1_bwd")
    summed = {host: _sum_received(r, "rs_sum_" + host) for host, r in received.items()}

    mx, my, mc = _mesh_place()
    place = jnp.stack([mc, 2 * mx + my]).astype(jnp.int32)

    def pair_stage(dw, tag):
        p4 = dw.reshape(N_DEV // 2, 2, -1, d_model)
        recv_a = _pair_exchange(p4, "rs_pair_exchange_" + tag)
        return p4, recv_a, _pair_sum(place, p4, recv_a, "rs_pair_sum_" + tag)

    dw, small_grads = _mm_tn(da1, [n1], "ffn1_dw_gate", gather=_pack_small(gs))
    gate = pair_stage(dw, "gate")
    dw, gate_b = _mm_tn(db1, [n1], "ffn1_dw_up", chip_exchange=gate[2])
    up = pair_stage(dw, "up")
    dw, up_b = _mm_tn(hm1, [df1], "ffn1_dw_down", chip_exchange=up[2])
    down = pair_stage(dw, "down")
    dw, down_b = _mm_tn(dproj, [n2], "mix_dw_in", chip_exchange=down[2])
    w_in = pair_stage(dw, "w_in")
    w_in_b = _chip_exchange(w_in[2], "rs_chip_exchange_w_in")
    for n, (p4, recv_a, _), recv_b in (("ffn1_w_gate", gate, gate_b), ("ffn1_w_up", up, up_b),
                                       ("ffn1_w_down", down, down_b), ("w_in", w_in, w_in_b)):
        summed[n] = _rs_final(place, p4, recv_a, recv_b, "rs_final_" + n)
    return loss, dx, small_grads, summed


def _pair_exchange(p4, name):
    nchip, _, r, c = p4.shape

    def body(p_ref, out_ref, send_sems, recv_sems):
        mx, my, mc = _mesh_place()
        copies = [pltpu.make_async_remote_copy(
            src_ref=p_ref.at[j, 1 - mc], dst_ref=out_ref.at[j],
            send_sem=send_sems.at[j], recv_sem=recv_sems.at[j],
            device_id=(mx, my, 1 - mc), device_id_type=MESH) for j in range(nchip)]
        for cp in copies:
            cp.start()
        for cp in copies:
            cp.wait()

    return pl.pallas_call(
        body,
        name=name,
        out_shape=jax.ShapeDtypeStruct((nchip, r, c), p4.dtype),
        in_specs=[_ANY],
        out_specs=_ANY,
        scratch_shapes=[pltpu.SemaphoreType.DMA((nchip,)), pltpu.SemaphoreType.DMA((nchip,))],
    )(p4)


def _chip_exchange(q, name):
    _, r, c = q.shape

    def body(q_ref, out_ref, send_sems, recv_sems, local_sem):
        for phase in _chip_exchange_phases(q_ref, out_ref, send_sems, recv_sems, local_sem):
            phase()

    return pl.pallas_call(
        body,
        name=name,
        out_shape=jax.ShapeDtypeStruct((3, r, c), q.dtype),
        in_specs=[_ANY],
        out_specs=_ANY,
        scratch_shapes=list(_AG_SCRATCH),
    )(q)


def _rs_row_tile(r):
    return _row_tile(r, 1024)


def _pair_sum(place, p4, recv_a, name):
    nchip, _, r, c = p4.shape
    tr = _rs_row_tile(r)

    def body(place_ref, p_ref, a_ref, q_ref):
        q_ref[0] = (p_ref[0, 0].astype(F32) + a_ref[0].astype(F32)).astype(BF16)

    return pl.pallas_call(
        body,
        name=name,
        grid_spec=pltpu.PrefetchScalarGridSpec(
            num_scalar_prefetch=1,
            grid=(nchip, r // tr),
            in_specs=[pl.BlockSpec((1, 1, tr, c), lambda j, i, pref: (j, pref[0], i, 0)),
                      pl.BlockSpec((1, tr, c), lambda j, i, pref: (j, i, 0))],
            out_specs=pl.BlockSpec((1, tr, c), lambda j, i, pref: (j, i, 0)),
        ),
        out_shape=jax.ShapeDtypeStruct((nchip, r, c), BF16),
    )(place, p4, recv_a)


def _rs_final(place, p4, recv_a, recv_b, name):
    _, _, r, c = p4.shape
    tr = _rs_row_tile(r)

    def body(place_ref, p_ref, a_ref, b_ref, g_ref):
        g = p_ref[0, 0].astype(F32) + a_ref[0].astype(F32)
        for k in range(3):
            g = g + b_ref[k].astype(F32)
        g_ref[...] = g

    return pl.pallas_call(
        body,
        name=name,
        grid_spec=pltpu.PrefetchScalarGridSpec(
            num_scalar_prefetch=1,
            grid=(r // tr,),
            in_specs=[pl.BlockSpec((1, 1, tr, c), lambda i, pref: (pref[1], pref[0], i, 0)),
                      pl.BlockSpec((1, tr, c), lambda i, pref: (pref[1], i, 0)),
                      pl.BlockSpec((3, tr, c), lambda i, pref: (0, i, 0))],
            out_specs=pl.BlockSpec((tr, c), lambda i, pref: (i, 0)),
        ),
        out_shape=jax.ShapeDtypeStruct((r, c), F32),
    )(place, p4, recv_a, recv_b)


def _adamw_math(w, g, m, v):
    m = ADAM_B1 * m + (1.0 - ADAM_B1) * g
    v = ADAM_B2 * v + (1.0 - ADAM_B2) * (g * g)
    m_hat = m / (1.0 - ADAM_B1 ** ADAM_STEP)
    v_hat = v / (1.0 - ADAM_B2 ** ADAM_STEP)
    delta = -ADAM_LR * (m_hat / (jnp.sqrt(v_hat) + ADAM_EPS) + ADAM_WD * w)
    return delta, m, v


def _adamw(w, g, m, v, name):
    r, c = w.shape
    tr = r if r <= 512 else 256

    def body(w_ref, g_ref, m_ref, v_ref, d_ref, mo_ref, vo_ref):
        d_ref[...], mo_ref[...], vo_ref[...] = _adamw_math(w_ref[...], g_ref[...], m_ref[...], v_ref[...])

    spec = pl.BlockSpec((tr, c), lambda i: (i, 0))
    out = jax.ShapeDtypeStruct((r, c), F32)
    return pl.pallas_call(
        body, name=name, grid=(r // tr,), in_specs=[spec] * 4, out_specs=[spec] * 3,
        out_shape=[out] * 3,
    )(w, g, m, v)


def _small_sum_adamw(gathered, ws, ms, vs):
    _, r, c = gathered.shape
    count = len(ws)

    def body(ga_ref, *refs):
        w_refs, m_refs, v_refs = refs[:count], refs[count:2 * count], refs[2 * count:3 * count]
        out_refs = refs[3 * count:]
        for o_ref in out_refs:
            o_ref[...] = jnp.zeros_like(o_ref)
        off = 0
        for w_ref, m_ref, v_ref in zip(w_refs, m_refs, v_refs):
            rows = pl.ds(off, w_ref.shape[0])
            g = ga_ref[0, rows, :]
            for k in range(1, N_DEV):
                g = g + ga_ref[k, rows, :]
            results = (g, *_adamw_math(w_ref[...], g, m_ref[...], v_ref[...]))
            for o_ref, val in zip(out_refs, results):
                o_ref[rows, :] = val
            off += w_ref.shape[0] + (-w_ref.shape[0]) % SMALL_ROW_ALIGN

    out = jax.ShapeDtypeStruct((r, c), F32)
    return pl.pallas_call(body, name="small_sum_adamw", out_shape=[out] * 4)(gathered, *ws, *ms, *vs)


_WEIGHTS = ["ffn1_pre_g", "ffn1_post_g", "ffn1_w_gate", "ffn1_w_up", "ffn1_w_down", "mix_pre_g",
            "mix_post_g", "w_in", "sgu_norm_g", "sgu_norm_b", "sgu_w_s", "sgu_b_s", "sgu_out_g",
            "sb_out_g", "w_out", "xa_pre_g", "xa_post_g", "mem_norm_g", "xa_w_q", "xa_w_kv", "xa_w_o",
            "ffn2_pre_g", "ffn2_post_g", "ffn2_w_gate", "ffn2_w_up", "ffn2_w_down", "final_norm_g"]
_BIG = ["ffn1_w_gate", "ffn1_w_up", "ffn1_w_down", "w_in", "w_out", "xa_w_q", "xa_w_kv", "xa_w_o",
        "ffn2_w_gate", "ffn2_w_up", "ffn2_w_down"]
_COL_SHARDED = ("ffn1_w_gate", "ffn1_w_up", "w_in", "xa_w_kv", "ffn2_w_gate", "ffn2_w_up")
_EARLY = ["ffn1_w_gate", "ffn1_w_up", "ffn1_w_down"]
GATHER_IN = {"ffn1_fwd": ["w_in", "ffn2_w_gate"],
             "sb_fwd": ["w_out", "xa_w_q", "xa_w_kv", "xa_w_o", "ffn2_w_up", "ffn2_w_down"]}
SCATTER_IN = {"xa_bwd": ["ffn2_w_gate"],
              "sb_bwd": ["ffn2_w_up", "ffn2_w_down", "xa_w_o", "w_out"],
              "sgu_bwd": ["xa_w_kv"],
              "inproj_bwd": ["xa_w_q"]}
_TAIL = _EARLY + ["w_in"]
_SMALL = [n for n in _WEIGHTS if n not in _BIG]
SMALL_LANES = 128
SMALL_ROW_ALIGN = 8


def _pack_small(tensors):
    parts = []
    for n in _SMALL:
        t = tensors[n].reshape(-1, SMALL_LANES)
        pad = (-t.shape[0]) % SMALL_ROW_ALIGN
        parts.append(jnp.pad(t, ((0, pad), (0, 0))) if pad else t)
    return jnp.concatenate(parts, axis=0)


def _unpack_small(packed, like):
    out, off = {}, 0
    for n in _SMALL:
        size = like[n].size
        rows = size // SMALL_LANES
        out[n] = packed[off:off + rows].reshape(like[n].shape)
        off += rows + (-rows) % SMALL_ROW_ALIGN
    return out


def kernel(x, mem, ffn1_pre_g, ffn1_post_g, ffn1_w_gate, ffn1_w_up, ffn1_w_down, mix_pre_g, mix_post_g, w_in, sgu_norm_g, sgu_norm_b, sgu_w_s, sgu_b_s, sgu_out_g, sb_out_g, w_out, xa_pre_g, xa_post_g, mem_norm_g, xa_w_q, xa_w_kv, xa_w_o, ffn2_pre_g, ffn2_post_g, ffn2_w_gate, ffn2_w_up, ffn2_w_down, final_norm_g, loss_target, m_ffn1_pre_g, m_ffn1_post_g, m_ffn1_w_gate, m_ffn1_w_up, m_ffn1_w_down, m_mix_pre_g, m_mix_post_g, m_w_in, m_sgu_norm_g, m_sgu_norm_b, m_sgu_w_s, m_sgu_b_s, m_sgu_out_g, m_sb_out_g, m_w_out, m_xa_pre_g, m_xa_post_g, m_mem_norm_g, m_xa_w_q, m_xa_w_kv, m_xa_w_o, m_ffn2_pre_g, m_ffn2_post_g, m_ffn2_w_gate, m_ffn2_w_up, m_ffn2_w_down, m_final_norm_g, v_ffn1_pre_g, v_ffn1_post_g, v_ffn1_w_gate, v_ffn1_w_up, v_ffn1_w_down, v_mix_pre_g, v_mix_post_g, v_w_in, v_sgu_norm_g, v_sgu_norm_b, v_sgu_w_s, v_sgu_b_s, v_sgu_out_g, v_sb_out_g, v_w_out, v_xa_pre_g, v_xa_post_g, v_mem_norm_g, v_xa_w_q, v_xa_w_kv, v_xa_w_o, v_ffn2_pre_g, v_ffn2_post_g, v_ffn2_w_gate, v_ffn2_w_up, v_ffn2_w_down, v_final_norm_g):
    vals = dict(locals())
    d_model = x.shape[-1]

    def packed(names):
        return jnp.concatenate(
            [(vals[n][0].T if n in _COL_SHARDED else vals[n][0]).astype(BF16) for n in names], axis=0)

    shard_rows = {n: vals[n].shape[2 if n in _COL_SHARDED else 1] for n in _BIG}
    big = _unpack_rows(_all_gather(packed(_EARLY), "ag_weights"), _EARLY, shard_rows)

    small = {n: vals[n] for n in _SMALL}
    loss_part, dx, gathered_small, summed = _local_step(
        x[0], mem[0], loss_target[0], _small_views(small), big,
        {host: packed(names) for host, names in GATHER_IN.items()}, shard_rows)
    loss = lax.psum(loss_part[0, 0], ("x", "y", "c"))

    grads, deltas, new_m, new_v = {}, {}, {}, {}
    for names, g_rows in [([n], summed[n]) for n in _TAIL] + [(SCATTER_IN[h], summed[h]) for h in SCATTER_IN]:
        off = 0
        for n in names:
            rows = shard_rows[n]
            g = g_rows[off:off + rows]
            off += rows
            state = [vals[n][0], vals["m_" + n][0], vals["v_" + n][0]]
            flipped = n in _COL_SHARDED and rows % SMALL_LANES != 0
            if flipped:
                state = [t.T for t in state]
            elif n in _COL_SHARDED:
                g = g.T
            outs = (g, *_adamw(state[0], g, state[1], state[2], "adamw_" + n))
            if flipped:
                outs = tuple(t.T for t in outs)
            grads[n], deltas[n], new_m[n], new_v[n] = (t[None] for t in outs)

    outs = _small_sum_adamw(gathered_small,
                            *([vals[pre + n].reshape(-1, SMALL_LANES) for n in _SMALL] for pre in ("", "m_", "v_")))
    for dst, packed in zip((grads, deltas, new_m, new_v), outs):
        dst.update(_unpack_small(packed, small))

    return (loss, dx[None], *[grads[n] for n in _WEIGHTS], *[deltas[n] for n in _WEIGHTS],
            *[new_m[n] for n in _WEIGHTS], *[new_v[n] for n in _WEIGHTS])
```

```python
import functools

import jax
import jax.numpy as jnp
from jax import lax
from jax.experimental import pallas as pl
from jax.experimental.pallas import tpu as pltpu

F32 = jnp.float32
BF16 = jnp.bfloat16
EPS = 1e-6
MESH = pl.DeviceIdType.MESH
N_DEV = 8

SGU_GROUPS = 4
GROUP_DIM = 128
CHUNK = 128
SB_HEADS = 8
SB_HEAD_DIM = 64
Q_BLOCK = 128
XA_HEADS = 4
XA_HEAD_DIM = 256

ADAM_LR = 0.001
ADAM_B1 = 0.9
ADAM_B2 = 0.999
ADAM_EPS = 1e-08
ADAM_WD = 0.01
ADAM_STEP = 10

VMEM_LIMIT_V7X = 56 * 1024 * 1024
GELU_C0 = 0.7978845608028654
GELU_C1 = 0.044715


def _dot(a, b):
    return jnp.dot(a.astype(BF16), b.astype(BF16), preferred_element_type=F32)


def _dot_nt(a, b):
    return lax.dot_general(a.astype(BF16), b.astype(BF16), (((1,), (1,)), ((), ())),
                           preferred_element_type=F32)


def _dot_tn(a, b):
    return lax.dot_general(a.astype(BF16), b.astype(BF16), (((0,), (0,)), ((), ())),
                           preferred_element_type=F32)


def _rms(x, g):
    r = lax.rsqrt(jnp.mean(x * x, axis=-1, keepdims=True) + EPS)
    return x * r * g, r


def _rms_bwd(x, g, dy):
    r = lax.rsqrt(jnp.mean(x * x, axis=-1, keepdims=True) + EPS)
    xh = x * r
    gy = dy * g
    dx = r * (gy - xh * jnp.mean(gy * xh, axis=-1, keepdims=True))
    dg = jnp.sum(dy * xh, axis=0, keepdims=True)
    return dx, dg


def _sigmoid(x):
    return jax.nn.sigmoid(x)


def _gelu(x):
    t = jnp.tanh(GELU_C0 * (x + GELU_C1 * x * x * x))
    return 0.5 * x * (1.0 + t)


def _gelu_grad(x):
    t = jnp.tanh(GELU_C0 * (x + GELU_C1 * x * x * x))
    return 0.5 * (1.0 + t) + 0.5 * x * (1.0 - t * t) * GELU_C0 * (1.0 + 3.0 * GELU_C1 * x * x)


def _split_bf16(x):
    hi = x.astype(BF16)
    lo = (x - hi.astype(F32)).astype(BF16)
    return hi, lo


def _row_spec(tm, cols):
    return pl.BlockSpec((tm, cols), lambda i: (i, 0))


def _full_spec(shape, buffers=None):
    nd = len(shape)
    mode = None if buffers is None else pl.Buffered(buffers)
    return pl.BlockSpec(tuple(shape), lambda i: (0,) * nd, pipeline_mode=mode)


FFN_TOKEN_TILE = 512
XA_TOKEN_TILE = 512


def _token_tile(s):
    return min(256, s)


def _row_call(body, name, s, tiled_in, full_in, tiled_out, acc_out, gather=None, scatter=None, tm=None):
    tm = _token_tile(s) if tm is None else min(tm, s)
    steps = s // tm
    in_specs = [_row_spec(tm, a.shape[1]) for a in tiled_in] + [_full_spec(a.shape, buffers=1) for a in full_in]
    out_specs = [_row_spec(tm, c) for c, _ in tiled_out] + [_full_spec(sh) for sh, _ in acc_out]
    out_shape = [jax.ShapeDtypeStruct((s, c), dt) for c, dt in tiled_out]
    out_shape += [jax.ShapeDtypeStruct(sh, dt) for sh, dt in acc_out]
    operands = [*tiled_in, *full_in]
    scratch = []
    kernel_body = functools.partial(body)
    sent = gather if gather is not None else scatter
    if sent is not None:
        n_in, n_out = len(operands), len(out_shape)
        out_shape.append(jax.ShapeDtypeStruct(
            (N_DEV,) + sent.shape if gather is not None else sent.shape, sent.dtype))
        operands.append(sent)
        in_specs.append(_ANY)
        out_specs.append(_ANY)
        scratch = list(_AG_SCRATCH)

        def kernel_body(*refs):
            ins, sent_ref = refs[:n_in], refs[n_in]
            outs, landed_ref = refs[n_in + 1:n_in + 1 + n_out], refs[n_in + 1 + n_out]
            step = pl.program_id(0)
            if gather is not None:
                start, relay, forward, finish = _gather_phases(sent_ref, landed_ref, *refs[-3:])
            else:
                start, finish = _scatter_phases(sent_ref, landed_ref, *refs[-3:])
            pl.when(step == 0)(start)
            body(*ins, *outs)
            if gather is not None:
                pl.when(step == steps // 3)(relay)
                pl.when(step == (2 * steps) // 3)(forward)
            pl.when(step == steps - 1)(finish)

    return pl.pallas_call(
        kernel_body,
        name=name,
        grid=(steps,),
        in_specs=in_specs,
        out_specs=out_specs,
        out_shape=out_shape,
        scratch_shapes=scratch,
        compiler_params=pltpu.CompilerParams(
            dimension_semantics=("arbitrary",), vmem_limit_bytes=VMEM_LIMIT_V7X),
    )(*operands)


def _acc(ref, val):
    @pl.when(pl.program_id(0) == 0)
    def _():
        ref[...] = val

    @pl.when(pl.program_id(0) != 0)
    def _():
        ref[...] += val


def _ffn_fwd_tile(x_ref, pre_ref, post_ref, wgt_ref, wut_ref, wd_ref, n_ref, a_ref, b_ref, f_ref):
    x = x_ref[...]
    n, _ = _rms(x, pre_ref[...])
    nb = n.astype(BF16)
    n_ref[...] = nb
    a = _dot_nt(nb, wgt_ref[...])
    b = _dot_nt(nb, wut_ref[...])
    a_ref[...] = a.astype(BF16)
    b_ref[...] = b.astype(BF16)
    hmid = a * _sigmoid(a) * b
    f = jnp.dot(hmid.astype(BF16), wd_ref[...], preferred_element_type=F32)
    f_ref[...] = f
    y, _ = _rms(f, post_ref[...])
    return x + 0.5 * y


def _ffn_fwd_body(x_ref, pre_ref, post_ref, wgt_ref, wut_ref, wd_ref,
                  h_ref, n_ref, a_ref, b_ref, f_ref):
    h_ref[...] = _ffn_fwd_tile(x_ref, pre_ref, post_ref, wgt_ref, wut_ref, wd_ref, n_ref, a_ref, b_ref, f_ref)


def _ffn_loss_body(x_ref, t_ref, pre_ref, post_ref, gfin_ref, wgt_ref, wut_ref, wd_ref,
                   n_ref, a_ref, b_ref, f_ref, dh_ref, loss_ref, dg_ref):
    h = _ffn_fwd_tile(x_ref, pre_ref, post_ref, wgt_ref, wut_ref, wd_ref, n_ref, a_ref, b_ref, f_ref)
    d = h.shape[1]
    y, _ = _rms(h, gfin_ref[...])
    err = y - t_ref[...]
    part = (0.5 / d) * jnp.sum(jnp.sum(err * err, axis=1, keepdims=True), axis=0, keepdims=True)
    dh, dg = _rms_bwd(h, gfin_ref[...], err * (1.0 / d))
    dh_ref[...] = dh
    _acc(loss_ref, part)
    _acc(dg_ref, dg)


def _ffn_fwd(x, pre_g, post_g, wgt, wut, wd, name, gather=None):
    s, d = x.shape
    f = wgt.shape[0]
    return _row_call(_ffn_fwd_body, name, s, [x], [pre_g, post_g, wgt, wut, wd],
                     [(d, F32), (d, BF16), (f, BF16), (f, BF16), (d, F32)], [], gather=gather, tm=FFN_TOKEN_TILE)


def _ffn_loss_fwd(x, target, pre_g, post_g, final_g, wgt, wut, wd, name):
    s, d = x.shape
    f = wgt.shape[0]
    return _row_call(_ffn_loss_body, name, s, [x, target], [pre_g, post_g, final_g, wgt, wut, wd],
                     [(d, BF16), (f, BF16), (f, BF16), (d, F32), (d, F32)],
                     [((1, 1), F32), ((1, d), F32)], tm=FFN_TOKEN_TILE)


def _ffn_bwd_body(dh_ref, x_ref, a_ref, b_ref, f_ref, pre_ref, post_ref, wgt_ref, wut_ref, wd_ref,
                  dx_ref, da_ref, db_ref, hm_ref, df_ref, dpre_ref, dpost_ref):
    dh = dh_ref[...]
    df, dpost = _rms_bwd(f_ref[...], post_ref[...], 0.5 * dh)
    dfb = df.astype(BF16)
    df_ref[...] = dfb
    dhmid = _dot_nt(dfb, wd_ref[...])
    a = a_ref[...].astype(F32)
    b = b_ref[...].astype(F32)
    sig = _sigmoid(a)
    sa = a * sig
    hm_ref[...] = (sa * b).astype(BF16)
    dab = (dhmid * b * sig * (1.0 + a * (1.0 - sig))).astype(BF16)
    dbb = (dhmid * sa).astype(BF16)
    da_ref[...] = dab
    db_ref[...] = dbb
    dn = _dot(dab, wgt_ref[...]) + _dot(dbb, wut_ref[...])
    dxn, dpre = _rms_bwd(x_ref[...], pre_ref[...], dn)
    dx_ref[...] = dh + dxn
    _acc(dpre_ref, dpre)
    _acc(dpost_ref, dpost)


def _ffn_bwd(dh, x, a, b, f, pre_g, post_g, wgt, wut, wd, name, scatter=None):
    s, d = x.shape
    ff = wgt.shape[0]
    return _row_call(_ffn_bwd_body, name, s, [dh, x, a, b, f], [pre_g, post_g, wgt, wut, wd],
                     [(d, F32), (ff, BF16), (ff, BF16), (ff, BF16), (d, BF16)],
                     [((1, d), F32), ((1, d), F32)], scatter=scatter)


def _inproj_fwd_body(h_ref, g_ref, wt_ref, n_ref, uv_ref, qkv_ref):
    n, _ = _rms(h_ref[...], g_ref[...])
    nb = n.astype(BF16)
    n_ref[...] = nb
    proj = _dot_nt(nb, wt_ref[...])
    nuv = uv_ref.shape[1]
    uv_ref[...] = proj[:, :nuv]
    qkv_ref[...] = proj[:, nuv:].astype(BF16)


def _inproj_fwd(h, g, w_in_t):
    s, d = h.shape
    sgu_w = SGU_GROUPS * GROUP_DIM
    sb_w = SB_HEADS * SB_HEAD_DIM
    return _row_call(_inproj_fwd_body, "inproj_fwd", s, [h], [g, w_in_t],
                     [(d, BF16), (2 * sgu_w, F32), (3 * sb_w, BF16)], [], tm=XA_TOKEN_TILE)


def _inproj_bwd_body(dh_ref, dproj_ref, h_ref, g_ref, wt_ref, dhout_ref, dg_ref):
    dn = _dot(dproj_ref[...], wt_ref[...])
    dhn, dg = _rms_bwd(h_ref[...], g_ref[...], dn)
    dhout_ref[...] = dh_ref[...] + dhn
    _acc(dg_ref, dg)


def _inproj_bwd(dh, dproj, h, g, w_in_t, scatter=None):
    s, d = h.shape
    return _row_call(_inproj_bwd_body, "inproj_bwd", s, [dh, dproj, h], [g, w_in_t],
                     [(d, F32)], [((1, d), F32)], scatter=scatter, tm=XA_TOKEN_TILE)


def _causal_w(ws_ref, g):
    row = lax.broadcasted_iota(jnp.int32, (CHUNK, CHUNK), 0)
    col = lax.broadcasted_iota(jnp.int32, (CHUNK, CHUNK), 1)
    return jnp.where(row >= col, ws_ref[g], 0.0), row >= col


def _group_norm(v):
    mu = jnp.mean(v, axis=-1, keepdims=True)
    d = v - mu
    rstd = lax.rsqrt(jnp.mean(d * d, axis=-1, keepdims=True) + EPS)
    return d * rstd, rstd


def _sgu_fwd_body(uv_ref, ng_ref, nb_ref, ws_ref, bs_ref, out_ref):
    width = SGU_GROUPS * GROUP_DIM
    for c in range(uv_ref.shape[0] // CHUNK):
        rows = pl.ds(c * CHUNK, CHUNK)
        for g in range(SGU_GROUPS):
            lanes = pl.ds(g * GROUP_DIM, GROUP_DIM)
            u = _gelu(uv_ref[rows, lanes])
            v = _gelu(uv_ref[rows, pl.ds(width + g * GROUP_DIM, GROUP_DIM)])
            vhat, _ = _group_norm(v)
            vn = vhat * ng_ref[:, lanes] + nb_ref[:, lanes]
            w, _ = _causal_w(ws_ref, g)
            mixed = _dot(w, vn) + bs_ref[g]
            out_ref[rows, lanes] = u * mixed


def _sgu_fwd(uv_pre, ng, nb, ws, bs):
    s = uv_pre.shape[0]
    return _row_call(_sgu_fwd_body, "sgu_fwd", s, [uv_pre], [ng, nb, ws, bs],
                     [(SGU_GROUPS * GROUP_DIM, F32)], [])[0]


def _sgu_bwd_body(uv_ref, do_ref, ng_ref, nb_ref, ws_ref, bs_ref,
                  duv_ref, dws_ref, dbs_ref, dng_ref, dnb_ref):
    width = SGU_GROUPS * GROUP_DIM

    @pl.when(pl.program_id(0) == 0)
    def _():
        dws_ref[...] = jnp.zeros_like(dws_ref)
        dbs_ref[...] = jnp.zeros_like(dbs_ref)
        dng_ref[...] = jnp.zeros_like(dng_ref)
        dnb_ref[...] = jnp.zeros_like(dnb_ref)

    for c in range(uv_ref.shape[0] // CHUNK):
        rows = pl.ds(c * CHUNK, CHUNK)
        for g in range(SGU_GROUPS):
            lanes = pl.ds(g * GROUP_DIM, GROUP_DIM)
            vlanes = pl.ds(width + g * GROUP_DIM, GROUP_DIM)
            u_pre = uv_ref[rows, lanes]
            v_pre = uv_ref[rows, vlanes]
            u = _gelu(u_pre)
            v = _gelu(v_pre)
            vhat, rstd = _group_norm(v)
            gain = ng_ref[:, lanes]
            vn = vhat * gain + nb_ref[:, lanes]
            w, causal = _causal_w(ws_ref, g)
            mixed = _dot(w, vn) + bs_ref[g]
            dout = do_ref[rows, lanes]
            du = dout * mixed
            dmixed = dout * u
            dbs_ref[g] += jnp.sum(dmixed, axis=1, keepdims=True)
            dws_ref[g] += jnp.where(causal, _dot_nt(dmixed, vn), 0.0)
            dvn = _dot_tn(w, dmixed)
            dng_ref[:, lanes] += jnp.sum(dvn * vhat, axis=0, keepdims=True)
            dnb_ref[:, lanes] += jnp.sum(dvn, axis=0, keepdims=True)
            dvh = dvn * gain
            dv = rstd * (dvh - jnp.mean(dvh, axis=-1, keepdims=True)
                         - vhat * jnp.mean(dvh * vhat, axis=-1, keepdims=True))
            duv_ref[rows, lanes] = (du * _gelu_grad(u_pre)).astype(BF16)
            duv_ref[rows, vlanes] = (dv * _gelu_grad(v_pre)).astype(BF16)


def _sgu_bwd(uv_pre, dout_a, ng, nb, ws, bs, scatter=None):
    s = uv_pre.shape[0]
    width = SGU_GROUPS * GROUP_DIM
    return _row_call(_sgu_bwd_body, "sgu_bwd", s, [uv_pre, dout_a], [ng, nb, ws, bs],
                     [(2 * width, BF16)],
                     [(ws.shape, F32), (bs.shape, F32), ((1, width), F32), ((1, width), F32)], scatter=scatter)


def _mesh_place():
    return lax.axis_index("x"), lax.axis_index("y"), lax.axis_index("c")


def _other_chips(mx, my):
    return [(1 - mx, my), (mx, 1 - my), (1 - mx, 1 - my)]


_ANY = pl.BlockSpec(memory_space=pl.ANY)
AG_SEMS = 8
_AG_SCRATCH = [pltpu.SemaphoreType.DMA((AG_SEMS,)), pltpu.SemaphoreType.DMA((AG_SEMS,)),
               pltpu.SemaphoreType.DMA(())]
ROW_ALIGN_ANY_DTYPE = 16


def _gather_phases(x_ref, out_ref, send_sems, recv_sems, local_sem):
    mx, my, mc = _mesh_place()
    me, sibling = (mx, my, mc), (mx, my, 1 - mc)
    x_chip, y_chip, far_chip = _other_chips(mx, my)
    rows = x_ref.shape[0]
    cut = (rows // (2 * ROW_ALIGN_ANY_DTYPE)) * ROW_ALIGN_ANY_DTYPE
    parts = {3: pl.ds(0, cut), 7: pl.ds(cut, rows - cut)}

    def slot(px, py, pc):
        return out_ref.at[4 * px + 2 * py + pc]

    def copy(k, block, to, src=None):
        where = slot(*block) if k not in parts else slot(*block).at[parts[k]]
        return pltpu.make_async_remote_copy(
            src_ref=where if src is None else src, dst_ref=where,
            send_sem=send_sems.at[k], recv_sem=recv_sems.at[k],
            device_id=to, device_id_type=MESH)

    mine = pltpu.make_async_copy(x_ref, slot(*me), local_sem)
    first = [copy(0, me, sibling, src=x_ref), copy(1, me, (*x_chip, mc), src=x_ref),
             copy(2, me, (*y_chip, mc), src=x_ref)]
    relayed = [copy(3, (*x_chip, mc), (*y_chip, mc)), copy(7, (*y_chip, mc), (*x_chip, mc))]
    passed = [copy(4, (*x_chip, mc), sibling), copy(5, (*y_chip, mc), sibling), copy(6, (*far_chip, mc), sibling)]

    def start():
        mine.start()
        for cp in first:
            cp.start()

    def relay():
        copy(1, (*x_chip, mc), me).wait_recv()
        relayed[0].start()
        passed[0].start()
        copy(2, (*y_chip, mc), me).wait_recv()
        relayed[1].start()
        passed[1].start()

    def forward():
        copy(3, (*far_chip, mc), me).wait_recv()
        copy(7, (*far_chip, mc), me).wait_recv()
        passed[2].start()

    def finish():
        copy(0, sibling, me).wait_recv()
        for k, chip in ((4, x_chip), (5, y_chip), (6, far_chip)):
            copy(k, (*chip, 1 - mc), me).wait_recv()
        for cp in first + relayed + passed:
            cp.wait_send()
        mine.wait()

    return start, relay, forward, finish


def _all_gather(x, name):
    r, c = x.shape

    def body(x_ref, out_ref, send_sems, recv_sems, local_sem):
        for phase in _gather_phases(x_ref, out_ref, send_sems, recv_sems, local_sem):
            phase()

    return pl.pallas_call(
        body,
        name=name,
        out_shape=jax.ShapeDtypeStruct((N_DEV, r, c), x.dtype),
        in_specs=[_ANY],
        out_specs=_ANY,
        scratch_shapes=list(_AG_SCRATCH),
    )(x)


def _scatter_phases(p_ref, out_ref, send_sems, recv_sems, local_sem):
    mx, my, mc = _mesh_place()
    me = 4 * mx + 2 * my + mc
    copies = []
    for k in range(1, N_DEV):
        tx, ty, tc = mx ^ ((k >> 2) & 1), my ^ ((k >> 1) & 1), mc ^ (k & 1)
        copies.append(pltpu.make_async_remote_copy(
            src_ref=p_ref.at[4 * tx + 2 * ty + tc], dst_ref=out_ref.at[me],
            send_sem=send_sems.at[k - 1], recv_sem=recv_sems.at[k - 1],
            device_id=(tx, ty, tc), device_id_type=MESH))
    mine = pltpu.make_async_copy(p_ref.at[me], out_ref.at[me], local_sem)

    def start():
        mine.start()
        for cp in copies:
            cp.start()

    def finish():
        for cp in copies:
            cp.wait()
        mine.wait()

    return start, finish


def _chip_exchange_phases(q_ref, out_ref, send_sems, recv_sems, local_sem):
    mx, my, mc = _mesh_place()
    copies = [pltpu.make_async_remote_copy(
        src_ref=q_ref.at[2 * cx + cy], dst_ref=out_ref.at[k],
        send_sem=send_sems.at[k], recv_sem=recv_sems.at[k],
        device_id=(cx, cy, mc), device_id_type=MESH)
        for k, (cx, cy) in enumerate(_other_chips(mx, my))]

    def start():
        for cp in copies:
            cp.start()

    def finish():
        for cp in copies:
            cp.wait()

    return start, finish


SB_DEAD = -105.0
HEADS_PER_TILE = 2
TILES_PER_STEP = 2
HEADS_PER_STEP = HEADS_PER_TILE * TILES_PER_STEP
STEP_LANES = TILES_PER_STEP * HEADS_PER_TILE * SB_HEAD_DIM
TILE_LANES = HEADS_PER_TILE * SB_HEAD_DIM
STACK_ROWS = HEADS_PER_STEP * Q_BLOCK
TILE_ROWS = HEADS_PER_TILE * Q_BLOCK
STACKS_PER_STEP = 2
GROUP_LANES = STACKS_PER_STEP * STEP_LANES
HEADS_PER_GROUP = STACKS_PER_STEP * HEADS_PER_STEP
SB_FORWARD_LEAD = 6


def _stack_heads(x):
    lane = lax.broadcasted_iota(jnp.int32, x.shape, 1)
    zero = jnp.zeros_like(x)
    return jnp.concatenate(
        [jnp.where(lane // SB_HEAD_DIM == h, x, zero) for h in range(HEADS_PER_STEP)], axis=0)


def _unstack_tile(x):
    first = lax.broadcasted_iota(jnp.int32, (Q_BLOCK, TILE_LANES), 1) < SB_HEAD_DIM
    return jnp.where(first, x[:Q_BLOCK], x[Q_BLOCK:])


def _sb_logs(qs, k, diagonal):
    z = _dot_nt(qs, k) * (SB_HEAD_DIM ** -0.5)
    sp = jnp.log1p(jnp.exp(-jnp.abs(z)))
    log_beta = jnp.minimum(z, 0.0) - sp
    log_1m_raw = -jnp.maximum(z, 0.0) - sp
    if not diagonal:
        return None, log_beta, log_1m_raw, log_1m_raw
    row = lax.broadcasted_iota(jnp.int32, z.shape, 0)
    col = lax.broadcasted_iota(jnp.int32, z.shape, 1)
    strict = col < jnp.bitwise_and(row, Q_BLOCK - 1)
    return strict, log_beta, log_1m_raw, jnp.where(strict, log_1m_raw, 0.0)


def _masked(strict, x):
    return x if strict is None else jnp.where(strict, x, 0.0)


def _key_sums(x, pick):
    hi, lo = _split_bf16(x)
    both = jnp.dot(jnp.concatenate([hi, lo], axis=0), pick, preferred_element_type=F32)
    return both[:x.shape[0]] + both[x.shape[0]:]


def _key_order():
    row = lax.broadcasted_iota(jnp.int32, (Q_BLOCK, Q_BLOCK), 0)
    col = lax.broadcasted_iota(jnp.int32, (Q_BLOCK, Q_BLOCK), 1)
    return row, col


def _sb_fwd_body(q_ref, k_ref, v_ref, shard_ref, o_ref, tot_ref, cnt_ref, gathered_ref,
                 acc_ref, send_sems, recv_sems, local_sem):
    grp, qb = pl.program_id(0), pl.program_id(1)
    last_grp, last_qb = pl.num_programs(0) - 1, pl.num_programs(1) - 1
    ag_start, ag_relay, ag_forward, ag_finish = _gather_phases(
        shard_ref, gathered_ref, send_sems, recv_sems, local_sem)
    pl.when(jnp.logical_and(grp == 0, qb == 0))(ag_start)

    stacks = range(STACKS_PER_STEP)
    lanes = [slice(st * STEP_LANES, (st + 1) * STEP_LANES) for st in stacks]
    qs = [_stack_heads(q_ref[:, lanes[st]]) for st in stacks]
    row, col = _key_order()
    later = (row > col).astype(BF16)

    def block(i, cs, diagonal):
        rows = pl.ds(pl.multiple_of((qb - i) * Q_BLOCK, Q_BLOCK), Q_BLOCK)
        new_cs = []
        for st in stacks:
            strict, log_beta, _, log_1m = _sb_logs(qs[st], k_ref[rows, lanes[st]], diagonal)
            a = _masked(strict, jnp.exp(log_beta + _key_sums(log_1m, later) + cs[st])).astype(BF16)
            for t in range(TILES_PER_STEP):
                tile = st * TILES_PER_STEP + t
                part = jnp.dot(a[t * TILE_ROWS:(t + 1) * TILE_ROWS],
                               v_ref[rows, tile * TILE_LANES:(tile + 1) * TILE_LANES],
                               preferred_element_type=F32)
                if diagonal:
                    acc_ref[tile] = part
                else:
                    acc_ref[tile] += part
            new_cs.append(cs[st] + jnp.sum(log_1m, axis=1, keepdims=True))
        return tuple(new_cs)

    cs = block(0, (jnp.zeros((STACK_ROWS, 1), F32),) * STACKS_PER_STEP, True)

    def alive(carry):
        i, cs = carry
        return jnp.logical_and(i <= qb, jnp.max(functools.reduce(jnp.maximum, cs)) > SB_DEAD)

    def step(carry):
        i, cs = carry
        return i + 1, block(i, cs, False)

    n, cs = lax.while_loop(alive, step, (jnp.int32(1), cs))
    for tile in range(STACKS_PER_STEP * TILES_PER_STEP):
        o_ref[:, tile * TILE_LANES:(tile + 1) * TILE_LANES] = _unstack_tile(acc_ref[tile])
    for st in stacks:
        for h in range(HEADS_PER_STEP):
            tot_ref[st * HEADS_PER_STEP + h] = cs[st][h * Q_BLOCK:(h + 1) * Q_BLOCK]
    cnt_ref[grp, qb] = n.astype(F32)
    place = grp * (last_qb + 1) + qb
    last = (last_grp + 1) * (last_qb + 1) - 1
    pl.when(place == (last + 1) // 2)(ag_relay)
    pl.when(place == jnp.maximum(last - SB_FORWARD_LEAD, (last + 1) // 2))(ag_forward)
    pl.when(place == last)(ag_finish)


def _sb_fwd(qkv, shard):
    s = qkv.shape[0]
    groups = SB_HEADS // HEADS_PER_GROUP
    nq = s // Q_BLOCK
    return pl.pallas_call(
        functools.partial(_sb_fwd_body),
        name="sb_fwd",
        grid=(groups, nq),
        in_specs=[pl.BlockSpec((Q_BLOCK, GROUP_LANES), lambda g, i: (i, g)),
                  pl.BlockSpec((s, GROUP_LANES), lambda g, i: (0, groups + g)),
                  pl.BlockSpec((s, GROUP_LANES), lambda g, i: (0, 2 * groups + g)),
                  _ANY],
        out_specs=[pl.BlockSpec((Q_BLOCK, GROUP_LANES), lambda g, i: (i, g)),
                   pl.BlockSpec((HEADS_PER_GROUP, Q_BLOCK, 1), lambda g, i: (g, i, 0)),
                   pl.BlockSpec(memory_space=pltpu.SMEM),
                   _ANY],
        out_shape=[jax.ShapeDtypeStruct((s, SB_HEADS * SB_HEAD_DIM), F32),
                   jax.ShapeDtypeStruct((SB_HEADS, s, 1), F32),
                   jax.ShapeDtypeStruct((groups, nq), F32),
                   jax.ShapeDtypeStruct((N_DEV,) + shard.shape, shard.dtype)],
        scratch_shapes=[pltpu.VMEM((STACKS_PER_STEP * TILES_PER_STEP, TILE_ROWS, TILE_LANES), F32)]
        + list(_AG_SCRATCH),
        compiler_params=pltpu.CompilerParams(
            dimension_semantics=("arbitrary", "arbitrary"), vmem_limit_bytes=VMEM_LIMIT_V7X),
    )(qkv, qkv, qkv, shard)


def _sb_bwd_body(cnt_ref, q_ref, k_ref, v_ref, tot_ref, do_ref, part_ref, dq_ref, dk_ref, dv_ref, recv_ref,
                 acc_ref, send_sems, recv_sems, local_sem):
    grp, qb = pl.program_id(0), pl.program_id(1)
    last_grp, last_qb = pl.num_programs(0) - 1, pl.num_programs(1) - 1
    rs_start, rs_finish = _scatter_phases(part_ref, recv_ref, send_sems, recv_sems, local_sem)
    pl.when(jnp.logical_and(grp == 0, qb == 0))(rs_start)

    @pl.when(qb == 0)
    def _():
        dk_ref[...] = jnp.zeros_like(dk_ref)
        dv_ref[...] = jnp.zeros_like(dv_ref)

    acc_ref[...] = jnp.zeros_like(acc_ref)
    stacks = range(STACKS_PER_STEP)
    lanes = [slice(st * STEP_LANES, (st + 1) * STEP_LANES) for st in stacks]
    qs = [_stack_heads(q_ref[:, lanes[st]]) for st in stacks]
    dos = [_stack_heads(do_ref[:, lanes[st]].astype(BF16)) for st in stacks]
    tots = [jnp.concatenate([tot_ref[st * HEADS_PER_STEP + h] for h in range(HEADS_PER_STEP)], axis=0)
            for st in stacks]
    row, col = _key_order()
    up_to = (row <= col).astype(BF16)
    earlier = (row < col).astype(BF16)
    scale = SB_HEAD_DIM ** -0.5
    n = jnp.clip(cnt_ref[grp, qb].astype(jnp.int32), 1, qb + 1)

    def block(kb, cs, ces, diagonal):
        rows = pl.ds(pl.multiple_of(kb * Q_BLOCK, Q_BLOCK), Q_BLOCK)
        new_cs, new_ces = [], []
        for st in stacks:
            k = k_ref[rows, lanes[st]]
            strict, log_beta, log_1m_raw, log_1m = _sb_logs(qs[st], k, diagonal)
            suffix = tots[st] - cs[st] - _key_sums(log_1m, up_to)
            a = _masked(strict, jnp.exp(log_beta + suffix))
            de = _dot_nt(dos[st], v_ref[rows, lanes[st]]) * a
            before = ces[st] + _key_sums(de, earlier)
            dz = _masked(strict, de * jnp.exp(log_1m_raw) - before * jnp.exp(log_beta)).astype(BF16)
            for t in range(TILES_PER_STEP):
                acc_ref[st * TILES_PER_STEP + t] += jnp.dot(
                    dz[t * TILE_ROWS:(t + 1) * TILE_ROWS], k[:, t * TILE_LANES:(t + 1) * TILE_LANES],
                    preferred_element_type=F32)
            dk_ref[rows, lanes[st]] += _dot_tn(dz, qs[st]) * scale
            dv_ref[rows, lanes[st]] += _dot_tn(a, dos[st])
            new_cs.append(cs[st] + jnp.sum(log_1m, axis=1, keepdims=True))
            new_ces.append(ces[st] + jnp.sum(de, axis=1, keepdims=True))
        return tuple(new_cs), tuple(new_ces)

    def step(i, carry):
        return block(qb - n + 1 + i, *carry, False)

    zc = (jnp.zeros((STACK_ROWS, 1), F32),) * STACKS_PER_STEP
    cs, ces = lax.fori_loop(0, n - 1, step, (zc, zc))
    block(qb, cs, ces, True)
    for tile in range(STACKS_PER_STEP * TILES_PER_STEP):
        dq_ref[:, tile * TILE_LANES:(tile + 1) * TILE_LANES] = _unstack_tile(acc_ref[tile]) * scale
    pl.when(jnp.logical_and(grp == last_grp, qb == last_qb))(rs_finish)


def _sb_bwd(cnt, qkv, tot, dout_b, parts):
    s = qkv.shape[0]
    groups = SB_HEADS // HEADS_PER_GROUP
    return pl.pallas_call(
        functools.partial(_sb_bwd_body),
        name="sb_bwd",
        grid=(groups, s // Q_BLOCK),
        in_specs=[pl.BlockSpec(memory_space=pltpu.SMEM),
                  pl.BlockSpec((Q_BLOCK, GROUP_LANES), lambda g, i: (i, g)),
                  pl.BlockSpec((s, GROUP_LANES), lambda g, i: (0, groups + g)),
                  pl.BlockSpec((s, GROUP_LANES), lambda g, i: (0, 2 * groups + g)),
                  pl.BlockSpec((HEADS_PER_GROUP, Q_BLOCK, 1), lambda g, i: (g, i, 0)),
                  pl.BlockSpec((Q_BLOCK, GROUP_LANES), lambda g, i: (i, g)),
                  _ANY],
        out_specs=[pl.BlockSpec((Q_BLOCK, GROUP_LANES), lambda g, i: (i, g)),
                   pl.BlockSpec((s, GROUP_LANES), lambda g, i: (0, g)),
                   pl.BlockSpec((s, GROUP_LANES), lambda g, i: (0, g)),
                   _ANY],
        out_shape=[jax.ShapeDtypeStruct((s, SB_HEADS * SB_HEAD_DIM), F32)] * 3
        + [jax.ShapeDtypeStruct(parts.shape, parts.dtype)],
        scratch_shapes=[pltpu.VMEM((STACKS_PER_STEP * TILES_PER_STEP, TILE_ROWS, TILE_LANES), F32)]
        + list(_AG_SCRATCH),
        compiler_params=pltpu.CompilerParams(
            dimension_semantics=("arbitrary", "arbitrary"), vmem_limit_bytes=VMEM_LIMIT_V7X),
    )(cnt, qkv, qkv, qkv, tot, dout_b, parts)


def _outproj_fwd_body(oa_ref, ob_ref, h_ref, ga_ref, gb_ref, gpost_ref, w_ref,
                      merged_ref, mo_ref, hout_ref):
    half = oa_ref.shape[1]
    ma, _ = _rms(oa_ref[...], ga_ref[...])
    mb, _ = _rms(ob_ref[...], gb_ref[...])
    mab = ma.astype(BF16)
    mbb = mb.astype(BF16)
    merged_ref[:, :half] = mab
    merged_ref[:, half:] = mbb
    mo = (jnp.dot(mab, w_ref[:half, :], preferred_element_type=F32)
          + jnp.dot(mbb, w_ref[half:, :], preferred_element_type=F32))
    mo_ref[...] = mo
    y, _ = _rms(mo, gpost_ref[...])
    hout_ref[...] = h_ref[...] + y


def _outproj_fwd(out_a, out_b, h, ga, gb, gpost, w_out):
    s, d = h.shape
    return _row_call(_outproj_fwd_body, "outproj_fwd", s, [out_a, out_b, h], [ga, gb, gpost, w_out],
                     [(d, BF16), (d, F32), (d, F32)], [], tm=XA_TOKEN_TILE)


def _outproj_bwd_body(dh_ref, mo_ref, oa_ref, ob_ref, ga_ref, gb_ref, gpost_ref, w_ref,
                      dmo_ref, doa_ref, dob_ref, dga_ref, dgb_ref, dgpost_ref):
    half = oa_ref.shape[1]
    dmo, dgpost = _rms_bwd(mo_ref[...], gpost_ref[...], dh_ref[...])
    dmob = dmo.astype(BF16)
    dmo_ref[...] = dmob
    dma = _dot_nt(dmob, w_ref[:half, :])
    dmb = _dot_nt(dmob, w_ref[half:, :])
    doa, dga = _rms_bwd(oa_ref[...], ga_ref[...], dma)
    dob, dgb = _rms_bwd(ob_ref[...], gb_ref[...], dmb)
    doa_ref[...] = doa
    dob_ref[...] = dob
    _acc(dga_ref, dga)
    _acc(dgb_ref, dgb)
    _acc(dgpost_ref, dgpost)


def _outproj_bwd(dh, mo, out_a, out_b, ga, gb, gpost, w_out):
    s, d = dh.shape
    half = out_a.shape[1]
    return _row_call(_outproj_bwd_body, "outproj_bwd", s, [dh, mo, out_a, out_b], [ga, gb, gpost, w_out],
                     [(d, BF16), (half, F32), (half, F32)],
                     [((1, half), F32), ((1, half), F32), ((1, d), F32)], tm=XA_TOKEN_TILE)


def _kv_fwd_body(mem_ref, g_ref, wt_ref, memn_ref, kv_ref):
    n, _ = _rms(mem_ref[...], g_ref[...])
    nb = n.astype(BF16)
    memn_ref[...] = nb
    kv_ref[...] = _dot_nt(nb, wt_ref[...]).astype(BF16)


def _kv_fwd(mem, g, w_kv_t):
    m, d = mem.shape
    return _row_call(_kv_fwd_body, "kv_fwd", m, [mem], [g, w_kv_t], [(d, BF16), (w_kv_t.shape[0], BF16)], [])


def _kv_bwd_body(dkv_ref, mem_ref, memn_ref, g_ref, wt_ref, dwt_ref, dg_ref):
    dkvb = dkv_ref[...].astype(BF16)
    dwt_ref[...] = _dot_tn(dkvb, memn_ref[...]).astype(BF16)
    dmemn = _dot(dkvb, wt_ref[...])
    _, dg = _rms_bwd(mem_ref[...], g_ref[...], dmemn)
    dg_ref[...] = dg


def _kv_bwd(dkv, mem, memn, g, w_kv_t):
    m, d = mem.shape
    return pl.pallas_call(
        functools.partial(_kv_bwd_body),
        name="kv_bwd",
        out_shape=[jax.ShapeDtypeStruct(w_kv_t.shape, BF16), jax.ShapeDtypeStruct((1, d), F32)],
        compiler_params=pltpu.CompilerParams(vmem_limit_bytes=VMEM_LIMIT_V7X),
    )(dkv, mem, memn, g, w_kv_t)


def _xa_fwd_body(h_ref, gpre_ref, gpost_ref, wq_ref, wo_ref, kv_ref,
                 n_ref, q_ref, o_ref, c_ref, hout_ref):
    h = h_ref[...]
    d = h.shape[1]
    n, _ = _rms(h, gpre_ref[...])
    nb = n.astype(BF16)
    n_ref[...] = nb
    qb = jnp.dot(nb, wq_ref[...], preferred_element_type=F32).astype(BF16)
    q_ref[...] = qb
    for hd in range(XA_HEADS):
        lanes = slice(hd * XA_HEAD_DIM, (hd + 1) * XA_HEAD_DIM)
        k = kv_ref[:, lanes]
        v = kv_ref[:, d + hd * XA_HEAD_DIM:d + (hd + 1) * XA_HEAD_DIM]
        logits = _dot_nt(qb[:, lanes], k) * (XA_HEAD_DIM ** -0.5)
        e = jnp.exp(logits - jnp.max(logits, axis=-1, keepdims=True))
        p = e / jnp.sum(e, axis=-1, keepdims=True)
        o_ref[:, lanes] = jnp.dot(p.astype(BF16), v, preferred_element_type=F32).astype(BF16)
    c = jnp.dot(o_ref[...], wo_ref[...], preferred_element_type=F32)
    c_ref[...] = c
    y, _ = _rms(c, gpost_ref[...])
    hout_ref[...] = h + y


def _xa_fwd(h, gpre, gpost, wq, wo, kv):
    s, d = h.shape
    return _row_call(_xa_fwd_body, "xa_fwd", s, [h], [gpre, gpost, wq, wo, kv],
                     [(d, BF16), (d, BF16), (d, BF16), (d, F32), (d, F32)], [], tm=XA_TOKEN_TILE)


def _xa_bwd_body(dh_ref, h_ref, c_ref, q_ref, o_ref, gpre_ref, gpost_ref, wq_ref, wo_ref, kv_ref,
                 dhout_ref, dc_ref, dq_ref, dkv_ref, dgpre_ref, dgpost_ref):
    dh = dh_ref[...]
    d = dh.shape[1]
    scale = XA_HEAD_DIM ** -0.5
    dc, dgpost = _rms_bwd(c_ref[...], gpost_ref[...], dh)
    dcb = dc.astype(BF16)
    dc_ref[...] = dcb
    dob = _dot_nt(dcb, wo_ref[...]).astype(BF16)

    @pl.when(pl.program_id(0) == 0)
    def _():
        dkv_ref[...] = jnp.zeros_like(dkv_ref)

    dqs, dks, dvs = [], [], []
    for hd in range(XA_HEADS):
        lanes = slice(hd * XA_HEAD_DIM, (hd + 1) * XA_HEAD_DIM)
        vlanes = slice(d + hd * XA_HEAD_DIM, d + (hd + 1) * XA_HEAD_DIM)
        qh = q_ref[:, lanes]
        k = kv_ref[:, lanes]
        v = kv_ref[:, vlanes]
        logits = _dot_nt(qh, k) * scale
        e = jnp.exp(logits - jnp.max(logits, axis=-1, keepdims=True))
        p = e / jnp.sum(e, axis=-1, keepdims=True)
        doh = dob[:, lanes]
        dp = _dot_nt(doh, v)
        dl = (p * (dp - jnp.sum(dp * p, axis=-1, keepdims=True)) * scale).astype(BF16)
        dqs.append(jnp.dot(dl, k, preferred_element_type=F32).astype(BF16))
        dks.append(_dot_tn(dl, qh))
        dvs.append(_dot_tn(p, doh))
    dq = jnp.concatenate(dqs, axis=1)
    dq_ref[...] = dq
    dkv_ref[...] += jnp.concatenate(dks + dvs, axis=1)
    dn = _dot_nt(dq, wq_ref[...])
    dhn, dgpre = _rms_bwd(h_ref[...], gpre_ref[...], dn)
    dhout_ref[...] = dh + dhn
    _acc(dgpre_ref, dgpre)
    _acc(dgpost_ref, dgpost)


def _xa_bwd(dh, h, c, q, o, gpre, gpost, wq, wo, kv, scatter=None):
    s, d = h.shape
    return _row_call(_xa_bwd_body, "xa_bwd", s, [dh, h, c, q, o], [gpre, gpost, wq, wo, kv],
                     [(d, F32), (d, BF16), (d, BF16)],
                     [(kv.shape, F32), ((1, d), F32), ((1, d), F32)], scatter=scatter, tm=XA_TOKEN_TILE)


def _largest_tile(n, cap):
    best = 128
    for t in range(128, cap + 1, 128):
        if n % t == 0:
            best = t
    return best


def _mm_tn(a, bs, name, gather=None, chip_exchange=None):
    s, k = a.shape
    n = bs[0].shape[1]
    nb = len(bs)
    ts = min(2048, s)
    tk = _largest_tile(k, 1536)
    tn = _largest_tile(n, 1536 // nb)

    steps = s // ts
    grid = (k // tk, n // tn, steps)
    sent = gather if gather is not None else chip_exchange
    hosted = sent is not None
    phases = _gather_phases if gather is not None else _chip_exchange_phases
    landed_shape = None
    if hosted:
        landed_shape = (N_DEV,) + sent.shape if gather is not None else (3,) + sent.shape[1:]

    def body(a_ref, *refs):
        if hosted:
            sent_ref, landed_ref, refs = refs[nb], refs[2 * nb + 1], refs[:nb] + refs[nb + 1:2 * nb + 1] + refs[2 * nb + 2:]
            start, *rest = phases(sent_ref, landed_ref, *refs[-3:])
            place = (pl.program_id(0) * grid[1] + pl.program_id(1)) * grid[2] + pl.program_id(2)
            pl.when(place == 0)(start)
        b_refs, o_refs, acc_refs = refs[:nb], refs[nb:2 * nb], refs[2 * nb:3 * nb]
        at = a_ref[...]
        t = pl.program_id(2)

        @pl.when(t == 0)
        def _():
            for acc_ref in acc_refs:
                acc_ref[...] = jnp.zeros_like(acc_ref)

        for b_ref, acc_ref in zip(b_refs, acc_refs):
            acc_ref[...] += _dot_tn(at, b_ref[...])

        @pl.when(t == steps - 1)
        def _():
            for o_ref, acc_ref in zip(o_refs, acc_refs):
                o_ref[...] = acc_ref[...].astype(BF16)

        if hosted:
            total = grid[0] * grid[1] * grid[2]

            @pl.when(place == total - 1)
            def _():
                for phase in rest:
                    phase()

    return pl.pallas_call(
        body,
        name=name,
        grid=grid,
        in_specs=[pl.BlockSpec((ts, tk), lambda i, j, t: (t, i))]
        + [pl.BlockSpec((ts, tn), lambda i, j, t: (t, j))] * nb + ([_ANY] if hosted else []),
        out_specs=[pl.BlockSpec((tk, tn), lambda i, j, t: (i, j))] * nb + ([_ANY] if hosted else []),
        out_shape=[jax.ShapeDtypeStruct((k, n), BF16)] * nb
        + ([jax.ShapeDtypeStruct(landed_shape, sent.dtype)] if hosted else []),
        scratch_shapes=[pltpu.VMEM((tk, tn), F32)] * nb + (list(_AG_SCRATCH) if hosted else []),
        compiler_params=pltpu.CompilerParams(
            dimension_semantics=("arbitrary", "arbitrary", "arbitrary"),
            vmem_limit_bytes=VMEM_LIMIT_V7X),
    )(a, *bs, *([sent] if hosted else []))


_SMALL_SHAPES = {
    "sgu_norm_g": (1, SGU_GROUPS * GROUP_DIM),
    "sgu_norm_b": (1, SGU_GROUPS * GROUP_DIM),
    "sgu_w_s": (SGU_GROUPS, CHUNK, CHUNK),
    "sgu_b_s": (SGU_GROUPS, CHUNK, 1),
}


def _small_views(small):
    return {n: v.reshape(_SMALL_SHAPES.get(n, v.shape)) for n, v in small.items()}


def _small_unviews(views, like):
    return {n: v.reshape(like[n].shape) for n, v in views.items()}


def _unpack_rows(gathered, names, shard_rows):
    out, off = {}, 0
    for n in names:
        rows = shard_rows[n]
        out[n] = gathered[:, off:off + rows, :].reshape(N_DEV * rows, gathered.shape[2])
        off += rows
    return out


def _row_tile(r, cap):
    best = 16
    for t in range(16, cap + 1, 16):
        if r % t == 0:
            best = t
    return best


def _sum_received(received, name):
    _, r, c = received.shape
    tr = _row_tile(r, 1024)

    def body(rc_ref, g_ref):
        g = rc_ref[0].astype(F32)
        for t in range(1, N_DEV):
            g = g + rc_ref[t].astype(F32)
        g_ref[...] = g

    return pl.pallas_call(
        body, name=name, grid=(r // tr,),
        in_specs=[pl.BlockSpec((N_DEV, tr, c), lambda i: (0, i, 0))],
        out_specs=pl.BlockSpec((tr, c), lambda i: (i, 0)),
        out_shape=jax.ShapeDtypeStruct((r, c), F32),
    )(received)


def _local_step(x, mem, target, small, big, shards, shard_rows):
    sm, w = small, dict(big)
    d_model = x.shape[1]

    def parts(names):
        return jnp.concatenate([gw.pop(n).reshape(N_DEV, -1, d_model) for n in names], axis=1)

    h1, n1, a1, b1, f1, landed = _ffn_fwd(
        x, sm["ffn1_pre_g"], sm["ffn1_post_g"], w["ffn1_w_gate"], w["ffn1_w_up"], w["ffn1_w_down"],
        "ffn1_fwd", gather=shards["ffn1_fwd"])
    w.update(_unpack_rows(landed, GATHER_IN["ffn1_fwd"], shard_rows))
    n2, uv_pre, qkv = _inproj_fwd(h1, sm["mix_pre_g"], w["w_in"])
    out_a = _sgu_fwd(uv_pre, sm["sgu_norm_g"], sm["sgu_norm_b"], sm["sgu_w_s"], sm["sgu_b_s"])
    out_b, tot, cnt, landed = _sb_fwd(qkv, shards["sb_fwd"])
    w.update(_unpack_rows(landed, GATHER_IN["sb_fwd"], shard_rows))
    merged, mo, h2 = _outproj_fwd(out_a, out_b, h1, sm["sgu_out_g"], sm["sb_out_g"],
                                  sm["mix_post_g"], w["w_out"])
    memn, kv = _kv_fwd(mem, sm["mem_norm_g"], w["xa_w_kv"])
    n3, qx, ox, cx, h3 = _xa_fwd(h2, sm["xa_pre_g"], sm["xa_post_g"], w["xa_w_q"], w["xa_w_o"], kv)
    n4, a2, b2, f2, dh4, loss, dg_final = _ffn_loss_fwd(
        h3, target, sm["ffn2_pre_g"], sm["ffn2_post_g"], sm["final_norm_g"],
        w["ffn2_w_gate"], w["ffn2_w_up"], w["ffn2_w_down"], "ffn2_fwd")

    gs, gw = {"final_norm_g": dg_final}, {}
    dh3, da2, db2, hm2, df2, gs["ffn2_pre_g"], gs["ffn2_post_g"] = _ffn_bwd(
        dh4, h3, a2, b2, f2, sm["ffn2_pre_g"], sm["ffn2_post_g"],
        w["ffn2_w_gate"], w["ffn2_w_up"], w["ffn2_w_down"], "ffn2_bwd")
    gw["ffn2_w_gate"], = _mm_tn(da2, [n4], "ffn2_dw_gate")
    gw["ffn2_w_up"], = _mm_tn(db2, [n4], "ffn2_dw_up")
    gw["ffn2_w_down"], = _mm_tn(hm2, [df2], "ffn2_dw_down")

    received = {}
    dh2, dc, dqx, dkv, gs["xa_pre_g"], gs["xa_post_g"], received["xa_bwd"] = _xa_bwd(
        dh3, h2, cx, qx, ox, sm["xa_pre_g"], sm["xa_post_g"], w["xa_w_q"], w["xa_w_o"], kv,
        scatter=parts(SCATTER_IN["xa_bwd"]))
    gw["xa_w_o"], = _mm_tn(ox, [dc], "xa_dw_o")
    gw["xa_w_q"], = _mm_tn(n3, [dqx], "xa_dw_q")
    gw["xa_w_kv"], gs["mem_norm_g"] = _kv_bwd(dkv, mem, memn, sm["mem_norm_g"], w["xa_w_kv"])

    dmo, dout_a, dout_b, gs["sgu_out_g"], gs["sb_out_g"], gs["mix_post_g"] = _outproj_bwd(
        dh2, mo, out_a, out_b, sm["sgu_out_g"], sm["sb_out_g"], sm["mix_post_g"], w["w_out"])
    gw["w_out"], = _mm_tn(merged, [dmo], "mix_dw_out")
    dq, dk, dv, received["sb_bwd"] = _sb_bwd(cnt, qkv, tot, dout_b, parts(SCATTER_IN["sb_bwd"]))
    duv, gs["sgu_w_s"], gs["sgu_b_s"], gs["sgu_norm_g"], gs["sgu_norm_b"], received["sgu_bwd"] = _sgu_bwd(
        uv_pre, dout_a, sm["sgu_norm_g"], sm["sgu_norm_b"], sm["sgu_w_s"], sm["sgu_b_s"],
        scatter=parts(SCATTER_IN["sgu_bwd"]))
    dproj = jnp.concatenate([duv] + [t.astype(BF16) for t in (dq, dk, dv)], axis=1)
    dh1, gs["mix_pre_g"], received["inproj_bwd"] = _inproj_bwd(
        dh2, dproj, h1, sm["mix_pre_g"], w["w_in"], scatter=parts(SCATTER_IN["inproj_bwd"]))
    gw["w_in"], = _mm_tn(dproj, [n2], "mix_dw_in")

    dx, da1, db1, hm1, df1, gs["ffn1_pre_g"], gs["ffn1_post_g"], received["ffn1_bwd"] = _ffn_bwd(
        dh1, x, a1, b1, f1, sm["ffn1_pre_g"], sm["ffn1_post_g"],
        w["ffn1_w_gate"], w["ffn1_w_up"], w["ffn1_w_down"], "ffn1_bwd",
        scatter=parts(SCATTER_IN["ffn1_bwd"]))
    summed = {host: _sum_received(r, "rs_sum_" + host) for host, r in received.items()}

    mx, my, mc = _mesh_place()
    place = jnp.stack([mc, 2 * mx + my]).astype(jnp.int32)

    def pair_stage(dw, tag):
        p4 = dw.reshape(N_DEV // 2, 2, -1, d_model)
        recv_a = _pair_exchange(p4, "rs_pair_exchange_" + tag)
        return p4, recv_a, _pair_sum(place, p4, recv_a, "rs_pair_sum_" + tag)

    dw, small_grads = _mm_tn(da1, [n1], "ffn1_dw_gate", gather=_pack_small(gs))
    gate = pair_stage(dw, "gate")
    dw, gate_b = _mm_tn(db1, [n1], "ffn1_dw_up", chip_exchange=gate[2])
    up = pair_stage(dw, "up")
    dw, up_b = _mm_tn(hm1, [df1], "ffn1_dw_down", chip_exchange=up[2])
    down = pair_stage(dw, "down")
    down_b = _chip_exchange(down[2], "rs_chip_exchange_down")
    for n, (p4, recv_a, _), recv_b in (("ffn1_w_gate", gate, gate_b), ("ffn1_w_up", up, up_b),
                                       ("ffn1_w_down", down, down_b)):
        summed[n] = _rs_final(place, p4, recv_a, recv_b, "rs_final_" + n)
    return loss, dx, small_grads, summed


def _pair_exchange(p4, name):
    nchip, _, r, c = p4.shape

    def body(p_ref, out_ref, send_sems, recv_sems):
        mx, my, mc = _mesh_place()
        copies = [pltpu.make_async_remote_copy(
            src_ref=p_ref.at[j, 1 - mc], dst_ref=out_ref.at[j],
            send_sem=send_sems.at[j], recv_sem=recv_sems.at[j],
            device_id=(mx, my, 1 - mc), device_id_type=MESH) for j in range(nchip)]
        for cp in copies:
            cp.start()
        for cp in copies:
            cp.wait()

    return pl.pallas_call(
        body,
        name=name,
        out_shape=jax.ShapeDtypeStruct((nchip, r, c), p4.dtype),
        in_specs=[_ANY],
        out_specs=_ANY,
        scratch_shapes=[pltpu.SemaphoreType.DMA((nchip,)), pltpu.SemaphoreType.DMA((nchip,))],
    )(p4)


def _chip_exchange(q, name):
    _, r, c = q.shape

    def body(q_ref, out_ref, send_sems, recv_sems, local_sem):
        for phase in _chip_exchange_phases(q_ref, out_ref, send_sems, recv_sems, local_sem):
            phase()

    return pl.pallas_call(
        body,
        name=name,
        out_shape=jax.ShapeDtypeStruct((3, r, c), q.dtype),
        in_specs=[_ANY],
        out_specs=_ANY,
        scratch_shapes=list(_AG_SCRATCH),
    )(q)


def _rs_row_tile(r):
    return _row_tile(r, 1024)


def _pair_sum(place, p4, recv_a, name):
    nchip, _, r, c = p4.shape
    tr = _rs_row_tile(r)

    def body(place_ref, p_ref, a_ref, q_ref):
        q_ref[0] = (p_ref[0, 0].astype(F32) + a_ref[0].astype(F32)).astype(BF16)

    return pl.pallas_call(
        body,
        name=name,
        grid_spec=pltpu.PrefetchScalarGridSpec(
            num_scalar_prefetch=1,
            grid=(nchip, r // tr),
            in_specs=[pl.BlockSpec((1, 1, tr, c), lambda j, i, pref: (j, pref[0], i, 0)),
                      pl.BlockSpec((1, tr, c), lambda j, i, pref: (j, i, 0))],
            out_specs=pl.BlockSpec((1, tr, c), lambda j, i, pref: (j, i, 0)),
        ),
        out_shape=jax.ShapeDtypeStruct((nchip, r, c), BF16),
    )(place, p4, recv_a)


def _rs_final(place, p4, recv_a, recv_b, name):
    _, _, r, c = p4.shape
    tr = _rs_row_tile(r)

    def body(place_ref, p_ref, a_ref, b_ref, g_ref):
        g = p_ref[0, 0].astype(F32) + a_ref[0].astype(F32)
        for k in range(3):
            g = g + b_ref[k].astype(F32)
        g_ref[...] = g

    return pl.pallas_call(
        body,
        name=name,
        grid_spec=pltpu.PrefetchScalarGridSpec(
            num_scalar_prefetch=1,
            grid=(r // tr,),
            in_specs=[pl.BlockSpec((1, 1, tr, c), lambda i, pref: (pref[1], pref[0], i, 0)),
                      pl.BlockSpec((1, tr, c), lambda i, pref: (pref[1], i, 0)),
                      pl.BlockSpec((3, tr, c), lambda i, pref: (0, i, 0))],
            out_specs=pl.BlockSpec((tr, c), lambda i, pref: (i, 0)),
        ),
        out_shape=jax.ShapeDtypeStruct((r, c), F32),
    )(place, p4, recv_a, recv_b)


def _adamw_math(w, g, m, v):
    m = ADAM_B1 * m + (1.0 - ADAM_B1) * g
    v = ADAM_B2 * v + (1.0 - ADAM_B2) * (g * g)
    m_hat = m / (1.0 - ADAM_B1 ** ADAM_STEP)
    v_hat = v / (1.0 - ADAM_B2 ** ADAM_STEP)
    delta = -ADAM_LR * (m_hat / (jnp.sqrt(v_hat) + ADAM_EPS) + ADAM_WD * w)
    return delta, m, v


def _adamw(w, g, m, v, name):
    r, c = w.shape
    tr = r if r <= 512 else 256

    def body(w_ref, g_ref, m_ref, v_ref, d_ref, mo_ref, vo_ref):
        d_ref[...], mo_ref[...], vo_ref[...] = _adamw_math(w_ref[...], g_ref[...], m_ref[...], v_ref[...])

    spec = pl.BlockSpec((tr, c), lambda i: (i, 0))
    out = jax.ShapeDtypeStruct((r, c), F32)
    return pl.pallas_call(
        body, name=name, grid=(r // tr,), in_specs=[spec] * 4, out_specs=[spec] * 3,
        out_shape=[out] * 3,
    )(w, g, m, v)


def _small_sum_adamw(gathered, ws, ms, vs):
    _, r, c = gathered.shape
    count = len(ws)

    def body(ga_ref, *refs):
        w_refs, m_refs, v_refs = refs[:count], refs[count:2 * count], refs[2 * count:3 * count]
        out_refs = refs[3 * count:]
        for o_ref in out_refs:
            o_ref[...] = jnp.zeros_like(o_ref)
        off = 0
        for w_ref, m_ref, v_ref in zip(w_refs, m_refs, v_refs):
            rows = pl.ds(off, w_ref.shape[0])
            g = ga_ref[0, rows, :]
            for k in range(1, N_DEV):
                g = g + ga_ref[k, rows, :]
            results = (g, *_adamw_math(w_ref[...], g, m_ref[...], v_ref[...]))
            for o_ref, val in zip(out_refs, results):
                o_ref[rows, :] = val
            off += w_ref.shape[0] + (-w_ref.shape[0]) % SMALL_ROW_ALIGN

    out = jax.ShapeDtypeStruct((r, c), F32)
    return pl.pallas_call(body, name="small_sum_adamw", out_shape=[out] * 4)(gathered, *ws, *ms, *vs)


_WEIGHTS = ["ffn1_pre_g", "ffn1_post_g", "ffn1_w_gate", "ffn1_w_up", "ffn1_w_down", "mix_pre_g",
            "mix_post_g", "w_in", "sgu_norm_g", "sgu_norm_b", "sgu_w_s", "sgu_b_s", "sgu_out_g",
            "sb_out_g", "w_out", "xa_pre_g", "xa_post_g", "mem_norm_g", "xa_w_q", "xa_w_kv", "xa_w_o",
            "ffn2_pre_g", "ffn2_post_g", "ffn2_w_gate", "ffn2_w_up", "ffn2_w_down", "final_norm_g"]
_BIG = ["ffn1_w_gate", "ffn1_w_up", "ffn1_w_down", "w_in", "w_out", "xa_w_q", "xa_w_kv", "xa_w_o",
        "ffn2_w_gate", "ffn2_w_up", "ffn2_w_down"]
_COL_SHARDED = ("ffn1_w_gate", "ffn1_w_up", "w_in", "xa_w_kv", "ffn2_w_gate", "ffn2_w_up")
_EARLY = ["ffn1_w_gate", "ffn1_w_up", "ffn1_w_down"]
GATHER_IN = {"ffn1_fwd": ["w_in", "ffn2_w_gate"],
             "sb_fwd": ["w_out", "xa_w_q", "xa_w_kv", "xa_w_o", "ffn2_w_up", "ffn2_w_down"]}
SCATTER_IN = {"xa_bwd": ["ffn2_w_gate"],
              "sb_bwd": ["ffn2_w_up", "ffn2_w_down", "xa_w_o", "w_out"],
              "sgu_bwd": ["xa_w_kv"],
              "inproj_bwd": ["xa_w_q"],
              "ffn1_bwd": ["w_in"]}
_SMALL = [n for n in _WEIGHTS if n not in _BIG]
SMALL_LANES = 128
SMALL_ROW_ALIGN = 8


def _pack_small(tensors):
    parts = []
    for n in _SMALL:
        t = tensors[n].reshape(-1, SMALL_LANES)
        pad = (-t.shape[0]) % SMALL_ROW_ALIGN
        parts.append(jnp.pad(t, ((0, pad), (0, 0))) if pad else t)
    return jnp.concatenate(parts, axis=0)


def _unpack_small(packed, like):
    out, off = {}, 0
    for n in _SMALL:
        size = like[n].size
        rows = size // SMALL_LANES
        out[n] = packed[off:off + rows].reshape(like[n].shape)
        off += rows + (-rows) % SMALL_ROW_ALIGN
    return out


def kernel(x, mem, ffn1_pre_g, ffn1_post_g, ffn1_w_gate, ffn1_w_up, ffn1_w_down, mix_pre_g, mix_post_g, w_in, sgu_norm_g, sgu_norm_b, sgu_w_s, sgu_b_s, sgu_out_g, sb_out_g, w_out, xa_pre_g, xa_post_g, mem_norm_g, xa_w_q, xa_w_kv, xa_w_o, ffn2_pre_g, ffn2_post_g, ffn2_w_gate, ffn2_w_up, ffn2_w_down, final_norm_g, loss_target, m_ffn1_pre_g, m_ffn1_post_g, m_ffn1_w_gate, m_ffn1_w_up, m_ffn1_w_down, m_mix_pre_g, m_mix_post_g, m_w_in, m_sgu_norm_g, m_sgu_norm_b, m_sgu_w_s, m_sgu_b_s, m_sgu_out_g, m_sb_out_g, m_w_out, m_xa_pre_g, m_xa_post_g, m_mem_norm_g, m_xa_w_q, m_xa_w_kv, m_xa_w_o, m_ffn2_pre_g, m_ffn2_post_g, m_ffn2_w_gate, m_ffn2_w_up, m_ffn2_w_down, m_final_norm_g, v_ffn1_pre_g, v_ffn1_post_g, v_ffn1_w_gate, v_ffn1_w_up, v_ffn1_w_down, v_mix_pre_g, v_mix_post_g, v_w_in, v_sgu_norm_g, v_sgu_norm_b, v_sgu_w_s, v_sgu_b_s, v_sgu_out_g, v_sb_out_g, v_w_out, v_xa_pre_g, v_xa_post_g, v_mem_norm_g, v_xa_w_q, v_xa_w_kv, v_xa_w_o, v_ffn2_pre_g, v_ffn2_post_g, v_ffn2_w_gate, v_ffn2_w_up, v_ffn2_w_down, v_final_norm_g):
    vals = dict(locals())
    d_model = x.shape[-1]

    def packed(names):
        return jnp.concatenate(
            [(vals[n][0].T if n in _COL_SHARDED else vals[n][0]).astype(BF16) for n in names], axis=0)

    shard_rows = {n: vals[n].shape[2 if n in _COL_SHARDED else 1] for n in _BIG}
    big = _unpack_rows(_all_gather(packed(_EARLY), "ag_weights"), _EARLY, shard_rows)

    small = {n: vals[n] for n in _SMALL}
    loss_part, dx, gathered_small, summed = _local_step(
        x[0], mem[0], loss_target[0], _small_views(small), big,
        {host: packed(names) for host, names in GATHER_IN.items()}, shard_rows)
    loss = lax.psum(loss_part[0, 0], ("x", "y", "c"))

    grads, deltas, new_m, new_v = {}, {}, {}, {}
    for names, g_rows in [([n], summed[n]) for n in _EARLY] + [(SCATTER_IN[h], summed[h]) for h in SCATTER_IN]:
        off = 0
        for n in names:
            rows = shard_rows[n]
            g = g_rows[off:off + rows]
            off += rows
            state = [vals[n][0], vals["m_" + n][0], vals["v_" + n][0]]
            flipped = n in _COL_SHARDED and rows % SMALL_LANES != 0
            if flipped:
                state = [t.T for t in state]
            elif n in _COL_SHARDED:
                g = g.T
            outs = (g, *_adamw(state[0], g, state[1], state[2], "adamw_" + n))
            if flipped:
                outs = tuple(t.T for t in outs)
            grads[n], deltas[n], new_m[n], new_v[n] = (t[None] for t in outs)

    outs = _small_sum_adamw(gathered_small,
                            *([vals[pre + n].reshape(-1, SMALL_LANES) for n in _SMALL] for pre in ("", "m_", "v_")))
    for dst, packed in zip((grads, deltas, new_m, new_v), outs):
        dst.update(_unpack_small(packed, small))

    return (loss, dx[None], *[grads[n] for n in _WEIGHTS], *[deltas[n] for n in _WEIGHTS],
            *[new_m[n] for n in _WEIGHTS], *[new_v[n] for n in _WEIGHTS])
```

```python
import functools

import jax
import jax.numpy as jnp
from jax import lax
from jax.experimental import pallas as pl
from jax.experimental.pallas import tpu as pltpu

F32 = jnp.float32
BF16 = jnp.bfloat16
EPS = 1e-6
MESH = pl.DeviceIdType.MESH
N_DEV = 8

SGU_GROUPS = 4
GROUP_DIM = 128
CHUNK = 128
SB_HEADS = 8
SB_HEAD_DIM = 64
Q_BLOCK = 128
XA_HEADS = 4
XA_HEAD_DIM = 256

ADAM_LR = 0.001
ADAM_B1 = 0.9
ADAM_B2 = 0.999
ADAM_EPS = 1e-08
ADAM_WD = 0.01
ADAM_STEP = 10

VMEM_LIMIT_V7X = 56 * 1024 * 1024
GELU_C0 = 0.7978845608028654
GELU_C1 = 0.044715


def _dot(a, b):
    return jnp.dot(a.astype(BF16), b.astype(BF16), preferred_element_type=F32)


def _dot_nt(a, b):
    return lax.dot_general(a.astype(BF16), b.astype(BF16), (((1,), (1,)), ((), ())),
                           preferred_element_type=F32)


def _dot_tn(a, b):
    return lax.dot_general(a.astype(BF16), b.astype(BF16), (((0,), (0,)), ((), ())),
                           preferred_element_type=F32)


def _rms(x, g):
    r = lax.rsqrt(jnp.mean(x * x, axis=-1, keepdims=True) + EPS)
    return x * r * g, r


def _rms_bwd(x, g, dy):
    r = lax.rsqrt(jnp.mean(x * x, axis=-1, keepdims=True) + EPS)
    xh = x * r
    gy = dy * g
    dx = r * (gy - xh * jnp.mean(gy * xh, axis=-1, keepdims=True))
    dg = jnp.sum(dy * xh, axis=0, keepdims=True)
    return dx, dg


def _sigmoid(x):
    return jax.nn.sigmoid(x)


def _gelu(x):
    t = jnp.tanh(GELU_C0 * (x + GELU_C1 * x * x * x))
    return 0.5 * x * (1.0 + t)


def _gelu_grad(x):
    t = jnp.tanh(GELU_C0 * (x + GELU_C1 * x * x * x))
    return 0.5 * (1.0 + t) + 0.5 * x * (1.0 - t * t) * GELU_C0 * (1.0 + 3.0 * GELU_C1 * x * x)


def _split_bf16(x):
    hi = x.astype(BF16)
    lo = (x - hi.astype(F32)).astype(BF16)
    return hi, lo


def _row_spec(tm, cols):
    return pl.BlockSpec((tm, cols), lambda i: (i, 0))


def _full_spec(shape, buffers=None):
    nd = len(shape)
    mode = None if buffers is None else pl.Buffered(buffers)
    return pl.BlockSpec(tuple(shape), lambda i: (0,) * nd, pipeline_mode=mode)


FFN_TOKEN_TILE = 512
XA_TOKEN_TILE = 512


def _token_tile(s):
    return min(256, s)


def _row_call(body, name, s, tiled_in, full_in, tiled_out, acc_out, gather=None, scatter=None, tm=None):
    tm = _token_tile(s) if tm is None else min(tm, s)
    steps = s // tm
    in_specs = [_row_spec(tm, a.shape[1]) for a in tiled_in] + [_full_spec(a.shape, buffers=1) for a in full_in]
    out_specs = [_row_spec(tm, c) for c, _ in tiled_out] + [_full_spec(sh) for sh, _ in acc_out]
    out_shape = [jax.ShapeDtypeStruct((s, c), dt) for c, dt in tiled_out]
    out_shape += [jax.ShapeDtypeStruct(sh, dt) for sh, dt in acc_out]
    operands = [*tiled_in, *full_in]
    scratch = []
    kernel_body = functools.partial(body)
    sent = gather if gather is not None else scatter
    if sent is not None:
        n_in, n_out = len(operands), len(out_shape)
        out_shape.append(jax.ShapeDtypeStruct(
            (N_DEV,) + sent.shape if gather is not None else sent.shape, sent.dtype))
        operands.append(sent)
        in_specs.append(_ANY)
        out_specs.append(_ANY)
        scratch = list(_AG_SCRATCH)

        def kernel_body(*refs):
            ins, sent_ref = refs[:n_in], refs[n_in]
            outs, landed_ref = refs[n_in + 1:n_in + 1 + n_out], refs[n_in + 1 + n_out]
            step = pl.program_id(0)
            if gather is not None:
                start, relay, forward, finish = _gather_phases(sent_ref, landed_ref, *refs[-3:])
            else:
                start, finish = _scatter_phases(sent_ref, landed_ref, *refs[-3:])
            pl.when(step == 0)(start)
            body(*ins, *outs)
            if gather is not None:
                pl.when(step == steps // 3)(relay)
                pl.when(step == (2 * steps) // 3)(forward)
            pl.when(step == steps - 1)(finish)

    return pl.pallas_call(
        kernel_body,
        name=name,
        grid=(steps,),
        in_specs=in_specs,
        out_specs=out_specs,
        out_shape=out_shape,
        scratch_shapes=scratch,
        compiler_params=pltpu.CompilerParams(
            dimension_semantics=("arbitrary",), vmem_limit_bytes=VMEM_LIMIT_V7X),
    )(*operands)


def _acc(ref, val):
    @pl.when(pl.program_id(0) == 0)
    def _():
        ref[...] = val

    @pl.when(pl.program_id(0) != 0)
    def _():
        ref[...] += val


def _ffn_fwd_tile(x_ref, pre_ref, post_ref, wgt_ref, wut_ref, wd_ref, n_ref, a_ref, b_ref, f_ref):
    x = x_ref[...]
    n, _ = _rms(x, pre_ref[...])
    nb = n.astype(BF16)
    n_ref[...] = nb
    a = _dot_nt(nb, wgt_ref[...])
    b = _dot_nt(nb, wut_ref[...])
    a_ref[...] = a.astype(BF16)
    b_ref[...] = b.astype(BF16)
    hmid = a * _sigmoid(a) * b
    f = jnp.dot(hmid.astype(BF16), wd_ref[...], preferred_element_type=F32)
    f_ref[...] = f
    y, _ = _rms(f, post_ref[...])
    return x + 0.5 * y


def _ffn_fwd_body(x_ref, pre_ref, post_ref, wgt_ref, wut_ref, wd_ref,
                  h_ref, n_ref, a_ref, b_ref, f_ref):
    h_ref[...] = _ffn_fwd_tile(x_ref, pre_ref, post_ref, wgt_ref, wut_ref, wd_ref, n_ref, a_ref, b_ref, f_ref)


def _ffn_loss_body(x_ref, t_ref, pre_ref, post_ref, gfin_ref, wgt_ref, wut_ref, wd_ref,
                   n_ref, a_ref, b_ref, f_ref, dh_ref, loss_ref, dg_ref):
    h = _ffn_fwd_tile(x_ref, pre_ref, post_ref, wgt_ref, wut_ref, wd_ref, n_ref, a_ref, b_ref, f_ref)
    d = h.shape[1]
    y, _ = _rms(h, gfin_ref[...])
    err = y - t_ref[...]
    part = (0.5 / d) * jnp.sum(jnp.sum(err * err, axis=1, keepdims=True), axis=0, keepdims=True)
    dh, dg = _rms_bwd(h, gfin_ref[...], err * (1.0 / d))
    dh_ref[...] = dh
    _acc(loss_ref, part)
    _acc(dg_ref, dg)


def _ffn_fwd(x, pre_g, post_g, wgt, wut, wd, name, gather=None):
    s, d = x.shape
    f = wgt.shape[0]
    return _row_call(_ffn_fwd_body, name, s, [x], [pre_g, post_g, wgt, wut, wd],
                     [(d, F32), (d, BF16), (f, BF16), (f, BF16), (d, F32)], [], gather=gather, tm=FFN_TOKEN_TILE)


def _ffn_loss_fwd(x, target, pre_g, post_g, final_g, wgt, wut, wd, name):
    s, d = x.shape
    f = wgt.shape[0]
    return _row_call(_ffn_loss_body, name, s, [x, target], [pre_g, post_g, final_g, wgt, wut, wd],
                     [(d, BF16), (f, BF16), (f, BF16), (d, F32), (d, F32)],
                     [((1, 1), F32), ((1, d), F32)], tm=FFN_TOKEN_TILE)


def _ffn_bwd_body(dh_ref, x_ref, a_ref, b_ref, f_ref, pre_ref, post_ref, wgt_ref, wut_ref, wd_ref,
                  dx_ref, da_ref, db_ref, hm_ref, df_ref, dpre_ref, dpost_ref):
    dh = dh_ref[...]
    df, dpost = _rms_bwd(f_ref[...], post_ref[...], 0.5 * dh)
    dfb = df.astype(BF16)
    df_ref[...] = dfb
    dhmid = _dot_nt(dfb, wd_ref[...])
    a = a_ref[...].astype(F32)
    b = b_ref[...].astype(F32)
    sig = _sigmoid(a)
    sa = a * sig
    hm_ref[...] = (sa * b).astype(BF16)
    dab = (dhmid * b * sig * (1.0 + a * (1.0 - sig))).astype(BF16)
    dbb = (dhmid * sa).astype(BF16)
    da_ref[...] = dab
    db_ref[...] = dbb
    dn = _dot(dab, wgt_ref[...]) + _dot(dbb, wut_ref[...])
    dxn, dpre = _rms_bwd(x_ref[...], pre_ref[...], dn)
    dx_ref[...] = dh + dxn
    _acc(dpre_ref, dpre)
    _acc(dpost_ref, dpost)


def _ffn_bwd(dh, x, a, b, f, pre_g, post_g, wgt, wut, wd, name, scatter=None):
    s, d = x.shape
    ff = wgt.shape[0]
    return _row_call(_ffn_bwd_body, name, s, [dh, x, a, b, f], [pre_g, post_g, wgt, wut, wd],
                     [(d, F32), (ff, BF16), (ff, BF16), (ff, BF16), (d, BF16)],
                     [((1, d), F32), ((1, d), F32)], scatter=scatter)


def _inproj_fwd_body(h_ref, g_ref, wt_ref, n_ref, uv_ref, qkv_ref):
    n, _ = _rms(h_ref[...], g_ref[...])
    nb = n.astype(BF16)
    n_ref[...] = nb
    proj = _dot_nt(nb, wt_ref[...])
    nuv = uv_ref.shape[1]
    uv_ref[...] = proj[:, :nuv]
    qkv_ref[...] = proj[:, nuv:].astype(BF16)


def _inproj_fwd(h, g, w_in_t):
    s, d = h.shape
    sgu_w = SGU_GROUPS * GROUP_DIM
    sb_w = SB_HEADS * SB_HEAD_DIM
    return _row_call(_inproj_fwd_body, "inproj_fwd", s, [h], [g, w_in_t],
                     [(d, BF16), (2 * sgu_w, F32), (3 * sb_w, BF16)], [], tm=XA_TOKEN_TILE)


def _inproj_bwd_body(dh_ref, dproj_ref, h_ref, g_ref, wt_ref, dhout_ref, dg_ref):
    dn = _dot(dproj_ref[...], wt_ref[...])
    dhn, dg = _rms_bwd(h_ref[...], g_ref[...], dn)
    dhout_ref[...] = dh_ref[...] + dhn
    _acc(dg_ref, dg)


def _inproj_bwd(dh, dproj, h, g, w_in_t, scatter=None):
    s, d = h.shape
    return _row_call(_inproj_bwd_body, "inproj_bwd", s, [dh, dproj, h], [g, w_in_t],
                     [(d, F32)], [((1, d), F32)], scatter=scatter, tm=XA_TOKEN_TILE)


def _causal_w(ws_ref, g):
    row = lax.broadcasted_iota(jnp.int32, (CHUNK, CHUNK), 0)
    col = lax.broadcasted_iota(jnp.int32, (CHUNK, CHUNK), 1)
    return jnp.where(row >= col, ws_ref[g], 0.0), row >= col


def _group_norm(v):
    mu = jnp.mean(v, axis=-1, keepdims=True)
    d = v - mu
    rstd = lax.rsqrt(jnp.mean(d * d, axis=-1, keepdims=True) + EPS)
    return d * rstd, rstd


def _sgu_fwd_body(uv_ref, ng_ref, nb_ref, ws_ref, bs_ref, out_ref):
    width = SGU_GROUPS * GROUP_DIM
    for c in range(uv_ref.shape[0] // CHUNK):
        rows = pl.ds(c * CHUNK, CHUNK)
        for g in range(SGU_GROUPS):
            lanes = pl.ds(g * GROUP_DIM, GROUP_DIM)
            u = _gelu(uv_ref[rows, lanes])
            v = _gelu(uv_ref[rows, pl.ds(width + g * GROUP_DIM, GROUP_DIM)])
            vhat, _ = _group_norm(v)
            vn = vhat * ng_ref[:, lanes] + nb_ref[:, lanes]
            w, _ = _causal_w(ws_ref, g)
            mixed = _dot(w, vn) + bs_ref[g]
            out_ref[rows, lanes] = u * mixed


def _sgu_fwd(uv_pre, ng, nb, ws, bs):
    s = uv_pre.shape[0]
    return _row_call(_sgu_fwd_body, "sgu_fwd", s, [uv_pre], [ng, nb, ws, bs],
                     [(SGU_GROUPS * GROUP_DIM, F32)], [])[0]


def _sgu_bwd_body(uv_ref, do_ref, ng_ref, nb_ref, ws_ref, bs_ref,
                  duv_ref, dws_ref, dbs_ref, dng_ref, dnb_ref):
    width = SGU_GROUPS * GROUP_DIM

    @pl.when(pl.program_id(0) == 0)
    def _():
        dws_ref[...] = jnp.zeros_like(dws_ref)
        dbs_ref[...] = jnp.zeros_like(dbs_ref)
        dng_ref[...] = jnp.zeros_like(dng_ref)
        dnb_ref[...] = jnp.zeros_like(dnb_ref)

    for c in range(uv_ref.shape[0] // CHUNK):
        rows = pl.ds(c * CHUNK, CHUNK)
        for g in range(SGU_GROUPS):
            lanes = pl.ds(g * GROUP_DIM, GROUP_DIM)
            vlanes = pl.ds(width + g * GROUP_DIM, GROUP_DIM)
            u_pre = uv_ref[rows, lanes]
            v_pre = uv_ref[rows, vlanes]
            u = _gelu(u_pre)
            v = _gelu(v_pre)
            vhat, rstd = _group_norm(v)
            gain = ng_ref[:, lanes]
            vn = vhat * gain + nb_ref[:, lanes]
            w, causal = _causal_w(ws_ref, g)
            mixed = _dot(w, vn) + bs_ref[g]
            dout = do_ref[rows, lanes]
            du = dout * mixed
            dmixed = dout * u
            dbs_ref[g] += jnp.sum(dmixed, axis=1, keepdims=True)
            dws_ref[g] += jnp.where(causal, _dot_nt(dmixed, vn), 0.0)
            dvn = _dot_tn(w, dmixed)
            dng_ref[:, lanes] += jnp.sum(dvn * vhat, axis=0, keepdims=True)
            dnb_ref[:, lanes] += jnp.sum(dvn, axis=0, keepdims=True)
            dvh = dvn * gain
            dv = rstd * (dvh - jnp.mean(dvh, axis=-1, keepdims=True)
                         - vhat * jnp.mean(dvh * vhat, axis=-1, keepdims=True))
            duv_ref[rows, lanes] = (du * _gelu_grad(u_pre)).astype(BF16)
            duv_ref[rows, vlanes] = (dv * _gelu_grad(v_pre)).astype(BF16)


def _sgu_bwd(uv_pre, dout_a, ng, nb, ws, bs, scatter=None):
    s = uv_pre.shape[0]
    width = SGU_GROUPS * GROUP_DIM
    return _row_call(_sgu_bwd_body, "sgu_bwd", s, [uv_pre, dout_a], [ng, nb, ws, bs],
                     [(2 * width, BF16)],
                     [(ws.shape, F32), (bs.shape, F32), ((1, width), F32), ((1, width), F32)], scatter=scatter)


def _mesh_place():
    return lax.axis_index("x"), lax.axis_index("y"), lax.axis_index("c")


def _other_chips(mx, my):
    return [(1 - mx, my), (mx, 1 - my), (1 - mx, 1 - my)]


_ANY = pl.BlockSpec(memory_space=pl.ANY)
AG_SEMS = 8
_AG_SCRATCH = [pltpu.SemaphoreType.DMA((AG_SEMS,)), pltpu.SemaphoreType.DMA((AG_SEMS,)),
               pltpu.SemaphoreType.DMA(())]
ROW_ALIGN_ANY_DTYPE = 16


def _gather_phases(x_ref, out_ref, send_sems, recv_sems, local_sem):
    mx, my, mc = _mesh_place()
    me, sibling = (mx, my, mc), (mx, my, 1 - mc)
    x_chip, y_chip, far_chip = _other_chips(mx, my)
    rows = x_ref.shape[0]
    cut = (rows // (2 * ROW_ALIGN_ANY_DTYPE)) * ROW_ALIGN_ANY_DTYPE
    parts = {3: pl.ds(0, cut), 7: pl.ds(cut, rows - cut)}

    def slot(px, py, pc):
        return out_ref.at[4 * px + 2 * py + pc]

    def copy(k, block, to, src=None):
        where = slot(*block) if k not in parts else slot(*block).at[parts[k]]
        return pltpu.make_async_remote_copy(
            src_ref=where if src is None else src, dst_ref=where,
            send_sem=send_sems.at[k], recv_sem=recv_sems.at[k],
            device_id=to, device_id_type=MESH)

    mine = pltpu.make_async_copy(x_ref, slot(*me), local_sem)
    first = [copy(0, me, sibling, src=x_ref), copy(1, me, (*x_chip, mc), src=x_ref),
             copy(2, me, (*y_chip, mc), src=x_ref)]
    relayed = [copy(3, (*x_chip, mc), (*y_chip, mc)), copy(7, (*y_chip, mc), (*x_chip, mc))]
    passed = [copy(4, (*x_chip, mc), sibling), copy(5, (*y_chip, mc), sibling), copy(6, (*far_chip, mc), sibling)]

    def start():
        mine.start()
        for cp in first:
            cp.start()

    def relay():
        copy(1, (*x_chip, mc), me).wait_recv()
        relayed[0].start()
        passed[0].start()
        copy(2, (*y_chip, mc), me).wait_recv()
        relayed[1].start()
        passed[1].start()

    def forward():
        copy(3, (*far_chip, mc), me).wait_recv()
        copy(7, (*far_chip, mc), me).wait_recv()
        passed[2].start()

    def finish():
        copy(0, sibling, me).wait_recv()
        for k, chip in ((4, x_chip), (5, y_chip), (6, far_chip)):
            copy(k, (*chip, 1 - mc), me).wait_recv()
        for cp in first + relayed + passed:
            cp.wait_send()
        mine.wait()

    return start, relay, forward, finish


def _all_gather(x, name):
    r, c = x.shape

    def body(x_ref, out_ref, send_sems, recv_sems, local_sem):
        for phase in _gather_phases(x_ref, out_ref, send_sems, recv_sems, local_sem):
            phase()

    return pl.pallas_call(
        body,
        name=name,
        out_shape=jax.ShapeDtypeStruct((N_DEV, r, c), x.dtype),
        in_specs=[_ANY],
        out_specs=_ANY,
        scratch_shapes=list(_AG_SCRATCH),
    )(x)


def _scatter_phases(p_ref, out_ref, send_sems, recv_sems, local_sem):
    mx, my, mc = _mesh_place()
    me = 4 * mx + 2 * my + mc
    copies = []
    for k in range(1, N_DEV):
        tx, ty, tc = mx ^ ((k >> 2) & 1), my ^ ((k >> 1) & 1), mc ^ (k & 1)
        copies.append(pltpu.make_async_remote_copy(
            src_ref=p_ref.at[4 * tx + 2 * ty + tc], dst_ref=out_ref.at[me],
            send_sem=send_sems.at[k - 1], recv_sem=recv_sems.at[k - 1],
            device_id=(tx, ty, tc), device_id_type=MESH))
    mine = pltpu.make_async_copy(p_ref.at[me], out_ref.at[me], local_sem)

    def start():
        mine.start()
        for cp in copies:
            cp.start()

    def finish():
        for cp in copies:
            cp.wait()
        mine.wait()

    return start, finish


def _chip_exchange_phases(q_ref, out_ref, send_sems, recv_sems, local_sem):
    mx, my, mc = _mesh_place()
    copies = [pltpu.make_async_remote_copy(
        src_ref=q_ref.at[2 * cx + cy], dst_ref=out_ref.at[k],
        send_sem=send_sems.at[k], recv_sem=recv_sems.at[k],
        device_id=(cx, cy, mc), device_id_type=MESH)
        for k, (cx, cy) in enumerate(_other_chips(mx, my))]

    def start():
        for cp in copies:
            cp.start()

    def finish():
        for cp in copies:
            cp.wait()

    return start, finish


SB_DEAD = -105.0
HEADS_PER_TILE = 2
TILES_PER_STEP = 2
HEADS_PER_STEP = HEADS_PER_TILE * TILES_PER_STEP
STEP_LANES = TILES_PER_STEP * HEADS_PER_TILE * SB_HEAD_DIM
TILE_LANES = HEADS_PER_TILE * SB_HEAD_DIM
STACK_ROWS = HEADS_PER_STEP * Q_BLOCK
TILE_ROWS = HEADS_PER_TILE * Q_BLOCK
STACKS_PER_STEP = 2
GROUP_LANES = STACKS_PER_STEP * STEP_LANES
HEADS_PER_GROUP = STACKS_PER_STEP * HEADS_PER_STEP
SB_FORWARD_LEAD = 6


def _stack_heads(x):
    lane = lax.broadcasted_iota(jnp.int32, x.shape, 1)
    zero = jnp.zeros_like(x)
    return jnp.concatenate(
        [jnp.where(lane // SB_HEAD_DIM == h, x, zero) for h in range(HEADS_PER_STEP)], axis=0)


def _unstack_tile(x):
    first = lax.broadcasted_iota(jnp.int32, (Q_BLOCK, TILE_LANES), 1) < SB_HEAD_DIM
    return jnp.where(first, x[:Q_BLOCK], x[Q_BLOCK:])


def _sb_logs(qs, k, diagonal):
    z = _dot_nt(qs, k) * (SB_HEAD_DIM ** -0.5)
    sp = jnp.log1p(jnp.exp(-jnp.abs(z)))
    log_beta = jnp.minimum(z, 0.0) - sp
    log_1m_raw = -jnp.maximum(z, 0.0) - sp
    if not diagonal:
        return None, log_beta, log_1m_raw, log_1m_raw
    row = lax.broadcasted_iota(jnp.int32, z.shape, 0)
    col = lax.broadcasted_iota(jnp.int32, z.shape, 1)
    strict = col < jnp.bitwise_and(row, Q_BLOCK - 1)
    return strict, log_beta, log_1m_raw, jnp.where(strict, log_1m_raw, 0.0)


def _masked(strict, x):
    return x if strict is None else jnp.where(strict, x, 0.0)


def _key_sums(x, pick):
    hi, lo = _split_bf16(x)
    both = jnp.dot(jnp.concatenate([hi, lo], axis=0), pick, preferred_element_type=F32)
    return both[:x.shape[0]] + both[x.shape[0]:]


def _key_order():
    row = lax.broadcasted_iota(jnp.int32, (Q_BLOCK, Q_BLOCK), 0)
    col = lax.broadcasted_iota(jnp.int32, (Q_BLOCK, Q_BLOCK), 1)
    return row, col


def _sb_fwd_body(q_ref, k_ref, v_ref, shard_ref, o_ref, tot_ref, cnt_ref, gathered_ref,
                 acc_ref, send_sems, recv_sems, local_sem):
    grp, qb = pl.program_id(0), pl.program_id(1)
    last_grp, last_qb = pl.num_programs(0) - 1, pl.num_programs(1) - 1
    ag_start, ag_relay, ag_forward, ag_finish = _gather_phases(
        shard_ref, gathered_ref, send_sems, recv_sems, local_sem)
    pl.when(jnp.logical_and(grp == 0, qb == 0))(ag_start)

    stacks = range(STACKS_PER_STEP)
    lanes = [slice(st * STEP_LANES, (st + 1) * STEP_LANES) for st in stacks]
    qs = [_stack_heads(q_ref[:, lanes[st]]) for st in stacks]
    row, col = _key_order()
    later = (row > col).astype(BF16)

    def block(i, cs, diagonal):
        rows = pl.ds(pl.multiple_of((qb - i) * Q_BLOCK, Q_BLOCK), Q_BLOCK)
        new_cs = []
        for st in stacks:
            strict, log_beta, _, log_1m = _sb_logs(qs[st], k_ref[rows, lanes[st]], diagonal)
            a = _masked(strict, jnp.exp(log_beta + _key_sums(log_1m, later) + cs[st])).astype(BF16)
            for t in range(TILES_PER_STEP):
                tile = st * TILES_PER_STEP + t
                part = jnp.dot(a[t * TILE_ROWS:(t + 1) * TILE_ROWS],
                               v_ref[rows, tile * TILE_LANES:(tile + 1) * TILE_LANES],
                               preferred_element_type=F32)
                if diagonal:
                    acc_ref[tile] = part
                else:
                    acc_ref[tile] += part
            new_cs.append(cs[st] + jnp.sum(log_1m, axis=1, keepdims=True))
        return tuple(new_cs)

    cs = block(0, (jnp.zeros((STACK_ROWS, 1), F32),) * STACKS_PER_STEP, True)

    def alive(carry):
        i, cs = carry
        return jnp.logical_and(i <= qb, jnp.max(functools.reduce(jnp.maximum, cs)) > SB_DEAD)

    def step(carry):
        i, cs = carry
        return i + 1, block(i, cs, False)

    n, cs = lax.while_loop(alive, step, (jnp.int32(1), cs))
    for tile in range(STACKS_PER_STEP * TILES_PER_STEP):
        o_ref[:, tile * TILE_LANES:(tile + 1) * TILE_LANES] = _unstack_tile(acc_ref[tile])
    for st in stacks:
        for h in range(HEADS_PER_STEP):
            tot_ref[st * HEADS_PER_STEP + h] = cs[st][h * Q_BLOCK:(h + 1) * Q_BLOCK]
    cnt_ref[grp, qb] = n.astype(F32)
    place = grp * (last_qb + 1) + qb
    last = (last_grp + 1) * (last_qb + 1) - 1
    pl.when(place == (last + 1) // 2)(ag_relay)
    pl.when(place == jnp.maximum(last - SB_FORWARD_LEAD, (last + 1) // 2))(ag_forward)
    pl.when(place == last)(ag_finish)


def _sb_fwd(qkv, shard):
    s = qkv.shape[0]
    groups = SB_HEADS // HEADS_PER_GROUP
    nq = s // Q_BLOCK
    return pl.pallas_call(
        functools.partial(_sb_fwd_body),
        name="sb_fwd",
        grid=(groups, nq),
        in_specs=[pl.BlockSpec((Q_BLOCK, GROUP_LANES), lambda g, i: (i, g)),
                  pl.BlockSpec((s, GROUP_LANES), lambda g, i: (0, groups + g)),
                  pl.BlockSpec((s, GROUP_LANES), lambda g, i: (0, 2 * groups + g)),
                  _ANY],
        out_specs=[pl.BlockSpec((Q_BLOCK, GROUP_LANES), lambda g, i: (i, g)),
                   pl.BlockSpec((HEADS_PER_GROUP, Q_BLOCK, 1), lambda g, i: (g, i, 0)),
                   pl.BlockSpec(memory_space=pltpu.SMEM),
                   _ANY],
        out_shape=[jax.ShapeDtypeStruct((s, SB_HEADS * SB_HEAD_DIM), F32),
                   jax.ShapeDtypeStruct((SB_HEADS, s, 1), F32),
                   jax.ShapeDtypeStruct((groups, nq), F32),
                   jax.ShapeDtypeStruct((N_DEV,) + shard.shape, shard.dtype)],
        scratch_shapes=[pltpu.VMEM((STACKS_PER_STEP * TILES_PER_STEP, TILE_ROWS, TILE_LANES), F32)]
        + list(_AG_SCRATCH),
        compiler_params=pltpu.CompilerParams(
            dimension_semantics=("arbitrary", "arbitrary"), vmem_limit_bytes=VMEM_LIMIT_V7X),
    )(qkv, qkv, qkv, shard)


def _sb_bwd_body(cnt_ref, q_ref, k_ref, v_ref, tot_ref, do_ref, part_ref, dq_ref, dk_ref, dv_ref, recv_ref,
                 acc_ref, send_sems, recv_sems, local_sem):
    grp, qb = pl.program_id(0), pl.program_id(1)
    last_grp, last_qb = pl.num_programs(0) - 1, pl.num_programs(1) - 1
    rs_start, rs_finish = _scatter_phases(part_ref, recv_ref, send_sems, recv_sems, local_sem)
    pl.when(jnp.logical_and(grp == 0, qb == 0))(rs_start)

    @pl.when(qb == 0)
    def _():
        dk_ref[...] = jnp.zeros_like(dk_ref)
        dv_ref[...] = jnp.zeros_like(dv_ref)

    acc_ref[...] = jnp.zeros_like(acc_ref)
    stacks = range(STACKS_PER_STEP)
    lanes = [slice(st * STEP_LANES, (st + 1) * STEP_LANES) for st in stacks]
    qs = [_stack_heads(q_ref[:, lanes[st]]) for st in stacks]
    dos = [_stack_heads(do_ref[:, lanes[st]].astype(BF16)) for st in stacks]
    tots = [jnp.concatenate([tot_ref[st * HEADS_PER_STEP + h] for h in range(HEADS_PER_STEP)], axis=0)
            for st in stacks]
    row, col = _key_order()
    up_to = (row <= col).astype(BF16)
    earlier = (row < col).astype(BF16)
    scale = SB_HEAD_DIM ** -0.5
    n = jnp.clip(cnt_ref[grp, qb].astype(jnp.int32), 1, qb + 1)

    def block(kb, cs, ces, diagonal):
        rows = pl.ds(pl.multiple_of(kb * Q_BLOCK, Q_BLOCK), Q_BLOCK)
        new_cs, new_ces = [], []
        for st in stacks:
            k = k_ref[rows, lanes[st]]
            strict, log_beta, log_1m_raw, log_1m = _sb_logs(qs[st], k, diagonal)
            suffix = tots[st] - cs[st] - _key_sums(log_1m, up_to)
            a = _masked(strict, jnp.exp(log_beta + suffix))
            de = _dot_nt(dos[st], v_ref[rows, lanes[st]]) * a
            before = ces[st] + _key_sums(de, earlier)
            dz = _masked(strict, de * jnp.exp(log_1m_raw) - before * jnp.exp(log_beta)).astype(BF16)
            for t in range(TILES_PER_STEP):
                acc_ref[st * TILES_PER_STEP + t] += jnp.dot(
                    dz[t * TILE_ROWS:(t + 1) * TILE_ROWS], k[:, t * TILE_LANES:(t + 1) * TILE_LANES],
                    preferred_element_type=F32)
            dk_ref[rows, lanes[st]] += _dot_tn(dz, qs[st]) * scale
            dv_ref[rows, lanes[st]] += _dot_tn(a, dos[st])
            new_cs.append(cs[st] + jnp.sum(log_1m, axis=1, keepdims=True))
            new_ces.append(ces[st] + jnp.sum(de, axis=1, keepdims=True))
        return tuple(new_cs), tuple(new_ces)

    def step(i, carry):
        return block(qb - n + 1 + i, *carry, False)

    zc = (jnp.zeros((STACK_ROWS, 1), F32),) * STACKS_PER_STEP
    cs, ces = lax.fori_loop(0, n - 1, step, (zc, zc))
    block(qb, cs, ces, True)
    for tile in range(STACKS_PER_STEP * TILES_PER_STEP):
        dq_ref[:, tile * TILE_LANES:(tile + 1) * TILE_LANES] = _unstack_tile(acc_ref[tile]) * scale
    pl.when(jnp.logical_and(grp == last_grp, qb == last_qb))(rs_finish)


def _sb_bwd(cnt, qkv, tot, dout_b, parts):
    s = qkv.shape[0]
    groups = SB_HEADS // HEADS_PER_GROUP
    return pl.pallas_call(
        functools.partial(_sb_bwd_body),
        name="sb_bwd",
        grid=(groups, s // Q_BLOCK),
        in_specs=[pl.BlockSpec(memory_space=pltpu.SMEM),
                  pl.BlockSpec((Q_BLOCK, GROUP_LANES), lambda g, i: (i, g)),
                  pl.BlockSpec((s, GROUP_LANES), lambda g, i: (0, groups + g)),
                  pl.BlockSpec((s, GROUP_LANES), lambda g, i: (0, 2 * groups + g)),
                  pl.BlockSpec((HEADS_PER_GROUP, Q_BLOCK, 1), lambda g, i: (g, i, 0)),
                  pl.BlockSpec((Q_BLOCK, GROUP_LANES), lambda g, i: (i, g)),
                  _ANY],
        out_specs=[pl.BlockSpec((Q_BLOCK, GROUP_LANES), lambda g, i: (i, g)),
                   pl.BlockSpec((s, GROUP_LANES), lambda g, i: (0, g)),
                   pl.BlockSpec((s, GROUP_LANES), lambda g, i: (0, g)),
                   _ANY],
        out_shape=[jax.ShapeDtypeStruct((s, SB_HEADS * SB_HEAD_DIM), F32)] * 3
        + [jax.ShapeDtypeStruct(parts.shape, parts.dtype)],
        scratch_shapes=[pltpu.VMEM((STACKS_PER_STEP * TILES_PER_STEP, TILE_ROWS, TILE_LANES), F32)]
        + list(_AG_SCRATCH),
        compiler_params=pltpu.CompilerParams(
            dimension_semantics=("arbitrary", "arbitrary"), vmem_limit_bytes=VMEM_LIMIT_V7X),
    )(cnt, qkv, qkv, qkv, tot, dout_b, parts)


def _outproj_fwd_body(oa_ref, ob_ref, h_ref, ga_ref, gb_ref, gpost_ref, w_ref,
                      merged_ref, mo_ref, hout_ref):
    half = oa_ref.shape[1]
    ma, _ = _rms(oa_ref[...], ga_ref[...])
    mb, _ = _rms(ob_ref[...], gb_ref[...])
    mab = ma.astype(BF16)
    mbb = mb.astype(BF16)
    merged_ref[:, :half] = mab
    merged_ref[:, half:] = mbb
    mo = (jnp.dot(mab, w_ref[:half, :], preferred_element_type=F32)
          + jnp.dot(mbb, w_ref[half:, :], preferred_element_type=F32))
    mo_ref[...] = mo
    y, _ = _rms(mo, gpost_ref[...])
    hout_ref[...] = h_ref[...] + y


def _outproj_fwd(out_a, out_b, h, ga, gb, gpost, w_out):
    s, d = h.shape
    return _row_call(_outproj_fwd_body, "outproj_fwd", s, [out_a, out_b, h], [ga, gb, gpost, w_out],
                     [(d, BF16), (d, F32), (d, F32)], [], tm=XA_TOKEN_TILE)


def _outproj_bwd_body(dh_ref, mo_ref, oa_ref, ob_ref, ga_ref, gb_ref, gpost_ref, w_ref,
                      dmo_ref, doa_ref, dob_ref, dga_ref, dgb_ref, dgpost_ref):
    half = oa_ref.shape[1]
    dmo, dgpost = _rms_bwd(mo_ref[...], gpost_ref[...], dh_ref[...])
    dmob = dmo.astype(BF16)
    dmo_ref[...] = dmob
    dma = _dot_nt(dmob, w_ref[:half, :])
    dmb = _dot_nt(dmob, w_ref[half:, :])
    doa, dga = _rms_bwd(oa_ref[...], ga_ref[...], dma)
    dob, dgb = _rms_bwd(ob_ref[...], gb_ref[...], dmb)
    doa_ref[...] = doa
    dob_ref[...] = dob
    _acc(dga_ref, dga)
    _acc(dgb_ref, dgb)
    _acc(dgpost_ref, dgpost)


def _outproj_bwd(dh, mo, out_a, out_b, ga, gb, gpost, w_out):
    s, d = dh.shape
    half = out_a.shape[1]
    return _row_call(_outproj_bwd_body, "outproj_bwd", s, [dh, mo, out_a, out_b], [ga, gb, gpost, w_out],
                     [(d, BF16), (half, F32), (half, F32)],
                     [((1, half), F32), ((1, half), F32), ((1, d), F32)], tm=XA_TOKEN_TILE)


def _kv_fwd_body(mem_ref, g_ref, wt_ref, memn_ref, kv_ref):
    n, _ = _rms(mem_ref[...], g_ref[...])
    nb = n.astype(BF16)
    memn_ref[...] = nb
    kv_ref[...] = _dot_nt(nb, wt_ref[...]).astype(BF16)


def _kv_fwd(mem, g, w_kv_t):
    m, d = mem.shape
    return _row_call(_kv_fwd_body, "kv_fwd", m, [mem], [g, w_kv_t], [(d, BF16), (w_kv_t.shape[0], BF16)], [])


def _kv_bwd_body(dkv_ref, mem_ref, memn_ref, g_ref, wt_ref, dwt_ref, dg_ref):
    dkvb = dkv_ref[...].astype(BF16)
    dwt_ref[...] = _dot_tn(dkvb, memn_ref[...]).astype(BF16)
    dmemn = _dot(dkvb, wt_ref[...])
    _, dg = _rms_bwd(mem_ref[...], g_ref[...], dmemn)
    dg_ref[...] = dg


def _kv_bwd(dkv, mem, memn, g, w_kv_t):
    m, d = mem.shape
    return pl.pallas_call(
        functools.partial(_kv_bwd_body),
        name="kv_bwd",
        out_shape=[jax.ShapeDtypeStruct(w_kv_t.shape, BF16), jax.ShapeDtypeStruct((1, d), F32)],
        compiler_params=pltpu.CompilerParams(vmem_limit_bytes=VMEM_LIMIT_V7X),
    )(dkv, mem, memn, g, w_kv_t)


def _xa_fwd_body(h_ref, gpre_ref, gpost_ref, wq_ref, wo_ref, kv_ref,
                 n_ref, q_ref, o_ref, c_ref, hout_ref):
    h = h_ref[...]
    d = h.shape[1]
    n, _ = _rms(h, gpre_ref[...])
    nb = n.astype(BF16)
    n_ref[...] = nb
    qb = jnp.dot(nb, wq_ref[...], preferred_element_type=F32).astype(BF16)
    q_ref[...] = qb
    for hd in range(XA_HEADS):
        lanes = slice(hd * XA_HEAD_DIM, (hd + 1) * XA_HEAD_DIM)
        k = kv_ref[:, lanes]
        v = kv_ref[:, d + hd * XA_HEAD_DIM:d + (hd + 1) * XA_HEAD_DIM]
        logits = _dot_nt(qb[:, lanes], k) * (XA_HEAD_DIM ** -0.5)
        e = jnp.exp(logits - jnp.max(logits, axis=-1, keepdims=True))
        p = e / jnp.sum(e, axis=-1, keepdims=True)
        o_ref[:, lanes] = jnp.dot(p.astype(BF16), v, preferred_element_type=F32).astype(BF16)
    c = jnp.dot(o_ref[...], wo_ref[...], preferred_element_type=F32)
    c_ref[...] = c
    y, _ = _rms(c, gpost_ref[...])
    hout_ref[...] = h + y


def _xa_fwd(h, gpre, gpost, wq, wo, kv):
    s, d = h.shape
    return _row_call(_xa_fwd_body, "xa_fwd", s, [h], [gpre, gpost, wq, wo, kv],
                     [(d, BF16), (d, BF16), (d, BF16), (d, F32), (d, F32)], [], tm=XA_TOKEN_TILE)


def _xa_bwd_body(dh_ref, h_ref, c_ref, q_ref, o_ref, gpre_ref, gpost_ref, wq_ref, wo_ref, kv_ref,
                 dhout_ref, dc_ref, dq_ref, dkv_ref, dgpre_ref, dgpost_ref):
    dh = dh_ref[...]
    d = dh.shape[1]
    scale = XA_HEAD_DIM ** -0.5
    dc, dgpost = _rms_bwd(c_ref[...], gpost_ref[...], dh)
    dcb = dc.astype(BF16)
    dc_ref[...] = dcb
    dob = _dot_nt(dcb, wo_ref[...]).astype(BF16)

    @pl.when(pl.program_id(0) == 0)
    def _():
        dkv_ref[...] = jnp.zeros_like(dkv_ref)

    dqs, dks, dvs = [], [], []
    for hd in range(XA_HEADS):
        lanes = slice(hd * XA_HEAD_DIM, (hd + 1) * XA_HEAD_DIM)
        vlanes = slice(d + hd * XA_HEAD_DIM, d + (hd + 1) * XA_HEAD_DIM)
        qh = q_ref[:, lanes]
        k = kv_ref[:, lanes]
        v = kv_ref[:, vlanes]
        logits = _dot_nt(qh, k) * scale
        e = jnp.exp(logits - jnp.max(logits, axis=-1, keepdims=True))
        p = e / jnp.sum(e, axis=-1, keepdims=True)
        doh = dob[:, lanes]
        dp = _dot_nt(doh, v)
        dl = (p * (dp - jnp.sum(dp * p, axis=-1, keepdims=True)) * scale).astype(BF16)
        dqs.append(jnp.dot(dl, k, preferred_element_type=F32).astype(BF16))
        dks.append(_dot_tn(dl, qh))
        dvs.append(_dot_tn(p, doh))
    dq = jnp.concatenate(dqs, axis=1)
    dq_ref[...] = dq
    dkv_ref[...] += jnp.concatenate(dks + dvs, axis=1)
    dn = _dot_nt(dq, wq_ref[...])
    dhn, dgpre = _rms_bwd(h_ref[...], gpre_ref[...], dn)
    dhout_ref[...] = dh + dhn
    _acc(dgpre_ref, dgpre)
    _acc(dgpost_ref, dgpost)


def _xa_bwd(dh, h, c, q, o, gpre, gpost, wq, wo, kv, scatter=None):
    s, d = h.shape
    return _row_call(_xa_bwd_body, "xa_bwd", s, [dh, h, c, q, o], [gpre, gpost, wq, wo, kv],
                     [(d, F32), (d, BF16), (d, BF16)],
                     [(kv.shape, F32), ((1, d), F32), ((1, d), F32)], scatter=scatter, tm=XA_TOKEN_TILE)


def _largest_tile(n, cap):
    best = 128
    for t in range(128, cap + 1, 128):
        if n % t == 0:
            best = t
    return best


def _mm_tn(a, bs, name, gather=None, chip_exchange=None):
    s, k = a.shape
    n = bs[0].shape[1]
    nb = len(bs)
    ts = min(2048, s)
    tk = _largest_tile(k, 1536)
    tn = _largest_tile(n, 1536 // nb)

    steps = s // ts
    grid = (k // tk, n // tn, steps)
    sent = gather if gather is not None else chip_exchange
    hosted = sent is not None
    phases = _gather_phases if gather is not None else _chip_exchange_phases
    landed_shape = None
    if hosted:
        landed_shape = (N_DEV,) + sent.shape if gather is not None else (3,) + sent.shape[1:]

    def body(a_ref, *refs):
        if hosted:
            sent_ref, landed_ref, refs = refs[nb], refs[2 * nb + 1], refs[:nb] + refs[nb + 1:2 * nb + 1] + refs[2 * nb + 2:]
            start, *rest = phases(sent_ref, landed_ref, *refs[-3:])
            place = (pl.program_id(0) * grid[1] + pl.program_id(1)) * grid[2] + pl.program_id(2)
            pl.when(place == 0)(start)
        b_refs, o_refs, acc_refs = refs[:nb], refs[nb:2 * nb], refs[2 * nb:3 * nb]
        at = a_ref[...]
        t = pl.program_id(2)

        @pl.when(t == 0)
        def _():
            for acc_ref in acc_refs:
                acc_ref[...] = jnp.zeros_like(acc_ref)

        for b_ref, acc_ref in zip(b_refs, acc_refs):
            acc_ref[...] += _dot_tn(at, b_ref[...])

        @pl.when(t == steps - 1)
        def _():
            for o_ref, acc_ref in zip(o_refs, acc_refs):
                o_ref[...] = acc_ref[...].astype(BF16)

        if hosted:
            total = grid[0] * grid[1] * grid[2]
            for j, phase in enumerate(rest):
                pl.when(place == max(total - len(rest) + j, 0))(phase)

    return pl.pallas_call(
        body,
        name=name,
        grid=grid,
        in_specs=[pl.BlockSpec((ts, tk), lambda i, j, t: (t, i))]
        + [pl.BlockSpec((ts, tn), lambda i, j, t: (t, j))] * nb + ([_ANY] if hosted else []),
        out_specs=[pl.BlockSpec((tk, tn), lambda i, j, t: (i, j))] * nb + ([_ANY] if hosted else []),
        out_shape=[jax.ShapeDtypeStruct((k, n), BF16)] * nb
        + ([jax.ShapeDtypeStruct(landed_shape, sent.dtype)] if hosted else []),
        scratch_shapes=[pltpu.VMEM((tk, tn), F32)] * nb + (list(_AG_SCRATCH) if hosted else []),
        compiler_params=pltpu.CompilerParams(
            dimension_semantics=("arbitrary", "arbitrary", "arbitrary"),
            vmem_limit_bytes=VMEM_LIMIT_V7X),
    )(a, *bs, *([sent] if hosted else []))


_SMALL_SHAPES = {
    "sgu_norm_g": (1, SGU_GROUPS * GROUP_DIM),
    "sgu_norm_b": (1, SGU_GROUPS * GROUP_DIM),
    "sgu_w_s": (SGU_GROUPS, CHUNK, CHUNK),
    "sgu_b_s": (SGU_GROUPS, CHUNK, 1),
}


def _small_views(small):
    return {n: v.reshape(_SMALL_SHAPES.get(n, v.shape)) for n, v in small.items()}


def _small_unviews(views, like):
    return {n: v.reshape(like[n].shape) for n, v in views.items()}


def _unpack_rows(gathered, names, shard_rows):
    out, off = {}, 0
    for n in names:
        rows = shard_rows[n]
        out[n] = gathered[:, off:off + rows, :].reshape(N_DEV * rows, gathered.shape[2])
        off += rows
    return out


def _row_tile(r, cap):
    best = 16
    for t in range(16, cap + 1, 16):
        if r % t == 0:
            best = t
    return best


def _sum_received(received, name):
    _, r, c = received.shape
    tr = _row_tile(r, 1024)

    def body(rc_ref, g_ref):
        g = rc_ref[0].astype(F32)
        for t in range(1, N_DEV):
            g = g + rc_ref[t].astype(F32)
        g_ref[...] = g

    return pl.pallas_call(
        body, name=name, grid=(r // tr,),
        in_specs=[pl.BlockSpec((N_DEV, tr, c), lambda i: (0, i, 0))],
        out_specs=pl.BlockSpec((tr, c), lambda i: (i, 0)),
        out_shape=jax.ShapeDtypeStruct((r, c), F32),
    )(received)


def _local_step(x, mem, target, small, big, shards, shard_rows):
    sm, w = small, dict(big)
    d_model = x.shape[1]

    def parts(names):
        return jnp.concatenate([gw.pop(n).reshape(N_DEV, -1, d_model) for n in names], axis=1)

    h1, n1, a1, b1, f1, landed = _ffn_fwd(
        x, sm["ffn1_pre_g"], sm["ffn1_post_g"], w["ffn1_w_gate"], w["ffn1_w_up"], w["ffn1_w_down"],
        "ffn1_fwd", gather=shards["ffn1_fwd"])
    w.update(_unpack_rows(landed, GATHER_IN["ffn1_fwd"], shard_rows))
    n2, uv_pre, qkv = _inproj_fwd(h1, sm["mix_pre_g"], w["w_in"])
    out_a = _sgu_fwd(uv_pre, sm["sgu_norm_g"], sm["sgu_norm_b"], sm["sgu_w_s"], sm["sgu_b_s"])
    out_b, tot, cnt, landed = _sb_fwd(qkv, shards["sb_fwd"])
    w.update(_unpack_rows(landed, GATHER_IN["sb_fwd"], shard_rows))
    merged, mo, h2 = _outproj_fwd(out_a, out_b, h1, sm["sgu_out_g"], sm["sb_out_g"],
                                  sm["mix_post_g"], w["w_out"])
    memn, kv = _kv_fwd(mem, sm["mem_norm_g"], w["xa_w_kv"])
    n3, qx, ox, cx, h3 = _xa_fwd(h2, sm["xa_pre_g"], sm["xa_post_g"], w["xa_w_q"], w["xa_w_o"], kv)
    n4, a2, b2, f2, dh4, loss, dg_final = _ffn_loss_fwd(
        h3, target, sm["ffn2_pre_g"], sm["ffn2_post_g"], sm["final_norm_g"],
        w["ffn2_w_gate"], w["ffn2_w_up"], w["ffn2_w_down"], "ffn2_fwd")

    gs, gw = {"final_norm_g": dg_final}, {}
    dh3, da2, db2, hm2, df2, gs["ffn2_pre_g"], gs["ffn2_post_g"] = _ffn_bwd(
        dh4, h3, a2, b2, f2, sm["ffn2_pre_g"], sm["ffn2_post_g"],
        w["ffn2_w_gate"], w["ffn2_w_up"], w["ffn2_w_down"], "ffn2_bwd")
    gw["ffn2_w_gate"], = _mm_tn(da2, [n4], "ffn2_dw_gate")
    gw["ffn2_w_up"], = _mm_tn(db2, [n4], "ffn2_dw_up")
    gw["ffn2_w_down"], = _mm_tn(hm2, [df2], "ffn2_dw_down")

    received = {}
    dh2, dc, dqx, dkv, gs["xa_pre_g"], gs["xa_post_g"], received["xa_bwd"] = _xa_bwd(
        dh3, h2, cx, qx, ox, sm["xa_pre_g"], sm["xa_post_g"], w["xa_w_q"], w["xa_w_o"], kv,
        scatter=parts(SCATTER_IN["xa_bwd"]))
    gw["xa_w_o"], = _mm_tn(ox, [dc], "xa_dw_o")
    gw["xa_w_q"], = _mm_tn(n3, [dqx], "xa_dw_q")
    gw["xa_w_kv"], gs["mem_norm_g"] = _kv_bwd(dkv, mem, memn, sm["mem_norm_g"], w["xa_w_kv"])

    dmo, dout_a, dout_b, gs["sgu_out_g"], gs["sb_out_g"], gs["mix_post_g"] = _outproj_bwd(
        dh2, mo, out_a, out_b, sm["sgu_out_g"], sm["sb_out_g"], sm["mix_post_g"], w["w_out"])
    gw["w_out"], = _mm_tn(merged, [dmo], "mix_dw_out")
    dq, dk, dv, received["sb_bwd"] = _sb_bwd(cnt, qkv, tot, dout_b, parts(SCATTER_IN["sb_bwd"]))
    duv, gs["sgu_w_s"], gs["sgu_b_s"], gs["sgu_norm_g"], gs["sgu_norm_b"], received["sgu_bwd"] = _sgu_bwd(
        uv_pre, dout_a, sm["sgu_norm_g"], sm["sgu_norm_b"], sm["sgu_w_s"], sm["sgu_b_s"],
        scatter=parts(SCATTER_IN["sgu_bwd"]))
    dproj = jnp.concatenate([duv] + [t.astype(BF16) for t in (dq, dk, dv)], axis=1)
    dh1, gs["mix_pre_g"], received["inproj_bwd"] = _inproj_bwd(
        dh2, dproj, h1, sm["mix_pre_g"], w["w_in"], scatter=parts(SCATTER_IN["inproj_bwd"]))
    gw["w_in"], = _mm_tn(dproj, [n2], "mix_dw_in")

    dx, da1, db1, hm1, df1, gs["ffn1_pre_g"], gs["ffn1_post_g"], received["ffn1_bwd"] = _ffn_bwd(
        dh1, x, a1, b1, f1, sm["ffn1_pre_g"], sm["ffn1_post_g"],
        w["ffn1_w_gate"], w["ffn1_w_up"], w["ffn1_w_down"], "ffn1_bwd",
        scatter=parts(SCATTER_IN["ffn1_bwd"]))
    summed = {host: _sum_received(r, "rs_sum_" + host) for host, r in received.items()}

    mx, my, mc = _mesh_place()
    place = jnp.stack([mc, 2 * mx + my]).astype(jnp.int32)

    def pair_stage(dw, tag):
        p4 = dw.reshape(N_DEV // 2, 2, -1, d_model)
        recv_a = _pair_exchange(p4, "rs_pair_exchange_" + tag)
        return p4, recv_a, _pair_sum(place, p4, recv_a, "rs_pair_sum_" + tag)

    dw, small_grads = _mm_tn(da1, [n1], "ffn1_dw_gate", gather=_pack_small(gs))
    gate = pair_stage(dw, "gate")
    dw, gate_b = _mm_tn(db1, [n1], "ffn1_dw_up", chip_exchange=gate[2])
    up = pair_stage(dw, "up")
    dw, up_b = _mm_tn(hm1, [df1], "ffn1_dw_down", chip_exchange=up[2])
    down = pair_stage(dw, "down")
    down_b = _chip_exchange(down[2], "rs_chip_exchange_down")
    for n, (p4, recv_a, _), recv_b in (("ffn1_w_gate", gate, gate_b), ("ffn1_w_up", up, up_b),
                                       ("ffn1_w_down", down, down_b)):
        summed[n] = _rs_final(place, p4, recv_a, recv_b, "rs_final_" + n)
    return loss, dx, small_grads, summed


def _pair_exchange(p4, name):
    nchip, _, r, c = p4.shape

    def body(p_ref, out_ref, send_sems, recv_sems):
        mx, my, mc = _mesh_place()
        copies = [pltpu.make_async_remote_copy(
            src_ref=p_ref.at[j, 1 - mc], dst_ref=out_ref.at[j],
            send_sem=send_sems.at[j], recv_sem=recv_sems.at[j],
            device_id=(mx, my, 1 - mc), device_id_type=MESH) for j in range(nchip)]
        for cp in copies:
            cp.start()
        for cp in copies:
            cp.wait()

    return pl.pallas_call(
        body,
        name=name,
        out_shape=jax.ShapeDtypeStruct((nchip, r, c), p4.dtype),
        in_specs=[_ANY],
        out_specs=_ANY,
        scratch_shapes=[pltpu.SemaphoreType.DMA((nchip,)), pltpu.SemaphoreType.DMA((nchip,))],
    )(p4)


def _chip_exchange(q, name):
    _, r, c = q.shape

    def body(q_ref, out_ref, send_sems, recv_sems, local_sem):
        for phase in _chip_exchange_phases(q_ref, out_ref, send_sems, recv_sems, local_sem):
            phase()

    return pl.pallas_call(
        body,
        name=name,
        out_shape=jax.ShapeDtypeStruct((3, r, c), q.dtype),
        in_specs=[_ANY],
        out_specs=_ANY,
        scratch_shapes=list(_AG_SCRATCH),
    )(q)


def _rs_row_tile(r):
    return _row_tile(r, 1024)


def _pair_sum(place, p4, recv_a, name):
    nchip, _, r, c = p4.shape
    tr = _rs_row_tile(r)

    def body(place_ref, p_ref, a_ref, q_ref):
        q_ref[0] = (p_ref[0, 0].astype(F32) + a_ref[0].astype(F32)).astype(BF16)

    return pl.pallas_call(
        body,
        name=name,
        grid_spec=pltpu.PrefetchScalarGridSpec(
            num_scalar_prefetch=1,
            grid=(nchip, r // tr),
            in_specs=[pl.BlockSpec((1, 1, tr, c), lambda j, i, pref: (j, pref[0], i, 0)),
                      pl.BlockSpec((1, tr, c), lambda j, i, pref: (j, i, 0))],
            out_specs=pl.BlockSpec((1, tr, c), lambda j, i, pref: (j, i, 0)),
        ),
        out_shape=jax.ShapeDtypeStruct((nchip, r, c), BF16),
    )(place, p4, recv_a)


def _rs_final(place, p4, recv_a, recv_b, name):
    _, _, r, c = p4.shape
    tr = _rs_row_tile(r)

    def body(place_ref, p_ref, a_ref, b_ref, g_ref):
        g = p_ref[0, 0].astype(F32) + a_ref[0].astype(F32)
        for k in range(3):
            g = g + b_ref[k].astype(F32)
        g_ref[...] = g

    return pl.pallas_call(
        body,
        name=name,
        grid_spec=pltpu.PrefetchScalarGridSpec(
            num_scalar_prefetch=1,
            grid=(r // tr,),
            in_specs=[pl.BlockSpec((1, 1, tr, c), lambda i, pref: (pref[1], pref[0], i, 0)),
                      pl.BlockSpec((1, tr, c), lambda i, pref: (pref[1], i, 0)),
                      pl.BlockSpec((3, tr, c), lambda i, pref: (0, i, 0))],
            out_specs=pl.BlockSpec((tr, c), lambda i, pref: (i, 0)),
        ),
        out_shape=jax.ShapeDtypeStruct((r, c), F32),
    )(place, p4, recv_a, recv_b)


def _adamw_math(w, g, m, v):
    m = ADAM_B1 * m + (1.0 - ADAM_B1) * g
    v = ADAM_B2 * v + (1.0 - ADAM_B2) * (g * g)
    m_hat = m / (1.0 - ADAM_B1 ** ADAM_STEP)
    v_hat = v / (1.0 - ADAM_B2 ** ADAM_STEP)
    delta = -ADAM_LR * (m_hat / (jnp.sqrt(v_hat) + ADAM_EPS) + ADAM_WD * w)
    return delta, m, v


ADAMW_ROW_BLOCKS = 4


def _adamw(w, g, m, v, name):
    r, c = w.shape
    tr = r // ADAMW_ROW_BLOCKS

    def body(w_ref, g_ref, m_ref, v_ref, d_ref, mo_ref, vo_ref):
        d_ref[...], mo_ref[...], vo_ref[...] = _adamw_math(w_ref[...], g_ref[...], m_ref[...], v_ref[...])

    spec = pl.BlockSpec((tr, c), lambda i: (i, 0))
    out = jax.ShapeDtypeStruct((r, c), F32)
    return pl.pallas_call(
        body, name=name, grid=(r // tr,), in_specs=[spec] * 4, out_specs=[spec] * 3,
        out_shape=[out] * 3,
    )(w, g, m, v)


def _small_sum_adamw(gathered, ws, ms, vs):
    _, r, c = gathered.shape
    count = len(ws)

    def body(ga_ref, *refs):
        w_refs, m_refs, v_refs = refs[:count], refs[count:2 * count], refs[2 * count:3 * count]
        out_refs = refs[3 * count:]
        for o_ref in out_refs:
            o_ref[...] = jnp.zeros_like(o_ref)
        off = 0
        for w_ref, m_ref, v_ref in zip(w_refs, m_refs, v_refs):
            rows = pl.ds(off, w_ref.shape[0])
            g = ga_ref[0, rows, :]
            for k in range(1, N_DEV):
                g = g + ga_ref[k, rows, :]
            results = (g, *_adamw_math(w_ref[...], g, m_ref[...], v_ref[...]))
            for o_ref, val in zip(out_refs, results):
                o_ref[rows, :] = val
            off += w_ref.shape[0] + (-w_ref.shape[0]) % SMALL_ROW_ALIGN

    out = jax.ShapeDtypeStruct((r, c), F32)
    return pl.pallas_call(body, name="small_sum_adamw", out_shape=[out] * 4)(gathered, *ws, *ms, *vs)


_WEIGHTS = ["ffn1_pre_g", "ffn1_post_g", "ffn1_w_gate", "ffn1_w_up", "ffn1_w_down", "mix_pre_g",
            "mix_post_g", "w_in", "sgu_norm_g", "sgu_norm_b", "sgu_w_s", "sgu_b_s", "sgu_out_g",
            "sb_out_g", "w_out", "xa_pre_g", "xa_post_g", "mem_norm_g", "xa_w_q", "xa_w_kv", "xa_w_o",
            "ffn2_pre_g", "ffn2_post_g", "ffn2_w_gate", "ffn2_w_up", "ffn2_w_down", "final_norm_g"]
_BIG = ["ffn1_w_gate", "ffn1_w_up", "ffn1_w_down", "w_in", "w_out", "xa_w_q", "xa_w_kv", "xa_w_o",
        "ffn2_w_gate", "ffn2_w_up", "ffn2_w_down"]
_COL_SHARDED = ("ffn1_w_gate", "ffn1_w_up", "w_in", "xa_w_kv", "ffn2_w_gate", "ffn2_w_up")
_EARLY = ["ffn1_w_gate", "ffn1_w_up", "ffn1_w_down"]
GATHER_IN = {"ffn1_fwd": ["w_in", "ffn2_w_gate"],
             "sb_fwd": ["w_out", "xa_w_q", "xa_w_kv", "xa_w_o", "ffn2_w_up", "ffn2_w_down"]}
SCATTER_IN = {"xa_bwd": ["ffn2_w_gate"],
              "sb_bwd": ["ffn2_w_up", "ffn2_w_down", "xa_w_o", "w_out"],
              "sgu_bwd": ["xa_w_kv"],
              "inproj_bwd": ["xa_w_q"],
              "ffn1_bwd": ["w_in"]}
_SMALL = [n for n in _WEIGHTS if n not in _BIG]
SMALL_LANES = 128
SMALL_ROW_ALIGN = 8


def _pack_small(tensors):
    parts = []
    for n in _SMALL:
        t = tensors[n].reshape(-1, SMALL_LANES)
        pad = (-t.shape[0]) % SMALL_ROW_ALIGN
        parts.append(jnp.pad(t, ((0, pad), (0, 0))) if pad else t)
    return jnp.concatenate(parts, axis=0)


def _unpack_small(packed, like):
    out, off = {}, 0
    for n in _SMALL:
        size = like[n].size
        rows = size // SMALL_LANES
        out[n] = packed[off:off + rows].reshape(like[n].shape)
        off += rows + (-rows) % SMALL_ROW_ALIGN
    return out


def kernel(x, mem, ffn1_pre_g, ffn1_post_g, ffn1_w_gate, ffn1_w_up, ffn1_w_down, mix_pre_g, mix_post_g, w_in, sgu_norm_g, sgu_norm_b, sgu_w_s, sgu_b_s, sgu_out_g, sb_out_g, w_out, xa_pre_g, xa_post_g, mem_norm_g, xa_w_q, xa_w_kv, xa_w_o, ffn2_pre_g, ffn2_post_g, ffn2_w_gate, ffn2_w_up, ffn2_w_down, final_norm_g, loss_target, m_ffn1_pre_g, m_ffn1_post_g, m_ffn1_w_gate, m_ffn1_w_up, m_ffn1_w_down, m_mix_pre_g, m_mix_post_g, m_w_in, m_sgu_norm_g, m_sgu_norm_b, m_sgu_w_s, m_sgu_b_s, m_sgu_out_g, m_sb_out_g, m_w_out, m_xa_pre_g, m_xa_post_g, m_mem_norm_g, m_xa_w_q, m_xa_w_kv, m_xa_w_o, m_ffn2_pre_g, m_ffn2_post_g, m_ffn2_w_gate, m_ffn2_w_up, m_ffn2_w_down, m_final_norm_g, v_ffn1_pre_g, v_ffn1_post_g, v_ffn1_w_gate, v_ffn1_w_up, v_ffn1_w_down, v_mix_pre_g, v_mix_post_g, v_w_in, v_sgu_norm_g, v_sgu_norm_b, v_sgu_w_s, v_sgu_b_s, v_sgu_out_g, v_sb_out_g, v_w_out, v_xa_pre_g, v_xa_post_g, v_mem_norm_g, v_xa_w_q, v_xa_w_kv, v_xa_w_o, v_ffn2_pre_g, v_ffn2_post_g, v_ffn2_w_gate, v_ffn2_w_up, v_ffn2_w_down, v_final_norm_g):
    vals = dict(locals())
    d_model = x.shape[-1]

    def packed(names):
        return jnp.concatenate(
            [(vals[n][0].T if n in _COL_SHARDED else vals[n][0]).astype(BF16) for n in names], axis=0)

    shard_rows = {n: vals[n].shape[2 if n in _COL_SHARDED else 1] for n in _BIG}
    big = _unpack_rows(_all_gather(packed(_EARLY), "ag_weights"), _EARLY, shard_rows)

    small = {n: vals[n] for n in _SMALL}
    loss_part, dx, gathered_small, summed = _local_step(
        x[0], mem[0], loss_target[0], _small_views(small), big,
        {host: packed(names) for host, names in GATHER_IN.items()}, shard_rows)
    loss = lax.psum(loss_part[0, 0], ("x", "y", "c"))

    grads, deltas, new_m, new_v = {}, {}, {}, {}
    for names, g_rows in [([n], summed[n]) for n in _EARLY] + [(SCATTER_IN[h], summed[h]) for h in SCATTER_IN]:
        off = 0
        for n in names:
            rows = shard_rows[n]
            g = g_rows[off:off + rows]
            off += rows
            state = [vals[n][0], vals["m_" + n][0], vals["v_" + n][0]]
            flipped = n in _COL_SHARDED and rows % SMALL_LANES != 0
            if flipped:
                state = [t.T for t in state]
            elif n in _COL_SHARDED:
                g = g.T
            outs = (g, *_adamw(state[0], g, state[1], state[2], "adamw_" + n))
            if flipped:
                outs = tuple(t.T for t in outs)
            grads[n], deltas[n], new_m[n], new_v[n] = (t[None] for t in outs)

    outs = _small_sum_adamw(gathered_small,
                            *([vals[pre + n].reshape(-1, SMALL_LANES) for n in _SMALL] for pre in ("", "m_", "v_")))
    for dst, packed in zip((grads, deltas, new_m, new_v), outs):
        dst.update(_unpack_small(packed, small))

    return (loss, dx[None], *[grads[n] for n in _WEIGHTS], *[deltas[n] for n in _WEIGHTS],
            *[new_m[n] for n in _WEIGHTS], *[new_v[n] for n in _WEIGHTS])
```

```python
import functools

import jax
import jax.numpy as jnp
from jax import lax
from jax.experimental import pallas as pl
from jax.experimental.pallas import tpu as pltpu

F32 = jnp.float32
BF16 = jnp.bfloat16
EPS = 1e-6
MESH = pl.DeviceIdType.MESH
N_DEV = 8

SGU_GROUPS = 4
GROUP_DIM = 128
CHUNK = 128
SB_HEADS = 8
SB_HEAD_DIM = 64
Q_BLOCK = 128
XA_HEADS = 4
XA_HEAD_DIM = 256

ADAM_LR = 0.001
ADAM_B1 = 0.9
ADAM_B2 = 0.999
ADAM_EPS = 1e-08
ADAM_WD = 0.01
ADAM_STEP = 10

VMEM_LIMIT_V7X = 56 * 1024 * 1024
GELU_C0 = 0.7978845608028654
GELU_C1 = 0.044715


def _dot(a, b):
    return jnp.dot(a.astype(BF16), b.astype(BF16), preferred_element_type=F32)


def _dot_nt(a, b):
    return lax.dot_general(a.astype(BF16), b.astype(BF16), (((1,), (1,)), ((), ())),
                           preferred_element_type=F32)


def _dot_tn(a, b):
    return lax.dot_general(a.astype(BF16), b.astype(BF16), (((0,), (0,)), ((), ())),
                           preferred_element_type=F32)


def _rms(x, g):
    r = lax.rsqrt(jnp.mean(x * x, axis=-1, keepdims=True) + EPS)
    return x * r * g, r


def _rms_bwd(x, g, dy):
    r = lax.rsqrt(jnp.mean(x * x, axis=-1, keepdims=True) + EPS)
    xh = x * r
    gy = dy * g
    dx = r * (gy - xh * jnp.mean(gy * xh, axis=-1, keepdims=True))
    dg = jnp.sum(dy * xh, axis=0, keepdims=True)
    return dx, dg


def _sigmoid(x):
    return jax.nn.sigmoid(x)


def _gelu(x):
    t = jnp.tanh(GELU_C0 * (x + GELU_C1 * x * x * x))
    return 0.5 * x * (1.0 + t)


def _gelu_grad(x):
    t = jnp.tanh(GELU_C0 * (x + GELU_C1 * x * x * x))
    return 0.5 * (1.0 + t) + 0.5 * x * (1.0 - t * t) * GELU_C0 * (1.0 + 3.0 * GELU_C1 * x * x)


def _split_bf16(x):
    hi = x.astype(BF16)
    lo = (x - hi.astype(F32)).astype(BF16)
    return hi, lo


def _row_spec(tm, cols):
    return pl.BlockSpec((tm, cols), lambda i: (i, 0))


def _full_spec(shape, buffers=None):
    nd = len(shape)
    mode = None if buffers is None else pl.Buffered(buffers)
    return pl.BlockSpec(tuple(shape), lambda i: (0,) * nd, pipeline_mode=mode)


FFN_TOKEN_TILE = 512
XA_TOKEN_TILE = 512


def _token_tile(s):
    return min(256, s)


def _row_call(body, name, s, tiled_in, full_in, tiled_out, acc_out, gather=None, scatter=None, tm=None):
    tm = _token_tile(s) if tm is None else min(tm, s)
    steps = s // tm
    in_specs = [_row_spec(tm, a.shape[1]) for a in tiled_in] + [_full_spec(a.shape, buffers=1) for a in full_in]
    out_specs = [_row_spec(tm, c) for c, _ in tiled_out] + [_full_spec(sh) for sh, _ in acc_out]
    out_shape = [jax.ShapeDtypeStruct((s, c), dt) for c, dt in tiled_out]
    out_shape += [jax.ShapeDtypeStruct(sh, dt) for sh, dt in acc_out]
    operands = [*tiled_in, *full_in]
    scratch = []
    kernel_body = functools.partial(body)
    sent = gather if gather is not None else scatter
    if sent is not None:
        n_in, n_out = len(operands), len(out_shape)
        out_shape.append(jax.ShapeDtypeStruct(
            (N_DEV,) + sent.shape if gather is not None else sent.shape, sent.dtype))
        operands.append(sent)
        in_specs.append(_ANY)
        out_specs.append(_ANY)
        scratch = list(_AG_SCRATCH)

        def kernel_body(*refs):
            ins, sent_ref = refs[:n_in], refs[n_in]
            outs, landed_ref = refs[n_in + 1:n_in + 1 + n_out], refs[n_in + 1 + n_out]
            step = pl.program_id(0)
            if gather is not None:
                start, relay, forward, finish = _gather_phases(sent_ref, landed_ref, *refs[-3:])
            else:
                start, finish = _scatter_phases(sent_ref, landed_ref, *refs[-3:])
            pl.when(step == 0)(start)
            body(*ins, *outs)
            if gather is not None:
                pl.when(step == steps // 3)(relay)
                pl.when(step == (2 * steps) // 3)(forward)
            pl.when(step == steps - 1)(finish)

    return pl.pallas_call(
        kernel_body,
        name=name,
        grid=(steps,),
        in_specs=in_specs,
        out_specs=out_specs,
        out_shape=out_shape,
        scratch_shapes=scratch,
        compiler_params=pltpu.CompilerParams(
            dimension_semantics=("arbitrary",), vmem_limit_bytes=VMEM_LIMIT_V7X),
    )(*operands)


def _acc(ref, val):
    @pl.when(pl.program_id(0) == 0)
    def _():
        ref[...] = val

    @pl.when(pl.program_id(0) != 0)
    def _():
        ref[...] += val


def _ffn_fwd_tile(x_ref, pre_ref, post_ref, wgt_ref, wut_ref, wd_ref, n_ref, a_ref, b_ref, f_ref):
    x = x_ref[...]
    n, _ = _rms(x, pre_ref[...])
    nb = n.astype(BF16)
    n_ref[...] = nb
    a = _dot_nt(nb, wgt_ref[...])
    b = _dot_nt(nb, wut_ref[...])
    a_ref[...] = a.astype(BF16)
    b_ref[...] = b.astype(BF16)
    hmid = a * _sigmoid(a) * b
    f = jnp.dot(hmid.astype(BF16), wd_ref[...], preferred_element_type=F32)
    f_ref[...] = f
    y, _ = _rms(f, post_ref[...])
    return x + 0.5 * y


def _ffn_fwd_body(x_ref, pre_ref, post_ref, wgt_ref, wut_ref, wd_ref,
                  h_ref, n_ref, a_ref, b_ref, f_ref):
    h_ref[...] = _ffn_fwd_tile(x_ref, pre_ref, post_ref, wgt_ref, wut_ref, wd_ref, n_ref, a_ref, b_ref, f_ref)


def _ffn_loss_body(x_ref, t_ref, pre_ref, post_ref, gfin_ref, wgt_ref, wut_ref, wd_ref,
                   n_ref, a_ref, b_ref, f_ref, dh_ref, loss_ref, dg_ref):
    h = _ffn_fwd_tile(x_ref, pre_ref, post_ref, wgt_ref, wut_ref, wd_ref, n_ref, a_ref, b_ref, f_ref)
    d = h.shape[1]
    y, _ = _rms(h, gfin_ref[...])
    err = y - t_ref[...]
    part = (0.5 / d) * jnp.sum(jnp.sum(err * err, axis=1, keepdims=True), axis=0, keepdims=True)
    dh, dg = _rms_bwd(h, gfin_ref[...], err * (1.0 / d))
    dh_ref[...] = dh
    _acc(loss_ref, part)
    _acc(dg_ref, dg)


def _ffn_fwd(x, pre_g, post_g, wgt, wut, wd, name, gather=None):
    s, d = x.shape
    f = wgt.shape[0]
    return _row_call(_ffn_fwd_body, name, s, [x], [pre_g, post_g, wgt, wut, wd],
                     [(d, F32), (d, BF16), (f, BF16), (f, BF16), (d, F32)], [], gather=gather, tm=FFN_TOKEN_TILE)


def _ffn_loss_fwd(x, target, pre_g, post_g, final_g, wgt, wut, wd, name):
    s, d = x.shape
    f = wgt.shape[0]
    return _row_call(_ffn_loss_body, name, s, [x, target], [pre_g, post_g, final_g, wgt, wut, wd],
                     [(d, BF16), (f, BF16), (f, BF16), (d, F32), (d, F32)],
                     [((1, 1), F32), ((1, d), F32)], tm=FFN_TOKEN_TILE)


def _ffn_bwd_body(dh_ref, x_ref, a_ref, b_ref, f_ref, pre_ref, post_ref, wgt_ref, wut_ref, wd_ref,
                  dx_ref, da_ref, db_ref, hm_ref, df_ref, dpre_ref, dpost_ref):
    dh = dh_ref[...]
    df, dpost = _rms_bwd(f_ref[...], post_ref[...], 0.5 * dh)
    dfb = df.astype(BF16)
    df_ref[...] = dfb
    dhmid = _dot_nt(dfb, wd_ref[...])
    a = a_ref[...].astype(F32)
    b = b_ref[...].astype(F32)
    sig = _sigmoid(a)
    sa = a * sig
    hm_ref[...] = (sa * b).astype(BF16)
    dab = (dhmid * b * sig * (1.0 + a * (1.0 - sig))).astype(BF16)
    dbb = (dhmid * sa).astype(BF16)
    da_ref[...] = dab
    db_ref[...] = dbb
    dn = _dot(dab, wgt_ref[...]) + _dot(dbb, wut_ref[...])
    dxn, dpre = _rms_bwd(x_ref[...], pre_ref[...], dn)
    dx_ref[...] = dh + dxn
    _acc(dpre_ref, dpre)
    _acc(dpost_ref, dpost)


def _ffn_bwd(dh, x, a, b, f, pre_g, post_g, wgt, wut, wd, name, scatter=None):
    s, d = x.shape
    ff = wgt.shape[0]
    return _row_call(_ffn_bwd_body, name, s, [dh, x, a, b, f], [pre_g, post_g, wgt, wut, wd],
                     [(d, F32), (ff, BF16), (ff, BF16), (ff, BF16), (d, BF16)],
                     [((1, d), F32), ((1, d), F32)], scatter=scatter)


def _inproj_fwd_body(h_ref, g_ref, wt_ref, n_ref, uv_ref, qkv_ref):
    n, _ = _rms(h_ref[...], g_ref[...])
    nb = n.astype(BF16)
    n_ref[...] = nb
    proj = _dot_nt(nb, wt_ref[...])
    nuv = uv_ref.shape[1]
    uv_ref[...] = proj[:, :nuv]
    qkv_ref[...] = proj[:, nuv:].astype(BF16)


def _inproj_fwd(h, g, w_in_t):
    s, d = h.shape
    sgu_w = SGU_GROUPS * GROUP_DIM
    sb_w = SB_HEADS * SB_HEAD_DIM
    return _row_call(_inproj_fwd_body, "inproj_fwd", s, [h], [g, w_in_t],
                     [(d, BF16), (2 * sgu_w, F32), (3 * sb_w, BF16)], [], tm=XA_TOKEN_TILE)


def _inproj_bwd_body(dh_ref, dproj_ref, h_ref, g_ref, wt_ref, dhout_ref, dg_ref):
    dn = _dot(dproj_ref[...], wt_ref[...])
    dhn, dg = _rms_bwd(h_ref[...], g_ref[...], dn)
    dhout_ref[...] = dh_ref[...] + dhn
    _acc(dg_ref, dg)


def _inproj_bwd(dh, dproj, h, g, w_in_t, scatter=None):
    s, d = h.shape
    return _row_call(_inproj_bwd_body, "inproj_bwd", s, [dh, dproj, h], [g, w_in_t],
                     [(d, F32)], [((1, d), F32)], scatter=scatter, tm=XA_TOKEN_TILE)


def _causal_w(ws_ref, g):
    row = lax.broadcasted_iota(jnp.int32, (CHUNK, CHUNK), 0)
    col = lax.broadcasted_iota(jnp.int32, (CHUNK, CHUNK), 1)
    return jnp.where(row >= col, ws_ref[g], 0.0), row >= col


def _group_norm(v):
    mu = jnp.mean(v, axis=-1, keepdims=True)
    d = v - mu
    rstd = lax.rsqrt(jnp.mean(d * d, axis=-1, keepdims=True) + EPS)
    return d * rstd, rstd


def _sgu_fwd_body(uv_ref, ng_ref, nb_ref, ws_ref, bs_ref, out_ref):
    width = SGU_GROUPS * GROUP_DIM
    for c in range(uv_ref.shape[0] // CHUNK):
        rows = pl.ds(c * CHUNK, CHUNK)
        for g in range(SGU_GROUPS):
            lanes = pl.ds(g * GROUP_DIM, GROUP_DIM)
            u = _gelu(uv_ref[rows, lanes])
            v = _gelu(uv_ref[rows, pl.ds(width + g * GROUP_DIM, GROUP_DIM)])
            vhat, _ = _group_norm(v)
            vn = vhat * ng_ref[:, lanes] + nb_ref[:, lanes]
            w, _ = _causal_w(ws_ref, g)
            mixed = _dot(w, vn) + bs_ref[g]
            out_ref[rows, lanes] = u * mixed


def _sgu_fwd(uv_pre, ng, nb, ws, bs):
    s = uv_pre.shape[0]
    return _row_call(_sgu_fwd_body, "sgu_fwd", s, [uv_pre], [ng, nb, ws, bs],
                     [(SGU_GROUPS * GROUP_DIM, F32)], [])[0]


def _sgu_bwd_body(uv_ref, do_ref, ng_ref, nb_ref, ws_ref, bs_ref,
                  duv_ref, dws_ref, dbs_ref, dng_ref, dnb_ref):
    width = SGU_GROUPS * GROUP_DIM

    @pl.when(pl.program_id(0) == 0)
    def _():
        dws_ref[...] = jnp.zeros_like(dws_ref)
        dbs_ref[...] = jnp.zeros_like(dbs_ref)
        dng_ref[...] = jnp.zeros_like(dng_ref)
        dnb_ref[...] = jnp.zeros_like(dnb_ref)

    for c in range(uv_ref.shape[0] // CHUNK):
        rows = pl.ds(c * CHUNK, CHUNK)
        for g in range(SGU_GROUPS):
            lanes = pl.ds(g * GROUP_DIM, GROUP_DIM)
            vlanes = pl.ds(width + g * GROUP_DIM, GROUP_DIM)
            u_pre = uv_ref[rows, lanes]
            v_pre = uv_ref[rows, vlanes]
            u = _gelu(u_pre)
            v = _gelu(v_pre)
            vhat, rstd = _group_norm(v)
            gain = ng_ref[:, lanes]
            vn = vhat * gain + nb_ref[:, lanes]
            w, causal = _causal_w(ws_ref, g)
            mixed = _dot(w, vn) + bs_ref[g]
            dout = do_ref[rows, lanes]
            du = dout * mixed
            dmixed = dout * u
            dbs_ref[g] += jnp.sum(dmixed, axis=1, keepdims=True)
            dws_ref[g] += jnp.where(causal, _dot_nt(dmixed, vn), 0.0)
            dvn = _dot_tn(w, dmixed)
            dng_ref[:, lanes] += jnp.sum(dvn * vhat, axis=0, keepdims=True)
            dnb_ref[:, lanes] += jnp.sum(dvn, axis=0, keepdims=True)
            dvh = dvn * gain
            dv = rstd * (dvh - jnp.mean(dvh, axis=-1, keepdims=True)
                         - vhat * jnp.mean(dvh * vhat, axis=-1, keepdims=True))
            duv_ref[rows, lanes] = (du * _gelu_grad(u_pre)).astype(BF16)
            duv_ref[rows, vlanes] = (dv * _gelu_grad(v_pre)).astype(BF16)


def _sgu_bwd(uv_pre, dout_a, ng, nb, ws, bs, scatter=None):
    s = uv_pre.shape[0]
    width = SGU_GROUPS * GROUP_DIM
    return _row_call(_sgu_bwd_body, "sgu_bwd", s, [uv_pre, dout_a], [ng, nb, ws, bs],
                     [(2 * width, BF16)],
                     [(ws.shape, F32), (bs.shape, F32), ((1, width), F32), ((1, width), F32)], scatter=scatter)


def _mesh_place():
    return lax.axis_index("x"), lax.axis_index("y"), lax.axis_index("c")


def _other_chips(mx, my):
    return [(1 - mx, my), (mx, 1 - my), (1 - mx, 1 - my)]


_ANY = pl.BlockSpec(memory_space=pl.ANY)
AG_SEMS = 8
_AG_SCRATCH = [pltpu.SemaphoreType.DMA((AG_SEMS,)), pltpu.SemaphoreType.DMA((AG_SEMS,)),
               pltpu.SemaphoreType.DMA(())]
ROW_ALIGN_ANY_DTYPE = 16


def _gather_phases(x_ref, out_ref, send_sems, recv_sems, local_sem):
    mx, my, mc = _mesh_place()
    me, sibling = (mx, my, mc), (mx, my, 1 - mc)
    x_chip, y_chip, far_chip = _other_chips(mx, my)
    rows = x_ref.shape[0]
    cut = (rows // (2 * ROW_ALIGN_ANY_DTYPE)) * ROW_ALIGN_ANY_DTYPE
    parts = {3: pl.ds(0, cut), 7: pl.ds(cut, rows - cut)}

    def slot(px, py, pc):
        return out_ref.at[4 * px + 2 * py + pc]

    def copy(k, block, to, src=None):
        where = slot(*block) if k not in parts else slot(*block).at[parts[k]]
        return pltpu.make_async_remote_copy(
            src_ref=where if src is None else src, dst_ref=where,
            send_sem=send_sems.at[k], recv_sem=recv_sems.at[k],
            device_id=to, device_id_type=MESH)

    mine = pltpu.make_async_copy(x_ref, slot(*me), local_sem)
    first = [copy(0, me, sibling, src=x_ref), copy(1, me, (*x_chip, mc), src=x_ref),
             copy(2, me, (*y_chip, mc), src=x_ref)]
    relayed = [copy(3, (*x_chip, mc), (*y_chip, mc)), copy(7, (*y_chip, mc), (*x_chip, mc))]
    passed = [copy(4, (*x_chip, mc), sibling), copy(5, (*y_chip, mc), sibling), copy(6, (*far_chip, mc), sibling)]

    def start():
        mine.start()
        for cp in first:
            cp.start()

    def relay():
        copy(1, (*x_chip, mc), me).wait_recv()
        relayed[0].start()
        passed[0].start()
        copy(2, (*y_chip, mc), me).wait_recv()
        relayed[1].start()
        passed[1].start()

    def forward():
        copy(3, (*far_chip, mc), me).wait_recv()
        copy(7, (*far_chip, mc), me).wait_recv()
        passed[2].start()

    def finish():
        copy(0, sibling, me).wait_recv()
        for k, chip in ((4, x_chip), (5, y_chip), (6, far_chip)):
            copy(k, (*chip, 1 - mc), me).wait_recv()
        for cp in first + relayed + passed:
            cp.wait_send()
        mine.wait()

    return start, relay, forward, finish


def _all_gather(x, name):
    r, c = x.shape

    def body(x_ref, out_ref, send_sems, recv_sems, local_sem):
        for phase in _gather_phases(x_ref, out_ref, send_sems, recv_sems, local_sem):
            phase()

    return pl.pallas_call(
        body,
        name=name,
        out_shape=jax.ShapeDtypeStruct((N_DEV, r, c), x.dtype),
        in_specs=[_ANY],
        out_specs=_ANY,
        scratch_shapes=list(_AG_SCRATCH),
    )(x)


def _scatter_phases(p_ref, out_ref, send_sems, recv_sems, local_sem):
    mx, my, mc = _mesh_place()
    me = 4 * mx + 2 * my + mc
    copies = []
    for k in range(1, N_DEV):
        tx, ty, tc = mx ^ ((k >> 2) & 1), my ^ ((k >> 1) & 1), mc ^ (k & 1)
        copies.append(pltpu.make_async_remote_copy(
            src_ref=p_ref.at[4 * tx + 2 * ty + tc], dst_ref=out_ref.at[me],
            send_sem=send_sems.at[k - 1], recv_sem=recv_sems.at[k - 1],
            device_id=(tx, ty, tc), device_id_type=MESH))
    mine = pltpu.make_async_copy(p_ref.at[me], out_ref.at[me], local_sem)

    def start():
        mine.start()
        for cp in copies:
            cp.start()

    def finish():
        for cp in copies:
            cp.wait()
        mine.wait()

    return start, finish


def _chip_exchange_phases(q_ref, out_ref, send_sems, recv_sems, local_sem):
    mx, my, mc = _mesh_place()
    copies = [pltpu.make_async_remote_copy(
        src_ref=q_ref.at[2 * cx + cy], dst_ref=out_ref.at[k],
        send_sem=send_sems.at[k], recv_sem=recv_sems.at[k],
        device_id=(cx, cy, mc), device_id_type=MESH)
        for k, (cx, cy) in enumerate(_other_chips(mx, my))]

    def start():
        for cp in copies:
            cp.start()

    def finish():
        for cp in copies:
            cp.wait()

    return start, finish


SB_DEAD = -105.0
HEADS_PER_TILE = 2
TILES_PER_STEP = 2
HEADS_PER_STEP = HEADS_PER_TILE * TILES_PER_STEP
STEP_LANES = TILES_PER_STEP * HEADS_PER_TILE * SB_HEAD_DIM
TILE_LANES = HEADS_PER_TILE * SB_HEAD_DIM
STACK_ROWS = HEADS_PER_STEP * Q_BLOCK
TILE_ROWS = HEADS_PER_TILE * Q_BLOCK
STACKS_PER_STEP = 2
GROUP_LANES = STACKS_PER_STEP * STEP_LANES
HEADS_PER_GROUP = STACKS_PER_STEP * HEADS_PER_STEP
SB_FORWARD_LEAD = 6


def _stack_heads(x):
    lane = lax.broadcasted_iota(jnp.int32, x.shape, 1)
    zero = jnp.zeros_like(x)
    return jnp.concatenate(
        [jnp.where(lane // SB_HEAD_DIM == h, x, zero) for h in range(HEADS_PER_STEP)], axis=0)


def _unstack_tile(x):
    first = lax.broadcasted_iota(jnp.int32, (Q_BLOCK, TILE_LANES), 1) < SB_HEAD_DIM
    return jnp.where(first, x[:Q_BLOCK], x[Q_BLOCK:])


def _sb_logs(qs, k, diagonal):
    z = _dot_nt(qs, k) * (SB_HEAD_DIM ** -0.5)
    sp = jnp.log1p(jnp.exp(-jnp.abs(z)))
    log_beta = jnp.minimum(z, 0.0) - sp
    log_1m_raw = -jnp.maximum(z, 0.0) - sp
    if not diagonal:
        return None, log_beta, log_1m_raw, log_1m_raw
    row = lax.broadcasted_iota(jnp.int32, z.shape, 0)
    col = lax.broadcasted_iota(jnp.int32, z.shape, 1)
    strict = col < jnp.bitwise_and(row, Q_BLOCK - 1)
    return strict, log_beta, log_1m_raw, jnp.where(strict, log_1m_raw, 0.0)


def _masked(strict, x):
    return x if strict is None else jnp.where(strict, x, 0.0)


def _key_sums(x, pick):
    hi, lo = _split_bf16(x)
    both = jnp.dot(jnp.concatenate([hi, lo], axis=0), pick, preferred_element_type=F32)
    return both[:x.shape[0]] + both[x.shape[0]:]


def _key_order():
    row = lax.broadcasted_iota(jnp.int32, (Q_BLOCK, Q_BLOCK), 0)
    col = lax.broadcasted_iota(jnp.int32, (Q_BLOCK, Q_BLOCK), 1)
    return row, col


def _sb_fwd_body(q_ref, k_ref, v_ref, shard_ref, o_ref, tot_ref, cnt_ref, gathered_ref,
                 acc_ref, send_sems, recv_sems, local_sem):
    grp, qb = pl.program_id(0), pl.program_id(1)
    last_grp, last_qb = pl.num_programs(0) - 1, pl.num_programs(1) - 1
    ag_start, ag_relay, ag_forward, ag_finish = _gather_phases(
        shard_ref, gathered_ref, send_sems, recv_sems, local_sem)
    pl.when(jnp.logical_and(grp == 0, qb == 0))(ag_start)

    stacks = range(STACKS_PER_STEP)
    lanes = [slice(st * STEP_LANES, (st + 1) * STEP_LANES) for st in stacks]
    qs = [_stack_heads(q_ref[:, lanes[st]]) for st in stacks]
    row, col = _key_order()
    later = (row > col).astype(BF16)

    def block(i, cs, diagonal):
        rows = pl.ds(pl.multiple_of((qb - i) * Q_BLOCK, Q_BLOCK), Q_BLOCK)
        new_cs = []
        for st in stacks:
            strict, log_beta, _, log_1m = _sb_logs(qs[st], k_ref[rows, lanes[st]], diagonal)
            a = _masked(strict, jnp.exp(log_beta + _key_sums(log_1m, later) + cs[st])).astype(BF16)
            for t in range(TILES_PER_STEP):
                tile = st * TILES_PER_STEP + t
                part = jnp.dot(a[t * TILE_ROWS:(t + 1) * TILE_ROWS],
                               v_ref[rows, tile * TILE_LANES:(tile + 1) * TILE_LANES],
                               preferred_element_type=F32)
                if diagonal:
                    acc_ref[tile] = part
                else:
                    acc_ref[tile] += part
            new_cs.append(cs[st] + jnp.sum(log_1m, axis=1, keepdims=True))
        return tuple(new_cs)

    cs = block(0, (jnp.zeros((STACK_ROWS, 1), F32),) * STACKS_PER_STEP, True)

    def alive(carry):
        i, cs = carry
        return jnp.logical_and(i <= qb, jnp.max(functools.reduce(jnp.maximum, cs)) > SB_DEAD)

    def step(carry):
        i, cs = carry
        return i + 1, block(i, cs, False)

    n, cs = lax.while_loop(alive, step, (jnp.int32(1), cs))
    for tile in range(STACKS_PER_STEP * TILES_PER_STEP):
        o_ref[:, tile * TILE_LANES:(tile + 1) * TILE_LANES] = _unstack_tile(acc_ref[tile])
    for st in stacks:
        for h in range(HEADS_PER_STEP):
            tot_ref[st * HEADS_PER_STEP + h] = cs[st][h * Q_BLOCK:(h + 1) * Q_BLOCK]
    cnt_ref[grp, qb] = n.astype(F32)
    place = grp * (last_qb + 1) + qb
    last = (last_grp + 1) * (last_qb + 1) - 1
    pl.when(place == (last + 1) // 2)(ag_relay)
    pl.when(place == jnp.maximum(last - SB_FORWARD_LEAD, (last + 1) // 2))(ag_forward)
    pl.when(place == last)(ag_finish)


def _sb_fwd(qkv, shard):
    s = qkv.shape[0]
    groups = SB_HEADS // HEADS_PER_GROUP
    nq = s // Q_BLOCK
    return pl.pallas_call(
        functools.partial(_sb_fwd_body),
        name="sb_fwd",
        grid=(groups, nq),
        in_specs=[pl.BlockSpec((Q_BLOCK, GROUP_LANES), lambda g, i: (i, g)),
                  pl.BlockSpec((s, GROUP_LANES), lambda g, i: (0, groups + g)),
                  pl.BlockSpec((s, GROUP_LANES), lambda g, i: (0, 2 * groups + g)),
                  _ANY],
        out_specs=[pl.BlockSpec((Q_BLOCK, GROUP_LANES), lambda g, i: (i, g)),
                   pl.BlockSpec((HEADS_PER_GROUP, Q_BLOCK, 1), lambda g, i: (g, i, 0)),
                   pl.BlockSpec(memory_space=pltpu.SMEM),
                   _ANY],
        out_shape=[jax.ShapeDtypeStruct((s, SB_HEADS * SB_HEAD_DIM), F32),
                   jax.ShapeDtypeStruct((SB_HEADS, s, 1), F32),
                   jax.ShapeDtypeStruct((groups, nq), F32),
                   jax.ShapeDtypeStruct((N_DEV,) + shard.shape, shard.dtype)],
        scratch_shapes=[pltpu.VMEM((STACKS_PER_STEP * TILES_PER_STEP, TILE_ROWS, TILE_LANES), F32)]
        + list(_AG_SCRATCH),
        compiler_params=pltpu.CompilerParams(
            dimension_semantics=("arbitrary", "arbitrary"), vmem_limit_bytes=VMEM_LIMIT_V7X),
    )(qkv, qkv, qkv, shard)


def _sb_bwd_body(cnt_ref, q_ref, k_ref, v_ref, tot_ref, do_ref, part_ref, dq_ref, dk_ref, dv_ref, recv_ref,
                 acc_ref, send_sems, recv_sems, local_sem):
    grp, qb = pl.program_id(0), pl.program_id(1)
    last_grp, last_qb = pl.num_programs(0) - 1, pl.num_programs(1) - 1
    rs_start, rs_finish = _scatter_phases(part_ref, recv_ref, send_sems, recv_sems, local_sem)
    pl.when(jnp.logical_and(grp == 0, qb == 0))(rs_start)

    @pl.when(qb == 0)
    def _():
        dk_ref[...] = jnp.zeros_like(dk_ref)
        dv_ref[...] = jnp.zeros_like(dv_ref)

    acc_ref[...] = jnp.zeros_like(acc_ref)
    stacks = range(STACKS_PER_STEP)
    lanes = [slice(st * STEP_LANES, (st + 1) * STEP_LANES) for st in stacks]
    qs = [_stack_heads(q_ref[:, lanes[st]]) for st in stacks]
    dos = [_stack_heads(do_ref[:, lanes[st]].astype(BF16)) for st in stacks]
    tots = [jnp.concatenate([tot_ref[st * HEADS_PER_STEP + h] for h in range(HEADS_PER_STEP)], axis=0)
            for st in stacks]
    row, col = _key_order()
    up_to = (row <= col).astype(BF16)
    earlier = (row < col).astype(BF16)
    scale = SB_HEAD_DIM ** -0.5
    n = jnp.clip(cnt_ref[grp, qb].astype(jnp.int32), 1, qb + 1)

    def block(kb, cs, ces, diagonal):
        rows = pl.ds(pl.multiple_of(kb * Q_BLOCK, Q_BLOCK), Q_BLOCK)
        new_cs, new_ces = [], []
        for st in stacks:
            k = k_ref[rows, lanes[st]]
            strict, log_beta, log_1m_raw, log_1m = _sb_logs(qs[st], k, diagonal)
            suffix = tots[st] - cs[st] - _key_sums(log_1m, up_to)
            a = _masked(strict, jnp.exp(log_beta + suffix))
            de = _dot_nt(dos[st], v_ref[rows, lanes[st]]) * a
            before = ces[st] + _key_sums(de, earlier)
            dz = _masked(strict, de * jnp.exp(log_1m_raw) - before * jnp.exp(log_beta)).astype(BF16)
            for t in range(TILES_PER_STEP):
                acc_ref[st * TILES_PER_STEP + t] += jnp.dot(
                    dz[t * TILE_ROWS:(t + 1) * TILE_ROWS], k[:, t * TILE_LANES:(t + 1) * TILE_LANES],
                    preferred_element_type=F32)
            dk_ref[rows, lanes[st]] += _dot_tn(dz, qs[st]) * scale
            dv_ref[rows, lanes[st]] += _dot_tn(a, dos[st])
            new_cs.append(cs[st] + jnp.sum(log_1m, axis=1, keepdims=True))
            new_ces.append(ces[st] + jnp.sum(de, axis=1, keepdims=True))
        return tuple(new_cs), tuple(new_ces)

    def step(i, carry):
        return block(qb - n + 1 + i, *carry, False)

    zc = (jnp.zeros((STACK_ROWS, 1), F32),) * STACKS_PER_STEP
    cs, ces = lax.fori_loop(0, n - 1, step, (zc, zc))
    block(qb, cs, ces, True)
    for tile in range(STACKS_PER_STEP * TILES_PER_STEP):
        dq_ref[:, tile * TILE_LANES:(tile + 1) * TILE_LANES] = _unstack_tile(acc_ref[tile]) * scale
    pl.when(jnp.logical_and(grp == last_grp, qb == last_qb))(rs_finish)


def _sb_bwd(cnt, qkv, tot, dout_b, parts):
    s = qkv.shape[0]
    groups = SB_HEADS // HEADS_PER_GROUP
    return pl.pallas_call(
        functools.partial(_sb_bwd_body),
        name="sb_bwd",
        grid=(groups, s // Q_BLOCK),
        in_specs=[pl.BlockSpec(memory_space=pltpu.SMEM),
                  pl.BlockSpec((Q_BLOCK, GROUP_LANES), lambda g, i: (i, g)),
                  pl.BlockSpec((s, GROUP_LANES), lambda g, i: (0, groups + g)),
                  pl.BlockSpec((s, GROUP_LANES), lambda g, i: (0, 2 * groups + g)),
                  pl.BlockSpec((HEADS_PER_GROUP, Q_BLOCK, 1), lambda g, i: (g, i, 0)),
                  pl.BlockSpec((Q_BLOCK, GROUP_LANES), lambda g, i: (i, g)),
                  _ANY],
        out_specs=[pl.BlockSpec((Q_BLOCK, GROUP_LANES), lambda g, i: (i, g)),
                   pl.BlockSpec((s, GROUP_LANES), lambda g, i: (0, g)),
                   pl.BlockSpec((s, GROUP_LANES), lambda g, i: (0, g)),
                   _ANY],
        out_shape=[jax.ShapeDtypeStruct((s, SB_HEADS * SB_HEAD_DIM), F32)] * 3
        + [jax.ShapeDtypeStruct(parts.shape, parts.dtype)],
        scratch_shapes=[pltpu.VMEM((STACKS_PER_STEP * TILES_PER_STEP, TILE_ROWS, TILE_LANES), F32)]
        + list(_AG_SCRATCH),
        compiler_params=pltpu.CompilerParams(
            dimension_semantics=("arbitrary", "arbitrary"), vmem_limit_bytes=VMEM_LIMIT_V7X),
    )(cnt, qkv, qkv, qkv, tot, dout_b, parts)


def _outproj_fwd_body(oa_ref, ob_ref, h_ref, ga_ref, gb_ref, gpost_ref, w_ref,
                      merged_ref, mo_ref, hout_ref):
    half = oa_ref.shape[1]
    ma, _ = _rms(oa_ref[...], ga_ref[...])
    mb, _ = _rms(ob_ref[...], gb_ref[...])
    mab = ma.astype(BF16)
    mbb = mb.astype(BF16)
    merged_ref[:, :half] = mab
    merged_ref[:, half:] = mbb
    mo = (jnp.dot(mab, w_ref[:half, :], preferred_element_type=F32)
          + jnp.dot(mbb, w_ref[half:, :], preferred_element_type=F32))
    mo_ref[...] = mo
    y, _ = _rms(mo, gpost_ref[...])
    hout_ref[...] = h_ref[...] + y


def _outproj_fwd(out_a, out_b, h, ga, gb, gpost, w_out):
    s, d = h.shape
    return _row_call(_outproj_fwd_body, "outproj_fwd", s, [out_a, out_b, h], [ga, gb, gpost, w_out],
                     [(d, BF16), (d, F32), (d, F32)], [], tm=XA_TOKEN_TILE)


def _outproj_bwd_body(dh_ref, mo_ref, oa_ref, ob_ref, ga_ref, gb_ref, gpost_ref, w_ref,
                      dmo_ref, doa_ref, dob_ref, dga_ref, dgb_ref, dgpost_ref):
    half = oa_ref.shape[1]
    dmo, dgpost = _rms_bwd(mo_ref[...], gpost_ref[...], dh_ref[...])
    dmob = dmo.astype(BF16)
    dmo_ref[...] = dmob
    dma = _dot_nt(dmob, w_ref[:half, :])
    dmb = _dot_nt(dmob, w_ref[half:, :])
    doa, dga = _rms_bwd(oa_ref[...], ga_ref[...], dma)
    dob, dgb = _rms_bwd(ob_ref[...], gb_ref[...], dmb)
    doa_ref[...] = doa
    dob_ref[...] = dob
    _acc(dga_ref, dga)
    _acc(dgb_ref, dgb)
    _acc(dgpost_ref, dgpost)


def _outproj_bwd(dh, mo, out_a, out_b, ga, gb, gpost, w_out):
    s, d = dh.shape
    half = out_a.shape[1]
    return _row_call(_outproj_bwd_body, "outproj_bwd", s, [dh, mo, out_a, out_b], [ga, gb, gpost, w_out],
                     [(d, BF16), (half, F32), (half, F32)],
                     [((1, half), F32), ((1, half), F32), ((1, d), F32)], tm=XA_TOKEN_TILE)


def _kv_fwd_body(mem_ref, g_ref, wt_ref, memn_ref, kv_ref):
    n, _ = _rms(mem_ref[...], g_ref[...])
    nb = n.astype(BF16)
    memn_ref[...] = nb
    kv_ref[...] = _dot_nt(nb, wt_ref[...]).astype(BF16)


def _kv_fwd(mem, g, w_kv_t):
    m, d = mem.shape
    return _row_call(_kv_fwd_body, "kv_fwd", m, [mem], [g, w_kv_t], [(d, BF16), (w_kv_t.shape[0], BF16)], [])


def _kv_bwd_body(dkv_ref, mem_ref, memn_ref, g_ref, wt_ref, dwt_ref, dg_ref):
    dkvb = dkv_ref[...].astype(BF16)
    dwt_ref[...] = _dot_tn(dkvb, memn_ref[...]).astype(BF16)
    dmemn = _dot(dkvb, wt_ref[...])
    _, dg = _rms_bwd(mem_ref[...], g_ref[...], dmemn)
    dg_ref[...] = dg


def _kv_bwd(dkv, mem, memn, g, w_kv_t):
    m, d = mem.shape
    return pl.pallas_call(
        functools.partial(_kv_bwd_body),
        name="kv_bwd",
        out_shape=[jax.ShapeDtypeStruct(w_kv_t.shape, BF16), jax.ShapeDtypeStruct((1, d), F32)],
        compiler_params=pltpu.CompilerParams(vmem_limit_bytes=VMEM_LIMIT_V7X),
    )(dkv, mem, memn, g, w_kv_t)


def _xa_fwd_body(h_ref, gpre_ref, gpost_ref, wq_ref, wo_ref, kv_ref,
                 n_ref, q_ref, o_ref, c_ref, hout_ref):
    h = h_ref[...]
    d = h.shape[1]
    n, _ = _rms(h, gpre_ref[...])
    nb = n.astype(BF16)
    n_ref[...] = nb
    qb = jnp.dot(nb, wq_ref[...], preferred_element_type=F32).astype(BF16)
    q_ref[...] = qb
    for hd in range(XA_HEADS):
        lanes = slice(hd * XA_HEAD_DIM, (hd + 1) * XA_HEAD_DIM)
        k = kv_ref[:, lanes]
        v = kv_ref[:, d + hd * XA_HEAD_DIM:d + (hd + 1) * XA_HEAD_DIM]
        logits = _dot_nt(qb[:, lanes], k) * (XA_HEAD_DIM ** -0.5)
        e = jnp.exp(logits - jnp.max(logits, axis=-1, keepdims=True))
        p = e / jnp.sum(e, axis=-1, keepdims=True)
        o_ref[:, lanes] = jnp.dot(p.astype(BF16), v, preferred_element_type=F32).astype(BF16)
    c = jnp.dot(o_ref[...], wo_ref[...], preferred_element_type=F32)
    c_ref[...] = c
    y, _ = _rms(c, gpost_ref[...])
    hout_ref[...] = h + y


def _xa_fwd(h, gpre, gpost, wq, wo, kv):
    s, d = h.shape
    return _row_call(_xa_fwd_body, "xa_fwd", s, [h], [gpre, gpost, wq, wo, kv],
                     [(d, BF16), (d, BF16), (d, BF16), (d, F32), (d, F32)], [], tm=XA_TOKEN_TILE)


def _xa_bwd_body(dh_ref, h_ref, c_ref, q_ref, o_ref, gpre_ref, gpost_ref, wq_ref, wo_ref, kv_ref,
                 dhout_ref, dc_ref, dq_ref, dkv_ref, dgpre_ref, dgpost_ref):
    dh = dh_ref[...]
    d = dh.shape[1]
    scale = XA_HEAD_DIM ** -0.5
    dc, dgpost = _rms_bwd(c_ref[...], gpost_ref[...], dh)
    dcb = dc.astype(BF16)
    dc_ref[...] = dcb
    dob = _dot_nt(dcb, wo_ref[...]).astype(BF16)

    @pl.when(pl.program_id(0) == 0)
    def _():
        dkv_ref[...] = jnp.zeros_like(dkv_ref)

    dqs, dks, dvs = [], [], []
    for hd in range(XA_HEADS):
        lanes = slice(hd * XA_HEAD_DIM, (hd + 1) * XA_HEAD_DIM)
        vlanes = slice(d + hd * XA_HEAD_DIM, d + (hd + 1) * XA_HEAD_DIM)
        qh = q_ref[:, lanes]
        k = kv_ref[:, lanes]
        v = kv_ref[:, vlanes]
        logits = _dot_nt(qh, k) * scale
        e = jnp.exp(logits - jnp.max(logits, axis=-1, keepdims=True))
        p = e / jnp.sum(e, axis=-1, keepdims=True)
        doh = dob[:, lanes]
        dp = _dot_nt(doh, v)
        dl = (p * (dp - jnp.sum(dp * p, axis=-1, keepdims=True)) * scale).astype(BF16)
        dqs.append(jnp.dot(dl, k, preferred_element_type=F32).astype(BF16))
        dks.append(_dot_tn(dl, qh))
        dvs.append(_dot_tn(p, doh))
    dq = jnp.concatenate(dqs, axis=1)
    dq_ref[...] = dq
    dkv_ref[...] += jnp.concatenate(dks + dvs, axis=1)
    dn = _dot_nt(dq, wq_ref[...])
    dhn, dgpre = _rms_bwd(h_ref[...], gpre_ref[...], dn)
    dhout_ref[...] = dh + dhn
    _acc(dgpre_ref, dgpre)
    _acc(dgpost_ref, dgpost)


def _xa_bwd(dh, h, c, q, o, gpre, gpost, wq, wo, kv, scatter=None):
    s, d = h.shape
    return _row_call(_xa_bwd_body, "xa_bwd", s, [dh, h, c, q, o], [gpre, gpost, wq, wo, kv],
                     [(d, F32), (d, BF16), (d, BF16)],
                     [(kv.shape, F32), ((1, d), F32), ((1, d), F32)], scatter=scatter, tm=XA_TOKEN_TILE)


def _largest_tile(n, cap):
    best = 128
    for t in range(128, cap + 1, 128):
        if n % t == 0:
            best = t
    return best


def _mm_tn(a, bs, name, gather=None, chip_exchange=None):
    s, k = a.shape
    n = bs[0].shape[1]
    nb = len(bs)
    ts = min(2048, s)
    tk = _largest_tile(k, 1536)
    tn = _largest_tile(n, 1536 // nb)

    steps = s // ts
    grid = (k // tk, n // tn, steps)
    sent = gather if gather is not None else chip_exchange
    hosted = sent is not None
    phases = _gather_phases if gather is not None else _chip_exchange_phases
    landed_shape = None
    if hosted:
        landed_shape = (N_DEV,) + sent.shape if gather is not None else (3,) + sent.shape[1:]

    def body(a_ref, *refs):
        if hosted:
            sent_ref, landed_ref, refs = refs[nb], refs[2 * nb + 1], refs[:nb] + refs[nb + 1:2 * nb + 1] + refs[2 * nb + 2:]
            start, *rest = phases(sent_ref, landed_ref, *refs[-3:])
            place = (pl.program_id(0) * grid[1] + pl.program_id(1)) * grid[2] + pl.program_id(2)
            pl.when(place == 0)(start)
        b_refs, o_refs, acc_refs = refs[:nb], refs[nb:2 * nb], refs[2 * nb:3 * nb]
        at = a_ref[...]
        t = pl.program_id(2)

        @pl.when(t == 0)
        def _():
            for acc_ref in acc_refs:
                acc_ref[...] = jnp.zeros_like(acc_ref)

        for b_ref, acc_ref in zip(b_refs, acc_refs):
            acc_ref[...] += _dot_tn(at, b_ref[...])

        @pl.when(t == steps - 1)
        def _():
            for o_ref, acc_ref in zip(o_refs, acc_refs):
                o_ref[...] = acc_ref[...].astype(BF16)

        if hosted:
            total = grid[0] * grid[1] * grid[2]
            for j, phase in enumerate(rest):
                pl.when(place == max(total - len(rest) + j, 0))(phase)

    return pl.pallas_call(
        body,
        name=name,
        grid=grid,
        in_specs=[pl.BlockSpec((ts, tk), lambda i, j, t: (t, i))]
        + [pl.BlockSpec((ts, tn), lambda i, j, t: (t, j))] * nb + ([_ANY] if hosted else []),
        out_specs=[pl.BlockSpec((tk, tn), lambda i, j, t: (i, j))] * nb + ([_ANY] if hosted else []),
        out_shape=[jax.ShapeDtypeStruct((k, n), BF16)] * nb
        + ([jax.ShapeDtypeStruct(landed_shape, sent.dtype)] if hosted else []),
        scratch_shapes=[pltpu.VMEM((tk, tn), F32)] * nb + (list(_AG_SCRATCH) if hosted else []),
        compiler_params=pltpu.CompilerParams(
            dimension_semantics=("arbitrary", "arbitrary", "arbitrary"),
            vmem_limit_bytes=VMEM_LIMIT_V7X),
    )(a, *bs, *([sent] if hosted else []))


_SMALL_SHAPES = {
    "sgu_norm_g": (1, SGU_GROUPS * GROUP_DIM),
    "sgu_norm_b": (1, SGU_GROUPS * GROUP_DIM),
    "sgu_w_s": (SGU_GROUPS, CHUNK, CHUNK),
    "sgu_b_s": (SGU_GROUPS, CHUNK, 1),
}


def _small_views(small):
    return {n: v.reshape(_SMALL_SHAPES.get(n, v.shape)) for n, v in small.items()}


def _small_unviews(views, like):
    return {n: v.reshape(like[n].shape) for n, v in views.items()}


def _unpack_rows(gathered, names, shard_rows):
    out, off = {}, 0
    for n in names:
        rows = shard_rows[n]
        out[n] = gathered[:, off:off + rows, :].reshape(N_DEV * rows, gathered.shape[2])
        off += rows
    return out


def _row_tile(r, cap):
    best = 16
    for t in range(16, cap + 1, 16):
        if r % t == 0:
            best = t
    return best


def _sum_received(received, name):
    _, r, c = received.shape
    tr = _row_tile(r, 1024)

    def body(rc_ref, g_ref):
        g = rc_ref[0].astype(F32)
        for t in range(1, N_DEV):
            g = g + rc_ref[t].astype(F32)
        g_ref[...] = g

    return pl.pallas_call(
        body, name=name, grid=(r // tr,),
        in_specs=[pl.BlockSpec((N_DEV, tr, c), lambda i: (0, i, 0))],
        out_specs=pl.BlockSpec((tr, c), lambda i: (i, 0)),
        out_shape=jax.ShapeDtypeStruct((r, c), F32),
    )(received)


def _local_step(x, mem, target, small, big, shards, shard_rows):
    sm, w = small, dict(big)
    d_model = x.shape[1]

    def parts(names):
        return jnp.concatenate([gw.pop(n).reshape(N_DEV, -1, d_model) for n in names], axis=1)

    h1, n1, a1, b1, f1, landed = _ffn_fwd(
        x, sm["ffn1_pre_g"], sm["ffn1_post_g"], w["ffn1_w_gate"], w["ffn1_w_up"], w["ffn1_w_down"],
        "ffn1_fwd", gather=shards["ffn1_fwd"])
    w.update(_unpack_rows(landed, GATHER_IN["ffn1_fwd"], shard_rows))
    n2, uv_pre, qkv = _inproj_fwd(h1, sm["mix_pre_g"], w["w_in"])
    out_a = _sgu_fwd(uv_pre, sm["sgu_norm_g"], sm["sgu_norm_b"], sm["sgu_w_s"], sm["sgu_b_s"])
    out_b, tot, cnt, landed = _sb_fwd(qkv, shards["sb_fwd"])
    w.update(_unpack_rows(landed, GATHER_IN["sb_fwd"], shard_rows))
    merged, mo, h2 = _outproj_fwd(out_a, out_b, h1, sm["sgu_out_g"], sm["sb_out_g"],
                                  sm["mix_post_g"], w["w_out"])
    memn, kv = _kv_fwd(mem, sm["mem_norm_g"], w["xa_w_kv"])
    n3, qx, ox, cx, h3 = _xa_fwd(h2, sm["xa_pre_g"], sm["xa_post_g"], w["xa_w_q"], w["xa_w_o"], kv)
    n4, a2, b2, f2, dh4, loss, dg_final = _ffn_loss_fwd(
        h3, target, sm["ffn2_pre_g"], sm["ffn2_post_g"], sm["final_norm_g"],
        w["ffn2_w_gate"], w["ffn2_w_up"], w["ffn2_w_down"], "ffn2_fwd")

    gs, gw = {"final_norm_g": dg_final}, {}
    dh3, da2, db2, hm2, df2, gs["ffn2_pre_g"], gs["ffn2_post_g"] = _ffn_bwd(
        dh4, h3, a2, b2, f2, sm["ffn2_pre_g"], sm["ffn2_post_g"],
        w["ffn2_w_gate"], w["ffn2_w_up"], w["ffn2_w_down"], "ffn2_bwd")
    gw["ffn2_w_gate"], = _mm_tn(da2, [n4], "ffn2_dw_gate")
    gw["ffn2_w_up"], = _mm_tn(db2, [n4], "ffn2_dw_up")
    gw["ffn2_w_down"], = _mm_tn(hm2, [df2], "ffn2_dw_down")

    received = {}
    dh2, dc, dqx, dkv, gs["xa_pre_g"], gs["xa_post_g"], received["xa_bwd"] = _xa_bwd(
        dh3, h2, cx, qx, ox, sm["xa_pre_g"], sm["xa_post_g"], w["xa_w_q"], w["xa_w_o"], kv,
        scatter=parts(SCATTER_IN["xa_bwd"]))
    gw["xa_w_o"], = _mm_tn(ox, [dc], "xa_dw_o")
    gw["xa_w_q"], = _mm_tn(n3, [dqx], "xa_dw_q")
    gw["xa_w_kv"], gs["mem_norm_g"] = _kv_bwd(dkv, mem, memn, sm["mem_norm_g"], w["xa_w_kv"])

    dmo, dout_a, dout_b, gs["sgu_out_g"], gs["sb_out_g"], gs["mix_post_g"] = _outproj_bwd(
        dh2, mo, out_a, out_b, sm["sgu_out_g"], sm["sb_out_g"], sm["mix_post_g"], w["w_out"])
    gw["w_out"], = _mm_tn(merged, [dmo], "mix_dw_out")
    dq, dk, dv, received["sb_bwd"] = _sb_bwd(cnt, qkv, tot, dout_b, parts(SCATTER_IN["sb_bwd"]))
    duv, gs["sgu_w_s"], gs["sgu_b_s"], gs["sgu_norm_g"], gs["sgu_norm_b"], received["sgu_bwd"] = _sgu_bwd(
        uv_pre, dout_a, sm["sgu_norm_g"], sm["sgu_norm_b"], sm["sgu_w_s"], sm["sgu_b_s"],
        scatter=parts(SCATTER_IN["sgu_bwd"]))
    dproj = jnp.concatenate([duv] + [t.astype(BF16) for t in (dq, dk, dv)], axis=1)
    dh1, gs["mix_pre_g"], received["inproj_bwd"] = _inproj_bwd(
        dh2, dproj, h1, sm["mix_pre_g"], w["w_in"], scatter=parts(SCATTER_IN["inproj_bwd"]))
    gw["w_in"], = _mm_tn(dproj, [n2], "mix_dw_in")

    dx, da1, db1, hm1, df1, gs["ffn1_pre_g"], gs["ffn1_post_g"], received["ffn1_bwd"] = _ffn_bwd(
        dh1, x, a1, b1, f1, sm["ffn1_pre_g"], sm["ffn1_post_g"],
        w["ffn1_w_gate"], w["ffn1_w_up"], w["ffn1_w_down"], "ffn1_bwd",
        scatter=parts(SCATTER_IN["ffn1_bwd"]))
    summed = {host: _sum_received(r, "rs_sum_" + host) for host, r in received.items()}

    mx, my, mc = _mesh_place()
    place = jnp.stack([mc, 2 * mx + my]).astype(jnp.int32)

    def pair_stage(dw, tag):
        p4 = dw.reshape(N_DEV // 2, 2, -1, d_model)
        recv_a = _pair_exchange(p4, "rs_pair_exchange_" + tag)
        return p4, recv_a, _pair_sum(place, p4, recv_a, "rs_pair_sum_" + tag)

    dw, small_grads = _mm_tn(da1, [n1], "ffn1_dw_gate", gather=_pack_small(gs))
    gate = pair_stage(dw, "gate")
    dw, gate_b = _mm_tn(db1, [n1], "ffn1_dw_up", chip_exchange=gate[2])
    up = pair_stage(dw, "up")
    dw, up_b = _mm_tn(hm1, [df1], "ffn1_dw_down", chip_exchange=up[2])
    down = pair_stage(dw, "down")
    down_b = _chip_exchange(down[2], "rs_chip_exchange_down")
    for n, (p4, recv_a, _), recv_b in (("ffn1_w_gate", gate, gate_b), ("ffn1_w_up", up, up_b),
                                       ("ffn1_w_down", down, down_b)):
        summed[n] = _rs_final(place, p4, recv_a, recv_b, "rs_final_" + n)
    return loss, dx, small_grads, summed


def _pair_exchange(p4, name):
    nchip, _, r, c = p4.shape

    def body(p_ref, out_ref, send_sems, recv_sems):
        mx, my, mc = _mesh_place()
        copies = [pltpu.make_async_remote_copy(
            src_ref=p_ref.at[j, 1 - mc], dst_ref=out_ref.at[j],
            send_sem=send_sems.at[j], recv_sem=recv_sems.at[j],
            device_id=(mx, my, 1 - mc), device_id_type=MESH) for j in range(nchip)]
        for cp in copies:
            cp.start()
        for cp in copies:
            cp.wait()

    return pl.pallas_call(
        body,
        name=name,
        out_shape=jax.ShapeDtypeStruct((nchip, r, c), p4.dtype),
        in_specs=[_ANY],
        out_specs=_ANY,
        scratch_shapes=[pltpu.SemaphoreType.DMA((nchip,)), pltpu.SemaphoreType.DMA((nchip,))],
    )(p4)


def _chip_exchange(q, name):
    _, r, c = q.shape

    def body(q_ref, out_ref, send_sems, recv_sems, local_sem):
        for phase in _chip_exchange_phases(q_ref, out_ref, send_sems, recv_sems, local_sem):
            phase()

    return pl.pallas_call(
        body,
        name=name,
        out_shape=jax.ShapeDtypeStruct((3, r, c), q.dtype),
        in_specs=[_ANY],
        out_specs=_ANY,
        scratch_shapes=list(_AG_SCRATCH),
    )(q)


def _rs_row_tile(r):
    return _row_tile(r, 1024)


def _pair_sum(place, p4, recv_a, name):
    nchip, _, r, c = p4.shape
    tr = _rs_row_tile(r)

    def body(place_ref, p_ref, a_ref, q_ref):
        q_ref[0] = (p_ref[0, 0].astype(F32) + a_ref[0].astype(F32)).astype(BF16)

    return pl.pallas_call(
        body,
        name=name,
        grid_spec=pltpu.PrefetchScalarGridSpec(
            num_scalar_prefetch=1,
            grid=(nchip, r // tr),
            in_specs=[pl.BlockSpec((1, 1, tr, c), lambda j, i, pref: (j, pref[0], i, 0)),
                      pl.BlockSpec((1, tr, c), lambda j, i, pref: (j, i, 0))],
            out_specs=pl.BlockSpec((1, tr, c), lambda j, i, pref: (j, i, 0)),
        ),
        out_shape=jax.ShapeDtypeStruct((nchip, r, c), BF16),
    )(place, p4, recv_a)


def _rs_final(place, p4, recv_a, recv_b, name):
    _, _, r, c = p4.shape
    tr = _rs_row_tile(r)

    def body(place_ref, p_ref, a_ref, b_ref, g_ref):
        g = p_ref[0, 0].astype(F32) + a_ref[0].astype(F32)
        for k in range(3):
            g = g + b_ref[k].astype(F32)
        g_ref[...] = g

    return pl.pallas_call(
        body,
        name=name,
        grid_spec=pltpu.PrefetchScalarGridSpec(
            num_scalar_prefetch=1,
            grid=(r // tr,),
            in_specs=[pl.BlockSpec((1, 1, tr, c), lambda i, pref: (pref[1], pref[0], i, 0)),
                      pl.BlockSpec((1, tr, c), lambda i, pref: (pref[1], i, 0)),
                      pl.BlockSpec((3, tr, c), lambda i, pref: (0, i, 0))],
            out_specs=pl.BlockSpec((tr, c), lambda i, pref: (i, 0)),
        ),
        out_shape=jax.ShapeDtypeStruct((r, c), F32),
    )(place, p4, recv_a, recv_b)


def _adamw_math(w, g, m, v):
    m = ADAM_B1 * m + (1.0 - ADAM_B1) * g
    v = ADAM_B2 * v + (1.0 - ADAM_B2) * (g * g)
    m_hat = m / (1.0 - ADAM_B1 ** ADAM_STEP)
    v_hat = v / (1.0 - ADAM_B2 ** ADAM_STEP)
    delta = -ADAM_LR * (m_hat / (jnp.sqrt(v_hat) + ADAM_EPS) + ADAM_WD * w)
    return delta, m, v


ADAMW_ROW_BLOCKS = 1


def _adamw(w, g, m, v, name):
    r, c = w.shape
    tr = r // ADAMW_ROW_BLOCKS

    def body(w_ref, g_ref, m_ref, v_ref, d_ref, mo_ref, vo_ref):
        d_ref[...], mo_ref[...], vo_ref[...] = _adamw_math(w_ref[...], g_ref[...], m_ref[...], v_ref[...])

    spec = pl.BlockSpec((tr, c), lambda i: (i, 0))
    out = jax.ShapeDtypeStruct((r, c), F32)
    return pl.pallas_call(
        body, name=name, grid=(r // tr,), in_specs=[spec] * 4, out_specs=[spec] * 3,
        out_shape=[out] * 3,
    )(w, g, m, v)


def _small_sum_adamw(gathered, ws, ms, vs):
    _, r, c = gathered.shape
    count = len(ws)

    def body(ga_ref, *refs):
        w_refs, m_refs, v_refs = refs[:count], refs[count:2 * count], refs[2 * count:3 * count]
        out_refs = refs[3 * count:]
        for o_ref in out_refs:
            o_ref[...] = jnp.zeros_like(o_ref)
        off = 0
        for w_ref, m_ref, v_ref in zip(w_refs, m_refs, v_refs):
            rows = pl.ds(off, w_ref.shape[0])
            g = ga_ref[0, rows, :]
            for k in range(1, N_DEV):
                g = g + ga_ref[k, rows, :]
            results = (g, *_adamw_math(w_ref[...], g, m_ref[...], v_ref[...]))
            for o_ref, val in zip(out_refs, results):
                o_ref[rows, :] = val
            off += w_ref.shape[0] + (-w_ref.shape[0]) % SMALL_ROW_ALIGN

    out = jax.ShapeDtypeStruct((r, c), F32)
    return pl.pallas_call(body, name="small_sum_adamw", out_shape=[out] * 4)(gathered, *ws, *ms, *vs)


_WEIGHTS = ["ffn1_pre_g", "ffn1_post_g", "ffn1_w_gate", "ffn1_w_up", "ffn1_w_down", "mix_pre_g",
            "mix_post_g", "w_in", "sgu_norm_g", "sgu_norm_b", "sgu_w_s", "sgu_b_s", "sgu_out_g",
            "sb_out_g", "w_out", "xa_pre_g", "xa_post_g", "mem_norm_g", "xa_w_q", "xa_w_kv", "xa_w_o",
            "ffn2_pre_g", "ffn2_post_g", "ffn2_w_gate", "ffn2_w_up", "ffn2_w_down", "final_norm_g"]
_BIG = ["ffn1_w_gate", "ffn1_w_up", "ffn1_w_down", "w_in", "w_out", "xa_w_q", "xa_w_kv", "xa_w_o",
        "ffn2_w_gate", "ffn2_w_up", "ffn2_w_down"]
_COL_SHARDED = ("ffn1_w_gate", "ffn1_w_up", "w_in", "xa_w_kv", "ffn2_w_gate", "ffn2_w_up")
_EARLY = ["ffn1_w_gate", "ffn1_w_up", "ffn1_w_down"]
GATHER_IN = {"ffn1_fwd": ["w_in", "ffn2_w_gate"],
             "sb_fwd": ["w_out", "xa_w_q", "xa_w_kv", "xa_w_o", "ffn2_w_up", "ffn2_w_down"]}
SCATTER_IN = {"xa_bwd": ["ffn2_w_gate"],
              "sb_bwd": ["ffn2_w_up", "ffn2_w_down", "xa_w_o", "w_out"],
              "sgu_bwd": ["xa_w_kv"],
              "inproj_bwd": ["xa_w_q"],
              "ffn1_bwd": ["w_in"]}
_SMALL = [n for n in _WEIGHTS if n not in _BIG]
SMALL_LANES = 128
SMALL_ROW_ALIGN = 8


def _pack_small(tensors):
    parts = []
    for n in _SMALL:
        t = tensors[n].reshape(-1, SMALL_LANES)
        pad = (-t.shape[0]) % SMALL_ROW_ALIGN
        parts.append(jnp.pad(t, ((0, pad), (0, 0))) if pad else t)
    return jnp.concatenate(parts, axis=0)


def _unpack_small(packed, like):
    out, off = {}, 0
    for n in _SMALL:
        size = like[n].size
        rows = size // SMALL_LANES
        out[n] = packed[off:off + rows].reshape(like[n].shape)
        off += rows + (-rows) % SMALL_ROW_ALIGN
    return out


def kernel(x, mem, ffn1_pre_g, ffn1_post_g, ffn1_w_gate, ffn1_w_up, ffn1_w_down, mix_pre_g, mix_post_g, w_in, sgu_norm_g, sgu_norm_b, sgu_w_s, sgu_b_s, sgu_out_g, sb_out_g, w_out, xa_pre_g, xa_post_g, mem_norm_g, xa_w_q, xa_w_kv, xa_w_o, ffn2_pre_g, ffn2_post_g, ffn2_w_gate, ffn2_w_up, ffn2_w_down, final_norm_g, loss_target, m_ffn1_pre_g, m_ffn1_post_g, m_ffn1_w_gate, m_ffn1_w_up, m_ffn1_w_down, m_mix_pre_g, m_mix_post_g, m_w_in, m_sgu_norm_g, m_sgu_norm_b, m_sgu_w_s, m_sgu_b_s, m_sgu_out_g, m_sb_out_g, m_w_out, m_xa_pre_g, m_xa_post_g, m_mem_norm_g, m_xa_w_q, m_xa_w_kv, m_xa_w_o, m_ffn2_pre_g, m_ffn2_post_g, m_ffn2_w_gate, m_ffn2_w_up, m_ffn2_w_down, m_final_norm_g, v_ffn1_pre_g, v_ffn1_post_g, v_ffn1_w_gate, v_ffn1_w_up, v_ffn1_w_down, v_mix_pre_g, v_mix_post_g, v_w_in, v_sgu_norm_g, v_sgu_norm_b, v_sgu_w_s, v_sgu_b_s, v_sgu_out_g, v_sb_out_g, v_w_out, v_xa_pre_g, v_xa_post_g, v_mem_norm_g, v_xa_w_q, v_xa_w_kv, v_xa_w_o, v_ffn2_pre_g, v_ffn2_post_g, v_ffn2_w_gate, v_ffn2_w_up, v_ffn2_w_down, v_final_norm_g):
    vals = dict(locals())
    d_model = x.shape[-1]

    def packed(names):
        return jnp.concatenate(
            [(vals[n][0].T if n in _COL_SHARDED else vals[n][0]).astype(BF16) for n in names], axis=0)

    shard_rows = {n: vals[n].shape[2 if n in _COL_SHARDED else 1] for n in _BIG}
    big = _unpack_rows(_all_gather(packed(_EARLY), "ag_weights"), _EARLY, shard_rows)

    small = {n: vals[n] for n in _SMALL}
    loss_part, dx, gathered_small, summed = _local_step(
        x[0], mem[0], loss_target[0], _small_views(small), big,
        {host: packed(names) for host, names in GATHER_IN.items()}, shard_rows)
    loss = lax.psum(loss_part[0, 0], ("x", "y", "c"))

    grads, deltas, new_m, new_v = {}, {}, {}, {}
    for names, g_rows in [([n], summed[n]) for n in _EARLY] + [(SCATTER_IN[h], summed[h]) for h in SCATTER_IN]:
        off = 0
        for n in names:
            rows = shard_rows[n]
            g = g_rows[off:off + rows]
            off += rows
            state = [vals[n][0], vals["m_" + n][0], vals["v_" + n][0]]
            flipped = n in _COL_SHARDED and rows % SMALL_LANES != 0
            if flipped:
                state = [t.T for t in state]
            elif n in _COL_SHARDED:
                g = g.T
            outs = (g, *_adamw(state[0], g, state[1], state[2], "adamw_" + n))
            if flipped:
                outs = tuple(t.T for t in outs)
            grads[n], deltas[n], new_m[n], new_v[n] = (t[None] for t in outs)

    outs = _small_sum_adamw(gathered_small,
                            *([vals[pre + n].reshape(-1, SMALL_LANES) for n in _SMALL] for pre in ("", "m_", "v_")))
    for dst, packed in zip((grads, deltas, new_m, new_v), outs):
        dst.update(_unpack_small(packed, small))

    return (loss, dx[None], *[grads[n] for n in _WEIGHTS], *[deltas[n] for n in _WEIGHTS],
            *[new_m[n] for n in _WEIGHTS], *[new_v[n] for n in _WEIGHTS])
```

```python
import functools

import jax
import jax.numpy as jnp
from jax import lax
from jax.experimental import pallas as pl
from jax.experimental.pallas import tpu as pltpu

F32 = jnp.float32
BF16 = jnp.bfloat16
EPS = 1e-6
MESH = pl.DeviceIdType.MESH
N_DEV = 8

SGU_GROUPS = 4
GROUP_DIM = 128
CHUNK = 128
SB_HEADS = 8
SB_HEAD_DIM = 64
Q_BLOCK = 128
XA_HEADS = 4
XA_HEAD_DIM = 256

ADAM_LR = 0.001
ADAM_B1 = 0.9
ADAM_B2 = 0.999
ADAM_EPS = 1e-08
ADAM_WD = 0.01
ADAM_STEP = 10

VMEM_LIMIT_V7X = 56 * 1024 * 1024
GELU_C0 = 0.7978845608028654
GELU_C1 = 0.044715


def _dot(a, b):
    return jnp.dot(a.astype(BF16), b.astype(BF16), preferred_element_type=F32)


def _dot_nt(a, b):
    return lax.dot_general(a.astype(BF16), b.astype(BF16), (((1,), (1,)), ((), ())),
                           preferred_element_type=F32)


def _dot_tn(a, b):
    return lax.dot_general(a.astype(BF16), b.astype(BF16), (((0,), (0,)), ((), ())),
                           preferred_element_type=F32)


def _rms(x, g):
    r = lax.rsqrt(jnp.mean(x * x, axis=-1, keepdims=True) + EPS)
    return x * r * g, r


def _rms_bwd(x, g, dy):
    r = lax.rsqrt(jnp.mean(x * x, axis=-1, keepdims=True) + EPS)
    xh = x * r
    gy = dy * g
    dx = r * (gy - xh * jnp.mean(gy * xh, axis=-1, keepdims=True))
    dg = jnp.sum(dy * xh, axis=0, keepdims=True)
    return dx, dg


def _sigmoid(x):
    return jax.nn.sigmoid(x)


def _gelu(x):
    t = jnp.tanh(GELU_C0 * (x + GELU_C1 * x * x * x))
    return 0.5 * x * (1.0 + t)


def _gelu_grad(x):
    t = jnp.tanh(GELU_C0 * (x + GELU_C1 * x * x * x))
    return 0.5 * (1.0 + t) + 0.5 * x * (1.0 - t * t) * GELU_C0 * (1.0 + 3.0 * GELU_C1 * x * x)


def _split_bf16(x):
    hi = x.astype(BF16)
    lo = (x - hi.astype(F32)).astype(BF16)
    return hi, lo


def _row_spec(tm, cols):
    return pl.BlockSpec((tm, cols), lambda i: (i, 0))


def _full_spec(shape, buffers=None):
    nd = len(shape)
    mode = None if buffers is None else pl.Buffered(buffers)
    return pl.BlockSpec(tuple(shape), lambda i: (0,) * nd, pipeline_mode=mode)


FFN_TOKEN_TILE = 512
XA_TOKEN_TILE = 512


def _token_tile(s):
    return min(256, s)


def _row_call(body, name, s, tiled_in, full_in, tiled_out, acc_out, gather=None, scatter=None, tm=None):
    tm = _token_tile(s) if tm is None else min(tm, s)
    steps = s // tm
    in_specs = [_row_spec(tm, a.shape[1]) for a in tiled_in] + [_full_spec(a.shape, buffers=1) for a in full_in]
    out_specs = [_row_spec(tm, c) for c, _ in tiled_out] + [_full_spec(sh) for sh, _ in acc_out]
    out_shape = [jax.ShapeDtypeStruct((s, c), dt) for c, dt in tiled_out]
    out_shape += [jax.ShapeDtypeStruct(sh, dt) for sh, dt in acc_out]
    operands = [*tiled_in, *full_in]
    scratch = []
    kernel_body = functools.partial(body)
    sent = gather if gather is not None else scatter
    if sent is not None:
        n_in, n_out = len(operands), len(out_shape)
        out_shape.append(jax.ShapeDtypeStruct(
            (N_DEV,) + sent.shape if gather is not None else sent.shape, sent.dtype))
        operands.append(sent)
        in_specs.append(_ANY)
        out_specs.append(_ANY)
        scratch = list(_AG_SCRATCH)

        def kernel_body(*refs):
            ins, sent_ref = refs[:n_in], refs[n_in]
            outs, landed_ref = refs[n_in + 1:n_in + 1 + n_out], refs[n_in + 1 + n_out]
            step = pl.program_id(0)
            if gather is not None:
                start, relay, forward, finish = _gather_phases(sent_ref, landed_ref, *refs[-3:])
            else:
                start, finish = _scatter_phases(sent_ref, landed_ref, *refs[-3:])
            pl.when(step == 0)(start)
            body(*ins, *outs)
            if gather is not None:
                pl.when(step == steps // 3)(relay)
                pl.when(step == (2 * steps) // 3)(forward)
            pl.when(step == steps - 1)(finish)

    return pl.pallas_call(
        kernel_body,
        name=name,
        grid=(steps,),
        in_specs=in_specs,
        out_specs=out_specs,
        out_shape=out_shape,
        scratch_shapes=scratch,
        compiler_params=pltpu.CompilerParams(
            dimension_semantics=("arbitrary",), vmem_limit_bytes=VMEM_LIMIT_V7X),
    )(*operands)


def _acc(ref, val):
    @pl.when(pl.program_id(0) == 0)
    def _():
        ref[...] = val

    @pl.when(pl.program_id(0) != 0)
    def _():
        ref[...] += val


def _ffn_fwd_tile(x_ref, pre_ref, post_ref, wgt_ref, wut_ref, wd_ref, n_ref, a_ref, b_ref, f_ref):
    x = x_ref[...]
    n, _ = _rms(x, pre_ref[...])
    nb = n.astype(BF16)
    n_ref[...] = nb
    a = _dot_nt(nb, wgt_ref[...])
    b = _dot_nt(nb, wut_ref[...])
    a_ref[...] = a.astype(BF16)
    b_ref[...] = b.astype(BF16)
    hmid = a * _sigmoid(a) * b
    f = jnp.dot(hmid.astype(BF16), wd_ref[...], preferred_element_type=F32)
    f_ref[...] = f
    y, _ = _rms(f, post_ref[...])
    return x + 0.5 * y


def _ffn_fwd_body(x_ref, pre_ref, post_ref, wgt_ref, wut_ref, wd_ref,
                  h_ref, n_ref, a_ref, b_ref, f_ref):
    h_ref[...] = _ffn_fwd_tile(x_ref, pre_ref, post_ref, wgt_ref, wut_ref, wd_ref, n_ref, a_ref, b_ref, f_ref)


def _ffn_loss_body(x_ref, t_ref, pre_ref, post_ref, gfin_ref, wgt_ref, wut_ref, wd_ref,
                   n_ref, a_ref, b_ref, f_ref, dh_ref, loss_ref, dg_ref):
    h = _ffn_fwd_tile(x_ref, pre_ref, post_ref, wgt_ref, wut_ref, wd_ref, n_ref, a_ref, b_ref, f_ref)
    d = h.shape[1]
    y, _ = _rms(h, gfin_ref[...])
    err = y - t_ref[...]
    part = (0.5 / d) * jnp.sum(jnp.sum(err * err, axis=1, keepdims=True), axis=0, keepdims=True)
    dh, dg = _rms_bwd(h, gfin_ref[...], err * (1.0 / d))
    dh_ref[...] = dh
    _acc(loss_ref, part)
    _acc(dg_ref, dg)


def _ffn_fwd(x, pre_g, post_g, wgt, wut, wd, name, gather=None):
    s, d = x.shape
    f = wgt.shape[0]
    return _row_call(_ffn_fwd_body, name, s, [x], [pre_g, post_g, wgt, wut, wd],
                     [(d, F32), (d, BF16), (f, BF16), (f, BF16), (d, F32)], [], gather=gather, tm=FFN_TOKEN_TILE)


def _ffn_loss_fwd(x, target, pre_g, post_g, final_g, wgt, wut, wd, name):
    s, d = x.shape
    f = wgt.shape[0]
    return _row_call(_ffn_loss_body, name, s, [x, target], [pre_g, post_g, final_g, wgt, wut, wd],
                     [(d, BF16), (f, BF16), (f, BF16), (d, F32), (d, F32)],
                     [((1, 1), F32), ((1, d), F32)], tm=FFN_TOKEN_TILE)


def _ffn_bwd_body(dh_ref, x_ref, a_ref, b_ref, f_ref, pre_ref, post_ref, wgt_ref, wut_ref, wd_ref,
                  dx_ref, da_ref, db_ref, hm_ref, df_ref, dpre_ref, dpost_ref):
    dh = dh_ref[...]
    df, dpost = _rms_bwd(f_ref[...], post_ref[...], 0.5 * dh)
    dfb = df.astype(BF16)
    df_ref[...] = dfb
    dhmid = _dot_nt(dfb, wd_ref[...])
    a = a_ref[...].astype(F32)
    b = b_ref[...].astype(F32)
    sig = _sigmoid(a)
    sa = a * sig
    hm_ref[...] = (sa * b).astype(BF16)
    dab = (dhmid * b * sig * (1.0 + a * (1.0 - sig))).astype(BF16)
    dbb = (dhmid * sa).astype(BF16)
    da_ref[...] = dab
    db_ref[...] = dbb
    dn = _dot(dab, wgt_ref[...]) + _dot(dbb, wut_ref[...])
    dxn, dpre = _rms_bwd(x_ref[...], pre_ref[...], dn)
    dx_ref[...] = dh + dxn
    _acc(dpre_ref, dpre)
    _acc(dpost_ref, dpost)


def _ffn_bwd(dh, x, a, b, f, pre_g, post_g, wgt, wut, wd, name, scatter=None):
    s, d = x.shape
    ff = wgt.shape[0]
    return _row_call(_ffn_bwd_body, name, s, [dh, x, a, b, f], [pre_g, post_g, wgt, wut, wd],
                     [(d, F32), (ff, BF16), (ff, BF16), (ff, BF16), (d, BF16)],
                     [((1, d), F32), ((1, d), F32)], scatter=scatter)


def _inproj_fwd_body(h_ref, g_ref, wt_ref, n_ref, uv_ref, qkv_ref):
    n, _ = _rms(h_ref[...], g_ref[...])
    nb = n.astype(BF16)
    n_ref[...] = nb
    proj = _dot_nt(nb, wt_ref[...])
    nuv = uv_ref.shape[1]
    uv_ref[...] = proj[:, :nuv]
    qkv_ref[...] = proj[:, nuv:].astype(BF16)


def _inproj_fwd(h, g, w_in_t):
    s, d = h.shape
    sgu_w = SGU_GROUPS * GROUP_DIM
    sb_w = SB_HEADS * SB_HEAD_DIM
    return _row_call(_inproj_fwd_body, "inproj_fwd", s, [h], [g, w_in_t],
                     [(d, BF16), (2 * sgu_w, F32), (3 * sb_w, BF16)], [], tm=XA_TOKEN_TILE)


def _inproj_bwd_body(dh_ref, dproj_ref, h_ref, g_ref, wt_ref, dhout_ref, dg_ref):
    dn = _dot(dproj_ref[...], wt_ref[...])
    dhn, dg = _rms_bwd(h_ref[...], g_ref[...], dn)
    dhout_ref[...] = dh_ref[...] + dhn
    _acc(dg_ref, dg)


def _inproj_bwd(dh, dproj, h, g, w_in_t, scatter=None):
    s, d = h.shape
    return _row_call(_inproj_bwd_body, "inproj_bwd", s, [dh, dproj, h], [g, w_in_t],
                     [(d, F32)], [((1, d), F32)], scatter=scatter, tm=XA_TOKEN_TILE)


def _causal_w(ws_ref, g):
    row = lax.broadcasted_iota(jnp.int32, (CHUNK, CHUNK), 0)
    col = lax.broadcasted_iota(jnp.int32, (CHUNK, CHUNK), 1)
    return jnp.where(row >= col, ws_ref[g], 0.0), row >= col


def _group_norm(v):
    mu = jnp.mean(v, axis=-1, keepdims=True)
    d = v - mu
    rstd = lax.rsqrt(jnp.mean(d * d, axis=-1, keepdims=True) + EPS)
    return d * rstd, rstd


def _sgu_fwd_body(uv_ref, ng_ref, nb_ref, ws_ref, bs_ref, out_ref):
    width = SGU_GROUPS * GROUP_DIM
    for c in range(uv_ref.shape[0] // CHUNK):
        rows = pl.ds(c * CHUNK, CHUNK)
        for g in range(SGU_GROUPS):
            lanes = pl.ds(g * GROUP_DIM, GROUP_DIM)
            u = _gelu(uv_ref[rows, lanes])
            v = _gelu(uv_ref[rows, pl.ds(width + g * GROUP_DIM, GROUP_DIM)])
            vhat, _ = _group_norm(v)
            vn = vhat * ng_ref[:, lanes] + nb_ref[:, lanes]
            w, _ = _causal_w(ws_ref, g)
            mixed = _dot(w, vn) + bs_ref[g]
            out_ref[rows, lanes] = u * mixed


def _sgu_fwd(uv_pre, ng, nb, ws, bs):
    s = uv_pre.shape[0]
    return _row_call(_sgu_fwd_body, "sgu_fwd", s, [uv_pre], [ng, nb, ws, bs],
                     [(SGU_GROUPS * GROUP_DIM, F32)], [])[0]


def _sgu_bwd_body(uv_ref, do_ref, ng_ref, nb_ref, ws_ref, bs_ref,
                  duv_ref, dws_ref, dbs_ref, dng_ref, dnb_ref):
    width = SGU_GROUPS * GROUP_DIM

    @pl.when(pl.program_id(0) == 0)
    def _():
        dws_ref[...] = jnp.zeros_like(dws_ref)
        dbs_ref[...] = jnp.zeros_like(dbs_ref)
        dng_ref[...] = jnp.zeros_like(dng_ref)
        dnb_ref[...] = jnp.zeros_like(dnb_ref)

    for c in range(uv_ref.shape[0] // CHUNK):
        rows = pl.ds(c * CHUNK, CHUNK)
        for g in range(SGU_GROUPS):
            lanes = pl.ds(g * GROUP_DIM, GROUP_DIM)
            vlanes = pl.ds(width + g * GROUP_DIM, GROUP_DIM)
            u_pre = uv_ref[rows, lanes]
            v_pre = uv_ref[rows, vlanes]
            u = _gelu(u_pre)
            v = _gelu(v_pre)
            vhat, rstd = _group_norm(v)
            gain = ng_ref[:, lanes]
            vn = vhat * gain + nb_ref[:, lanes]
            w, causal = _causal_w(ws_ref, g)
            mixed = _dot(w, vn) + bs_ref[g]
            dout = do_ref[rows, lanes]
            du = dout * mixed
            dmixed = dout * u
            dbs_ref[g] += jnp.sum(dmixed, axis=1, keepdims=True)
            dws_ref[g] += jnp.where(causal, _dot_nt(dmixed, vn), 0.0)
            dvn = _dot_tn(w, dmixed)
            dng_ref[:, lanes] += jnp.sum(dvn * vhat, axis=0, keepdims=True)
            dnb_ref[:, lanes] += jnp.sum(dvn, axis=0, keepdims=True)
            dvh = dvn * gain
            dv = rstd * (dvh - jnp.mean(dvh, axis=-1, keepdims=True)
                         - vhat * jnp.mean(dvh * vhat, axis=-1, keepdims=True))
            duv_ref[rows, lanes] = (du * _gelu_grad(u_pre)).astype(BF16)
            duv_ref[rows, vlanes] = (dv * _gelu_grad(v_pre)).astype(BF16)


def _sgu_bwd(uv_pre, dout_a, ng, nb, ws, bs, scatter=None):
    s = uv_pre.shape[0]
    width = SGU_GROUPS * GROUP_DIM
    return _row_call(_sgu_bwd_body, "sgu_bwd", s, [uv_pre, dout_a], [ng, nb, ws, bs],
                     [(2 * width, BF16)],
                     [(ws.shape, F32), (bs.shape, F32), ((1, width), F32), ((1, width), F32)], scatter=scatter)


def _mesh_place():
    return lax.axis_index("x"), lax.axis_index("y"), lax.axis_index("c")


def _other_chips(mx, my):
    return [(1 - mx, my), (mx, 1 - my), (1 - mx, 1 - my)]


_ANY = pl.BlockSpec(memory_space=pl.ANY)
AG_SEMS = 8
_AG_SCRATCH = [pltpu.SemaphoreType.DMA((AG_SEMS,)), pltpu.SemaphoreType.DMA((AG_SEMS,)),
               pltpu.SemaphoreType.DMA(())]
ROW_ALIGN_ANY_DTYPE = 16


def _gather_phases(x_ref, out_ref, send_sems, recv_sems, local_sem):
    mx, my, mc = _mesh_place()
    me, sibling = (mx, my, mc), (mx, my, 1 - mc)
    x_chip, y_chip, far_chip = _other_chips(mx, my)
    rows = x_ref.shape[0]
    cut = (rows // (2 * ROW_ALIGN_ANY_DTYPE)) * ROW_ALIGN_ANY_DTYPE
    parts = {3: pl.ds(0, cut), 7: pl.ds(cut, rows - cut)}

    def slot(px, py, pc):
        return out_ref.at[4 * px + 2 * py + pc]

    def copy(k, block, to, src=None):
        where = slot(*block) if k not in parts else slot(*block).at[parts[k]]
        return pltpu.make_async_remote_copy(
            src_ref=where if src is None else src, dst_ref=where,
            send_sem=send_sems.at[k], recv_sem=recv_sems.at[k],
            device_id=to, device_id_type=MESH)

    mine = pltpu.make_async_copy(x_ref, slot(*me), local_sem)
    first = [copy(0, me, sibling, src=x_ref), copy(1, me, (*x_chip, mc), src=x_ref),
             copy(2, me, (*y_chip, mc), src=x_ref)]
    relayed = [copy(3, (*x_chip, mc), (*y_chip, mc)), copy(7, (*y_chip, mc), (*x_chip, mc))]
    passed = [copy(4, (*x_chip, mc), sibling), copy(5, (*y_chip, mc), sibling), copy(6, (*far_chip, mc), sibling)]

    def start():
        mine.start()
        for cp in first:
            cp.start()

    def relay():
        copy(1, (*x_chip, mc), me).wait_recv()
        relayed[0].start()
        passed[0].start()
        copy(2, (*y_chip, mc), me).wait_recv()
        relayed[1].start()
        passed[1].start()

    def forward():
        copy(3, (*far_chip, mc), me).wait_recv()
        copy(7, (*far_chip, mc), me).wait_recv()
        passed[2].start()

    def finish():
        copy(0, sibling, me).wait_recv()
        for k, chip in ((4, x_chip), (5, y_chip), (6, far_chip)):
            copy(k, (*chip, 1 - mc), me).wait_recv()
        for cp in first + relayed + passed:
            cp.wait_send()
        mine.wait()

    return start, relay, forward, finish


def _all_gather(x, name):
    r, c = x.shape

    def body(x_ref, out_ref, send_sems, recv_sems, local_sem):
        for phase in _gather_phases(x_ref, out_ref, send_sems, recv_sems, local_sem):
            phase()

    return pl.pallas_call(
        body,
        name=name,
        out_shape=jax.ShapeDtypeStruct((N_DEV, r, c), x.dtype),
        in_specs=[_ANY],
        out_specs=_ANY,
        scratch_shapes=list(_AG_SCRATCH),
    )(x)


def _scatter_phases(p_ref, out_ref, send_sems, recv_sems, local_sem):
    mx, my, mc = _mesh_place()
    me = 4 * mx + 2 * my + mc
    copies = []
    for k in range(1, N_DEV):
        tx, ty, tc = mx ^ ((k >> 2) & 1), my ^ ((k >> 1) & 1), mc ^ (k & 1)
        copies.append(pltpu.make_async_remote_copy(
            src_ref=p_ref.at[4 * tx + 2 * ty + tc], dst_ref=out_ref.at[me],
            send_sem=send_sems.at[k - 1], recv_sem=recv_sems.at[k - 1],
            device_id=(tx, ty, tc), device_id_type=MESH))
    mine = pltpu.make_async_copy(p_ref.at[me], out_ref.at[me], local_sem)

    def start():
        mine.start()
        for cp in copies:
            cp.start()

    def finish():
        for cp in copies:
            cp.wait()
        mine.wait()

    return start, finish


def _chip_exchange_phases(q_ref, out_ref, send_sems, recv_sems, local_sem):
    mx, my, mc = _mesh_place()
    copies = [pltpu.make_async_remote_copy(
        src_ref=q_ref.at[2 * cx + cy], dst_ref=out_ref.at[k],
        send_sem=send_sems.at[k], recv_sem=recv_sems.at[k],
        device_id=(cx, cy, mc), device_id_type=MESH)
        for k, (cx, cy) in enumerate(_other_chips(mx, my))]

    def start():
        for cp in copies:
            cp.start()

    def finish():
        for cp in copies:
            cp.wait()

    return start, finish


SB_DEAD = -105.0
HEADS_PER_TILE = 2
TILES_PER_STEP = 2
HEADS_PER_STEP = HEADS_PER_TILE * TILES_PER_STEP
STEP_LANES = TILES_PER_STEP * HEADS_PER_TILE * SB_HEAD_DIM
TILE_LANES = HEADS_PER_TILE * SB_HEAD_DIM
STACK_ROWS = HEADS_PER_STEP * Q_BLOCK
TILE_ROWS = HEADS_PER_TILE * Q_BLOCK
STACKS_PER_STEP = 2
GROUP_LANES = STACKS_PER_STEP * STEP_LANES
HEADS_PER_GROUP = STACKS_PER_STEP * HEADS_PER_STEP
SB_FORWARD_LEAD = 6


def _stack_heads(x):
    lane = lax.broadcasted_iota(jnp.int32, x.shape, 1)
    zero = jnp.zeros_like(x)
    return jnp.concatenate(
        [jnp.where(lane // SB_HEAD_DIM == h, x, zero) for h in range(HEADS_PER_STEP)], axis=0)


def _unstack_tile(x):
    first = lax.broadcasted_iota(jnp.int32, (Q_BLOCK, TILE_LANES), 1) < SB_HEAD_DIM
    return jnp.where(first, x[:Q_BLOCK], x[Q_BLOCK:])


def _sb_logs(qs, k, diagonal):
    z = _dot_nt(qs, k) * (SB_HEAD_DIM ** -0.5)
    sp = jnp.log1p(jnp.exp(-jnp.abs(z)))
    log_beta = jnp.minimum(z, 0.0) - sp
    log_1m_raw = -jnp.maximum(z, 0.0) - sp
    if not diagonal:
        return None, log_beta, log_1m_raw, log_1m_raw
    row = lax.broadcasted_iota(jnp.int32, z.shape, 0)
    col = lax.broadcasted_iota(jnp.int32, z.shape, 1)
    strict = col < jnp.bitwise_and(row, Q_BLOCK - 1)
    return strict, log_beta, log_1m_raw, jnp.where(strict, log_1m_raw, 0.0)


def _masked(strict, x):
    return x if strict is None else jnp.where(strict, x, 0.0)


def _key_sums(x, pick):
    hi, lo = _split_bf16(x)
    both = jnp.dot(jnp.concatenate([hi, lo], axis=0), pick, preferred_element_type=F32)
    return both[:x.shape[0]] + both[x.shape[0]:]


def _key_order():
    row = lax.broadcasted_iota(jnp.int32, (Q_BLOCK, Q_BLOCK), 0)
    col = lax.broadcasted_iota(jnp.int32, (Q_BLOCK, Q_BLOCK), 1)
    return row, col


def _sb_fwd_body(q_ref, k_ref, v_ref, shard_ref, o_ref, tot_ref, cnt_ref, gathered_ref,
                 acc_ref, send_sems, recv_sems, local_sem):
    grp, qb = pl.program_id(0), pl.program_id(1)
    last_grp, last_qb = pl.num_programs(0) - 1, pl.num_programs(1) - 1
    ag_start, ag_relay, ag_forward, ag_finish = _gather_phases(
        shard_ref, gathered_ref, send_sems, recv_sems, local_sem)
    pl.when(jnp.logical_and(grp == 0, qb == 0))(ag_start)

    stacks = range(STACKS_PER_STEP)
    lanes = [slice(st * STEP_LANES, (st + 1) * STEP_LANES) for st in stacks]
    qs = [_stack_heads(q_ref[:, lanes[st]]) for st in stacks]
    row, col = _key_order()
    later = (row > col).astype(BF16)

    def block(i, cs, diagonal):
        rows = pl.ds(pl.multiple_of((qb - i) * Q_BLOCK, Q_BLOCK), Q_BLOCK)
        new_cs = []
        for st in stacks:
            strict, log_beta, _, log_1m = _sb_logs(qs[st], k_ref[rows, lanes[st]], diagonal)
            a = _masked(strict, jnp.exp(log_beta + _key_sums(log_1m, later) + cs[st])).astype(BF16)
            for t in range(TILES_PER_STEP):
                tile = st * TILES_PER_STEP + t
                part = jnp.dot(a[t * TILE_ROWS:(t + 1) * TILE_ROWS],
                               v_ref[rows, tile * TILE_LANES:(tile + 1) * TILE_LANES],
                               preferred_element_type=F32)
                if diagonal:
                    acc_ref[tile] = part
                else:
                    acc_ref[tile] += part
            new_cs.append(cs[st] + jnp.sum(log_1m, axis=1, keepdims=True))
        return tuple(new_cs)

    cs = block(0, (jnp.zeros((STACK_ROWS, 1), F32),) * STACKS_PER_STEP, True)

    def alive(carry):
        i, cs = carry
        return jnp.logical_and(i <= qb, jnp.max(functools.reduce(jnp.maximum, cs)) > SB_DEAD)

    def step(carry):
        i, cs = carry
        return i + 1, block(i, cs, False)

    n, cs = lax.while_loop(alive, step, (jnp.int32(1), cs))
    for tile in range(STACKS_PER_STEP * TILES_PER_STEP):
        o_ref[:, tile * TILE_LANES:(tile + 1) * TILE_LANES] = _unstack_tile(acc_ref[tile])
    for st in stacks:
        for h in range(HEADS_PER_STEP):
            tot_ref[st * HEADS_PER_STEP + h] = cs[st][h * Q_BLOCK:(h + 1) * Q_BLOCK]
    cnt_ref[grp, qb] = n.astype(F32)
    place = grp * (last_qb + 1) + qb
    last = (last_grp + 1) * (last_qb + 1) - 1
    pl.when(place == (last + 1) // 2)(ag_relay)
    pl.when(place == jnp.maximum(last - SB_FORWARD_LEAD, (last + 1) // 2))(ag_forward)
    pl.when(place == last)(ag_finish)


def _sb_fwd(qkv, shard):
    s = qkv.shape[0]
    groups = SB_HEADS // HEADS_PER_GROUP
    nq = s // Q_BLOCK
    return pl.pallas_call(
        functools.partial(_sb_fwd_body),
        name="sb_fwd",
        grid=(groups, nq),
        in_specs=[pl.BlockSpec((Q_BLOCK, GROUP_LANES), lambda g, i: (i, g)),
                  pl.BlockSpec((s, GROUP_LANES), lambda g, i: (0, groups + g)),
                  pl.BlockSpec((s, GROUP_LANES), lambda g, i: (0, 2 * groups + g)),
                  _ANY],
        out_specs=[pl.BlockSpec((Q_BLOCK, GROUP_LANES), lambda g, i: (i, g)),
                   pl.BlockSpec((HEADS_PER_GROUP, Q_BLOCK, 1), lambda g, i: (g, i, 0)),
                   pl.BlockSpec(memory_space=pltpu.SMEM),
                   _ANY],
        out_shape=[jax.ShapeDtypeStruct((s, SB_HEADS * SB_HEAD_DIM), F32),
                   jax.ShapeDtypeStruct((SB_HEADS, s, 1), F32),
                   jax.ShapeDtypeStruct((groups, nq), F32),
                   jax.ShapeDtypeStruct((N_DEV,) + shard.shape, shard.dtype)],
        scratch_shapes=[pltpu.VMEM((STACKS_PER_STEP * TILES_PER_STEP, TILE_ROWS, TILE_LANES), F32)]
        + list(_AG_SCRATCH),
        compiler_params=pltpu.CompilerParams(
            dimension_semantics=("arbitrary", "arbitrary"), vmem_limit_bytes=VMEM_LIMIT_V7X),
    )(qkv, qkv, qkv, shard)


def _sb_bwd_body(cnt_ref, q_ref, k_ref, v_ref, tot_ref, do_ref, part_ref, dq_ref, dk_ref, dv_ref, recv_ref,
                 acc_ref, send_sems, recv_sems, local_sem):
    grp, qb = pl.program_id(0), pl.program_id(1)
    last_grp, last_qb = pl.num_programs(0) - 1, pl.num_programs(1) - 1
    rs_start, rs_finish = _scatter_phases(part_ref, recv_ref, send_sems, recv_sems, local_sem)
    pl.when(jnp.logical_and(grp == 0, qb == 0))(rs_start)

    @pl.when(qb == 0)
    def _():
        dk_ref[...] = jnp.zeros_like(dk_ref)
        dv_ref[...] = jnp.zeros_like(dv_ref)

    acc_ref[...] = jnp.zeros_like(acc_ref)
    stacks = range(STACKS_PER_STEP)
    lanes = [slice(st * STEP_LANES, (st + 1) * STEP_LANES) for st in stacks]
    qs = [_stack_heads(q_ref[:, lanes[st]]) for st in stacks]
    dos = [_stack_heads(do_ref[:, lanes[st]].astype(BF16)) for st in stacks]
    tots = [jnp.concatenate([tot_ref[st * HEADS_PER_STEP + h] for h in range(HEADS_PER_STEP)], axis=0)
            for st in stacks]
    row, col = _key_order()
    up_to = (row <= col).astype(BF16)
    earlier = (row < col).astype(BF16)
    scale = SB_HEAD_DIM ** -0.5
    n = jnp.clip(cnt_ref[grp, qb].astype(jnp.int32), 1, qb + 1)

    def block(kb, cs, ces, diagonal):
        rows = pl.ds(pl.multiple_of(kb * Q_BLOCK, Q_BLOCK), Q_BLOCK)
        new_cs, new_ces = [], []
        for st in stacks:
            k = k_ref[rows, lanes[st]]
            strict, log_beta, log_1m_raw, log_1m = _sb_logs(qs[st], k, diagonal)
            suffix = tots[st] - cs[st] - _key_sums(log_1m, up_to)
            a = _masked(strict, jnp.exp(log_beta + suffix))
            de = _dot_nt(dos[st], v_ref[rows, lanes[st]]) * a
            before = ces[st] + _key_sums(de, earlier)
            dz = _masked(strict, de * jnp.exp(log_1m_raw) - before * jnp.exp(log_beta)).astype(BF16)
            for t in range(TILES_PER_STEP):
                acc_ref[st * TILES_PER_STEP + t] += jnp.dot(
                    dz[t * TILE_ROWS:(t + 1) * TILE_ROWS], k[:, t * TILE_LANES:(t + 1) * TILE_LANES],
                    preferred_element_type=F32)
            dk_ref[rows, lanes[st]] += _dot_tn(dz, qs[st]) * scale
            dv_ref[rows, lanes[st]] += _dot_tn(a, dos[st])
            new_cs.append(cs[st] + jnp.sum(log_1m, axis=1, keepdims=True))
            new_ces.append(ces[st] + jnp.sum(de, axis=1, keepdims=True))
        return tuple(new_cs), tuple(new_ces)

    def step(i, carry):
        return block(qb - n + 1 + i, *carry, False)

    zc = (jnp.zeros((STACK_ROWS, 1), F32),) * STACKS_PER_STEP
    cs, ces = lax.fori_loop(0, n - 1, step, (zc, zc))
    block(qb, cs, ces, True)
    for tile in range(STACKS_PER_STEP * TILES_PER_STEP):
        dq_ref[:, tile * TILE_LANES:(tile + 1) * TILE_LANES] = _unstack_tile(acc_ref[tile]) * scale
    pl.when(jnp.logical_and(grp == last_grp, qb == last_qb))(rs_finish)


def _sb_bwd(cnt, qkv, tot, dout_b, parts):
    s = qkv.shape[0]
    groups = SB_HEADS // HEADS_PER_GROUP
    return pl.pallas_call(
        functools.partial(_sb_bwd_body),
        name="sb_bwd",
        grid=(groups, s // Q_BLOCK),
        in_specs=[pl.BlockSpec(memory_space=pltpu.SMEM),
                  pl.BlockSpec((Q_BLOCK, GROUP_LANES), lambda g, i: (i, g)),
                  pl.BlockSpec((s, GROUP_LANES), lambda g, i: (0, groups + g)),
                  pl.BlockSpec((s, GROUP_LANES), lambda g, i: (0, 2 * groups + g)),
                  pl.BlockSpec((HEADS_PER_GROUP, Q_BLOCK, 1), lambda g, i: (g, i, 0)),
                  pl.BlockSpec((Q_BLOCK, GROUP_LANES), lambda g, i: (i, g)),
                  _ANY],
        out_specs=[pl.BlockSpec((Q_BLOCK, GROUP_LANES), lambda g, i: (i, g)),
                   pl.BlockSpec((s, GROUP_LANES), lambda g, i: (0, g)),
                   pl.BlockSpec((s, GROUP_LANES), lambda g, i: (0, g)),
                   _ANY],
        out_shape=[jax.ShapeDtypeStruct((s, SB_HEADS * SB_HEAD_DIM), F32)] * 3
        + [jax.ShapeDtypeStruct(parts.shape, parts.dtype)],
        scratch_shapes=[pltpu.VMEM((STACKS_PER_STEP * TILES_PER_STEP, TILE_ROWS, TILE_LANES), F32)]
        + list(_AG_SCRATCH),
        compiler_params=pltpu.CompilerParams(
            dimension_semantics=("arbitrary", "arbitrary"), vmem_limit_bytes=VMEM_LIMIT_V7X),
    )(cnt, qkv, qkv, qkv, tot, dout_b, parts)


def _outproj_fwd_body(oa_ref, ob_ref, h_ref, ga_ref, gb_ref, gpost_ref, w_ref,
                      merged_ref, mo_ref, hout_ref):
    half = oa_ref.shape[1]
    ma, _ = _rms(oa_ref[...], ga_ref[...])
    mb, _ = _rms(ob_ref[...], gb_ref[...])
    mab = ma.astype(BF16)
    mbb = mb.astype(BF16)
    merged_ref[:, :half] = mab
    merged_ref[:, half:] = mbb
    mo = (jnp.dot(mab, w_ref[:half, :], preferred_element_type=F32)
          + jnp.dot(mbb, w_ref[half:, :], preferred_element_type=F32))
    mo_ref[...] = mo
    y, _ = _rms(mo, gpost_ref[...])
    hout_ref[...] = h_ref[...] + y


def _outproj_fwd(out_a, out_b, h, ga, gb, gpost, w_out):
    s, d = h.shape
    return _row_call(_outproj_fwd_body, "outproj_fwd", s, [out_a, out_b, h], [ga, gb, gpost, w_out],
                     [(d, BF16), (d, F32), (d, F32)], [], tm=XA_TOKEN_TILE)


def _outproj_bwd_body(dh_ref, mo_ref, oa_ref, ob_ref, ga_ref, gb_ref, gpost_ref, w_ref,
                      dmo_ref, doa_ref, dob_ref, dga_ref, dgb_ref, dgpost_ref):
    half = oa_ref.shape[1]
    dmo, dgpost = _rms_bwd(mo_ref[...], gpost_ref[...], dh_ref[...])
    dmob = dmo.astype(BF16)
    dmo_ref[...] = dmob
    dma = _dot_nt(dmob, w_ref[:half, :])
    dmb = _dot_nt(dmob, w_ref[half:, :])
    doa, dga = _rms_bwd(oa_ref[...], ga_ref[...], dma)
    dob, dgb = _rms_bwd(ob_ref[...], gb_ref[...], dmb)
    doa_ref[...] = doa
    dob_ref[...] = dob
    _acc(dga_ref, dga)
    _acc(dgb_ref, dgb)
    _acc(dgpost_ref, dgpost)


def _outproj_bwd(dh, mo, out_a, out_b, ga, gb, gpost, w_out):
    s, d = dh.shape
    half = out_a.shape[1]
    return _row_call(_outproj_bwd_body, "outproj_bwd", s, [dh, mo, out_a, out_b], [ga, gb, gpost, w_out],
                     [(d, BF16), (half, F32), (half, F32)],
                     [((1, half), F32), ((1, half), F32), ((1, d), F32)], tm=XA_TOKEN_TILE)


def _kv_fwd_body(mem_ref, g_ref, wt_ref, memn_ref, kv_ref):
    n, _ = _rms(mem_ref[...], g_ref[...])
    nb = n.astype(BF16)
    memn_ref[...] = nb
    kv_ref[...] = _dot_nt(nb, wt_ref[...]).astype(BF16)


def _kv_fwd(mem, g, w_kv_t):
    m, d = mem.shape
    return _row_call(_kv_fwd_body, "kv_fwd", m, [mem], [g, w_kv_t], [(d, BF16), (w_kv_t.shape[0], BF16)], [])


def _kv_bwd_body(dkv_ref, mem_ref, memn_ref, g_ref, wt_ref, dwt_ref, dg_ref):
    dkvb = dkv_ref[...].astype(BF16)
    dwt_ref[...] = _dot_tn(dkvb, memn_ref[...]).astype(BF16)
    dmemn = _dot(dkvb, wt_ref[...])
    _, dg = _rms_bwd(mem_ref[...], g_ref[...], dmemn)
    dg_ref[...] = dg


def _kv_bwd(dkv, mem, memn, g, w_kv_t):
    m, d = mem.shape
    return pl.pallas_call(
        functools.partial(_kv_bwd_body),
        name="kv_bwd",
        out_shape=[jax.ShapeDtypeStruct(w_kv_t.shape, BF16), jax.ShapeDtypeStruct((1, d), F32)],
        compiler_params=pltpu.CompilerParams(vmem_limit_bytes=VMEM_LIMIT_V7X),
    )(dkv, mem, memn, g, w_kv_t)


def _xa_fwd_body(h_ref, gpre_ref, gpost_ref, wq_ref, wo_ref, kv_ref,
                 n_ref, q_ref, o_ref, c_ref, hout_ref):
    h = h_ref[...]
    d = h.shape[1]
    n, _ = _rms(h, gpre_ref[...])
    nb = n.astype(BF16)
    n_ref[...] = nb
    qb = jnp.dot(nb, wq_ref[...], preferred_element_type=F32).astype(BF16)
    q_ref[...] = qb
    for hd in range(XA_HEADS):
        lanes = slice(hd * XA_HEAD_DIM, (hd + 1) * XA_HEAD_DIM)
        k = kv_ref[:, lanes]
        v = kv_ref[:, d + hd * XA_HEAD_DIM:d + (hd + 1) * XA_HEAD_DIM]
        logits = _dot_nt(qb[:, lanes], k) * (XA_HEAD_DIM ** -0.5)
        e = jnp.exp(logits - jnp.max(logits, axis=-1, keepdims=True))
        p = e / jnp.sum(e, axis=-1, keepdims=True)
        o_ref[:, lanes] = jnp.dot(p.astype(BF16), v, preferred_element_type=F32).astype(BF16)
    c = jnp.dot(o_ref[...], wo_ref[...], preferred_element_type=F32)
    c_ref[...] = c
    y, _ = _rms(c, gpost_ref[...])
    hout_ref[...] = h + y


def _xa_fwd(h, gpre, gpost, wq, wo, kv):
    s, d = h.shape
    return _row_call(_xa_fwd_body, "xa_fwd", s, [h], [gpre, gpost, wq, wo, kv],
                     [(d, BF16), (d, BF16), (d, BF16), (d, F32), (d, F32)], [], tm=XA_TOKEN_TILE)


def _xa_bwd_body(dh_ref, h_ref, c_ref, q_ref, o_ref, gpre_ref, gpost_ref, wq_ref, wo_ref, kv_ref,
                 dhout_ref, dc_ref, dq_ref, dkv_ref, dgpre_ref, dgpost_ref):
    dh = dh_ref[...]
    d = dh.shape[1]
    scale = XA_HEAD_DIM ** -0.5
    dc, dgpost = _rms_bwd(c_ref[...], gpost_ref[...], dh)
    dcb = dc.astype(BF16)
    dc_ref[...] = dcb
    dob = _dot_nt(dcb, wo_ref[...]).astype(BF16)

    @pl.when(pl.program_id(0) == 0)
    def _():
        dkv_ref[...] = jnp.zeros_like(dkv_ref)

    dqs, dks, dvs = [], [], []
    for hd in range(XA_HEADS):
        lanes = slice(hd * XA_HEAD_DIM, (hd + 1) * XA_HEAD_DIM)
        vlanes = slice(d + hd * XA_HEAD_DIM, d + (hd + 1) * XA_HEAD_DIM)
        qh = q_ref[:, lanes]
        k = kv_ref[:, lanes]
        v = kv_ref[:, vlanes]
        logits = _dot_nt(qh, k) * scale
        e = jnp.exp(logits - jnp.max(logits, axis=-1, keepdims=True))
        p = e / jnp.sum(e, axis=-1, keepdims=True)
        doh = dob[:, lanes]
        dp = _dot_nt(doh, v)
        dl = (p * (dp - jnp.sum(dp * p, axis=-1, keepdims=True)) * scale).astype(BF16)
        dqs.append(jnp.dot(dl, k, preferred_element_type=F32).astype(BF16))
        dks.append(_dot_tn(dl, qh))
        dvs.append(_dot_tn(p, doh))
    dq = jnp.concatenate(dqs, axis=1)
    dq_ref[...] = dq
    dkv_ref[...] += jnp.concatenate(dks + dvs, axis=1)
    dn = _dot_nt(dq, wq_ref[...])
    dhn, dgpre = _rms_bwd(h_ref[...], gpre_ref[...], dn)
    dhout_ref[...] = dh + dhn
    _acc(dgpre_ref, dgpre)
    _acc(dgpost_ref, dgpost)


def _xa_bwd(dh, h, c, q, o, gpre, gpost, wq, wo, kv, scatter=None):
    s, d = h.shape
    return _row_call(_xa_bwd_body, "xa_bwd", s, [dh, h, c, q, o], [gpre, gpost, wq, wo, kv],
                     [(d, F32), (d, BF16), (d, BF16)],
                     [(kv.shape, F32), ((1, d), F32), ((1, d), F32)], scatter=scatter, tm=XA_TOKEN_TILE)


def _largest_tile(n, cap):
    best = 128
    for t in range(128, cap + 1, 128):
        if n % t == 0:
            best = t
    return best


def _mm_tn(a, bs, name, gather=None, chip_exchange=None):
    s, k = a.shape
    n = bs[0].shape[1]
    nb = len(bs)
    ts = min(2048, s)
    tk = _largest_tile(k, 1536)
    tn = _largest_tile(n, 1536 // nb)

    steps = s // ts
    grid = (k // tk, n // tn, steps)
    sent = gather if gather is not None else chip_exchange
    hosted = sent is not None
    phases = _gather_phases if gather is not None else _chip_exchange_phases
    landed_shape = None
    if hosted:
        landed_shape = (N_DEV,) + sent.shape if gather is not None else (3,) + sent.shape[1:]

    def body(a_ref, *refs):
        if hosted:
            sent_ref, landed_ref, refs = refs[nb], refs[2 * nb + 1], refs[:nb] + refs[nb + 1:2 * nb + 1] + refs[2 * nb + 2:]
            start, *rest = phases(sent_ref, landed_ref, *refs[-3:])
            place = (pl.program_id(0) * grid[1] + pl.program_id(1)) * grid[2] + pl.program_id(2)
            pl.when(place == 0)(start)
        b_refs, o_refs, acc_refs = refs[:nb], refs[nb:2 * nb], refs[2 * nb:3 * nb]
        at = a_ref[...]
        t = pl.program_id(2)

        @pl.when(t == 0)
        def _():
            for acc_ref in acc_refs:
                acc_ref[...] = jnp.zeros_like(acc_ref)

        for b_ref, acc_ref in zip(b_refs, acc_refs):
            acc_ref[...] += _dot_tn(at, b_ref[...])

        @pl.when(t == steps - 1)
        def _():
            for o_ref, acc_ref in zip(o_refs, acc_refs):
                o_ref[...] = acc_ref[...].astype(BF16)

        if hosted:
            total = grid[0] * grid[1] * grid[2]
            for j, phase in enumerate(rest):
                pl.when(place == max(total - len(rest) + j, 0))(phase)

    return pl.pallas_call(
        body,
        name=name,
        grid=grid,
        in_specs=[pl.BlockSpec((ts, tk), lambda i, j, t: (t, i))]
        + [pl.BlockSpec((ts, tn), lambda i, j, t: (t, j))] * nb + ([_ANY] if hosted else []),
        out_specs=[pl.BlockSpec((tk, tn), lambda i, j, t: (i, j))] * nb + ([_ANY] if hosted else []),
        out_shape=[jax.ShapeDtypeStruct((k, n), BF16)] * nb
        + ([jax.ShapeDtypeStruct(landed_shape, sent.dtype)] if hosted else []),
        scratch_shapes=[pltpu.VMEM((tk, tn), F32)] * nb + (list(_AG_SCRATCH) if hosted else []),
        compiler_params=pltpu.CompilerParams(
            dimension_semantics=("arbitrary", "arbitrary", "arbitrary"),
            vmem_limit_bytes=VMEM_LIMIT_V7X),
    )(a, *bs, *([sent] if hosted else []))


_SMALL_SHAPES = {
    "sgu_norm_g": (1, SGU_GROUPS * GROUP_DIM),
    "sgu_norm_b": (1, SGU_GROUPS * GROUP_DIM),
    "sgu_w_s": (SGU_GROUPS, CHUNK, CHUNK),
    "sgu_b_s": (SGU_GROUPS, CHUNK, 1),
}


def _small_views(small):
    return {n: v.reshape(_SMALL_SHAPES.get(n, v.shape)) for n, v in small.items()}


def _small_unviews(views, like):
    return {n: v.reshape(like[n].shape) for n, v in views.items()}


def _unpack_rows(gathered, names, shard_rows):
    out, off = {}, 0
    for n in names:
        rows = shard_rows[n]
        out[n] = gathered[:, off:off + rows, :].reshape(N_DEV * rows, gathered.shape[2])
        off += rows
    return out


def _row_tile(r, cap):
    best = 16
    for t in range(16, cap + 1, 16):
        if r % t == 0:
            best = t
    return best


def _sum_received(received, name):
    _, r, c = received.shape
    tr = _row_tile(r, 1024)

    def body(rc_ref, g_ref):
        g = rc_ref[0].astype(F32)
        for t in range(1, N_DEV):
            g = g + rc_ref[t].astype(F32)
        g_ref[...] = g

    return pl.pallas_call(
        body, name=name, grid=(r // tr,),
        in_specs=[pl.BlockSpec((N_DEV, tr, c), lambda i: (0, i, 0))],
        out_specs=pl.BlockSpec((tr, c), lambda i: (i, 0)),
        out_shape=jax.ShapeDtypeStruct((r, c), F32),
    )(received)


def _local_step(x, mem, target, small, big, shards, shard_rows):
    sm, w = small, dict(big)
    d_model = x.shape[1]

    def parts(names):
        return jnp.concatenate([gw.pop(n).reshape(N_DEV, -1, d_model) for n in names], axis=1)

    h1, n1, a1, b1, f1, landed = _ffn_fwd(
        x, sm["ffn1_pre_g"], sm["ffn1_post_g"], w["ffn1_w_gate"], w["ffn1_w_up"], w["ffn1_w_down"],
        "ffn1_fwd", gather=shards["ffn1_fwd"])
    w.update(_unpack_rows(landed, GATHER_IN["ffn1_fwd"], shard_rows))
    n2, uv_pre, qkv = _inproj_fwd(h1, sm["mix_pre_g"], w["w_in"])
    out_a = _sgu_fwd(uv_pre, sm["sgu_norm_g"], sm["sgu_norm_b"], sm["sgu_w_s"], sm["sgu_b_s"])
    out_b, tot, cnt, landed = _sb_fwd(qkv, shards["sb_fwd"])
    w.update(_unpack_rows(landed, GATHER_IN["sb_fwd"], shard_rows))
    merged, mo, h2 = _outproj_fwd(out_a, out_b, h1, sm["sgu_out_g"], sm["sb_out_g"],
                                  sm["mix_post_g"], w["w_out"])
    memn, kv = _kv_fwd(mem, sm["mem_norm_g"], w["xa_w_kv"])
    n3, qx, ox, cx, h3 = _xa_fwd(h2, sm["xa_pre_g"], sm["xa_post_g"], w["xa_w_q"], w["xa_w_o"], kv)
    n4, a2, b2, f2, dh4, loss, dg_final = _ffn_loss_fwd(
        h3, target, sm["ffn2_pre_g"], sm["ffn2_post_g"], sm["final_norm_g"],
        w["ffn2_w_gate"], w["ffn2_w_up"], w["ffn2_w_down"], "ffn2_fwd")

    gs, gw = {"final_norm_g": dg_final}, {}
    dh3, da2, db2, hm2, df2, gs["ffn2_pre_g"], gs["ffn2_post_g"] = _ffn_bwd(
        dh4, h3, a2, b2, f2, sm["ffn2_pre_g"], sm["ffn2_post_g"],
        w["ffn2_w_gate"], w["ffn2_w_up"], w["ffn2_w_down"], "ffn2_bwd")
    gw["ffn2_w_gate"], = _mm_tn(da2, [n4], "ffn2_dw_gate")
    gw["ffn2_w_up"], = _mm_tn(db2, [n4], "ffn2_dw_up")
    gw["ffn2_w_down"], = _mm_tn(hm2, [df2], "ffn2_dw_down")

    received = {}
    dh2, dc, dqx, dkv, gs["xa_pre_g"], gs["xa_post_g"], received["xa_bwd"] = _xa_bwd(
        dh3, h2, cx, qx, ox, sm["xa_pre_g"], sm["xa_post_g"], w["xa_w_q"], w["xa_w_o"], kv,
        scatter=parts(SCATTER_IN["xa_bwd"]))
    gw["xa_w_o"], = _mm_tn(ox, [dc], "xa_dw_o")
    gw["xa_w_q"], = _mm_tn(n3, [dqx], "xa_dw_q")
    gw["xa_w_kv"], gs["mem_norm_g"] = _kv_bwd(dkv, mem, memn, sm["mem_norm_g"], w["xa_w_kv"])

    dmo, dout_a, dout_b, gs["sgu_out_g"], gs["sb_out_g"], gs["mix_post_g"] = _outproj_bwd(
        dh2, mo, out_a, out_b, sm["sgu_out_g"], sm["sb_out_g"], sm["mix_post_g"], w["w_out"])
    gw["w_out"], = _mm_tn(merged, [dmo], "mix_dw_out")
    dq, dk, dv, received["sb_bwd"] = _sb_bwd(cnt, qkv, tot, dout_b, parts(SCATTER_IN["sb_bwd"]))
    duv, gs["sgu_w_s"], gs["sgu_b_s"], gs["sgu_norm_g"], gs["sgu_norm_b"], received["sgu_bwd"] = _sgu_bwd(
        uv_pre, dout_a, sm["sgu_norm_g"], sm["sgu_norm_b"], sm["sgu_w_s"], sm["sgu_b_s"],
        scatter=parts(SCATTER_IN["sgu_bwd"]))
    dproj = jnp.concatenate([duv] + [t.astype(BF16) for t in (dq, dk, dv)], axis=1)
    dh1, gs["mix_pre_g"], received["inproj_bwd"] = _inproj_bwd(
        dh2, dproj, h1, sm["mix_pre_g"], w["w_in"], scatter=parts(SCATTER_IN["inproj_bwd"]))
    gw["w_in"], = _mm_tn(dproj, [n2], "mix_dw_in")

    dx, da1, db1, hm1, df1, gs["ffn1_pre_g"], gs["ffn1_post_g"], received["ffn1_bwd"] = _ffn_bwd(
        dh1, x, a1, b1, f1, sm["ffn1_pre_g"], sm["ffn1_post_g"],
        w["ffn1_w_gate"], w["ffn1_w_up"], w["ffn1_w_down"], "ffn1_bwd",
        scatter=parts(SCATTER_IN["ffn1_bwd"]))
    summed = {host: _sum_received(r, "rs_sum_" + host) for host, r in received.items()}

    mx, my, mc = _mesh_place()
    place = jnp.stack([mc, 2 * mx + my]).astype(jnp.int32)

    def pair_stage(dw, tag):
        p4 = dw.reshape(N_DEV // 2, 2, -1, d_model)
        recv_a = _pair_exchange(p4, "rs_pair_exchange_" + tag)
        return p4, recv_a, _pair_sum(place, p4, recv_a, "rs_pair_sum_" + tag)

    dw, small_grads = _mm_tn(da1, [n1], "ffn1_dw_gate", gather=_pack_small(gs))
    gate = pair_stage(dw, "gate")
    dw, gate_b = _mm_tn(db1, [n1], "ffn1_dw_up", chip_exchange=gate[2])
    up = pair_stage(dw, "up")
    dw, up_b = _mm_tn(hm1, [df1], "ffn1_dw_down", chip_exchange=up[2])
    down = pair_stage(dw, "down")
    down_b = _chip_exchange(down[2], "rs_chip_exchange_down")
    for n, (p4, recv_a, _), recv_b in (("ffn1_w_gate", gate, gate_b), ("ffn1_w_up", up, up_b),
                                       ("ffn1_w_down", down, down_b)):
        summed[n] = _rs_final(place, p4, recv_a, recv_b, "rs_final_" + n)
    return loss, dx, small_grads, summed


def _pair_exchange(p4, name):
    nchip, _, r, c = p4.shape

    def body(p_ref, out_ref, send_sems, recv_sems):
        mx, my, mc = _mesh_place()
        copies = [pltpu.make_async_remote_copy(
            src_ref=p_ref.at[j, 1 - mc], dst_ref=out_ref.at[j],
            send_sem=send_sems.at[j], recv_sem=recv_sems.at[j],
            device_id=(mx, my, 1 - mc), device_id_type=MESH) for j in range(nchip)]
        for cp in copies:
            cp.start()
        for cp in copies:
            cp.wait()

    return pl.pallas_call(
        body,
        name=name,
        out_shape=jax.ShapeDtypeStruct((nchip, r, c), p4.dtype),
        in_specs=[_ANY],
        out_specs=_ANY,
        scratch_shapes=[pltpu.SemaphoreType.DMA((nchip,)), pltpu.SemaphoreType.DMA((nchip,))],
    )(p4)


def _chip_exchange(q, name):
    _, r, c = q.shape

    def body(q_ref, out_ref, send_sems, recv_sems, local_sem):
        for phase in _chip_exchange_phases(q_ref, out_ref, send_sems, recv_sems, local_sem):
            phase()

    return pl.pallas_call(
        body,
        name=name,
        out_shape=jax.ShapeDtypeStruct((3, r, c), q.dtype),
        in_specs=[_ANY],
        out_specs=_ANY,
        scratch_shapes=list(_AG_SCRATCH),
    )(q)


def _rs_row_tile(r):
    return _row_tile(r, 1024)


def _pair_sum(place, p4, recv_a, name):
    nchip, _, r, c = p4.shape
    tr = _rs_row_tile(r)

    def body(place_ref, p_ref, a_ref, q_ref):
        q_ref[0] = (p_ref[0, 0].astype(F32) + a_ref[0].astype(F32)).astype(BF16)

    return pl.pallas_call(
        body,
        name=name,
        grid_spec=pltpu.PrefetchScalarGridSpec(
            num_scalar_prefetch=1,
            grid=(nchip, r // tr),
            in_specs=[pl.BlockSpec((1, 1, tr, c), lambda j, i, pref: (j, pref[0], i, 0)),
                      pl.BlockSpec((1, tr, c), lambda j, i, pref: (j, i, 0))],
            out_specs=pl.BlockSpec((1, tr, c), lambda j, i, pref: (j, i, 0)),
        ),
        out_shape=jax.ShapeDtypeStruct((nchip, r, c), BF16),
    )(place, p4, recv_a)


def _rs_final(place, p4, recv_a, recv_b, name):
    _, _, r, c = p4.shape
    tr = _rs_row_tile(r)

    def body(place_ref, p_ref, a_ref, b_ref, g_ref):
        g = p_ref[0, 0].astype(F32) + a_ref[0].astype(F32)
        for k in range(3):
            g = g + b_ref[k].astype(F32)
        g_ref[...] = g

    return pl.pallas_call(
        body,
        name=name,
        grid_spec=pltpu.PrefetchScalarGridSpec(
            num_scalar_prefetch=1,
            grid=(r // tr,),
            in_specs=[pl.BlockSpec((1, 1, tr, c), lambda i, pref: (pref[1], pref[0], i, 0)),
                      pl.BlockSpec((1, tr, c), lambda i, pref: (pref[1], i, 0)),
                      pl.BlockSpec((3, tr, c), lambda i, pref: (0, i, 0))],
            out_specs=pl.BlockSpec((tr, c), lambda i, pref: (i, 0)),
        ),
        out_shape=jax.ShapeDtypeStruct((r, c), F32),
    )(place, p4, recv_a, recv_b)


def _adamw_math(w, g, m, v):
    m = ADAM_B1 * m + (1.0 - ADAM_B1) * g
    v = ADAM_B2 * v + (1.0 - ADAM_B2) * (g * g)
    m_hat = m / (1.0 - ADAM_B1 ** ADAM_STEP)
    v_hat = v / (1.0 - ADAM_B2 ** ADAM_STEP)
    delta = -ADAM_LR * (m_hat / (jnp.sqrt(v_hat) + ADAM_EPS) + ADAM_WD * w)
    return delta, m, v


ADAMW_ROW_BLOCKS = 1


def _adamw(w, g, m, v, name):
    r, c = w.shape
    tr = r // ADAMW_ROW_BLOCKS

    def body(w_ref, g_ref, m_ref, v_ref, d_ref, mo_ref, vo_ref):
        d_ref[...], mo_ref[...], vo_ref[...] = _adamw_math(w_ref[...], g_ref[...], m_ref[...], v_ref[...])

    spec = pl.BlockSpec((tr, c), lambda i: (i, 0))
    out = jax.ShapeDtypeStruct((r, c), F32)
    return pl.pallas_call(
        body, name=name, grid=(r // tr,), in_specs=[spec] * 4, out_specs=[spec] * 3,
        out_shape=[out] * 3,
    )(w, g, m, v)


def _small_sum_adamw(gathered, ws, ms, vs):
    _, r, c = gathered.shape
    count = len(ws)

    def body(ga_ref, *refs):
        w_refs, m_refs, v_refs = refs[:count], refs[count:2 * count], refs[2 * count:3 * count]
        out_refs = refs[3 * count:]
        for o_ref in out_refs:
            o_ref[...] = jnp.zeros_like(o_ref)
        off = 0
        for w_ref, m_ref, v_ref in zip(w_refs, m_refs, v_refs):
            rows = pl.ds(off, w_ref.shape[0])
            g = ga_ref[0, rows, :]
            for k in range(1, N_DEV):
                g = g + ga_ref[k, rows, :]
            results = (g, *_adamw_math(w_ref[...], g, m_ref[...], v_ref[...]))
            for o_ref, val in zip(out_refs, results):
                o_ref[rows, :] = val
            off += w_ref.shape[0] + (-w_ref.shape[0]) % SMALL_ROW_ALIGN

    out = jax.ShapeDtypeStruct((r, c), F32)
    return pl.pallas_call(body, name="small_sum_adamw", out_shape=[out] * 4)(gathered, *ws, *ms, *vs)


_WEIGHTS = ["ffn1_pre_g", "ffn1_post_g", "ffn1_w_gate", "ffn1_w_up", "ffn1_w_down", "mix_pre_g",
            "mix_post_g", "w_in", "sgu_norm_g", "sgu_norm_b", "sgu_w_s", "sgu_b_s", "sgu_out_g",
            "sb_out_g", "w_out", "xa_pre_g", "xa_post_g", "mem_norm_g", "xa_w_q", "xa_w_kv", "xa_w_o",
            "ffn2_pre_g", "ffn2_post_g", "ffn2_w_gate", "ffn2_w_up", "ffn2_w_down", "final_norm_g"]
_BIG = ["ffn1_w_gate", "ffn1_w_up", "ffn1_w_down", "w_in", "w_out", "xa_w_q", "xa_w_kv", "xa_w_o",
        "ffn2_w_gate", "ffn2_w_up", "ffn2_w_down"]
_COL_SHARDED = ("ffn1_w_gate", "ffn1_w_up", "w_in", "xa_w_kv", "ffn2_w_gate", "ffn2_w_up")
_EARLY = ["ffn1_w_gate", "ffn1_w_up", "ffn1_w_down"]
GATHER_IN = {"ffn1_fwd": ["w_in", "ffn2_w_gate"],
             "sb_fwd": ["w_out", "xa_w_q", "xa_w_kv", "xa_w_o", "ffn2_w_up", "ffn2_w_down"]}
SCATTER_IN = {"xa_bwd": ["ffn2_w_gate"],
              "sb_bwd": ["ffn2_w_up", "ffn2_w_down", "w_out"],
              "sgu_bwd": ["xa_w_q"],
              "inproj_bwd": ["xa_w_o"],
              "ffn1_bwd": ["w_in", "xa_w_kv"]}
_SMALL = [n for n in _WEIGHTS if n not in _BIG]
SMALL_LANES = 128
SMALL_ROW_ALIGN = 8


def _pack_small(tensors):
    parts = []
    for n in _SMALL:
        t = tensors[n].reshape(-1, SMALL_LANES)
        pad = (-t.shape[0]) % SMALL_ROW_ALIGN
        parts.append(jnp.pad(t, ((0, pad), (0, 0))) if pad else t)
    return jnp.concatenate(parts, axis=0)


def _unpack_small(packed, like):
    out, off = {}, 0
    for n in _SMALL:
        size = like[n].size
        rows = size // SMALL_LANES
        out[n] = packed[off:off + rows].reshape(like[n].shape)
        off += rows + (-rows) % SMALL_ROW_ALIGN
    return out


def kernel(x, mem, ffn1_pre_g, ffn1_post_g, ffn1_w_gate, ffn1_w_up, ffn1_w_down, mix_pre_g, mix_post_g, w_in, sgu_norm_g, sgu_norm_b, sgu_w_s, sgu_b_s, sgu_out_g, sb_out_g, w_out, xa_pre_g, xa_post_g, mem_norm_g, xa_w_q, xa_w_kv, xa_w_o, ffn2_pre_g, ffn2_post_g, ffn2_w_gate, ffn2_w_up, ffn2_w_down, final_norm_g, loss_target, m_ffn1_pre_g, m_ffn1_post_g, m_ffn1_w_gate, m_ffn1_w_up, m_ffn1_w_down, m_mix_pre_g, m_mix_post_g, m_w_in, m_sgu_norm_g, m_sgu_norm_b, m_sgu_w_s, m_sgu_b_s, m_sgu_out_g, m_sb_out_g, m_w_out, m_xa_pre_g, m_xa_post_g, m_mem_norm_g, m_xa_w_q, m_xa_w_kv, m_xa_w_o, m_ffn2_pre_g, m_ffn2_post_g, m_ffn2_w_gate, m_ffn2_w_up, m_ffn2_w_down, m_final_norm_g, v_ffn1_pre_g, v_ffn1_post_g, v_ffn1_w_gate, v_ffn1_w_up, v_ffn1_w_down, v_mix_pre_g, v_mix_post_g, v_w_in, v_sgu_norm_g, v_sgu_norm_b, v_sgu_w_s, v_sgu_b_s, v_sgu_out_g, v_sb_out_g, v_w_out, v_xa_pre_g, v_xa_post_g, v_mem_norm_g, v_xa_w_q, v_xa_w_kv, v_xa_w_o, v_ffn2_pre_g, v_ffn2_post_g, v_ffn2_w_gate, v_ffn2_w_up, v_ffn2_w_down, v_final_norm_g):
    vals = dict(locals())
    d_model = x.shape[-1]

    def packed(names):
        return jnp.concatenate(
            [(vals[n][0].T if n in _COL_SHARDED else vals[n][0]).astype(BF16) for n in names], axis=0)

    shard_rows = {n: vals[n].shape[2 if n in _COL_SHARDED else 1] for n in _BIG}
    big = _unpack_rows(_all_gather(packed(_EARLY), "ag_weights"), _EARLY, shard_rows)

    small = {n: vals[n] for n in _SMALL}
    loss_part, dx, gathered_small, summed = _local_step(
        x[0], mem[0], loss_target[0], _small_views(small), big,
        {host: packed(names) for host, names in GATHER_IN.items()}, shard_rows)
    loss = lax.psum(loss_part[0, 0], ("x", "y", "c"))

    grads, deltas, new_m, new_v = {}, {}, {}, {}
    for names, g_rows in [([n], summed[n]) for n in _EARLY] + [(SCATTER_IN[h], summed[h]) for h in SCATTER_IN]:
        off = 0
        for n in names:
            rows = shard_rows[n]
            g = g_rows[off:off + rows]
            off += rows
            state = [vals[n][0], vals["m_" + n][0], vals["v_" + n][0]]
            flipped = n in _COL_SHARDED and rows % SMALL_LANES != 0
            if flipped:
                state = [t.T for t in state]
            elif n in _COL_SHARDED:
                g = g.T
            outs = (g, *_adamw(state[0], g, state[1], state[2], "adamw_" + n))
            if flipped:
                outs = tuple(t.T for t in outs)
            grads[n], deltas[n], new_m[n], new_v[n] = (t[None] for t in outs)

    outs = _small_sum_adamw(gathered_small,
                            *([vals[pre + n].reshape(-1, SMALL_LANES) for n in _SMALL] for pre in ("", "m_", "v_")))
    for dst, packed in zip((grads, deltas, new_m, new_v), outs):
        dst.update(_unpack_small(packed, small))

    return (loss, dx[None], *[grads[n] for n in _WEIGHTS], *[deltas[n] for n in _WEIGHTS],
            *[new_m[n] for n in _WEIGHTS], *[new_v[n] for n in _WEIGHTS])
```
